```python
import jax, jax.numpy as jnp
from jax import lax
import numpy as np


D_MODEL = 2048
BATCH = 8
SEQ = 4096
DEPTH = 2

D_A = D_MODEL // 2
A_HEAD = 128
A_HEADS = D_A // A_HEAD
A_CHUNK = 128
D_B = D_MODEL // 2
B_GROUPS = 4
B_GROUP = D_B // B_GROUPS
B_WINDOWS = (2, 4, 8, 16)
D_C = D_MODEL
C_HEAD = 128
C_HEADS = D_C // C_HEAD
C_CHUNK = 64
D_FF = 5632
CONV_W = 3
N_EVEN = (DEPTH + 1) // 2
N_ODD = DEPTH // 2
ALPHA = (2 * DEPTH) ** 0.25
BETA = (8 * DEPTH) ** -0.25
LN_EPS = 1e-5

kernel_name = 'hybrid_gmlp_pool_hgrn2_convffn_deepnorm'


def layer_norm(x, g, b):
    xf = x.astype(jnp.float32)
    mu = jnp.mean(xf, axis=-1, keepdims=True)
    var = jnp.mean(jnp.square(xf - mu), axis=-1, keepdims=True)
    return ((xf - mu) * lax.rsqrt(var + LN_EPS) * g + b).astype(x.dtype)


def rms_norm(x, g):
    xf = x.astype(jnp.float32)
    return xf * lax.rsqrt(jnp.mean(jnp.square(xf), axis=-1, keepdims=True) + LN_EPS) * g


def shift_right(x, s):
    pad = [(0, 0)] * x.ndim
    pad[1] = (s, 0)
    return jnp.pad(x, pad)[:, :x.shape[1]]


def spatial_gating(za, ln_g, ln_b, w_s, b_s):
    bn, t, _ = za.shape
    u, v = jnp.split(za, 2, axis=-1)
    v = layer_norm(v, ln_g, ln_b)
    v = v.reshape(bn, t // A_CHUNK, A_CHUNK, A_HEADS, A_HEAD)
    w = jnp.tril(w_s)
    s = jnp.einsum('hts,bnshc->bnthc', w, v) + b_s.T[None, None, :, :, None]
    return u * s.reshape(bn, t, D_A)


def multiscale_pool(xb, w_pool, scale):
    bn, t, _ = xb.shape
    xg = xb.reshape(bn, t, B_GROUPS, B_GROUP).astype(jnp.float32)
    csum = jnp.cumsum(xg, axis=1)
    pos = jnp.arange(1, t + 1, dtype=jnp.float32)
    outs = []
    for gi, win in enumerate(B_WINDOWS):
        c = csum[:, :, gi]
        wsum = c - shift_right(c, win)
        cnt = jnp.minimum(pos, float(win))[None, :, None]
        outs.append(wsum / cnt - xg[:, :, gi])
    p = jnp.stack(outs, axis=2).astype(xb.dtype)
    y = jnp.einsum('btgc,gcd->btgd', p, w_pool)
    return y.reshape(bn, t, D_B) * scale


def hgrn2(q, f_logit, inp, lb):
    bn, t, _ = q.shape
    n = t // C_CHUNK
    f32 = jnp.float32

    def heads(a):
        return a.astype(f32).reshape(bn, n, C_CHUNK, C_HEADS, C_HEAD).transpose(0, 3, 1, 2, 4)

    f = lb + (1.0 - lb) * jax.nn.sigmoid(f_logit.astype(f32))
    qh = heads(jax.nn.silu(q.astype(f32)))
    kh = heads(1.0 - f)
    vh = heads(inp)
    bcum = jnp.cumsum(heads(jnp.log(f)), axis=3)
    blast = bcum[:, :, :, -1:, :]
    q_dec = qh * jnp.exp(bcum)
    k_dec = kh * jnp.exp(-bcum)
    k_end = kh * jnp.exp(blast - bcum)
    mask = jnp.tril(jnp.ones((C_CHUNK, C_CHUNK), dtype=bool))
    att = jnp.where(mask, jnp.einsum('bhntd,bhnsd->bhnts', q_dec, k_dec), 0.0)
    o_intra = jnp.einsum('bhnts,bhnse->bhnte', att, vh)
    upd = jnp.einsum('bhnsd,bhnse->bhnde', k_end, vh)
    dec = jnp.exp(blast[:, :, :, 0, :])

    def step(state, xs):
        d_n, u_n = xs
        return d_n[..., None] * state + u_n, state

    s0 = jnp.zeros((bn, C_HEADS, C_HEAD, C_HEAD), f32)
    _, s_prev = lax.scan(step, s0, (jnp.moveaxis(dec, 2, 0), jnp.moveaxis(upd, 2, 0)))
    s_prev = jnp.moveaxis(s_prev, 0, 2)
    o = o_intra + jnp.einsum('bhntd,bhnde->bhnte', q_dec, s_prev)
    return o.transpose(0, 2, 3, 1, 4).reshape(bn, t, C_HEADS, C_HEAD)


def conv_ffn(x, w_up, conv_w, conv_b, w_down):
    h = x @ w_up
    hc = conv_b + conv_w[CONV_W - 1] * h
    for j in range(CONV_W - 1):
        hc = hc + conv_w[j] * shift_right(h, CONV_W - 1 - j)
    a, v = jnp.split(hc, 2, axis=-1)
    return (jax.nn.silu(a) * v) @ w_down


def _fwd_setup_inputs(seed: int = 0) -> dict:
    key = jax.random.key(seed)
    ks = jax.random.split(key, 24)
    f32 = jnp.float32

    def nrm(k, shape, scale):
        return jax.random.normal(k, shape, f32) * scale

    return {
        'x': nrm(ks[0], (BATCH, SEQ, D_MODEL), 1.0),
        'ev_w_in': nrm(ks[1], (N_EVEN, D_MODEL, 2 * D_A + D_B), D_MODEL ** -0.5),
        'ev_ln_v_g': 1.0 + nrm(ks[2], (N_EVEN, D_A), 0.02),
        'ev_ln_v_b': nrm(ks[3], (N_EVEN, D_A), 0.02),
        'ev_w_s': nrm(ks[4], (N_EVEN, A_HEADS, A_CHUNK, A_CHUNK), 0.5 * A_CHUNK ** -0.5),
        'ev_b_s': 1.0 + nrm(ks[5], (N_EVEN, A_HEADS, A_CHUNK), 0.02),
        'ev_w_pool': nrm(ks[6], (N_EVEN, B_GROUPS, B_GROUP, B_GROUP), B_GROUP ** -0.5),
        'ev_pool_scale': 1.0 + nrm(ks[7], (N_EVEN, D_B), 0.02),
        'ev_w_out': nrm(ks[8], (N_EVEN, D_A + D_B, D_MODEL), BETA * (D_A + D_B) ** -0.5),
        'od_w_in': nrm(ks[9], (N_ODD, D_MODEL, 4 * D_C), D_MODEL ** -0.5),
        'od_norm_g': 1.0 + nrm(ks[10], (N_ODD, D_C), 0.02),
        'od_w_out': nrm(ks[11], (N_ODD, D_C, D_MODEL), BETA * D_C ** -0.5),
        'lb_param': nrm(ks[12], (DEPTH, D_C), 0.1),
        'ffn_w_up': nrm(ks[13], (DEPTH, D_MODEL, 2 * D_FF), D_MODEL ** -0.5),
        'ffn_conv_w': nrm(ks[14], (DEPTH, CONV_W, 2 * D_FF), CONV_W ** -0.5),
        'ffn_conv_b': nrm(ks[15], (DEPTH, 2 * D_FF), 0.02),
        'ffn_w_down': nrm(ks[16], (DEPTH, D_FF, D_MODEL), BETA * D_FF ** -0.5),
        'ln1_g': 1.0 + nrm(ks[17], (DEPTH, D_MODEL), 0.02),
        'ln1_b': nrm(ks[18], (DEPTH, D_MODEL), 0.02),
        'ln2_g': 1.0 + nrm(ks[19], (DEPTH, D_MODEL), 0.02),
        'ln2_b': nrm(ks[20], (DEPTH, D_MODEL), 0.02),
    }


def _fwd_reference(x, ev_w_in, ev_ln_v_g, ev_ln_v_b, ev_w_s, ev_b_s, ev_w_pool, ev_pool_scale,
              ev_w_out, od_w_in, od_norm_g, od_w_out, lb_param, ffn_w_up, ffn_conv_w,
              ffn_conv_b, ffn_w_down, ln1_g, ln1_b, ln2_g, ln2_b):
    bn, t, _ = x.shape
    lb_all = jnp.cumsum(jax.nn.softmax(lb_param.astype(jnp.float32), axis=0), axis=0)
    lb_all = lb_all - lb_all[0]
    for l in range(DEPTH):
        if l % 2 == 0:
            e = l // 2
            h = x @ ev_w_in[e]
            za = jax.nn.gelu(h[..., :2 * D_A])
            xb = h[..., 2 * D_A:]
            ya = spatial_gating(za, ev_ln_v_g[e], ev_ln_v_b[e], ev_w_s[e], ev_b_s[e])
            yb = multiscale_pool(xb, ev_w_pool[e], ev_pool_scale[e])
            mix = jnp.concatenate([ya, yb], axis=-1) @ ev_w_out[e]
        else:
            o = l // 2
            h = x @ od_w_in[o]
            q, f_logit, inp, g = jnp.split(h, 4, axis=-1)
            y = hgrn2(q, f_logit, inp, lb_all[l])
            y = rms_norm(y, od_norm_g[o].reshape(C_HEADS, C_HEAD)).reshape(bn, t, D_C)
            y = (y * jax.nn.sigmoid(g.astype(jnp.float32))).astype(x.dtype)
            mix = y @ od_w_out[o]
        x = layer_norm(ALPHA * x + mix, ln1_g[l], ln1_b[l])
        x = layer_norm(ALPHA * x + conv_ffn(x, ffn_w_up[l], ffn_conv_w[l], ffn_conv_b[l], ffn_w_down[l]),
                       ln2_g[l], ln2_b[l])
    return x


import jax as _jax
import jax.numpy as _jnp

TWIN_FORMAT = 'train_step'
FWD_PARAMS = ['x', 'ev_w_in', 'ev_ln_v_g', 'ev_ln_v_b', 'ev_w_s', 'ev_b_s', 'ev_w_pool', 'ev_pool_scale', 'ev_w_out', 'od_w_in', 'od_norm_g', 'od_w_out', 'lb_param', 'ffn_w_up', 'ffn_conv_w', 'ffn_conv_b', 'ffn_w_down', 'ln1_g', 'ln1_b', 'ln2_g', 'ln2_b']
TWIN_WEIGHTS = ['ev_w_in', 'ev_ln_v_g', 'ev_ln_v_b', 'ev_w_s', 'ev_b_s', 'ev_w_pool', 'ev_pool_scale', 'ev_w_out', 'od_w_in', 'od_norm_g', 'od_w_out', 'lb_param', 'ffn_w_up', 'ffn_conv_w', 'ffn_conv_b', 'ffn_w_down', 'ln1_g', 'ln1_b', 'ln2_g', 'ln2_b']
TWIN_DIFF_INPUT = 'x'
TWIN_INPUTS = ['x', 'ev_w_in', 'ev_ln_v_g', 'ev_ln_v_b', 'ev_w_s', 'ev_b_s', 'ev_w_pool', 'ev_pool_scale', 'ev_w_out', 'od_w_in', 'od_norm_g', 'od_w_out', 'lb_param', 'ffn_w_up', 'ffn_conv_w', 'ffn_conv_b', 'ffn_w_down', 'ln1_g', 'ln1_b', 'ln2_g', 'ln2_b', 'loss_target', 'm_ev_w_in', 'm_ev_ln_v_g', 'm_ev_ln_v_b', 'm_ev_w_s', 'm_ev_b_s', 'm_ev_w_pool', 'm_ev_pool_scale', 'm_ev_w_out', 'm_od_w_in', 'm_od_norm_g', 'm_od_w_out', 'm_lb_param', 'm_ffn_w_up', 'm_ffn_conv_w', 'm_ffn_conv_b', 'm_ffn_w_down', 'm_ln1_g', 'm_ln1_b', 'm_ln2_g', 'm_ln2_b', 'v_ev_w_in', 'v_ev_ln_v_g', 'v_ev_ln_v_b', 'v_ev_w_s', 'v_ev_b_s', 'v_ev_w_pool', 'v_ev_pool_scale', 'v_ev_w_out', 'v_od_w_in', 'v_od_norm_g', 'v_od_w_out', 'v_lb_param', 'v_ffn_w_up', 'v_ffn_conv_w', 'v_ffn_conv_b', 'v_ffn_w_down', 'v_ln1_g', 'v_ln1_b', 'v_ln2_g', 'v_ln2_b']
TWIN_OUTPUTS = ['loss', 'grad_x', 'grad_ev_w_in', 'grad_ev_ln_v_g', 'grad_ev_ln_v_b', 'grad_ev_w_s', 'grad_ev_b_s', 'grad_ev_w_pool', 'grad_ev_pool_scale', 'grad_ev_w_out', 'grad_od_w_in', 'grad_od_norm_g', 'grad_od_w_out', 'grad_lb_param', 'grad_ffn_w_up', 'grad_ffn_conv_w', 'grad_ffn_conv_b', 'grad_ffn_w_down', 'grad_ln1_g', 'grad_ln1_b', 'grad_ln2_g', 'grad_ln2_b', 'delta_ev_w_in', 'delta_ev_ln_v_g', 'delta_ev_ln_v_b', 'delta_ev_w_s', 'delta_ev_b_s', 'delta_ev_w_pool', 'delta_ev_pool_scale', 'delta_ev_w_out', 'delta_od_w_in', 'delta_od_norm_g', 'delta_od_w_out', 'delta_lb_param', 'delta_ffn_w_up', 'delta_ffn_conv_w', 'delta_ffn_conv_b', 'delta_ffn_w_down', 'delta_ln1_g', 'delta_ln1_b', 'delta_ln2_g', 'delta_ln2_b', 'new_m_ev_w_in', 'new_m_ev_ln_v_g', 'new_m_ev_ln_v_b', 'new_m_ev_w_s', 'new_m_ev_b_s', 'new_m_ev_w_pool', 'new_m_ev_pool_scale', 'new_m_ev_w_out', 'new_m_od_w_in', 'new_m_od_norm_g', 'new_m_od_w_out', 'new_m_lb_param', 'new_m_ffn_w_up', 'new_m_ffn_conv_w', 'new_m_ffn_conv_b', 'new_m_ffn_w_down', 'new_m_ln1_g', 'new_m_ln1_b', 'new_m_ln2_g', 'new_m_ln2_b', 'new_v_ev_w_in', 'new_v_ev_ln_v_g', 'new_v_ev_ln_v_b', 'new_v_ev_w_s', 'new_v_ev_b_s', 'new_v_ev_w_pool', 'new_v_ev_pool_scale', 'new_v_ev_w_out', 'new_v_od_w_in', 'new_v_od_norm_g', 'new_v_od_w_out', 'new_v_lb_param', 'new_v_ffn_w_up', 'new_v_ffn_conv_w', 'new_v_ffn_conv_b', 'new_v_ffn_w_down', 'new_v_ln1_g', 'new_v_ln1_b', 'new_v_ln2_g', 'new_v_ln2_b']
TWIN_LEAF_KINDS = {'loss': 'loss', 'grad_x': 'grad_x', 'grad_ev_w_in': 'grad_w', 'grad_ev_ln_v_g': 'grad_w', 'grad_ev_ln_v_b': 'grad_w', 'grad_ev_w_s': 'grad_w', 'grad_ev_b_s': 'grad_w', 'grad_ev_w_pool': 'grad_w', 'grad_ev_pool_scale': 'grad_w', 'grad_ev_w_out': 'grad_w', 'grad_od_w_in': 'grad_w', 'grad_od_norm_g': 'grad_w', 'grad_od_w_out': 'grad_w', 'grad_lb_param': 'grad_w', 'grad_ffn_w_up': 'grad_w', 'grad_ffn_conv_w': 'grad_w', 'grad_ffn_conv_b': 'grad_w', 'grad_ffn_w_down': 'grad_w', 'grad_ln1_g': 'grad_w', 'grad_ln1_b': 'grad_w', 'grad_ln2_g': 'grad_w', 'grad_ln2_b': 'grad_w', 'delta_ev_w_in': 'delta_w', 'delta_ev_ln_v_g': 'delta_w', 'delta_ev_ln_v_b': 'delta_w', 'delta_ev_w_s': 'delta_w', 'delta_ev_b_s': 'delta_w', 'delta_ev_w_pool': 'delta_w', 'delta_ev_pool_scale': 'delta_w', 'delta_ev_w_out': 'delta_w', 'delta_od_w_in': 'delta_w', 'delta_od_norm_g': 'delta_w', 'delta_od_w_out': 'delta_w', 'delta_lb_param': 'delta_w', 'delta_ffn_w_up': 'delta_w', 'delta_ffn_conv_w': 'delta_w', 'delta_ffn_conv_b': 'delta_w', 'delta_ffn_w_down': 'delta_w', 'delta_ln1_g': 'delta_w', 'delta_ln1_b': 'delta_w', 'delta_ln2_g': 'delta_w', 'delta_ln2_b': 'delta_w', 'new_m_ev_w_in': 'new_m', 'new_m_ev_ln_v_g': 'new_m', 'new_m_ev_ln_v_b': 'new_m', 'new_m_ev_w_s': 'new_m', 'new_m_ev_b_s': 'new_m', 'new_m_ev_w_pool': 'new_m', 'new_m_ev_pool_scale': 'new_m', 'new_m_ev_w_out': 'new_m', 'new_m_od_w_in': 'new_m', 'new_m_od_norm_g': 'new_m', 'new_m_od_w_out': 'new_m', 'new_m_lb_param': 'new_m', 'new_m_ffn_w_up': 'new_m', 'new_m_ffn_conv_w': 'new_m', 'new_m_ffn_conv_b': 'new_m', 'new_m_ffn_w_down': 'new_m', 'new_m_ln1_g': 'new_m', 'new_m_ln1_b': 'new_m', 'new_m_ln2_g': 'new_m', 'new_m_ln2_b': 'new_m', 'new_v_ev_w_in': 'new_v', 'new_v_ev_ln_v_g': 'new_v', 'new_v_ev_ln_v_b': 'new_v', 'new_v_ev_w_s': 'new_v', 'new_v_ev_b_s': 'new_v', 'new_v_ev_w_pool': 'new_v', 'new_v_ev_pool_scale': 'new_v', 'new_v_ev_w_out': 'new_v', 'new_v_od_w_in': 'new_v', 'new_v_od_norm_g': 'new_v', 'new_v_od_w_out': 'new_v', 'new_v_lb_param': 'new_v', 'new_v_ffn_w_up': 'new_v', 'new_v_ffn_conv_w': 'new_v', 'new_v_ffn_conv_b': 'new_v', 'new_v_ffn_w_down': 'new_v', 'new_v_ln1_g': 'new_v', 'new_v_ln1_b': 'new_v', 'new_v_ln2_g': 'new_v', 'new_v_ln2_b': 'new_v'}


def _forward(args):
    return _fwd_reference(*[args[k] for k in FWD_PARAMS])


def _output_shape():
    def fwd():
        inp = _fwd_setup_inputs(0)
        return _fwd_reference(*[inp[k] for k in FWD_PARAMS])
    out = _jax.eval_shape(fwd)
    return out.shape, out.dtype

N_MICROBATCH = 1
ADAM_LR = 0.001
ADAM_B1 = 0.9
ADAM_B2 = 0.999
ADAM_EPS = 1e-08
ADAM_WD = 0.01
ADAM_STEP = 10
PER_EXAMPLE_BATCH_AXIS = {'x': 0, 'loss_target': 0}
SHARED_INPUTS = []
_WEIGHT_DTYPES = {'ev_w_in': _jnp.float32, 'ev_ln_v_g': _jnp.float32, 'ev_ln_v_b': _jnp.float32, 'ev_w_s': _jnp.float32, 'ev_b_s': _jnp.float32, 'ev_w_pool': _jnp.float32, 'ev_pool_scale': _jnp.float32, 'ev_w_out': _jnp.float32, 'od_w_in': _jnp.float32, 'od_norm_g': _jnp.float32, 'od_w_out': _jnp.float32, 'lb_param': _jnp.float32, 'ffn_w_up': _jnp.float32, 'ffn_conv_w': _jnp.float32, 'ffn_conv_b': _jnp.float32, 'ffn_w_down': _jnp.float32, 'ln1_g': _jnp.float32, 'ln1_b': _jnp.float32, 'ln2_g': _jnp.float32, 'ln2_b': _jnp.float32}
MOMENT_SCALE = {'ev_w_in': 2.121844e-02, 'ev_ln_v_g': 7.282069e-03, 'ev_ln_v_b': 7.207866e-03, 'ev_w_s': 1.458519e-02, 'ev_b_s': 2.113650e-02, 'ev_w_pool': 2.767317e-02, 'ev_pool_scale': 2.767933e-02, 'ev_w_out': 5.928119e-02, 'od_w_in': 9.971290e-03, 'od_norm_g': 1.859973e-02, 'od_w_out': 3.701814e-02, 'lb_param': 1.666958e-03, 'ffn_w_up': 1.157346e-02, 'ffn_conv_w': 1.185658e-02, 'ffn_conv_b': 1.411092e-02, 'ffn_w_down': 3.788737e-02, 'ln1_g': 5.500502e-01, 'ln1_b': 2.870598e-01, 'ln2_g': 1.133699e+01, 'ln2_b': 9.122005e-01}


def _to_microbatches(a, axis):
    t = _jnp.moveaxis(a, axis, 0)
    t = t.reshape((N_MICROBATCH, t.shape[0] // N_MICROBATCH) + t.shape[1:])
    return _jnp.moveaxis(t, 1, axis + 1)


def setup_inputs(seed: int = 0) -> dict:
    inp = _fwd_setup_inputs(seed)
    key = _jax.random.fold_in(_jax.random.key(seed), 7919)
    shape, _ = _output_shape()
    out = dict(inp)
    out["loss_target"] = _jax.random.normal(_jax.random.fold_in(key, 0), shape, _jnp.float32)
    for i, name in enumerate(TWIN_WEIGHTS):
        w = inp[name].astype(_jnp.float32)
        if MOMENT_SCALE is None:
            s = _jnp.sqrt(_jnp.mean(_jnp.square(w)) + 1e-30)
        else:
            s = MOMENT_SCALE[name]
        km, kv = _jax.random.split(_jax.random.fold_in(key, i + 1))
        out[name] = w
        out["m_" + name] = s * _jax.random.normal(km, w.shape, _jnp.float32)
        out["v_" + name] = (s * s) * _jax.random.uniform(kv, w.shape, _jnp.float32, 0.5, 1.5)
    if N_MICROBATCH > 1:
        for name, axis in PER_EXAMPLE_BATCH_AXIS.items():
            out[name] = _to_microbatches(out[name], axis)
    return {'x': out['x'], 'ev_w_in': out['ev_w_in'], 'ev_ln_v_g': out['ev_ln_v_g'], 'ev_ln_v_b': out['ev_ln_v_b'], 'ev_w_s': out['ev_w_s'], 'ev_b_s': out['ev_b_s'], 'ev_w_pool': out['ev_w_pool'], 'ev_pool_scale': out['ev_pool_scale'], 'ev_w_out': out['ev_w_out'], 'od_w_in': out['od_w_in'], 'od_norm_g': out['od_norm_g'], 'od_w_out': out['od_w_out'], 'lb_param': out['lb_param'], 'ffn_w_up': out['ffn_w_up'], 'ffn_conv_w': out['ffn_conv_w'], 'ffn_conv_b': out['ffn_conv_b'], 'ffn_w_down': out['ffn_w_down'], 'ln1_g': out['ln1_g'], 'ln1_b': out['ln1_b'], 'ln2_g': out['ln2_g'], 'ln2_b': out['ln2_b'], 'loss_target': out['loss_target'], 'm_ev_w_in': out['m_ev_w_in'], 'm_ev_ln_v_g': out['m_ev_ln_v_g'], 'm_ev_ln_v_b': out['m_ev_ln_v_b'], 'm_ev_w_s': out['m_ev_w_s'], 'm_ev_b_s': out['m_ev_b_s'], 'm_ev_w_pool': out['m_ev_w_pool'], 'm_ev_pool_scale': out['m_ev_pool_scale'], 'm_ev_w_out': out['m_ev_w_out'], 'm_od_w_in': out['m_od_w_in'], 'm_od_norm_g': out['m_od_norm_g'], 'm_od_w_out': out['m_od_w_out'], 'm_lb_param': out['m_lb_param'], 'm_ffn_w_up': out['m_ffn_w_up'], 'm_ffn_conv_w': out['m_ffn_conv_w'], 'm_ffn_conv_b': out['m_ffn_conv_b'], 'm_ffn_w_down': out['m_ffn_w_down'], 'm_ln1_g': out['m_ln1_g'], 'm_ln1_b': out['m_ln1_b'], 'm_ln2_g': out['m_ln2_g'], 'm_ln2_b': out['m_ln2_b'], 'v_ev_w_in': out['v_ev_w_in'], 'v_ev_ln_v_g': out['v_ev_ln_v_g'], 'v_ev_ln_v_b': out['v_ev_ln_v_b'], 'v_ev_w_s': out['v_ev_w_s'], 'v_ev_b_s': out['v_ev_b_s'], 'v_ev_w_pool': out['v_ev_w_pool'], 'v_ev_pool_scale': out['v_ev_pool_scale'], 'v_ev_w_out': out['v_ev_w_out'], 'v_od_w_in': out['v_od_w_in'], 'v_od_norm_g': out['v_od_norm_g'], 'v_od_w_out': out['v_od_w_out'], 'v_lb_param': out['v_lb_param'], 'v_ffn_w_up': out['v_ffn_w_up'], 'v_ffn_conv_w': out['v_ffn_conv_w'], 'v_ffn_conv_b': out['v_ffn_conv_b'], 'v_ffn_w_down': out['v_ffn_w_down'], 'v_ln1_g': out['v_ln1_g'], 'v_ln1_b': out['v_ln1_b'], 'v_ln2_g': out['v_ln2_g'], 'v_ln2_b': out['v_ln2_b']}


def _loss(weights, diff, rest, loss_target):
    with _jax.named_scope("forward"):
        args = {**rest, TWIN_DIFF_INPUT: diff, **{k: w.astype(_WEIGHT_DTYPES[k]) for k, w in weights.items()}}
        y = _forward(args)
    with _jax.named_scope("loss_head"):
        err = _jnp.square(y.astype(_jnp.float32) - loss_target)
        return 0.5 * _jnp.sum(_jnp.mean(err, axis=-1)) if err.ndim else 0.5 * err


def _adamw(w, g, m, v):
    m = ADAM_B1 * m + (1.0 - ADAM_B1) * g
    v = ADAM_B2 * v + (1.0 - ADAM_B2) * _jnp.square(g)
    m_hat = m / (1.0 - ADAM_B1 ** ADAM_STEP)
    v_hat = v / (1.0 - ADAM_B2 ** ADAM_STEP)
    delta = -ADAM_LR * (m_hat / (_jnp.sqrt(v_hat) + ADAM_EPS) + ADAM_WD * w)
    return delta, m, v


def reference(x, ev_w_in, ev_ln_v_g, ev_ln_v_b, ev_w_s, ev_b_s, ev_w_pool, ev_pool_scale, ev_w_out, od_w_in, od_norm_g, od_w_out, lb_param, ffn_w_up, ffn_conv_w, ffn_conv_b, ffn_w_down, ln1_g, ln1_b, ln2_g, ln2_b, loss_target, m_ev_w_in, m_ev_ln_v_g, m_ev_ln_v_b, m_ev_w_s, m_ev_b_s, m_ev_w_pool, m_ev_pool_scale, m_ev_w_out, m_od_w_in, m_od_norm_g, m_od_w_out, m_lb_param, m_ffn_w_up, m_ffn_conv_w, m_ffn_conv_b, m_ffn_w_down, m_ln1_g, m_ln1_b, m_ln2_g, m_ln2_b, v_ev_w_in, v_ev_ln_v_g, v_ev_ln_v_b, v_ev_w_s, v_ev_b_s, v_ev_w_pool, v_ev_pool_scale, v_ev_w_out, v_od_w_in, v_od_norm_g, v_od_w_out, v_lb_param, v_ffn_w_up, v_ffn_conv_w, v_ffn_conv_b, v_ffn_w_down, v_ln1_g, v_ln1_b, v_ln2_g, v_ln2_b):
    given = dict(x=x, ev_w_in=ev_w_in, ev_ln_v_g=ev_ln_v_g, ev_ln_v_b=ev_ln_v_b, ev_w_s=ev_w_s, ev_b_s=ev_b_s, ev_w_pool=ev_w_pool, ev_pool_scale=ev_pool_scale, ev_w_out=ev_w_out, od_w_in=od_w_in, od_norm_g=od_norm_g, od_w_out=od_w_out, lb_param=lb_param, ffn_w_up=ffn_w_up, ffn_conv_w=ffn_conv_w, ffn_conv_b=ffn_conv_b, ffn_w_down=ffn_w_down, ln1_g=ln1_g, ln1_b=ln1_b, ln2_g=ln2_g, ln2_b=ln2_b, loss_target=loss_target, m_ev_w_in=m_ev_w_in, m_ev_ln_v_g=m_ev_ln_v_g, m_ev_ln_v_b=m_ev_ln_v_b, m_ev_w_s=m_ev_w_s, m_ev_b_s=m_ev_b_s, m_ev_w_pool=m_ev_w_pool, m_ev_pool_scale=m_ev_pool_scale, m_ev_w_out=m_ev_w_out, m_od_w_in=m_od_w_in, m_od_norm_g=m_od_norm_g, m_od_w_out=m_od_w_out, m_lb_param=m_lb_param, m_ffn_w_up=m_ffn_w_up, m_ffn_conv_w=m_ffn_conv_w, m_ffn_conv_b=m_ffn_conv_b, m_ffn_w_down=m_ffn_w_down, m_ln1_g=m_ln1_g, m_ln1_b=m_ln1_b, m_ln2_g=m_ln2_g, m_ln2_b=m_ln2_b, v_ev_w_in=v_ev_w_in, v_ev_ln_v_g=v_ev_ln_v_g, v_ev_ln_v_b=v_ev_ln_v_b, v_ev_w_s=v_ev_w_s, v_ev_b_s=v_ev_b_s, v_ev_w_pool=v_ev_w_pool, v_ev_pool_scale=v_ev_pool_scale, v_ev_w_out=v_ev_w_out, v_od_w_in=v_od_w_in, v_od_norm_g=v_od_norm_g, v_od_w_out=v_od_w_out, v_lb_param=v_lb_param, v_ffn_w_up=v_ffn_w_up, v_ffn_conv_w=v_ffn_conv_w, v_ffn_conv_b=v_ffn_conv_b, v_ffn_w_down=v_ffn_w_down, v_ln1_g=v_ln1_g, v_ln1_b=v_ln1_b, v_ln2_g=v_ln2_g, v_ln2_b=v_ln2_b)
    weights = {n: given[n] for n in TWIN_WEIGHTS}
    shared = {n: given[n] for n in SHARED_INPUTS}
    per_example = {n: given[n] for n in ['x']}
    grad_fn = _jax.value_and_grad(_loss, argnums=(0, 1))

    def one_microbatch(ex, loss_target):
        ex = dict(ex)
        diff = ex.pop(TWIN_DIFF_INPUT)
        return grad_fn(weights, diff, {**shared, **ex}, loss_target)

    if N_MICROBATCH == 1:
        loss, (grad_w, grad_x) = one_microbatch(per_example, given["loss_target"])
    else:
        def body(carry, xs):
            loss_sum, grad_sum = carry
            l_k, (gw_k, gx_k) = one_microbatch(xs[0], xs[1])
            with _jax.named_scope("update"):
                return (loss_sum + l_k, _jax.tree.map(_jnp.add, grad_sum, gw_k)), gx_k

        init = (_jnp.zeros((), _jnp.float32), _jax.tree.map(_jnp.zeros_like, weights))
        (loss, grad_w), grad_x = _jax.lax.scan(body, init, (per_example, given["loss_target"]))
    with _jax.named_scope("update"):
        delta_w, new_m, new_v = {}, {}, {}
        for n in TWIN_WEIGHTS:
            delta_w[n], new_m[n], new_v[n] = _adamw(weights[n], grad_w[n], given["m_" + n], given["v_" + n])
    return (loss, grad_x, *[grad_w[n] for n in TWIN_WEIGHTS], *[delta_w[n] for n in TWIN_WEIGHTS],
            *[new_m[n] for n in TWIN_WEIGHTS], *[new_v[n] for n in TWIN_WEIGHTS])
```

```python
import functools

import jax
import jax.numpy as jnp
from jax import lax
from jax.experimental import pallas as pl
from jax.experimental.pallas import tpu as pltpu

_MM = jnp.bfloat16
_XCH = jnp.bfloat16

DEPTH = 2
ALPHA = (2 * DEPTH) ** 0.25
LN_EPS = 1e-5
A_CHUNK = 128
A_HEAD = 128
B_GROUPS = 4
POOL_HALO = 16
C_CHUNK = 64
C_HEAD = 128
CONV_HALO = 8
ADAM_LR, ADAM_B1, ADAM_B2, ADAM_EPS, ADAM_WD, ADAM_STEP = 0.001, 0.9, 0.999, 1e-08, 0.01, 10
N_DEV = 8
LANE = 128
VMEM_LIMIT = 56 * 1024 * 1024

_F32 = jnp.float32
_NN = (((1,), (0,)), ((), ()))
_NT = (((1,), (1,)), ((), ()))
_TN = (((0,), (0,)), ((), ()))
_S = jax.ShapeDtypeStruct


def _dot(a, b, dims=_NN):
    return lax.dot_general(a, b, dims, preferred_element_type=_F32)


def _tile(dim, pref):
    best = None
    d = LANE
    while d <= min(dim, pref):
        if dim % d == 0:
            best = d
        d += LANE
    return best if best is not None else dim


def _params(sem):
    return pltpu.CompilerParams(dimension_semantics=sem, vmem_limit_bytes=VMEM_LIMIT)


def _sigmoid(x):
    return 1.0 / (1.0 + jnp.exp(-x))


_GELU_C = 0.7978845608028654
_GELU_A = 0.044715


def _gelu_and_grad(x):
    t = jnp.tanh(_GELU_C * (x + _GELU_A * x * x * x))
    y = 0.5 * x * (1.0 + t)
    dy = 0.5 * (1.0 + t) + 0.5 * x * (1.0 - t * t) * _GELU_C * (1.0 + 3.0 * _GELU_A * x * x)
    return y, dy


def _row_index(n):
    return lax.broadcasted_iota(jnp.int32, (n, 1), 0)


def _mm(a, b, mode, out_dtype, name, *, a_parts=1, b_parts=1, out_parts=1, add=None, add_scale=1.0):
    if mode == "nn":
        M, K = a.shape
        N = b.shape[1]
    elif mode == "nt":
        if a_parts > 1:
            M, K = a.shape[1], a.shape[2] * a_parts
        else:
            M, K = a.shape
        N = b.shape[0]
    else:
        K, M = a.shape
        N = b.shape[-1] * b_parts
    tm = _tile(M, 1024)
    tn = _tile(N // max(b_parts, out_parts), 1536)
    tk = _tile(K // a_parts, 512)
    nk = K // tk
    npj = (N // max(b_parts, out_parts)) // tn
    nkp = (K // a_parts) // tk
    if mode == "nn":
        a_spec = pl.BlockSpec((tm, tk), lambda i, j, k: (i, k))
        b_spec = pl.BlockSpec((tk, tn), lambda i, j, k: (k, j))
        dims = _NN
    elif mode == "nt":
        if a_parts > 1:
            a_spec = pl.BlockSpec((None, tm, tk), lambda i, j, k: (k // nkp, i, k % nkp))
        else:
            a_spec = pl.BlockSpec((tm, tk), lambda i, j, k: (i, k))
        b_spec = pl.BlockSpec((tn, tk), lambda i, j, k: (j, k))
        dims = _NT
    else:
        a_spec = pl.BlockSpec((tk, tm), lambda i, j, k: (k, i))
        if b_parts > 1:
            b_spec = pl.BlockSpec((None, tk, tn), lambda i, j, k: (j // npj, k, j % npj))
        else:
            b_spec = pl.BlockSpec((tk, tn), lambda i, j, k: (k, j))
        dims = _TN
    if out_parts > 1:
        out_spec = pl.BlockSpec((None, tm, tn), lambda i, j, k: (j // npj, i, j % npj))
        out_shape = _S((out_parts, M, N // out_parts), out_dtype)
    else:
        out_spec = pl.BlockSpec((tm, tn), lambda i, j, k: (i, j))
        out_shape = _S((M, N), out_dtype)
    in_specs = [a_spec, b_spec]
    args = [a, b]
    if add is not None:
        in_specs.append(pl.BlockSpec((tm, tn), lambda i, j, k: (i, j)))
        args.append(add)

    def body(*refs):
        a_ref, b_ref = refs[0], refs[1]
        o_ref, acc = refs[-2], refs[-1]
        k = pl.program_id(2)

        @pl.when(k == 0)
        def _():
            acc[...] = jnp.zeros_like(acc)

        acc[...] += _dot(a_ref[...], b_ref[...], dims)

        @pl.when(k == nk - 1)
        def _():
            r = acc[...]
            if add is not None:
                r = r + add_scale * refs[2][...]
            o_ref[...] = r.astype(o_ref.dtype)

    return pl.pallas_call(
        body, name=name, grid=(M // tm, N // tn, nk), in_specs=in_specs, out_specs=out_spec, out_shape=out_shape,
        scratch_shapes=[pltpu.VMEM((tm, tn), _F32)],
        compiler_params=_params(("parallel", "parallel", "arbitrary")),
    )(*args)


def _cast(x2d, dtype, name):
    R, C = x2d.shape
    tr = _tile(R, 512) if R % LANE == 0 else R

    def body(x_ref, o_ref):
        o_ref[...] = x_ref[...].astype(o_ref.dtype)

    return pl.pallas_call(
        body, name=name, grid=(R // tr,), in_specs=[pl.BlockSpec((tr, C), lambda i: (i, 0))],
        out_specs=pl.BlockSpec((tr, C), lambda i: (i, 0)), out_shape=_S((R, C), dtype),
        compiler_params=_params(("parallel",)),
    )(x2d)


def _ln_fwd(z, g, b, name):
    T, D = z.shape
    tr = _tile(T, 256)

    def body(z_ref, g_ref, b_ref, y_ref, yb_ref):
        zz = z_ref[...]
        mu = jnp.mean(zz, axis=-1, keepdims=True)
        zc = zz - mu
        var = jnp.mean(zc * zc, axis=-1, keepdims=True)
        y = zc * lax.rsqrt(var + LN_EPS) * g_ref[...] + b_ref[...]
        y_ref[...] = y
        yb_ref[...] = y.astype(yb_ref.dtype)

    row = pl.BlockSpec((tr, D), lambda i: (i, 0))
    vec = pl.BlockSpec((1, D), lambda i: (0, 0))
    return pl.pallas_call(
        body, name=name, grid=(T // tr,), in_specs=[row, vec, vec], out_specs=[row, row],
        out_shape=[_S((T, D), _F32), _S((T, D), _MM)], compiler_params=_params(("parallel",)),
    )(z, g, b)


def _ln_bwd(z, g, dy, name):
    T, D = z.shape
    tr = _tile(T, 256)

    def body(z_ref, g_ref, dy_ref, dz_ref, dzb_ref, dg_ref, db_ref):
        @pl.when(pl.program_id(0) == 0)
        def _():
            dg_ref[...] = jnp.zeros_like(dg_ref)
            db_ref[...] = jnp.zeros_like(db_ref)

        zz = z_ref[...]
        mu = jnp.mean(zz, axis=-1, keepdims=True)
        zc = zz - mu
        rstd = lax.rsqrt(jnp.mean(zc * zc, axis=-1, keepdims=True) + LN_EPS)
        xh = zc * rstd
        d = dy_ref[...]
        dg_ref[...] += jnp.sum(d * xh, axis=0, keepdims=True)
        db_ref[...] += jnp.sum(d, axis=0, keepdims=True)
        dxh = d * g_ref[...]
        dz = rstd * (dxh - jnp.mean(dxh, axis=-1, keepdims=True) - xh * jnp.mean(dxh * xh, axis=-1, keepdims=True))
        dz_ref[...] = dz
        dzb_ref[...] = dz.astype(dzb_ref.dtype)

    row = pl.BlockSpec((tr, D), lambda i: (i, 0))
    vec = pl.BlockSpec((1, D), lambda i: (0, 0))
    return pl.pallas_call(
        body, name=name, grid=(T // tr,), in_specs=[row, vec, row], out_specs=[row, row, vec, vec],
        out_shape=[_S((T, D), _F32), _S((T, D), _MM), _S((1, D), _F32), _S((1, D), _F32)],
        compiler_params=_params(("arbitrary",)),
    )(z, g, dy)


def _ln_loss_bwd(z, g, b, target, name):
    T, D = z.shape
    tr = _tile(T, 256)

    def body(z_ref, g_ref, b_ref, t_ref, loss_ref, dz_ref, dzb_ref, dg_ref, db_ref, lacc):
        i = pl.program_id(0)

        @pl.when(i == 0)
        def _():
            dg_ref[...] = jnp.zeros_like(dg_ref)
            db_ref[...] = jnp.zeros_like(db_ref)
            lacc[...] = jnp.zeros_like(lacc)

        zz = z_ref[...]
        mu = jnp.mean(zz, axis=-1, keepdims=True)
        zc = zz - mu
        rstd = lax.rsqrt(jnp.mean(zc * zc, axis=-1, keepdims=True) + LN_EPS)
        xh = zc * rstd
        err = xh * g_ref[...] + b_ref[...] - t_ref[...]
        lacc[...] += jnp.sum(err * err, axis=0, keepdims=True)
        d = err * (1.0 / D)
        dg_ref[...] += jnp.sum(d * xh, axis=0, keepdims=True)
        db_ref[...] += jnp.sum(d, axis=0, keepdims=True)
        dxh = d * g_ref[...]
        dz = rstd * (dxh - jnp.mean(dxh, axis=-1, keepdims=True) - xh * jnp.mean(dxh * xh, axis=-1, keepdims=True))
        dz_ref[...] = dz
        dzb_ref[...] = dz.astype(dzb_ref.dtype)

        @pl.when(i == pl.num_programs(0) - 1)
        def _():
            loss_ref[...] = jnp.sum(lacc[...], axis=-1, keepdims=True) * (0.5 / D)

    row = pl.BlockSpec((tr, D), lambda i: (i, 0))
    vec = pl.BlockSpec((1, D), lambda i: (0, 0))
    one = pl.BlockSpec((1, 1), lambda i: (0, 0))
    return pl.pallas_call(
        body, name=name, grid=(T // tr,), in_specs=[row, vec, vec, row], out_specs=[one, row, row, vec, vec],
        out_shape=[_S((1, 1), _F32), _S((T, D), _F32), _S((T, D), _MM), _S((1, D), _F32), _S((1, D), _F32)],
        scratch_shapes=[pltpu.VMEM((1, D), _F32)], compiler_params=_params(("arbitrary",)),
    )(z, g, b, target)


def _conv3(X, cw, cb):
    return cb + cw[2:3] * X + cw[1:2] * pltpu.roll(X, 1, 0) + cw[0:1] * pltpu.roll(X, 2, 0)


def _ffn_mid_fwd(h, cw, cb, name):
    _, T, F = h.shape
    tr = _tile(T, 256)
    tc = _tile(F, 512)
    nb = tr // CONV_HALO

    def body(h_ref, p_ref, cw_ref, cb_ref, o_ref):
        i = pl.program_id(0)
        hc = []
        for part in range(2):
            prev = jnp.where(i == 0, 0.0, p_ref[part])
            X = jnp.concatenate([prev, h_ref[part]], axis=0)
            hc.append(_conv3(X, cw_ref[part], cb_ref[part])[CONV_HALO:])
        a, v = hc
        o_ref[...] = (a * _sigmoid(a) * v).astype(o_ref.dtype)

    return pl.pallas_call(
        body, name=name, grid=(T // tr, F // tc),
        in_specs=[pl.BlockSpec((2, tr, tc), lambda i, j: (0, i, j)),
                  pl.BlockSpec((2, CONV_HALO, tc), lambda i, j: (0, jnp.maximum(i * nb - 1, 0), j)),
                  pl.BlockSpec((2, 3, tc), lambda i, j: (0, 0, j)),
                  pl.BlockSpec((2, 1, tc), lambda i, j: (0, 0, j))],
        out_specs=pl.BlockSpec((tr, tc), lambda i, j: (i, j)), out_shape=_S((T, F), _MM),
        compiler_params=_params(("parallel", "parallel")),
    )(h, h, cw, cb)


def _ffn_mid_bwd(h, dact, cw, cb, name):
    _, T, F = h.shape
    tr = _tile(T, 256)
    tc = _tile(F, 512)
    nb = tr // CONV_HALO
    last_blk = T // CONV_HALO - 1
    n = tr + 2 * CONV_HALO

    def body(h_ref, p_ref, n_ref, d_ref, dn_ref, cw_ref, cb_ref, dh_ref, dcw_ref, dcb_ref):
        i = pl.program_id(1)
        is_first = i == 0
        is_last = i == pl.num_programs(1) - 1

        @pl.when(is_first)
        def _():
            dcw_ref[...] = jnp.zeros_like(dcw_ref)
            dcb_ref[...] = jnp.zeros_like(dcb_ref)

        X, hc = [], []
        for part in range(2):
            prev = jnp.where(is_first, 0.0, p_ref[part])
            nxt = jnp.where(is_last, 0.0, n_ref[part])
            Xp = jnp.concatenate([prev, h_ref[part], nxt], axis=0)
            X.append(Xp)
            hc.append(_conv3(Xp, cw_ref[part], cb_ref[part]))
        D = jnp.concatenate([jnp.zeros((CONV_HALO, tc), _F32), d_ref[...], jnp.where(is_last, 0.0, dn_ref[...])], axis=0)
        a, v = hc
        sg = _sigmoid(a)
        dhc = [D * v * sg * (1.0 + a * (1.0 - sg)), D * a * sg]
        lo, hi = CONV_HALO, CONV_HALO + tr
        for part in range(2):
            g = dhc[part]
            cwp = cw_ref[part]
            dh = cwp[2:3] * g + cwp[1:2] * pltpu.roll(g, n - 1, 0) + cwp[0:1] * pltpu.roll(g, n - 2, 0)
            dh_ref[part] = dh[lo:hi].astype(dh_ref.dtype)
            gt = g[lo:hi]
            dcw_ref[part, 2:3, :] += jnp.sum(gt * X[part][lo:hi], axis=0, keepdims=True)
            dcw_ref[part, 1:2, :] += jnp.sum(gt * pltpu.roll(X[part], 1, 0)[lo:hi], axis=0, keepdims=True)
            dcw_ref[part, 0:1, :] += jnp.sum(gt * pltpu.roll(X[part], 2, 0)[lo:hi], axis=0, keepdims=True)
            dcb_ref[part] += jnp.sum(gt, axis=0, keepdims=True)

    return pl.pallas_call(
        body, name=name, grid=(F // tc, T // tr),
        in_specs=[pl.BlockSpec((2, tr, tc), lambda j, i: (0, i, j)),
                  pl.BlockSpec((2, CONV_HALO, tc), lambda j, i: (0, jnp.maximum(i * nb - 1, 0), j)),
                  pl.BlockSpec((2, CONV_HALO, tc), lambda j, i: (0, jnp.minimum((i + 1) * nb, last_blk), j)),
                  pl.BlockSpec((tr, tc), lambda j, i: (i, j)),
                  pl.BlockSpec((CONV_HALO, tc), lambda j, i: (jnp.minimum((i + 1) * nb, last_blk), j)),
                  pl.BlockSpec((2, 3, tc), lambda j, i: (0, 0, j)),
                  pl.BlockSpec((2, 1, tc), lambda j, i: (0, 0, j))],
        out_specs=[pl.BlockSpec((2, tr, tc), lambda j, i: (0, i, j)),
                   pl.BlockSpec((2, 3, tc), lambda j, i: (0, 0, j)),
                   pl.BlockSpec((2, 1, tc), lambda j, i: (0, 0, j))],
        out_shape=[_S((2, T, F), _MM), _S((2, 3, F), _F32), _S((2, 1, F), _F32)],
        compiler_params=_params(("parallel", "arbitrary")),
    )(h, h, h, dact, dact, cw, cb)


def _ev_common(h_ref, hp_ref, lng_ref, lnb_ref, ws_ref, bias_ref, i, tr, W):
    H = W // A_HEAD
    u, gu = _gelu_and_grad(h_ref[0])
    v, gv = _gelu_and_grad(h_ref[1])
    mu = jnp.mean(v, axis=-1, keepdims=True)
    vc = v - mu
    rstd = lax.rsqrt(jnp.mean(vc * vc, axis=-1, keepdims=True) + LN_EPS)
    vhat = vc * rstd
    vb = (vhat * lng_ref[...] + lnb_ref[...]).astype(_MM)
    s_chunks = []
    for c in range(tr // A_CHUNK):
        r0 = c * A_CHUNK
        heads = [_dot(ws_ref[hd], vb[r0:r0 + A_CHUNK, hd * A_HEAD:(hd + 1) * A_HEAD]) for hd in range(H)]
        s_chunks.append(jnp.concatenate(heads, axis=1) + bias_ref[...])
    prev = jnp.where(i == 0, 0.0, hp_ref[...])
    X = jnp.concatenate([prev, h_ref[2]], axis=0)
    return u, gu, gv, rstd, vhat, vb, s_chunks, X


def _pool_inv_count(i, tr, rows, win):
    pos = i * tr + _row_index(rows) + 1
    return 1.0 / jnp.minimum(pos, win).astype(_F32)


def _pool_fwd(X, g, Wg, i, tr):
    xg = X[:, g * Wg:(g + 1) * Wg]
    s = xg
    for k in range(g + 1):
        s = s + pltpu.roll(s, 2 ** k, 0)
    return s[POOL_HALO:] * _pool_inv_count(i, tr, tr, 2 ** (g + 1)) - xg[POOL_HALO:]


def _ev_mid_fwd(h, lng, lnb, ws, bias, wp, sc, name):
    _, T, W = h.shape
    tr = _tile(T, 256)
    H = W // A_HEAD
    Wg = W // B_GROUPS
    nb = tr // POOL_HALO

    def body(h_ref, hp_ref, lng_ref, lnb_ref, ws_ref, bias_ref, wp_ref, sc_ref, o_ref):
        i = pl.program_id(0)
        u, _, _, _, _, _, s_chunks, X = _ev_common(h_ref, hp_ref, lng_ref, lnb_ref, ws_ref, bias_ref, i, tr, W)
        for c, s in enumerate(s_chunks):
            r0 = c * A_CHUNK
            o_ref[r0:r0 + A_CHUNK, 0:W] = (u[r0:r0 + A_CHUNK] * s).astype(o_ref.dtype)
        for g in range(B_GROUPS):
            p = _pool_fwd(X, g, Wg, i, tr)
            y = _dot(p.astype(_MM), wp_ref[g]) * sc_ref[:, g * Wg:(g + 1) * Wg]
            o_ref[:, W + g * Wg:W + (g + 1) * Wg] = y.astype(o_ref.dtype)

    vec = pl.BlockSpec((1, W), lambda i: (0, 0))
    return pl.pallas_call(
        body, name=name, grid=(T // tr,),
        in_specs=[pl.BlockSpec((3, tr, W), lambda i: (0, i, 0)),
                  pl.BlockSpec((None, POOL_HALO, W), lambda i: (2, jnp.maximum(i * nb - 1, 0), 0)),
                  vec, vec,
                  pl.BlockSpec((H, A_CHUNK, A_CHUNK), lambda i: (0, 0, 0)),
                  pl.BlockSpec((A_CHUNK, W), lambda i: (0, 0)),
                  pl.BlockSpec((B_GROUPS, Wg, Wg), lambda i: (0, 0, 0)),
                  vec],
        out_specs=pl.BlockSpec((tr, 2 * W), lambda i: (i, 0)), out_shape=_S((T, 2 * W), _MM),
        compiler_params=_params(("parallel",)),
    )(h, h, lng, lnb, ws, bias, wp, sc)


def _ev_mid_bwd(h, dy, lng, lnb, ws, bias, wp, sc, name):
    _, T, W = h.shape
    tr = _tile(T, 256)
    H = W // A_HEAD
    Wg = W // B_GROUPS
    nb = tr // POOL_HALO
    last_blk = T // POOL_HALO - 1
    n = tr + POOL_HALO

    def body(h_ref, hp_ref, dy_ref, dyn_ref, lng_ref, lnb_ref, ws_ref, bias_ref, wp_ref, sc_ref,
             dh_ref, dws_ref, dbias_ref, dlng_ref, dlnb_ref, dwp_ref, dsc_ref):
        i = pl.program_id(0)

        @pl.when(i == 0)
        def _():
            for r in (dws_ref, dbias_ref, dlng_ref, dlnb_ref, dwp_ref, dsc_ref):
                r[...] = jnp.zeros_like(r)

        u, gu, gv, rstd, vhat, vb, s_chunks, X = _ev_common(h_ref, hp_ref, lng_ref, lnb_ref, ws_ref, bias_ref, i, tr, W)
        rr = lax.broadcasted_iota(jnp.int32, (A_CHUNK, A_CHUNK), 0)
        cc = lax.broadcasted_iota(jnp.int32, (A_CHUNK, A_CHUNK), 1)
        tril = rr >= cc
        du_chunks, dvln_chunks = [], []
        for c, s in enumerate(s_chunks):
            r0 = c * A_CHUNK
            dya = dy_ref[r0:r0 + A_CHUNK, 0:W]
            du_chunks.append(dya * s)
            ds = dya * u[r0:r0 + A_CHUNK]
            dbias_ref[...] += ds
            dsb = ds.astype(_MM)
            heads = []
            for hd in range(H):
                cols = slice(hd * A_HEAD, (hd + 1) * A_HEAD)
                dws_ref[hd] += jnp.where(tril, _dot(dsb[:, cols], vb[r0:r0 + A_CHUNK, cols], _NT), 0.0)
                heads.append(_dot(ws_ref[hd], dsb[:, cols], _TN))
            dvln_chunks.append(jnp.concatenate(heads, axis=1))
        du = jnp.concatenate(du_chunks, axis=0)
        dvln = jnp.concatenate(dvln_chunks, axis=0)
        dlng_ref[...] += jnp.sum(dvln * vhat, axis=0, keepdims=True)
        dlnb_ref[...] += jnp.sum(dvln, axis=0, keepdims=True)
        dxh = dvln * lng_ref[...]
        dv = rstd * (dxh - jnp.mean(dxh, axis=-1, keepdims=True) - vhat * jnp.mean(dxh * vhat, axis=-1, keepdims=True))
        dh_ref[0] = (du * gu).astype(dh_ref.dtype)
        dh_ref[1] = (dv * gv).astype(dh_ref.dtype)

        dyb = dy_ref[:, W:2 * W]
        dyb_full = jnp.concatenate([dyb, jnp.where(i == pl.num_programs(0) - 1, 0.0, dyn_ref[...])], axis=0)
        for g in range(B_GROUPS):
            cols = slice(g * Wg, (g + 1) * Wg)
            pb = _pool_fwd(X, g, Wg, i, tr).astype(_MM)
            ypre = _dot(pb, wp_ref[g])
            dsc_ref[:, cols] += jnp.sum(dyb[:, cols] * ypre, axis=0, keepdims=True)
            dyp = (dyb_full[:, cols] * sc_ref[:, cols]).astype(_MM)
            dwp_ref[g] += _dot(pb, dyp[0:tr], _TN)
            dp = _dot(dyp, wp_ref[g], _NT)
            s = dp * _pool_inv_count(i, tr, n, 2 ** (g + 1))
            for k in range(g + 1):
                s = s + pltpu.roll(s, n - 2 ** k, 0)
            dh_ref[2, :, cols] = (s[0:tr] - dp[0:tr]).astype(dh_ref.dtype)

    vec = pl.BlockSpec((1, W), lambda i: (0, 0))
    ws_spec = pl.BlockSpec((H, A_CHUNK, A_CHUNK), lambda i: (0, 0, 0))
    bias_spec = pl.BlockSpec((A_CHUNK, W), lambda i: (0, 0))
    wp_spec = pl.BlockSpec((B_GROUPS, Wg, Wg), lambda i: (0, 0, 0))
    return pl.pallas_call(
        body, name=name, grid=(T // tr,),
        in_specs=[pl.BlockSpec((3, tr, W), lambda i: (0, i, 0)),
                  pl.BlockSpec((None, POOL_HALO, W), lambda i: (2, jnp.maximum(i * nb - 1, 0), 0)),
                  pl.BlockSpec((tr, 2 * W), lambda i: (i, 0)),
                  pl.BlockSpec((POOL_HALO, W), lambda i: (jnp.minimum((i + 1) * nb, last_blk), 1)),
                  vec, vec, ws_spec, bias_spec, wp_spec, vec],
        out_specs=[pl.BlockSpec((3, tr, W), lambda i: (0, i, 0)), ws_spec, bias_spec, vec, vec, wp_spec, vec],
        out_shape=[_S((3, T, W), _MM), _S((H, A_CHUNK, A_CHUNK), _F32), _S((A_CHUNK, W), _F32), _S((1, W), _F32),
                   _S((1, W), _F32), _S((B_GROUPS, Wg, Wg), _F32), _S((1, W), _F32)],
        compiler_params=_params(("arbitrary",)),
    )(h, h, dy, dy, lng, lnb, ws, bias, wp, sc)


def _chunk_cumsum(x, rin):
    s = 1
    while s < C_CHUNK:
        x = x + jnp.where(rin >= s, pltpu.roll(x, s, 0), 0.0)
        s *= 2
    return x


def _chunk_revcumsum(x, rin):
    n = x.shape[0]
    s = 1
    while s < C_CHUNK:
        x = x + jnp.where(rin + s < C_CHUNK, pltpu.roll(x, n - s, 0), 0.0)
        s *= 2
    return x


def _hgrn_gates(q, fl, lb, tr, tc):
    nch = tr // C_CHUNK
    sq = _sigmoid(q)
    sf = _sigmoid(fl)
    f = lb + (1.0 - lb) * sf
    logf = jnp.log(f)
    rin = _row_index(tr) % C_CHUNK
    b = _chunk_cumsum(logf, rin)
    tot3 = jnp.sum(logf.reshape(nch, C_CHUNK, tc), axis=1, keepdims=True)
    eb = jnp.exp(b)
    enb = jnp.exp(-b)
    ekb = jnp.exp(tot3 - b.reshape(nch, C_CHUNK, tc)).reshape(tr, tc)
    return sq, sf, f, rin, tot3, eb, enb, ekb


def _hgrn_prep_fwd(h, lb, name):
    _, T, D = h.shape
    tr = _tile(T, 512)
    tc = _tile(D, 512)
    nch = tr // C_CHUNK

    def body(q_ref, f_ref, v_ref, lb_ref, qd_ref, kd_ref, ke_ref, vb_ref, dec_ref):
        q = q_ref[...]
        sq, _, f, _, tot3, eb, enb, ekb = _hgrn_gates(q, f_ref[...], lb_ref[...], tr, tc)
        kk = 1.0 - f
        qd_ref[...] = (q * sq * eb).astype(qd_ref.dtype)
        kd_ref[...] = (kk * enb).astype(kd_ref.dtype)
        ke_ref[...] = (kk * ekb).astype(ke_ref.dtype)
        vb_ref[...] = v_ref[...].astype(vb_ref.dtype)
        dec_ref[...] = jnp.exp(tot3).reshape(nch, tc)

    def part(p):
        return pl.BlockSpec((None, tr, tc), lambda i, j: (p, i, j))

    blk = pl.BlockSpec((tr, tc), lambda i, j: (i, j))
    return pl.pallas_call(
        body, name=name, grid=(T // tr, D // tc),
        in_specs=[part(0), part(1), part(2), pl.BlockSpec((1, tc), lambda i, j: (0, j))],
        out_specs=[blk, blk, blk, blk, pl.BlockSpec((nch, tc), lambda i, j: (i, j))],
        out_shape=[_S((T, D), _MM)] * 4 + [_S((T // C_CHUNK, D), _F32)],
        compiler_params=_params(("parallel", "parallel")),
    )(h, h, h, lb)


def _tril_mask():
    rr = lax.broadcasted_iota(jnp.int32, (C_CHUNK, C_CHUNK), 0)
    cc = lax.broadcasted_iota(jnp.int32, (C_CHUNK, C_CHUNK), 1)
    return rr >= cc


def _hgrn_scan_fwd(qd, kd, ke, vb, dec, h, ng, name):
    T, D = qd.shape
    NH = D // C_HEAD
    N = T // C_CHUNK

    def body(qd_ref, kd_ref, ke_ref, vb_ref, dec_ref, g_ref, ng_ref, o_ref, y_ref, st_ref):
        mask = _tril_mask()

        def step(n, St):
            r = pl.ds(pl.multiple_of(n * C_CHUNK, C_CHUNK), C_CHUNK)
            Qd, Kd, Ke, V = qd_ref[r, :], kd_ref[r, :], ke_ref[r, :], vb_ref[r, :]
            att = jnp.where(mask, _dot(Qd, Kd, _NT), 0.0).astype(_MM)
            o_ref[r, :] = _dot(att, V) + _dot(Qd, St.astype(_MM), _NT)
            st_ref[n] = St
            return St * dec_ref[pl.ds(n, 1), :] + _dot(V, Ke, _TN)

        lax.fori_loop(0, N, step, jnp.zeros((C_HEAD, C_HEAD), _F32))
        o = o_ref[...]
        r = lax.rsqrt(jnp.mean(o * o, axis=-1, keepdims=True) + LN_EPS)
        y_ref[...] = (o * r * ng_ref[...] * _sigmoid(g_ref[...])).astype(y_ref.dtype)

    col = pl.BlockSpec((T, C_HEAD), lambda j: (0, j))
    return pl.pallas_call(
        body, name=name, grid=(NH,),
        in_specs=[col, col, col, col, pl.BlockSpec((N, C_HEAD), lambda j: (0, j)),
                  pl.BlockSpec((None, T, C_HEAD), lambda j: (3, 0, j)), pl.BlockSpec((1, C_HEAD), lambda j: (0, j))],
        out_specs=[col, col, pl.BlockSpec((None, N, C_HEAD, C_HEAD), lambda j: (j, 0, 0, 0))],
        out_shape=[_S((T, D), _F32), _S((T, D), _MM), _S((NH, N, C_HEAD, C_HEAD), _F32)],
        compiler_params=_params(("parallel",)),
    )(qd, kd, ke, vb, dec, h, ng)


def _hgrn_scan_bwd(qd, kd, ke, vb, dec, st, o, h, ng, dy, name):
    T, D = qd.shape
    NH = D // C_HEAD
    N = T // C_CHUNK

    def body(qd_ref, kd_ref, ke_ref, vb_ref, dec_ref, st_ref, o_ref, g_ref, ng_ref, dy_ref,
             dqd_ref, dkd_ref, dke_ref, dv_ref, dgate_ref, ddec_ref, dng_ref, do_s):
        o = o_ref[...]
        r = lax.rsqrt(jnp.mean(o * o, axis=-1, keepdims=True) + LN_EPS)
        oh = o * r
        gn = ng_ref[...]
        sg = _sigmoid(g_ref[...])
        d = dy_ref[...]
        dyn = d * sg
        dgate_ref[...] = (d * oh * gn * sg * (1.0 - sg)).astype(dgate_ref.dtype)
        dng_ref[...] = jnp.sum(dyn * oh, axis=0, keepdims=True)
        doh = dyn * gn
        do_s[...] = (r * (doh - oh * jnp.mean(doh * oh, axis=-1, keepdims=True))).astype(do_s.dtype)
        mask = _tril_mask()

        def step(k, dSt):
            n = N - 1 - k
            rws = pl.ds(pl.multiple_of(n * C_CHUNK, C_CHUNK), C_CHUNK)
            Qd, Kd, Ke, V, dO = qd_ref[rws, :], kd_ref[rws, :], ke_ref[rws, :], vb_ref[rws, :], do_s[rws, :]
            St = st_ref[n]
            Stb = St.astype(_MM)
            dStb = dSt.astype(_MM)
            att = jnp.where(mask, _dot(Qd, Kd, _NT), 0.0).astype(_MM)
            dA = jnp.where(mask, _dot(dO, V, _NT), 0.0).astype(_MM)
            dv_ref[rws, :] = (_dot(att, dO, _TN) + _dot(Ke, dStb, _NT)).astype(dv_ref.dtype)
            dqd_ref[rws, :] = _dot(dA, Kd) + _dot(dO, Stb)
            dkd_ref[rws, :] = _dot(dA, Qd, _TN)
            dke_ref[rws, :] = _dot(V, dStb)
            ddec_ref[pl.ds(n, 1), :] = jnp.sum(dSt * St, axis=0, keepdims=True)
            return dSt * dec_ref[pl.ds(n, 1), :] + _dot(dO, Qd, _TN)

        lax.fori_loop(0, N, step, jnp.zeros((C_HEAD, C_HEAD), _F32))

    col = pl.BlockSpec((T, C_HEAD), lambda j: (0, j))
    chk = pl.BlockSpec((N, C_HEAD), lambda j: (0, j))
    one = pl.BlockSpec((1, C_HEAD), lambda j: (0, j))
    return pl.pallas_call(
        body, name=name, grid=(NH,),
        in_specs=[col, col, col, col, chk, pl.BlockSpec((None, N, C_HEAD, C_HEAD), lambda j: (j, 0, 0, 0)), col,
                  pl.BlockSpec((None, T, C_HEAD), lambda j: (3, 0, j)), one, col],
        out_specs=[col, col, col, col, col, chk, one],
        out_shape=[_S((T, D), _F32)] * 3 + [_S((T, D), _MM)] * 2 + [_S((N, D), _F32), _S((1, D), _F32)],
        scratch_shapes=[pltpu.VMEM((T, C_HEAD), _MM)],
        compiler_params=_params(("parallel",)),
    )(qd, kd, ke, vb, dec, st, o, h, ng, dy)


def _hgrn_prep_bwd(h, lb, dqd, dkd, dke, dv, dgate, ddec, name):
    _, T, D = h.shape
    tr = _tile(T, 512)
    tc = _tile(D, 256)
    nch = tr // C_CHUNK

    def body(q_ref, f_ref, lb_ref, dqd_ref, dkd_ref, dke_ref, dv_ref, dgate_ref, ddec_ref, dh_ref, dlb_ref):
        @pl.when(pl.program_id(1) == 0)
        def _():
            dlb_ref[...] = jnp.zeros_like(dlb_ref)

        q = q_ref[...]
        lb = lb_ref[...]
        sq, sf, f, rin, tot3, eb, enb, ekb = _hgrn_gates(q, f_ref[...], lb, tr, tc)
        kk = 1.0 - f
        dQd, dKd, dKe = dqd_ref[...], dkd_ref[...], dke_ref[...]
        tq = dQd * eb
        tkd = dKd * enb
        tke = dKe * ekb
        ke_term = tke * kk
        db = tq * (q * sq) - tkd * kk - ke_term
        dtot3 = (jnp.sum(ke_term.reshape(nch, C_CHUNK, tc), axis=1, keepdims=True)
                 + (ddec_ref[...] * jnp.exp(tot3).reshape(nch, tc)).reshape(nch, 1, tc))
        dlogf = (_chunk_revcumsum(db, rin).reshape(nch, C_CHUNK, tc) + dtot3).reshape(tr, tc)
        df = dlogf / f - (tkd + tke)
        dh_ref[0] = (tq * sq * (1.0 + q * (1.0 - sq))).astype(dh_ref.dtype)
        dh_ref[1] = (df * (1.0 - lb) * sf * (1.0 - sf)).astype(dh_ref.dtype)
        dh_ref[2] = dv_ref[...]
        dh_ref[3] = dgate_ref[...]
        dlb_ref[...] += jnp.sum(df * (1.0 - sf), axis=0, keepdims=True)

    def part(p):
        return pl.BlockSpec((None, tr, tc), lambda j, i: (p, i, j))

    blk = pl.BlockSpec((tr, tc), lambda j, i: (i, j))
    vec = pl.BlockSpec((1, tc), lambda j, i: (0, j))
    return pl.pallas_call(
        body, name=name, grid=(D // tc, T // tr),
        in_specs=[part(0), part(1), vec, blk, blk, blk, blk, blk, pl.BlockSpec((nch, tc), lambda j, i: (i, j))],
        out_specs=[pl.BlockSpec((4, tr, tc), lambda j, i: (0, i, j)), vec],
        out_shape=[_S((4, T, D), _MM), _S((1, D), _F32)],
        compiler_params=_params(("parallel", "arbitrary")),
    )(h, h, lb, dqd, dkd, dke, dv, dgate, ddec)


def _sum_slots(parts, name):
    P, R, C = parts.shape
    tr = _tile(R, 256) if R % 8 == 0 and R >= 256 else R

    def body(p_ref, o_ref):
        g = p_ref[0].astype(_F32)
        for s in range(1, P):
            g = g + p_ref[s].astype(_F32)
        o_ref[...] = g

    return pl.pallas_call(
        body, name=name, grid=(R // tr,), in_specs=[pl.BlockSpec((P, tr, C), lambda i: (0, i, 0))],
        out_specs=pl.BlockSpec((tr, C), lambda i: (i, 0)), out_shape=_S((R, C), _F32),
        compiler_params=_params(("parallel",)),
    )(parts)


def _adamw(parts, w, m, v, name):
    P, R, C = parts.shape
    tr = _tile(R, 128) if R % LANE == 0 else R

    def body(p_ref, w_ref, m_ref, v_ref, g_ref, d_ref, nm_ref, nv_ref):
        g = p_ref[0].astype(_F32)
        for s in range(1, P):
            g = g + p_ref[s].astype(_F32)
        nm = ADAM_B1 * m_ref[...] + (1.0 - ADAM_B1) * g
        nv = ADAM_B2 * v_ref[...] + (1.0 - ADAM_B2) * (g * g)
        m_hat = nm / (1.0 - ADAM_B1 ** ADAM_STEP)
        v_hat = nv / (1.0 - ADAM_B2 ** ADAM_STEP)
        g_ref[...] = g
        d_ref[...] = -ADAM_LR * (m_hat / (jnp.sqrt(v_hat) + ADAM_EPS) + ADAM_WD * w_ref[...])
        nm_ref[...] = nm
        nv_ref[...] = nv

    blk = pl.BlockSpec((tr, C), lambda i: (i, 0))
    return pl.pallas_call(
        body, name=name, grid=(R // tr,), in_specs=[pl.BlockSpec((P, tr, C), lambda i: (0, i, 0)), blk, blk, blk],
        out_specs=[blk] * 4, out_shape=[_S((R, C), _F32)] * 4, compiler_params=_params(("parallel",)),
    )(parts, w, m, v)


def _exchange(name, srcs, out_shapes, jobs):
    ns, nj = len(srcs), len(jobs)

    def body(*refs):
        ins, outs = refs[:ns], refs[ns:ns + len(out_shapes)]
        send_sems, recv_sems, local_sems = refs[-3:]
        x, y, c = lax.axis_index("x"), lax.axis_index("y"), lax.axis_index("c")
        me = 4 * x + 2 * y + c
        local = []
        for ji, (si, src_fn, di, dst_fn) in enumerate(jobs):
            cp = pltpu.make_async_copy(src_fn(ins[si], me, me), dst_fn(outs[di], me), local_sems.at[ji])
            cp.start()
            local.append(cp)
        remote = []
        for k in range(1, N_DEV):
            px, py, pc = (x + (k >> 2)) % 2, (y + ((k >> 1) & 1)) % 2, (c + (k & 1)) % 2
            to = 4 * px + 2 * py + pc
            for ji, (si, src_fn, di, dst_fn) in enumerate(jobs):
                sem = (k - 1) * nj + ji
                cp = pltpu.make_async_remote_copy(
                    src_ref=src_fn(ins[si], me, to), dst_ref=dst_fn(outs[di], me),
                    send_sem=send_sems.at[sem], recv_sem=recv_sems.at[sem],
                    device_id=(px, py, pc), device_id_type=pl.DeviceIdType.MESH)
                cp.start()
                remote.append(cp)
        for cp in remote:
            cp.wait_recv()
        for cp in remote:
            cp.wait_send()
        for cp in local:
            cp.wait()

    hbm = pl.BlockSpec(memory_space=pltpu.HBM)
    return pl.pallas_call(
        body, name=name, in_specs=[hbm] * ns, out_specs=[hbm] * len(out_shapes), out_shape=list(out_shapes),
        scratch_shapes=[pltpu.SemaphoreType.DMA(((N_DEV - 1) * nj,)), pltpu.SemaphoreType.DMA(((N_DEV - 1) * nj,)),
                        pltpu.SemaphoreType.DMA((nj,))],
    )(*srcs)


def _whole(ref, me, to):
    return ref


def _gather_job(i, o, axis, n, sub=None):
    def src(ref, me, to):
        return ref if sub is None else ref.at[sub]

    def dst(ref, me):
        idx = [slice(None)] * axis + [pl.ds(pl.multiple_of(me * n, n), n)]
        return ref.at[tuple(idx)]
    return (i, src, o, dst)


def _scatter_job(i, axis, n, o, sub=None):
    def src(ref, me, to):
        idx = [slice(None)] * axis + [pl.ds(pl.multiple_of(to * n, n), n)]
        return ref.at[tuple(idx)]

    def dst(ref, me):
        return ref.at[me] if sub is None else ref.at[me, sub]
    return (i, src, o, dst)


def _slot_job(i, o):
    def dst(ref, me):
        return ref.at[me]
    return (i, _whole, o, dst)


def _pack(arrs):
    flat = jnp.concatenate([a.reshape(-1).astype(_F32) for a in arrs])
    pad = (-flat.shape[0]) % (LANE * LANE)
    return jnp.pad(flat, (0, pad)).reshape(-1, LANE)


def _unpack(mat, shapes):
    flat = mat.reshape(-1)
    out, off = [], 0
    for s in shapes:
        n = 1
        for d in s:
            n *= d
        out.append(flat[off:off + n].reshape(s))
        off += n
    return out


def _lb_of(lb_param):
    lb_all = jnp.cumsum(jax.nn.softmax(lb_param.astype(_F32), axis=0), axis=0)
    return (lb_all - lb_all[0])[1:2]


def kernel(x, ev_w_in, ev_ln_v_g, ev_ln_v_b, ev_w_s, ev_b_s, ev_w_pool, ev_pool_scale, ev_w_out, od_w_in, od_norm_g, od_w_out, lb_param, ffn_w_up, ffn_conv_w, ffn_conv_b, ffn_w_down, ln1_g, ln1_b, ln2_g, ln2_b, loss_target, m_ev_w_in, m_ev_ln_v_g, m_ev_ln_v_b, m_ev_w_s, m_ev_b_s, m_ev_w_pool, m_ev_pool_scale, m_ev_w_out, m_od_w_in, m_od_norm_g, m_od_w_out, m_lb_param, m_ffn_w_up, m_ffn_conv_w, m_ffn_conv_b, m_ffn_w_down, m_ln1_g, m_ln1_b, m_ln2_g, m_ln2_b, v_ev_w_in, v_ev_ln_v_g, v_ev_ln_v_b, v_ev_w_s, v_ev_b_s, v_ev_w_pool, v_ev_pool_scale, v_ev_w_out, v_od_w_in, v_od_norm_g, v_od_w_out, v_lb_param, v_ffn_w_up, v_ffn_conv_w, v_ffn_conv_b, v_ffn_w_down, v_ln1_g, v_ln1_b, v_ln2_g, v_ln2_b):
    me = 4 * lax.axis_index("x") + 2 * lax.axis_index("y") + lax.axis_index("c")
    T, D = x.shape[1], x.shape[2]
    W = ev_ln_v_g.shape[1]
    H = W // A_HEAD
    Wg = W // B_GROUPS
    F2 = ffn_conv_b.shape[1]
    F = F2 // 2
    n_in0, n_out0 = ev_w_in.shape[2], ev_w_out.shape[1]
    n_in1, n_out1 = od_w_in.shape[2], od_w_out.shape[1]
    n_up, n_dn = ffn_w_up.shape[2], ffn_w_down.shape[1]
    n_pool, n_ng, n_cw = ev_w_pool.shape[2], od_norm_g.shape[1], ffn_conv_w.shape[2]

    small_shards = [od_norm_g, ffn_conv_w, ev_w_pool]
    small_pack = _pack(small_shards)
    cast = [_cast(ev_w_in[0], _MM, "cast_w_in0"), _cast(ev_w_out[0], _MM, "cast_w_out0"),
            _cast(od_w_in[0], _MM, "cast_w_in1"), _cast(od_w_out[0], _MM, "cast_w_out1"),
            _cast(ffn_w_up.reshape(DEPTH * D, n_up), _MM, "cast_w_up").reshape(DEPTH, D, n_up),
            _cast(ffn_w_down.reshape(DEPTH * n_dn, D), _MM, "cast_w_dn").reshape(DEPTH, n_dn, D)]
    gathered = _exchange(
        "gather_weights", cast + [small_pack],
        [_S((D, N_DEV * n_in0), _MM), _S((N_DEV * n_out0, D), _MM), _S((D, N_DEV * n_in1), _MM),
         _S((N_DEV * n_out1, D), _MM), _S((D, N_DEV * n_up), _MM), _S((D, N_DEV * n_up), _MM),
         _S((N_DEV * n_dn, D), _MM), _S((N_DEV * n_dn, D), _MM), _S((N_DEV,) + small_pack.shape, _F32)],
        [_gather_job(0, 0, 1, n_in0), _gather_job(1, 1, 0, n_out0), _gather_job(2, 2, 1, n_in1),
         _gather_job(3, 3, 0, n_out1), _gather_job(4, 4, 1, n_up, 0), _gather_job(4, 5, 1, n_up, 1),
         _gather_job(5, 6, 0, n_dn, 0), _gather_job(5, 7, 0, n_dn, 1), _slot_job(6, 8)])
    w_in0, w_out0, w_in1, w_out1, w_up0, w_up1, w_dn0, w_dn1, small_all = gathered
    w_up, w_dn = (w_up0, w_up1), (w_dn0, w_dn1)
    ng_parts, cw_parts, wp_parts = [], [], []
    for j in range(N_DEV):
        a, b, c = _unpack(small_all[j], [s.shape for s in small_shards])
        ng_parts.append(a)
        cw_parts.append(b)
        wp_parts.append(c)
    norm_g = jnp.concatenate(ng_parts, axis=1)
    conv_w = jnp.concatenate(cw_parts, axis=2)
    w_pool = jnp.concatenate(wp_parts, axis=2)[0]
    cw_l = [conv_w[l].reshape(3, 2, F).transpose(1, 0, 2) for l in range(DEPTH)]
    cb_l = [ffn_conv_b[l].reshape(2, 1, F) for l in range(DEPTH)]
    ws_tril = jnp.tril(ev_w_s[0]).astype(_MM)
    bias = jnp.repeat(ev_b_s[0].T, A_HEAD, axis=1)
    wp_b = w_pool.astype(_MM)
    lb, lb_vjp = jax.vjp(_lb_of, lb_param)

    x2 = x[0]
    xb = _cast(x2, _MM, "cast_x")
    h0 = _mm(xb, w_in0, "nn", _F32, "ev_in", out_parts=3)
    yab = _ev_mid_fwd(h0, ev_ln_v_g, ev_ln_v_b, ws_tril, bias, wp_b, ev_pool_scale, "ev_mid_fwd")
    z1 = _mm(yab, w_out0, "nn", _F32, "ev_out", add=x2, add_scale=ALPHA)
    x1, x1b = _ln_fwd(z1, ln1_g[0:1], ln1_b[0:1], "ln1_0")
    hf0 = _mm(x1b, w_up[0], "nn", _F32, "ffn_up", out_parts=2)
    act0 = _ffn_mid_fwd(hf0, cw_l[0], cb_l[0], "ffn_mid_fwd")
    z2 = _mm(act0, w_dn[0], "nn", _F32, "ffn_down", add=x1, add_scale=ALPHA)
    x2_, x2b = _ln_fwd(z2, ln2_g[0:1], ln2_b[0:1], "ln2_0")
    h1 = _mm(x2b, w_in1, "nn", _F32, "od_in", out_parts=4)
    qd, kd, ke, vb, dec = _hgrn_prep_fwd(h1, lb, "hgrn_prep_fwd")
    o, yo, st = _hgrn_scan_fwd(qd, kd, ke, vb, dec, h1, norm_g, "hgrn_scan_fwd")
    z3 = _mm(yo, w_out1, "nn", _F32, "od_out", add=x2_, add_scale=ALPHA)
    x3, x3b = _ln_fwd(z3, ln1_g[1:2], ln1_b[1:2], "ln1_1")
    hf1 = _mm(x3b, w_up[1], "nn", _F32, "ffn_up", out_parts=2)
    act1 = _ffn_mid_fwd(hf1, cw_l[1], cb_l[1], "ffn_mid_fwd")
    z4 = _mm(act1, w_dn[1], "nn", _F32, "ffn_down", add=x3, add_scale=ALPHA)

    loss11, dz4, dz4b, g_ln2_1, b_ln2_1 = _ln_loss_bwd(z4, ln2_g[1:2], ln2_b[1:2], loss_target[0], "ln_loss_bwd")
    dact1 = _mm(dz4b, w_dn[1], "nt", _F32, "ffn_down_dx")
    gw_dn1 = _mm(act1, dz4b, "tn", _XCH, "ffn_down_dw")
    dhf1, dcw1, dcb1 = _ffn_mid_bwd(hf1, dact1, cw_l[1], cb_l[1], "ffn_mid_bwd")
    dx3 = _mm(dhf1, w_up[1], "nt", _F32, "ffn_up_dx", a_parts=2, add=dz4, add_scale=ALPHA)
    gw_up1 = _mm(x3b, dhf1, "tn", _XCH, "ffn_up_dw", b_parts=2)
    dz3, dz3b, g_ln1_1, b_ln1_1 = _ln_bwd(z3, ln1_g[1:2], dx3, "ln_bwd")
    dyo = _mm(dz3b, w_out1, "nt", _F32, "od_out_dx")
    gw_out1 = _mm(yo, dz3b, "tn", _XCH, "od_out_dw")
    dqd, dkd, dke, dv, dgate, ddec, dng = _hgrn_scan_bwd(qd, kd, ke, vb, dec, st, o, h1, norm_g, dyo, "hgrn_scan_bwd")
    dh1, dlb = _hgrn_prep_bwd(h1, lb, dqd, dkd, dke, dv, dgate, ddec, "hgrn_prep_bwd")
    dx2 = _mm(dh1, w_in1, "nt", _F32, "od_in_dx", a_parts=4, add=dz3, add_scale=ALPHA)
    gw_in1 = _mm(x2b, dh1, "tn", _XCH, "od_in_dw", b_parts=4)
    dz2, dz2b, g_ln2_0, b_ln2_0 = _ln_bwd(z2, ln2_g[0:1], dx2, "ln_bwd")
    dact0 = _mm(dz2b, w_dn[0], "nt", _F32, "ffn_down_dx")
    gw_dn0 = _mm(act0, dz2b, "tn", _XCH, "ffn_down_dw")
    dhf0, dcw0, dcb0 = _ffn_mid_bwd(hf0, dact0, cw_l[0], cb_l[0], "ffn_mid_bwd")
    dx1 = _mm(dhf0, w_up[0], "nt", _F32, "ffn_up_dx", a_parts=2, add=dz2, add_scale=ALPHA)
    gw_up0 = _mm(x1b, dhf0, "tn", _XCH, "ffn_up_dw", b_parts=2)
    dz1, dz1b, g_ln1_0, b_ln1_0 = _ln_bwd(z1, ln1_g[0:1], dx1, "ln_bwd")
    dyab = _mm(dz1b, w_out0, "nt", _F32, "ev_out_dx")
    gw_out0 = _mm(yab, dz1b, "tn", _XCH, "ev_out_dw")
    dh0, dws, dbias, dlng, dlnb, dwp, dsc = _ev_mid_bwd(h0, dyab, ev_ln_v_g, ev_ln_v_b, ws_tril, bias, wp_b,
                                                        ev_pool_scale, "ev_mid_bwd")
    grad_x = _mm(dh0, w_in0, "nt", _F32, "ev_in_dx", a_parts=3, add=dz1, add_scale=ALPHA)
    gw_in0 = _mm(xb, dh0, "tn", _XCH, "ev_in_dw", b_parts=3)

    recv = _exchange(
        "scatter_grads", [gw_in0, gw_out0, gw_in1, gw_out1, gw_up0, gw_up1, gw_dn0, gw_dn1],
        [_S((N_DEV, D, n_in0), _XCH), _S((N_DEV, n_out0, D), _XCH), _S((N_DEV, D, n_in1), _XCH),
         _S((N_DEV, n_out1, D), _XCH), _S((N_DEV, DEPTH, D, n_up), _XCH), _S((N_DEV, DEPTH, n_dn, D), _XCH)],
        [_scatter_job(0, 1, n_in0, 0), _scatter_job(1, 0, n_out0, 1), _scatter_job(2, 1, n_in1, 2),
         _scatter_job(3, 0, n_out1, 3), _scatter_job(4, 1, n_up, 4, 0), _scatter_job(5, 1, n_up, 4, 1),
         _scatter_job(6, 0, n_dn, 5, 0), _scatter_job(7, 0, n_dn, 5, 1)])
    big = {}
    big["ev_w_in"] = _adamw(recv[0], ev_w_in[0], m_ev_w_in[0], v_ev_w_in[0], "adamw_w_in0")
    big["ev_w_out"] = _adamw(recv[1], ev_w_out[0], m_ev_w_out[0], v_ev_w_out[0], "adamw_w_out0")
    big["od_w_in"] = _adamw(recv[2], od_w_in[0], m_od_w_in[0], v_od_w_in[0], "adamw_w_in1")
    big["od_w_out"] = _adamw(recv[3], od_w_out[0], m_od_w_out[0], v_od_w_out[0], "adamw_w_out1")
    big["ffn_w_up"] = _adamw(recv[4].reshape(N_DEV, DEPTH * D, n_up), ffn_w_up.reshape(DEPTH * D, n_up),
                             m_ffn_w_up.reshape(DEPTH * D, n_up), v_ffn_w_up.reshape(DEPTH * D, n_up), "adamw_w_up")
    big["ffn_w_down"] = _adamw(recv[5].reshape(N_DEV, DEPTH * n_dn, D), ffn_w_down.reshape(DEPTH * n_dn, D),
                               m_ffn_w_down.reshape(DEPTH * n_dn, D), v_ffn_w_down.reshape(DEPTH * n_dn, D), "adamw_w_dn")

    g_b_s = dbias.reshape(A_CHUNK, H, A_HEAD).sum(axis=-1).T[None]
    g_conv_w = jnp.stack([d.transpose(1, 0, 2).reshape(3, F2) for d in (dcw0, dcw1)])
    g_conv_b = jnp.stack([d.reshape(F2) for d in (dcb0, dcb1)])
    small_names = ["ev_ln_v_g", "ev_ln_v_b", "ev_w_s", "ev_b_s", "ev_w_pool", "ev_pool_scale", "od_norm_g", "lb_param",
                   "ffn_conv_w", "ffn_conv_b", "ln1_g", "ln1_b", "ln2_g", "ln2_b"]
    small_grads = [dlng, dlnb, dws[None], g_b_s, dwp[None], dsc, dng, lb_vjp(dlb)[0], g_conv_w, g_conv_b,
                   jnp.concatenate([g_ln1_0, g_ln1_1]), jnp.concatenate([b_ln1_0, b_ln1_1]),
                   jnp.concatenate([g_ln2_0, g_ln2_1]), jnp.concatenate([b_ln2_0, b_ln2_1])]
    full_shapes = [g.shape for g in small_grads]
    gpack = _pack(small_grads)
    gall = _exchange("gather_small_grads", [gpack], [_S((N_DEV,) + gpack.shape, _F32)], [_slot_job(0, 0)])[0]
    gsum = _unpack(_sum_slots(gall, "sum_small_grads"), full_shapes)
    given = dict(ev_ln_v_g=(ev_ln_v_g, m_ev_ln_v_g, v_ev_ln_v_g), ev_ln_v_b=(ev_ln_v_b, m_ev_ln_v_b, v_ev_ln_v_b),
                 ev_w_s=(ev_w_s, m_ev_w_s, v_ev_w_s), ev_b_s=(ev_b_s, m_ev_b_s, v_ev_b_s),
                 ev_w_pool=(ev_w_pool, m_ev_w_pool, v_ev_w_pool),
                 ev_pool_scale=(ev_pool_scale, m_ev_pool_scale, v_ev_pool_scale),
                 od_norm_g=(od_norm_g, m_od_norm_g, v_od_norm_g), lb_param=(lb_param, m_lb_param, v_lb_param),
                 ffn_conv_w=(ffn_conv_w, m_ffn_conv_w, v_ffn_conv_w), ffn_conv_b=(ffn_conv_b, m_ffn_conv_b, v_ffn_conv_b),
                 ln1_g=(ln1_g, m_ln1_g, v_ln1_g), ln1_b=(ln1_b, m_ln1_b, v_ln1_b), ln2_g=(ln2_g, m_ln2_g, v_ln2_g),
                 ln2_b=(ln2_b, m_ln2_b, v_ln2_b))
    shard_axis = dict(ev_w_pool=(2, n_pool), od_norm_g=(1, n_ng), ffn_conv_w=(2, n_cw))
    local_g = []
    for name, g in zip(small_names, gsum):
        if name in shard_axis:
            ax, n = shard_axis[name]
            g = lax.dynamic_slice_in_dim(g, me * n, n, axis=ax)
        local_g.append(g)
    local_shapes = [g.shape for g in local_g]
    res = _adamw(_pack(local_g)[None], _pack([given[n][0] for n in small_names]), _pack([given[n][1] for n in small_names]),
                 _pack([given[n][2] for n in small_names]), "adamw_small")
    small = {n: [] for n in small_names}
    for r in res:
        for n, a in zip(small_names, _unpack(r, local_shapes)):
            small[n].append(a)

    loss = lax.psum(loss11[0, 0], ("x", "y", "c"))
    order = ["ev_w_in", "ev_ln_v_g", "ev_ln_v_b", "ev_w_s", "ev_b_s", "ev_w_pool", "ev_pool_scale", "ev_w_out", "od_w_in",
             "od_norm_g", "od_w_out", "lb_param", "ffn_w_up", "ffn_conv_w", "ffn_conv_b", "ffn_w_down", "ln1_g", "ln1_b",
             "ln2_g", "ln2_b"]
    shapes = dict(ev_w_in=ev_w_in.shape, ev_w_out=ev_w_out.shape, od_w_in=od_w_in.shape, od_w_out=od_w_out.shape,
                  ffn_w_up=ffn_w_up.shape, ffn_w_down=ffn_w_down.shape)
    outs = [loss, grad_x[None]]
    for kind in range(4):
        for n in order:
            outs.append(big[n][kind].reshape(shapes[n]) if n in big else small[n][kind])
    return tuple(outs)
```

```python
import functools

import jax
import jax.numpy as jnp
from jax import lax
from jax.experimental import pallas as pl
from jax.experimental.pallas import tpu as pltpu

_MM = jnp.bfloat16
_XCH = jnp.bfloat16

DEPTH = 2
ALPHA = (2 * DEPTH) ** 0.25
LN_EPS = 1e-5
A_CHUNK = 128
A_HEAD = 128
B_GROUPS = 4
POOL_HALO = 16
C_CHUNK = 64
C_HEAD = 128
CONV_HALO = 8
ADAM_LR, ADAM_B1, ADAM_B2, ADAM_EPS, ADAM_WD, ADAM_STEP = 0.001, 0.9, 0.999, 1e-08, 0.01, 10
N_DEV = 8
LANE = 128
VMEM_LIMIT = 56 * 1024 * 1024

_F32 = jnp.float32
_NN = (((1,), (0,)), ((), ()))
_NT = (((1,), (1,)), ((), ()))
_TN = (((0,), (0,)), ((), ()))
_S = jax.ShapeDtypeStruct


def _dot(a, b, dims=_NN):
    return lax.dot_general(a, b, dims, preferred_element_type=_F32)


def _tile(dim, pref):
    best = None
    d = LANE
    while d <= min(dim, pref):
        if dim % d == 0:
            best = d
        d += LANE
    return best if best is not None else dim


def _params(sem):
    return pltpu.CompilerParams(dimension_semantics=sem, vmem_limit_bytes=VMEM_LIMIT)


def _sigmoid(x):
    return 1.0 / (1.0 + jnp.exp(-x))


_GELU_C = 0.7978845608028654
_GELU_A = 0.044715


def _gelu_and_grad(x):
    t = jnp.tanh(_GELU_C * (x + _GELU_A * x * x * x))
    y = 0.5 * x * (1.0 + t)
    dy = 0.5 * (1.0 + t) + 0.5 * x * (1.0 - t * t) * _GELU_C * (1.0 + 3.0 * _GELU_A * x * x)
    return y, dy


def _row_index(n):
    return lax.broadcasted_iota(jnp.int32, (n, 1), 0)


def _mm(a, b, mode, out_dtype, name, *, a_parts=1, b_parts=1, out_parts=1, add=None, add_scale=1.0, deps=()):
    if mode == "nn":
        M, K = a.shape
        N = b.shape[1]
    elif mode == "nt":
        if a_parts > 1:
            M, K = a.shape[1], a.shape[2] * a_parts
        else:
            M, K = a.shape
        N = b.shape[0]
    else:
        K, M = a.shape
        N = b.shape[-1] * b_parts
    tm = _tile(M, 1024)
    tn = _tile(N // max(b_parts, out_parts), 1536)
    tk = _tile(K // a_parts, 512)
    nk = K // tk
    npj = (N // max(b_parts, out_parts)) // tn
    nkp = (K // a_parts) // tk
    if mode == "nn":
        a_spec = pl.BlockSpec((tm, tk), lambda i, j, k: (i, k))
        b_spec = pl.BlockSpec((tk, tn), lambda i, j, k: (k, j))
        dims = _NN
    elif mode == "nt":
        if a_parts > 1:
            a_spec = pl.BlockSpec((None, tm, tk), lambda i, j, k: (k // nkp, i, k % nkp))
        else:
            a_spec = pl.BlockSpec((tm, tk), lambda i, j, k: (i, k))
        b_spec = pl.BlockSpec((tn, tk), lambda i, j, k: (j, k))
        dims = _NT
    else:
        a_spec = pl.BlockSpec((tk, tm), lambda i, j, k: (k, i))
        if b_parts > 1:
            b_spec = pl.BlockSpec((None, tk, tn), lambda i, j, k: (j // npj, k, j % npj))
        else:
            b_spec = pl.BlockSpec((tk, tn), lambda i, j, k: (k, j))
        dims = _TN
    if out_parts > 1:
        out_spec = pl.BlockSpec((None, tm, tn), lambda i, j, k: (j // npj, i, j % npj))
        out_shape = _S((out_parts, M, N // out_parts), out_dtype)
    else:
        out_spec = pl.BlockSpec((tm, tn), lambda i, j, k: (i, j))
        out_shape = _S((M, N), out_dtype)
    in_specs = [a_spec, b_spec]
    args = [a, b]
    if add is not None:
        in_specs.append(pl.BlockSpec((tm, tn), lambda i, j, k: (i, j)))
        args.append(add)
    in_specs += [_ANY] * len(deps)
    args += list(deps)

    def body(*refs):
        a_ref, b_ref = refs[0], refs[1]
        o_ref, acc = refs[-2], refs[-1]
        k = pl.program_id(2)

        @pl.when(k == 0)
        def _():
            acc[...] = jnp.zeros_like(acc)

        acc[...] += _dot(a_ref[...], b_ref[...], dims)

        @pl.when(k == nk - 1)
        def _():
            r = acc[...]
            if add is not None:
                r = r + add_scale * refs[2][...]
            o_ref[...] = r.astype(o_ref.dtype)

    return pl.pallas_call(
        body, name=name, grid=(M // tm, N // tn, nk), in_specs=in_specs, out_specs=out_spec, out_shape=out_shape,
        scratch_shapes=[pltpu.VMEM((tm, tn), _F32)],
        compiler_params=_params(("parallel", "parallel", "arbitrary")),
    )(*args)


def _cast(x2d, dtype, name, deps=()):
    R, C = x2d.shape
    tr = _tile(R, 512) if R % LANE == 0 else R

    def body(x_ref, *rest):
        rest[-1][...] = x_ref[...].astype(rest[-1].dtype)

    return pl.pallas_call(
        body, name=name, grid=(R // tr,), in_specs=[pl.BlockSpec((tr, C), lambda i: (i, 0))] + [_ANY] * len(deps),
        out_specs=pl.BlockSpec((tr, C), lambda i: (i, 0)), out_shape=_S((R, C), dtype),
        compiler_params=_params(("parallel",)),
    )(x2d, *deps)


def _ln_fwd(z, g, b, name):
    T, D = z.shape
    tr = _tile(T, 256)

    def body(z_ref, g_ref, b_ref, y_ref, yb_ref):
        zz = z_ref[...]
        mu = jnp.mean(zz, axis=-1, keepdims=True)
        zc = zz - mu
        var = jnp.mean(zc * zc, axis=-1, keepdims=True)
        y = zc * lax.rsqrt(var + LN_EPS) * g_ref[...] + b_ref[...]
        y_ref[...] = y
        yb_ref[...] = y.astype(yb_ref.dtype)

    row = pl.BlockSpec((tr, D), lambda i: (i, 0))
    vec = pl.BlockSpec((1, D), lambda i: (0, 0))
    return pl.pallas_call(
        body, name=name, grid=(T // tr,), in_specs=[row, vec, vec], out_specs=[row, row],
        out_shape=[_S((T, D), _F32), _S((T, D), _MM)], compiler_params=_params(("parallel",)),
    )(z, g, b)


def _ln_bwd(z, g, dy, name):
    T, D = z.shape
    tr = _tile(T, 256)

    def body(z_ref, g_ref, dy_ref, dz_ref, dzb_ref, dg_ref, db_ref):
        @pl.when(pl.program_id(0) == 0)
        def _():
            dg_ref[...] = jnp.zeros_like(dg_ref)
            db_ref[...] = jnp.zeros_like(db_ref)

        zz = z_ref[...]
        mu = jnp.mean(zz, axis=-1, keepdims=True)
        zc = zz - mu
        rstd = lax.rsqrt(jnp.mean(zc * zc, axis=-1, keepdims=True) + LN_EPS)
        xh = zc * rstd
        d = dy_ref[...]
        dg_ref[...] += jnp.sum(d * xh, axis=0, keepdims=True)
        db_ref[...] += jnp.sum(d, axis=0, keepdims=True)
        dxh = d * g_ref[...]
        dz = rstd * (dxh - jnp.mean(dxh, axis=-1, keepdims=True) - xh * jnp.mean(dxh * xh, axis=-1, keepdims=True))
        dz_ref[...] = dz
        dzb_ref[...] = dz.astype(dzb_ref.dtype)

    row = pl.BlockSpec((tr, D), lambda i: (i, 0))
    vec = pl.BlockSpec((1, D), lambda i: (0, 0))
    return pl.pallas_call(
        body, name=name, grid=(T // tr,), in_specs=[row, vec, row], out_specs=[row, row, vec, vec],
        out_shape=[_S((T, D), _F32), _S((T, D), _MM), _S((1, D), _F32), _S((1, D), _F32)],
        compiler_params=_params(("arbitrary",)),
    )(z, g, dy)


def _ln_loss_bwd(z, g, b, target, name):
    T, D = z.shape
    tr = _tile(T, 256)

    def body(z_ref, g_ref, b_ref, t_ref, loss_ref, dz_ref, dzb_ref, dg_ref, db_ref, lacc):
        i = pl.program_id(0)

        @pl.when(i == 0)
        def _():
            dg_ref[...] = jnp.zeros_like(dg_ref)
            db_ref[...] = jnp.zeros_like(db_ref)
            lacc[...] = jnp.zeros_like(lacc)

        zz = z_ref[...]
        mu = jnp.mean(zz, axis=-1, keepdims=True)
        zc = zz - mu
        rstd = lax.rsqrt(jnp.mean(zc * zc, axis=-1, keepdims=True) + LN_EPS)
        xh = zc * rstd
        err = xh * g_ref[...] + b_ref[...] - t_ref[...]
        lacc[...] += jnp.sum(err * err, axis=0, keepdims=True)
        d = err * (1.0 / D)
        dg_ref[...] += jnp.sum(d * xh, axis=0, keepdims=True)
        db_ref[...] += jnp.sum(d, axis=0, keepdims=True)
        dxh = d * g_ref[...]
        dz = rstd * (dxh - jnp.mean(dxh, axis=-1, keepdims=True) - xh * jnp.mean(dxh * xh, axis=-1, keepdims=True))
        dz_ref[...] = dz
        dzb_ref[...] = dz.astype(dzb_ref.dtype)

        @pl.when(i == pl.num_programs(0) - 1)
        def _():
            loss_ref[...] = jnp.sum(lacc[...], axis=-1, keepdims=True) * (0.5 / D)

    row = pl.BlockSpec((tr, D), lambda i: (i, 0))
    vec = pl.BlockSpec((1, D), lambda i: (0, 0))
    one = pl.BlockSpec((1, 1), lambda i: (0, 0))
    return pl.pallas_call(
        body, name=name, grid=(T // tr,), in_specs=[row, vec, vec, row], out_specs=[one, row, row, vec, vec],
        out_shape=[_S((1, 1), _F32), _S((T, D), _F32), _S((T, D), _MM), _S((1, D), _F32), _S((1, D), _F32)],
        scratch_shapes=[pltpu.VMEM((1, D), _F32)], compiler_params=_params(("arbitrary",)),
    )(z, g, b, target)


def _conv3(X, cw, cb):
    return cb + cw[2:3] * X + cw[1:2] * pltpu.roll(X, 1, 0) + cw[0:1] * pltpu.roll(X, 2, 0)


def _ffn_mid_fwd(h, cw, cb, name):
    _, T, F = h.shape
    tr = _tile(T, 256)
    tc = _tile(F, 512)
    nb = tr // CONV_HALO

    def body(h_ref, p_ref, cw_ref, cb_ref, o_ref):
        i = pl.program_id(0)
        hc = []
        for part in range(2):
            prev = jnp.where(i == 0, 0.0, p_ref[part])
            X = jnp.concatenate([prev, h_ref[part]], axis=0)
            hc.append(_conv3(X, cw_ref[part], cb_ref[part])[CONV_HALO:])
        a, v = hc
        o_ref[...] = (a * _sigmoid(a) * v).astype(o_ref.dtype)

    return pl.pallas_call(
        body, name=name, grid=(T // tr, F // tc),
        in_specs=[pl.BlockSpec((2, tr, tc), lambda i, j: (0, i, j)),
                  pl.BlockSpec((2, CONV_HALO, tc), lambda i, j: (0, jnp.maximum(i * nb - 1, 0), j)),
                  pl.BlockSpec((2, 3, tc), lambda i, j: (0, 0, j)),
                  pl.BlockSpec((2, 1, tc), lambda i, j: (0, 0, j))],
        out_specs=pl.BlockSpec((tr, tc), lambda i, j: (i, j)), out_shape=_S((T, F), _MM),
        compiler_params=_params(("parallel", "parallel")),
    )(h, h, cw, cb)


def _ffn_mid_bwd(h, dact, cw, cb, name):
    _, T, F = h.shape
    tr = _tile(T, 256)
    tc = _tile(F, 512)
    nb = tr // CONV_HALO
    last_blk = T // CONV_HALO - 1
    n = tr + 2 * CONV_HALO

    def body(h_ref, p_ref, n_ref, d_ref, dn_ref, cw_ref, cb_ref, dh_ref, dcw_ref, dcb_ref):
        i = pl.program_id(1)
        is_first = i == 0
        is_last = i == pl.num_programs(1) - 1

        @pl.when(is_first)
        def _():
            dcw_ref[...] = jnp.zeros_like(dcw_ref)
            dcb_ref[...] = jnp.zeros_like(dcb_ref)

        X, hc = [], []
        for part in range(2):
            prev = jnp.where(is_first, 0.0, p_ref[part])
            nxt = jnp.where(is_last, 0.0, n_ref[part])
            Xp = jnp.concatenate([prev, h_ref[part], nxt], axis=0)
            X.append(Xp)
            hc.append(_conv3(Xp, cw_ref[part], cb_ref[part]))
        D = jnp.concatenate([jnp.zeros((CONV_HALO, tc), _F32), d_ref[...], jnp.where(is_last, 0.0, dn_ref[...])], axis=0)
        a, v = hc
        sg = _sigmoid(a)
        dhc = [D * v * sg * (1.0 + a * (1.0 - sg)), D * a * sg]
        lo, hi = CONV_HALO, CONV_HALO + tr
        for part in range(2):
            g = dhc[part]
            cwp = cw_ref[part]
            dh = cwp[2:3] * g + cwp[1:2] * pltpu.roll(g, n - 1, 0) + cwp[0:1] * pltpu.roll(g, n - 2, 0)
            dh_ref[part] = dh[lo:hi].astype(dh_ref.dtype)
            gt = g[lo:hi]
            dcw_ref[part, 2:3, :] += jnp.sum(gt * X[part][lo:hi], axis=0, keepdims=True)
            dcw_ref[part, 1:2, :] += jnp.sum(gt * pltpu.roll(X[part], 1, 0)[lo:hi], axis=0, keepdims=True)
            dcw_ref[part, 0:1, :] += jnp.sum(gt * pltpu.roll(X[part], 2, 0)[lo:hi], axis=0, keepdims=True)
            dcb_ref[part] += jnp.sum(gt, axis=0, keepdims=True)

    return pl.pallas_call(
        body, name=name, grid=(F // tc, T // tr),
        in_specs=[pl.BlockSpec((2, tr, tc), lambda j, i: (0, i, j)),
                  pl.BlockSpec((2, CONV_HALO, tc), lambda j, i: (0, jnp.maximum(i * nb - 1, 0), j)),
                  pl.BlockSpec((2, CONV_HALO, tc), lambda j, i: (0, jnp.minimum((i + 1) * nb, last_blk), j)),
                  pl.BlockSpec((tr, tc), lambda j, i: (i, j)),
                  pl.BlockSpec((CONV_HALO, tc), lambda j, i: (jnp.minimum((i + 1) * nb, last_blk), j)),
                  pl.BlockSpec((2, 3, tc), lambda j, i: (0, 0, j)),
                  pl.BlockSpec((2, 1, tc), lambda j, i: (0, 0, j))],
        out_specs=[pl.BlockSpec((2, tr, tc), lambda j, i: (0, i, j)),
                   pl.BlockSpec((2, 3, tc), lambda j, i: (0, 0, j)),
                   pl.BlockSpec((2, 1, tc), lambda j, i: (0, 0, j))],
        out_shape=[_S((2, T, F), _MM), _S((2, 3, F), _F32), _S((2, 1, F), _F32)],
        compiler_params=_params(("parallel", "arbitrary")),
    )(h, h, h, dact, dact, cw, cb)


def _ev_common(h_ref, hp_ref, lng_ref, lnb_ref, ws_ref, bias_ref, i, tr, W):
    H = W // A_HEAD
    u, gu = _gelu_and_grad(h_ref[0])
    v, gv = _gelu_and_grad(h_ref[1])
    mu = jnp.mean(v, axis=-1, keepdims=True)
    vc = v - mu
    rstd = lax.rsqrt(jnp.mean(vc * vc, axis=-1, keepdims=True) + LN_EPS)
    vhat = vc * rstd
    vb = (vhat * lng_ref[...] + lnb_ref[...]).astype(_MM)
    s_chunks = []
    for c in range(tr // A_CHUNK):
        r0 = c * A_CHUNK
        heads = [_dot(ws_ref[hd], vb[r0:r0 + A_CHUNK, hd * A_HEAD:(hd + 1) * A_HEAD]) for hd in range(H)]
        s_chunks.append(jnp.concatenate(heads, axis=1) + bias_ref[...])
    prev = jnp.where(i == 0, 0.0, hp_ref[...])
    X = jnp.concatenate([prev, h_ref[2]], axis=0)
    return u, gu, gv, rstd, vhat, vb, s_chunks, X


def _pool_inv_count(i, tr, rows, win):
    pos = i * tr + _row_index(rows) + 1
    return 1.0 / jnp.minimum(pos, win).astype(_F32)


def _pool_fwd(X, g, Wg, i, tr):
    xg = X[:, g * Wg:(g + 1) * Wg]
    s = xg
    for k in range(g + 1):
        s = s + pltpu.roll(s, 2 ** k, 0)
    return s[POOL_HALO:] * _pool_inv_count(i, tr, tr, 2 ** (g + 1)) - xg[POOL_HALO:]


def _ev_mid_fwd(h, lng, lnb, ws, bias, wp, sc, name):
    _, T, W = h.shape
    tr = _tile(T, 256)
    H = W // A_HEAD
    Wg = W // B_GROUPS
    nb = tr // POOL_HALO

    def body(h_ref, hp_ref, lng_ref, lnb_ref, ws_ref, bias_ref, wp_ref, sc_ref, o_ref):
        i = pl.program_id(0)
        u, _, _, _, _, _, s_chunks, X = _ev_common(h_ref, hp_ref, lng_ref, lnb_ref, ws_ref, bias_ref, i, tr, W)
        for c, s in enumerate(s_chunks):
            r0 = c * A_CHUNK
            o_ref[r0:r0 + A_CHUNK, 0:W] = (u[r0:r0 + A_CHUNK] * s).astype(o_ref.dtype)
        for g in range(B_GROUPS):
            p = _pool_fwd(X, g, Wg, i, tr)
            y = _dot(p.astype(_MM), wp_ref[g]) * sc_ref[:, g * Wg:(g + 1) * Wg]
            o_ref[:, W + g * Wg:W + (g + 1) * Wg] = y.astype(o_ref.dtype)

    vec = pl.BlockSpec((1, W), lambda i: (0, 0))
    return pl.pallas_call(
        body, name=name, grid=(T // tr,),
        in_specs=[pl.BlockSpec((3, tr, W), lambda i: (0, i, 0)),
                  pl.BlockSpec((None, POOL_HALO, W), lambda i: (2, jnp.maximum(i * nb - 1, 0), 0)),
                  vec, vec,
                  pl.BlockSpec((H, A_CHUNK, A_CHUNK), lambda i: (0, 0, 0)),
                  pl.BlockSpec((A_CHUNK, W), lambda i: (0, 0)),
                  pl.BlockSpec((B_GROUPS, Wg, Wg), lambda i: (0, 0, 0)),
                  vec],
        out_specs=pl.BlockSpec((tr, 2 * W), lambda i: (i, 0)), out_shape=_S((T, 2 * W), _MM),
        compiler_params=_params(("parallel",)),
    )(h, h, lng, lnb, ws, bias, wp, sc)


def _ev_mid_bwd(h, dy, lng, lnb, ws, bias, wp, sc, name):
    _, T, W = h.shape
    tr = _tile(T, 256)
    H = W // A_HEAD
    Wg = W // B_GROUPS
    nb = tr // POOL_HALO
    last_blk = T // POOL_HALO - 1
    n = tr + POOL_HALO

    def body(h_ref, hp_ref, dy_ref, dyn_ref, lng_ref, lnb_ref, ws_ref, bias_ref, wp_ref, sc_ref,
             dh_ref, dws_ref, dbias_ref, dlng_ref, dlnb_ref, dwp_ref, dsc_ref):
        i = pl.program_id(0)

        @pl.when(i == 0)
        def _():
            for r in (dws_ref, dbias_ref, dlng_ref, dlnb_ref, dwp_ref, dsc_ref):
                r[...] = jnp.zeros_like(r)

        u, gu, gv, rstd, vhat, vb, s_chunks, X = _ev_common(h_ref, hp_ref, lng_ref, lnb_ref, ws_ref, bias_ref, i, tr, W)
        rr = lax.broadcasted_iota(jnp.int32, (A_CHUNK, A_CHUNK), 0)
        cc = lax.broadcasted_iota(jnp.int32, (A_CHUNK, A_CHUNK), 1)
        tril = rr >= cc
        du_chunks, dvln_chunks = [], []
        for c, s in enumerate(s_chunks):
            r0 = c * A_CHUNK
            dya = dy_ref[r0:r0 + A_CHUNK, 0:W]
            du_chunks.append(dya * s)
            ds = dya * u[r0:r0 + A_CHUNK]
            dbias_ref[...] += ds
            dsb = ds.astype(_MM)
            heads = []
            for hd in range(H):
                cols = slice(hd * A_HEAD, (hd + 1) * A_HEAD)
                dws_ref[hd] += jnp.where(tril, _dot(dsb[:, cols], vb[r0:r0 + A_CHUNK, cols], _NT), 0.0)
                heads.append(_dot(ws_ref[hd], dsb[:, cols], _TN))
            dvln_chunks.append(jnp.concatenate(heads, axis=1))
        du = jnp.concatenate(du_chunks, axis=0)
        dvln = jnp.concatenate(dvln_chunks, axis=0)
        dlng_ref[...] += jnp.sum(dvln * vhat, axis=0, keepdims=True)
        dlnb_ref[...] += jnp.sum(dvln, axis=0, keepdims=True)
        dxh = dvln * lng_ref[...]
        dv = rstd * (dxh - jnp.mean(dxh, axis=-1, keepdims=True) - vhat * jnp.mean(dxh * vhat, axis=-1, keepdims=True))
        dh_ref[0] = (du * gu).astype(dh_ref.dtype)
        dh_ref[1] = (dv * gv).astype(dh_ref.dtype)

        dyb = dy_ref[:, W:2 * W]
        dyb_full = jnp.concatenate([dyb, jnp.where(i == pl.num_programs(0) - 1, 0.0, dyn_ref[...])], axis=0)
        for g in range(B_GROUPS):
            cols = slice(g * Wg, (g + 1) * Wg)
            pb = _pool_fwd(X, g, Wg, i, tr).astype(_MM)
            ypre = _dot(pb, wp_ref[g])
            dsc_ref[:, cols] += jnp.sum(dyb[:, cols] * ypre, axis=0, keepdims=True)
            dyp = (dyb_full[:, cols] * sc_ref[:, cols]).astype(_MM)
            dwp_ref[g] += _dot(pb, dyp[0:tr], _TN)
            dp = _dot(dyp, wp_ref[g], _NT)
            s = dp * _pool_inv_count(i, tr, n, 2 ** (g + 1))
            for k in range(g + 1):
                s = s + pltpu.roll(s, n - 2 ** k, 0)
            dh_ref[2, :, cols] = (s[0:tr] - dp[0:tr]).astype(dh_ref.dtype)

    vec = pl.BlockSpec((1, W), lambda i: (0, 0))
    ws_spec = pl.BlockSpec((H, A_CHUNK, A_CHUNK), lambda i: (0, 0, 0))
    bias_spec = pl.BlockSpec((A_CHUNK, W), lambda i: (0, 0))
    wp_spec = pl.BlockSpec((B_GROUPS, Wg, Wg), lambda i: (0, 0, 0))
    return pl.pallas_call(
        body, name=name, grid=(T // tr,),
        in_specs=[pl.BlockSpec((3, tr, W), lambda i: (0, i, 0)),
                  pl.BlockSpec((None, POOL_HALO, W), lambda i: (2, jnp.maximum(i * nb - 1, 0), 0)),
                  pl.BlockSpec((tr, 2 * W), lambda i: (i, 0)),
                  pl.BlockSpec((POOL_HALO, W), lambda i: (jnp.minimum((i + 1) * nb, last_blk), 1)),
                  vec, vec, ws_spec, bias_spec, wp_spec, vec],
        out_specs=[pl.BlockSpec((3, tr, W), lambda i: (0, i, 0)), ws_spec, bias_spec, vec, vec, wp_spec, vec],
        out_shape=[_S((3, T, W), _MM), _S((H, A_CHUNK, A_CHUNK), _F32), _S((A_CHUNK, W), _F32), _S((1, W), _F32),
                   _S((1, W), _F32), _S((B_GROUPS, Wg, Wg), _F32), _S((1, W), _F32)],
        compiler_params=_params(("arbitrary",)),
    )(h, h, dy, dy, lng, lnb, ws, bias, wp, sc)


def _chunk_cumsum(x, rin):
    s = 1
    while s < C_CHUNK:
        x = x + jnp.where(rin >= s, pltpu.roll(x, s, 0), 0.0)
        s *= 2
    return x


def _chunk_revcumsum(x, rin):
    n = x.shape[0]
    s = 1
    while s < C_CHUNK:
        x = x + jnp.where(rin + s < C_CHUNK, pltpu.roll(x, n - s, 0), 0.0)
        s *= 2
    return x


def _hgrn_gates(q, fl, lb, tr, tc):
    nch = tr // C_CHUNK
    sq = _sigmoid(q)
    sf = _sigmoid(fl)
    f = lb + (1.0 - lb) * sf
    logf = jnp.log(f)
    rin = _row_index(tr) % C_CHUNK
    b = _chunk_cumsum(logf, rin)
    tot3 = jnp.sum(logf.reshape(nch, C_CHUNK, tc), axis=1, keepdims=True)
    eb = jnp.exp(b)
    enb = jnp.exp(-b)
    ekb = jnp.exp(tot3 - b.reshape(nch, C_CHUNK, tc)).reshape(tr, tc)
    return sq, sf, f, rin, tot3, eb, enb, ekb


def _hgrn_prep_fwd(h, lb, name):
    _, T, D = h.shape
    tr = _tile(T, 512)
    tc = _tile(D, 512)
    nch = tr // C_CHUNK

    def body(q_ref, f_ref, v_ref, lb_ref, qd_ref, kd_ref, ke_ref, vb_ref, dec_ref):
        q = q_ref[...]
        sq, _, f, _, tot3, eb, enb, ekb = _hgrn_gates(q, f_ref[...], lb_ref[...], tr, tc)
        kk = 1.0 - f
        qd_ref[...] = (q * sq * eb).astype(qd_ref.dtype)
        kd_ref[...] = (kk * enb).astype(kd_ref.dtype)
        ke_ref[...] = (kk * ekb).astype(ke_ref.dtype)
        vb_ref[...] = v_ref[...].astype(vb_ref.dtype)
        dec_ref[...] = jnp.exp(tot3).reshape(nch, tc)

    def part(p):
        return pl.BlockSpec((None, tr, tc), lambda i, j: (p, i, j))

    blk = pl.BlockSpec((tr, tc), lambda i, j: (i, j))
    return pl.pallas_call(
        body, name=name, grid=(T // tr, D // tc),
        in_specs=[part(0), part(1), part(2), pl.BlockSpec((1, tc), lambda i, j: (0, j))],
        out_specs=[blk, blk, blk, blk, pl.BlockSpec((nch, tc), lambda i, j: (i, j))],
        out_shape=[_S((T, D), _MM)] * 4 + [_S((T // C_CHUNK, D), _F32)],
        compiler_params=_params(("parallel", "parallel")),
    )(h, h, h, lb)


def _tril_mask():
    rr = lax.broadcasted_iota(jnp.int32, (C_CHUNK, C_CHUNK), 0)
    cc = lax.broadcasted_iota(jnp.int32, (C_CHUNK, C_CHUNK), 1)
    return rr >= cc


def _hgrn_scan_fwd(qd, kd, ke, vb, dec, h, ng, name):
    T, D = qd.shape
    NH = D // C_HEAD
    N = T // C_CHUNK

    def body(qd_ref, kd_ref, ke_ref, vb_ref, dec_ref, g_ref, ng_ref, o_ref, y_ref, st_ref):
        mask = _tril_mask()

        def step(n, St):
            r = pl.ds(pl.multiple_of(n * C_CHUNK, C_CHUNK), C_CHUNK)
            Qd, Kd, Ke, V = qd_ref[r, :], kd_ref[r, :], ke_ref[r, :], vb_ref[r, :]
            att = jnp.where(mask, _dot(Qd, Kd, _NT), 0.0).astype(_MM)
            o_ref[r, :] = _dot(att, V) + _dot(Qd, St.astype(_MM), _NT)
            st_ref[n] = St
            return St * dec_ref[pl.ds(n, 1), :] + _dot(V, Ke, _TN)

        lax.fori_loop(0, N, step, jnp.zeros((C_HEAD, C_HEAD), _F32))
        o = o_ref[...]
        r = lax.rsqrt(jnp.mean(o * o, axis=-1, keepdims=True) + LN_EPS)
        y_ref[...] = (o * r * ng_ref[...] * _sigmoid(g_ref[...])).astype(y_ref.dtype)

    col = pl.BlockSpec((T, C_HEAD), lambda j: (0, j))
    return pl.pallas_call(
        body, name=name, grid=(NH,),
        in_specs=[col, col, col, col, pl.BlockSpec((N, C_HEAD), lambda j: (0, j)),
                  pl.BlockSpec((None, T, C_HEAD), lambda j: (3, 0, j)), pl.BlockSpec((1, C_HEAD), lambda j: (0, j))],
        out_specs=[col, col, pl.BlockSpec((None, N, C_HEAD, C_HEAD), lambda j: (j, 0, 0, 0))],
        out_shape=[_S((T, D), _F32), _S((T, D), _MM), _S((NH, N, C_HEAD, C_HEAD), _F32)],
        compiler_params=_params(("parallel",)),
    )(qd, kd, ke, vb, dec, h, ng)


def _hgrn_scan_bwd(qd, kd, ke, vb, dec, st, o, h, ng, dy, name):
    T, D = qd.shape
    NH = D // C_HEAD
    N = T // C_CHUNK

    def body(qd_ref, kd_ref, ke_ref, vb_ref, dec_ref, st_ref, o_ref, g_ref, ng_ref, dy_ref,
             dqd_ref, dkd_ref, dke_ref, dv_ref, dgate_ref, ddec_ref, dng_ref, do_s):
        o = o_ref[...]
        r = lax.rsqrt(jnp.mean(o * o, axis=-1, keepdims=True) + LN_EPS)
        oh = o * r
        gn = ng_ref[...]
        sg = _sigmoid(g_ref[...])
        d = dy_ref[...]
        dyn = d * sg
        dgate_ref[...] = (d * oh * gn * sg * (1.0 - sg)).astype(dgate_ref.dtype)
        dng_ref[...] = jnp.sum(dyn * oh, axis=0, keepdims=True)
        doh = dyn * gn
        do_s[...] = (r * (doh - oh * jnp.mean(doh * oh, axis=-1, keepdims=True))).astype(do_s.dtype)
        mask = _tril_mask()

        def step(k, dSt):
            n = N - 1 - k
            rws = pl.ds(pl.multiple_of(n * C_CHUNK, C_CHUNK), C_CHUNK)
            Qd, Kd, Ke, V, dO = qd_ref[rws, :], kd_ref[rws, :], ke_ref[rws, :], vb_ref[rws, :], do_s[rws, :]
            St = st_ref[n]
            Stb = St.astype(_MM)
            dStb = dSt.astype(_MM)
            att = jnp.where(mask, _dot(Qd, Kd, _NT), 0.0).astype(_MM)
            dA = jnp.where(mask, _dot(dO, V, _NT), 0.0).astype(_MM)
            dv_ref[rws, :] = (_dot(att, dO, _TN) + _dot(Ke, dStb, _NT)).astype(dv_ref.dtype)
            dqd_ref[rws, :] = _dot(dA, Kd) + _dot(dO, Stb)
            dkd_ref[rws, :] = _dot(dA, Qd, _TN)
            dke_ref[rws, :] = _dot(V, dStb)
            ddec_ref[pl.ds(n, 1), :] = jnp.sum(dSt * St, axis=0, keepdims=True)
            return dSt * dec_ref[pl.ds(n, 1), :] + _dot(dO, Qd, _TN)

        lax.fori_loop(0, N, step, jnp.zeros((C_HEAD, C_HEAD), _F32))

    col = pl.BlockSpec((T, C_HEAD), lambda j: (0, j))
    chk = pl.BlockSpec((N, C_HEAD), lambda j: (0, j))
    one = pl.BlockSpec((1, C_HEAD), lambda j: (0, j))
    return pl.pallas_call(
        body, name=name, grid=(NH,),
        in_specs=[col, col, col, col, chk, pl.BlockSpec((None, N, C_HEAD, C_HEAD), lambda j: (j, 0, 0, 0)), col,
                  pl.BlockSpec((None, T, C_HEAD), lambda j: (3, 0, j)), one, col],
        out_specs=[col, col, col, col, col, chk, one],
        out_shape=[_S((T, D), _F32)] * 3 + [_S((T, D), _MM)] * 2 + [_S((N, D), _F32), _S((1, D), _F32)],
        scratch_shapes=[pltpu.VMEM((T, C_HEAD), _MM)],
        compiler_params=_params(("parallel",)),
    )(qd, kd, ke, vb, dec, st, o, h, ng, dy)


def _hgrn_prep_bwd(h, lb, dqd, dkd, dke, dv, dgate, ddec, name):
    _, T, D = h.shape
    tr = _tile(T, 512)
    tc = _tile(D, 256)
    nch = tr // C_CHUNK

    def body(q_ref, f_ref, lb_ref, dqd_ref, dkd_ref, dke_ref, dv_ref, dgate_ref, ddec_ref, dh_ref, dlb_ref):
        @pl.when(pl.program_id(1) == 0)
        def _():
            dlb_ref[...] = jnp.zeros_like(dlb_ref)

        q = q_ref[...]
        lb = lb_ref[...]
        sq, sf, f, rin, tot3, eb, enb, ekb = _hgrn_gates(q, f_ref[...], lb, tr, tc)
        kk = 1.0 - f
        dQd, dKd, dKe = dqd_ref[...], dkd_ref[...], dke_ref[...]
        tq = dQd * eb
        tkd = dKd * enb
        tke = dKe * ekb
        ke_term = tke * kk
        db = tq * (q * sq) - tkd * kk - ke_term
        dtot3 = (jnp.sum(ke_term.reshape(nch, C_CHUNK, tc), axis=1, keepdims=True)
                 + (ddec_ref[...] * jnp.exp(tot3).reshape(nch, tc)).reshape(nch, 1, tc))
        dlogf = (_chunk_revcumsum(db, rin).reshape(nch, C_CHUNK, tc) + dtot3).reshape(tr, tc)
        df = dlogf / f - (tkd + tke)
        dh_ref[0] = (tq * sq * (1.0 + q * (1.0 - sq))).astype(dh_ref.dtype)
        dh_ref[1] = (df * (1.0 - lb) * sf * (1.0 - sf)).astype(dh_ref.dtype)
        dh_ref[2] = dv_ref[...]
        dh_ref[3] = dgate_ref[...]
        dlb_ref[...] += jnp.sum(df * (1.0 - sf), axis=0, keepdims=True)

    def part(p):
        return pl.BlockSpec((None, tr, tc), lambda j, i: (p, i, j))

    blk = pl.BlockSpec((tr, tc), lambda j, i: (i, j))
    vec = pl.BlockSpec((1, tc), lambda j, i: (0, j))
    return pl.pallas_call(
        body, name=name, grid=(D // tc, T // tr),
        in_specs=[part(0), part(1), vec, blk, blk, blk, blk, blk, pl.BlockSpec((nch, tc), lambda j, i: (i, j))],
        out_specs=[pl.BlockSpec((4, tr, tc), lambda j, i: (0, i, j)), vec],
        out_shape=[_S((4, T, D), _MM), _S((1, D), _F32)],
        compiler_params=_params(("parallel", "arbitrary")),
    )(h, h, lb, dqd, dkd, dke, dv, dgate, ddec)


def _sum_slots(parts, name):
    P, R, C = parts.shape
    tr = _tile(R, 256) if R % 8 == 0 and R >= 256 else R

    def body(p_ref, o_ref):
        g = p_ref[0].astype(_F32)
        for s in range(1, P):
            g = g + p_ref[s].astype(_F32)
        o_ref[...] = g

    return pl.pallas_call(
        body, name=name, grid=(R // tr,), in_specs=[pl.BlockSpec((P, tr, C), lambda i: (0, i, 0))],
        out_specs=pl.BlockSpec((tr, C), lambda i: (i, 0)), out_shape=_S((R, C), _F32),
        compiler_params=_params(("parallel",)),
    )(parts)


def _adamw(parts, w, m, v, name):
    P, R, C = parts.shape
    tr = _tile(R, 128) if R % LANE == 0 else R

    def body(p_ref, w_ref, m_ref, v_ref, g_ref, d_ref, nm_ref, nv_ref):
        g = p_ref[0].astype(_F32)
        for s in range(1, P):
            g = g + p_ref[s].astype(_F32)
        nm = ADAM_B1 * m_ref[...] + (1.0 - ADAM_B1) * g
        nv = ADAM_B2 * v_ref[...] + (1.0 - ADAM_B2) * (g * g)
        m_hat = nm / (1.0 - ADAM_B1 ** ADAM_STEP)
        v_hat = nv / (1.0 - ADAM_B2 ** ADAM_STEP)
        g_ref[...] = g
        d_ref[...] = -ADAM_LR * (m_hat / (jnp.sqrt(v_hat) + ADAM_EPS) + ADAM_WD * w_ref[...])
        nm_ref[...] = nm
        nv_ref[...] = nv

    blk = pl.BlockSpec((tr, C), lambda i: (i, 0))
    return pl.pallas_call(
        body, name=name, grid=(R // tr,), in_specs=[pl.BlockSpec((P, tr, C), lambda i: (0, i, 0)), blk, blk, blk],
        out_specs=[blk] * 4, out_shape=[_S((R, C), _F32)] * 4, compiler_params=_params(("parallel",)),
    )(parts, w, m, v)


def _exchange(name, srcs, out_shapes, jobs, deps=()):
    ns, nj = len(srcs), len(jobs)

    nd = len(deps)

    def body(*refs):
        ins, outs = refs[:ns], refs[ns + nd:ns + nd + len(out_shapes)]
        send_sems, recv_sems, local_sems = refs[-3:]
        x, y, c = lax.axis_index("x"), lax.axis_index("y"), lax.axis_index("c")
        me = 4 * x + 2 * y + c
        local = []
        for ji, (si, src_fn, di, dst_fn) in enumerate(jobs):
            cp = pltpu.make_async_copy(src_fn(ins[si], me, me), dst_fn(outs[di], me), local_sems.at[ji])
            cp.start()
            local.append(cp)
        remote = []
        for k in range(1, N_DEV):
            px, py, pc = (x + (k >> 2)) % 2, (y + ((k >> 1) & 1)) % 2, (c + (k & 1)) % 2
            to = 4 * px + 2 * py + pc
            for ji, (si, src_fn, di, dst_fn) in enumerate(jobs):
                sem = (k - 1) * nj + ji
                cp = pltpu.make_async_remote_copy(
                    src_ref=src_fn(ins[si], me, to), dst_ref=dst_fn(outs[di], me),
                    send_sem=send_sems.at[sem], recv_sem=recv_sems.at[sem],
                    device_id=(px, py, pc), device_id_type=pl.DeviceIdType.MESH)
                cp.start()
                remote.append(cp)
        for cp in remote:
            cp.wait_recv()
        for cp in remote:
            cp.wait_send()
        for cp in local:
            cp.wait()

    hbm = pl.BlockSpec(memory_space=pltpu.HBM)
    return pl.pallas_call(
        body, name=name, in_specs=[hbm] * ns + [_ANY] * nd, out_specs=[hbm] * len(out_shapes), out_shape=list(out_shapes),
        scratch_shapes=[pltpu.SemaphoreType.DMA(((N_DEV - 1) * nj,)), pltpu.SemaphoreType.DMA(((N_DEV - 1) * nj,)),
                        pltpu.SemaphoreType.DMA((nj,))],
    )(*srcs, *deps)


def _whole(ref, me, to):
    return ref


def _slot_job(i, o):
    def dst(ref, me):
        return ref.at[me]
    return (i, _whole, o, dst)


_HBM = pl.BlockSpec(memory_space=pltpu.HBM)
_SEM = pl.BlockSpec(memory_space=pltpu.SEMAPHORE)
_ANY = pl.BlockSpec(memory_space=pl.ANY)
_N_PEER = N_DEV - 1


def _split_params():
    return pltpu.CompilerParams(has_side_effects=pltpu.SideEffectType.DATAFLOW_SIDE_EFFECTING)


def _blk(ref, axis, n, idx):
    return ref.at[tuple([slice(None)] * axis + [pl.ds(pl.multiple_of(idx * n, n), n)])]


def _peer(k):
    x, y, c = lax.axis_index("x"), lax.axis_index("y"), lax.axis_index("c")
    px, py, pc = (x + (k >> 2)) % 2, (y + ((k >> 1) & 1)) % 2, (c + (k & 1)) % 2
    return (px, py, pc), 4 * px + 2 * py + pc, 4 * x + 2 * y + c


def _row_tile(rows, pref):
    best = None
    for d in range(16, min(rows, pref) + 1, 16):
        if rows % d == 0:
            best = d
    return best if best is not None else rows


def _place(w, me1, axis, name, layer=None, deps=()):
    R, C = w.shape[-2:]
    tr = _row_tile(R, 512)
    nb = R // tr
    lead = () if layer is None else (None,)
    pre = () if layer is None else (layer,)

    def body(me_ref, w_ref, *rest):
        rest[-1][...] = w_ref[...].astype(rest[-1].dtype)

    if axis == 1:
        out_spec = pl.BlockSpec((tr, C), lambda i, me: (i, me[0]))
        out_shape = _S((R, N_DEV * C), _MM)
    else:
        out_spec = pl.BlockSpec((tr, C), lambda i, me: (me[0] * nb + i, 0))
        out_shape = _S((N_DEV * R, C), _MM)
    return pl.pallas_call(
        body, name=name,
        grid_spec=pltpu.PrefetchScalarGridSpec(
            num_scalar_prefetch=1, grid=(nb,),
            in_specs=[pl.BlockSpec(lead + (tr, C), lambda i, me: pre + (i, 0))] + [_ANY] * len(deps), out_specs=out_spec),
        out_shape=out_shape, compiler_params=_params(("parallel",)),
    )(me1, w, *deps)


def _gather_start(name, fulls, axes, ns):
    nt = len(fulls)

    def body(*refs):
        ins, send, recv, token = refs[:nt], refs[nt], refs[nt + 1], refs[-1]
        for t in range(nt):
            for k in range(1, N_DEV):
                dev, _, me = _peer(k)
                blk = _blk(ins[t], axes[t], ns[t], me)
                pltpu.make_async_remote_copy(
                    src_ref=blk, dst_ref=blk, send_sem=send.at[t * _N_PEER + k - 1], recv_sem=recv.at[t * _N_PEER + k - 1],
                    device_id=dev, device_id_type=pl.DeviceIdType.MESH).start()
        token[...] = jnp.zeros_like(token)

    res = pl.pallas_call(
        body, name=name,
        out_shape=(pltpu.SemaphoreType.DMA((nt * _N_PEER,)), pltpu.SemaphoreType.DMA((nt * _N_PEER,)))
        + tuple(pltpu.HBM(f.shape, f.dtype) for f in fulls) + (_S((8, LANE), _F32),),
        in_specs=(_HBM,) * nt, out_specs=(_SEM, _SEM) + (_HBM,) * nt + (pl.BlockSpec(memory_space=pltpu.VMEM),),
        input_output_aliases={t: 2 + t for t in range(nt)}, compiler_params=_split_params(),
    )(*[pltpu.with_memory_space_constraint(f, pltpu.HBM) for f in fulls])
    return res[0], res[1], list(res[2:2 + nt]), res[-1]


def _gather_wait(name, fulls, axes, ns, send, recv, after):
    nt = len(fulls)

    def body(*refs):
        ins, send_r, recv_r = refs[:nt], refs[nt], refs[nt + 1]
        for t in range(nt):
            for k in range(1, N_DEV):
                dev, _, me = _peer(k)
                blk = _blk(ins[t], axes[t], ns[t], me)
                cp = pltpu.make_async_remote_copy(
                    src_ref=blk, dst_ref=blk, send_sem=send_r.at[t * _N_PEER + k - 1],
                    recv_sem=recv_r.at[t * _N_PEER + k - 1], device_id=dev, device_id_type=pl.DeviceIdType.MESH)
                cp.wait_send()
                cp.wait_recv()

    res = pl.pallas_call(
        body, name=name, out_shape=tuple(pltpu.HBM(f.shape, f.dtype) for f in fulls),
        in_specs=(_HBM,) * nt + (_SEM, _SEM, _ANY), out_specs=(_HBM,) * nt,
        input_output_aliases={t: t for t in range(nt)}, compiler_params=_split_params(),
    )(*fulls, send, recv, after)
    return list(res)


def _scatter_start(name, dw, axis, n):
    shard = tuple(n if a == axis else d for a, d in enumerate(dw.shape))
    land = lax.empty((_N_PEER,) + shard, dw.dtype)

    def body(dw_ref, land_ref, send, recv, dw_out, land_out, token):
        for k in range(1, N_DEV):
            dev, to, _ = _peer(k)
            pltpu.make_async_remote_copy(
                src_ref=_blk(dw_ref, axis, n, to), dst_ref=land_ref.at[k - 1], send_sem=send.at[k - 1],
                recv_sem=recv.at[k - 1], device_id=dev, device_id_type=pl.DeviceIdType.MESH).start()
        token[...] = jnp.zeros_like(token)

    return pl.pallas_call(
        body, name=name,
        out_shape=(pltpu.SemaphoreType.DMA((_N_PEER,)), pltpu.SemaphoreType.DMA((_N_PEER,)),
                   pltpu.HBM(dw.shape, dw.dtype), pltpu.HBM(land.shape, land.dtype), _S((8, LANE), _F32)),
        in_specs=(_HBM, _HBM), out_specs=(_SEM, _SEM, _HBM, _HBM, pl.BlockSpec(memory_space=pltpu.VMEM)),
        input_output_aliases={0: 2, 1: 3}, compiler_params=_split_params(),
    )(pltpu.with_memory_space_constraint(dw, pltpu.HBM), pltpu.with_memory_space_constraint(land, pltpu.HBM))


def _scatter_wait(name, dw, land, send, recv, axis, n, after):
    def body(dw_ref, land_ref, send_r, recv_r, after_ref, dw_out, land_out):
        for k in range(1, N_DEV):
            dev, to, _ = _peer(k)
            cp = pltpu.make_async_remote_copy(
                src_ref=_blk(dw_ref, axis, n, to), dst_ref=land_ref.at[k - 1], send_sem=send_r.at[k - 1],
                recv_sem=recv_r.at[k - 1], device_id=dev, device_id_type=pl.DeviceIdType.MESH)
            cp.wait_send()
            cp.wait_recv()

    return pl.pallas_call(
        body, name=name, out_shape=(pltpu.HBM(dw.shape, dw.dtype), pltpu.HBM(land.shape, land.dtype)),
        in_specs=(_HBM, _HBM, _SEM, _SEM, _ANY), out_specs=(_HBM, _HBM), input_output_aliases={0: 0, 1: 1},
        compiler_params=_split_params(),
    )(dw, land, send, recv, after)


def _adamw_big(me1, dw, land, w, m, v, axis, n, name, layer=None, into=None):
    R, C = land.shape[1:]
    tr = _row_tile(R, 128)
    nb = R // tr
    lead = () if layer is None else (None,)
    pre = () if layer is None else (layer,)

    def body(me_ref, own_ref, land_ref, w_ref, m_ref, v_ref, *rest):
        g_ref, d_ref, nm_ref, nv_ref = rest[-4:]
        g = own_ref[...].astype(_F32)
        for s in range(_N_PEER):
            g = g + land_ref[s].astype(_F32)
        nm = ADAM_B1 * m_ref[...] + (1.0 - ADAM_B1) * g
        nv = ADAM_B2 * v_ref[...] + (1.0 - ADAM_B2) * (g * g)
        m_hat = nm / (1.0 - ADAM_B1 ** ADAM_STEP)
        v_hat = nv / (1.0 - ADAM_B2 ** ADAM_STEP)
        g_ref[...] = g
        d_ref[...] = -ADAM_LR * (m_hat / (jnp.sqrt(v_hat) + ADAM_EPS) + ADAM_WD * w_ref[...])
        nm_ref[...] = nm
        nv_ref[...] = nv

    if axis == 1:
        own_spec = pl.BlockSpec((tr, C), lambda i, me: (i, me[0]))
    else:
        own_spec = pl.BlockSpec((tr, C), lambda i, me: (me[0] * nb + i, 0))
    blk = pl.BlockSpec(lead + (tr, C), lambda i, me: pre + (i, 0))
    in_specs = [own_spec, pl.BlockSpec((_N_PEER, tr, C), lambda i, me: (0, i, 0)), blk, blk, blk]
    args = [me1, dw, land, w, m, v]
    aliases = {}
    if into is not None:
        in_specs += [_ANY] * 4
        aliases = {6 + j: j for j in range(4)}
        args += list(into)
    return pl.pallas_call(
        body, name=name,
        grid_spec=pltpu.PrefetchScalarGridSpec(num_scalar_prefetch=1, grid=(nb,), in_specs=in_specs, out_specs=[blk] * 4),
        out_shape=[_S(w.shape, _F32)] * 4, input_output_aliases=aliases, compiler_params=_params(("parallel",)),
    )(*args)


def _pack(arrs):
    flat = jnp.concatenate([a.reshape(-1).astype(_F32) for a in arrs])
    pad = (-flat.shape[0]) % (LANE * LANE)
    return jnp.pad(flat, (0, pad)).reshape(-1, LANE)


def _unpack(mat, shapes):
    flat = mat.reshape(-1)
    out, off = [], 0
    for s in shapes:
        n = 1
        for d in s:
            n *= d
        out.append(flat[off:off + n].reshape(s))
        off += n
    return out


def _lb_of(lb_param):
    lb_all = jnp.cumsum(jax.nn.softmax(lb_param.astype(_F32), axis=0), axis=0)
    return (lb_all - lb_all[0])[1:2]


def kernel(x, ev_w_in, ev_ln_v_g, ev_ln_v_b, ev_w_s, ev_b_s, ev_w_pool, ev_pool_scale, ev_w_out, od_w_in, od_norm_g, od_w_out, lb_param, ffn_w_up, ffn_conv_w, ffn_conv_b, ffn_w_down, ln1_g, ln1_b, ln2_g, ln2_b, loss_target, m_ev_w_in, m_ev_ln_v_g, m_ev_ln_v_b, m_ev_w_s, m_ev_b_s, m_ev_w_pool, m_ev_pool_scale, m_ev_w_out, m_od_w_in, m_od_norm_g, m_od_w_out, m_lb_param, m_ffn_w_up, m_ffn_conv_w, m_ffn_conv_b, m_ffn_w_down, m_ln1_g, m_ln1_b, m_ln2_g, m_ln2_b, v_ev_w_in, v_ev_ln_v_g, v_ev_ln_v_b, v_ev_w_s, v_ev_b_s, v_ev_w_pool, v_ev_pool_scale, v_ev_w_out, v_od_w_in, v_od_norm_g, v_od_w_out, v_lb_param, v_ffn_w_up, v_ffn_conv_w, v_ffn_conv_b, v_ffn_w_down, v_ln1_g, v_ln1_b, v_ln2_g, v_ln2_b):
    me = 4 * lax.axis_index("x") + 2 * lax.axis_index("y") + lax.axis_index("c")
    T, D = x.shape[1], x.shape[2]
    W = ev_ln_v_g.shape[1]
    H = W // A_HEAD
    Wg = W // B_GROUPS
    F2 = ffn_conv_b.shape[1]
    F = F2 // 2
    n_in0, n_out0 = ev_w_in.shape[2], ev_w_out.shape[1]
    n_in1, n_out1 = od_w_in.shape[2], od_w_out.shape[1]
    n_up, n_dn = ffn_w_up.shape[2], ffn_w_down.shape[1]
    n_pool, n_ng, n_cw = ev_w_pool.shape[2], od_norm_g.shape[1], ffn_conv_w.shape[2]

    me1 = me.astype(jnp.int32).reshape(1)
    groups = [
        ("g0", ((ev_w_in[0], None, "w_in0"), (ev_w_out[0], None, "w_out0")), (n_in0, n_out0)),
        ("g1", ((ffn_w_up, 0, "w_up0"), (ffn_w_down, 0, "w_dn0")), (n_up, n_dn)),
        ("g2", ((od_w_in[0], None, "w_in1"), (od_w_out[0], None, "w_out1")), (n_in1, n_out1)),
        ("g3", ((ffn_w_up, 1, "w_up1"), (ffn_w_down, 1, "w_dn1")), (n_up, n_dn)),
    ]
    started, tokens = {}, []
    for gname, shards, ns in groups:
        axes = (1, 0)
        fulls = [_place(w, me1, ax, "place_" + nm, layer, deps=tokens) for (w, layer, nm), ax in zip(shards, axes)]
        send, recv, fulls, token = _gather_start("gather_start_" + gname, fulls, axes, ns)
        started[gname] = (fulls, axes, ns, send, recv)
        tokens = [token]

    def gathered(gname, after):
        fulls, axes, ns, send, recv = started[gname]
        return _gather_wait("gather_wait_" + gname, fulls, axes, ns, send, recv, after)

    small_shards = [od_norm_g, ffn_conv_w, ev_w_pool]
    small_pack = _pack(small_shards)
    small_all = _exchange("gather_small_params", [small_pack], [_S((N_DEV,) + small_pack.shape, _F32)], [_slot_job(0, 0)])[0]
    ng_parts, cw_parts, wp_parts = [], [], []
    for j in range(N_DEV):
        a, b, c = _unpack(small_all[j], [s.shape for s in small_shards])
        ng_parts.append(a)
        cw_parts.append(b)
        wp_parts.append(c)
    norm_g = jnp.concatenate(ng_parts, axis=1)
    conv_w = jnp.concatenate(cw_parts, axis=2)
    w_pool = jnp.concatenate(wp_parts, axis=2)[0]
    cw_l = [conv_w[l].reshape(3, 2, F).transpose(1, 0, 2) for l in range(DEPTH)]
    cb_l = [ffn_conv_b[l].reshape(2, 1, F) for l in range(DEPTH)]
    ws_tril = jnp.tril(ev_w_s[0]).astype(_MM)
    bias = jnp.repeat(ev_b_s[0].T, A_HEAD, axis=1)
    wp_b = w_pool.astype(_MM)
    lb, lb_vjp = jax.vjp(_lb_of, lb_param)

    x2 = x[0]
    xb = _cast(x2, _MM, "cast_x", deps=tokens)
    w_in0, w_out0 = gathered("g0", xb)
    h0 = _mm(xb, w_in0, "nn", _F32, "ev_in", out_parts=3)
    yab = _ev_mid_fwd(h0, ev_ln_v_g, ev_ln_v_b, ws_tril, bias, wp_b, ev_pool_scale, "ev_mid_fwd")
    z1 = _mm(yab, w_out0, "nn", _F32, "ev_out", add=x2, add_scale=ALPHA)
    x1, x1b = _ln_fwd(z1, ln1_g[0:1], ln1_b[0:1], "ln1_0")
    w_up0, w_dn0 = gathered("g1", x1b)
    hf0 = _mm(x1b, w_up0, "nn", _F32, "ffn_up", out_parts=2)
    act0 = _ffn_mid_fwd(hf0, cw_l[0], cb_l[0], "ffn_mid_fwd")
    z2 = _mm(act0, w_dn0, "nn", _F32, "ffn_down", add=x1, add_scale=ALPHA)
    x2_, x2b = _ln_fwd(z2, ln2_g[0:1], ln2_b[0:1], "ln2_0")
    w_in1, w_out1 = gathered("g2", x2b)
    h1 = _mm(x2b, w_in1, "nn", _F32, "od_in", out_parts=4)
    qd, kd, ke, vb, dec = _hgrn_prep_fwd(h1, lb, "hgrn_prep_fwd")
    o, yo, st = _hgrn_scan_fwd(qd, kd, ke, vb, dec, h1, norm_g, "hgrn_scan_fwd")
    z3 = _mm(yo, w_out1, "nn", _F32, "od_out", add=x2_, add_scale=ALPHA)
    x3, x3b = _ln_fwd(z3, ln1_g[1:2], ln1_b[1:2], "ln1_1")
    w_up1, w_dn1 = gathered("g3", x3b)
    hf1 = _mm(x3b, w_up1, "nn", _F32, "ffn_up", out_parts=2)
    act1 = _ffn_mid_fwd(hf1, cw_l[1], cb_l[1], "ffn_mid_fwd")
    z4 = _mm(act1, w_dn1, "nn", _F32, "ffn_down", add=x3, add_scale=ALPHA)

    scat = {}

    def scatter(key, dw, axis, n):
        send, recv, dw, land, token = _scatter_start("scatter_start_" + key, dw, axis, n)
        scat[key] = (dw, land, send, recv, axis, n)
        return [token]

    loss11, dz4, dz4b, g_ln2_1, b_ln2_1 = _ln_loss_bwd(z4, ln2_g[1:2], ln2_b[1:2], loss_target[0], "ln_loss_bwd")
    tok = scatter("dn1", _mm(act1, dz4b, "tn", _XCH, "ffn_down_dw"), 0, n_dn)
    dact1 = _mm(dz4b, w_dn1, "nt", _F32, "ffn_down_dx", deps=tok)
    dhf1, dcw1, dcb1 = _ffn_mid_bwd(hf1, dact1, cw_l[1], cb_l[1], "ffn_mid_bwd")
    tok = scatter("up1", _mm(x3b, dhf1, "tn", _XCH, "ffn_up_dw", b_parts=2, deps=tok), 1, n_up)
    dx3 = _mm(dhf1, w_up1, "nt", _F32, "ffn_up_dx", a_parts=2, add=dz4, add_scale=ALPHA, deps=tok)
    dz3, dz3b, g_ln1_1, b_ln1_1 = _ln_bwd(z3, ln1_g[1:2], dx3, "ln_bwd")
    tok = scatter("out1", _mm(yo, dz3b, "tn", _XCH, "od_out_dw", deps=tok), 0, n_out1)
    dyo = _mm(dz3b, w_out1, "nt", _F32, "od_out_dx", deps=tok)
    dqd, dkd, dke, dv, dgate, ddec, dng = _hgrn_scan_bwd(qd, kd, ke, vb, dec, st, o, h1, norm_g, dyo, "hgrn_scan_bwd")
    dh1, dlb = _hgrn_prep_bwd(h1, lb, dqd, dkd, dke, dv, dgate, ddec, "hgrn_prep_bwd")
    tok = scatter("in1", _mm(x2b, dh1, "tn", _XCH, "od_in_dw", b_parts=4, deps=tok), 1, n_in1)
    dx2 = _mm(dh1, w_in1, "nt", _F32, "od_in_dx", a_parts=4, add=dz3, add_scale=ALPHA, deps=tok)
    dz2, dz2b, g_ln2_0, b_ln2_0 = _ln_bwd(z2, ln2_g[0:1], dx2, "ln_bwd")
    tok = scatter("dn0", _mm(act0, dz2b, "tn", _XCH, "ffn_down_dw", deps=tok), 0, n_dn)
    dact0 = _mm(dz2b, w_dn0, "nt", _F32, "ffn_down_dx", deps=tok)
    dhf0, dcw0, dcb0 = _ffn_mid_bwd(hf0, dact0, cw_l[0], cb_l[0], "ffn_mid_bwd")
    tok = scatter("up0", _mm(x1b, dhf0, "tn", _XCH, "ffn_up_dw", b_parts=2, deps=tok), 1, n_up)
    dx1 = _mm(dhf0, w_up0, "nt", _F32, "ffn_up_dx", a_parts=2, add=dz2, add_scale=ALPHA, deps=tok)
    dz1, dz1b, g_ln1_0, b_ln1_0 = _ln_bwd(z1, ln1_g[0:1], dx1, "ln_bwd")
    tok = scatter("out0", _mm(yab, dz1b, "tn", _XCH, "ev_out_dw", deps=tok), 0, n_out0)
    dyab = _mm(dz1b, w_out0, "nt", _F32, "ev_out_dx", deps=tok)
    dh0, dws, dbias, dlng, dlnb, dwp, dsc = _ev_mid_bwd(h0, dyab, ev_ln_v_g, ev_ln_v_b, ws_tril, bias, wp_b,
                                                        ev_pool_scale, "ev_mid_bwd")
    tok = scatter("in0", _mm(xb, dh0, "tn", _XCH, "ev_in_dw", b_parts=3, deps=tok), 1, n_in0)
    grad_x = _mm(dh0, w_in0, "nt", _F32, "ev_in_dx", a_parts=3, add=dz1, add_scale=ALPHA, deps=tok)

    g_b_s = dbias.reshape(A_CHUNK, H, A_HEAD).sum(axis=-1).T[None]
    g_conv_w = jnp.stack([d.transpose(1, 0, 2).reshape(3, F2) for d in (dcw0, dcw1)])
    g_conv_b = jnp.stack([d.reshape(F2) for d in (dcb0, dcb1)])
    small_names = ["ev_ln_v_g", "ev_ln_v_b", "ev_w_s", "ev_b_s", "ev_w_pool", "ev_pool_scale", "od_norm_g", "lb_param",
                   "ffn_conv_w", "ffn_conv_b", "ln1_g", "ln1_b", "ln2_g", "ln2_b"]
    small_grads = [dlng, dlnb, dws[None], g_b_s, dwp[None], dsc, dng, lb_vjp(dlb)[0], g_conv_w, g_conv_b,
                   jnp.concatenate([g_ln1_0, g_ln1_1]), jnp.concatenate([b_ln1_0, b_ln1_1]),
                   jnp.concatenate([g_ln2_0, g_ln2_1]), jnp.concatenate([b_ln2_0, b_ln2_1])]
    full_shapes = [g.shape for g in small_grads]
    gpack = _pack(small_grads)
    gall = _exchange("gather_small_grads", [gpack], [_S((N_DEV,) + gpack.shape, _F32)], [_slot_job(0, 0)], deps=[grad_x])[0]
    gsum_mat = _sum_slots(gall, "sum_small_grads")
    gsum = _unpack(gsum_mat, full_shapes)

    def landed(key, after):
        dw, land, send, recv, axis, n = scat[key]
        dw, land = _scatter_wait("scatter_wait_" + key, dw, land, send, recv, axis, n, after)
        return me1, dw, land

    big = {}
    r_dn = _adamw_big(*landed("dn1", gsum_mat), ffn_w_down, m_ffn_w_down, v_ffn_w_down, 0, n_dn, "adamw_w_dn1", layer=1)
    r_up = _adamw_big(*landed("up1", r_dn[0]), ffn_w_up, m_ffn_w_up, v_ffn_w_up, 1, n_up, "adamw_w_up1", layer=1)
    big["od_w_out"] = _adamw_big(*landed("out1", r_up[0]), od_w_out[0], m_od_w_out[0], v_od_w_out[0], 0, n_out1, "adamw_w_out1")
    big["od_w_in"] = _adamw_big(*landed("in1", big["od_w_out"][0]), od_w_in[0], m_od_w_in[0], v_od_w_in[0], 1, n_in1, "adamw_w_in1")
    big["ffn_w_down"] = _adamw_big(*landed("dn0", big["od_w_in"][0]), ffn_w_down, m_ffn_w_down, v_ffn_w_down, 0, n_dn,
                                   "adamw_w_dn0", layer=0, into=r_dn)
    big["ffn_w_up"] = _adamw_big(*landed("up0", big["ffn_w_down"][0]), ffn_w_up, m_ffn_w_up, v_ffn_w_up, 1, n_up,
                                 "adamw_w_up0", layer=0, into=r_up)
    big["ev_w_out"] = _adamw_big(*landed("out0", big["ffn_w_up"][0]), ev_w_out[0], m_ev_w_out[0], v_ev_w_out[0], 0, n_out0, "adamw_w_out0")
    big["ev_w_in"] = _adamw_big(*landed("in0", big["ev_w_out"][0]), ev_w_in[0], m_ev_w_in[0], v_ev_w_in[0], 1, n_in0, "adamw_w_in0")

    given = dict(ev_ln_v_g=(ev_ln_v_g, m_ev_ln_v_g, v_ev_ln_v_g), ev_ln_v_b=(ev_ln_v_b, m_ev_ln_v_b, v_ev_ln_v_b),
                 ev_w_s=(ev_w_s, m_ev_w_s, v_ev_w_s), ev_b_s=(ev_b_s, m_ev_b_s, v_ev_b_s),
                 ev_w_pool=(ev_w_pool, m_ev_w_pool, v_ev_w_pool),
                 ev_pool_scale=(ev_pool_scale, m_ev_pool_scale, v_ev_pool_scale),
                 od_norm_g=(od_norm_g, m_od_norm_g, v_od_norm_g), lb_param=(lb_param, m_lb_param, v_lb_param),
                 ffn_conv_w=(ffn_conv_w, m_ffn_conv_w, v_ffn_conv_w), ffn_conv_b=(ffn_conv_b, m_ffn_conv_b, v_ffn_conv_b),
                 ln1_g=(ln1_g, m_ln1_g, v_ln1_g), ln1_b=(ln1_b, m_ln1_b, v_ln1_b), ln2_g=(ln2_g, m_ln2_g, v_ln2_g),
                 ln2_b=(ln2_b, m_ln2_b, v_ln2_b))
    shard_axis = dict(ev_w_pool=(2, n_pool), od_norm_g=(1, n_ng), ffn_conv_w=(2, n_cw))
    local_g = []
    for name, g in zip(small_names, gsum):
        if name in shard_axis:
            ax, n = shard_axis[name]
            g = lax.dynamic_slice_in_dim(g, me * n, n, axis=ax)
        local_g.append(g)
    local_shapes = [g.shape for g in local_g]
    res = _adamw(_pack(local_g)[None], _pack([given[n][0] for n in small_names]), _pack([given[n][1] for n in small_names]),
                 _pack([given[n][2] for n in small_names]), "adamw_small")
    small = {n: [] for n in small_names}
    for r in res:
        for n, a in zip(small_names, _unpack(r, local_shapes)):
            small[n].append(a)

    loss = lax.psum(loss11[0, 0], ("x", "y", "c"))
    order = ["ev_w_in", "ev_ln_v_g", "ev_ln_v_b", "ev_w_s", "ev_b_s", "ev_w_pool", "ev_pool_scale", "ev_w_out", "od_w_in",
             "od_norm_g", "od_w_out", "lb_param", "ffn_w_up", "ffn_conv_w", "ffn_conv_b", "ffn_w_down", "ln1_g", "ln1_b",
             "ln2_g", "ln2_b"]
    shapes = dict(ev_w_in=ev_w_in.shape, ev_w_out=ev_w_out.shape, od_w_in=od_w_in.shape, od_w_out=od_w_out.shape,
                  ffn_w_up=ffn_w_up.shape, ffn_w_down=ffn_w_down.shape)
    outs = [loss, grad_x[None]]
    for kind in range(4):
        for n in order:
            outs.append(big[n][kind].reshape(shapes[n]) if n in big else small[n][kind])
    return tuple(outs)
```

```python
import functools

import jax
import jax.numpy as jnp
from jax import lax
from jax.experimental import pallas as pl
from jax.experimental.pallas import tpu as pltpu

_MM = jnp.bfloat16
_XCH = jnp.bfloat16

DEPTH = 2
ALPHA = (2 * DEPTH) ** 0.25
LN_EPS = 1e-5
A_CHUNK = 128
A_HEAD = 128
B_GROUPS = 4
POOL_HALO = 16
C_CHUNK = 64
C_HEAD = 128
CONV_HALO = 8
ADAM_LR, ADAM_B1, ADAM_B2, ADAM_EPS, ADAM_WD, ADAM_STEP = 0.001, 0.9, 0.999, 1e-08, 0.01, 10
N_DEV = 8
LANE = 128
VMEM_LIMIT = 56 * 1024 * 1024
MM_FULL_K = 2048
MM_FULL_K_TN = 4096
MM_DEEP_K = 2816

_F32 = jnp.float32
_NN = (((1,), (0,)), ((), ()))
_NT = (((1,), (1,)), ((), ()))
_TN = (((0,), (0,)), ((), ()))
_S = jax.ShapeDtypeStruct


def _dot(a, b, dims=_NN):
    return lax.dot_general(a, b, dims, preferred_element_type=_F32)


def _tile(dim, pref):
    best = None
    d = LANE
    while d <= min(dim, pref):
        if dim % d == 0:
            best = d
        d += LANE
    return best if best is not None else dim


def _params(sem):
    return pltpu.CompilerParams(dimension_semantics=sem, vmem_limit_bytes=VMEM_LIMIT)


def _sigmoid(x):
    return 1.0 / (1.0 + jnp.exp(-x))


_GELU_C = 0.7978845608028654
_GELU_A = 0.044715


def _gelu_and_grad(x):
    t = jnp.tanh(_GELU_C * (x + _GELU_A * x * x * x))
    y = 0.5 * x * (1.0 + t)
    dy = 0.5 * (1.0 + t) + 0.5 * x * (1.0 - t * t) * _GELU_C * (1.0 + 3.0 * _GELU_A * x * x)
    return y, dy


def _row_index(n):
    return lax.broadcasted_iota(jnp.int32, (n, 1), 0)


def _mm_tiles(mode, M, N, K, with_add):
    if mode == "tn":
        return _tile(M, 1024), _tile(N, 1024), _tile(K, MM_FULL_K_TN)
    if K <= MM_FULL_K:
        return _tile(M, 1024 if with_add else 2048), _tile(N, 1024 if mode == "nn" else 512), K
    return _tile(M, 1024), _tile(N, 1024), _tile(K, MM_DEEP_K)


def _mm(a, b, mode, out_dtype, name, *, a_parts=1, b_parts=1, out_parts=1, add=None, add_scale=1.0, deps=(), tiles=None):
    if mode == "nn":
        M, K = a.shape
        N = b.shape[1]
    elif mode == "nt":
        if a_parts > 1:
            M, K = a.shape[1], a.shape[2] * a_parts
        else:
            M, K = a.shape
        N = b.shape[0]
    else:
        K, M = a.shape
        N = b.shape[-1] * b_parts
    tm, tn, tk = tiles if tiles is not None else _mm_tiles(mode, M, N // max(b_parts, out_parts), K // a_parts, add is not None)
    nk = K // tk
    npj = (N // max(b_parts, out_parts)) // tn
    nkp = (K // a_parts) // tk
    if mode == "nn":
        a_spec = pl.BlockSpec((tm, tk), lambda i, j, k: (i, k))
        b_spec = pl.BlockSpec((tk, tn), lambda i, j, k: (k, j))
        dims = _NN
    elif mode == "nt":
        if a_parts > 1:
            a_spec = pl.BlockSpec((None, tm, tk), lambda i, j, k: (k // nkp, i, k % nkp))
        else:
            a_spec = pl.BlockSpec((tm, tk), lambda i, j, k: (i, k))
        b_spec = pl.BlockSpec((tn, tk), lambda i, j, k: (j, k))
        dims = _NT
    else:
        a_spec = pl.BlockSpec((tk, tm), lambda i, j, k: (k, i))
        if b_parts > 1:
            b_spec = pl.BlockSpec((None, tk, tn), lambda i, j, k: (j // npj, k, j % npj))
        else:
            b_spec = pl.BlockSpec((tk, tn), lambda i, j, k: (k, j))
        dims = _TN
    if out_parts > 1:
        out_spec = pl.BlockSpec((None, tm, tn), lambda i, j, k: (j // npj, i, j % npj))
        out_shape = _S((out_parts, M, N // out_parts), out_dtype)
    else:
        out_spec = pl.BlockSpec((tm, tn), lambda i, j, k: (i, j))
        out_shape = _S((M, N), out_dtype)
    in_specs = [a_spec, b_spec]
    args = [a, b]
    if add is not None:
        in_specs.append(pl.BlockSpec((tm, tn), lambda i, j, k: (i, j)))
        args.append(add)
    in_specs += [_ANY] * len(deps)
    args += list(deps)

    def finish(r, refs, o_ref):
        if add is not None:
            r = r + add_scale * refs[2][...]
        o_ref[...] = r.astype(o_ref.dtype)

    def body_one(*refs):
        finish(_dot(refs[0][...], refs[1][...], dims), refs, refs[-1])

    def body_acc(*refs):
        o_ref, acc = refs[-2], refs[-1]
        k = pl.program_id(2)

        @pl.when(k == 0)
        def _():
            acc[...] = jnp.zeros_like(acc)

        acc[...] += _dot(refs[0][...], refs[1][...], dims)

        @pl.when(k == nk - 1)
        def _():
            finish(acc[...], refs, o_ref)

    return pl.pallas_call(
        body_one if nk == 1 else body_acc, name=name, grid=(M // tm, N // tn, nk), in_specs=in_specs,
        out_specs=out_spec, out_shape=out_shape,
        scratch_shapes=[] if nk == 1 else [pltpu.VMEM((tm, tn), _F32)],
        compiler_params=_params(("parallel", "parallel", "arbitrary")),
    )(*args)


def _cast(x2d, dtype, name, deps=()):
    R, C = x2d.shape
    tr = _tile(R, 512) if R % LANE == 0 else R

    def body(x_ref, *rest):
        rest[-1][...] = x_ref[...].astype(rest[-1].dtype)

    return pl.pallas_call(
        body, name=name, grid=(R // tr,), in_specs=[pl.BlockSpec((tr, C), lambda i: (i, 0))] + [_ANY] * len(deps),
        out_specs=pl.BlockSpec((tr, C), lambda i: (i, 0)), out_shape=_S((R, C), dtype),
        compiler_params=_params(("parallel",)),
    )(x2d, *deps)


def _ln_fwd(z, g, b, name):
    T, D = z.shape
    tr = _tile(T, 256)

    def body(z_ref, g_ref, b_ref, y_ref, yb_ref):
        zz = z_ref[...]
        mu = jnp.mean(zz, axis=-1, keepdims=True)
        zc = zz - mu
        var = jnp.mean(zc * zc, axis=-1, keepdims=True)
        y = zc * lax.rsqrt(var + LN_EPS) * g_ref[...] + b_ref[...]
        y_ref[...] = y
        yb_ref[...] = y.astype(yb_ref.dtype)

    row = pl.BlockSpec((tr, D), lambda i: (i, 0))
    vec = pl.BlockSpec((1, D), lambda i: (0, 0))
    return pl.pallas_call(
        body, name=name, grid=(T // tr,), in_specs=[row, vec, vec], out_specs=[row, row],
        out_shape=[_S((T, D), _F32), _S((T, D), _MM)], compiler_params=_params(("parallel",)),
    )(z, g, b)


def _ln_bwd(z, g, dy, name):
    T, D = z.shape
    tr = _tile(T, 256)

    def body(z_ref, g_ref, dy_ref, dz_ref, dzb_ref, dg_ref, db_ref):
        @pl.when(pl.program_id(0) == 0)
        def _():
            dg_ref[...] = jnp.zeros_like(dg_ref)
            db_ref[...] = jnp.zeros_like(db_ref)

        zz = z_ref[...]
        mu = jnp.mean(zz, axis=-1, keepdims=True)
        zc = zz - mu
        rstd = lax.rsqrt(jnp.mean(zc * zc, axis=-1, keepdims=True) + LN_EPS)
        xh = zc * rstd
        d = dy_ref[...]
        dg_ref[...] += jnp.sum(d * xh, axis=0, keepdims=True)
        db_ref[...] += jnp.sum(d, axis=0, keepdims=True)
        dxh = d * g_ref[...]
        dz = rstd * (dxh - jnp.mean(dxh, axis=-1, keepdims=True) - xh * jnp.mean(dxh * xh, axis=-1, keepdims=True))
        dz_ref[...] = dz
        dzb_ref[...] = dz.astype(dzb_ref.dtype)

    row = pl.BlockSpec((tr, D), lambda i: (i, 0))
    vec = pl.BlockSpec((1, D), lambda i: (0, 0))
    return pl.pallas_call(
        body, name=name, grid=(T // tr,), in_specs=[row, vec, row], out_specs=[row, row, vec, vec],
        out_shape=[_S((T, D), _F32), _S((T, D), _MM), _S((1, D), _F32), _S((1, D), _F32)],
        compiler_params=_params(("arbitrary",)),
    )(z, g, dy)


def _ln_loss_bwd(z, g, b, target, name):
    T, D = z.shape
    tr = _tile(T, 256)

    def body(z_ref, g_ref, b_ref, t_ref, loss_ref, dz_ref, dzb_ref, dg_ref, db_ref, lacc):
        i = pl.program_id(0)

        @pl.when(i == 0)
        def _():
            dg_ref[...] = jnp.zeros_like(dg_ref)
            db_ref[...] = jnp.zeros_like(db_ref)
            lacc[...] = jnp.zeros_like(lacc)

        zz = z_ref[...]
        mu = jnp.mean(zz, axis=-1, keepdims=True)
        zc = zz - mu
        rstd = lax.rsqrt(jnp.mean(zc * zc, axis=-1, keepdims=True) + LN_EPS)
        xh = zc * rstd
        err = xh * g_ref[...] + b_ref[...] - t_ref[...]
        lacc[...] += jnp.sum(err * err, axis=0, keepdims=True)
        d = err * (1.0 / D)
        dg_ref[...] += jnp.sum(d * xh, axis=0, keepdims=True)
        db_ref[...] += jnp.sum(d, axis=0, keepdims=True)
        dxh = d * g_ref[...]
        dz = rstd * (dxh - jnp.mean(dxh, axis=-1, keepdims=True) - xh * jnp.mean(dxh * xh, axis=-1, keepdims=True))
        dz_ref[...] = dz
        dzb_ref[...] = dz.astype(dzb_ref.dtype)

        @pl.when(i == pl.num_programs(0) - 1)
        def _():
            loss_ref[...] = jnp.sum(lacc[...], axis=-1, keepdims=True) * (0.5 / D)

    row = pl.BlockSpec((tr, D), lambda i: (i, 0))
    vec = pl.BlockSpec((1, D), lambda i: (0, 0))
    one = pl.BlockSpec((1, 1), lambda i: (0, 0))
    return pl.pallas_call(
        body, name=name, grid=(T // tr,), in_specs=[row, vec, vec, row], out_specs=[one, row, row, vec, vec],
        out_shape=[_S((1, 1), _F32), _S((T, D), _F32), _S((T, D), _MM), _S((1, D), _F32), _S((1, D), _F32)],
        scratch_shapes=[pltpu.VMEM((1, D), _F32)], compiler_params=_params(("arbitrary",)),
    )(z, g, b, target)


def _conv3(X, cw, cb):
    return cb + cw[2:3] * X + cw[1:2] * pltpu.roll(X, 1, 0) + cw[0:1] * pltpu.roll(X, 2, 0)


def _ffn_mid_fwd(h, cw, cb, name):
    _, T, F = h.shape
    tr = _tile(T, 256)
    tc = _tile(F, 512)
    nb = tr // CONV_HALO

    def body(h_ref, p_ref, cw_ref, cb_ref, o_ref):
        i = pl.program_id(0)
        hc = []
        for part in range(2):
            prev = jnp.where(i == 0, 0.0, p_ref[part])
            X = jnp.concatenate([prev, h_ref[part]], axis=0)
            hc.append(_conv3(X, cw_ref[part], cb_ref[part])[CONV_HALO:])
        a, v = hc
        o_ref[...] = (a * _sigmoid(a) * v).astype(o_ref.dtype)

    return pl.pallas_call(
        body, name=name, grid=(T // tr, F // tc),
        in_specs=[pl.BlockSpec((2, tr, tc), lambda i, j: (0, i, j)),
                  pl.BlockSpec((2, CONV_HALO, tc), lambda i, j: (0, jnp.maximum(i * nb - 1, 0), j)),
                  pl.BlockSpec((2, 3, tc), lambda i, j: (0, 0, j)),
                  pl.BlockSpec((2, 1, tc), lambda i, j: (0, 0, j))],
        out_specs=pl.BlockSpec((tr, tc), lambda i, j: (i, j)), out_shape=_S((T, F), _MM),
        compiler_params=_params(("parallel", "parallel")),
    )(h, h, cw, cb)


def _ffn_mid_bwd(h, dact, cw, cb, name):
    _, T, F = h.shape
    tr = _tile(T, 256)
    tc = _tile(F, 512)
    nb = tr // CONV_HALO
    last_blk = T // CONV_HALO - 1
    n = tr + 2 * CONV_HALO

    def body(h_ref, p_ref, n_ref, d_ref, dn_ref, cw_ref, cb_ref, dh_ref, dcw_ref, dcb_ref):
        i = pl.program_id(1)
        is_first = i == 0
        is_last = i == pl.num_programs(1) - 1

        @pl.when(is_first)
        def _():
            dcw_ref[...] = jnp.zeros_like(dcw_ref)
            dcb_ref[...] = jnp.zeros_like(dcb_ref)

        X, hc = [], []
        for part in range(2):
            prev = jnp.where(is_first, 0.0, p_ref[part])
            nxt = jnp.where(is_last, 0.0, n_ref[part])
            Xp = jnp.concatenate([prev, h_ref[part], nxt], axis=0)
            X.append(Xp)
            hc.append(_conv3(Xp, cw_ref[part], cb_ref[part]))
        D = jnp.concatenate([jnp.zeros((CONV_HALO, tc), _F32), d_ref[...], jnp.where(is_last, 0.0, dn_ref[...])], axis=0)
        a, v = hc
        sg = _sigmoid(a)
        dhc = [D * v * sg * (1.0 + a * (1.0 - sg)), D * a * sg]
        lo, hi = CONV_HALO, CONV_HALO + tr
        for part in range(2):
            g = dhc[part]
            cwp = cw_ref[part]
            dh = cwp[2:3] * g + cwp[1:2] * pltpu.roll(g, n - 1, 0) + cwp[0:1] * pltpu.roll(g, n - 2, 0)
            dh_ref[part] = dh[lo:hi].astype(dh_ref.dtype)
            gt = g[lo:hi]
            dcw_ref[part, 2:3, :] += jnp.sum(gt * X[part][lo:hi], axis=0, keepdims=True)
            dcw_ref[part, 1:2, :] += jnp.sum(gt * pltpu.roll(X[part], 1, 0)[lo:hi], axis=0, keepdims=True)
            dcw_ref[part, 0:1, :] += jnp.sum(gt * pltpu.roll(X[part], 2, 0)[lo:hi], axis=0, keepdims=True)
            dcb_ref[part] += jnp.sum(gt, axis=0, keepdims=True)

    return pl.pallas_call(
        body, name=name, grid=(F // tc, T // tr),
        in_specs=[pl.BlockSpec((2, tr, tc), lambda j, i: (0, i, j)),
                  pl.BlockSpec((2, CONV_HALO, tc), lambda j, i: (0, jnp.maximum(i * nb - 1, 0), j)),
                  pl.BlockSpec((2, CONV_HALO, tc), lambda j, i: (0, jnp.minimum((i + 1) * nb, last_blk), j)),
                  pl.BlockSpec((tr, tc), lambda j, i: (i, j)),
                  pl.BlockSpec((CONV_HALO, tc), lambda j, i: (jnp.minimum((i + 1) * nb, last_blk), j)),
                  pl.BlockSpec((2, 3, tc), lambda j, i: (0, 0, j)),
                  pl.BlockSpec((2, 1, tc), lambda j, i: (0, 0, j))],
        out_specs=[pl.BlockSpec((2, tr, tc), lambda j, i: (0, i, j)),
                   pl.BlockSpec((2, 3, tc), lambda j, i: (0, 0, j)),
                   pl.BlockSpec((2, 1, tc), lambda j, i: (0, 0, j))],
        out_shape=[_S((2, T, F), _MM), _S((2, 3, F), _F32), _S((2, 1, F), _F32)],
        compiler_params=_params(("parallel", "arbitrary")),
    )(h, h, h, dact, dact, cw, cb)


def _ev_common(h_ref, hp_ref, lng_ref, lnb_ref, ws_ref, bias_ref, i, tr, W):
    H = W // A_HEAD
    u, gu = _gelu_and_grad(h_ref[0])
    v, gv = _gelu_and_grad(h_ref[1])
    mu = jnp.mean(v, axis=-1, keepdims=True)
    vc = v - mu
    rstd = lax.rsqrt(jnp.mean(vc * vc, axis=-1, keepdims=True) + LN_EPS)
    vhat = vc * rstd
    vb = (vhat * lng_ref[...] + lnb_ref[...]).astype(_MM)
    s_chunks = []
    for c in range(tr // A_CHUNK):
        r0 = c * A_CHUNK
        heads = [_dot(ws_ref[hd], vb[r0:r0 + A_CHUNK, hd * A_HEAD:(hd + 1) * A_HEAD]) for hd in range(H)]
        s_chunks.append(jnp.concatenate(heads, axis=1) + bias_ref[...])
    prev = jnp.where(i == 0, 0.0, hp_ref[...])
    X = jnp.concatenate([prev, h_ref[2]], axis=0)
    return u, gu, gv, rstd, vhat, vb, s_chunks, X


def _pool_inv_count(i, tr, rows, win):
    pos = i * tr + _row_index(rows) + 1
    return 1.0 / jnp.minimum(pos, win).astype(_F32)


def _pool_fwd(X, g, Wg, i, tr):
    xg = X[:, g * Wg:(g + 1) * Wg]
    s = xg
    for k in range(g + 1):
        s = s + pltpu.roll(s, 2 ** k, 0)
    return s[POOL_HALO:] * _pool_inv_count(i, tr, tr, 2 ** (g + 1)) - xg[POOL_HALO:]


def _ev_mid_fwd(h, lng, lnb, ws, bias, wp, sc, name):
    _, T, W = h.shape
    tr = _tile(T, 256)
    H = W // A_HEAD
    Wg = W // B_GROUPS
    nb = tr // POOL_HALO

    def body(h_ref, hp_ref, lng_ref, lnb_ref, ws_ref, bias_ref, wp_ref, sc_ref, o_ref):
        i = pl.program_id(0)
        u, _, _, _, _, _, s_chunks, X = _ev_common(h_ref, hp_ref, lng_ref, lnb_ref, ws_ref, bias_ref, i, tr, W)
        for c, s in enumerate(s_chunks):
            r0 = c * A_CHUNK
            o_ref[r0:r0 + A_CHUNK, 0:W] = (u[r0:r0 + A_CHUNK] * s).astype(o_ref.dtype)
        for g in range(B_GROUPS):
            p = _pool_fwd(X, g, Wg, i, tr)
            y = _dot(p.astype(_MM), wp_ref[g]) * sc_ref[:, g * Wg:(g + 1) * Wg]
            o_ref[:, W + g * Wg:W + (g + 1) * Wg] = y.astype(o_ref.dtype)

    vec = pl.BlockSpec((1, W), lambda i: (0, 0))
    return pl.pallas_call(
        body, name=name, grid=(T // tr,),
        in_specs=[pl.BlockSpec((3, tr, W), lambda i: (0, i, 0)),
                  pl.BlockSpec((None, POOL_HALO, W), lambda i: (2, jnp.maximum(i * nb - 1, 0), 0)),
                  vec, vec,
                  pl.BlockSpec((H, A_CHUNK, A_CHUNK), lambda i: (0, 0, 0)),
                  pl.BlockSpec((A_CHUNK, W), lambda i: (0, 0)),
                  pl.BlockSpec((B_GROUPS, Wg, Wg), lambda i: (0, 0, 0)),
                  vec],
        out_specs=pl.BlockSpec((tr, 2 * W), lambda i: (i, 0)), out_shape=_S((T, 2 * W), _MM),
        compiler_params=_params(("parallel",)),
    )(h, h, lng, lnb, ws, bias, wp, sc)


def _ev_mid_bwd(h, dy, lng, lnb, ws, bias, wp, sc, name):
    _, T, W = h.shape
    tr = _tile(T, 256)
    H = W // A_HEAD
    Wg = W // B_GROUPS
    nb = tr // POOL_HALO
    last_blk = T // POOL_HALO - 1
    n = tr + POOL_HALO

    def body(h_ref, hp_ref, dy_ref, dyn_ref, lng_ref, lnb_ref, ws_ref, bias_ref, wp_ref, sc_ref,
             dh_ref, dws_ref, dbias_ref, dlng_ref, dlnb_ref, dwp_ref, dsc_ref):
        i = pl.program_id(0)

        @pl.when(i == 0)
        def _():
            for r in (dws_ref, dbias_ref, dlng_ref, dlnb_ref, dwp_ref, dsc_ref):
                r[...] = jnp.zeros_like(r)

        u, gu, gv, rstd, vhat, vb, s_chunks, X = _ev_common(h_ref, hp_ref, lng_ref, lnb_ref, ws_ref, bias_ref, i, tr, W)
        rr = lax.broadcasted_iota(jnp.int32, (A_CHUNK, A_CHUNK), 0)
        cc = lax.broadcasted_iota(jnp.int32, (A_CHUNK, A_CHUNK), 1)
        tril = rr >= cc
        du_chunks, dvln_chunks = [], []
        for c, s in enumerate(s_chunks):
            r0 = c * A_CHUNK
            dya = dy_ref[r0:r0 + A_CHUNK, 0:W]
            du_chunks.append(dya * s)
            ds = dya * u[r0:r0 + A_CHUNK]
            dbias_ref[...] += ds
            dsb = ds.astype(_MM)
            heads = []
            for hd in range(H):
                cols = slice(hd * A_HEAD, (hd + 1) * A_HEAD)
                dws_ref[hd] += jnp.where(tril, _dot(dsb[:, cols], vb[r0:r0 + A_CHUNK, cols], _NT), 0.0)
                heads.append(_dot(ws_ref[hd], dsb[:, cols], _TN))
            dvln_chunks.append(jnp.concatenate(heads, axis=1))
        du = jnp.concatenate(du_chunks, axis=0)
        dvln = jnp.concatenate(dvln_chunks, axis=0)
        dlng_ref[...] += jnp.sum(dvln * vhat, axis=0, keepdims=True)
        dlnb_ref[...] += jnp.sum(dvln, axis=0, keepdims=True)
        dxh = dvln * lng_ref[...]
        dv = rstd * (dxh - jnp.mean(dxh, axis=-1, keepdims=True) - vhat * jnp.mean(dxh * vhat, axis=-1, keepdims=True))
        dh_ref[0] = (du * gu).astype(dh_ref.dtype)
        dh_ref[1] = (dv * gv).astype(dh_ref.dtype)

        dyb = dy_ref[:, W:2 * W]
        dyb_full = jnp.concatenate([dyb, jnp.where(i == pl.num_programs(0) - 1, 0.0, dyn_ref[...])], axis=0)
        for g in range(B_GROUPS):
            cols = slice(g * Wg, (g + 1) * Wg)
            pb = _pool_fwd(X, g, Wg, i, tr).astype(_MM)
            ypre = _dot(pb, wp_ref[g])
            dsc_ref[:, cols] += jnp.sum(dyb[:, cols] * ypre, axis=0, keepdims=True)
            dyp = (dyb_full[:, cols] * sc_ref[:, cols]).astype(_MM)
            dwp_ref[g] += _dot(pb, dyp[0:tr], _TN)
            dp = _dot(dyp, wp_ref[g], _NT)
            s = dp * _pool_inv_count(i, tr, n, 2 ** (g + 1))
            for k in range(g + 1):
                s = s + pltpu.roll(s, n - 2 ** k, 0)
            dh_ref[2, :, cols] = (s[0:tr] - dp[0:tr]).astype(dh_ref.dtype)

    vec = pl.BlockSpec((1, W), lambda i: (0, 0))
    ws_spec = pl.BlockSpec((H, A_CHUNK, A_CHUNK), lambda i: (0, 0, 0))
    bias_spec = pl.BlockSpec((A_CHUNK, W), lambda i: (0, 0))
    wp_spec = pl.BlockSpec((B_GROUPS, Wg, Wg), lambda i: (0, 0, 0))
    return pl.pallas_call(
        body, name=name, grid=(T // tr,),
        in_specs=[pl.BlockSpec((3, tr, W), lambda i: (0, i, 0)),
                  pl.BlockSpec((None, POOL_HALO, W), lambda i: (2, jnp.maximum(i * nb - 1, 0), 0)),
                  pl.BlockSpec((tr, 2 * W), lambda i: (i, 0)),
                  pl.BlockSpec((POOL_HALO, W), lambda i: (jnp.minimum((i + 1) * nb, last_blk), 1)),
                  vec, vec, ws_spec, bias_spec, wp_spec, vec],
        out_specs=[pl.BlockSpec((3, tr, W), lambda i: (0, i, 0)), ws_spec, bias_spec, vec, vec, wp_spec, vec],
        out_shape=[_S((3, T, W), _MM), _S((H, A_CHUNK, A_CHUNK), _F32), _S((A_CHUNK, W), _F32), _S((1, W), _F32),
                   _S((1, W), _F32), _S((B_GROUPS, Wg, Wg), _F32), _S((1, W), _F32)],
        compiler_params=_params(("arbitrary",)),
    )(h, h, dy, dy, lng, lnb, ws, bias, wp, sc)


def _chunk_cumsum(x, rin):
    s = 1
    while s < C_CHUNK:
        x = x + jnp.where(rin >= s, pltpu.roll(x, s, 0), 0.0)
        s *= 2
    return x


def _chunk_revcumsum(x, rin):
    n = x.shape[0]
    s = 1
    while s < C_CHUNK:
        x = x + jnp.where(rin + s < C_CHUNK, pltpu.roll(x, n - s, 0), 0.0)
        s *= 2
    return x


def _hgrn_gates(q, fl, lb, tr, tc):
    nch = tr // C_CHUNK
    sq = _sigmoid(q)
    sf = _sigmoid(fl)
    f = lb + (1.0 - lb) * sf
    logf = jnp.log(f)
    rin = _row_index(tr) % C_CHUNK
    b = _chunk_cumsum(logf, rin)
    tot3 = jnp.sum(logf.reshape(nch, C_CHUNK, tc), axis=1, keepdims=True)
    eb = jnp.exp(b)
    enb = jnp.exp(-b)
    ekb = jnp.exp(tot3 - b.reshape(nch, C_CHUNK, tc)).reshape(tr, tc)
    return sq, sf, f, rin, tot3, eb, enb, ekb


def _hgrn_prep_fwd(h, lb, name):
    _, T, D = h.shape
    tr = _tile(T, 512)
    tc = _tile(D, 512)
    nch = tr // C_CHUNK

    def body(q_ref, f_ref, v_ref, lb_ref, qd_ref, kd_ref, ke_ref, vb_ref, dec_ref):
        q = q_ref[...]
        sq, _, f, _, tot3, eb, enb, ekb = _hgrn_gates(q, f_ref[...], lb_ref[...], tr, tc)
        kk = 1.0 - f
        qd_ref[...] = (q * sq * eb).astype(qd_ref.dtype)
        kd_ref[...] = (kk * enb).astype(kd_ref.dtype)
        ke_ref[...] = (kk * ekb).astype(ke_ref.dtype)
        vb_ref[...] = v_ref[...].astype(vb_ref.dtype)
        dec_ref[...] = jnp.exp(tot3).reshape(nch, tc)

    def part(p):
        return pl.BlockSpec((None, tr, tc), lambda i, j: (p, i, j))

    blk = pl.BlockSpec((tr, tc), lambda i, j: (i, j))
    return pl.pallas_call(
        body, name=name, grid=(T // tr, D // tc),
        in_specs=[part(0), part(1), part(2), pl.BlockSpec((1, tc), lambda i, j: (0, j))],
        out_specs=[blk, blk, blk, blk, pl.BlockSpec((nch, tc), lambda i, j: (i, j))],
        out_shape=[_S((T, D), _MM)] * 4 + [_S((T // C_CHUNK, D), _F32)],
        compiler_params=_params(("parallel", "parallel")),
    )(h, h, h, lb)


def _tril_mask():
    rr = lax.broadcasted_iota(jnp.int32, (C_CHUNK, C_CHUNK), 0)
    cc = lax.broadcasted_iota(jnp.int32, (C_CHUNK, C_CHUNK), 1)
    return rr >= cc


def _hgrn_scan_fwd(qd, kd, ke, vb, dec, h, ng, name):
    T, D = qd.shape
    NH = D // C_HEAD
    N = T // C_CHUNK

    def body(qd_ref, kd_ref, ke_ref, vb_ref, dec_ref, g_ref, ng_ref, o_ref, y_ref, st_ref):
        mask = _tril_mask()

        def step(n, St):
            r = pl.ds(pl.multiple_of(n * C_CHUNK, C_CHUNK), C_CHUNK)
            Qd, Kd, Ke, V = qd_ref[r, :], kd_ref[r, :], ke_ref[r, :], vb_ref[r, :]
            att = jnp.where(mask, _dot(Qd, Kd, _NT), 0.0).astype(_MM)
            o_ref[r, :] = _dot(att, V) + _dot(Qd, St.astype(_MM), _NT)
            st_ref[n] = St
            return St * dec_ref[pl.ds(n, 1), :] + _dot(V, Ke, _TN)

        lax.fori_loop(0, N, step, jnp.zeros((C_HEAD, C_HEAD), _F32))
        o = o_ref[...]
        r = lax.rsqrt(jnp.mean(o * o, axis=-1, keepdims=True) + LN_EPS)
        y_ref[...] = (o * r * ng_ref[...] * _sigmoid(g_ref[...])).astype(y_ref.dtype)

    col = pl.BlockSpec((T, C_HEAD), lambda j: (0, j))
    return pl.pallas_call(
        body, name=name, grid=(NH,),
        in_specs=[col, col, col, col, pl.BlockSpec((N, C_HEAD), lambda j: (0, j)),
                  pl.BlockSpec((None, T, C_HEAD), lambda j: (3, 0, j)), pl.BlockSpec((1, C_HEAD), lambda j: (0, j))],
        out_specs=[col, col, pl.BlockSpec((None, N, C_HEAD, C_HEAD), lambda j: (j, 0, 0, 0))],
        out_shape=[_S((T, D), _F32), _S((T, D), _MM), _S((NH, N, C_HEAD, C_HEAD), _F32)],
        compiler_params=_params(("parallel",)),
    )(qd, kd, ke, vb, dec, h, ng)


def _hgrn_scan_bwd(qd, kd, ke, vb, dec, st, o, h, ng, dy, name):
    T, D = qd.shape
    NH = D // C_HEAD
    N = T // C_CHUNK

    def body(qd_ref, kd_ref, ke_ref, vb_ref, dec_ref, st_ref, o_ref, g_ref, ng_ref, dy_ref,
             dqd_ref, dkd_ref, dke_ref, dv_ref, dgate_ref, ddec_ref, dng_ref, do_s):
        o = o_ref[...]
        r = lax.rsqrt(jnp.mean(o * o, axis=-1, keepdims=True) + LN_EPS)
        oh = o * r
        gn = ng_ref[...]
        sg = _sigmoid(g_ref[...])
        d = dy_ref[...]
        dyn = d * sg
        dgate_ref[...] = (d * oh * gn * sg * (1.0 - sg)).astype(dgate_ref.dtype)
        dng_ref[...] = jnp.sum(dyn * oh, axis=0, keepdims=True)
        doh = dyn * gn
        do_s[...] = (r * (doh - oh * jnp.mean(doh * oh, axis=-1, keepdims=True))).astype(do_s.dtype)
        mask = _tril_mask()

        def step(k, dSt):
            n = N - 1 - k
            rws = pl.ds(pl.multiple_of(n * C_CHUNK, C_CHUNK), C_CHUNK)
            Qd, Kd, Ke, V, dO = qd_ref[rws, :], kd_ref[rws, :], ke_ref[rws, :], vb_ref[rws, :], do_s[rws, :]
            St = st_ref[n]
            Stb = St.astype(_MM)
            dStb = dSt.astype(_MM)
            att = jnp.where(mask, _dot(Qd, Kd, _NT), 0.0).astype(_MM)
            dA = jnp.where(mask, _dot(dO, V, _NT), 0.0).astype(_MM)
            dv_ref[rws, :] = (_dot(att, dO, _TN) + _dot(Ke, dStb, _NT)).astype(dv_ref.dtype)
            dqd_ref[rws, :] = _dot(dA, Kd) + _dot(dO, Stb)
            dkd_ref[rws, :] = _dot(dA, Qd, _TN)
            dke_ref[rws, :] = _dot(V, dStb)
            ddec_ref[pl.ds(n, 1), :] = jnp.sum(dSt * St, axis=0, keepdims=True)
            return dSt * dec_ref[pl.ds(n, 1), :] + _dot(dO, Qd, _TN)

        lax.fori_loop(0, N, step, jnp.zeros((C_HEAD, C_HEAD), _F32))

    col = pl.BlockSpec((T, C_HEAD), lambda j: (0, j))
    chk = pl.BlockSpec((N, C_HEAD), lambda j: (0, j))
    one = pl.BlockSpec((1, C_HEAD), lambda j: (0, j))
    return pl.pallas_call(
        body, name=name, grid=(NH,),
        in_specs=[col, col, col, col, chk, pl.BlockSpec((None, N, C_HEAD, C_HEAD), lambda j: (j, 0, 0, 0)), col,
                  pl.BlockSpec((None, T, C_HEAD), lambda j: (3, 0, j)), one, col],
        out_specs=[col, col, col, col, col, chk, one],
        out_shape=[_S((T, D), _F32)] * 3 + [_S((T, D), _MM)] * 2 + [_S((N, D), _F32), _S((1, D), _F32)],
        scratch_shapes=[pltpu.VMEM((T, C_HEAD), _MM)],
        compiler_params=_params(("parallel",)),
    )(qd, kd, ke, vb, dec, st, o, h, ng, dy)


def _hgrn_prep_bwd(h, lb, dqd, dkd, dke, dv, dgate, ddec, name):
    _, T, D = h.shape
    tr = _tile(T, 512)
    tc = _tile(D, 256)
    nch = tr // C_CHUNK

    def body(q_ref, f_ref, lb_ref, dqd_ref, dkd_ref, dke_ref, dv_ref, dgate_ref, ddec_ref, dh_ref, dlb_ref):
        @pl.when(pl.program_id(1) == 0)
        def _():
            dlb_ref[...] = jnp.zeros_like(dlb_ref)

        q = q_ref[...]
        lb = lb_ref[...]
        sq, sf, f, rin, tot3, eb, enb, ekb = _hgrn_gates(q, f_ref[...], lb, tr, tc)
        kk = 1.0 - f
        dQd, dKd, dKe = dqd_ref[...], dkd_ref[...], dke_ref[...]
        tq = dQd * eb
        tkd = dKd * enb
        tke = dKe * ekb
        ke_term = tke * kk
        db = tq * (q * sq) - tkd * kk - ke_term
        dtot3 = (jnp.sum(ke_term.reshape(nch, C_CHUNK, tc), axis=1, keepdims=True)
                 + (ddec_ref[...] * jnp.exp(tot3).reshape(nch, tc)).reshape(nch, 1, tc))
        dlogf = (_chunk_revcumsum(db, rin).reshape(nch, C_CHUNK, tc) + dtot3).reshape(tr, tc)
        df = dlogf / f - (tkd + tke)
        dh_ref[0] = (tq * sq * (1.0 + q * (1.0 - sq))).astype(dh_ref.dtype)
        dh_ref[1] = (df * (1.0 - lb) * sf * (1.0 - sf)).astype(dh_ref.dtype)
        dh_ref[2] = dv_ref[...]
        dh_ref[3] = dgate_ref[...]
        dlb_ref[...] += jnp.sum(df * (1.0 - sf), axis=0, keepdims=True)

    def part(p):
        return pl.BlockSpec((None, tr, tc), lambda j, i: (p, i, j))

    blk = pl.BlockSpec((tr, tc), lambda j, i: (i, j))
    vec = pl.BlockSpec((1, tc), lambda j, i: (0, j))
    return pl.pallas_call(
        body, name=name, grid=(D // tc, T // tr),
        in_specs=[part(0), part(1), vec, blk, blk, blk, blk, blk, pl.BlockSpec((nch, tc), lambda j, i: (i, j))],
        out_specs=[pl.BlockSpec((4, tr, tc), lambda j, i: (0, i, j)), vec],
        out_shape=[_S((4, T, D), _MM), _S((1, D), _F32)],
        compiler_params=_params(("parallel", "arbitrary")),
    )(h, h, lb, dqd, dkd, dke, dv, dgate, ddec)


def _sum_in_device_order(me1, own, land, name):
    R, C = own.shape
    tr = _tile(R, 256)

    def body(me_ref, own_ref, land_ref, o_ref):
        me = me_ref[0]
        g = None
        for j in range(N_DEV):
            slot = jnp.maximum(jnp.bitwise_xor(me, j) - 1, 0)
            p = jnp.where(me == j, own_ref[...], land_ref[slot])
            g = p if g is None else g + p
        o_ref[...] = g

    return pl.pallas_call(
        body, name=name,
        grid_spec=pltpu.PrefetchScalarGridSpec(
            num_scalar_prefetch=1, grid=(R // tr,),
            in_specs=[pl.BlockSpec((tr, C), lambda i, me: (i, 0)), pl.BlockSpec((N_DEV - 1, tr, C), lambda i, me: (0, i, 0))],
            out_specs=pl.BlockSpec((tr, C), lambda i, me: (i, 0))),
        out_shape=_S((R, C), _F32), compiler_params=_params(("parallel",)),
    )(me1, own, land)


def _adamw(parts, w, m, v, name):
    P, R, C = parts.shape
    tr = _tile(R, 128) if R % LANE == 0 else R

    def body(p_ref, w_ref, m_ref, v_ref, g_ref, d_ref, nm_ref, nv_ref):
        g = p_ref[0].astype(_F32)
        for s in range(1, P):
            g = g + p_ref[s].astype(_F32)
        nm = ADAM_B1 * m_ref[...] + (1.0 - ADAM_B1) * g
        nv = ADAM_B2 * v_ref[...] + (1.0 - ADAM_B2) * (g * g)
        m_hat = nm / (1.0 - ADAM_B1 ** ADAM_STEP)
        v_hat = nv / (1.0 - ADAM_B2 ** ADAM_STEP)
        g_ref[...] = g
        d_ref[...] = -ADAM_LR * (m_hat / (jnp.sqrt(v_hat) + ADAM_EPS) + ADAM_WD * w_ref[...])
        nm_ref[...] = nm
        nv_ref[...] = nv

    blk = pl.BlockSpec((tr, C), lambda i: (i, 0))
    return pl.pallas_call(
        body, name=name, grid=(R // tr,), in_specs=[pl.BlockSpec((P, tr, C), lambda i: (0, i, 0)), blk, blk, blk],
        out_specs=[blk] * 4, out_shape=[_S((R, C), _F32)] * 4, compiler_params=_params(("parallel",)),
    )(parts, w, m, v)


def _exchange(name, srcs, out_shapes, jobs, deps=()):
    ns, nj = len(srcs), len(jobs)

    nd = len(deps)

    def body(*refs):
        ins, outs = refs[:ns], refs[ns + nd:ns + nd + len(out_shapes)]
        send_sems, recv_sems, local_sems = refs[-3:]
        x, y, c = lax.axis_index("x"), lax.axis_index("y"), lax.axis_index("c")
        me = 4 * x + 2 * y + c
        local = []
        for ji, (si, src_fn, di, dst_fn) in enumerate(jobs):
            cp = pltpu.make_async_copy(src_fn(ins[si], me, me), dst_fn(outs[di], me), local_sems.at[ji])
            cp.start()
            local.append(cp)
        remote = []
        for k in range(1, N_DEV):
            px, py, pc = (x + (k >> 2)) % 2, (y + ((k >> 1) & 1)) % 2, (c + (k & 1)) % 2
            to = 4 * px + 2 * py + pc
            for ji, (si, src_fn, di, dst_fn) in enumerate(jobs):
                sem = (k - 1) * nj + ji
                cp = pltpu.make_async_remote_copy(
                    src_ref=src_fn(ins[si], me, to), dst_ref=dst_fn(outs[di], me),
                    send_sem=send_sems.at[sem], recv_sem=recv_sems.at[sem],
                    device_id=(px, py, pc), device_id_type=pl.DeviceIdType.MESH)
                cp.start()
                remote.append(cp)
        for cp in remote:
            cp.wait_recv()
        for cp in remote:
            cp.wait_send()
        for cp in local:
            cp.wait()

    hbm = pl.BlockSpec(memory_space=pltpu.HBM)
    return pl.pallas_call(
        body, name=name, in_specs=[hbm] * ns + [_ANY] * nd, out_specs=[hbm] * len(out_shapes), out_shape=list(out_shapes),
        scratch_shapes=[pltpu.SemaphoreType.DMA(((N_DEV - 1) * nj,)), pltpu.SemaphoreType.DMA(((N_DEV - 1) * nj,)),
                        pltpu.SemaphoreType.DMA((nj,))],
    )(*srcs, *deps)


def _whole(ref, me, to):
    return ref


def _slot_job(i, o):
    def dst(ref, me):
        return ref.at[me]
    return (i, _whole, o, dst)


_HBM = pl.BlockSpec(memory_space=pltpu.HBM)
_SEM = pl.BlockSpec(memory_space=pltpu.SEMAPHORE)
_ANY = pl.BlockSpec(memory_space=pl.ANY)
_N_PEER = N_DEV - 1


def _split_params():
    return pltpu.CompilerParams(has_side_effects=pltpu.SideEffectType.DATAFLOW_SIDE_EFFECTING)


def _blk(ref, axis, n, idx):
    if axis is None:
        return ref
    return ref.at[tuple([slice(None)] * axis + [pl.ds(pl.multiple_of(idx * n, n), n)])]


def _peer(k):
    x, y, c = lax.axis_index("x"), lax.axis_index("y"), lax.axis_index("c")
    px, py, pc = (x + (k >> 2)) % 2, (y + ((k >> 1) & 1)) % 2, (c + (k & 1)) % 2
    return (px, py, pc), 4 * px + 2 * py + pc, 4 * x + 2 * y + c


def _row_tile(rows, pref):
    best = None
    for d in range(16, min(rows, pref) + 1, 16):
        if rows % d == 0:
            best = d
    return best if best is not None else rows


def _place(w, me1, axis, name, layer=None, deps=()):
    R, C = w.shape[-2:]
    tr = _row_tile(R, 512)
    nb = R // tr
    lead = () if layer is None else (None,)
    pre = () if layer is None else (layer,)

    def body(me_ref, w_ref, *rest):
        rest[-1][...] = w_ref[...].astype(rest[-1].dtype)

    if axis == 1:
        out_spec = pl.BlockSpec((tr, C), lambda i, me: (i, me[0]))
        out_shape = _S((R, N_DEV * C), _MM)
    else:
        out_spec = pl.BlockSpec((tr, C), lambda i, me: (me[0] * nb + i, 0))
        out_shape = _S((N_DEV * R, C), _MM)
    return pl.pallas_call(
        body, name=name,
        grid_spec=pltpu.PrefetchScalarGridSpec(
            num_scalar_prefetch=1, grid=(nb,),
            in_specs=[pl.BlockSpec(lead + (tr, C), lambda i, me: pre + (i, 0))] + [_ANY] * len(deps), out_specs=out_spec),
        out_shape=out_shape, compiler_params=_params(("parallel",)),
    )(me1, w, *deps)


def _gather_start(name, fulls, axes, ns):
    nt = len(fulls)

    def body(*refs):
        ins, send, recv, token = refs[:nt], refs[nt], refs[nt + 1], refs[-1]
        for t in range(nt):
            for k in range(1, N_DEV):
                dev, _, me = _peer(k)
                blk = _blk(ins[t], axes[t], ns[t], me)
                pltpu.make_async_remote_copy(
                    src_ref=blk, dst_ref=blk, send_sem=send.at[t * _N_PEER + k - 1], recv_sem=recv.at[t * _N_PEER + k - 1],
                    device_id=dev, device_id_type=pl.DeviceIdType.MESH).start()
        token[...] = jnp.zeros_like(token)

    res = pl.pallas_call(
        body, name=name,
        out_shape=(pltpu.SemaphoreType.DMA((nt * _N_PEER,)), pltpu.SemaphoreType.DMA((nt * _N_PEER,)))
        + tuple(pltpu.HBM(f.shape, f.dtype) for f in fulls) + (_S((8, LANE), _F32),),
        in_specs=(_HBM,) * nt, out_specs=(_SEM, _SEM) + (_HBM,) * nt + (pl.BlockSpec(memory_space=pltpu.VMEM),),
        input_output_aliases={t: 2 + t for t in range(nt)}, compiler_params=_split_params(),
    )(*[pltpu.with_memory_space_constraint(f, pltpu.HBM) for f in fulls])
    return res[0], res[1], list(res[2:2 + nt]), res[-1]


def _gather_wait(name, fulls, axes, ns, send, recv, after):
    nt = len(fulls)

    def body(*refs):
        ins, send_r, recv_r = refs[:nt], refs[nt], refs[nt + 1]
        for t in range(nt):
            for k in range(1, N_DEV):
                dev, _, me = _peer(k)
                blk = _blk(ins[t], axes[t], ns[t], me)
                cp = pltpu.make_async_remote_copy(
                    src_ref=blk, dst_ref=blk, send_sem=send_r.at[t * _N_PEER + k - 1],
                    recv_sem=recv_r.at[t * _N_PEER + k - 1], device_id=dev, device_id_type=pl.DeviceIdType.MESH)
                cp.wait_send()
                cp.wait_recv()

    res = pl.pallas_call(
        body, name=name, out_shape=tuple(pltpu.HBM(f.shape, f.dtype) for f in fulls),
        in_specs=(_HBM,) * nt + (_SEM, _SEM, _ANY), out_specs=(_HBM,) * nt,
        input_output_aliases={t: t for t in range(nt)}, compiler_params=_split_params(),
    )(*fulls, send, recv, after)
    return list(res)


def _scatter_start(name, dw, axis, n):
    shard = tuple(n if a == axis else d for a, d in enumerate(dw.shape))
    land = lax.empty((_N_PEER,) + shard, dw.dtype)

    def body(dw_ref, land_ref, send, recv, dw_out, land_out, token):
        for k in range(1, N_DEV):
            dev, to, _ = _peer(k)
            pltpu.make_async_remote_copy(
                src_ref=_blk(dw_ref, axis, n, to), dst_ref=land_ref.at[k - 1], send_sem=send.at[k - 1],
                recv_sem=recv.at[k - 1], device_id=dev, device_id_type=pl.DeviceIdType.MESH).start()
        token[...] = jnp.zeros_like(token)

    return pl.pallas_call(
        body, name=name,
        out_shape=(pltpu.SemaphoreType.DMA((_N_PEER,)), pltpu.SemaphoreType.DMA((_N_PEER,)),
                   pltpu.HBM(dw.shape, dw.dtype), pltpu.HBM(land.shape, land.dtype), _S((8, LANE), _F32)),
        in_specs=(_HBM, _HBM), out_specs=(_SEM, _SEM, _HBM, _HBM, pl.BlockSpec(memory_space=pltpu.VMEM)),
        input_output_aliases={0: 2, 1: 3}, compiler_params=_split_params(),
    )(pltpu.with_memory_space_constraint(dw, pltpu.HBM), pltpu.with_memory_space_constraint(land, pltpu.HBM))


def _scatter_wait(name, dw, land, send, recv, axis, n, after):
    def body(dw_ref, land_ref, send_r, recv_r, after_ref, dw_out, land_out):
        for k in range(1, N_DEV):
            dev, to, _ = _peer(k)
            cp = pltpu.make_async_remote_copy(
                src_ref=_blk(dw_ref, axis, n, to), dst_ref=land_ref.at[k - 1], send_sem=send_r.at[k - 1],
                recv_sem=recv_r.at[k - 1], device_id=dev, device_id_type=pl.DeviceIdType.MESH)
            cp.wait_send()
            cp.wait_recv()

    return pl.pallas_call(
        body, name=name, out_shape=(pltpu.HBM(dw.shape, dw.dtype), pltpu.HBM(land.shape, land.dtype)),
        in_specs=(_HBM, _HBM, _SEM, _SEM, _ANY), out_specs=(_HBM, _HBM), input_output_aliases={0: 0, 1: 1},
        compiler_params=_split_params(),
    )(dw, land, send, recv, after)


def _adamw_big(me1, dw, land, w, m, v, axis, n, name, layer=None, into=None):
    R, C = land.shape[1:]
    tr = _row_tile(R, 128)
    nb = R // tr
    lead = () if layer is None else (None,)
    pre = () if layer is None else (layer,)

    def body(me_ref, own_ref, land_ref, w_ref, m_ref, v_ref, *rest):
        g_ref, d_ref, nm_ref, nv_ref = rest[-4:]
        g = own_ref[...].astype(_F32)
        for s in range(_N_PEER):
            g = g + land_ref[s].astype(_F32)
        nm = ADAM_B1 * m_ref[...] + (1.0 - ADAM_B1) * g
        nv = ADAM_B2 * v_ref[...] + (1.0 - ADAM_B2) * (g * g)
        m_hat = nm / (1.0 - ADAM_B1 ** ADAM_STEP)
        v_hat = nv / (1.0 - ADAM_B2 ** ADAM_STEP)
        g_ref[...] = g
        d_ref[...] = -ADAM_LR * (m_hat / (jnp.sqrt(v_hat) + ADAM_EPS) + ADAM_WD * w_ref[...])
        nm_ref[...] = nm
        nv_ref[...] = nv

    if axis == 1:
        own_spec = pl.BlockSpec((tr, C), lambda i, me: (i, me[0]))
    else:
        own_spec = pl.BlockSpec((tr, C), lambda i, me: (me[0] * nb + i, 0))
    blk = pl.BlockSpec(lead + (tr, C), lambda i, me: pre + (i, 0))
    in_specs = [own_spec, pl.BlockSpec((_N_PEER, tr, C), lambda i, me: (0, i, 0)), blk, blk, blk]
    args = [me1, dw, land, w, m, v]
    aliases = {}
    if into is not None:
        in_specs += [_ANY] * 4
        aliases = {6 + j: j for j in range(4)}
        args += list(into)
    return pl.pallas_call(
        body, name=name,
        grid_spec=pltpu.PrefetchScalarGridSpec(num_scalar_prefetch=1, grid=(nb,), in_specs=in_specs, out_specs=[blk] * 4),
        out_shape=[_S(w.shape, _F32)] * 4, input_output_aliases=aliases, compiler_params=_params(("parallel",)),
    )(*args)


def _pack(arrs):
    flat = jnp.concatenate([a.reshape(-1).astype(_F32) for a in arrs])
    pad = (-flat.shape[0]) % (LANE * LANE)
    return jnp.pad(flat, (0, pad)).reshape(-1, LANE)


def _unpack(mat, shapes):
    flat = mat.reshape(-1)
    out, off = [], 0
    for s in shapes:
        n = 1
        for d in s:
            n *= d
        out.append(flat[off:off + n].reshape(s))
        off += n
    return out


def _lb_of(lb_param):
    lb_all = jnp.cumsum(jax.nn.softmax(lb_param.astype(_F32), axis=0), axis=0)
    return (lb_all - lb_all[0])[1:2]


def kernel(x, ev_w_in, ev_ln_v_g, ev_ln_v_b, ev_w_s, ev_b_s, ev_w_pool, ev_pool_scale, ev_w_out, od_w_in, od_norm_g, od_w_out, lb_param, ffn_w_up, ffn_conv_w, ffn_conv_b, ffn_w_down, ln1_g, ln1_b, ln2_g, ln2_b, loss_target, m_ev_w_in, m_ev_ln_v_g, m_ev_ln_v_b, m_ev_w_s, m_ev_b_s, m_ev_w_pool, m_ev_pool_scale, m_ev_w_out, m_od_w_in, m_od_norm_g, m_od_w_out, m_lb_param, m_ffn_w_up, m_ffn_conv_w, m_ffn_conv_b, m_ffn_w_down, m_ln1_g, m_ln1_b, m_ln2_g, m_ln2_b, v_ev_w_in, v_ev_ln_v_g, v_ev_ln_v_b, v_ev_w_s, v_ev_b_s, v_ev_w_pool, v_ev_pool_scale, v_ev_w_out, v_od_w_in, v_od_norm_g, v_od_w_out, v_lb_param, v_ffn_w_up, v_ffn_conv_w, v_ffn_conv_b, v_ffn_w_down, v_ln1_g, v_ln1_b, v_ln2_g, v_ln2_b):
    me = 4 * lax.axis_index("x") + 2 * lax.axis_index("y") + lax.axis_index("c")
    T, D = x.shape[1], x.shape[2]
    W = ev_ln_v_g.shape[1]
    H = W // A_HEAD
    Wg = W // B_GROUPS
    F2 = ffn_conv_b.shape[1]
    F = F2 // 2
    n_in0, n_out0 = ev_w_in.shape[2], ev_w_out.shape[1]
    n_in1, n_out1 = od_w_in.shape[2], od_w_out.shape[1]
    n_up, n_dn = ffn_w_up.shape[2], ffn_w_down.shape[1]
    n_pool, n_ng, n_cw = ev_w_pool.shape[2], od_norm_g.shape[1], ffn_conv_w.shape[2]

    small_shards = [od_norm_g, ffn_conv_w, ev_w_pool]
    small_pack = _pack(small_shards)
    small_all = _exchange("gather_small_params", [small_pack], [_S((N_DEV,) + small_pack.shape, _F32)], [_slot_job(0, 0)])[0]

    me1 = me.astype(jnp.int32).reshape(1)
    groups = [
        ("g0", ((ev_w_in[0], None, "w_in0"), (ev_w_out[0], None, "w_out0")), (n_in0, n_out0)),
        ("g1", ((ffn_w_up, 0, "w_up0"), (ffn_w_down, 0, "w_dn0")), (n_up, n_dn)),
        ("g2", ((od_w_in[0], None, "w_in1"), (od_w_out[0], None, "w_out1")), (n_in1, n_out1)),
        ("g3", ((ffn_w_up, 1, "w_up1"), (ffn_w_down, 1, "w_dn1")), (n_up, n_dn)),
    ]
    started, tokens = {}, [small_all]
    for gname, shards, ns in groups:
        axes = (1, 0)
        fulls = [_place(w, me1, ax, "place_" + nm, layer, deps=tokens) for (w, layer, nm), ax in zip(shards, axes)]
        send, recv, fulls, token = _gather_start("gather_start_" + gname, fulls, axes, ns)
        started[gname] = (fulls, axes, ns, send, recv)
        tokens = [token]

    def gathered(gname, after):
        fulls, axes, ns, send, recv = started[gname]
        return _gather_wait("gather_wait_" + gname, fulls, axes, ns, send, recv, after)

    ng_parts, cw_parts, wp_parts = [], [], []
    for j in range(N_DEV):
        a, b, c = _unpack(small_all[j], [s.shape for s in small_shards])
        ng_parts.append(a)
        cw_parts.append(b)
        wp_parts.append(c)
    norm_g = jnp.concatenate(ng_parts, axis=1)
    conv_w = jnp.concatenate(cw_parts, axis=2)
    w_pool = jnp.concatenate(wp_parts, axis=2)[0]
    cw_l = [conv_w[l].reshape(3, 2, F).transpose(1, 0, 2) for l in range(DEPTH)]
    cb_l = [ffn_conv_b[l].reshape(2, 1, F) for l in range(DEPTH)]
    ws_tril = jnp.tril(ev_w_s[0]).astype(_MM)
    bias = jnp.repeat(ev_b_s[0].T, A_HEAD, axis=1)
    wp_b = w_pool.astype(_MM)
    lb, lb_vjp = jax.vjp(_lb_of, lb_param)

    x2 = x[0]
    xb = _cast(x2, _MM, "cast_x", deps=tokens)
    w_in0, w_out0 = gathered("g0", xb)
    h0 = _mm(xb, w_in0, "nn", _F32, "ev_in", out_parts=3)
    yab = _ev_mid_fwd(h0, ev_ln_v_g, ev_ln_v_b, ws_tril, bias, wp_b, ev_pool_scale, "ev_mid_fwd")
    z1 = _mm(yab, w_out0, "nn", _F32, "ev_out", add=x2, add_scale=ALPHA)
    x1, x1b = _ln_fwd(z1, ln1_g[0:1], ln1_b[0:1], "ln1_0")
    w_up0, w_dn0 = gathered("g1", x1b)
    hf0 = _mm(x1b, w_up0, "nn", _F32, "ffn_up", out_parts=2)
    act0 = _ffn_mid_fwd(hf0, cw_l[0], cb_l[0], "ffn_mid_fwd")
    z2 = _mm(act0, w_dn0, "nn", _F32, "ffn_down", add=x1, add_scale=ALPHA)
    x2_, x2b = _ln_fwd(z2, ln2_g[0:1], ln2_b[0:1], "ln2_0")
    w_in1, w_out1 = gathered("g2", x2b)
    h1 = _mm(x2b, w_in1, "nn", _F32, "od_in", out_parts=4)
    qd, kd, ke, vb, dec = _hgrn_prep_fwd(h1, lb, "hgrn_prep_fwd")
    o, yo, st = _hgrn_scan_fwd(qd, kd, ke, vb, dec, h1, norm_g, "hgrn_scan_fwd")
    z3 = _mm(yo, w_out1, "nn", _F32, "od_out", add=x2_, add_scale=ALPHA)
    x3, x3b = _ln_fwd(z3, ln1_g[1:2], ln1_b[1:2], "ln1_1")
    w_up1, w_dn1 = gathered("g3", x3b)
    hf1 = _mm(x3b, w_up1, "nn", _F32, "ffn_up", out_parts=2)
    act1 = _ffn_mid_fwd(hf1, cw_l[1], cb_l[1], "ffn_mid_fwd")
    z4 = _mm(act1, w_dn1, "nn", _F32, "ffn_down", add=x3, add_scale=ALPHA)

    scat = {}

    def scatter(key, dw, axis, n):
        send, recv, dw, land, token = _scatter_start("scatter_start_" + key, dw, axis, n)
        scat[key] = (dw, land, send, recv, axis, n)
        return [token]

    loss11, dz4, dz4b, g_ln2_1, b_ln2_1 = _ln_loss_bwd(z4, ln2_g[1:2], ln2_b[1:2], loss_target[0], "ln_loss_bwd")
    tok = scatter("dn1", _mm(act1, dz4b, "tn", _XCH, "ffn_down_dw"), 0, n_dn)
    dact1 = _mm(dz4b, w_dn1, "nt", _F32, "ffn_down_dx", deps=tok)
    dhf1, dcw1, dcb1 = _ffn_mid_bwd(hf1, dact1, cw_l[1], cb_l[1], "ffn_mid_bwd")
    tok = scatter("up1", _mm(x3b, dhf1, "tn", _XCH, "ffn_up_dw", b_parts=2, deps=tok), 1, n_up)
    dx3 = _mm(dhf1, w_up1, "nt", _F32, "ffn_up_dx", a_parts=2, add=dz4, add_scale=ALPHA, deps=tok)
    dz3, dz3b, g_ln1_1, b_ln1_1 = _ln_bwd(z3, ln1_g[1:2], dx3, "ln_bwd")
    tok = scatter("out1", _mm(yo, dz3b, "tn", _XCH, "od_out_dw", deps=tok), 0, n_out1)
    dyo = _mm(dz3b, w_out1, "nt", _F32, "od_out_dx", deps=tok)
    dqd, dkd, dke, dv, dgate, ddec, dng = _hgrn_scan_bwd(qd, kd, ke, vb, dec, st, o, h1, norm_g, dyo, "hgrn_scan_bwd")
    dh1, dlb = _hgrn_prep_bwd(h1, lb, dqd, dkd, dke, dv, dgate, ddec, "hgrn_prep_bwd")
    tok = scatter("in1", _mm(x2b, dh1, "tn", _XCH, "od_in_dw", b_parts=4, deps=tok), 1, n_in1)
    dx2 = _mm(dh1, w_in1, "nt", _F32, "od_in_dx", a_parts=4, add=dz3, add_scale=ALPHA, deps=tok)
    dz2, dz2b, g_ln2_0, b_ln2_0 = _ln_bwd(z2, ln2_g[0:1], dx2, "ln_bwd")
    tok = scatter("dn0", _mm(act0, dz2b, "tn", _XCH, "ffn_down_dw", deps=tok), 0, n_dn)
    dact0 = _mm(dz2b, w_dn0, "nt", _F32, "ffn_down_dx", deps=tok)
    dhf0, dcw0, dcb0 = _ffn_mid_bwd(hf0, dact0, cw_l[0], cb_l[0], "ffn_mid_bwd")
    tok = scatter("up0", _mm(x1b, dhf0, "tn", _XCH, "ffn_up_dw", b_parts=2, deps=tok), 1, n_up)
    dx1 = _mm(dhf0, w_up0, "nt", _F32, "ffn_up_dx", a_parts=2, add=dz2, add_scale=ALPHA, deps=tok)
    dz1, dz1b, g_ln1_0, b_ln1_0 = _ln_bwd(z1, ln1_g[0:1], dx1, "ln_bwd")
    tok = scatter("out0", _mm(yab, dz1b, "tn", _XCH, "ev_out_dw", deps=tok), 0, n_out0)
    dyab = _mm(dz1b, w_out0, "nt", _F32, "ev_out_dx", deps=tok)
    dh0, dws, dbias, dlng, dlnb, dwp, dsc = _ev_mid_bwd(h0, dyab, ev_ln_v_g, ev_ln_v_b, ws_tril, bias, wp_b,
                                                        ev_pool_scale, "ev_mid_bwd")

    g_b_s = dbias.reshape(A_CHUNK, H, A_HEAD).sum(axis=-1).T[None]
    g_conv_w = jnp.stack([d.transpose(1, 0, 2).reshape(3, F2) for d in (dcw0, dcw1)])
    g_conv_b = jnp.stack([d.reshape(F2) for d in (dcb0, dcb1)])
    small_names = ["ev_ln_v_g", "ev_ln_v_b", "ev_w_s", "ev_b_s", "ev_w_pool", "ev_pool_scale", "od_norm_g", "lb_param",
                   "ffn_conv_w", "ffn_conv_b", "ln1_g", "ln1_b", "ln2_g", "ln2_b"]
    small_grads = [dlng, dlnb, dws[None], g_b_s, dwp[None], dsc, dng, lb_vjp(dlb)[0], g_conv_w, g_conv_b,
                   jnp.concatenate([g_ln1_0, g_ln1_1]), jnp.concatenate([b_ln1_0, b_ln1_1]),
                   jnp.concatenate([g_ln2_0, g_ln2_1]), jnp.concatenate([b_ln2_0, b_ln2_1])]
    full_shapes = [g.shape for g in small_grads]
    tok = tok + scatter("small", _pack(small_grads), None, None)
    tok = scatter("in0", _mm(xb, dh0, "tn", _XCH, "ev_in_dw", b_parts=3, deps=tok), 1, n_in0)
    grad_x = _mm(dh0, w_in0, "nt", _F32, "ev_in_dx", a_parts=3, add=dz1, add_scale=ALPHA, deps=tok)

    def landed(key, after):
        dw, land, send, recv, axis, n = scat[key]
        dw, land = _scatter_wait("scatter_wait_" + key, dw, land, send, recv, axis, n, after)
        return me1, dw, land

    big = {}
    r_dn = _adamw_big(*landed("dn1", grad_x), ffn_w_down, m_ffn_w_down, v_ffn_w_down, 0, n_dn, "adamw_w_dn1", layer=1)
    r_up = _adamw_big(*landed("up1", r_dn[0]), ffn_w_up, m_ffn_w_up, v_ffn_w_up, 1, n_up, "adamw_w_up1", layer=1)
    big["od_w_out"] = _adamw_big(*landed("out1", r_up[0]), od_w_out[0], m_od_w_out[0], v_od_w_out[0], 0, n_out1, "adamw_w_out1")
    big["od_w_in"] = _adamw_big(*landed("in1", big["od_w_out"][0]), od_w_in[0], m_od_w_in[0], v_od_w_in[0], 1, n_in1, "adamw_w_in1")
    big["ffn_w_down"] = _adamw_big(*landed("dn0", big["od_w_in"][0]), ffn_w_down, m_ffn_w_down, v_ffn_w_down, 0, n_dn,
                                   "adamw_w_dn0", layer=0, into=r_dn)
    big["ffn_w_up"] = _adamw_big(*landed("up0", big["ffn_w_down"][0]), ffn_w_up, m_ffn_w_up, v_ffn_w_up, 1, n_up,
                                 "adamw_w_up0", layer=0, into=r_up)
    big["ev_w_out"] = _adamw_big(*landed("out0", big["ffn_w_up"][0]), ev_w_out[0], m_ev_w_out[0], v_ev_w_out[0], 0, n_out0, "adamw_w_out0")
    big["ev_w_in"] = _adamw_big(*landed("in0", big["ev_w_out"][0]), ev_w_in[0], m_ev_w_in[0], v_ev_w_in[0], 1, n_in0, "adamw_w_in0")

    gsum = _unpack(_sum_in_device_order(*landed("small", big["ev_w_in"][0]), "sum_small_grads"), full_shapes)
    given = dict(ev_ln_v_g=(ev_ln_v_g, m_ev_ln_v_g, v_ev_ln_v_g), ev_ln_v_b=(ev_ln_v_b, m_ev_ln_v_b, v_ev_ln_v_b),
                 ev_w_s=(ev_w_s, m_ev_w_s, v_ev_w_s), ev_b_s=(ev_b_s, m_ev_b_s, v_ev_b_s),
                 ev_w_pool=(ev_w_pool, m_ev_w_pool, v_ev_w_pool),
                 ev_pool_scale=(ev_pool_scale, m_ev_pool_scale, v_ev_pool_scale),
                 od_norm_g=(od_norm_g, m_od_norm_g, v_od_norm_g), lb_param=(lb_param, m_lb_param, v_lb_param),
                 ffn_conv_w=(ffn_conv_w, m_ffn_conv_w, v_ffn_conv_w), ffn_conv_b=(ffn_conv_b, m_ffn_conv_b, v_ffn_conv_b),
                 ln1_g=(ln1_g, m_ln1_g, v_ln1_g), ln1_b=(ln1_b, m_ln1_b, v_ln1_b), ln2_g=(ln2_g, m_ln2_g, v_ln2_g),
                 ln2_b=(ln2_b, m_ln2_b, v_ln2_b))
    shard_axis = dict(ev_w_pool=(2, n_pool), od_norm_g=(1, n_ng), ffn_conv_w=(2, n_cw))
    local_g = []
    for name, g in zip(small_names, gsum):
        if name in shard_axis:
            ax, n = shard_axis[name]
            g = lax.dynamic_slice_in_dim(g, me * n, n, axis=ax)
        local_g.append(g)
    local_shapes = [g.shape for g in local_g]
    res = _adamw(_pack(local_g)[None], _pack([given[n][0] for n in small_names]), _pack([given[n][1] for n in small_names]),
                 _pack([given[n][2] for n in small_names]), "adamw_small")
    small = {n: [] for n in small_names}
    for r in res:
        for n, a in zip(small_names, _unpack(r, local_shapes)):
            small[n].append(a)

    loss = lax.psum(loss11[0, 0], ("x", "y", "c"))
    order = ["ev_w_in", "ev_ln_v_g", "ev_ln_v_b", "ev_w_s", "ev_b_s", "ev_w_pool", "ev_pool_scale", "ev_w_out", "od_w_in",
             "od_norm_g", "od_w_out", "lb_param", "ffn_w_up", "ffn_conv_w", "ffn_conv_b", "ffn_w_down", "ln1_g", "ln1_b",
             "ln2_g", "ln2_b"]
    shapes = dict(ev_w_in=ev_w_in.shape, ev_w_out=ev_w_out.shape, od_w_in=od_w_in.shape, od_w_out=od_w_out.shape,
                  ffn_w_up=ffn_w_up.shape, ffn_w_down=ffn_w_down.shape)
    outs = [loss, grad_x[None]]
    for kind in range(4):
        for n in order:
            outs.append(big[n][kind].reshape(shapes[n]) if n in big else small[n][kind])
    return tuple(outs)
```

```python
import functools

import jax
import jax.numpy as jnp
from jax import lax
from jax.experimental import pallas as pl
from jax.experimental.pallas import tpu as pltpu

_MM = jnp.bfloat16
_XCH = jnp.bfloat16

DEPTH = 2
ALPHA = (2 * DEPTH) ** 0.25
LN_EPS = 1e-5
A_CHUNK = 128
A_HEAD = 128
B_GROUPS = 4
POOL_HALO = 16
C_CHUNK = 64
C_HEAD = 128
CONV_HALO = 8
ADAM_LR, ADAM_B1, ADAM_B2, ADAM_EPS, ADAM_WD, ADAM_STEP = 0.001, 0.9, 0.999, 1e-08, 0.01, 10
N_DEV = 8
LANE = 128
VMEM_LIMIT = 56 * 1024 * 1024
MM_FULL_K = 2048
MM_FULL_K_TN = 4096
MM_DEEP_K = 2816

_F32 = jnp.float32
_NN = (((1,), (0,)), ((), ()))
_NT = (((1,), (1,)), ((), ()))
_TN = (((0,), (0,)), ((), ()))
_S = jax.ShapeDtypeStruct


def _dot(a, b, dims=_NN):
    return lax.dot_general(a, b, dims, preferred_element_type=_F32)


def _tile(dim, pref):
    best = None
    d = LANE
    while d <= min(dim, pref):
        if dim % d == 0:
            best = d
        d += LANE
    return best if best is not None else dim


def _params(sem):
    return pltpu.CompilerParams(dimension_semantics=sem, vmem_limit_bytes=VMEM_LIMIT)


def _sigmoid(x):
    return 1.0 / (1.0 + jnp.exp(-x))


_GELU_C = 0.7978845608028654
_GELU_A = 0.044715


def _gelu_and_grad(x):
    t = jnp.tanh(_GELU_C * (x + _GELU_A * x * x * x))
    y = 0.5 * x * (1.0 + t)
    dy = 0.5 * (1.0 + t) + 0.5 * x * (1.0 - t * t) * _GELU_C * (1.0 + 3.0 * _GELU_A * x * x)
    return y, dy


def _row_index(n):
    return lax.broadcasted_iota(jnp.int32, (n, 1), 0)


def _mm_tiles(mode, M, N, K, with_add):
    if mode == "tn":
        return _tile(M, 1024), _tile(N, 1024), _tile(K, MM_FULL_K_TN)
    if K <= MM_FULL_K:
        return _tile(M, 1024 if with_add else 2048), _tile(N, 1024 if mode == "nn" else 512), K
    return _tile(M, 1024), _tile(N, 1024), _tile(K, MM_DEEP_K)


def _mm(a, b, mode, out_dtype, name, *, a_parts=1, b_parts=1, out_parts=1, add=None, add_scale=1.0, deps=(), tiles=None):
    if mode == "nn":
        M, K = a.shape
        N = b.shape[1]
    elif mode == "nt":
        if a_parts > 1:
            M, K = a.shape[1], a.shape[2] * a_parts
        else:
            M, K = a.shape
        N = b.shape[0]
    else:
        K, M = a.shape
        N = b.shape[-1] * b_parts
    tm, tn, tk = tiles if tiles is not None else _mm_tiles(mode, M, N // max(b_parts, out_parts), K // a_parts, add is not None)
    nk = K // tk
    npj = (N // max(b_parts, out_parts)) // tn
    nkp = (K // a_parts) // tk
    if mode == "nn":
        a_spec = pl.BlockSpec((tm, tk), lambda i, j, k: (i, k))
        b_spec = pl.BlockSpec((tk, tn), lambda i, j, k: (k, j))
        dims = _NN
    elif mode == "nt":
        if a_parts > 1:
            a_spec = pl.BlockSpec((None, tm, tk), lambda i, j, k: (k // nkp, i, k % nkp))
        else:
            a_spec = pl.BlockSpec((tm, tk), lambda i, j, k: (i, k))
        b_spec = pl.BlockSpec((tn, tk), lambda i, j, k: (j, k))
        dims = _NT
    else:
        a_spec = pl.BlockSpec((tk, tm), lambda i, j, k: (k, i))
        if b_parts > 1:
            b_spec = pl.BlockSpec((None, tk, tn), lambda i, j, k: (j // npj, k, j % npj))
        else:
            b_spec = pl.BlockSpec((tk, tn), lambda i, j, k: (k, j))
        dims = _TN
    if out_parts > 1:
        out_spec = pl.BlockSpec((None, tm, tn), lambda i, j, k: (j // npj, i, j % npj))
        out_shape = _S((out_parts, M, N // out_parts), out_dtype)
    else:
        out_spec = pl.BlockSpec((tm, tn), lambda i, j, k: (i, j))
        out_shape = _S((M, N), out_dtype)
    in_specs = [a_spec, b_spec]
    args = [a, b]
    if add is not None:
        in_specs.append(pl.BlockSpec((tm, tn), lambda i, j, k: (i, j)))
        args.append(add)
    in_specs += [_ANY] * len(deps)
    args += list(deps)

    def finish(r, refs, o_ref):
        if add is not None:
            r = r + add_scale * refs[2][...]
        o_ref[...] = r.astype(o_ref.dtype)

    def body_one(*refs):
        finish(_dot(refs[0][...], refs[1][...], dims), refs, refs[-1])

    def body_acc(*refs):
        o_ref, acc = refs[-2], refs[-1]
        k = pl.program_id(2)

        @pl.when(k == 0)
        def _():
            acc[...] = jnp.zeros_like(acc)

        acc[...] += _dot(refs[0][...], refs[1][...], dims)

        @pl.when(k == nk - 1)
        def _():
            finish(acc[...], refs, o_ref)

    return pl.pallas_call(
        body_one if nk == 1 else body_acc, name=name, grid=(M // tm, N // tn, nk), in_specs=in_specs,
        out_specs=out_spec, out_shape=out_shape,
        scratch_shapes=[] if nk == 1 else [pltpu.VMEM((tm, tn), _F32)],
        compiler_params=_params(("parallel", "parallel", "arbitrary")),
    )(*args)


def _cast(x2d, dtype, name, deps=()):
    R, C = x2d.shape
    tr = _tile(R, 512) if R % LANE == 0 else R

    def body(x_ref, *rest):
        rest[-1][...] = x_ref[...].astype(rest[-1].dtype)

    return pl.pallas_call(
        body, name=name, grid=(R // tr,), in_specs=[pl.BlockSpec((tr, C), lambda i: (i, 0))] + [_ANY] * len(deps),
        out_specs=pl.BlockSpec((tr, C), lambda i: (i, 0)), out_shape=_S((R, C), dtype),
        compiler_params=_params(("parallel",)),
    )(x2d, *deps)


def _ln_fwd(z, g, b, name):
    T, D = z.shape
    tr = _tile(T, 256)

    def body(z_ref, g_ref, b_ref, y_ref, yb_ref):
        zz = z_ref[...]
        mu = jnp.mean(zz, axis=-1, keepdims=True)
        zc = zz - mu
        var = jnp.mean(zc * zc, axis=-1, keepdims=True)
        y = zc * lax.rsqrt(var + LN_EPS) * g_ref[...] + b_ref[...]
        y_ref[...] = y
        yb_ref[...] = y.astype(yb_ref.dtype)

    row = pl.BlockSpec((tr, D), lambda i: (i, 0))
    vec = pl.BlockSpec((1, D), lambda i: (0, 0))
    return pl.pallas_call(
        body, name=name, grid=(T // tr,), in_specs=[row, vec, vec], out_specs=[row, row],
        out_shape=[_S((T, D), _F32), _S((T, D), _MM)], compiler_params=_params(("parallel",)),
    )(z, g, b)


def _ln_bwd(z, g, dy, name):
    T, D = z.shape
    tr = _tile(T, 256)

    def body(z_ref, g_ref, dy_ref, dz_ref, dzb_ref, dg_ref, db_ref):
        @pl.when(pl.program_id(0) == 0)
        def _():
            dg_ref[...] = jnp.zeros_like(dg_ref)
            db_ref[...] = jnp.zeros_like(db_ref)

        zz = z_ref[...]
        mu = jnp.mean(zz, axis=-1, keepdims=True)
        zc = zz - mu
        rstd = lax.rsqrt(jnp.mean(zc * zc, axis=-1, keepdims=True) + LN_EPS)
        xh = zc * rstd
        d = dy_ref[...]
        dg_ref[...] += jnp.sum(d * xh, axis=0, keepdims=True)
        db_ref[...] += jnp.sum(d, axis=0, keepdims=True)
        dxh = d * g_ref[...]
        dz = rstd * (dxh - jnp.mean(dxh, axis=-1, keepdims=True) - xh * jnp.mean(dxh * xh, axis=-1, keepdims=True))
        dz_ref[...] = dz
        dzb_ref[...] = dz.astype(dzb_ref.dtype)

    row = pl.BlockSpec((tr, D), lambda i: (i, 0))
    vec = pl.BlockSpec((1, D), lambda i: (0, 0))
    return pl.pallas_call(
        body, name=name, grid=(T // tr,), in_specs=[row, vec, row], out_specs=[row, row, vec, vec],
        out_shape=[_S((T, D), _F32), _S((T, D), _MM), _S((1, D), _F32), _S((1, D), _F32)],
        compiler_params=_params(("arbitrary",)),
    )(z, g, dy)


def _ln_loss_bwd(z, g, b, target, name):
    T, D = z.shape
    tr = _tile(T, 256)

    def body(z_ref, g_ref, b_ref, t_ref, loss_ref, dz_ref, dzb_ref, dg_ref, db_ref, lacc):
        i = pl.program_id(0)

        @pl.when(i == 0)
        def _():
            dg_ref[...] = jnp.zeros_like(dg_ref)
            db_ref[...] = jnp.zeros_like(db_ref)
            lacc[...] = jnp.zeros_like(lacc)

        zz = z_ref[...]
        mu = jnp.mean(zz, axis=-1, keepdims=True)
        zc = zz - mu
        rstd = lax.rsqrt(jnp.mean(zc * zc, axis=-1, keepdims=True) + LN_EPS)
        xh = zc * rstd
        err = xh * g_ref[...] + b_ref[...] - t_ref[...]
        lacc[...] += jnp.sum(err * err, axis=0, keepdims=True)
        d = err * (1.0 / D)
        dg_ref[...] += jnp.sum(d * xh, axis=0, keepdims=True)
        db_ref[...] += jnp.sum(d, axis=0, keepdims=True)
        dxh = d * g_ref[...]
        dz = rstd * (dxh - jnp.mean(dxh, axis=-1, keepdims=True) - xh * jnp.mean(dxh * xh, axis=-1, keepdims=True))
        dz_ref[...] = dz
        dzb_ref[...] = dz.astype(dzb_ref.dtype)

        @pl.when(i == pl.num_programs(0) - 1)
        def _():
            loss_ref[...] = jnp.sum(lacc[...], axis=-1, keepdims=True) * (0.5 / D)

    row = pl.BlockSpec((tr, D), lambda i: (i, 0))
    vec = pl.BlockSpec((1, D), lambda i: (0, 0))
    one = pl.BlockSpec((1, 1), lambda i: (0, 0))
    return pl.pallas_call(
        body, name=name, grid=(T // tr,), in_specs=[row, vec, vec, row], out_specs=[one, row, row, vec, vec],
        out_shape=[_S((1, 1), _F32), _S((T, D), _F32), _S((T, D), _MM), _S((1, D), _F32), _S((1, D), _F32)],
        scratch_shapes=[pltpu.VMEM((1, D), _F32)], compiler_params=_params(("arbitrary",)),
    )(z, g, b, target)


def _conv3(X, cw, cb):
    return cb + cw[2:3] * X + cw[1:2] * pltpu.roll(X, 1, 0) + cw[0:1] * pltpu.roll(X, 2, 0)


def _ffn_mid_fwd(h, cw, cb, name):
    _, T, F = h.shape
    tr = _tile(T, 256)
    tc = _tile(F, 512)
    nb = tr // CONV_HALO

    def body(h_ref, p_ref, cw_ref, cb_ref, o_ref):
        i = pl.program_id(0)
        hc = []
        for part in range(2):
            prev = jnp.where(i == 0, 0.0, p_ref[part])
            X = jnp.concatenate([prev, h_ref[part]], axis=0)
            hc.append(_conv3(X, cw_ref[part], cb_ref[part])[CONV_HALO:])
        a, v = hc
        o_ref[...] = (a * _sigmoid(a) * v).astype(o_ref.dtype)

    return pl.pallas_call(
        body, name=name, grid=(T // tr, F // tc),
        in_specs=[pl.BlockSpec((2, tr, tc), lambda i, j: (0, i, j)),
                  pl.BlockSpec((2, CONV_HALO, tc), lambda i, j: (0, jnp.maximum(i * nb - 1, 0), j)),
                  pl.BlockSpec((2, 3, tc), lambda i, j: (0, 0, j)),
                  pl.BlockSpec((2, 1, tc), lambda i, j: (0, 0, j))],
        out_specs=pl.BlockSpec((tr, tc), lambda i, j: (i, j)), out_shape=_S((T, F), _MM),
        compiler_params=_params(("parallel", "parallel")),
    )(h, h, cw, cb)


def _ffn_mid_bwd(h, dact, cw, cb, name):
    _, T, F = h.shape
    tr = _tile(T, 256)
    tc = _tile(F, 512)
    nb = tr // CONV_HALO
    last_blk = T // CONV_HALO - 1
    n = tr + 2 * CONV_HALO

    def body(h_ref, p_ref, n_ref, d_ref, dn_ref, cw_ref, cb_ref, dh_ref, dcw_ref, dcb_ref):
        i = pl.program_id(1)
        is_first = i == 0
        is_last = i == pl.num_programs(1) - 1

        @pl.when(is_first)
        def _():
            dcw_ref[...] = jnp.zeros_like(dcw_ref)
            dcb_ref[...] = jnp.zeros_like(dcb_ref)

        X, hc = [], []
        for part in range(2):
            prev = jnp.where(is_first, 0.0, p_ref[part])
            nxt = jnp.where(is_last, 0.0, n_ref[part])
            Xp = jnp.concatenate([prev, h_ref[part], nxt], axis=0)
            X.append(Xp)
            hc.append(_conv3(Xp, cw_ref[part], cb_ref[part]))
        D = jnp.concatenate([jnp.zeros((CONV_HALO, tc), _F32), d_ref[...], jnp.where(is_last, 0.0, dn_ref[...])], axis=0)
        a, v = hc
        sg = _sigmoid(a)
        dhc = [D * v * sg * (1.0 + a * (1.0 - sg)), D * a * sg]
        lo, hi = CONV_HALO, CONV_HALO + tr
        for part in range(2):
            g = dhc[part]
            cwp = cw_ref[part]
            dh = cwp[2:3] * g + cwp[1:2] * pltpu.roll(g, n - 1, 0) + cwp[0:1] * pltpu.roll(g, n - 2, 0)
            dh_ref[part] = dh[lo:hi].astype(dh_ref.dtype)
            gt = g[lo:hi]
            dcw_ref[part, 2:3, :] += jnp.sum(gt * X[part][lo:hi], axis=0, keepdims=True)
            dcw_ref[part, 1:2, :] += jnp.sum(gt * pltpu.roll(X[part], 1, 0)[lo:hi], axis=0, keepdims=True)
            dcw_ref[part, 0:1, :] += jnp.sum(gt * pltpu.roll(X[part], 2, 0)[lo:hi], axis=0, keepdims=True)
            dcb_ref[part] += jnp.sum(gt, axis=0, keepdims=True)

    return pl.pallas_call(
        body, name=name, grid=(F // tc, T // tr),
        in_specs=[pl.BlockSpec((2, tr, tc), lambda j, i: (0, i, j)),
                  pl.BlockSpec((2, CONV_HALO, tc), lambda j, i: (0, jnp.maximum(i * nb - 1, 0), j)),
                  pl.BlockSpec((2, CONV_HALO, tc), lambda j, i: (0, jnp.minimum((i + 1) * nb, last_blk), j)),
                  pl.BlockSpec((tr, tc), lambda j, i: (i, j)),
                  pl.BlockSpec((CONV_HALO, tc), lambda j, i: (jnp.minimum((i + 1) * nb, last_blk), j)),
                  pl.BlockSpec((2, 3, tc), lambda j, i: (0, 0, j)),
                  pl.BlockSpec((2, 1, tc), lambda j, i: (0, 0, j))],
        out_specs=[pl.BlockSpec((2, tr, tc), lambda j, i: (0, i, j)),
                   pl.BlockSpec((2, 3, tc), lambda j, i: (0, 0, j)),
                   pl.BlockSpec((2, 1, tc), lambda j, i: (0, 0, j))],
        out_shape=[_S((2, T, F), _MM), _S((2, 3, F), _F32), _S((2, 1, F), _F32)],
        compiler_params=_params(("parallel", "arbitrary")),
    )(h, h, h, dact, dact, cw, cb)


def _ev_common(h_ref, hp_ref, lng_ref, lnb_ref, ws_ref, bias_ref, i, tr, W):
    H = W // A_HEAD
    u, gu = _gelu_and_grad(h_ref[0])
    v, gv = _gelu_and_grad(h_ref[1])
    mu = jnp.mean(v, axis=-1, keepdims=True)
    vc = v - mu
    rstd = lax.rsqrt(jnp.mean(vc * vc, axis=-1, keepdims=True) + LN_EPS)
    vhat = vc * rstd
    vb = (vhat * lng_ref[...] + lnb_ref[...]).astype(_MM)
    s_chunks = []
    for c in range(tr // A_CHUNK):
        r0 = c * A_CHUNK
        heads = [_dot(ws_ref[hd], vb[r0:r0 + A_CHUNK, hd * A_HEAD:(hd + 1) * A_HEAD]) for hd in range(H)]
        s_chunks.append(jnp.concatenate(heads, axis=1) + bias_ref[...])
    prev = jnp.where(i == 0, 0.0, hp_ref[...])
    X = jnp.concatenate([prev, h_ref[2]], axis=0)
    return u, gu, gv, rstd, vhat, vb, s_chunks, X


def _pool_inv_count(i, tr, rows, win):
    pos = i * tr + _row_index(rows) + 1
    return 1.0 / jnp.minimum(pos, win).astype(_F32)


def _pool_fwd(X, g, Wg, i, tr):
    xg = X[:, g * Wg:(g + 1) * Wg]
    s = xg
    for k in range(g + 1):
        s = s + pltpu.roll(s, 2 ** k, 0)
    return s[POOL_HALO:] * _pool_inv_count(i, tr, tr, 2 ** (g + 1)) - xg[POOL_HALO:]


def _ev_mid_fwd(h, lng, lnb, ws, bias, wp, sc, name):
    _, T, W = h.shape
    tr = _tile(T, 256)
    H = W // A_HEAD
    Wg = W // B_GROUPS
    nb = tr // POOL_HALO

    def body(h_ref, hp_ref, lng_ref, lnb_ref, ws_ref, bias_ref, wp_ref, sc_ref, o_ref):
        i = pl.program_id(0)
        u, _, _, _, _, _, s_chunks, X = _ev_common(h_ref, hp_ref, lng_ref, lnb_ref, ws_ref, bias_ref, i, tr, W)
        for c, s in enumerate(s_chunks):
            r0 = c * A_CHUNK
            o_ref[r0:r0 + A_CHUNK, 0:W] = (u[r0:r0 + A_CHUNK] * s).astype(o_ref.dtype)
        for g in range(B_GROUPS):
            p = _pool_fwd(X, g, Wg, i, tr)
            y = _dot(p.astype(_MM), wp_ref[g]) * sc_ref[:, g * Wg:(g + 1) * Wg]
            o_ref[:, W + g * Wg:W + (g + 1) * Wg] = y.astype(o_ref.dtype)

    vec = pl.BlockSpec((1, W), lambda i: (0, 0))
    return pl.pallas_call(
        body, name=name, grid=(T // tr,),
        in_specs=[pl.BlockSpec((3, tr, W), lambda i: (0, i, 0)),
                  pl.BlockSpec((None, POOL_HALO, W), lambda i: (2, jnp.maximum(i * nb - 1, 0), 0)),
                  vec, vec,
                  pl.BlockSpec((H, A_CHUNK, A_CHUNK), lambda i: (0, 0, 0)),
                  pl.BlockSpec((A_CHUNK, W), lambda i: (0, 0)),
                  pl.BlockSpec((B_GROUPS, Wg, Wg), lambda i: (0, 0, 0)),
                  vec],
        out_specs=pl.BlockSpec((tr, 2 * W), lambda i: (i, 0)), out_shape=_S((T, 2 * W), _MM),
        compiler_params=_params(("parallel",)),
    )(h, h, lng, lnb, ws, bias, wp, sc)


def _ev_mid_bwd(h, dy, lng, lnb, ws, bias, wp, sc, name):
    _, T, W = h.shape
    tr = _tile(T, 256)
    H = W // A_HEAD
    Wg = W // B_GROUPS
    nb = tr // POOL_HALO
    last_blk = T // POOL_HALO - 1
    n = tr + POOL_HALO

    def body(h_ref, hp_ref, dy_ref, dyn_ref, lng_ref, lnb_ref, ws_ref, bias_ref, wp_ref, sc_ref,
             dh_ref, dws_ref, dbias_ref, dlng_ref, dlnb_ref, dwp_ref, dsc_ref):
        i = pl.program_id(0)

        @pl.when(i == 0)
        def _():
            for r in (dws_ref, dbias_ref, dlng_ref, dlnb_ref, dwp_ref, dsc_ref):
                r[...] = jnp.zeros_like(r)

        u, gu, gv, rstd, vhat, vb, s_chunks, X = _ev_common(h_ref, hp_ref, lng_ref, lnb_ref, ws_ref, bias_ref, i, tr, W)
        rr = lax.broadcasted_iota(jnp.int32, (A_CHUNK, A_CHUNK), 0)
        cc = lax.broadcasted_iota(jnp.int32, (A_CHUNK, A_CHUNK), 1)
        tril = rr >= cc
        du_chunks, dvln_chunks = [], []
        for c, s in enumerate(s_chunks):
            r0 = c * A_CHUNK
            dya = dy_ref[r0:r0 + A_CHUNK, 0:W]
            du_chunks.append(dya * s)
            ds = dya * u[r0:r0 + A_CHUNK]
            dbias_ref[...] += ds
            dsb = ds.astype(_MM)
            heads = []
            for hd in range(H):
                cols = slice(hd * A_HEAD, (hd + 1) * A_HEAD)
                dws_ref[hd] += jnp.where(tril, _dot(dsb[:, cols], vb[r0:r0 + A_CHUNK, cols], _NT), 0.0)
                heads.append(_dot(ws_ref[hd], dsb[:, cols], _TN))
            dvln_chunks.append(jnp.concatenate(heads, axis=1))
        du = jnp.concatenate(du_chunks, axis=0)
        dvln = jnp.concatenate(dvln_chunks, axis=0)
        dlng_ref[...] += jnp.sum(dvln * vhat, axis=0, keepdims=True)
        dlnb_ref[...] += jnp.sum(dvln, axis=0, keepdims=True)
        dxh = dvln * lng_ref[...]
        dv = rstd * (dxh - jnp.mean(dxh, axis=-1, keepdims=True) - vhat * jnp.mean(dxh * vhat, axis=-1, keepdims=True))
        dh_ref[0] = (du * gu).astype(dh_ref.dtype)
        dh_ref[1] = (dv * gv).astype(dh_ref.dtype)

        dyb = dy_ref[:, W:2 * W]
        dyb_full = jnp.concatenate([dyb, jnp.where(i == pl.num_programs(0) - 1, 0.0, dyn_ref[...])], axis=0)
        for g in range(B_GROUPS):
            cols = slice(g * Wg, (g + 1) * Wg)
            pb = _pool_fwd(X, g, Wg, i, tr).astype(_MM)
            ypre = _dot(pb, wp_ref[g])
            dsc_ref[:, cols] += jnp.sum(dyb[:, cols] * ypre, axis=0, keepdims=True)
            dyp = (dyb_full[:, cols] * sc_ref[:, cols]).astype(_MM)
            dwp_ref[g] += _dot(pb, dyp[0:tr], _TN)
            dp = _dot(dyp, wp_ref[g], _NT)
            s = dp * _pool_inv_count(i, tr, n, 2 ** (g + 1))
            for k in range(g + 1):
                s = s + pltpu.roll(s, n - 2 ** k, 0)
            dh_ref[2, :, cols] = (s[0:tr] - dp[0:tr]).astype(dh_ref.dtype)

    vec = pl.BlockSpec((1, W), lambda i: (0, 0))
    ws_spec = pl.BlockSpec((H, A_CHUNK, A_CHUNK), lambda i: (0, 0, 0))
    bias_spec = pl.BlockSpec((A_CHUNK, W), lambda i: (0, 0))
    wp_spec = pl.BlockSpec((B_GROUPS, Wg, Wg), lambda i: (0, 0, 0))
    return pl.pallas_call(
        body, name=name, grid=(T // tr,),
        in_specs=[pl.BlockSpec((3, tr, W), lambda i: (0, i, 0)),
                  pl.BlockSpec((None, POOL_HALO, W), lambda i: (2, jnp.maximum(i * nb - 1, 0), 0)),
                  pl.BlockSpec((tr, 2 * W), lambda i: (i, 0)),
                  pl.BlockSpec((POOL_HALO, W), lambda i: (jnp.minimum((i + 1) * nb, last_blk), 1)),
                  vec, vec, ws_spec, bias_spec, wp_spec, vec],
        out_specs=[pl.BlockSpec((3, tr, W), lambda i: (0, i, 0)), ws_spec, bias_spec, vec, vec, wp_spec, vec],
        out_shape=[_S((3, T, W), _MM), _S((H, A_CHUNK, A_CHUNK), _F32), _S((A_CHUNK, W), _F32), _S((1, W), _F32),
                   _S((1, W), _F32), _S((B_GROUPS, Wg, Wg), _F32), _S((1, W), _F32)],
        compiler_params=_params(("arbitrary",)),
    )(h, h, dy, dy, lng, lnb, ws, bias, wp, sc)


def _chunk_cumsum(x, rin):
    s = 1
    while s < C_CHUNK:
        x = x + jnp.where(rin >= s, pltpu.roll(x, s, 0), 0.0)
        s *= 2
    return x


def _chunk_revcumsum(x, rin):
    n = x.shape[0]
    s = 1
    while s < C_CHUNK:
        x = x + jnp.where(rin + s < C_CHUNK, pltpu.roll(x, n - s, 0), 0.0)
        s *= 2
    return x


def _hgrn_gates(q, fl, lb, tr, tc):
    nch = tr // C_CHUNK
    sq = _sigmoid(q)
    sf = _sigmoid(fl)
    f = lb + (1.0 - lb) * sf
    logf = jnp.log(f)
    rin = _row_index(tr) % C_CHUNK
    b = _chunk_cumsum(logf, rin)
    tot3 = jnp.sum(logf.reshape(nch, C_CHUNK, tc), axis=1, keepdims=True)
    eb = jnp.exp(b)
    enb = jnp.exp(-b)
    ekb = jnp.exp(tot3 - b.reshape(nch, C_CHUNK, tc)).reshape(tr, tc)
    return sq, sf, f, rin, tot3, eb, enb, ekb


def _hgrn_prep_fwd(h, lb, name):
    _, T, D = h.shape
    tr = _tile(T, 512)
    tc = _tile(D, 512)
    nch = tr // C_CHUNK

    def body(q_ref, f_ref, v_ref, lb_ref, qd_ref, kd_ref, ke_ref, vb_ref, dec_ref):
        q = q_ref[...]
        sq, _, f, _, tot3, eb, enb, ekb = _hgrn_gates(q, f_ref[...], lb_ref[...], tr, tc)
        kk = 1.0 - f
        qd_ref[...] = (q * sq * eb).astype(qd_ref.dtype)
        kd_ref[...] = (kk * enb).astype(kd_ref.dtype)
        ke_ref[...] = (kk * ekb).astype(ke_ref.dtype)
        vb_ref[...] = v_ref[...].astype(vb_ref.dtype)
        dec_ref[...] = jnp.exp(tot3).reshape(nch, tc)

    def part(p):
        return pl.BlockSpec((None, tr, tc), lambda i, j: (p, i, j))

    blk = pl.BlockSpec((tr, tc), lambda i, j: (i, j))
    return pl.pallas_call(
        body, name=name, grid=(T // tr, D // tc),
        in_specs=[part(0), part(1), part(2), pl.BlockSpec((1, tc), lambda i, j: (0, j))],
        out_specs=[blk, blk, blk, blk, pl.BlockSpec((nch, tc), lambda i, j: (i, j))],
        out_shape=[_S((T, D), _MM)] * 4 + [_S((T // C_CHUNK, D), _F32)],
        compiler_params=_params(("parallel", "parallel")),
    )(h, h, h, lb)


def _tril_mask():
    rr = lax.broadcasted_iota(jnp.int32, (C_CHUNK, C_CHUNK), 0)
    cc = lax.broadcasted_iota(jnp.int32, (C_CHUNK, C_CHUNK), 1)
    return rr >= cc


def _hgrn_scan_fwd(qd, kd, ke, vb, dec, h, ng, name):
    T, D = qd.shape
    NH = D // C_HEAD
    N = T // C_CHUNK

    def body(qd_ref, kd_ref, ke_ref, vb_ref, dec_ref, g_ref, ng_ref, o_ref, y_ref, st_ref):
        mask = _tril_mask()

        def step(n, St):
            r = pl.ds(pl.multiple_of(n * C_CHUNK, C_CHUNK), C_CHUNK)
            Qd, Kd, Ke, V = qd_ref[r, :], kd_ref[r, :], ke_ref[r, :], vb_ref[r, :]
            att = jnp.where(mask, _dot(Qd, Kd, _NT), 0.0).astype(_MM)
            o_ref[r, :] = _dot(att, V) + _dot(Qd, St.astype(_MM), _NT)
            st_ref[n] = St
            return St * dec_ref[pl.ds(n, 1), :] + _dot(V, Ke, _TN)

        lax.fori_loop(0, N, step, jnp.zeros((C_HEAD, C_HEAD), _F32))
        o = o_ref[...]
        r = lax.rsqrt(jnp.mean(o * o, axis=-1, keepdims=True) + LN_EPS)
        y_ref[...] = (o * r * ng_ref[...] * _sigmoid(g_ref[...])).astype(y_ref.dtype)

    col = pl.BlockSpec((T, C_HEAD), lambda j: (0, j))
    return pl.pallas_call(
        body, name=name, grid=(NH,),
        in_specs=[col, col, col, col, pl.BlockSpec((N, C_HEAD), lambda j: (0, j)),
                  pl.BlockSpec((None, T, C_HEAD), lambda j: (3, 0, j)), pl.BlockSpec((1, C_HEAD), lambda j: (0, j))],
        out_specs=[col, col, pl.BlockSpec((None, N, C_HEAD, C_HEAD), lambda j: (j, 0, 0, 0))],
        out_shape=[_S((T, D), _F32), _S((T, D), _MM), _S((NH, N, C_HEAD, C_HEAD), _F32)],
        compiler_params=_params(("parallel",)),
    )(qd, kd, ke, vb, dec, h, ng)


def _hgrn_scan_bwd(qd, kd, ke, vb, dec, st, o, h, ng, dy, name):
    T, D = qd.shape
    NH = D // C_HEAD
    N = T // C_CHUNK

    def body(qd_ref, kd_ref, ke_ref, vb_ref, dec_ref, st_ref, o_ref, g_ref, ng_ref, dy_ref,
             dqd_ref, dkd_ref, dke_ref, dv_ref, dgate_ref, ddec_ref, dng_ref, do_s):
        o = o_ref[...]
        r = lax.rsqrt(jnp.mean(o * o, axis=-1, keepdims=True) + LN_EPS)
        oh = o * r
        gn = ng_ref[...]
        sg = _sigmoid(g_ref[...])
        d = dy_ref[...]
        dyn = d * sg
        dgate_ref[...] = (d * oh * gn * sg * (1.0 - sg)).astype(dgate_ref.dtype)
        dng_ref[...] = jnp.sum(dyn * oh, axis=0, keepdims=True)
        doh = dyn * gn
        do_s[...] = (r * (doh - oh * jnp.mean(doh * oh, axis=-1, keepdims=True))).astype(do_s.dtype)
        mask = _tril_mask()

        def step(k, dSt):
            n = N - 1 - k
            rws = pl.ds(pl.multiple_of(n * C_CHUNK, C_CHUNK), C_CHUNK)
            Qd, Kd, Ke, V, dO = qd_ref[rws, :], kd_ref[rws, :], ke_ref[rws, :], vb_ref[rws, :], do_s[rws, :]
            St = st_ref[n]
            Stb = St.astype(_MM)
            dStb = dSt.astype(_MM)
            att = jnp.where(mask, _dot(Qd, Kd, _NT), 0.0).astype(_MM)
            dA = jnp.where(mask, _dot(dO, V, _NT), 0.0).astype(_MM)
            dv_ref[rws, :] = (_dot(att, dO, _TN) + _dot(Ke, dStb, _NT)).astype(dv_ref.dtype)
            dqd_ref[rws, :] = _dot(dA, Kd) + _dot(dO, Stb)
            dkd_ref[rws, :] = _dot(dA, Qd, _TN)
            dke_ref[rws, :] = _dot(V, dStb)
            ddec_ref[pl.ds(n, 1), :] = jnp.sum(dSt * St, axis=0, keepdims=True)
            return dSt * dec_ref[pl.ds(n, 1), :] + _dot(dO, Qd, _TN)

        lax.fori_loop(0, N, step, jnp.zeros((C_HEAD, C_HEAD), _F32))

    col = pl.BlockSpec((T, C_HEAD), lambda j: (0, j))
    chk = pl.BlockSpec((N, C_HEAD), lambda j: (0, j))
    one = pl.BlockSpec((1, C_HEAD), lambda j: (0, j))
    return pl.pallas_call(
        body, name=name, grid=(NH,),
        in_specs=[col, col, col, col, chk, pl.BlockSpec((None, N, C_HEAD, C_HEAD), lambda j: (j, 0, 0, 0)), col,
                  pl.BlockSpec((None, T, C_HEAD), lambda j: (3, 0, j)), one, col],
        out_specs=[col, col, col, col, col, chk, one],
        out_shape=[_S((T, D), _F32)] * 3 + [_S((T, D), _MM)] * 2 + [_S((N, D), _F32), _S((1, D), _F32)],
        scratch_shapes=[pltpu.VMEM((T, C_HEAD), _MM)],
        compiler_params=_params(("parallel",)),
    )(qd, kd, ke, vb, dec, st, o, h, ng, dy)


def _hgrn_prep_bwd(h, lb, dqd, dkd, dke, dv, dgate, ddec, name):
    _, T, D = h.shape
    tr = _tile(T, 512)
    tc = _tile(D, 256)
    nch = tr // C_CHUNK

    def body(q_ref, f_ref, lb_ref, dqd_ref, dkd_ref, dke_ref, dv_ref, dgate_ref, ddec_ref, dh_ref, dlb_ref):
        @pl.when(pl.program_id(1) == 0)
        def _():
            dlb_ref[...] = jnp.zeros_like(dlb_ref)

        q = q_ref[...]
        lb = lb_ref[...]
        sq, sf, f, rin, tot3, eb, enb, ekb = _hgrn_gates(q, f_ref[...], lb, tr, tc)
        kk = 1.0 - f
        dQd, dKd, dKe = dqd_ref[...], dkd_ref[...], dke_ref[...]
        tq = dQd * eb
        tkd = dKd * enb
        tke = dKe * ekb
        ke_term = tke * kk
        db = tq * (q * sq) - tkd * kk - ke_term
        dtot3 = (jnp.sum(ke_term.reshape(nch, C_CHUNK, tc), axis=1, keepdims=True)
                 + (ddec_ref[...] * jnp.exp(tot3).reshape(nch, tc)).reshape(nch, 1, tc))
        dlogf = (_chunk_revcumsum(db, rin).reshape(nch, C_CHUNK, tc) + dtot3).reshape(tr, tc)
        df = dlogf / f - (tkd + tke)
        dh_ref[0] = (tq * sq * (1.0 + q * (1.0 - sq))).astype(dh_ref.dtype)
        dh_ref[1] = (df * (1.0 - lb) * sf * (1.0 - sf)).astype(dh_ref.dtype)
        dh_ref[2] = dv_ref[...]
        dh_ref[3] = dgate_ref[...]
        dlb_ref[...] += jnp.sum(df * (1.0 - sf), axis=0, keepdims=True)

    def part(p):
        return pl.BlockSpec((None, tr, tc), lambda j, i: (p, i, j))

    blk = pl.BlockSpec((tr, tc), lambda j, i: (i, j))
    vec = pl.BlockSpec((1, tc), lambda j, i: (0, j))
    return pl.pallas_call(
        body, name=name, grid=(D // tc, T // tr),
        in_specs=[part(0), part(1), vec, blk, blk, blk, blk, blk, pl.BlockSpec((nch, tc), lambda j, i: (i, j))],
        out_specs=[pl.BlockSpec((4, tr, tc), lambda j, i: (0, i, j)), vec],
        out_shape=[_S((4, T, D), _MM), _S((1, D), _F32)],
        compiler_params=_params(("parallel", "arbitrary")),
    )(h, h, lb, dqd, dkd, dke, dv, dgate, ddec)


def _sum_in_device_order(me1, own, land, name):
    R, C = own.shape
    tr = _tile(R, 256)

    def body(me_ref, own_ref, land_ref, o_ref):
        me = me_ref[0]
        g = None
        for j in range(N_DEV):
            slot = jnp.maximum(jnp.bitwise_xor(me, j) - 1, 0)
            p = jnp.where(me == j, own_ref[...], land_ref[slot])
            g = p if g is None else g + p
        o_ref[...] = g

    return pl.pallas_call(
        body, name=name,
        grid_spec=pltpu.PrefetchScalarGridSpec(
            num_scalar_prefetch=1, grid=(R // tr,),
            in_specs=[pl.BlockSpec((tr, C), lambda i, me: (i, 0)), pl.BlockSpec((N_DEV - 1, tr, C), lambda i, me: (0, i, 0))],
            out_specs=pl.BlockSpec((tr, C), lambda i, me: (i, 0))),
        out_shape=_S((R, C), _F32), compiler_params=_params(("parallel",)),
    )(me1, own, land)


def _adamw(parts, w, m, v, name):
    P, R, C = parts.shape
    tr = _tile(R, 128) if R % LANE == 0 else R

    def body(p_ref, w_ref, m_ref, v_ref, g_ref, d_ref, nm_ref, nv_ref):
        g = p_ref[0].astype(_F32)
        for s in range(1, P):
            g = g + p_ref[s].astype(_F32)
        nm = ADAM_B1 * m_ref[...] + (1.0 - ADAM_B1) * g
        nv = ADAM_B2 * v_ref[...] + (1.0 - ADAM_B2) * (g * g)
        m_hat = nm / (1.0 - ADAM_B1 ** ADAM_STEP)
        v_hat = nv / (1.0 - ADAM_B2 ** ADAM_STEP)
        g_ref[...] = g
        d_ref[...] = -ADAM_LR * (m_hat / (jnp.sqrt(v_hat) + ADAM_EPS) + ADAM_WD * w_ref[...])
        nm_ref[...] = nm
        nv_ref[...] = nv

    blk = pl.BlockSpec((tr, C), lambda i: (i, 0))
    return pl.pallas_call(
        body, name=name, grid=(R // tr,), in_specs=[pl.BlockSpec((P, tr, C), lambda i: (0, i, 0)), blk, blk, blk],
        out_specs=[blk] * 4, out_shape=[_S((R, C), _F32)] * 4, compiler_params=_params(("parallel",)),
    )(parts, w, m, v)


def _exchange(name, srcs, out_shapes, jobs, deps=()):
    ns, nj = len(srcs), len(jobs)

    nd = len(deps)

    def body(*refs):
        ins, outs = refs[:ns], refs[ns + nd:ns + nd + len(out_shapes)]
        send_sems, recv_sems, local_sems = refs[-3:]
        x, y, c = lax.axis_index("x"), lax.axis_index("y"), lax.axis_index("c")
        me = 4 * x + 2 * y + c
        local = []
        for ji, (si, src_fn, di, dst_fn) in enumerate(jobs):
            cp = pltpu.make_async_copy(src_fn(ins[si], me, me), dst_fn(outs[di], me), local_sems.at[ji])
            cp.start()
            local.append(cp)
        remote = []
        for k in range(1, N_DEV):
            px, py, pc = (x + (k >> 2)) % 2, (y + ((k >> 1) & 1)) % 2, (c + (k & 1)) % 2
            to = 4 * px + 2 * py + pc
            for ji, (si, src_fn, di, dst_fn) in enumerate(jobs):
                sem = (k - 1) * nj + ji
                cp = pltpu.make_async_remote_copy(
                    src_ref=src_fn(ins[si], me, to), dst_ref=dst_fn(outs[di], me),
                    send_sem=send_sems.at[sem], recv_sem=recv_sems.at[sem],
                    device_id=(px, py, pc), device_id_type=pl.DeviceIdType.MESH)
                cp.start()
                remote.append(cp)
        for cp in remote:
            cp.wait_recv()
        for cp in remote:
            cp.wait_send()
        for cp in local:
            cp.wait()

    hbm = pl.BlockSpec(memory_space=pltpu.HBM)
    return pl.pallas_call(
        body, name=name, in_specs=[hbm] * ns + [_ANY] * nd, out_specs=[hbm] * len(out_shapes), out_shape=list(out_shapes),
        scratch_shapes=[pltpu.SemaphoreType.DMA(((N_DEV - 1) * nj,)), pltpu.SemaphoreType.DMA(((N_DEV - 1) * nj,)),
                        pltpu.SemaphoreType.DMA((nj,))],
    )(*srcs, *deps)


def _whole(ref, me, to):
    return ref


def _slot_job(i, o):
    def dst(ref, me):
        return ref.at[me]
    return (i, _whole, o, dst)


_HBM = pl.BlockSpec(memory_space=pltpu.HBM)
_SEM = pl.BlockSpec(memory_space=pltpu.SEMAPHORE)
_ANY = pl.BlockSpec(memory_space=pl.ANY)
_N_PEER = N_DEV - 1


def _split_params():
    return pltpu.CompilerParams(has_side_effects=pltpu.SideEffectType.DATAFLOW_SIDE_EFFECTING)


def _blk(ref, axis, n, idx):
    if axis is None:
        return ref
    return ref.at[tuple([slice(None)] * axis + [pl.ds(pl.multiple_of(idx * n, n), n)])]


def _peer(k):
    x, y, c = lax.axis_index("x"), lax.axis_index("y"), lax.axis_index("c")
    px, py, pc = (x + (k >> 2)) % 2, (y + ((k >> 1) & 1)) % 2, (c + (k & 1)) % 2
    return (px, py, pc), 4 * px + 2 * py + pc, 4 * x + 2 * y + c


def _row_tile(rows, pref):
    best = None
    for d in range(16, min(rows, pref) + 1, 16):
        if rows % d == 0:
            best = d
    return best if best is not None else rows


def _place(w, me1, axis, name, layer=None, deps=()):
    R, C = w.shape[-2:]
    tr = _row_tile(R, 512)
    nb = R // tr
    lead = () if layer is None else (None,)
    pre = () if layer is None else (layer,)

    def body(me_ref, w_ref, *rest):
        rest[-1][...] = w_ref[...].astype(rest[-1].dtype)

    if axis == 1:
        out_spec = pl.BlockSpec((tr, C), lambda i, me: (i, me[0]))
        out_shape = _S((R, N_DEV * C), _MM)
    else:
        out_spec = pl.BlockSpec((tr, C), lambda i, me: (me[0] * nb + i, 0))
        out_shape = _S((N_DEV * R, C), _MM)
    return pl.pallas_call(
        body, name=name,
        grid_spec=pltpu.PrefetchScalarGridSpec(
            num_scalar_prefetch=1, grid=(nb,),
            in_specs=[pl.BlockSpec(lead + (tr, C), lambda i, me: pre + (i, 0))] + [_ANY] * len(deps), out_specs=out_spec),
        out_shape=out_shape, compiler_params=_params(("parallel",)),
    )(me1, w, *deps)


_SIBLING = 1
_CHIPS = (2, 4, 6)
_VMEM_TOKEN = pl.BlockSpec(memory_space=pltpu.VMEM)


def _remote(ref_blk, send_sem, recv_sem, dev):
    return pltpu.make_async_remote_copy(src_ref=ref_blk, dst_ref=ref_blk, send_sem=send_sem, recv_sem=recv_sem,
                                        device_id=dev, device_id_type=pl.DeviceIdType.MESH)


def _gather_start(name, full, axis, n):
    def body(f_ref, send, recv, f_out, token):
        for i, k in enumerate((_SIBLING,) + _CHIPS):
            dev, _, me = _peer(k)
            _remote(_blk(f_ref, axis, n, me), send.at[i], recv.at[i], dev).start()
        token[...] = jnp.zeros_like(token)

    return pl.pallas_call(
        body, name=name,
        out_shape=(pltpu.SemaphoreType.DMA((4,)), pltpu.SemaphoreType.DMA((4,)), pltpu.HBM(full.shape, full.dtype),
                   _S((8, LANE), _F32)),
        in_specs=(_HBM,), out_specs=(_SEM, _SEM, _HBM, _VMEM_TOKEN),
        input_output_aliases={0: 2}, compiler_params=_split_params(),
    )(pltpu.with_memory_space_constraint(full, pltpu.HBM))


def _gather_forward(name, full, axis, n, recv, after):
    def body(f_ref, recv_r, after_ref, send2, recv2, f_out, token):
        sib, _, _ = _peer(_SIBLING)
        for i, k in enumerate(_CHIPS):
            dev, frm, _ = _peer(k)
            blk = _blk(f_ref, axis, n, frm)
            _remote(blk, send2.at[i], recv_r.at[1 + i], dev).wait_recv()
            _remote(blk, send2.at[i], recv2.at[i], sib).start()
        token[...] = jnp.zeros_like(token)

    return pl.pallas_call(
        body, name=name,
        out_shape=(pltpu.SemaphoreType.DMA((3,)), pltpu.SemaphoreType.DMA((3,)), pltpu.HBM(full.shape, full.dtype),
                   _S((8, LANE), _F32)),
        in_specs=(_HBM, _SEM, _ANY), out_specs=(_SEM, _SEM, _HBM, _VMEM_TOKEN),
        input_output_aliases={0: 2}, compiler_params=_split_params(),
    )(full, recv, after)


def _gather_wait(name, full, axis, n, send, recv, send2, recv2, after):
    def body(f_ref, send_r, recv_r, send2_r, recv2_r, after_ref, f_out):
        sib, _, me = _peer(_SIBLING)
        blk = _blk(f_ref, axis, n, me)
        for i in range(4):
            _remote(blk, send_r.at[i], recv_r.at[0], sib).wait_send()
        _remote(blk, send_r.at[0], recv_r.at[0], sib).wait_recv()
        for i in range(3):
            cp = _remote(blk, send2_r.at[i], recv2_r.at[i], sib)
            cp.wait_send()
            cp.wait_recv()

    return pl.pallas_call(
        body, name=name, out_shape=pltpu.HBM(full.shape, full.dtype),
        in_specs=(_HBM, _SEM, _SEM, _SEM, _SEM, _ANY), out_specs=_HBM,
        input_output_aliases={0: 0}, compiler_params=_split_params(),
    )(full, send, recv, send2, recv2, after)


def _scatter_start(name, dw, axis, n):
    shard = tuple(n if a == axis else d for a, d in enumerate(dw.shape))
    land = lax.empty((_N_PEER,) + shard, dw.dtype)

    def body(dw_ref, land_ref, send, recv, dw_out, land_out, token):
        for k in range(1, N_DEV):
            dev, to, _ = _peer(k)
            pltpu.make_async_remote_copy(
                src_ref=_blk(dw_ref, axis, n, to), dst_ref=land_ref.at[k - 1], send_sem=send.at[k - 1],
                recv_sem=recv.at[k - 1], device_id=dev, device_id_type=pl.DeviceIdType.MESH).start()
        token[...] = jnp.zeros_like(token)

    return pl.pallas_call(
        body, name=name,
        out_shape=(pltpu.SemaphoreType.DMA((_N_PEER,)), pltpu.SemaphoreType.DMA((_N_PEER,)),
                   pltpu.HBM(dw.shape, dw.dtype), pltpu.HBM(land.shape, land.dtype), _S((8, LANE), _F32)),
        in_specs=(_HBM, _HBM), out_specs=(_SEM, _SEM, _HBM, _HBM, pl.BlockSpec(memory_space=pltpu.VMEM)),
        input_output_aliases={0: 2, 1: 3}, compiler_params=_split_params(),
    )(pltpu.with_memory_space_constraint(dw, pltpu.HBM), pltpu.with_memory_space_constraint(land, pltpu.HBM))


def _scatter_wait(name, dw, land, send, recv, axis, n, after):
    def body(dw_ref, land_ref, send_r, recv_r, after_ref, dw_out, land_out):
        for k in range(1, N_DEV):
            dev, to, _ = _peer(k)
            cp = pltpu.make_async_remote_copy(
                src_ref=_blk(dw_ref, axis, n, to), dst_ref=land_ref.at[k - 1], send_sem=send_r.at[k - 1],
                recv_sem=recv_r.at[k - 1], device_id=dev, device_id_type=pl.DeviceIdType.MESH)
            cp.wait_send()
            cp.wait_recv()

    return pl.pallas_call(
        body, name=name, out_shape=(pltpu.HBM(dw.shape, dw.dtype), pltpu.HBM(land.shape, land.dtype)),
        in_specs=(_HBM, _HBM, _SEM, _SEM, _ANY), out_specs=(_HBM, _HBM), input_output_aliases={0: 0, 1: 1},
        compiler_params=_split_params(),
    )(dw, land, send, recv, after)


def _adamw_big(me1, dw, land, w, m, v, axis, n, name, layer=None, into=None):
    R, C = land.shape[1:]
    tr = _row_tile(R, 128)
    nb = R // tr
    lead = () if layer is None else (None,)
    pre = () if layer is None else (layer,)

    def body(me_ref, own_ref, land_ref, w_ref, m_ref, v_ref, *rest):
        g_ref, d_ref, nm_ref, nv_ref = rest[-4:]
        g = own_ref[...].astype(_F32)
        for s in range(_N_PEER):
            g = g + land_ref[s].astype(_F32)
        nm = ADAM_B1 * m_ref[...] + (1.0 - ADAM_B1) * g
        nv = ADAM_B2 * v_ref[...] + (1.0 - ADAM_B2) * (g * g)
        m_hat = nm / (1.0 - ADAM_B1 ** ADAM_STEP)
        v_hat = nv / (1.0 - ADAM_B2 ** ADAM_STEP)
        g_ref[...] = g
        d_ref[...] = -ADAM_LR * (m_hat / (jnp.sqrt(v_hat) + ADAM_EPS) + ADAM_WD * w_ref[...])
        nm_ref[...] = nm
        nv_ref[...] = nv

    if axis == 1:
        own_spec = pl.BlockSpec((tr, C), lambda i, me: (i, me[0]))
    else:
        own_spec = pl.BlockSpec((tr, C), lambda i, me: (me[0] * nb + i, 0))
    blk = pl.BlockSpec(lead + (tr, C), lambda i, me: pre + (i, 0))
    in_specs = [own_spec, pl.BlockSpec((_N_PEER, tr, C), lambda i, me: (0, i, 0)), blk, blk, blk]
    args = [me1, dw, land, w, m, v]
    aliases = {}
    if into is not None:
        in_specs += [_ANY] * 4
        aliases = {6 + j: j for j in range(4)}
        args += list(into)
    return pl.pallas_call(
        body, name=name,
        grid_spec=pltpu.PrefetchScalarGridSpec(num_scalar_prefetch=1, grid=(nb,), in_specs=in_specs, out_specs=[blk] * 4),
        out_shape=[_S(w.shape, _F32)] * 4, input_output_aliases=aliases, compiler_params=_params(("parallel",)),
    )(*args)


def _pack(arrs):
    flat = jnp.concatenate([a.reshape(-1).astype(_F32) for a in arrs])
    pad = (-flat.shape[0]) % (LANE * LANE)
    return jnp.pad(flat, (0, pad)).reshape(-1, LANE)


def _unpack(mat, shapes):
    flat = mat.reshape(-1)
    out, off = [], 0
    for s in shapes:
        n = 1
        for d in s:
            n *= d
        out.append(flat[off:off + n].reshape(s))
        off += n
    return out


def _lb_of(lb_param):
    lb_all = jnp.cumsum(jax.nn.softmax(lb_param.astype(_F32), axis=0), axis=0)
    return (lb_all - lb_all[0])[1:2]


def kernel(x, ev_w_in, ev_ln_v_g, ev_ln_v_b, ev_w_s, ev_b_s, ev_w_pool, ev_pool_scale, ev_w_out, od_w_in, od_norm_g, od_w_out, lb_param, ffn_w_up, ffn_conv_w, ffn_conv_b, ffn_w_down, ln1_g, ln1_b, ln2_g, ln2_b, loss_target, m_ev_w_in, m_ev_ln_v_g, m_ev_ln_v_b, m_ev_w_s, m_ev_b_s, m_ev_w_pool, m_ev_pool_scale, m_ev_w_out, m_od_w_in, m_od_norm_g, m_od_w_out, m_lb_param, m_ffn_w_up, m_ffn_conv_w, m_ffn_conv_b, m_ffn_w_down, m_ln1_g, m_ln1_b, m_ln2_g, m_ln2_b, v_ev_w_in, v_ev_ln_v_g, v_ev_ln_v_b, v_ev_w_s, v_ev_b_s, v_ev_w_pool, v_ev_pool_scale, v_ev_w_out, v_od_w_in, v_od_norm_g, v_od_w_out, v_lb_param, v_ffn_w_up, v_ffn_conv_w, v_ffn_conv_b, v_ffn_w_down, v_ln1_g, v_ln1_b, v_ln2_g, v_ln2_b):
    me = 4 * lax.axis_index("x") + 2 * lax.axis_index("y") + lax.axis_index("c")
    T, D = x.shape[1], x.shape[2]
    W = ev_ln_v_g.shape[1]
    H = W // A_HEAD
    Wg = W // B_GROUPS
    F2 = ffn_conv_b.shape[1]
    F = F2 // 2
    n_in0, n_out0 = ev_w_in.shape[2], ev_w_out.shape[1]
    n_in1, n_out1 = od_w_in.shape[2], od_w_out.shape[1]
    n_up, n_dn = ffn_w_up.shape[2], ffn_w_down.shape[1]
    n_pool, n_ng, n_cw = ev_w_pool.shape[2], od_norm_g.shape[1], ffn_conv_w.shape[2]

    small_shards = [od_norm_g, ffn_conv_w, ev_w_pool]
    small_pack = _pack(small_shards)
    small_all = _exchange("gather_small_params", [small_pack], [_S((N_DEV,) + small_pack.shape, _F32)], [_slot_job(0, 0)])[0]

    me1 = me.astype(jnp.int32).reshape(1)
    weights = [
        ("w_in0", ev_w_in[0], None, 1, n_in0), ("w_out0", ev_w_out[0], None, 0, n_out0),
        ("w_up0", ffn_w_up, 0, 1, n_up), ("w_dn0", ffn_w_down, 0, 0, n_dn),
        ("w_in1", od_w_in[0], None, 1, n_in1), ("w_out1", od_w_out[0], None, 0, n_out1),
        ("w_up1", ffn_w_up, 1, 1, n_up), ("w_dn1", ffn_w_down, 1, 0, n_dn),
    ]
    started, tokens = {}, [small_all]
    for key, w, layer, axis, n in weights:
        full = _place(w, me1, axis, "place_" + key, layer, deps=tokens)
        send, recv, full, token = _gather_start("gather_start_" + key, full, axis, n)
        started[key] = (full, axis, n, send, recv)
        tokens = [token]

    def pass_on(key, after):
        full, axis, n, send, recv = started[key]
        send2, recv2, full, token = _gather_forward("gather_forward_" + key, full, axis, n, recv, after)
        started[key] = (full, axis, n, send, recv, send2, recv2)
        return token

    def gathered(key, after):
        return _gather_wait("gather_wait_" + key, *started[key], after)

    ng_parts, cw_parts, wp_parts = [], [], []
    for j in range(N_DEV):
        a, b, c = _unpack(small_all[j], [s.shape for s in small_shards])
        ng_parts.append(a)
        cw_parts.append(b)
        wp_parts.append(c)
    norm_g = jnp.concatenate(ng_parts, axis=1)
    conv_w = jnp.concatenate(cw_parts, axis=2)
    w_pool = jnp.concatenate(wp_parts, axis=2)[0]
    cw_l = [conv_w[l].reshape(3, 2, F).transpose(1, 0, 2) for l in range(DEPTH)]
    cb_l = [ffn_conv_b[l].reshape(2, 1, F) for l in range(DEPTH)]
    ws_tril = jnp.tril(ev_w_s[0]).astype(_MM)
    bias = jnp.repeat(ev_b_s[0].T, A_HEAD, axis=1)
    wp_b = w_pool.astype(_MM)
    lb, lb_vjp = jax.vjp(_lb_of, lb_param)

    x2 = x[0]
    xb = _cast(x2, _MM, "cast_x", deps=[pass_on("w_in0", tokens[0])])
    w_in0 = gathered("w_in0", xb)
    h0 = _mm(xb, w_in0, "nn", _F32, "ev_in", out_parts=3)
    tie = pass_on("w_out0", h0)
    yab = _ev_mid_fwd(h0, ev_ln_v_g + tie[0, 0], ev_ln_v_b, ws_tril, bias, wp_b, ev_pool_scale, "ev_mid_fwd")
    w_out0 = gathered("w_out0", yab)
    z1 = _mm(yab, w_out0, "nn", _F32, "ev_out", add=x2, add_scale=ALPHA)
    tie = pass_on("w_up0", z1)
    x1, x1b = _ln_fwd(z1, ln1_g[0:1] + tie[0, 0], ln1_b[0:1], "ln1_0")
    w_up0 = gathered("w_up0", x1b)
    hf0 = _mm(x1b, w_up0, "nn", _F32, "ffn_up", out_parts=2)
    tie = pass_on("w_dn0", hf0)
    act0 = _ffn_mid_fwd(hf0, cw_l[0], cb_l[0] + tie[0, 0], "ffn_mid_fwd")
    w_dn0 = gathered("w_dn0", act0)
    z2 = _mm(act0, w_dn0, "nn", _F32, "ffn_down", add=x1, add_scale=ALPHA)
    tie = pass_on("w_in1", z2)
    x2_, x2b = _ln_fwd(z2, ln2_g[0:1] + tie[0, 0], ln2_b[0:1], "ln2_0")
    w_in1 = gathered("w_in1", x2b)
    h1 = _mm(x2b, w_in1, "nn", _F32, "od_in", out_parts=4)
    qd, kd, ke, vb, dec = _hgrn_prep_fwd(h1, lb, "hgrn_prep_fwd")
    tie = pass_on("w_out1", qd)
    o, yo, st = _hgrn_scan_fwd(qd, kd, ke, vb, dec, h1, norm_g + tie[0, 0], "hgrn_scan_fwd")
    w_out1 = gathered("w_out1", yo)
    z3 = _mm(yo, w_out1, "nn", _F32, "od_out", add=x2_, add_scale=ALPHA)
    tie = pass_on("w_up1", z3)
    x3, x3b = _ln_fwd(z3, ln1_g[1:2] + tie[0, 0], ln1_b[1:2], "ln1_1")
    w_up1 = gathered("w_up1", x3b)
    hf1 = _mm(x3b, w_up1, "nn", _F32, "ffn_up", out_parts=2)
    tie = pass_on("w_dn1", hf1)
    act1 = _ffn_mid_fwd(hf1, cw_l[1], cb_l[1] + tie[0, 0], "ffn_mid_fwd")
    w_dn1 = gathered("w_dn1", act1)
    z4 = _mm(act1, w_dn1, "nn", _F32, "ffn_down", add=x3, add_scale=ALPHA)

    scat = {}

    def scatter(key, dw, axis, n):
        send, recv, dw, land, token = _scatter_start("scatter_start_" + key, dw, axis, n)
        scat[key] = (dw, land, send, recv, axis, n)
        return [token]

    loss11, dz4, dz4b, g_ln2_1, b_ln2_1 = _ln_loss_bwd(z4, ln2_g[1:2], ln2_b[1:2], loss_target[0], "ln_loss_bwd")
    tok = scatter("dn1", _mm(act1, dz4b, "tn", _XCH, "ffn_down_dw"), 0, n_dn)
    dact1 = _mm(dz4b, w_dn1, "nt", _F32, "ffn_down_dx", deps=tok)
    dhf1, dcw1, dcb1 = _ffn_mid_bwd(hf1, dact1, cw_l[1], cb_l[1], "ffn_mid_bwd")
    tok = scatter("up1", _mm(x3b, dhf1, "tn", _XCH, "ffn_up_dw", b_parts=2, deps=tok), 1, n_up)
    dx3 = _mm(dhf1, w_up1, "nt", _F32, "ffn_up_dx", a_parts=2, add=dz4, add_scale=ALPHA, deps=tok)
    dz3, dz3b, g_ln1_1, b_ln1_1 = _ln_bwd(z3, ln1_g[1:2], dx3, "ln_bwd")
    tok = scatter("out1", _mm(yo, dz3b, "tn", _XCH, "od_out_dw", deps=tok), 0, n_out1)
    dyo = _mm(dz3b, w_out1, "nt", _F32, "od_out_dx", deps=tok)
    dqd, dkd, dke, dv, dgate, ddec, dng = _hgrn_scan_bwd(qd, kd, ke, vb, dec, st, o, h1, norm_g, dyo, "hgrn_scan_bwd")
    dh1, dlb = _hgrn_prep_bwd(h1, lb, dqd, dkd, dke, dv, dgate, ddec, "hgrn_prep_bwd")
    tok = scatter("in1", _mm(x2b, dh1, "tn", _XCH, "od_in_dw", b_parts=4, deps=tok), 1, n_in1)
    dx2 = _mm(dh1, w_in1, "nt", _F32, "od_in_dx", a_parts=4, add=dz3, add_scale=ALPHA, deps=tok)
    dz2, dz2b, g_ln2_0, b_ln2_0 = _ln_bwd(z2, ln2_g[0:1], dx2, "ln_bwd")
    tok = scatter("dn0", _mm(act0, dz2b, "tn", _XCH, "ffn_down_dw", deps=tok), 0, n_dn)
    dact0 = _mm(dz2b, w_dn0, "nt", _F32, "ffn_down_dx", deps=tok)
    dhf0, dcw0, dcb0 = _ffn_mid_bwd(hf0, dact0, cw_l[0], cb_l[0], "ffn_mid_bwd")
    tok = scatter("up0", _mm(x1b, dhf0, "tn", _XCH, "ffn_up_dw", b_parts=2, deps=tok), 1, n_up)
    dx1 = _mm(dhf0, w_up0, "nt", _F32, "ffn_up_dx", a_parts=2, add=dz2, add_scale=ALPHA, deps=tok)
    dz1, dz1b, g_ln1_0, b_ln1_0 = _ln_bwd(z1, ln1_g[0:1], dx1, "ln_bwd")
    tok = scatter("out0", _mm(yab, dz1b, "tn", _XCH, "ev_out_dw", deps=tok), 0, n_out0)
    dyab = _mm(dz1b, w_out0, "nt", _F32, "ev_out_dx", deps=tok)
    dh0, dws, dbias, dlng, dlnb, dwp, dsc = _ev_mid_bwd(h0, dyab, ev_ln_v_g, ev_ln_v_b, ws_tril, bias, wp_b,
                                                        ev_pool_scale, "ev_mid_bwd")

    g_b_s = dbias.reshape(A_CHUNK, H, A_HEAD).sum(axis=-1).T[None]
    g_conv_w = jnp.stack([d.transpose(1, 0, 2).reshape(3, F2) for d in (dcw0, dcw1)])
    g_conv_b = jnp.stack([d.reshape(F2) for d in (dcb0, dcb1)])
    small_names = ["ev_ln_v_g", "ev_ln_v_b", "ev_w_s", "ev_b_s", "ev_w_pool", "ev_pool_scale", "od_norm_g", "lb_param",
                   "ffn_conv_w", "ffn_conv_b", "ln1_g", "ln1_b", "ln2_g", "ln2_b"]
    small_grads = [dlng, dlnb, dws[None], g_b_s, dwp[None], dsc, dng, lb_vjp(dlb)[0], g_conv_w, g_conv_b,
                   jnp.concatenate([g_ln1_0, g_ln1_1]), jnp.concatenate([b_ln1_0, b_ln1_1]),
                   jnp.concatenate([g_ln2_0, g_ln2_1]), jnp.concatenate([b_ln2_0, b_ln2_1])]
    full_shapes = [g.shape for g in small_grads]
    tok = scatter("in0", _mm(xb, dh0, "tn", _XCH, "ev_in_dw", b_parts=3, deps=tok), 1, n_in0)
    tok = scatter("small", _pack(small_grads) + tok[0][0, 0], None, None)
    grad_x = _mm(dh0, w_in0, "nt", _F32, "ev_in_dx", a_parts=3, add=dz1, add_scale=ALPHA, deps=tok)

    def landed(key, after):
        dw, land, send, recv, axis, n = scat[key]
        dw, land = _scatter_wait("scatter_wait_" + key, dw, land, send, recv, axis, n, after)
        return me1, dw, land

    big = {}
    r_dn = _adamw_big(*landed("dn1", grad_x), ffn_w_down, m_ffn_w_down, v_ffn_w_down, 0, n_dn, "adamw_w_dn1", layer=1)
    r_up = _adamw_big(*landed("up1", r_dn[0]), ffn_w_up, m_ffn_w_up, v_ffn_w_up, 1, n_up, "adamw_w_up1", layer=1)
    big["od_w_out"] = _adamw_big(*landed("out1", r_up[0]), od_w_out[0], m_od_w_out[0], v_od_w_out[0], 0, n_out1, "adamw_w_out1")
    big["od_w_in"] = _adamw_big(*landed("in1", big["od_w_out"][0]), od_w_in[0], m_od_w_in[0], v_od_w_in[0], 1, n_in1, "adamw_w_in1")
    big["ffn_w_down"] = _adamw_big(*landed("dn0", big["od_w_in"][0]), ffn_w_down, m_ffn_w_down, v_ffn_w_down, 0, n_dn,
                                   "adamw_w_dn0", layer=0, into=r_dn)
    big["ffn_w_up"] = _adamw_big(*landed("up0", big["ffn_w_down"][0]), ffn_w_up, m_ffn_w_up, v_ffn_w_up, 1, n_up,
                                 "adamw_w_up0", layer=0, into=r_up)
    big["ev_w_out"] = _adamw_big(*landed("out0", big["ffn_w_up"][0]), ev_w_out[0], m_ev_w_out[0], v_ev_w_out[0], 0, n_out0, "adamw_w_out0")
    big["ev_w_in"] = _adamw_big(*landed("in0", big["ev_w_out"][0]), ev_w_in[0], m_ev_w_in[0], v_ev_w_in[0], 1, n_in0, "adamw_w_in0")

    gsum = _unpack(_sum_in_device_order(*landed("small", big["ev_w_in"][0]), "sum_small_grads"), full_shapes)
    given = dict(ev_ln_v_g=(ev_ln_v_g, m_ev_ln_v_g, v_ev_ln_v_g), ev_ln_v_b=(ev_ln_v_b, m_ev_ln_v_b, v_ev_ln_v_b),
                 ev_w_s=(ev_w_s, m_ev_w_s, v_ev_w_s), ev_b_s=(ev_b_s, m_ev_b_s, v_ev_b_s),
                 ev_w_pool=(ev_w_pool, m_ev_w_pool, v_ev_w_pool),
                 ev_pool_scale=(ev_pool_scale, m_ev_pool_scale, v_ev_pool_scale),
                 od_norm_g=(od_norm_g, m_od_norm_g, v_od_norm_g), lb_param=(lb_param, m_lb_param, v_lb_param),
                 ffn_conv_w=(ffn_conv_w, m_ffn_conv_w, v_ffn_conv_w), ffn_conv_b=(ffn_conv_b, m_ffn_conv_b, v_ffn_conv_b),
                 ln1_g=(ln1_g, m_ln1_g, v_ln1_g), ln1_b=(ln1_b, m_ln1_b, v_ln1_b), ln2_g=(ln2_g, m_ln2_g, v_ln2_g),
                 ln2_b=(ln2_b, m_ln2_b, v_ln2_b))
    shard_axis = dict(ev_w_pool=(2, n_pool), od_norm_g=(1, n_ng), ffn_conv_w=(2, n_cw))
    local_g = []
    for name, g in zip(small_names, gsum):
        if name in shard_axis:
            ax, n = shard_axis[name]
            g = lax.dynamic_slice_in_dim(g, me * n, n, axis=ax)
        local_g.append(g)
    local_shapes = [g.shape for g in local_g]
    res = _adamw(_pack(local_g)[None], _pack([given[n][0] for n in small_names]), _pack([given[n][1] for n in small_names]),
                 _pack([given[n][2] for n in small_names]), "adamw_small")
    small = {n: [] for n in small_names}
    for r in res:
        for n, a in zip(small_names, _unpack(r, local_shapes)):
            small[n].append(a)

    loss = lax.psum(loss11[0, 0], ("x", "y", "c"))
    order = ["ev_w_in", "ev_ln_v_g", "ev_ln_v_b", "ev_w_s", "ev_b_s", "ev_w_pool", "ev_pool_scale", "ev_w_out", "od_w_in",
             "od_norm_g", "od_w_out", "lb_param", "ffn_w_up", "ffn_conv_w", "ffn_conv_b", "ffn_w_down", "ln1_g", "ln1_b",
             "ln2_g", "ln2_b"]
    shapes = dict(ev_w_in=ev_w_in.shape, ev_w_out=ev_w_out.shape, od_w_in=od_w_in.shape, od_w_out=od_w_out.shape,
                  ffn_w_up=ffn_w_up.shape, ffn_w_down=ffn_w_down.shape)
    outs = [loss, grad_x[None]]
    for kind in range(4):
        for n in order:
            outs.append(big[n][kind].reshape(shapes[n]) if n in big else small[n][kind])
    return tuple(outs)
```

```python
import functools

import jax
import jax.numpy as jnp
from jax import lax
from jax.experimental import pallas as pl
from jax.experimental.pallas import tpu as pltpu

_MM = jnp.bfloat16
_XCH = jnp.bfloat16

DEPTH = 2
ALPHA = (2 * DEPTH) ** 0.25
LN_EPS = 1e-5
A_CHUNK = 128
A_HEAD = 128
B_GROUPS = 4
POOL_HALO = 16
C_CHUNK = 64
C_HEAD = 128
SCAN_UNROLL = 8
CONV_HALO = 8
FFN_ROWS, FFN_FWD_COLS = 256, 1408
FFN_BWD_ROWS, FFN_BWD_COLS = 512, 512
ADAM_LR, ADAM_B1, ADAM_B2, ADAM_EPS, ADAM_WD, ADAM_STEP = 0.001, 0.9, 0.999, 1e-08, 0.01, 10
N_DEV = 8
LANE = 128
VMEM_LIMIT = 56 * 1024 * 1024
MM_FULL_K = 2048
MM_FULL_K_TN = 4096
MM_DEEP_K = 2816

_F32 = jnp.float32
_NN = (((1,), (0,)), ((), ()))
_NT = (((1,), (1,)), ((), ()))
_TN = (((0,), (0,)), ((), ()))
_S = jax.ShapeDtypeStruct


def _dot(a, b, dims=_NN):
    return lax.dot_general(a, b, dims, preferred_element_type=_F32)


def _tile(dim, pref):
    best = None
    d = LANE
    while d <= min(dim, pref):
        if dim % d == 0:
            best = d
        d += LANE
    return best if best is not None else dim


def _params(sem):
    return pltpu.CompilerParams(dimension_semantics=sem, vmem_limit_bytes=VMEM_LIMIT)


def _sigmoid(x):
    return 0.5 * jnp.tanh(0.5 * x) + 0.5


def _sigmoid_rel(x):
    return 1.0 / (1.0 + jnp.exp(-x))


_GELU_C = 0.7978845608028654
_GELU_A = 0.044715


def _gelu_and_grad(x):
    t = jnp.tanh(_GELU_C * (x + _GELU_A * x * x * x))
    y = 0.5 * x * (1.0 + t)
    dy = 0.5 * (1.0 + t) + 0.5 * x * (1.0 - t * t) * _GELU_C * (1.0 + 3.0 * _GELU_A * x * x)
    return y, dy


def _row_index(n):
    return lax.broadcasted_iota(jnp.int32, (n, 1), 0)


def _mm_tiles(mode, M, N, K, with_add):
    if mode == "tn":
        return _tile(M, 1024), _tile(N, 1024), _tile(K, MM_FULL_K_TN)
    if K <= MM_FULL_K:
        return _tile(M, 1024 if with_add else 2048), _tile(N, 1024 if mode == "nn" else 512), K
    return _tile(M, 1024), _tile(N, 1024), _tile(K, MM_DEEP_K)


def _mm(a, b, mode, out_dtype, name, *, a_parts=1, b_parts=1, out_parts=1, add=None, add_scale=1.0, deps=(), tiles=None):
    if mode == "nn":
        M, K = a.shape
        N = b.shape[1]
    elif mode == "nt":
        if a_parts > 1:
            M, K = a.shape[1], a.shape[2] * a_parts
        else:
            M, K = a.shape
        N = b.shape[0]
    else:
        K, M = a.shape
        N = b.shape[-1] * b_parts
    tm, tn, tk = tiles if tiles is not None else _mm_tiles(mode, M, N // max(b_parts, out_parts), K // a_parts, add is not None)
    nk = K // tk
    npj = (N // max(b_parts, out_parts)) // tn
    nkp = (K // a_parts) // tk
    if mode == "nn":
        a_spec = pl.BlockSpec((tm, tk), lambda i, j, k: (i, k))
        b_spec = pl.BlockSpec((tk, tn), lambda i, j, k: (k, j))
        dims = _NN
    elif mode == "nt":
        if a_parts > 1:
            a_spec = pl.BlockSpec((None, tm, tk), lambda i, j, k: (k // nkp, i, k % nkp))
        else:
            a_spec = pl.BlockSpec((tm, tk), lambda i, j, k: (i, k))
        b_spec = pl.BlockSpec((tn, tk), lambda i, j, k: (j, k))
        dims = _NT
    else:
        a_spec = pl.BlockSpec((tk, tm), lambda i, j, k: (k, i))
        if b_parts > 1:
            b_spec = pl.BlockSpec((None, tk, tn), lambda i, j, k: (j // npj, k, j % npj))
        else:
            b_spec = pl.BlockSpec((tk, tn), lambda i, j, k: (k, j))
        dims = _TN
    if out_parts > 1:
        out_spec = pl.BlockSpec((None, tm, tn), lambda i, j, k: (j // npj, i, j % npj))
        out_shape = _S((out_parts, M, N // out_parts), out_dtype)
    else:
        out_spec = pl.BlockSpec((tm, tn), lambda i, j, k: (i, j))
        out_shape = _S((M, N), out_dtype)
    in_specs = [a_spec, b_spec]
    args = [a, b]
    if add is not None:
        in_specs.append(pl.BlockSpec((tm, tn), lambda i, j, k: (i, j)))
        args.append(add)
    in_specs += [_ANY] * len(deps)
    args += list(deps)

    def finish(r, refs, o_ref):
        if add is not None:
            r = r + add_scale * refs[2][...]
        o_ref[...] = r.astype(o_ref.dtype)

    def body_one(*refs):
        finish(_dot(refs[0][...], refs[1][...], dims), refs, refs[-1])

    def body_acc(*refs):
        o_ref, acc = refs[-2], refs[-1]
        k = pl.program_id(2)

        @pl.when(k == 0)
        def _():
            acc[...] = jnp.zeros_like(acc)

        acc[...] += _dot(refs[0][...], refs[1][...], dims)

        @pl.when(k == nk - 1)
        def _():
            finish(acc[...], refs, o_ref)

    return pl.pallas_call(
        body_one if nk == 1 else body_acc, name=name, grid=(M // tm, N // tn, nk), in_specs=in_specs,
        out_specs=out_spec, out_shape=out_shape,
        scratch_shapes=[] if nk == 1 else [pltpu.VMEM((tm, tn), _F32)],
        compiler_params=_params(("parallel", "parallel", "arbitrary")),
    )(*args)


def _cast(x2d, dtype, name, deps=()):
    R, C = x2d.shape
    tr = _tile(R, 512) if R % LANE == 0 else R

    def body(x_ref, *rest):
        rest[-1][...] = x_ref[...].astype(rest[-1].dtype)

    return pl.pallas_call(
        body, name=name, grid=(R // tr,), in_specs=[pl.BlockSpec((tr, C), lambda i: (i, 0))] + [_ANY] * len(deps),
        out_specs=pl.BlockSpec((tr, C), lambda i: (i, 0)), out_shape=_S((R, C), dtype),
        compiler_params=_params(("parallel",)),
    )(x2d, *deps)


def _ln_fwd(z, g, b, name):
    T, D = z.shape
    tr = _tile(T, 256)

    def body(z_ref, g_ref, b_ref, y_ref, yb_ref):
        zz = z_ref[...]
        mu = jnp.mean(zz, axis=-1, keepdims=True)
        zc = zz - mu
        var = jnp.mean(zc * zc, axis=-1, keepdims=True)
        y = zc * lax.rsqrt(var + LN_EPS) * g_ref[...] + b_ref[...]
        y_ref[...] = y
        yb_ref[...] = y.astype(yb_ref.dtype)

    row = pl.BlockSpec((tr, D), lambda i: (i, 0))
    vec = pl.BlockSpec((1, D), lambda i: (0, 0))
    return pl.pallas_call(
        body, name=name, grid=(T // tr,), in_specs=[row, vec, vec], out_specs=[row, row],
        out_shape=[_S((T, D), _F32), _S((T, D), _MM)], compiler_params=_params(("parallel",)),
    )(z, g, b)


def _ln_bwd(z, g, dy, name):
    T, D = z.shape
    tr = _tile(T, 256)

    def body(z_ref, g_ref, dy_ref, dz_ref, dzb_ref, dg_ref, db_ref):
        @pl.when(pl.program_id(0) == 0)
        def _():
            dg_ref[...] = jnp.zeros_like(dg_ref)
            db_ref[...] = jnp.zeros_like(db_ref)

        zz = z_ref[...]
        mu = jnp.mean(zz, axis=-1, keepdims=True)
        zc = zz - mu
        rstd = lax.rsqrt(jnp.mean(zc * zc, axis=-1, keepdims=True) + LN_EPS)
        xh = zc * rstd
        d = dy_ref[...]
        dg_ref[...] += jnp.sum(d * xh, axis=0, keepdims=True)
        db_ref[...] += jnp.sum(d, axis=0, keepdims=True)
        dxh = d * g_ref[...]
        dz = rstd * (dxh - jnp.mean(dxh, axis=-1, keepdims=True) - xh * jnp.mean(dxh * xh, axis=-1, keepdims=True))
        dz_ref[...] = dz
        dzb_ref[...] = dz.astype(dzb_ref.dtype)

    row = pl.BlockSpec((tr, D), lambda i: (i, 0))
    vec = pl.BlockSpec((1, D), lambda i: (0, 0))
    return pl.pallas_call(
        body, name=name, grid=(T // tr,), in_specs=[row, vec, row], out_specs=[row, row, vec, vec],
        out_shape=[_S((T, D), _F32), _S((T, D), _MM), _S((1, D), _F32), _S((1, D), _F32)],
        compiler_params=_params(("arbitrary",)),
    )(z, g, dy)


def _ln_loss_bwd(z, g, b, target, name):
    T, D = z.shape
    tr = _tile(T, 256)

    def body(z_ref, g_ref, b_ref, t_ref, loss_ref, dz_ref, dzb_ref, dg_ref, db_ref, lacc):
        i = pl.program_id(0)

        @pl.when(i == 0)
        def _():
            dg_ref[...] = jnp.zeros_like(dg_ref)
            db_ref[...] = jnp.zeros_like(db_ref)
            lacc[...] = jnp.zeros_like(lacc)

        zz = z_ref[...]
        mu = jnp.mean(zz, axis=-1, keepdims=True)
        zc = zz - mu
        rstd = lax.rsqrt(jnp.mean(zc * zc, axis=-1, keepdims=True) + LN_EPS)
        xh = zc * rstd
        err = xh * g_ref[...] + b_ref[...] - t_ref[...]
        lacc[...] += jnp.sum(err * err, axis=0, keepdims=True)
        d = err * (1.0 / D)
        dg_ref[...] += jnp.sum(d * xh, axis=0, keepdims=True)
        db_ref[...] += jnp.sum(d, axis=0, keepdims=True)
        dxh = d * g_ref[...]
        dz = rstd * (dxh - jnp.mean(dxh, axis=-1, keepdims=True) - xh * jnp.mean(dxh * xh, axis=-1, keepdims=True))
        dz_ref[...] = dz
        dzb_ref[...] = dz.astype(dzb_ref.dtype)

        @pl.when(i == pl.num_programs(0) - 1)
        def _():
            loss_ref[...] = jnp.sum(lacc[...], axis=-1, keepdims=True) * (0.5 / D)

    row = pl.BlockSpec((tr, D), lambda i: (i, 0))
    vec = pl.BlockSpec((1, D), lambda i: (0, 0))
    one = pl.BlockSpec((1, 1), lambda i: (0, 0))
    return pl.pallas_call(
        body, name=name, grid=(T // tr,), in_specs=[row, vec, vec, row], out_specs=[one, row, row, vec, vec],
        out_shape=[_S((1, 1), _F32), _S((T, D), _F32), _S((T, D), _MM), _S((1, D), _F32), _S((1, D), _F32)],
        scratch_shapes=[pltpu.VMEM((1, D), _F32)], compiler_params=_params(("arbitrary",)),
    )(z, g, b, target)


def _conv3(X, cw, cb):
    return cb + cw[2:3] * X + cw[1:2] * pltpu.roll(X, 1, 0) + cw[0:1] * pltpu.roll(X, 2, 0)


def _ffn_mid_fwd(h, cw, cb, name):
    _, T, F = h.shape
    tr = _tile(T, FFN_ROWS)
    tc = _tile(F, FFN_FWD_COLS)
    nb = tr // CONV_HALO

    def body(h_ref, p_ref, cw_ref, cb_ref, o_ref):
        i = pl.program_id(0)
        hc = []
        for part in range(2):
            prev = jnp.where(i == 0, 0.0, p_ref[part])
            X = jnp.concatenate([prev, h_ref[part]], axis=0)
            hc.append(_conv3(X, cw_ref[part], cb_ref[part])[CONV_HALO:])
        a, v = hc
        o_ref[...] = (a * _sigmoid(a) * v).astype(o_ref.dtype)

    return pl.pallas_call(
        body, name=name, grid=(T // tr, F // tc),
        in_specs=[pl.BlockSpec((2, tr, tc), lambda i, j: (0, i, j)),
                  pl.BlockSpec((2, CONV_HALO, tc), lambda i, j: (0, jnp.maximum(i * nb - 1, 0), j)),
                  pl.BlockSpec((2, 3, tc), lambda i, j: (0, 0, j)),
                  pl.BlockSpec((2, 1, tc), lambda i, j: (0, 0, j))],
        out_specs=pl.BlockSpec((tr, tc), lambda i, j: (i, j)), out_shape=_S((T, F), _MM),
        compiler_params=_params(("parallel", "parallel")),
    )(h, h, cw, cb)


def _ffn_mid_bwd(h, dact, cw, cb, name):
    _, T, F = h.shape
    tr = _tile(T, FFN_BWD_ROWS)
    tc = _tile(F, FFN_BWD_COLS)
    nb = tr // CONV_HALO
    last_blk = T // CONV_HALO - 1
    lo, hi = CONV_HALO, CONV_HALO + tr
    n = tr + 2 * CONV_HALO

    def body(h_ref, p_ref, n_ref, d_ref, dn_ref, cw_ref, cb_ref, dh_ref, dcw_ref, dcb_ref):
        i = pl.program_id(1)
        is_first = i == 0
        is_last = i == pl.num_programs(1) - 1

        @pl.when(is_first)
        def _():
            dcw_ref[...] = jnp.zeros_like(dcw_ref)
            dcb_ref[...] = jnp.zeros_like(dcb_ref)

        X, hc = [], []
        for part in range(2):
            prev = jnp.where(is_first, 0.0, p_ref[part])
            nxt = jnp.where(is_last, 0.0, n_ref[part])
            Xp = jnp.concatenate([prev, h_ref[part], nxt], axis=0)
            X.append(Xp)
            hc.append(_conv3(Xp, cw_ref[part], cb_ref[part]))
        D = jnp.concatenate([jnp.zeros((CONV_HALO, tc), _F32), d_ref[...], jnp.where(is_last, 0.0, dn_ref[...])], axis=0)
        a, v = hc
        sg = _sigmoid(a)
        dhc = [D * v * sg * (1.0 + a * (1.0 - sg)), D * a * sg]
        for part in range(2):
            g = dhc[part]
            cwp = cw_ref[part]
            dh = cwp[2:3] * g + cwp[1:2] * pltpu.roll(g, n - 1, 0) + cwp[0:1] * pltpu.roll(g, n - 2, 0)
            dh_ref[part] = dh[lo:hi].astype(dh_ref.dtype)
            gt = g[lo:hi]
            dcw_ref[part, 2:3, :] += jnp.sum(gt * X[part][lo:hi], axis=0, keepdims=True)
            dcw_ref[part, 1:2, :] += jnp.sum(gt * pltpu.roll(X[part], 1, 0)[lo:hi], axis=0, keepdims=True)
            dcw_ref[part, 0:1, :] += jnp.sum(gt * pltpu.roll(X[part], 2, 0)[lo:hi], axis=0, keepdims=True)
            dcb_ref[part] += jnp.sum(gt, axis=0, keepdims=True)

    return pl.pallas_call(
        body, name=name, grid=(F // tc, T // tr),
        in_specs=[pl.BlockSpec((2, tr, tc), lambda j, i: (0, i, j)),
                  pl.BlockSpec((2, CONV_HALO, tc), lambda j, i: (0, jnp.maximum(i * nb - 1, 0), j)),
                  pl.BlockSpec((2, CONV_HALO, tc), lambda j, i: (0, jnp.minimum((i + 1) * nb, last_blk), j)),
                  pl.BlockSpec((tr, tc), lambda j, i: (i, j)),
                  pl.BlockSpec((CONV_HALO, tc), lambda j, i: (jnp.minimum((i + 1) * nb, last_blk), j)),
                  pl.BlockSpec((2, 3, tc), lambda j, i: (0, 0, j)),
                  pl.BlockSpec((2, 1, tc), lambda j, i: (0, 0, j))],
        out_specs=[pl.BlockSpec((2, tr, tc), lambda j, i: (0, i, j)),
                   pl.BlockSpec((2, 3, tc), lambda j, i: (0, 0, j)),
                   pl.BlockSpec((2, 1, tc), lambda j, i: (0, 0, j))],
        out_shape=[_S((2, T, F), _MM), _S((2, 3, F), _F32), _S((2, 1, F), _F32)],
        compiler_params=_params(("parallel", "arbitrary")),
    )(h, h, h, dact, dact, cw, cb)


def _ev_common(h_ref, hp_ref, lng_ref, lnb_ref, ws_ref, bias_ref, i, tr, W):
    H = W // A_HEAD
    u, gu = _gelu_and_grad(h_ref[0])
    v, gv = _gelu_and_grad(h_ref[1])
    mu = jnp.mean(v, axis=-1, keepdims=True)
    vc = v - mu
    rstd = lax.rsqrt(jnp.mean(vc * vc, axis=-1, keepdims=True) + LN_EPS)
    vhat = vc * rstd
    vb = (vhat * lng_ref[...] + lnb_ref[...]).astype(_MM)
    s_chunks = []
    for c in range(tr // A_CHUNK):
        r0 = c * A_CHUNK
        heads = [_dot(ws_ref[hd], vb[r0:r0 + A_CHUNK, hd * A_HEAD:(hd + 1) * A_HEAD]) for hd in range(H)]
        s_chunks.append(jnp.concatenate(heads, axis=1) + bias_ref[...])
    prev = jnp.where(i == 0, 0.0, hp_ref[...])
    X = jnp.concatenate([prev, h_ref[2]], axis=0)
    return u, gu, gv, rstd, vhat, vb, s_chunks, X


def _pool_inv_count(i, tr, rows, win):
    pos = i * tr + _row_index(rows) + 1
    return 1.0 / jnp.minimum(pos, win).astype(_F32)


def _pool_fwd(X, g, Wg, i, tr):
    xg = X[:, g * Wg:(g + 1) * Wg]
    s = xg
    for k in range(g + 1):
        s = s + pltpu.roll(s, 2 ** k, 0)
    return s[POOL_HALO:] * _pool_inv_count(i, tr, tr, 2 ** (g + 1)) - xg[POOL_HALO:]


def _ev_mid_fwd(h, lng, lnb, ws, bias, wp, sc, name):
    _, T, W = h.shape
    tr = _tile(T, 256)
    H = W // A_HEAD
    Wg = W // B_GROUPS
    nb = tr // POOL_HALO

    def body(h_ref, hp_ref, lng_ref, lnb_ref, ws_ref, bias_ref, wp_ref, sc_ref, o_ref):
        i = pl.program_id(0)
        u, _, _, _, _, _, s_chunks, X = _ev_common(h_ref, hp_ref, lng_ref, lnb_ref, ws_ref, bias_ref, i, tr, W)
        for c, s in enumerate(s_chunks):
            r0 = c * A_CHUNK
            o_ref[r0:r0 + A_CHUNK, 0:W] = (u[r0:r0 + A_CHUNK] * s).astype(o_ref.dtype)
        for g in range(B_GROUPS):
            p = _pool_fwd(X, g, Wg, i, tr)
            y = _dot(p.astype(_MM), wp_ref[g]) * sc_ref[:, g * Wg:(g + 1) * Wg]
            o_ref[:, W + g * Wg:W + (g + 1) * Wg] = y.astype(o_ref.dtype)

    vec = pl.BlockSpec((1, W), lambda i: (0, 0))
    return pl.pallas_call(
        body, name=name, grid=(T // tr,),
        in_specs=[pl.BlockSpec((3, tr, W), lambda i: (0, i, 0)),
                  pl.BlockSpec((None, POOL_HALO, W), lambda i: (2, jnp.maximum(i * nb - 1, 0), 0)),
                  vec, vec,
                  pl.BlockSpec((H, A_CHUNK, A_CHUNK), lambda i: (0, 0, 0)),
                  pl.BlockSpec((A_CHUNK, W), lambda i: (0, 0)),
                  pl.BlockSpec((B_GROUPS, Wg, Wg), lambda i: (0, 0, 0)),
                  vec],
        out_specs=pl.BlockSpec((tr, 2 * W), lambda i: (i, 0)), out_shape=_S((T, 2 * W), _MM),
        compiler_params=_params(("parallel",)),
    )(h, h, lng, lnb, ws, bias, wp, sc)


def _ev_mid_bwd(h, dy, lng, lnb, ws, bias, wp, sc, name):
    _, T, W = h.shape
    tr = _tile(T, 256)
    H = W // A_HEAD
    Wg = W // B_GROUPS
    nb = tr // POOL_HALO
    last_blk = T // POOL_HALO - 1
    n = tr + POOL_HALO

    def body(h_ref, hp_ref, dy_ref, dyn_ref, lng_ref, lnb_ref, ws_ref, bias_ref, wp_ref, sc_ref,
             dh_ref, dws_ref, dbias_ref, dlng_ref, dlnb_ref, dwp_ref, dsc_ref):
        i = pl.program_id(0)

        @pl.when(i == 0)
        def _():
            for r in (dws_ref, dbias_ref, dlng_ref, dlnb_ref, dwp_ref, dsc_ref):
                r[...] = jnp.zeros_like(r)

        u, gu, gv, rstd, vhat, vb, s_chunks, X = _ev_common(h_ref, hp_ref, lng_ref, lnb_ref, ws_ref, bias_ref, i, tr, W)
        rr = lax.broadcasted_iota(jnp.int32, (A_CHUNK, A_CHUNK), 0)
        cc = lax.broadcasted_iota(jnp.int32, (A_CHUNK, A_CHUNK), 1)
        tril = rr >= cc
        du_chunks, dvln_chunks = [], []
        for c, s in enumerate(s_chunks):
            r0 = c * A_CHUNK
            dya = dy_ref[r0:r0 + A_CHUNK, 0:W]
            du_chunks.append(dya * s)
            ds = dya * u[r0:r0 + A_CHUNK]
            dbias_ref[...] += ds
            dsb = ds.astype(_MM)
            heads = []
            for hd in range(H):
                cols = slice(hd * A_HEAD, (hd + 1) * A_HEAD)
                dws_ref[hd] += jnp.where(tril, _dot(dsb[:, cols], vb[r0:r0 + A_CHUNK, cols], _NT), 0.0)
                heads.append(_dot(ws_ref[hd], dsb[:, cols], _TN))
            dvln_chunks.append(jnp.concatenate(heads, axis=1))
        du = jnp.concatenate(du_chunks, axis=0)
        dvln = jnp.concatenate(dvln_chunks, axis=0)
        dlng_ref[...] += jnp.sum(dvln * vhat, axis=0, keepdims=True)
        dlnb_ref[...] += jnp.sum(dvln, axis=0, keepdims=True)
        dxh = dvln * lng_ref[...]
        dv = rstd * (dxh - jnp.mean(dxh, axis=-1, keepdims=True) - vhat * jnp.mean(dxh * vhat, axis=-1, keepdims=True))
        dh_ref[0] = (du * gu).astype(dh_ref.dtype)
        dh_ref[1] = (dv * gv).astype(dh_ref.dtype)

        dyb = dy_ref[:, W:2 * W]
        dyb_full = jnp.concatenate([dyb, jnp.where(i == pl.num_programs(0) - 1, 0.0, dyn_ref[...])], axis=0)
        for g in range(B_GROUPS):
            cols = slice(g * Wg, (g + 1) * Wg)
            pb = _pool_fwd(X, g, Wg, i, tr).astype(_MM)
            ypre = _dot(pb, wp_ref[g])
            dsc_ref[:, cols] += jnp.sum(dyb[:, cols] * ypre, axis=0, keepdims=True)
            dyp = (dyb_full[:, cols] * sc_ref[:, cols]).astype(_MM)
            dwp_ref[g] += _dot(pb, dyp[0:tr], _TN)
            dp = _dot(dyp, wp_ref[g], _NT)
            s = dp * _pool_inv_count(i, tr, n, 2 ** (g + 1))
            for k in range(g + 1):
                s = s + pltpu.roll(s, n - 2 ** k, 0)
            dh_ref[2, :, cols] = (s[0:tr] - dp[0:tr]).astype(dh_ref.dtype)

    vec = pl.BlockSpec((1, W), lambda i: (0, 0))
    ws_spec = pl.BlockSpec((H, A_CHUNK, A_CHUNK), lambda i: (0, 0, 0))
    bias_spec = pl.BlockSpec((A_CHUNK, W), lambda i: (0, 0))
    wp_spec = pl.BlockSpec((B_GROUPS, Wg, Wg), lambda i: (0, 0, 0))
    return pl.pallas_call(
        body, name=name, grid=(T // tr,),
        in_specs=[pl.BlockSpec((3, tr, W), lambda i: (0, i, 0)),
                  pl.BlockSpec((None, POOL_HALO, W), lambda i: (2, jnp.maximum(i * nb - 1, 0), 0)),
                  pl.BlockSpec((tr, 2 * W), lambda i: (i, 0)),
                  pl.BlockSpec((POOL_HALO, W), lambda i: (jnp.minimum((i + 1) * nb, last_blk), 1)),
                  vec, vec, ws_spec, bias_spec, wp_spec, vec],
        out_specs=[pl.BlockSpec((3, tr, W), lambda i: (0, i, 0)), ws_spec, bias_spec, vec, vec, wp_spec, vec],
        out_shape=[_S((3, T, W), _MM), _S((H, A_CHUNK, A_CHUNK), _F32), _S((A_CHUNK, W), _F32), _S((1, W), _F32),
                   _S((1, W), _F32), _S((B_GROUPS, Wg, Wg), _F32), _S((1, W), _F32)],
        compiler_params=_params(("arbitrary",)),
    )(h, h, dy, dy, lng, lnb, ws, bias, wp, sc)


def _chunk_cumsum(x, rin):
    s = 1
    while s < C_CHUNK:
        x = x + jnp.where(rin >= s, pltpu.roll(x, s, 0), 0.0)
        s *= 2
    return x


def _chunk_revcumsum(x, rin):
    n = x.shape[0]
    s = 1
    while s < C_CHUNK:
        x = x + jnp.where(rin + s < C_CHUNK, pltpu.roll(x, n - s, 0), 0.0)
        s *= 2
    return x


def _hgrn_gates(q, fl, lb, tr, tc):
    nch = tr // C_CHUNK
    sq = _sigmoid(q)
    sf = _sigmoid_rel(fl)
    f = lb + (1.0 - lb) * sf
    logf = jnp.log(f)
    rin = _row_index(tr) % C_CHUNK
    b = _chunk_cumsum(logf, rin)
    tot3 = jnp.sum(logf.reshape(nch, C_CHUNK, tc), axis=1, keepdims=True)
    eb = jnp.exp(b)
    enb = jnp.exp(-b)
    ekb = jnp.exp(tot3 - b.reshape(nch, C_CHUNK, tc)).reshape(tr, tc)
    return sq, sf, f, rin, tot3, eb, enb, ekb


def _hgrn_prep_fwd(h, lb, name):
    _, T, D = h.shape
    tr = _tile(T, 512)
    tc = _tile(D, 512)
    nch = tr // C_CHUNK

    def body(q_ref, f_ref, v_ref, lb_ref, qd_ref, kd_ref, ke_ref, vb_ref, dec_ref):
        q = q_ref[...]
        sq, _, f, _, tot3, eb, enb, ekb = _hgrn_gates(q, f_ref[...], lb_ref[...], tr, tc)
        kk = 1.0 - f
        qd_ref[...] = (q * sq * eb).astype(qd_ref.dtype)
        kd_ref[...] = (kk * enb).astype(kd_ref.dtype)
        ke_ref[...] = (kk * ekb).astype(ke_ref.dtype)
        vb_ref[...] = v_ref[...].astype(vb_ref.dtype)
        dec_ref[...] = jnp.exp(tot3).reshape(nch, tc)

    def part(p):
        return pl.BlockSpec((None, tr, tc), lambda i, j: (p, i, j))

    blk = pl.BlockSpec((tr, tc), lambda i, j: (i, j))
    return pl.pallas_call(
        body, name=name, grid=(T // tr, D // tc),
        in_specs=[part(0), part(1), part(2), pl.BlockSpec((1, tc), lambda i, j: (0, j))],
        out_specs=[blk, blk, blk, blk, pl.BlockSpec((nch, tc), lambda i, j: (i, j))],
        out_shape=[_S((T, D), _MM)] * 4 + [_S((T // C_CHUNK, D), _F32)],
        compiler_params=_params(("parallel", "parallel")),
    )(h, h, h, lb)


def _tril_mask():
    rr = lax.broadcasted_iota(jnp.int32, (C_CHUNK, C_CHUNK), 0)
    cc = lax.broadcasted_iota(jnp.int32, (C_CHUNK, C_CHUNK), 1)
    return rr >= cc


def _hgrn_scan_fwd(qd, kd, ke, vb, dec, h, ng, name):
    T, D = qd.shape
    NH = D // C_HEAD
    N = T // C_CHUNK

    def body(qd_ref, kd_ref, ke_ref, vb_ref, dec_ref, g_ref, ng_ref, o_ref, y_ref, st_ref):
        mask = _tril_mask()

        def step(n, St):
            r = pl.ds(pl.multiple_of(n * C_CHUNK, C_CHUNK), C_CHUNK)
            Qd, Kd, Ke, V = qd_ref[r, :], kd_ref[r, :], ke_ref[r, :], vb_ref[r, :]
            att = jnp.where(mask, _dot(Qd, Kd, _NT), 0.0).astype(_MM)
            o_ref[r, :] = _dot(att, V) + _dot(Qd, St.astype(_MM), _NT)
            st_ref[n] = St
            return St * dec_ref[pl.ds(n, 1), :] + _dot(V, Ke, _TN)

        def trip(i, state):
            for u in range(SCAN_UNROLL):
                state = step(i * SCAN_UNROLL + u, state)
            return state

        lax.fori_loop(0, N // SCAN_UNROLL, trip, jnp.zeros((C_HEAD, C_HEAD), _F32))
        o = o_ref[...]
        r = lax.rsqrt(jnp.mean(o * o, axis=-1, keepdims=True) + LN_EPS)
        y_ref[...] = (o * r * ng_ref[...] * _sigmoid(g_ref[...])).astype(y_ref.dtype)

    col = pl.BlockSpec((T, C_HEAD), lambda j: (0, j))
    return pl.pallas_call(
        body, name=name, grid=(NH,),
        in_specs=[col, col, col, col, pl.BlockSpec((N, C_HEAD), lambda j: (0, j)),
                  pl.BlockSpec((None, T, C_HEAD), lambda j: (3, 0, j)), pl.BlockSpec((1, C_HEAD), lambda j: (0, j))],
        out_specs=[col, col, pl.BlockSpec((None, N, C_HEAD, C_HEAD), lambda j: (j, 0, 0, 0))],
        out_shape=[_S((T, D), _F32), _S((T, D), _MM), _S((NH, N, C_HEAD, C_HEAD), _F32)],
        compiler_params=_params(("parallel",)),
    )(qd, kd, ke, vb, dec, h, ng)


def _hgrn_scan_bwd(qd, kd, ke, vb, dec, st, o, h, ng, dy, name):
    T, D = qd.shape
    NH = D // C_HEAD
    N = T // C_CHUNK

    def body(qd_ref, kd_ref, ke_ref, vb_ref, dec_ref, st_ref, o_ref, g_ref, ng_ref, dy_ref,
             dqd_ref, dkd_ref, dke_ref, dv_ref, dgate_ref, ddec_ref, dng_ref, do_s):
        o = o_ref[...]
        r = lax.rsqrt(jnp.mean(o * o, axis=-1, keepdims=True) + LN_EPS)
        oh = o * r
        gn = ng_ref[...]
        sg = _sigmoid(g_ref[...])
        d = dy_ref[...]
        dyn = d * sg
        dgate_ref[...] = (d * oh * gn * sg * (1.0 - sg)).astype(dgate_ref.dtype)
        dng_ref[...] = jnp.sum(dyn * oh, axis=0, keepdims=True)
        doh = dyn * gn
        do_s[...] = (r * (doh - oh * jnp.mean(doh * oh, axis=-1, keepdims=True))).astype(do_s.dtype)
        mask = _tril_mask()

        def step(k, dSt):
            n = N - 1 - k
            rws = pl.ds(pl.multiple_of(n * C_CHUNK, C_CHUNK), C_CHUNK)
            Qd, Kd, Ke, V, dO = qd_ref[rws, :], kd_ref[rws, :], ke_ref[rws, :], vb_ref[rws, :], do_s[rws, :]
            St = st_ref[n]
            Stb = St.astype(_MM)
            dStb = dSt.astype(_MM)
            att = jnp.where(mask, _dot(Qd, Kd, _NT), 0.0).astype(_MM)
            dA = jnp.where(mask, _dot(dO, V, _NT), 0.0).astype(_MM)
            dv_ref[rws, :] = (_dot(att, dO, _TN) + _dot(Ke, dStb, _NT)).astype(dv_ref.dtype)
            dqd_ref[rws, :] = _dot(dA, Kd) + _dot(dO, Stb)
            dkd_ref[rws, :] = _dot(dA, Qd, _TN)
            dke_ref[rws, :] = _dot(V, dStb)
            ddec_ref[pl.ds(n, 1), :] = jnp.sum(dSt * St, axis=0, keepdims=True)
            return dSt * dec_ref[pl.ds(n, 1), :] + _dot(dO, Qd, _TN)

        def trip(i, state):
            for u in range(SCAN_UNROLL):
                state = step(i * SCAN_UNROLL + u, state)
            return state

        lax.fori_loop(0, N // SCAN_UNROLL, trip, jnp.zeros((C_HEAD, C_HEAD), _F32))

    col = pl.BlockSpec((T, C_HEAD), lambda j: (0, j))
    chk = pl.BlockSpec((N, C_HEAD), lambda j: (0, j))
    one = pl.BlockSpec((1, C_HEAD), lambda j: (0, j))
    return pl.pallas_call(
        body, name=name, grid=(NH,),
        in_specs=[col, col, col, col, chk, pl.BlockSpec((None, N, C_HEAD, C_HEAD), lambda j: (j, 0, 0, 0)), col,
                  pl.BlockSpec((None, T, C_HEAD), lambda j: (3, 0, j)), one, col],
        out_specs=[col, col, col, col, col, chk, one],
        out_shape=[_S((T, D), _F32)] * 3 + [_S((T, D), _MM)] * 2 + [_S((N, D), _F32), _S((1, D), _F32)],
        scratch_shapes=[pltpu.VMEM((T, C_HEAD), _MM)],
        compiler_params=_params(("parallel",)),
    )(qd, kd, ke, vb, dec, st, o, h, ng, dy)


def _hgrn_prep_bwd(h, lb, dqd, dkd, dke, dv, dgate, ddec, name):
    _, T, D = h.shape
    tr = _tile(T, 512)
    tc = _tile(D, 256)
    nch = tr // C_CHUNK

    def body(q_ref, f_ref, lb_ref, dqd_ref, dkd_ref, dke_ref, dv_ref, dgate_ref, ddec_ref, dh_ref, dlb_ref):
        @pl.when(pl.program_id(1) == 0)
        def _():
            dlb_ref[...] = jnp.zeros_like(dlb_ref)

        q = q_ref[...]
        lb = lb_ref[...]
        sq, sf, f, rin, tot3, eb, enb, ekb = _hgrn_gates(q, f_ref[...], lb, tr, tc)
        kk = 1.0 - f
        dQd, dKd, dKe = dqd_ref[...], dkd_ref[...], dke_ref[...]
        tq = dQd * eb
        tkd = dKd * enb
        tke = dKe * ekb
        ke_term = tke * kk
        db = tq * (q * sq) - tkd * kk - ke_term
        dtot3 = (jnp.sum(ke_term.reshape(nch, C_CHUNK, tc), axis=1, keepdims=True)
                 + (ddec_ref[...] * jnp.exp(tot3).reshape(nch, tc)).reshape(nch, 1, tc))
        dlogf = (_chunk_revcumsum(db, rin).reshape(nch, C_CHUNK, tc) + dtot3).reshape(tr, tc)
        df = dlogf / f - (tkd + tke)
        dh_ref[0] = (tq * sq * (1.0 + q * (1.0 - sq))).astype(dh_ref.dtype)
        dh_ref[1] = (df * (1.0 - lb) * sf * (1.0 - sf)).astype(dh_ref.dtype)
        dh_ref[2] = dv_ref[...]
        dh_ref[3] = dgate_ref[...]
        dlb_ref[...] += jnp.sum(df * (1.0 - sf), axis=0, keepdims=True)

    def part(p):
        return pl.BlockSpec((None, tr, tc), lambda j, i: (p, i, j))

    blk = pl.BlockSpec((tr, tc), lambda j, i: (i, j))
    vec = pl.BlockSpec((1, tc), lambda j, i: (0, j))
    return pl.pallas_call(
        body, name=name, grid=(D // tc, T // tr),
        in_specs=[part(0), part(1), vec, blk, blk, blk, blk, blk, pl.BlockSpec((nch, tc), lambda j, i: (i, j))],
        out_specs=[pl.BlockSpec((4, tr, tc), lambda j, i: (0, i, j)), vec],
        out_shape=[_S((4, T, D), _MM), _S((1, D), _F32)],
        compiler_params=_params(("parallel", "arbitrary")),
    )(h, h, lb, dqd, dkd, dke, dv, dgate, ddec)


def _sum_in_device_order(me1, own, land, name):
    R, C = own.shape
    tr = _tile(R, 256)

    def body(me_ref, own_ref, land_ref, o_ref):
        me = me_ref[0]
        g = None
        for j in range(N_DEV):
            slot = jnp.maximum(jnp.bitwise_xor(me, j) - 1, 0)
            p = jnp.where(me == j, own_ref[...], land_ref[slot])
            g = p if g is None else g + p
        o_ref[...] = g

    return pl.pallas_call(
        body, name=name,
        grid_spec=pltpu.PrefetchScalarGridSpec(
            num_scalar_prefetch=1, grid=(R // tr,),
            in_specs=[pl.BlockSpec((tr, C), lambda i, me: (i, 0)), pl.BlockSpec((N_DEV - 1, tr, C), lambda i, me: (0, i, 0))],
            out_specs=pl.BlockSpec((tr, C), lambda i, me: (i, 0))),
        out_shape=_S((R, C), _F32), compiler_params=_params(("parallel",)),
    )(me1, own, land)


def _adamw(parts, w, m, v, name):
    P, R, C = parts.shape
    tr = _tile(R, 128) if R % LANE == 0 else R

    def body(p_ref, w_ref, m_ref, v_ref, g_ref, d_ref, nm_ref, nv_ref):
        g = p_ref[0].astype(_F32)
        for s in range(1, P):
            g = g + p_ref[s].astype(_F32)
        nm = ADAM_B1 * m_ref[...] + (1.0 - ADAM_B1) * g
        nv = ADAM_B2 * v_ref[...] + (1.0 - ADAM_B2) * (g * g)
        m_hat = nm / (1.0 - ADAM_B1 ** ADAM_STEP)
        v_hat = nv / (1.0 - ADAM_B2 ** ADAM_STEP)
        g_ref[...] = g
        d_ref[...] = -ADAM_LR * (m_hat / (jnp.sqrt(v_hat) + ADAM_EPS) + ADAM_WD * w_ref[...])
        nm_ref[...] = nm
        nv_ref[...] = nv

    blk = pl.BlockSpec((tr, C), lambda i: (i, 0))
    return pl.pallas_call(
        body, name=name, grid=(R // tr,), in_specs=[pl.BlockSpec((P, tr, C), lambda i: (0, i, 0)), blk, blk, blk],
        out_specs=[blk] * 4, out_shape=[_S((R, C), _F32)] * 4, compiler_params=_params(("parallel",)),
    )(parts, w, m, v)


def _exchange(name, srcs, out_shapes, jobs, deps=()):
    ns, nj = len(srcs), len(jobs)

    nd = len(deps)

    def body(*refs):
        ins, outs = refs[:ns], refs[ns + nd:ns + nd + len(out_shapes)]
        send_sems, recv_sems, local_sems = refs[-3:]
        x, y, c = lax.axis_index("x"), lax.axis_index("y"), lax.axis_index("c")
        me = 4 * x + 2 * y + c
        local = []
        for ji, (si, src_fn, di, dst_fn) in enumerate(jobs):
            cp = pltpu.make_async_copy(src_fn(ins[si], me, me), dst_fn(outs[di], me), local_sems.at[ji])
            cp.start()
            local.append(cp)
        remote = []
        for k in range(1, N_DEV):
            px, py, pc = (x + (k >> 2)) % 2, (y + ((k >> 1) & 1)) % 2, (c + (k & 1)) % 2
            to = 4 * px + 2 * py + pc
            for ji, (si, src_fn, di, dst_fn) in enumerate(jobs):
                sem = (k - 1) * nj + ji
                cp = pltpu.make_async_remote_copy(
                    src_ref=src_fn(ins[si], me, to), dst_ref=dst_fn(outs[di], me),
                    send_sem=send_sems.at[sem], recv_sem=recv_sems.at[sem],
                    device_id=(px, py, pc), device_id_type=pl.DeviceIdType.MESH)
                cp.start()
                remote.append(cp)
        for cp in remote:
            cp.wait_recv()
        for cp in remote:
            cp.wait_send()
        for cp in local:
            cp.wait()

    hbm = pl.BlockSpec(memory_space=pltpu.HBM)
    return pl.pallas_call(
        body, name=name, in_specs=[hbm] * ns + [_ANY] * nd, out_specs=[hbm] * len(out_shapes), out_shape=list(out_shapes),
        scratch_shapes=[pltpu.SemaphoreType.DMA(((N_DEV - 1) * nj,)), pltpu.SemaphoreType.DMA(((N_DEV - 1) * nj,)),
                        pltpu.SemaphoreType.DMA((nj,))],
    )(*srcs, *deps)


def _whole(ref, me, to):
    return ref


def _slot_job(i, o):
    def dst(ref, me):
        return ref.at[me]
    return (i, _whole, o, dst)


_HBM = pl.BlockSpec(memory_space=pltpu.HBM)
_SEM = pl.BlockSpec(memory_space=pltpu.SEMAPHORE)
_ANY = pl.BlockSpec(memory_space=pl.ANY)
_N_PEER = N_DEV - 1


def _split_params():
    return pltpu.CompilerParams(has_side_effects=pltpu.SideEffectType.DATAFLOW_SIDE_EFFECTING)


def _blk(ref, axis, n, idx):
    if axis is None:
        return ref
    return ref.at[tuple([slice(None)] * axis + [pl.ds(pl.multiple_of(idx * n, n), n)])]


def _peer(k):
    x, y, c = lax.axis_index("x"), lax.axis_index("y"), lax.axis_index("c")
    px, py, pc = (x + (k >> 2)) % 2, (y + ((k >> 1) & 1)) % 2, (c + (k & 1)) % 2
    return (px, py, pc), 4 * px + 2 * py + pc, 4 * x + 2 * y + c


def _row_tile(rows, pref):
    best = None
    for d in range(16, min(rows, pref) + 1, 16):
        if rows % d == 0:
            best = d
    return best if best is not None else rows


def _place(w, me1, axis, name, layer=None, deps=()):
    R, C = w.shape[-2:]
    tr = _row_tile(R, 512)
    nb = R // tr
    lead = () if layer is None else (None,)
    pre = () if layer is None else (layer,)

    def body(me_ref, w_ref, *rest):
        rest[-1][...] = w_ref[...].astype(rest[-1].dtype)

    if axis == 1:
        out_spec = pl.BlockSpec((tr, C), lambda i, me: (i, me[0]))
        out_shape = _S((R, N_DEV * C), _MM)
    else:
        out_spec = pl.BlockSpec((tr, C), lambda i, me: (me[0] * nb + i, 0))
        out_shape = _S((N_DEV * R, C), _MM)
    return pl.pallas_call(
        body, name=name,
        grid_spec=pltpu.PrefetchScalarGridSpec(
            num_scalar_prefetch=1, grid=(nb,),
            in_specs=[pl.BlockSpec(lead + (tr, C), lambda i, me: pre + (i, 0))] + [_ANY] * len(deps), out_specs=out_spec),
        out_shape=out_shape, compiler_params=_params(("parallel",)),
    )(me1, w, *deps)


_SIBLING = 1
_CHIPS = (2, 4, 6)
_VMEM_TOKEN = pl.BlockSpec(memory_space=pltpu.VMEM)


def _remote(ref_blk, send_sem, recv_sem, dev):
    return pltpu.make_async_remote_copy(src_ref=ref_blk, dst_ref=ref_blk, send_sem=send_sem, recv_sem=recv_sem,
                                        device_id=dev, device_id_type=pl.DeviceIdType.MESH)


def _gather_start(name, full, axis, n):
    def body(f_ref, send, recv, f_out, token):
        for i, k in enumerate((_SIBLING,) + _CHIPS):
            dev, _, me = _peer(k)
            _remote(_blk(f_ref, axis, n, me), send.at[i], recv.at[i], dev).start()
        token[...] = jnp.zeros_like(token)

    return pl.pallas_call(
        body, name=name,
        out_shape=(pltpu.SemaphoreType.DMA((4,)), pltpu.SemaphoreType.DMA((4,)), pltpu.HBM(full.shape, full.dtype),
                   _S((8, LANE), _F32)),
        in_specs=(_HBM,), out_specs=(_SEM, _SEM, _HBM, _VMEM_TOKEN),
        input_output_aliases={0: 2}, compiler_params=_split_params(),
    )(pltpu.with_memory_space_constraint(full, pltpu.HBM))


def _gather_forward(name, full, axis, n, recv, after):
    def body(f_ref, recv_r, after_ref, send2, recv2, f_out, token):
        sib, _, _ = _peer(_SIBLING)
        for i, k in enumerate(_CHIPS):
            dev, frm, _ = _peer(k)
            blk = _blk(f_ref, axis, n, frm)
            _remote(blk, send2.at[i], recv_r.at[1 + i], dev).wait_recv()
            _remote(blk, send2.at[i], recv2.at[i], sib).start()
        token[...] = jnp.zeros_like(token)

    return pl.pallas_call(
        body, name=name,
        out_shape=(pltpu.SemaphoreType.DMA((3,)), pltpu.SemaphoreType.DMA((3,)), pltpu.HBM(full.shape, full.dtype),
                   _S((8, LANE), _F32)),
        in_specs=(_HBM, _SEM, _ANY), out_specs=(_SEM, _SEM, _HBM, _VMEM_TOKEN),
        input_output_aliases={0: 2}, compiler_params=_split_params(),
    )(full, recv, after)


def _gather_wait(name, full, axis, n, send, recv, send2, recv2, after):
    def body(f_ref, send_r, recv_r, send2_r, recv2_r, after_ref, f_out):
        sib, _, me = _peer(_SIBLING)
        blk = _blk(f_ref, axis, n, me)
        for i in range(4):
            _remote(blk, send_r.at[i], recv_r.at[0], sib).wait_send()
        _remote(blk, send_r.at[0], recv_r.at[0], sib).wait_recv()
        for i in range(3):
            cp = _remote(blk, send2_r.at[i], recv2_r.at[i], sib)
            cp.wait_send()
            cp.wait_recv()

    return pl.pallas_call(
        body, name=name, out_shape=pltpu.HBM(full.shape, full.dtype),
        in_specs=(_HBM, _SEM, _SEM, _SEM, _SEM, _ANY), out_specs=_HBM,
        input_output_aliases={0: 0}, compiler_params=_split_params(),
    )(full, send, recv, send2, recv2, after)


def _scatter_start(name, dw, axis, n):
    shard = tuple(n if a == axis else d for a, d in enumerate(dw.shape))
    land = lax.empty((_N_PEER,) + shard, dw.dtype)

    def body(dw_ref, land_ref, send, recv, dw_out, land_out, token):
        for k in range(1, N_DEV):
            dev, to, _ = _peer(k)
            pltpu.make_async_remote_copy(
                src_ref=_blk(dw_ref, axis, n, to), dst_ref=land_ref.at[k - 1], send_sem=send.at[k - 1],
                recv_sem=recv.at[k - 1], device_id=dev, device_id_type=pl.DeviceIdType.MESH).start()
        token[...] = jnp.zeros_like(token)

    return pl.pallas_call(
        body, name=name,
        out_shape=(pltpu.SemaphoreType.DMA((_N_PEER,)), pltpu.SemaphoreType.DMA((_N_PEER,)),
                   pltpu.HBM(dw.shape, dw.dtype), pltpu.HBM(land.shape, land.dtype), _S((8, LANE), _F32)),
        in_specs=(_HBM, _HBM), out_specs=(_SEM, _SEM, _HBM, _HBM, pl.BlockSpec(memory_space=pltpu.VMEM)),
        input_output_aliases={0: 2, 1: 3}, compiler_params=_split_params(),
    )(pltpu.with_memory_space_constraint(dw, pltpu.HBM), pltpu.with_memory_space_constraint(land, pltpu.HBM))


def _scatter_wait(name, dw, land, send, recv, axis, n, after):
    def body(dw_ref, land_ref, send_r, recv_r, after_ref, dw_out, land_out):
        for k in range(1, N_DEV):
            dev, to, _ = _peer(k)
            cp = pltpu.make_async_remote_copy(
                src_ref=_blk(dw_ref, axis, n, to), dst_ref=land_ref.at[k - 1], send_sem=send_r.at[k - 1],
                recv_sem=recv_r.at[k - 1], device_id=dev, device_id_type=pl.DeviceIdType.MESH)
            cp.wait_send()
            cp.wait_recv()

    return pl.pallas_call(
        body, name=name, out_shape=(pltpu.HBM(dw.shape, dw.dtype), pltpu.HBM(land.shape, land.dtype)),
        in_specs=(_HBM, _HBM, _SEM, _SEM, _ANY), out_specs=(_HBM, _HBM), input_output_aliases={0: 0, 1: 1},
        compiler_params=_split_params(),
    )(dw, land, send, recv, after)


def _adamw_big(me1, dw, land, w, m, v, axis, n, name, layer=None, into=None):
    R, C = land.shape[1:]
    tr = _row_tile(R, 128)
    nb = R // tr
    lead = () if layer is None else (None,)
    pre = () if layer is None else (layer,)

    def body(me_ref, own_ref, land_ref, w_ref, m_ref, v_ref, *rest):
        g_ref, d_ref, nm_ref, nv_ref = rest[-4:]
        g = own_ref[...].astype(_F32)
        for s in range(_N_PEER):
            g = g + land_ref[s].astype(_F32)
        nm = ADAM_B1 * m_ref[...] + (1.0 - ADAM_B1) * g
        nv = ADAM_B2 * v_ref[...] + (1.0 - ADAM_B2) * (g * g)
        m_hat = nm / (1.0 - ADAM_B1 ** ADAM_STEP)
        v_hat = nv / (1.0 - ADAM_B2 ** ADAM_STEP)
        g_ref[...] = g
        d_ref[...] = -ADAM_LR * (m_hat / (jnp.sqrt(v_hat) + ADAM_EPS) + ADAM_WD * w_ref[...])
        nm_ref[...] = nm
        nv_ref[...] = nv

    if axis == 1:
        own_spec = pl.BlockSpec((tr, C), lambda i, me: (i, me[0]))
    else:
        own_spec = pl.BlockSpec((tr, C), lambda i, me: (me[0] * nb + i, 0))
    blk = pl.BlockSpec(lead + (tr, C), lambda i, me: pre + (i, 0))
    in_specs = [own_spec, pl.BlockSpec((_N_PEER, tr, C), lambda i, me: (0, i, 0)), blk, blk, blk]
    args = [me1, dw, land, w, m, v]
    aliases = {}
    if into is not None:
        in_specs += [_ANY] * 4
        aliases = {6 + j: j for j in range(4)}
        args += list(into)
    return pl.pallas_call(
        body, name=name,
        grid_spec=pltpu.PrefetchScalarGridSpec(num_scalar_prefetch=1, grid=(nb,), in_specs=in_specs, out_specs=[blk] * 4),
        out_shape=[_S(w.shape, _F32)] * 4, input_output_aliases=aliases, compiler_params=_params(("parallel",)),
    )(*args)


def _pack(arrs):
    flat = jnp.concatenate([a.reshape(-1).astype(_F32) for a in arrs])
    pad = (-flat.shape[0]) % (LANE * LANE)
    return jnp.pad(flat, (0, pad)).reshape(-1, LANE)


def _unpack(mat, shapes):
    flat = mat.reshape(-1)
    out, off = [], 0
    for s in shapes:
        n = 1
        for d in s:
            n *= d
        out.append(flat[off:off + n].reshape(s))
        off += n
    return out


def _lb_of(lb_param):
    lb_all = jnp.cumsum(jax.nn.softmax(lb_param.astype(_F32), axis=0), axis=0)
    return (lb_all - lb_all[0])[1:2]


def kernel(x, ev_w_in, ev_ln_v_g, ev_ln_v_b, ev_w_s, ev_b_s, ev_w_pool, ev_pool_scale, ev_w_out, od_w_in, od_norm_g, od_w_out, lb_param, ffn_w_up, ffn_conv_w, ffn_conv_b, ffn_w_down, ln1_g, ln1_b, ln2_g, ln2_b, loss_target, m_ev_w_in, m_ev_ln_v_g, m_ev_ln_v_b, m_ev_w_s, m_ev_b_s, m_ev_w_pool, m_ev_pool_scale, m_ev_w_out, m_od_w_in, m_od_norm_g, m_od_w_out, m_lb_param, m_ffn_w_up, m_ffn_conv_w, m_ffn_conv_b, m_ffn_w_down, m_ln1_g, m_ln1_b, m_ln2_g, m_ln2_b, v_ev_w_in, v_ev_ln_v_g, v_ev_ln_v_b, v_ev_w_s, v_ev_b_s, v_ev_w_pool, v_ev_pool_scale, v_ev_w_out, v_od_w_in, v_od_norm_g, v_od_w_out, v_lb_param, v_ffn_w_up, v_ffn_conv_w, v_ffn_conv_b, v_ffn_w_down, v_ln1_g, v_ln1_b, v_ln2_g, v_ln2_b):
    me = 4 * lax.axis_index("x") + 2 * lax.axis_index("y") + lax.axis_index("c")
    T, D = x.shape[1], x.shape[2]
    W = ev_ln_v_g.shape[1]
    H = W // A_HEAD
    Wg = W // B_GROUPS
    F2 = ffn_conv_b.shape[1]
    F = F2 // 2
    n_in0, n_out0 = ev_w_in.shape[2], ev_w_out.shape[1]
    n_in1, n_out1 = od_w_in.shape[2], od_w_out.shape[1]
    n_up, n_dn = ffn_w_up.shape[2], ffn_w_down.shape[1]
    n_pool, n_ng, n_cw = ev_w_pool.shape[2], od_norm_g.shape[1], ffn_conv_w.shape[2]

    small_shards = [od_norm_g, ffn_conv_w, ev_w_pool]
    small_pack = _pack(small_shards)
    small_all = _exchange("gather_small_params", [small_pack], [_S((N_DEV,) + small_pack.shape, _F32)], [_slot_job(0, 0)])[0]

    me1 = me.astype(jnp.int32).reshape(1)
    weights = [
        ("w_in0", ev_w_in[0], None, 1, n_in0), ("w_out0", ev_w_out[0], None, 0, n_out0),
        ("w_up0", ffn_w_up, 0, 1, n_up), ("w_dn0", ffn_w_down, 0, 0, n_dn),
        ("w_in1", od_w_in[0], None, 1, n_in1), ("w_out1", od_w_out[0], None, 0, n_out1),
        ("w_up1", ffn_w_up, 1, 1, n_up), ("w_dn1", ffn_w_down, 1, 0, n_dn),
    ]
    started, tokens = {}, [small_all]
    for key, w, layer, axis, n in weights:
        full = _place(w, me1, axis, "place_" + key, layer, deps=tokens)
        send, recv, full, token = _gather_start("gather_start_" + key, full, axis, n)
        started[key] = (full, axis, n, send, recv)
        tokens = [token]

    def pass_on(key, after):
        full, axis, n, send, recv = started[key]
        send2, recv2, full, token = _gather_forward("gather_forward_" + key, full, axis, n, recv, after)
        started[key] = (full, axis, n, send, recv, send2, recv2)
        return token

    def gathered(key, after):
        return _gather_wait("gather_wait_" + key, *started[key], after)

    ng_parts, cw_parts, wp_parts = [], [], []
    for j in range(N_DEV):
        a, b, c = _unpack(small_all[j], [s.shape for s in small_shards])
        ng_parts.append(a)
        cw_parts.append(b)
        wp_parts.append(c)
    norm_g = jnp.concatenate(ng_parts, axis=1)
    conv_w = jnp.concatenate(cw_parts, axis=2)
    w_pool = jnp.concatenate(wp_parts, axis=2)[0]
    cw_l = [conv_w[l].reshape(3, 2, F).transpose(1, 0, 2) for l in range(DEPTH)]
    cb_l = [ffn_conv_b[l].reshape(2, 1, F) for l in range(DEPTH)]
    ws_tril = jnp.tril(ev_w_s[0]).astype(_MM)
    bias = jnp.repeat(ev_b_s[0].T, A_HEAD, axis=1)
    wp_b = w_pool.astype(_MM)
    lb, lb_vjp = jax.vjp(_lb_of, lb_param)

    x2 = x[0]
    xb = _cast(x2, _MM, "cast_x", deps=[pass_on("w_in0", tokens[0])])
    w_in0 = gathered("w_in0", xb)
    h0 = _mm(xb, w_in0, "nn", _F32, "ev_in", out_parts=3)
    tie = pass_on("w_out0", h0)
    yab = _ev_mid_fwd(h0, ev_ln_v_g + tie[0, 0], ev_ln_v_b, ws_tril, bias, wp_b, ev_pool_scale, "ev_mid_fwd")
    w_out0 = gathered("w_out0", yab)
    z1 = _mm(yab, w_out0, "nn", _F32, "ev_out", add=x2, add_scale=ALPHA)
    tie = pass_on("w_up0", z1)
    x1, x1b = _ln_fwd(z1, ln1_g[0:1] + tie[0, 0], ln1_b[0:1], "ln1_0")
    w_up0 = gathered("w_up0", x1b)
    hf0 = _mm(x1b, w_up0, "nn", _F32, "ffn_up", out_parts=2)
    tie = pass_on("w_dn0", hf0)
    act0 = _ffn_mid_fwd(hf0, cw_l[0], cb_l[0] + tie[0, 0], "ffn_mid_fwd")
    w_dn0 = gathered("w_dn0", act0)
    z2 = _mm(act0, w_dn0, "nn", _F32, "ffn_down", add=x1, add_scale=ALPHA)
    tie = pass_on("w_in1", z2)
    x2_, x2b = _ln_fwd(z2, ln2_g[0:1] + tie[0, 0], ln2_b[0:1], "ln2_0")
    w_in1 = gathered("w_in1", x2b)
    h1 = _mm(x2b, w_in1, "nn", _F32, "od_in", out_parts=4)
    qd, kd, ke, vb, dec = _hgrn_prep_fwd(h1, lb, "hgrn_prep_fwd")
    tie = pass_on("w_out1", qd)
    o, yo, st = _hgrn_scan_fwd(qd, kd, ke, vb, dec, h1, norm_g + tie[0, 0], "hgrn_scan_fwd")
    w_out1 = gathered("w_out1", yo)
    z3 = _mm(yo, w_out1, "nn", _F32, "od_out", add=x2_, add_scale=ALPHA)
    tie = pass_on("w_up1", z3)
    x3, x3b = _ln_fwd(z3, ln1_g[1:2] + tie[0, 0], ln1_b[1:2], "ln1_1")
    w_up1 = gathered("w_up1", x3b)
    hf1 = _mm(x3b, w_up1, "nn", _F32, "ffn_up", out_parts=2)
    tie = pass_on("w_dn1", hf1)
    act1 = _ffn_mid_fwd(hf1, cw_l[1], cb_l[1] + tie[0, 0], "ffn_mid_fwd")
    w_dn1 = gathered("w_dn1", act1)
    z4 = _mm(act1, w_dn1, "nn", _F32, "ffn_down", add=x3, add_scale=ALPHA)

    scat = {}

    def scatter(key, dw, axis, n):
        send, recv, dw, land, token = _scatter_start("scatter_start_" + key, dw, axis, n)
        scat[key] = (dw, land, send, recv, axis, n)
        return [token]

    loss11, dz4, dz4b, g_ln2_1, b_ln2_1 = _ln_loss_bwd(z4, ln2_g[1:2], ln2_b[1:2], loss_target[0], "ln_loss_bwd")
    tok = scatter("dn1", _mm(act1, dz4b, "tn", _XCH, "ffn_down_dw"), 0, n_dn)
    dact1 = _mm(dz4b, w_dn1, "nt", _F32, "ffn_down_dx", deps=tok)
    dhf1, dcw1, dcb1 = _ffn_mid_bwd(hf1, dact1, cw_l[1], cb_l[1], "ffn_mid_bwd")
    tok = scatter("up1", _mm(x3b, dhf1, "tn", _XCH, "ffn_up_dw", b_parts=2, deps=tok), 1, n_up)
    dx3 = _mm(dhf1, w_up1, "nt", _F32, "ffn_up_dx", a_parts=2, add=dz4, add_scale=ALPHA, deps=tok)
    dz3, dz3b, g_ln1_1, b_ln1_1 = _ln_bwd(z3, ln1_g[1:2], dx3, "ln_bwd")
    tok = scatter("out1", _mm(yo, dz3b, "tn", _XCH, "od_out_dw", deps=tok), 0, n_out1)
    dyo = _mm(dz3b, w_out1, "nt", _F32, "od_out_dx", deps=tok)
    dqd, dkd, dke, dv, dgate, ddec, dng = _hgrn_scan_bwd(qd, kd, ke, vb, dec, st, o, h1, norm_g, dyo, "hgrn_scan_bwd")
    dh1, dlb = _hgrn_prep_bwd(h1, lb, dqd, dkd, dke, dv, dgate, ddec, "hgrn_prep_bwd")
    tok = scatter("in1", _mm(x2b, dh1, "tn", _XCH, "od_in_dw", b_parts=4, deps=tok), 1, n_in1)
    dx2 = _mm(dh1, w_in1, "nt", _F32, "od_in_dx", a_parts=4, add=dz3, add_scale=ALPHA, deps=tok)
    dz2, dz2b, g_ln2_0, b_ln2_0 = _ln_bwd(z2, ln2_g[0:1], dx2, "ln_bwd")
    tok = scatter("dn0", _mm(act0, dz2b, "tn", _XCH, "ffn_down_dw", deps=tok), 0, n_dn)
    dact0 = _mm(dz2b, w_dn0, "nt", _F32, "ffn_down_dx", deps=tok)
    dhf0, dcw0, dcb0 = _ffn_mid_bwd(hf0, dact0, cw_l[0], cb_l[0], "ffn_mid_bwd")
    tok = scatter("up0", _mm(x1b, dhf0, "tn", _XCH, "ffn_up_dw", b_parts=2, deps=tok), 1, n_up)
    dx1 = _mm(dhf0, w_up0, "nt", _F32, "ffn_up_dx", a_parts=2, add=dz2, add_scale=ALPHA, deps=tok)
    dz1, dz1b, g_ln1_0, b_ln1_0 = _ln_bwd(z1, ln1_g[0:1], dx1, "ln_bwd")
    tok = scatter("out0", _mm(yab, dz1b, "tn", _XCH, "ev_out_dw", deps=tok), 0, n_out0)
    dyab = _mm(dz1b, w_out0, "nt", _F32, "ev_out_dx", deps=tok)
    dh0, dws, dbias, dlng, dlnb, dwp, dsc = _ev_mid_bwd(h0, dyab, ev_ln_v_g, ev_ln_v_b, ws_tril, bias, wp_b,
                                                        ev_pool_scale, "ev_mid_bwd")

    g_b_s = dbias.reshape(A_CHUNK, H, A_HEAD).sum(axis=-1).T[None]
    g_conv_w = jnp.stack([d.transpose(1, 0, 2).reshape(3, F2) for d in (dcw0, dcw1)])
    g_conv_b = jnp.stack([d.reshape(F2) for d in (dcb0, dcb1)])
    small_names = ["ev_ln_v_g", "ev_ln_v_b", "ev_w_s", "ev_b_s", "ev_w_pool", "ev_pool_scale", "od_norm_g", "lb_param",
                   "ffn_conv_w", "ffn_conv_b", "ln1_g", "ln1_b", "ln2_g", "ln2_b"]
    small_grads = [dlng, dlnb, dws[None], g_b_s, dwp[None], dsc, dng, lb_vjp(dlb)[0], g_conv_w, g_conv_b,
                   jnp.concatenate([g_ln1_0, g_ln1_1]), jnp.concatenate([b_ln1_0, b_ln1_1]),
                   jnp.concatenate([g_ln2_0, g_ln2_1]), jnp.concatenate([b_ln2_0, b_ln2_1])]
    full_shapes = [g.shape for g in small_grads]
    tok = scatter("in0", _mm(xb, dh0, "tn", _XCH, "ev_in_dw", b_parts=3, deps=tok), 1, n_in0)
    tok = scatter("small", _pack(small_grads) + tok[0][0, 0], None, None)
    grad_x = _mm(dh0, w_in0, "nt", _F32, "ev_in_dx", a_parts=3, add=dz1, add_scale=ALPHA, deps=tok)

    def landed(key, after):
        dw, land, send, recv, axis, n = scat[key]
        dw, land = _scatter_wait("scatter_wait_" + key, dw, land, send, recv, axis, n, after)
        return me1, dw, land

    big = {}
    r_dn = _adamw_big(*landed("dn1", grad_x), ffn_w_down, m_ffn_w_down, v_ffn_w_down, 0, n_dn, "adamw_w_dn1", layer=1)
    r_up = _adamw_big(*landed("up1", r_dn[0]), ffn_w_up, m_ffn_w_up, v_ffn_w_up, 1, n_up, "adamw_w_up1", layer=1)
    big["od_w_out"] = _adamw_big(*landed("out1", r_up[0]), od_w_out[0], m_od_w_out[0], v_od_w_out[0], 0, n_out1, "adamw_w_out1")
    big["od_w_in"] = _adamw_big(*landed("in1", big["od_w_out"][0]), od_w_in[0], m_od_w_in[0], v_od_w_in[0], 1, n_in1, "adamw_w_in1")
    big["ffn_w_down"] = _adamw_big(*landed("dn0", big["od_w_in"][0]), ffn_w_down, m_ffn_w_down, v_ffn_w_down, 0, n_dn,
                                   "adamw_w_dn0", layer=0, into=r_dn)
    big["ffn_w_up"] = _adamw_big(*landed("up0", big["ffn_w_down"][0]), ffn_w_up, m_ffn_w_up, v_ffn_w_up, 1, n_up,
                                 "adamw_w_up0", layer=0, into=r_up)
    big["ev_w_out"] = _adamw_big(*landed("out0", big["ffn_w_up"][0]), ev_w_out[0], m_ev_w_out[0], v_ev_w_out[0], 0, n_out0, "adamw_w_out0")
    big["ev_w_in"] = _adamw_big(*landed("in0", big["ev_w_out"][0]), ev_w_in[0], m_ev_w_in[0], v_ev_w_in[0], 1, n_in0, "adamw_w_in0")

    gsum = _unpack(_sum_in_device_order(*landed("small", big["ev_w_in"][0]), "sum_small_grads"), full_shapes)
    given = dict(ev_ln_v_g=(ev_ln_v_g, m_ev_ln_v_g, v_ev_ln_v_g), ev_ln_v_b=(ev_ln_v_b, m_ev_ln_v_b, v_ev_ln_v_b),
                 ev_w_s=(ev_w_s, m_ev_w_s, v_ev_w_s), ev_b_s=(ev_b_s, m_ev_b_s, v_ev_b_s),
                 ev_w_pool=(ev_w_pool, m_ev_w_pool, v_ev_w_pool),
                 ev_pool_scale=(ev_pool_scale, m_ev_pool_scale, v_ev_pool_scale),
                 od_norm_g=(od_norm_g, m_od_norm_g, v_od_norm_g), lb_param=(lb_param, m_lb_param, v_lb_param),
                 ffn_conv_w=(ffn_conv_w, m_ffn_conv_w, v_ffn_conv_w), ffn_conv_b=(ffn_conv_b, m_ffn_conv_b, v_ffn_conv_b),
                 ln1_g=(ln1_g, m_ln1_g, v_ln1_g), ln1_b=(ln1_b, m_ln1_b, v_ln1_b), ln2_g=(ln2_g, m_ln2_g, v_ln2_g),
                 ln2_b=(ln2_b, m_ln2_b, v_ln2_b))
    shard_axis = dict(ev_w_pool=(2, n_pool), od_norm_g=(1, n_ng), ffn_conv_w=(2, n_cw))
    local_g = []
    for name, g in zip(small_names, gsum):
        if name in shard_axis:
            ax, n = shard_axis[name]
            g = lax.dynamic_slice_in_dim(g, me * n, n, axis=ax)
        local_g.append(g)
    local_shapes = [g.shape for g in local_g]
    res = _adamw(_pack(local_g)[None], _pack([given[n][0] for n in small_names]), _pack([given[n][1] for n in small_names]),
                 _pack([given[n][2] for n in small_names]), "adamw_small")
    small = {n: [] for n in small_names}
    for r in res:
        for n, a in zip(small_names, _unpack(r, local_shapes)):
            small[n].append(a)

    loss = lax.psum(loss11[0, 0], ("x", "y", "c"))
    order = ["ev_w_in", "ev_ln_v_g", "ev_ln_v_b", "ev_w_s", "ev_b_s", "ev_w_pool", "ev_pool_scale", "ev_w_out", "od_w_in",
             "od_norm_g", "od_w_out", "lb_param", "ffn_w_up", "ffn_conv_w", "ffn_conv_b", "ffn_w_down", "ln1_g", "ln1_b",
             "ln2_g", "ln2_b"]
    shapes = dict(ev_w_in=ev_w_in.shape, ev_w_out=ev_w_out.shape, od_w_in=od_w_in.shape, od_w_out=od_w_out.shape,
                  ffn_w_up=ffn_w_up.shape, ffn_w_down=ffn_w_down.shape)
    outs = [loss, grad_x[None]]
    for kind in range(4):
        for n in order:
            outs.append(big[n][kind].reshape(shapes[n]) if n in big else small[n][kind])
    return tuple(outs)
```

```python
import functools

import jax
import jax.numpy as jnp
from jax import lax
from jax.experimental import pallas as pl
from jax.experimental.pallas import tpu as pltpu

_MM = jnp.bfloat16
_XCH = jnp.bfloat16

DEPTH = 2
ALPHA = (2 * DEPTH) ** 0.25
LN_EPS = 1e-5
A_CHUNK = 128
A_HEAD = 128
B_GROUPS = 4
POOL_HALO = 16
C_CHUNK = 64
C_HEAD = 128
SCAN_UNROLL = 8
CONV_HALO = 8
FFN_ROWS, FFN_FWD_COLS = 256, 1408
FFN_BWD_ROWS, FFN_BWD_COLS = 512, 512
ADAM_LR, ADAM_B1, ADAM_B2, ADAM_EPS, ADAM_WD, ADAM_STEP = 0.001, 0.9, 0.999, 1e-08, 0.01, 10
N_DEV = 8
LANE = 128
VMEM_LIMIT = 56 * 1024 * 1024
MM_FULL_K = 2048
MM_FULL_K_TN = 4096
MM_DEEP_K = 2816

_F32 = jnp.float32
_NN = (((1,), (0,)), ((), ()))
_NT = (((1,), (1,)), ((), ()))
_TN = (((0,), (0,)), ((), ()))
_S = jax.ShapeDtypeStruct


def _dot(a, b, dims=_NN):
    return lax.dot_general(a, b, dims, preferred_element_type=_F32)


def _tile(dim, pref):
    best = None
    d = LANE
    while d <= min(dim, pref):
        if dim % d == 0:
            best = d
        d += LANE
    return best if best is not None else dim


def _params(sem):
    return pltpu.CompilerParams(dimension_semantics=sem, vmem_limit_bytes=VMEM_LIMIT)


def _sigmoid(x):
    return 0.5 * jnp.tanh(0.5 * x) + 0.5


def _sigmoid_rel(x):
    return 1.0 / (1.0 + jnp.exp(-x))


_GELU_C = 0.7978845608028654
_GELU_A = 0.044715


def _gelu_and_grad(x):
    t = jnp.tanh(_GELU_C * (x + _GELU_A * x * x * x))
    y = 0.5 * x * (1.0 + t)
    dy = 0.5 * (1.0 + t) + 0.5 * x * (1.0 - t * t) * _GELU_C * (1.0 + 3.0 * _GELU_A * x * x)
    return y, dy


def _row_index(n):
    return lax.broadcasted_iota(jnp.int32, (n, 1), 0)


def _mm_tiles(mode, M, N, K, with_add):
    if mode == "tn":
        return _tile(M, 1024), _tile(N, 1024), _tile(K, MM_FULL_K_TN)
    if K <= MM_FULL_K:
        return _tile(M, 1024 if with_add else 2048), _tile(N, 1024 if mode == "nn" else 512), K
    return _tile(M, 1024), _tile(N, 1024), _tile(K, MM_DEEP_K)


def _mm(a, b, mode, out_dtype, name, *, a_parts=1, b_parts=1, out_parts=1, add=None, add_scale=1.0, deps=(), tiles=None):
    if mode == "nn":
        M, K = a.shape
        N = b.shape[1]
    elif mode == "nt":
        if a_parts > 1:
            M, K = a.shape[1], a.shape[2] * a_parts
        else:
            M, K = a.shape
        N = b.shape[0]
    else:
        K, M = a.shape
        N = b.shape[-1] * b_parts
    tm, tn, tk = tiles if tiles is not None else _mm_tiles(mode, M, N // max(b_parts, out_parts), K // a_parts, add is not None)
    nk = K // tk
    npj = (N // max(b_parts, out_parts)) // tn
    nkp = (K // a_parts) // tk
    if mode == "nn":
        a_spec = pl.BlockSpec((tm, tk), lambda i, j, k: (i, k))
        b_spec = pl.BlockSpec((tk, tn), lambda i, j, k: (k, j))
        dims = _NN
    elif mode == "nt":
        if a_parts > 1:
            a_spec = pl.BlockSpec((None, tm, tk), lambda i, j, k: (k // nkp, i, k % nkp))
        else:
            a_spec = pl.BlockSpec((tm, tk), lambda i, j, k: (i, k))
        b_spec = pl.BlockSpec((tn, tk), lambda i, j, k: (j, k))
        dims = _NT
    else:
        a_spec = pl.BlockSpec((tk, tm), lambda i, j, k: (k, i))
        if b_parts > 1:
            b_spec = pl.BlockSpec((None, tk, tn), lambda i, j, k: (j // npj, k, j % npj))
        else:
            b_spec = pl.BlockSpec((tk, tn), lambda i, j, k: (k, j))
        dims = _TN
    if out_parts > 1:
        out_spec = pl.BlockSpec((None, tm, tn), lambda i, j, k: (j // npj, i, j % npj))
        out_shape = _S((out_parts, M, N // out_parts), out_dtype)
    else:
        out_spec = pl.BlockSpec((tm, tn), lambda i, j, k: (i, j))
        out_shape = _S((M, N), out_dtype)
    in_specs = [a_spec, b_spec]
    args = [a, b]
    if add is not None:
        in_specs.append(pl.BlockSpec((tm, tn), lambda i, j, k: (i, j)))
        args.append(add)
    in_specs += [_ANY] * len(deps)
    args += list(deps)

    def finish(r, refs, o_ref):
        if add is not None:
            r = r + add_scale * refs[2][...]
        o_ref[...] = r.astype(o_ref.dtype)

    def body_one(*refs):
        finish(_dot(refs[0][...], refs[1][...], dims), refs, refs[-1])

    def body_acc(*refs):
        o_ref, acc = refs[-2], refs[-1]
        k = pl.program_id(2)

        @pl.when(k == 0)
        def _():
            acc[...] = jnp.zeros_like(acc)

        acc[...] += _dot(refs[0][...], refs[1][...], dims)

        @pl.when(k == nk - 1)
        def _():
            finish(acc[...], refs, o_ref)

    return pl.pallas_call(
        body_one if nk == 1 else body_acc, name=name, grid=(M // tm, N // tn, nk), in_specs=in_specs,
        out_specs=out_spec, out_shape=out_shape,
        scratch_shapes=[] if nk == 1 else [pltpu.VMEM((tm, tn), _F32)],
        compiler_params=_params(("parallel", "parallel", "arbitrary")),
    )(*args)


def _cast(x2d, dtype, name, deps=()):
    R, C = x2d.shape
    tr = _tile(R, 512) if R % LANE == 0 else R

    def body(x_ref, *rest):
        rest[-1][...] = x_ref[...].astype(rest[-1].dtype)

    return pl.pallas_call(
        body, name=name, grid=(R // tr,), in_specs=[pl.BlockSpec((tr, C), lambda i: (i, 0))] + [_ANY] * len(deps),
        out_specs=pl.BlockSpec((tr, C), lambda i: (i, 0)), out_shape=_S((R, C), dtype),
        compiler_params=_params(("parallel",)),
    )(x2d, *deps)


def _ln_fwd(z, g, b, name):
    T, D = z.shape
    tr = _tile(T, 256)

    def body(z_ref, g_ref, b_ref, y_ref, yb_ref):
        zz = z_ref[...]
        mu = jnp.mean(zz, axis=-1, keepdims=True)
        zc = zz - mu
        var = jnp.mean(zc * zc, axis=-1, keepdims=True)
        y = zc * lax.rsqrt(var + LN_EPS) * g_ref[...] + b_ref[...]
        y_ref[...] = y
        yb_ref[...] = y.astype(yb_ref.dtype)

    row = pl.BlockSpec((tr, D), lambda i: (i, 0))
    vec = pl.BlockSpec((1, D), lambda i: (0, 0))
    return pl.pallas_call(
        body, name=name, grid=(T // tr,), in_specs=[row, vec, vec], out_specs=[row, row],
        out_shape=[_S((T, D), _F32), _S((T, D), _MM)], compiler_params=_params(("parallel",)),
    )(z, g, b)


def _ln_bwd(z, g, dy, name):
    T, D = z.shape
    tr = _tile(T, 256)

    def body(z_ref, g_ref, dy_ref, dz_ref, dzb_ref, dg_ref, db_ref):
        @pl.when(pl.program_id(0) == 0)
        def _():
            dg_ref[...] = jnp.zeros_like(dg_ref)
            db_ref[...] = jnp.zeros_like(db_ref)

        zz = z_ref[...]
        mu = jnp.mean(zz, axis=-1, keepdims=True)
        zc = zz - mu
        rstd = lax.rsqrt(jnp.mean(zc * zc, axis=-1, keepdims=True) + LN_EPS)
        xh = zc * rstd
        d = dy_ref[...]
        dg_ref[...] += jnp.sum(d * xh, axis=0, keepdims=True)
        db_ref[...] += jnp.sum(d, axis=0, keepdims=True)
        dxh = d * g_ref[...]
        dz = rstd * (dxh - jnp.mean(dxh, axis=-1, keepdims=True) - xh * jnp.mean(dxh * xh, axis=-1, keepdims=True))
        dz_ref[...] = dz
        dzb_ref[...] = dz.astype(dzb_ref.dtype)

    row = pl.BlockSpec((tr, D), lambda i: (i, 0))
    vec = pl.BlockSpec((1, D), lambda i: (0, 0))
    return pl.pallas_call(
        body, name=name, grid=(T // tr,), in_specs=[row, vec, row], out_specs=[row, row, vec, vec],
        out_shape=[_S((T, D), _F32), _S((T, D), _MM), _S((1, D), _F32), _S((1, D), _F32)],
        compiler_params=_params(("arbitrary",)),
    )(z, g, dy)


def _ln_loss_bwd(z, g, b, target, name):
    T, D = z.shape
    tr = _tile(T, 256)

    def body(z_ref, g_ref, b_ref, t_ref, loss_ref, dz_ref, dzb_ref, dg_ref, db_ref, lacc):
        i = pl.program_id(0)

        @pl.when(i == 0)
        def _():
            dg_ref[...] = jnp.zeros_like(dg_ref)
            db_ref[...] = jnp.zeros_like(db_ref)
            lacc[...] = jnp.zeros_like(lacc)

        zz = z_ref[...]
        mu = jnp.mean(zz, axis=-1, keepdims=True)
        zc = zz - mu
        rstd = lax.rsqrt(jnp.mean(zc * zc, axis=-1, keepdims=True) + LN_EPS)
        xh = zc * rstd
        err = xh * g_ref[...] + b_ref[...] - t_ref[...]
        lacc[...] += jnp.sum(err * err, axis=0, keepdims=True)
        d = err * (1.0 / D)
        dg_ref[...] += jnp.sum(d * xh, axis=0, keepdims=True)
        db_ref[...] += jnp.sum(d, axis=0, keepdims=True)
        dxh = d * g_ref[...]
        dz = rstd * (dxh - jnp.mean(dxh, axis=-1, keepdims=True) - xh * jnp.mean(dxh * xh, axis=-1, keepdims=True))
        dz_ref[...] = dz
        dzb_ref[...] = dz.astype(dzb_ref.dtype)

        @pl.when(i == pl.num_programs(0) - 1)
        def _():
            loss_ref[...] = jnp.sum(lacc[...], axis=-1, keepdims=True) * (0.5 / D)

    row = pl.BlockSpec((tr, D), lambda i: (i, 0))
    vec = pl.BlockSpec((1, D), lambda i: (0, 0))
    one = pl.BlockSpec((1, 1), lambda i: (0, 0))
    return pl.pallas_call(
        body, name=name, grid=(T // tr,), in_specs=[row, vec, vec, row], out_specs=[one, row, row, vec, vec],
        out_shape=[_S((1, 1), _F32), _S((T, D), _F32), _S((T, D), _MM), _S((1, D), _F32), _S((1, D), _F32)],
        scratch_shapes=[pltpu.VMEM((1, D), _F32)], compiler_params=_params(("arbitrary",)),
    )(z, g, b, target)


def _conv3(X, cw, cb):
    return cb + cw[2:3] * X + cw[1:2] * pltpu.roll(X, 1, 0) + cw[0:1] * pltpu.roll(X, 2, 0)


def _ffn_mid_fwd(h, cw, cb, name):
    _, T, F = h.shape
    tr = _tile(T, FFN_ROWS)
    tc = _tile(F, FFN_FWD_COLS)
    nb = tr // CONV_HALO

    def body(h_ref, p_ref, cw_ref, cb_ref, o_ref):
        i = pl.program_id(0)
        hc = []
        for part in range(2):
            prev = jnp.where(i == 0, 0.0, p_ref[part])
            X = jnp.concatenate([prev, h_ref[part]], axis=0)
            hc.append(_conv3(X, cw_ref[part], cb_ref[part])[CONV_HALO:])
        a, v = hc
        o_ref[...] = (a * _sigmoid(a) * v).astype(o_ref.dtype)

    return pl.pallas_call(
        body, name=name, grid=(T // tr, F // tc),
        in_specs=[pl.BlockSpec((2, tr, tc), lambda i, j: (0, i, j)),
                  pl.BlockSpec((2, CONV_HALO, tc), lambda i, j: (0, jnp.maximum(i * nb - 1, 0), j)),
                  pl.BlockSpec((2, 3, tc), lambda i, j: (0, 0, j)),
                  pl.BlockSpec((2, 1, tc), lambda i, j: (0, 0, j))],
        out_specs=pl.BlockSpec((tr, tc), lambda i, j: (i, j)), out_shape=_S((T, F), _MM),
        compiler_params=_params(("parallel", "parallel")),
    )(h, h, cw, cb)


def _ffn_mid_bwd(h, dact, cw, cb, name):
    _, T, F = h.shape
    tr = _tile(T, FFN_BWD_ROWS)
    tc = _tile(F, FFN_BWD_COLS)
    nb = tr // CONV_HALO
    last_blk = T // CONV_HALO - 1
    lo, hi = CONV_HALO, CONV_HALO + tr
    n = tr + 2 * CONV_HALO

    def body(h_ref, p_ref, n_ref, d_ref, dn_ref, cw_ref, cb_ref, dh_ref, dcw_ref, dcb_ref):
        i = pl.program_id(1)
        is_first = i == 0
        is_last = i == pl.num_programs(1) - 1

        @pl.when(is_first)
        def _():
            dcw_ref[...] = jnp.zeros_like(dcw_ref)
            dcb_ref[...] = jnp.zeros_like(dcb_ref)

        X, hc = [], []
        for part in range(2):
            prev = jnp.where(is_first, 0.0, p_ref[part])
            nxt = jnp.where(is_last, 0.0, n_ref[part])
            Xp = jnp.concatenate([prev, h_ref[part], nxt], axis=0)
            X.append(Xp)
            hc.append(_conv3(Xp, cw_ref[part], cb_ref[part]))
        D = jnp.concatenate([jnp.zeros((CONV_HALO, tc), _F32), d_ref[...], jnp.where(is_last, 0.0, dn_ref[...])], axis=0)
        a, v = hc
        sg = _sigmoid(a)
        dhc = [D * v * sg * (1.0 + a * (1.0 - sg)), D * a * sg]
        for part in range(2):
            g = dhc[part]
            cwp = cw_ref[part]
            dh = cwp[2:3] * g + cwp[1:2] * pltpu.roll(g, n - 1, 0) + cwp[0:1] * pltpu.roll(g, n - 2, 0)
            dh_ref[part] = dh[lo:hi].astype(dh_ref.dtype)
            gt = g[lo:hi]
            dcw_ref[part, 2:3, :] += jnp.sum(gt * X[part][lo:hi], axis=0, keepdims=True)
            dcw_ref[part, 1:2, :] += jnp.sum(gt * pltpu.roll(X[part], 1, 0)[lo:hi], axis=0, keepdims=True)
            dcw_ref[part, 0:1, :] += jnp.sum(gt * pltpu.roll(X[part], 2, 0)[lo:hi], axis=0, keepdims=True)
            dcb_ref[part] += jnp.sum(gt, axis=0, keepdims=True)

    return pl.pallas_call(
        body, name=name, grid=(F // tc, T // tr),
        in_specs=[pl.BlockSpec((2, tr, tc), lambda j, i: (0, i, j)),
                  pl.BlockSpec((2, CONV_HALO, tc), lambda j, i: (0, jnp.maximum(i * nb - 1, 0), j)),
                  pl.BlockSpec((2, CONV_HALO, tc), lambda j, i: (0, jnp.minimum((i + 1) * nb, last_blk), j)),
                  pl.BlockSpec((tr, tc), lambda j, i: (i, j)),
                  pl.BlockSpec((CONV_HALO, tc), lambda j, i: (jnp.minimum((i + 1) * nb, last_blk), j)),
                  pl.BlockSpec((2, 3, tc), lambda j, i: (0, 0, j)),
                  pl.BlockSpec((2, 1, tc), lambda j, i: (0, 0, j))],
        out_specs=[pl.BlockSpec((2, tr, tc), lambda j, i: (0, i, j)),
                   pl.BlockSpec((2, 3, tc), lambda j, i: (0, 0, j)),
                   pl.BlockSpec((2, 1, tc), lambda j, i: (0, 0, j))],
        out_shape=[_S((2, T, F), _MM), _S((2, 3, F), _F32), _S((2, 1, F), _F32)],
        compiler_params=_params(("parallel", "arbitrary")),
    )(h, h, h, dact, dact, cw, cb)


def _ev_common(h_ref, hp_ref, lng_ref, lnb_ref, ws_ref, bias_ref, i, tr, W):
    H = W // A_HEAD
    u, gu = _gelu_and_grad(h_ref[0])
    v, gv = _gelu_and_grad(h_ref[1])
    mu = jnp.mean(v, axis=-1, keepdims=True)
    vc = v - mu
    rstd = lax.rsqrt(jnp.mean(vc * vc, axis=-1, keepdims=True) + LN_EPS)
    vhat = vc * rstd
    vb = (vhat * lng_ref[...] + lnb_ref[...]).astype(_MM)
    s_chunks = []
    for c in range(tr // A_CHUNK):
        r0 = c * A_CHUNK
        heads = [_dot(ws_ref[hd], vb[r0:r0 + A_CHUNK, hd * A_HEAD:(hd + 1) * A_HEAD]) for hd in range(H)]
        s_chunks.append(jnp.concatenate(heads, axis=1) + bias_ref[...])
    prev = jnp.where(i == 0, 0.0, hp_ref[...])
    X = jnp.concatenate([prev, h_ref[2]], axis=0)
    return u, gu, gv, rstd, vhat, vb, s_chunks, X


def _pool_inv_count(i, tr, rows, win):
    pos = i * tr + _row_index(rows) + 1
    return 1.0 / jnp.minimum(pos, win).astype(_F32)


def _pool_fwd(X, g, Wg, i, tr):
    xg = X[:, g * Wg:(g + 1) * Wg]
    s = xg
    for k in range(g + 1):
        s = s + pltpu.roll(s, 2 ** k, 0)
    return s[POOL_HALO:] * _pool_inv_count(i, tr, tr, 2 ** (g + 1)) - xg[POOL_HALO:]


def _ev_mid_fwd(h, lng, lnb, ws, bias, wp, sc, name):
    _, T, W = h.shape
    tr = _tile(T, 256)
    H = W // A_HEAD
    Wg = W // B_GROUPS
    nb = tr // POOL_HALO

    def body(h_ref, hp_ref, lng_ref, lnb_ref, ws_ref, bias_ref, wp_ref, sc_ref, o_ref):
        i = pl.program_id(0)
        u, _, _, _, _, _, s_chunks, X = _ev_common(h_ref, hp_ref, lng_ref, lnb_ref, ws_ref, bias_ref, i, tr, W)
        for c, s in enumerate(s_chunks):
            r0 = c * A_CHUNK
            o_ref[r0:r0 + A_CHUNK, 0:W] = (u[r0:r0 + A_CHUNK] * s).astype(o_ref.dtype)
        for g in range(B_GROUPS):
            p = _pool_fwd(X, g, Wg, i, tr)
            y = _dot(p.astype(_MM), wp_ref[g]) * sc_ref[:, g * Wg:(g + 1) * Wg]
            o_ref[:, W + g * Wg:W + (g + 1) * Wg] = y.astype(o_ref.dtype)

    vec = pl.BlockSpec((1, W), lambda i: (0, 0))
    return pl.pallas_call(
        body, name=name, grid=(T // tr,),
        in_specs=[pl.BlockSpec((3, tr, W), lambda i: (0, i, 0)),
                  pl.BlockSpec((None, POOL_HALO, W), lambda i: (2, jnp.maximum(i * nb - 1, 0), 0)),
                  vec, vec,
                  pl.BlockSpec((H, A_CHUNK, A_CHUNK), lambda i: (0, 0, 0)),
                  pl.BlockSpec((A_CHUNK, W), lambda i: (0, 0)),
                  pl.BlockSpec((B_GROUPS, Wg, Wg), lambda i: (0, 0, 0)),
                  vec],
        out_specs=pl.BlockSpec((tr, 2 * W), lambda i: (i, 0)), out_shape=_S((T, 2 * W), _MM),
        compiler_params=_params(("parallel",)),
    )(h, h, lng, lnb, ws, bias, wp, sc)


def _ev_mid_bwd(h, dy, lng, lnb, ws, bias, wp, sc, name):
    _, T, W = h.shape
    tr = _tile(T, 256)
    H = W // A_HEAD
    Wg = W // B_GROUPS
    nb = tr // POOL_HALO
    last_blk = T // POOL_HALO - 1
    n = tr + POOL_HALO

    def body(h_ref, hp_ref, dy_ref, dyn_ref, lng_ref, lnb_ref, ws_ref, bias_ref, wp_ref, sc_ref,
             dh_ref, dws_ref, dbias_ref, dlng_ref, dlnb_ref, dwp_ref, dsc_ref):
        i = pl.program_id(0)

        @pl.when(i == 0)
        def _():
            for r in (dws_ref, dbias_ref, dlng_ref, dlnb_ref, dwp_ref, dsc_ref):
                r[...] = jnp.zeros_like(r)

        u, gu, gv, rstd, vhat, vb, s_chunks, X = _ev_common(h_ref, hp_ref, lng_ref, lnb_ref, ws_ref, bias_ref, i, tr, W)
        rr = lax.broadcasted_iota(jnp.int32, (A_CHUNK, A_CHUNK), 0)
        cc = lax.broadcasted_iota(jnp.int32, (A_CHUNK, A_CHUNK), 1)
        tril = rr >= cc
        du_chunks, dvln_chunks = [], []
        for c, s in enumerate(s_chunks):
            r0 = c * A_CHUNK
            dya = dy_ref[r0:r0 + A_CHUNK, 0:W]
            du_chunks.append(dya * s)
            ds = dya * u[r0:r0 + A_CHUNK]
            dbias_ref[...] += ds
            dsb = ds.astype(_MM)
            heads = []
            for hd in range(H):
                cols = slice(hd * A_HEAD, (hd + 1) * A_HEAD)
                dws_ref[hd] += jnp.where(tril, _dot(dsb[:, cols], vb[r0:r0 + A_CHUNK, cols], _NT), 0.0)
                heads.append(_dot(ws_ref[hd], dsb[:, cols], _TN))
            dvln_chunks.append(jnp.concatenate(heads, axis=1))
        du = jnp.concatenate(du_chunks, axis=0)
        dvln = jnp.concatenate(dvln_chunks, axis=0)
        dlng_ref[...] += jnp.sum(dvln * vhat, axis=0, keepdims=True)
        dlnb_ref[...] += jnp.sum(dvln, axis=0, keepdims=True)
        dxh = dvln * lng_ref[...]
        dv = rstd * (dxh - jnp.mean(dxh, axis=-1, keepdims=True) - vhat * jnp.mean(dxh * vhat, axis=-1, keepdims=True))
        dh_ref[0] = (du * gu).astype(dh_ref.dtype)
        dh_ref[1] = (dv * gv).astype(dh_ref.dtype)

        dyb = dy_ref[:, W:2 * W]
        dyb_full = jnp.concatenate([dyb, jnp.where(i == pl.num_programs(0) - 1, 0.0, dyn_ref[...])], axis=0)
        for g in range(B_GROUPS):
            cols = slice(g * Wg, (g + 1) * Wg)
            pb = _pool_fwd(X, g, Wg, i, tr).astype(_MM)
            ypre = _dot(pb, wp_ref[g])
            dsc_ref[:, cols] += jnp.sum(dyb[:, cols] * ypre, axis=0, keepdims=True)
            dyp = (dyb_full[:, cols] * sc_ref[:, cols]).astype(_MM)
            dwp_ref[g] += _dot(pb, dyp[0:tr], _TN)
            dp = _dot(dyp, wp_ref[g], _NT)
            s = dp * _pool_inv_count(i, tr, n, 2 ** (g + 1))
            for k in range(g + 1):
                s = s + pltpu.roll(s, n - 2 ** k, 0)
            dh_ref[2, :, cols] = (s[0:tr] - dp[0:tr]).astype(dh_ref.dtype)

    vec = pl.BlockSpec((1, W), lambda i: (0, 0))
    ws_spec = pl.BlockSpec((H, A_CHUNK, A_CHUNK), lambda i: (0, 0, 0))
    bias_spec = pl.BlockSpec((A_CHUNK, W), lambda i: (0, 0))
    wp_spec = pl.BlockSpec((B_GROUPS, Wg, Wg), lambda i: (0, 0, 0))
    return pl.pallas_call(
        body, name=name, grid=(T // tr,),
        in_specs=[pl.BlockSpec((3, tr, W), lambda i: (0, i, 0)),
                  pl.BlockSpec((None, POOL_HALO, W), lambda i: (2, jnp.maximum(i * nb - 1, 0), 0)),
                  pl.BlockSpec((tr, 2 * W), lambda i: (i, 0)),
                  pl.BlockSpec((POOL_HALO, W), lambda i: (jnp.minimum((i + 1) * nb, last_blk), 1)),
                  vec, vec, ws_spec, bias_spec, wp_spec, vec],
        out_specs=[pl.BlockSpec((3, tr, W), lambda i: (0, i, 0)), ws_spec, bias_spec, vec, vec, wp_spec, vec],
        out_shape=[_S((3, T, W), _MM), _S((H, A_CHUNK, A_CHUNK), _F32), _S((A_CHUNK, W), _F32), _S((1, W), _F32),
                   _S((1, W), _F32), _S((B_GROUPS, Wg, Wg), _F32), _S((1, W), _F32)],
        compiler_params=_params(("arbitrary",)),
    )(h, h, dy, dy, lng, lnb, ws, bias, wp, sc)


def _chunk_cumsum(x, rin):
    s = 1
    while s < C_CHUNK:
        x = x + jnp.where(rin >= s, pltpu.roll(x, s, 0), 0.0)
        s *= 2
    return x


def _chunk_revcumsum(x, rin):
    n = x.shape[0]
    s = 1
    while s < C_CHUNK:
        x = x + jnp.where(rin + s < C_CHUNK, pltpu.roll(x, n - s, 0), 0.0)
        s *= 2
    return x


def _hgrn_gates(q, fl, lb, tr, tc):
    nch = tr // C_CHUNK
    sq = _sigmoid(q)
    sf = _sigmoid_rel(fl)
    f = lb + (1.0 - lb) * sf
    logf = jnp.log(f)
    rin = _row_index(tr) % C_CHUNK
    b = _chunk_cumsum(logf, rin)
    tot3 = jnp.sum(logf.reshape(nch, C_CHUNK, tc), axis=1, keepdims=True)
    eb = jnp.exp(b)
    enb = jnp.exp(-b)
    ekb = jnp.exp(tot3 - b.reshape(nch, C_CHUNK, tc)).reshape(tr, tc)
    return sq, sf, f, rin, tot3, eb, enb, ekb


def _hgrn_prep_fwd(h, lb, name):
    _, T, D = h.shape
    tr = _tile(T, 512)
    tc = _tile(D, 512)
    nch = tr // C_CHUNK

    def body(q_ref, f_ref, v_ref, lb_ref, qd_ref, kd_ref, ke_ref, vb_ref, dec_ref):
        q = q_ref[...]
        sq, _, f, _, tot3, eb, enb, ekb = _hgrn_gates(q, f_ref[...], lb_ref[...], tr, tc)
        kk = 1.0 - f
        qd_ref[...] = (q * sq * eb).astype(qd_ref.dtype)
        kd_ref[...] = (kk * enb).astype(kd_ref.dtype)
        ke_ref[...] = (kk * ekb).astype(ke_ref.dtype)
        vb_ref[...] = v_ref[...].astype(vb_ref.dtype)
        dec_ref[...] = jnp.exp(tot3).reshape(nch, tc)

    def part(p):
        return pl.BlockSpec((None, tr, tc), lambda i, j: (p, i, j))

    blk = pl.BlockSpec((tr, tc), lambda i, j: (i, j))
    return pl.pallas_call(
        body, name=name, grid=(T // tr, D // tc),
        in_specs=[part(0), part(1), part(2), pl.BlockSpec((1, tc), lambda i, j: (0, j))],
        out_specs=[blk, blk, blk, blk, pl.BlockSpec((nch, tc), lambda i, j: (i, j))],
        out_shape=[_S((T, D), _MM)] * 4 + [_S((T // C_CHUNK, D), _F32)],
        compiler_params=_params(("parallel", "parallel")),
    )(h, h, h, lb)


def _tril_mask():
    rr = lax.broadcasted_iota(jnp.int32, (C_CHUNK, C_CHUNK), 0)
    cc = lax.broadcasted_iota(jnp.int32, (C_CHUNK, C_CHUNK), 1)
    return rr >= cc


def _hgrn_scan_fwd(qd, kd, ke, vb, dec, h, ng, name):
    T, D = qd.shape
    NH = D // C_HEAD
    N = T // C_CHUNK

    def body(qd_ref, kd_ref, ke_ref, vb_ref, dec_ref, g_ref, ng_ref, o_ref, y_ref, st_ref):
        mask = _tril_mask()

        def step(n, St):
            r = pl.ds(pl.multiple_of(n * C_CHUNK, C_CHUNK), C_CHUNK)
            Qd, Kd, Ke, V = qd_ref[r, :], kd_ref[r, :], ke_ref[r, :], vb_ref[r, :]
            att = jnp.where(mask, _dot(Qd, Kd, _NT), 0.0).astype(_MM)
            o_ref[r, :] = _dot(att, V) + _dot(Qd, St.astype(_MM), _NT)
            st_ref[n] = St
            return St * dec_ref[pl.ds(n, 1), :] + _dot(V, Ke, _TN)

        def trip(i, state):
            for u in range(SCAN_UNROLL):
                state = step(i * SCAN_UNROLL + u, state)
            return state

        lax.fori_loop(0, N // SCAN_UNROLL, trip, jnp.zeros((C_HEAD, C_HEAD), _F32))
        o = o_ref[...]
        r = lax.rsqrt(jnp.mean(o * o, axis=-1, keepdims=True) + LN_EPS)
        y_ref[...] = (o * r * ng_ref[...] * _sigmoid(g_ref[...])).astype(y_ref.dtype)

    col = pl.BlockSpec((T, C_HEAD), lambda j: (0, j))
    return pl.pallas_call(
        body, name=name, grid=(NH,),
        in_specs=[col, col, col, col, pl.BlockSpec((N, C_HEAD), lambda j: (0, j)),
                  pl.BlockSpec((None, T, C_HEAD), lambda j: (3, 0, j)), pl.BlockSpec((1, C_HEAD), lambda j: (0, j))],
        out_specs=[col, col, pl.BlockSpec((None, N, C_HEAD, C_HEAD), lambda j: (j, 0, 0, 0))],
        out_shape=[_S((T, D), _F32), _S((T, D), _MM), _S((NH, N, C_HEAD, C_HEAD), _F32)],
        compiler_params=_params(("parallel",)),
    )(qd, kd, ke, vb, dec, h, ng)


def _hgrn_scan_bwd(qd, kd, ke, vb, dec, st, o, h, ng, dy, name):
    T, D = qd.shape
    NH = D // C_HEAD
    N = T // C_CHUNK

    def body(qd_ref, kd_ref, ke_ref, vb_ref, dec_ref, st_ref, o_ref, g_ref, ng_ref, dy_ref,
             dqd_ref, dkd_ref, dke_ref, dv_ref, dgate_ref, ddec_ref, dng_ref, do_s):
        o = o_ref[...]
        r = lax.rsqrt(jnp.mean(o * o, axis=-1, keepdims=True) + LN_EPS)
        oh = o * r
        gn = ng_ref[...]
        sg = _sigmoid(g_ref[...])
        d = dy_ref[...]
        dyn = d * sg
        dgate_ref[...] = (d * oh * gn * sg * (1.0 - sg)).astype(dgate_ref.dtype)
        dng_ref[...] = jnp.sum(dyn * oh, axis=0, keepdims=True)
        doh = dyn * gn
        do_s[...] = (r * (doh - oh * jnp.mean(doh * oh, axis=-1, keepdims=True))).astype(do_s.dtype)
        mask = _tril_mask()

        def step(k, dSt):
            n = N - 1 - k
            rws = pl.ds(pl.multiple_of(n * C_CHUNK, C_CHUNK), C_CHUNK)
            Qd, Kd, Ke, V, dO = qd_ref[rws, :], kd_ref[rws, :], ke_ref[rws, :], vb_ref[rws, :], do_s[rws, :]
            St = st_ref[n]
            Stb = St.astype(_MM)
            dStb = dSt.astype(_MM)
            att = jnp.where(mask, _dot(Qd, Kd, _NT), 0.0).astype(_MM)
            dA = jnp.where(mask, _dot(dO, V, _NT), 0.0).astype(_MM)
            dv_ref[rws, :] = (_dot(att, dO, _TN) + _dot(Ke, dStb, _NT)).astype(dv_ref.dtype)
            dqd_ref[rws, :] = _dot(dA, Kd) + _dot(dO, Stb)
            dkd_ref[rws, :] = _dot(dA, Qd, _TN)
            dke_ref[rws, :] = _dot(V, dStb)
            ddec_ref[pl.ds(n, 1), :] = jnp.sum(dSt * St, axis=0, keepdims=True)
            return dSt * dec_ref[pl.ds(n, 1), :] + _dot(dO, Qd, _TN)

        def trip(i, state):
            for u in range(SCAN_UNROLL):
                state = step(i * SCAN_UNROLL + u, state)
            return state

        lax.fori_loop(0, N // SCAN_UNROLL, trip, jnp.zeros((C_HEAD, C_HEAD), _F32))

    col = pl.BlockSpec((T, C_HEAD), lambda j: (0, j))
    chk = pl.BlockSpec((N, C_HEAD), lambda j: (0, j))
    one = pl.BlockSpec((1, C_HEAD), lambda j: (0, j))
    return pl.pallas_call(
        body, name=name, grid=(NH,),
        in_specs=[col, col, col, col, chk, pl.BlockSpec((None, N, C_HEAD, C_HEAD), lambda j: (j, 0, 0, 0)), col,
                  pl.BlockSpec((None, T, C_HEAD), lambda j: (3, 0, j)), one, col],
        out_specs=[col, col, col, col, col, chk, one],
        out_shape=[_S((T, D), _F32)] * 3 + [_S((T, D), _MM)] * 2 + [_S((N, D), _F32), _S((1, D), _F32)],
        scratch_shapes=[pltpu.VMEM((T, C_HEAD), _MM)],
        compiler_params=_params(("parallel",)),
    )(qd, kd, ke, vb, dec, st, o, h, ng, dy)


def _hgrn_prep_bwd(h, lb, dqd, dkd, dke, dv, dgate, ddec, name):
    _, T, D = h.shape
    tr = _tile(T, 512)
    tc = _tile(D, 256)
    nch = tr // C_CHUNK

    def body(q_ref, f_ref, lb_ref, dqd_ref, dkd_ref, dke_ref, dv_ref, dgate_ref, ddec_ref, dh_ref, dlb_ref):
        @pl.when(pl.program_id(1) == 0)
        def _():
            dlb_ref[...] = jnp.zeros_like(dlb_ref)

        q = q_ref[...]
        lb = lb_ref[...]
        sq, sf, f, rin, tot3, eb, enb, ekb = _hgrn_gates(q, f_ref[...], lb, tr, tc)
        kk = 1.0 - f
        dQd, dKd, dKe = dqd_ref[...], dkd_ref[...], dke_ref[...]
        tq = dQd * eb
        tkd = dKd * enb
        tke = dKe * ekb
        ke_term = tke * kk
        db = tq * (q * sq) - tkd * kk - ke_term
        dtot3 = (jnp.sum(ke_term.reshape(nch, C_CHUNK, tc), axis=1, keepdims=True)
                 + (ddec_ref[...] * jnp.exp(tot3).reshape(nch, tc)).reshape(nch, 1, tc))
        dlogf = (_chunk_revcumsum(db, rin).reshape(nch, C_CHUNK, tc) + dtot3).reshape(tr, tc)
        df = dlogf / f - (tkd + tke)
        dh_ref[0] = (tq * sq * (1.0 + q * (1.0 - sq))).astype(dh_ref.dtype)
        dh_ref[1] = (df * (1.0 - lb) * sf * (1.0 - sf)).astype(dh_ref.dtype)
        dh_ref[2] = dv_ref[...]
        dh_ref[3] = dgate_ref[...]
        dlb_ref[...] += jnp.sum(df * (1.0 - sf), axis=0, keepdims=True)

    def part(p):
        return pl.BlockSpec((None, tr, tc), lambda j, i: (p, i, j))

    blk = pl.BlockSpec((tr, tc), lambda j, i: (i, j))
    vec = pl.BlockSpec((1, tc), lambda j, i: (0, j))
    return pl.pallas_call(
        body, name=name, grid=(D // tc, T // tr),
        in_specs=[part(0), part(1), vec, blk, blk, blk, blk, blk, pl.BlockSpec((nch, tc), lambda j, i: (i, j))],
        out_specs=[pl.BlockSpec((4, tr, tc), lambda j, i: (0, i, j)), vec],
        out_shape=[_S((4, T, D), _MM), _S((1, D), _F32)],
        compiler_params=_params(("parallel", "arbitrary")),
    )(h, h, lb, dqd, dkd, dke, dv, dgate, ddec)


def _sum_in_device_order(me1, own, land, name):
    R, C = own.shape
    tr = _tile(R, 256)

    def body(me_ref, own_ref, land_ref, o_ref):
        me = me_ref[0]
        g = None
        for j in range(N_DEV):
            slot = jnp.maximum(jnp.bitwise_xor(me, j) - 1, 0)
            p = jnp.where(me == j, own_ref[...], land_ref[slot])
            g = p if g is None else g + p
        o_ref[...] = g

    return pl.pallas_call(
        body, name=name,
        grid_spec=pltpu.PrefetchScalarGridSpec(
            num_scalar_prefetch=1, grid=(R // tr,),
            in_specs=[pl.BlockSpec((tr, C), lambda i, me: (i, 0)), pl.BlockSpec((N_DEV - 1, tr, C), lambda i, me: (0, i, 0))],
            out_specs=pl.BlockSpec((tr, C), lambda i, me: (i, 0))),
        out_shape=_S((R, C), _F32), compiler_params=_params(("parallel",)),
    )(me1, own, land)


def _adamw(parts, w, m, v, name):
    P, R, C = parts.shape
    tr = _tile(R, 128) if R % LANE == 0 else R

    def body(p_ref, w_ref, m_ref, v_ref, g_ref, d_ref, nm_ref, nv_ref):
        g = p_ref[0].astype(_F32)
        for s in range(1, P):
            g = g + p_ref[s].astype(_F32)
        nm = ADAM_B1 * m_ref[...] + (1.0 - ADAM_B1) * g
        nv = ADAM_B2 * v_ref[...] + (1.0 - ADAM_B2) * (g * g)
        m_hat = nm / (1.0 - ADAM_B1 ** ADAM_STEP)
        v_hat = nv / (1.0 - ADAM_B2 ** ADAM_STEP)
        g_ref[...] = g
        d_ref[...] = -ADAM_LR * (m_hat / (jnp.sqrt(v_hat) + ADAM_EPS) + ADAM_WD * w_ref[...])
        nm_ref[...] = nm
        nv_ref[...] = nv

    blk = pl.BlockSpec((tr, C), lambda i: (i, 0))
    return pl.pallas_call(
        body, name=name, grid=(R // tr,), in_specs=[pl.BlockSpec((P, tr, C), lambda i: (0, i, 0)), blk, blk, blk],
        out_specs=[blk] * 4, out_shape=[_S((R, C), _F32)] * 4, compiler_params=_params(("parallel",)),
    )(parts, w, m, v)


def _exchange(name, srcs, out_shapes, jobs, deps=()):
    ns, nj = len(srcs), len(jobs)

    nd = len(deps)

    def body(*refs):
        ins, outs = refs[:ns], refs[ns + nd:ns + nd + len(out_shapes)]
        send_sems, recv_sems, local_sems = refs[-3:]
        x, y, c = lax.axis_index("x"), lax.axis_index("y"), lax.axis_index("c")
        me = 4 * x + 2 * y + c
        local = []
        for ji, (si, src_fn, di, dst_fn) in enumerate(jobs):
            cp = pltpu.make_async_copy(src_fn(ins[si], me, me), dst_fn(outs[di], me), local_sems.at[ji])
            cp.start()
            local.append(cp)
        remote = []
        for k in range(1, N_DEV):
            px, py, pc = (x + (k >> 2)) % 2, (y + ((k >> 1) & 1)) % 2, (c + (k & 1)) % 2
            to = 4 * px + 2 * py + pc
            for ji, (si, src_fn, di, dst_fn) in enumerate(jobs):
                sem = (k - 1) * nj + ji
                cp = pltpu.make_async_remote_copy(
                    src_ref=src_fn(ins[si], me, to), dst_ref=dst_fn(outs[di], me),
                    send_sem=send_sems.at[sem], recv_sem=recv_sems.at[sem],
                    device_id=(px, py, pc), device_id_type=pl.DeviceIdType.MESH)
                cp.start()
                remote.append(cp)
        for cp in remote:
            cp.wait_recv()
        for cp in remote:
            cp.wait_send()
        for cp in local:
            cp.wait()

    hbm = pl.BlockSpec(memory_space=pltpu.HBM)
    return pl.pallas_call(
        body, name=name, in_specs=[hbm] * ns + [_ANY] * nd, out_specs=[hbm] * len(out_shapes), out_shape=list(out_shapes),
        scratch_shapes=[pltpu.SemaphoreType.DMA(((N_DEV - 1) * nj,)), pltpu.SemaphoreType.DMA(((N_DEV - 1) * nj,)),
                        pltpu.SemaphoreType.DMA((nj,))],
    )(*srcs, *deps)


def _whole(ref, me, to):
    return ref


def _slot_job(i, o):
    def dst(ref, me):
        return ref.at[me]
    return (i, _whole, o, dst)


_HBM = pl.BlockSpec(memory_space=pltpu.HBM)
_SEM = pl.BlockSpec(memory_space=pltpu.SEMAPHORE)
_ANY = pl.BlockSpec(memory_space=pl.ANY)
_N_PEER = N_DEV - 1


def _split_params():
    return pltpu.CompilerParams(has_side_effects=pltpu.SideEffectType.DATAFLOW_SIDE_EFFECTING)


def _blk(ref, axis, n, idx):
    if axis is None:
        return ref
    return ref.at[tuple([slice(None)] * axis + [pl.ds(pl.multiple_of(idx * n, n), n)])]


def _peer(k):
    x, y, c = lax.axis_index("x"), lax.axis_index("y"), lax.axis_index("c")
    px, py, pc = (x + (k >> 2)) % 2, (y + ((k >> 1) & 1)) % 2, (c + (k & 1)) % 2
    return (px, py, pc), 4 * px + 2 * py + pc, 4 * x + 2 * y + c


def _row_tile(rows, pref):
    best = None
    for d in range(16, min(rows, pref) + 1, 16):
        if rows % d == 0:
            best = d
    return best if best is not None else rows


def _place(w, me1, axis, name, layer=None, deps=()):
    R, C = w.shape[-2:]
    tr = _row_tile(R, 512)
    nb = R // tr
    lead = () if layer is None else (None,)
    pre = () if layer is None else (layer,)

    def body(me_ref, w_ref, *rest):
        rest[-1][...] = w_ref[...].astype(rest[-1].dtype)

    if axis == 1:
        out_spec = pl.BlockSpec((tr, C), lambda i, me: (i, me[0]))
        out_shape = _S((R, N_DEV * C), _MM)
    else:
        out_spec = pl.BlockSpec((tr, C), lambda i, me: (me[0] * nb + i, 0))
        out_shape = _S((N_DEV * R, C), _MM)
    return pl.pallas_call(
        body, name=name,
        grid_spec=pltpu.PrefetchScalarGridSpec(
            num_scalar_prefetch=1, grid=(nb,),
            in_specs=[pl.BlockSpec(lead + (tr, C), lambda i, me: pre + (i, 0))] + [_ANY] * len(deps), out_specs=out_spec),
        out_shape=out_shape, compiler_params=_params(("parallel",)),
    )(me1, w, *deps)


_SIBLING = 1
_CHIPS = (2, 4, 6)
_VMEM_TOKEN = pl.BlockSpec(memory_space=pltpu.VMEM)


def _remote(ref_blk, send_sem, recv_sem, dev):
    return pltpu.make_async_remote_copy(src_ref=ref_blk, dst_ref=ref_blk, send_sem=send_sem, recv_sem=recv_sem,
                                        device_id=dev, device_id_type=pl.DeviceIdType.MESH)


def _gather_start(name, full, axis, n):
    def body(f_ref, send, recv, f_out, token):
        for i, k in enumerate((_SIBLING,) + _CHIPS):
            dev, _, me = _peer(k)
            _remote(_blk(f_ref, axis, n, me), send.at[i], recv.at[i], dev).start()
        token[...] = jnp.zeros_like(token)

    return pl.pallas_call(
        body, name=name,
        out_shape=(pltpu.SemaphoreType.DMA((4,)), pltpu.SemaphoreType.DMA((4,)), pltpu.HBM(full.shape, full.dtype),
                   _S((8, LANE), _F32)),
        in_specs=(_HBM,), out_specs=(_SEM, _SEM, _HBM, _VMEM_TOKEN),
        input_output_aliases={0: 2}, compiler_params=_split_params(),
    )(pltpu.with_memory_space_constraint(full, pltpu.HBM))


def _gather_forward(name, full, axis, n, recv, after):
    after = tuple(after) if isinstance(after, (tuple, list)) else (after,)

    def body(f_ref, recv_r, *rest):
        send2, recv2, f_out, token = rest[-4:]
        sib, _, _ = _peer(_SIBLING)
        for i, k in enumerate(_CHIPS):
            dev, frm, _ = _peer(k)
            blk = _blk(f_ref, axis, n, frm)
            _remote(blk, send2.at[i], recv_r.at[1 + i], dev).wait_recv()
            _remote(blk, send2.at[i], recv2.at[i], sib).start()
        token[...] = jnp.zeros_like(token)

    return pl.pallas_call(
        body, name=name,
        out_shape=(pltpu.SemaphoreType.DMA((3,)), pltpu.SemaphoreType.DMA((3,)), pltpu.HBM(full.shape, full.dtype),
                   _S((8, LANE), _F32)),
        in_specs=(_HBM, _SEM) + (_ANY,) * len(after), out_specs=(_SEM, _SEM, _HBM, _VMEM_TOKEN),
        input_output_aliases={0: 2}, compiler_params=_split_params(),
    )(full, recv, *after)


def _gather_wait(name, full, axis, n, send, recv, send2, recv2, after):
    def body(f_ref, send_r, recv_r, send2_r, recv2_r, after_ref, f_out):
        sib, _, me = _peer(_SIBLING)
        blk = _blk(f_ref, axis, n, me)
        for i in range(4):
            _remote(blk, send_r.at[i], recv_r.at[0], sib).wait_send()
        _remote(blk, send_r.at[0], recv_r.at[0], sib).wait_recv()
        for i in range(3):
            cp = _remote(blk, send2_r.at[i], recv2_r.at[i], sib)
            cp.wait_send()
            cp.wait_recv()

    return pl.pallas_call(
        body, name=name, out_shape=pltpu.HBM(full.shape, full.dtype),
        in_specs=(_HBM, _SEM, _SEM, _SEM, _SEM, _ANY), out_specs=_HBM,
        input_output_aliases={0: 0}, compiler_params=_split_params(),
    )(full, send, recv, send2, recv2, after)


def _scatter_start(name, dw, axis, n):
    shard = tuple(n if a == axis else d for a, d in enumerate(dw.shape))
    land = lax.empty((_N_PEER,) + shard, dw.dtype)

    def body(dw_ref, land_ref, send, recv, dw_out, land_out, token):
        for k in range(1, N_DEV):
            dev, to, _ = _peer(k)
            pltpu.make_async_remote_copy(
                src_ref=_blk(dw_ref, axis, n, to), dst_ref=land_ref.at[k - 1], send_sem=send.at[k - 1],
                recv_sem=recv.at[k - 1], device_id=dev, device_id_type=pl.DeviceIdType.MESH).start()
        token[...] = jnp.zeros_like(token)

    return pl.pallas_call(
        body, name=name,
        out_shape=(pltpu.SemaphoreType.DMA((_N_PEER,)), pltpu.SemaphoreType.DMA((_N_PEER,)),
                   pltpu.HBM(dw.shape, dw.dtype), pltpu.HBM(land.shape, land.dtype), _S((8, LANE), _F32)),
        in_specs=(_HBM, _HBM), out_specs=(_SEM, _SEM, _HBM, _HBM, pl.BlockSpec(memory_space=pltpu.VMEM)),
        input_output_aliases={0: 2, 1: 3}, compiler_params=_split_params(),
    )(pltpu.with_memory_space_constraint(dw, pltpu.HBM), pltpu.with_memory_space_constraint(land, pltpu.HBM))


def _scatter_wait(name, dw, land, send, recv, axis, n, after):
    def body(dw_ref, land_ref, send_r, recv_r, after_ref, dw_out, land_out):
        for k in range(1, N_DEV):
            dev, to, _ = _peer(k)
            cp = pltpu.make_async_remote_copy(
                src_ref=_blk(dw_ref, axis, n, to), dst_ref=land_ref.at[k - 1], send_sem=send_r.at[k - 1],
                recv_sem=recv_r.at[k - 1], device_id=dev, device_id_type=pl.DeviceIdType.MESH)
            cp.wait_send()
            cp.wait_recv()

    return pl.pallas_call(
        body, name=name, out_shape=(pltpu.HBM(dw.shape, dw.dtype), pltpu.HBM(land.shape, land.dtype)),
        in_specs=(_HBM, _HBM, _SEM, _SEM, _ANY), out_specs=(_HBM, _HBM), input_output_aliases={0: 0, 1: 1},
        compiler_params=_split_params(),
    )(dw, land, send, recv, after)


def _adamw_big(me1, dw, land, w, m, v, axis, n, name, layer=None, into=None):
    R, C = land.shape[1:]
    tr = _row_tile(R, 128)
    nb = R // tr
    lead = () if layer is None else (None,)
    pre = () if layer is None else (layer,)

    def body(me_ref, own_ref, land_ref, w_ref, m_ref, v_ref, *rest):
        g_ref, d_ref, nm_ref, nv_ref = rest[-4:]
        g = own_ref[...].astype(_F32)
        for s in range(_N_PEER):
            g = g + land_ref[s].astype(_F32)
        nm = ADAM_B1 * m_ref[...] + (1.0 - ADAM_B1) * g
        nv = ADAM_B2 * v_ref[...] + (1.0 - ADAM_B2) * (g * g)
        m_hat = nm / (1.0 - ADAM_B1 ** ADAM_STEP)
        v_hat = nv / (1.0 - ADAM_B2 ** ADAM_STEP)
        g_ref[...] = g
        d_ref[...] = -ADAM_LR * (m_hat / (jnp.sqrt(v_hat) + ADAM_EPS) + ADAM_WD * w_ref[...])
        nm_ref[...] = nm
        nv_ref[...] = nv

    if axis == 1:
        own_spec = pl.BlockSpec((tr, C), lambda i, me: (i, me[0]))
    else:
        own_spec = pl.BlockSpec((tr, C), lambda i, me: (me[0] * nb + i, 0))
    blk = pl.BlockSpec(lead + (tr, C), lambda i, me: pre + (i, 0))
    in_specs = [own_spec, pl.BlockSpec((_N_PEER, tr, C), lambda i, me: (0, i, 0)), blk, blk, blk]
    args = [me1, dw, land, w, m, v]
    aliases = {}
    if into is not None:
        in_specs += [_ANY] * 4
        aliases = {6 + j: j for j in range(4)}
        args += list(into)
    return pl.pallas_call(
        body, name=name,
        grid_spec=pltpu.PrefetchScalarGridSpec(num_scalar_prefetch=1, grid=(nb,), in_specs=in_specs, out_specs=[blk] * 4),
        out_shape=[_S(w.shape, _F32)] * 4, input_output_aliases=aliases, compiler_params=_params(("parallel",)),
    )(*args)


def _pack(arrs):
    flat = jnp.concatenate([a.reshape(-1).astype(_F32) for a in arrs])
    pad = (-flat.shape[0]) % (LANE * LANE)
    return jnp.pad(flat, (0, pad)).reshape(-1, LANE)


def _unpack(mat, shapes):
    flat = mat.reshape(-1)
    out, off = [], 0
    for s in shapes:
        n = 1
        for d in s:
            n *= d
        out.append(flat[off:off + n].reshape(s))
        off += n
    return out


def _lb_of(lb_param):
    lb_all = jnp.cumsum(jax.nn.softmax(lb_param.astype(_F32), axis=0), axis=0)
    return (lb_all - lb_all[0])[1:2]


def kernel(x, ev_w_in, ev_ln_v_g, ev_ln_v_b, ev_w_s, ev_b_s, ev_w_pool, ev_pool_scale, ev_w_out, od_w_in, od_norm_g, od_w_out, lb_param, ffn_w_up, ffn_conv_w, ffn_conv_b, ffn_w_down, ln1_g, ln1_b, ln2_g, ln2_b, loss_target, m_ev_w_in, m_ev_ln_v_g, m_ev_ln_v_b, m_ev_w_s, m_ev_b_s, m_ev_w_pool, m_ev_pool_scale, m_ev_w_out, m_od_w_in, m_od_norm_g, m_od_w_out, m_lb_param, m_ffn_w_up, m_ffn_conv_w, m_ffn_conv_b, m_ffn_w_down, m_ln1_g, m_ln1_b, m_ln2_g, m_ln2_b, v_ev_w_in, v_ev_ln_v_g, v_ev_ln_v_b, v_ev_w_s, v_ev_b_s, v_ev_w_pool, v_ev_pool_scale, v_ev_w_out, v_od_w_in, v_od_norm_g, v_od_w_out, v_lb_param, v_ffn_w_up, v_ffn_conv_w, v_ffn_conv_b, v_ffn_w_down, v_ln1_g, v_ln1_b, v_ln2_g, v_ln2_b):
    me = 4 * lax.axis_index("x") + 2 * lax.axis_index("y") + lax.axis_index("c")
    T, D = x.shape[1], x.shape[2]
    W = ev_ln_v_g.shape[1]
    H = W // A_HEAD
    Wg = W // B_GROUPS
    F2 = ffn_conv_b.shape[1]
    F = F2 // 2
    n_in0, n_out0 = ev_w_in.shape[2], ev_w_out.shape[1]
    n_in1, n_out1 = od_w_in.shape[2], od_w_out.shape[1]
    n_up, n_dn = ffn_w_up.shape[2], ffn_w_down.shape[1]
    n_pool, n_ng, n_cw = ev_w_pool.shape[2], od_norm_g.shape[1], ffn_conv_w.shape[2]

    small_shards = [od_norm_g, ffn_conv_w, ev_w_pool]
    small_pack = _pack(small_shards)
    small_all = _exchange("gather_small_params", [small_pack], [_S((N_DEV,) + small_pack.shape, _F32)], [_slot_job(0, 0)])[0]

    me1 = me.astype(jnp.int32).reshape(1)
    weights = [
        ("w_in0", ev_w_in[0], None, 1, n_in0), ("w_out0", ev_w_out[0], None, 0, n_out0),
        ("w_up0", ffn_w_up, 0, 1, n_up), ("w_dn0", ffn_w_down, 0, 0, n_dn),
        ("w_in1", od_w_in[0], None, 1, n_in1), ("w_out1", od_w_out[0], None, 0, n_out1),
        ("w_up1", ffn_w_up, 1, 1, n_up), ("w_dn1", ffn_w_down, 1, 0, n_dn),
    ]
    started, tokens = {}, [small_all]
    for key, w, layer, axis, n in weights:
        full = _place(w, me1, axis, "place_" + key, layer, deps=tokens)
        send, recv, full, token = _gather_start("gather_start_" + key, full, axis, n)
        started[key] = (full, axis, n, send, recv)
        tokens = [token]

    def pass_on(key, after):
        full, axis, n, send, recv = started[key]
        send2, recv2, full, token = _gather_forward("gather_forward_" + key, full, axis, n, recv, after)
        started[key] = (full, axis, n, send, recv, send2, recv2)
        return token

    def gathered(key, after):
        return _gather_wait("gather_wait_" + key, *started[key], after)

    ng_parts, cw_parts, wp_parts = [], [], []
    for j in range(N_DEV):
        a, b, c = _unpack(small_all[j], [s.shape for s in small_shards])
        ng_parts.append(a)
        cw_parts.append(b)
        wp_parts.append(c)
    norm_g = jnp.concatenate(ng_parts, axis=1)
    conv_w = jnp.concatenate(cw_parts, axis=2)
    w_pool = jnp.concatenate(wp_parts, axis=2)[0]
    cw_l = [conv_w[l].reshape(3, 2, F).transpose(1, 0, 2) for l in range(DEPTH)]
    cb_l = [ffn_conv_b[l].reshape(2, 1, F) for l in range(DEPTH)]
    ws_tril = jnp.tril(ev_w_s[0]).astype(_MM)
    bias = jnp.repeat(ev_b_s[0].T, A_HEAD, axis=1)
    wp_b = w_pool.astype(_MM)
    lb, lb_vjp = jax.vjp(_lb_of, lb_param)

    small_names = ["ev_ln_v_g", "ev_ln_v_b", "ev_w_s", "ev_b_s", "ev_w_pool", "ev_pool_scale", "od_norm_g", "lb_param",
                   "ffn_conv_w", "ffn_conv_b", "ln1_g", "ln1_b", "ln2_g", "ln2_b"]
    given = dict(ev_ln_v_g=(ev_ln_v_g, m_ev_ln_v_g, v_ev_ln_v_g), ev_ln_v_b=(ev_ln_v_b, m_ev_ln_v_b, v_ev_ln_v_b),
                 ev_w_s=(ev_w_s, m_ev_w_s, v_ev_w_s), ev_b_s=(ev_b_s, m_ev_b_s, v_ev_b_s),
                 ev_w_pool=(ev_w_pool, m_ev_w_pool, v_ev_w_pool),
                 ev_pool_scale=(ev_pool_scale, m_ev_pool_scale, v_ev_pool_scale),
                 od_norm_g=(od_norm_g, m_od_norm_g, v_od_norm_g), lb_param=(lb_param, m_lb_param, v_lb_param),
                 ffn_conv_w=(ffn_conv_w, m_ffn_conv_w, v_ffn_conv_w), ffn_conv_b=(ffn_conv_b, m_ffn_conv_b, v_ffn_conv_b),
                 ln1_g=(ln1_g, m_ln1_g, v_ln1_g), ln1_b=(ln1_b, m_ln1_b, v_ln1_b), ln2_g=(ln2_g, m_ln2_g, v_ln2_g),
                 ln2_b=(ln2_b, m_ln2_b, v_ln2_b))
    shard_axis = dict(ev_w_pool=2, od_norm_g=1, ffn_conv_w=2)
    rep_names = [n for n in small_names if n not in shard_axis]
    shd_names = [n for n in small_names if n in shard_axis]
    small_packs = [_pack([given[n][j] for n in small_names]) for j in range(3)]

    x2 = x[0]
    xb = _cast(x2, _MM, "cast_x", deps=[pass_on("w_in0", tokens[0])])
    w_in0 = gathered("w_in0", xb)
    h0 = _mm(xb, w_in0, "nn", _F32, "ev_in", out_parts=3)
    tie = pass_on("w_out0", h0)
    yab = _ev_mid_fwd(h0, ev_ln_v_g + tie[0, 0], ev_ln_v_b, ws_tril, bias, wp_b, ev_pool_scale, "ev_mid_fwd")
    w_out0 = gathered("w_out0", yab)
    z1 = _mm(yab, w_out0, "nn", _F32, "ev_out", add=x2, add_scale=ALPHA)
    tie = pass_on("w_up0", (z1, *small_packs))
    x1, x1b = _ln_fwd(z1, ln1_g[0:1] + tie[0, 0], ln1_b[0:1], "ln1_0")
    w_up0 = gathered("w_up0", x1b)
    hf0 = _mm(x1b, w_up0, "nn", _F32, "ffn_up", out_parts=2)
    tie = pass_on("w_dn0", hf0)
    act0 = _ffn_mid_fwd(hf0, cw_l[0], cb_l[0] + tie[0, 0], "ffn_mid_fwd")
    w_dn0 = gathered("w_dn0", act0)
    z2 = _mm(act0, w_dn0, "nn", _F32, "ffn_down", add=x1, add_scale=ALPHA)
    tie = pass_on("w_in1", z2)
    x2_, x2b = _ln_fwd(z2, ln2_g[0:1] + tie[0, 0], ln2_b[0:1], "ln2_0")
    w_in1 = gathered("w_in1", x2b)
    h1 = _mm(x2b, w_in1, "nn", _F32, "od_in", out_parts=4)
    qd, kd, ke, vb, dec = _hgrn_prep_fwd(h1, lb, "hgrn_prep_fwd")
    tie = pass_on("w_out1", qd)
    o, yo, st = _hgrn_scan_fwd(qd, kd, ke, vb, dec, h1, norm_g + tie[0, 0], "hgrn_scan_fwd")
    w_out1 = gathered("w_out1", yo)
    z3 = _mm(yo, w_out1, "nn", _F32, "od_out", add=x2_, add_scale=ALPHA)
    tie = pass_on("w_up1", z3)
    x3, x3b = _ln_fwd(z3, ln1_g[1:2] + tie[0, 0], ln1_b[1:2], "ln1_1")
    w_up1 = gathered("w_up1", x3b)
    hf1 = _mm(x3b, w_up1, "nn", _F32, "ffn_up", out_parts=2)
    tie = pass_on("w_dn1", hf1)
    act1 = _ffn_mid_fwd(hf1, cw_l[1], cb_l[1] + tie[0, 0], "ffn_mid_fwd")
    w_dn1 = gathered("w_dn1", act1)
    z4 = _mm(act1, w_dn1, "nn", _F32, "ffn_down", add=x3, add_scale=ALPHA)

    scat = {}

    def scatter(key, dw, axis, n):
        send, recv, dw, land, token = _scatter_start("scatter_start_" + key, dw, axis, n)
        scat[key] = (dw, land, send, recv, axis, n)
        return [token]

    loss11, dz4, dz4b, g_ln2_1, b_ln2_1 = _ln_loss_bwd(z4, ln2_g[1:2], ln2_b[1:2], loss_target[0], "ln_loss_bwd")
    tok = scatter("dn1", _mm(act1, dz4b, "tn", _XCH, "ffn_down_dw"), 0, n_dn)
    dact1 = _mm(dz4b, w_dn1, "nt", _F32, "ffn_down_dx", deps=tok)
    dhf1, dcw1, dcb1 = _ffn_mid_bwd(hf1, dact1, cw_l[1], cb_l[1], "ffn_mid_bwd")
    tok = scatter("up1", _mm(x3b, dhf1, "tn", _XCH, "ffn_up_dw", b_parts=2, deps=tok), 1, n_up)
    dx3 = _mm(dhf1, w_up1, "nt", _F32, "ffn_up_dx", a_parts=2, add=dz4, add_scale=ALPHA, deps=tok)
    dz3, dz3b, g_ln1_1, b_ln1_1 = _ln_bwd(z3, ln1_g[1:2], dx3, "ln_bwd")
    tok = scatter("out1", _mm(yo, dz3b, "tn", _XCH, "od_out_dw", deps=tok), 0, n_out1)
    dyo = _mm(dz3b, w_out1, "nt", _F32, "od_out_dx", deps=tok)
    dqd, dkd, dke, dv, dgate, ddec, dng = _hgrn_scan_bwd(qd, kd, ke, vb, dec, st, o, h1, norm_g, dyo, "hgrn_scan_bwd")
    dh1, dlb = _hgrn_prep_bwd(h1, lb, dqd, dkd, dke, dv, dgate, ddec, "hgrn_prep_bwd")
    tok = scatter("in1", _mm(x2b, dh1, "tn", _XCH, "od_in_dw", b_parts=4, deps=tok), 1, n_in1)
    dx2 = _mm(dh1, w_in1, "nt", _F32, "od_in_dx", a_parts=4, add=dz3, add_scale=ALPHA, deps=tok)
    dz2, dz2b, g_ln2_0, b_ln2_0 = _ln_bwd(z2, ln2_g[0:1], dx2, "ln_bwd")
    tok = scatter("dn0", _mm(act0, dz2b, "tn", _XCH, "ffn_down_dw", deps=tok), 0, n_dn)
    dact0 = _mm(dz2b, w_dn0, "nt", _F32, "ffn_down_dx", deps=tok)
    dhf0, dcw0, dcb0 = _ffn_mid_bwd(hf0, dact0, cw_l[0], cb_l[0], "ffn_mid_bwd")
    tok = scatter("up0", _mm(x1b, dhf0, "tn", _XCH, "ffn_up_dw", b_parts=2, deps=tok), 1, n_up)
    dx1 = _mm(dhf0, w_up0, "nt", _F32, "ffn_up_dx", a_parts=2, add=dz2, add_scale=ALPHA, deps=tok)
    dz1, dz1b, g_ln1_0, b_ln1_0 = _ln_bwd(z1, ln1_g[0:1], dx1, "ln_bwd")
    tok = scatter("out0", _mm(yab, dz1b, "tn", _XCH, "ev_out_dw", deps=tok), 0, n_out0)
    dyab = _mm(dz1b, w_out0, "nt", _F32, "ev_out_dx", deps=tok)
    dh0, dws, dbias, dlng, dlnb, dwp, dsc = _ev_mid_bwd(h0, dyab, ev_ln_v_g, ev_ln_v_b, ws_tril, bias, wp_b,
                                                        ev_pool_scale, "ev_mid_bwd")

    g_b_s = dbias.reshape(A_CHUNK, H, A_HEAD).sum(axis=-1).T[None]
    g_conv_w = jnp.stack([d.transpose(1, 0, 2).reshape(3, F2) for d in (dcw0, dcw1)])
    g_conv_b = jnp.stack([d.reshape(F2) for d in (dcb0, dcb1)])
    small_grads = dict(zip(small_names, [
        dlng, dlnb, dws[None], g_b_s, dwp[None], dsc, dng, lb_vjp(dlb)[0], g_conv_w, g_conv_b,
        jnp.concatenate([g_ln1_0, g_ln1_1]), jnp.concatenate([b_ln1_0, b_ln1_1]),
        jnp.concatenate([g_ln2_0, g_ln2_1]), jnp.concatenate([b_ln2_0, b_ln2_1])]))

    def by_device(g, ax):
        g = g.reshape(g.shape[:ax] + (N_DEV, g.shape[ax] // N_DEV) + g.shape[ax + 1:])
        return jnp.moveaxis(g, ax, 0).reshape(N_DEV, -1)

    shd = jnp.concatenate([by_device(small_grads[n], shard_axis[n]) for n in shd_names], axis=1)
    shd_pack = jnp.pad(shd, ((0, 0), (0, (-shd.shape[1]) % (LANE * LANE)))).reshape(-1, LANE)
    shd_rows = shd_pack.shape[0] // N_DEV
    rep_pack = _pack([small_grads[n] for n in rep_names])
    tok = scatter("in0", _mm(xb, dh0, "tn", _XCH, "ev_in_dw", b_parts=3, deps=tok), 1, n_in0)
    tok = scatter("small_rep", rep_pack + tok[0][0, 0], None, None)
    tok = scatter("small_shd", shd_pack + tok[0][0, 0], 0, shd_rows)
    grad_x = _mm(dh0, w_in0, "nt", _F32, "ev_in_dx", a_parts=3, add=dz1, add_scale=ALPHA, deps=tok)

    def landed(key, after):
        dw, land, send, recv, axis, n = scat[key]
        dw, land = _scatter_wait("scatter_wait_" + key, dw, land, send, recv, axis, n, after)
        return me1, dw, land

    big = {}
    r_dn = _adamw_big(*landed("dn1", grad_x), ffn_w_down, m_ffn_w_down, v_ffn_w_down, 0, n_dn, "adamw_w_dn1", layer=1)
    r_up = _adamw_big(*landed("up1", r_dn[0]), ffn_w_up, m_ffn_w_up, v_ffn_w_up, 1, n_up, "adamw_w_up1", layer=1)
    big["od_w_out"] = _adamw_big(*landed("out1", r_up[0]), od_w_out[0], m_od_w_out[0], v_od_w_out[0], 0, n_out1, "adamw_w_out1")
    big["od_w_in"] = _adamw_big(*landed("in1", big["od_w_out"][0]), od_w_in[0], m_od_w_in[0], v_od_w_in[0], 1, n_in1, "adamw_w_in1")
    big["ffn_w_down"] = _adamw_big(*landed("dn0", big["od_w_in"][0]), ffn_w_down, m_ffn_w_down, v_ffn_w_down, 0, n_dn,
                                   "adamw_w_dn0", layer=0, into=r_dn)
    big["ffn_w_up"] = _adamw_big(*landed("up0", big["ffn_w_down"][0]), ffn_w_up, m_ffn_w_up, v_ffn_w_up, 1, n_up,
                                 "adamw_w_up0", layer=0, into=r_up)
    big["ev_w_out"] = _adamw_big(*landed("out0", big["ffn_w_up"][0]), ev_w_out[0], m_ev_w_out[0], v_ev_w_out[0], 0, n_out0, "adamw_w_out0")
    big["ev_w_in"] = _adamw_big(*landed("in0", big["ev_w_out"][0]), ev_w_in[0], m_ev_w_in[0], v_ev_w_in[0], 1, n_in0, "adamw_w_in0")

    rep_mat = _sum_in_device_order(*landed("small_rep", big["ev_w_in"][0]), "sum_small_rep")
    local_g = dict(zip(rep_names, _unpack(rep_mat, [small_grads[n].shape for n in rep_names])))
    _, shd_all, shd_land = landed("small_shd", rep_mat)
    shd_own = lax.dynamic_slice_in_dim(shd_all, me * shd_rows, shd_rows, axis=0)
    shd_mat = _sum_in_device_order(me1, shd_own, shd_land, "sum_small_shd")
    local_g.update(zip(shd_names, _unpack(shd_mat, [given[n][0].shape for n in shd_names])))
    local_shapes = [given[n][0].shape for n in small_names]
    res = _adamw(_pack([local_g[n] for n in small_names])[None], *small_packs, "adamw_small")
    small = {n: [] for n in small_names}
    for r in res:
        for n, a in zip(small_names, _unpack(r, local_shapes)):
            small[n].append(a)

    loss = lax.psum(loss11[0, 0], ("x", "y", "c"))
    order = ["ev_w_in", "ev_ln_v_g", "ev_ln_v_b", "ev_w_s", "ev_b_s", "ev_w_pool", "ev_pool_scale", "ev_w_out", "od_w_in",
             "od_norm_g", "od_w_out", "lb_param", "ffn_w_up", "ffn_conv_w", "ffn_conv_b", "ffn_w_down", "ln1_g", "ln1_b",
             "ln2_g", "ln2_b"]
    shapes = dict(ev_w_in=ev_w_in.shape, ev_w_out=ev_w_out.shape, od_w_in=od_w_in.shape, od_w_out=od_w_out.shape,
                  ffn_w_up=ffn_w_up.shape, ffn_w_down=ffn_w_down.shape)
    outs = [loss, grad_x[None]]
    for kind in range(4):
        for n in order:
            outs.append(big[n][kind].reshape(shapes[n]) if n in big else small[n][kind])
    return tuple(outs)
```

```python
import functools

import jax
import jax.numpy as jnp
from jax import lax
from jax.experimental import pallas as pl
from jax.experimental.pallas import tpu as pltpu

_MM = jnp.bfloat16
_XCH = jnp.bfloat16

DEPTH = 2
ALPHA = (2 * DEPTH) ** 0.25
LN_EPS = 1e-5
A_CHUNK = 128
A_HEAD = 128
B_GROUPS = 4
POOL_HALO = 16
C_CHUNK = 64
C_HEAD = 128
SCAN_UNROLL = 8
CONV_HALO = 8
PACKED_ROWS = 16
FFN_ROWS, FFN_FWD_COLS = 512, 1408
FFN_BWD_ROWS, FFN_BWD_COLS = 512, 512
FFN_CHUNK = 128
ADAM_LR, ADAM_B1, ADAM_B2, ADAM_EPS, ADAM_WD, ADAM_STEP = 0.001, 0.9, 0.999, 1e-08, 0.01, 10
N_DEV = 8
LANE = 128
VMEM_LIMIT = 56 * 1024 * 1024
MM_FULL_K = 2048
MM_FULL_K_TN = 4096
MM_DEEP_K = 2816

_F32 = jnp.float32
_NN = (((1,), (0,)), ((), ()))
_NT = (((1,), (1,)), ((), ()))
_TN = (((0,), (0,)), ((), ()))
_S = jax.ShapeDtypeStruct


def _dot(a, b, dims=_NN):
    return lax.dot_general(a, b, dims, preferred_element_type=_F32)


def _tile(dim, pref):
    best = None
    d = LANE
    while d <= min(dim, pref):
        if dim % d == 0:
            best = d
        d += LANE
    return best if best is not None else dim


def _params(sem):
    return pltpu.CompilerParams(dimension_semantics=sem, vmem_limit_bytes=VMEM_LIMIT)


def _sigmoid(x):
    return 0.5 * jnp.tanh(0.5 * x) + 0.5


def _sigmoid_rel(x):
    return 1.0 / (1.0 + jnp.exp(-x))


_GELU_C = 0.7978845608028654
_GELU_A = 0.044715


def _gelu_and_grad(x):
    t = jnp.tanh(_GELU_C * (x + _GELU_A * x * x * x))
    y = 0.5 * x * (1.0 + t)
    dy = 0.5 * (1.0 + t) + 0.5 * x * (1.0 - t * t) * _GELU_C * (1.0 + 3.0 * _GELU_A * x * x)
    return y, dy


def _row_index(n):
    return lax.broadcasted_iota(jnp.int32, (n, 1), 0)


def _mm_tiles(mode, M, N, K, with_add):
    if mode == "tn":
        return _tile(M, 1024), _tile(N, 1024), _tile(K, MM_FULL_K_TN)
    if K <= MM_FULL_K:
        return _tile(M, 1024 if with_add else 2048), _tile(N, 1024 if mode == "nn" else 512), K
    return _tile(M, 1024), _tile(N, 1024), _tile(K, MM_DEEP_K)


def _mm(a, b, mode, out_dtype, name, *, a_parts=1, b_parts=1, out_parts=1, add=None, add_scale=1.0, deps=(), tiles=None):
    if mode == "nn":
        M, K = a.shape
        N = b.shape[1]
    elif mode == "nt":
        if a_parts > 1:
            M, K = a.shape[1], a.shape[2] * a_parts
        else:
            M, K = a.shape
        N = b.shape[0]
    else:
        K, M = a.shape
        N = b.shape[-1] * b_parts
    tm, tn, tk = tiles if tiles is not None else _mm_tiles(mode, M, N // max(b_parts, out_parts), K // a_parts, add is not None)
    nk = K // tk
    npj = (N // max(b_parts, out_parts)) // tn
    nkp = (K // a_parts) // tk
    if mode == "nn":
        a_spec = pl.BlockSpec((tm, tk), lambda i, j, k: (i, k))
        b_spec = pl.BlockSpec((tk, tn), lambda i, j, k: (k, j))
        dims = _NN
    elif mode == "nt":
        if a_parts > 1:
            a_spec = pl.BlockSpec((None, tm, tk), lambda i, j, k: (k // nkp, i, k % nkp))
        else:
            a_spec = pl.BlockSpec((tm, tk), lambda i, j, k: (i, k))
        b_spec = pl.BlockSpec((tn, tk), lambda i, j, k: (j, k))
        dims = _NT
    else:
        a_spec = pl.BlockSpec((tk, tm), lambda i, j, k: (k, i))
        if b_parts > 1:
            b_spec = pl.BlockSpec((None, tk, tn), lambda i, j, k: (j // npj, k, j % npj))
        else:
            b_spec = pl.BlockSpec((tk, tn), lambda i, j, k: (k, j))
        dims = _TN
    if out_parts > 1:
        out_spec = pl.BlockSpec((None, tm, tn), lambda i, j, k: (j // npj, i, j % npj))
        out_shape = _S((out_parts, M, N // out_parts), out_dtype)
    else:
        out_spec = pl.BlockSpec((tm, tn), lambda i, j, k: (i, j))
        out_shape = _S((M, N), out_dtype)
    in_specs = [a_spec, b_spec]
    args = [a, b]
    if add is not None:
        in_specs.append(pl.BlockSpec((tm, tn), lambda i, j, k: (i, j)))
        args.append(add)
    in_specs += [_ANY] * len(deps)
    args += list(deps)

    def finish(r, refs, o_ref):
        if add is not None:
            r = r + add_scale * refs[2][...]
        o_ref[...] = r.astype(o_ref.dtype)

    def body_one(*refs):
        finish(_dot(refs[0][...], refs[1][...], dims), refs, refs[-1])

    def body_acc(*refs):
        o_ref, acc = refs[-2], refs[-1]
        k = pl.program_id(2)

        @pl.when(k == 0)
        def _():
            acc[...] = jnp.zeros_like(acc)

        acc[...] += _dot(refs[0][...], refs[1][...], dims)

        @pl.when(k == nk - 1)
        def _():
            finish(acc[...], refs, o_ref)

    return pl.pallas_call(
        body_one if nk == 1 else body_acc, name=name, grid=(M // tm, N // tn, nk), in_specs=in_specs,
        out_specs=out_spec, out_shape=out_shape,
        scratch_shapes=[] if nk == 1 else [pltpu.VMEM((tm, tn), _F32)],
        compiler_params=_params(("parallel", "parallel", "arbitrary")),
    )(*args)


def _cast(x2d, dtype, name, deps=()):
    R, C = x2d.shape
    tr = _tile(R, 512) if R % LANE == 0 else R

    def body(x_ref, *rest):
        rest[-1][...] = x_ref[...].astype(rest[-1].dtype)

    return pl.pallas_call(
        body, name=name, grid=(R // tr,), in_specs=[pl.BlockSpec((tr, C), lambda i: (i, 0))] + [_ANY] * len(deps),
        out_specs=pl.BlockSpec((tr, C), lambda i: (i, 0)), out_shape=_S((R, C), dtype),
        compiler_params=_params(("parallel",)),
    )(x2d, *deps)


def _ln_fwd(z, g, b, name):
    T, D = z.shape
    tr = _tile(T, 256)

    def body(z_ref, g_ref, b_ref, y_ref, yb_ref):
        zz = z_ref[...]
        mu = jnp.mean(zz, axis=-1, keepdims=True)
        zc = zz - mu
        var = jnp.mean(zc * zc, axis=-1, keepdims=True)
        y = zc * lax.rsqrt(var + LN_EPS) * g_ref[...] + b_ref[...]
        y_ref[...] = y
        yb_ref[...] = y.astype(yb_ref.dtype)

    row = pl.BlockSpec((tr, D), lambda i: (i, 0))
    vec = pl.BlockSpec((1, D), lambda i: (0, 0))
    return pl.pallas_call(
        body, name=name, grid=(T // tr,), in_specs=[row, vec, vec], out_specs=[row, row],
        out_shape=[_S((T, D), _F32), _S((T, D), _MM)], compiler_params=_params(("parallel",)),
    )(z, g, b)


def _ln_bwd(z, g, dy, name):
    T, D = z.shape
    tr = _tile(T, 256)

    def body(z_ref, g_ref, dy_ref, dz_ref, dzb_ref, dg_ref, db_ref):
        @pl.when(pl.program_id(0) == 0)
        def _():
            dg_ref[...] = jnp.zeros_like(dg_ref)
            db_ref[...] = jnp.zeros_like(db_ref)

        zz = z_ref[...]
        mu = jnp.mean(zz, axis=-1, keepdims=True)
        zc = zz - mu
        rstd = lax.rsqrt(jnp.mean(zc * zc, axis=-1, keepdims=True) + LN_EPS)
        xh = zc * rstd
        d = dy_ref[...]
        dg_ref[...] += jnp.sum(d * xh, axis=0, keepdims=True)
        db_ref[...] += jnp.sum(d, axis=0, keepdims=True)
        dxh = d * g_ref[...]
        dz = rstd * (dxh - jnp.mean(dxh, axis=-1, keepdims=True) - xh * jnp.mean(dxh * xh, axis=-1, keepdims=True))
        dz_ref[...] = dz
        dzb_ref[...] = dz.astype(dzb_ref.dtype)

    row = pl.BlockSpec((tr, D), lambda i: (i, 0))
    vec = pl.BlockSpec((1, D), lambda i: (0, 0))
    return pl.pallas_call(
        body, name=name, grid=(T // tr,), in_specs=[row, vec, row], out_specs=[row, row, vec, vec],
        out_shape=[_S((T, D), _F32), _S((T, D), _MM), _S((1, D), _F32), _S((1, D), _F32)],
        compiler_params=_params(("arbitrary",)),
    )(z, g, dy)


def _ln_loss_bwd(z, g, b, target, name):
    T, D = z.shape
    tr = _tile(T, 256)

    def body(z_ref, g_ref, b_ref, t_ref, loss_ref, dz_ref, dzb_ref, dg_ref, db_ref, lacc):
        i = pl.program_id(0)

        @pl.when(i == 0)
        def _():
            dg_ref[...] = jnp.zeros_like(dg_ref)
            db_ref[...] = jnp.zeros_like(db_ref)
            lacc[...] = jnp.zeros_like(lacc)

        zz = z_ref[...]
        mu = jnp.mean(zz, axis=-1, keepdims=True)
        zc = zz - mu
        rstd = lax.rsqrt(jnp.mean(zc * zc, axis=-1, keepdims=True) + LN_EPS)
        xh = zc * rstd
        err = xh * g_ref[...] + b_ref[...] - t_ref[...]
        lacc[...] += jnp.sum(err * err, axis=0, keepdims=True)
        d = err * (1.0 / D)
        dg_ref[...] += jnp.sum(d * xh, axis=0, keepdims=True)
        db_ref[...] += jnp.sum(d, axis=0, keepdims=True)
        dxh = d * g_ref[...]
        dz = rstd * (dxh - jnp.mean(dxh, axis=-1, keepdims=True) - xh * jnp.mean(dxh * xh, axis=-1, keepdims=True))
        dz_ref[...] = dz
        dzb_ref[...] = dz.astype(dzb_ref.dtype)

        @pl.when(i == pl.num_programs(0) - 1)
        def _():
            loss_ref[...] = jnp.sum(lacc[...], axis=-1, keepdims=True) * (0.5 / D)

    row = pl.BlockSpec((tr, D), lambda i: (i, 0))
    vec = pl.BlockSpec((1, D), lambda i: (0, 0))
    one = pl.BlockSpec((1, 1), lambda i: (0, 0))
    return pl.pallas_call(
        body, name=name, grid=(T // tr,), in_specs=[row, vec, vec, row], out_specs=[one, row, row, vec, vec],
        out_shape=[_S((1, 1), _F32), _S((T, D), _F32), _S((T, D), _MM), _S((1, D), _F32), _S((1, D), _F32)],
        scratch_shapes=[pltpu.VMEM((1, D), _F32)], compiler_params=_params(("arbitrary",)),
    )(z, g, b, target)


def _conv3(X, cw, cb):
    return cb + cw[2:3] * X + cw[1:2] * pltpu.roll(X, 1, 0) + cw[0:1] * pltpu.roll(X, 2, 0)


def _ffn_mid_fwd(h, cw, cb, name):
    _, T, F = h.shape
    tr = _tile(T, FFN_ROWS)
    tc = _tile(F, FFN_FWD_COLS)
    nb = tr // CONV_HALO

    rc = _tile(tr, FFN_CHUNK)
    lanes = [slice(cs * LANE, (cs + 1) * LANE) for cs in range(tc // LANE)]

    def body(h_ref, p_ref, cw_ref, cb_ref, o_ref, c_ref):
        i = pl.program_id(0)

        def work(r0, cols, X):
            hc = [_conv3(X[part], cw_ref[part, :, cols], cb_ref[part, :, cols])[CONV_HALO:] for part in range(2)]
            for part in range(2):
                c_ref[part, pl.ds(r0, rc), cols] = hc[part].astype(c_ref.dtype)
            a, v = hc
            o_ref[pl.ds(r0, rc), cols] = (a * _sigmoid(a) * v).astype(o_ref.dtype)

        for cols in lanes:
            work(0, cols, [jnp.concatenate([jnp.where(i == 0, 0.0, p_ref[part, :, cols]), h_ref[part, 0:rc, cols]], axis=0)
                           for part in range(2)])

        def chunk(c, carry):
            r0 = pl.multiple_of(c * rc, rc)
            for cols in lanes:
                work(r0, cols, [h_ref[part, pl.ds(r0 - CONV_HALO, rc + CONV_HALO), cols] for part in range(2)])
            return carry

        lax.fori_loop(1, tr // rc, chunk, 0)

    return pl.pallas_call(
        body, name=name, grid=(T // tr, F // tc),
        in_specs=[pl.BlockSpec((2, tr, tc), lambda i, j: (0, i, j)),
                  pl.BlockSpec((2, CONV_HALO, tc), lambda i, j: (0, jnp.maximum(i * nb - 1, 0), j)),
                  pl.BlockSpec((2, 3, tc), lambda i, j: (0, 0, j)),
                  pl.BlockSpec((2, 1, tc), lambda i, j: (0, 0, j))],
        out_specs=[pl.BlockSpec((tr, tc), lambda i, j: (i, j)), pl.BlockSpec((2, tr, tc), lambda i, j: (0, i, j))],
        out_shape=[_S((T, F), _MM), _S((2, T, F), _MM)],
        compiler_params=_params(("parallel", "parallel")),
    )(h, h, cw, cb)


def _ffn_mid_bwd(h, hc, dact, cw, name):
    _, T, F = h.shape
    tr = _tile(T, FFN_BWD_ROWS)
    tc = _tile(F, FFN_BWD_COLS)
    nb = tr // CONV_HALO
    nb_c = tr // PACKED_ROWS
    rc = _tile(tr, FFN_CHUNK)
    n = rc + CONV_HALO

    def body(h_ref, c_ref, cn_ref, d_ref, dn_ref, cw_ref, dh_ref, dcw_ref, dcb_ref):
        i = pl.program_id(1)
        is_last = i == pl.num_programs(1) - 1

        @pl.when(i == 0)
        def _():
            dcw_ref[...] = jnp.zeros_like(dcw_ref)
            dcb_ref[...] = jnp.zeros_like(dcb_ref)

        def work(r0, cols, a, v, D):
            sg = _sigmoid(a)
            dhc = [D * v * sg * (1.0 + a * (1.0 - sg)), D * a * sg]
            for part in range(2):
                X = h_ref[part, pl.ds(r0, rc), cols]
                cwp = cw_ref[part, :, cols]
                dh = None
                for k in range(3):
                    g = (dhc[part] if k == 0 else pltpu.roll(dhc[part], n - k, 0))[0:rc]
                    term = cwp[2 - k:3 - k] * g
                    dh = term if dh is None else dh + term
                    dcw_ref[part, 2 - k:3 - k, cols] += jnp.sum(g * X, axis=0, keepdims=True)
                    if k == 0:
                        dcb_ref[part, :, cols] += jnp.sum(g, axis=0, keepdims=True)
                dh_ref[part, pl.ds(r0, rc), cols] = dh.astype(dh_ref.dtype)

        lanes = [slice(cs * LANE, (cs + 1) * LANE) for cs in range(tc // LANE)]

        def chunk(c, carry):
            r0 = pl.multiple_of(c * rc, rc)
            for cols in lanes:
                a, v = [c_ref[part, pl.ds(r0, rc + PACKED_ROWS), cols].astype(_F32)[0:n] for part in range(2)]
                work(r0, cols, a, v, d_ref[pl.ds(r0, n), cols])
            return carry

        lax.fori_loop(0, tr // rc - 1, chunk, 0)
        r0 = tr - rc
        for cols in lanes:
            a, v = [jnp.concatenate([c_ref[part, r0:tr, cols].astype(_F32), cn_ref[part, :, cols].astype(_F32)[0:CONV_HALO]],
                                    axis=0) for part in range(2)]
            D = jnp.concatenate([d_ref[r0:tr, cols], jnp.where(is_last, 0.0, dn_ref[:, cols])], axis=0)
            work(r0, cols, a, v, D)

    return pl.pallas_call(
        body, name=name, grid=(F // tc, T // tr),
        in_specs=[pl.BlockSpec((2, tr, tc), lambda j, i: (0, i, j)),
                  pl.BlockSpec((2, tr, tc), lambda j, i: (0, i, j)),
                  pl.BlockSpec((2, PACKED_ROWS, tc), lambda j, i: (0, jnp.minimum((i + 1) * nb_c, T // PACKED_ROWS - 1), j)),
                  pl.BlockSpec((tr, tc), lambda j, i: (i, j)),
                  pl.BlockSpec((CONV_HALO, tc), lambda j, i: (jnp.minimum((i + 1) * nb, T // CONV_HALO - 1), j)),
                  pl.BlockSpec((2, 3, tc), lambda j, i: (0, 0, j))],
        out_specs=[pl.BlockSpec((2, tr, tc), lambda j, i: (0, i, j)),
                   pl.BlockSpec((2, 3, tc), lambda j, i: (0, 0, j)),
                   pl.BlockSpec((2, 1, tc), lambda j, i: (0, 0, j))],
        out_shape=[_S((2, T, F), _MM), _S((2, 3, F), _F32), _S((2, 1, F), _F32)],
        compiler_params=_params(("parallel", "arbitrary")),
    )(h, hc, hc, dact, dact, cw)


def _ev_common(h_ref, hp_ref, lng_ref, lnb_ref, ws_ref, bias_ref, i, tr, W):
    H = W // A_HEAD
    u, gu = _gelu_and_grad(h_ref[0])
    v, gv = _gelu_and_grad(h_ref[1])
    mu = jnp.mean(v, axis=-1, keepdims=True)
    vc = v - mu
    rstd = lax.rsqrt(jnp.mean(vc * vc, axis=-1, keepdims=True) + LN_EPS)
    vhat = vc * rstd
    vb = (vhat * lng_ref[...] + lnb_ref[...]).astype(_MM)
    s_chunks = []
    for c in range(tr // A_CHUNK):
        r0 = c * A_CHUNK
        heads = [_dot(ws_ref[hd], vb[r0:r0 + A_CHUNK, hd * A_HEAD:(hd + 1) * A_HEAD]) for hd in range(H)]
        s_chunks.append(jnp.concatenate(heads, axis=1) + bias_ref[...])
    prev = jnp.where(i == 0, 0.0, hp_ref[...])
    X = jnp.concatenate([prev, h_ref[2]], axis=0)
    return u, gu, gv, rstd, vhat, vb, s_chunks, X


def _pool_inv_count(i, tr, rows, win):
    pos = i * tr + _row_index(rows) + 1
    return 1.0 / jnp.minimum(pos, win).astype(_F32)


def _pool_fwd(X, g, Wg, i, tr):
    xg = X[:, g * Wg:(g + 1) * Wg]
    s = xg
    for k in range(g + 1):
        s = s + pltpu.roll(s, 2 ** k, 0)
    return s[POOL_HALO:] * _pool_inv_count(i, tr, tr, 2 ** (g + 1)) - xg[POOL_HALO:]


def _ev_mid_fwd(h, lng, lnb, ws, bias, wp, sc, name):
    _, T, W = h.shape
    tr = _tile(T, 256)
    H = W // A_HEAD
    Wg = W // B_GROUPS
    nb = tr // POOL_HALO

    def body(h_ref, hp_ref, lng_ref, lnb_ref, ws_ref, bias_ref, wp_ref, sc_ref, o_ref):
        i = pl.program_id(0)
        u, _, _, _, _, _, s_chunks, X = _ev_common(h_ref, hp_ref, lng_ref, lnb_ref, ws_ref, bias_ref, i, tr, W)
        for c, s in enumerate(s_chunks):
            r0 = c * A_CHUNK
            o_ref[r0:r0 + A_CHUNK, 0:W] = (u[r0:r0 + A_CHUNK] * s).astype(o_ref.dtype)
        for g in range(B_GROUPS):
            p = _pool_fwd(X, g, Wg, i, tr)
            y = _dot(p.astype(_MM), wp_ref[g]) * sc_ref[:, g * Wg:(g + 1) * Wg]
            o_ref[:, W + g * Wg:W + (g + 1) * Wg] = y.astype(o_ref.dtype)

    vec = pl.BlockSpec((1, W), lambda i: (0, 0))
    return pl.pallas_call(
        body, name=name, grid=(T // tr,),
        in_specs=[pl.BlockSpec((3, tr, W), lambda i: (0, i, 0)),
                  pl.BlockSpec((None, POOL_HALO, W), lambda i: (2, jnp.maximum(i * nb - 1, 0), 0)),
                  vec, vec,
                  pl.BlockSpec((H, A_CHUNK, A_CHUNK), lambda i: (0, 0, 0)),
                  pl.BlockSpec((A_CHUNK, W), lambda i: (0, 0)),
                  pl.BlockSpec((B_GROUPS, Wg, Wg), lambda i: (0, 0, 0)),
                  vec],
        out_specs=pl.BlockSpec((tr, 2 * W), lambda i: (i, 0)), out_shape=_S((T, 2 * W), _MM),
        compiler_params=_params(("parallel",)),
    )(h, h, lng, lnb, ws, bias, wp, sc)


def _ev_mid_bwd(h, dy, lng, lnb, ws, bias, wp, sc, name):
    _, T, W = h.shape
    tr = _tile(T, 256)
    H = W // A_HEAD
    Wg = W // B_GROUPS
    nb = tr // POOL_HALO
    last_blk = T // POOL_HALO - 1
    n = tr + POOL_HALO

    def body(h_ref, hp_ref, dy_ref, dyn_ref, lng_ref, lnb_ref, ws_ref, bias_ref, wp_ref, sc_ref,
             dh_ref, dws_ref, dbias_ref, dlng_ref, dlnb_ref, dwp_ref, dsc_ref):
        i = pl.program_id(0)

        @pl.when(i == 0)
        def _():
            for r in (dws_ref, dbias_ref, dlng_ref, dlnb_ref, dwp_ref, dsc_ref):
                r[...] = jnp.zeros_like(r)

        u, gu, gv, rstd, vhat, vb, s_chunks, X = _ev_common(h_ref, hp_ref, lng_ref, lnb_ref, ws_ref, bias_ref, i, tr, W)
        rr = lax.broadcasted_iota(jnp.int32, (A_CHUNK, A_CHUNK), 0)
        cc = lax.broadcasted_iota(jnp.int32, (A_CHUNK, A_CHUNK), 1)
        tril = rr >= cc
        du_chunks, dvln_chunks = [], []
        for c, s in enumerate(s_chunks):
            r0 = c * A_CHUNK
            dya = dy_ref[r0:r0 + A_CHUNK, 0:W]
            du_chunks.append(dya * s)
            ds = dya * u[r0:r0 + A_CHUNK]
            dbias_ref[...] += ds
            dsb = ds.astype(_MM)
            heads = []
            for hd in range(H):
                cols = slice(hd * A_HEAD, (hd + 1) * A_HEAD)
                dws_ref[hd] += jnp.where(tril, _dot(dsb[:, cols], vb[r0:r0 + A_CHUNK, cols], _NT), 0.0)
                heads.append(_dot(ws_ref[hd], dsb[:, cols], _TN))
            dvln_chunks.append(jnp.concatenate(heads, axis=1))
        du = jnp.concatenate(du_chunks, axis=0)
        dvln = jnp.concatenate(dvln_chunks, axis=0)
        dlng_ref[...] += jnp.sum(dvln * vhat, axis=0, keepdims=True)
        dlnb_ref[...] += jnp.sum(dvln, axis=0, keepdims=True)
        dxh = dvln * lng_ref[...]
        dv = rstd * (dxh - jnp.mean(dxh, axis=-1, keepdims=True) - vhat * jnp.mean(dxh * vhat, axis=-1, keepdims=True))
        dh_ref[0] = (du * gu).astype(dh_ref.dtype)
        dh_ref[1] = (dv * gv).astype(dh_ref.dtype)

        dyb = dy_ref[:, W:2 * W]
        dyb_full = jnp.concatenate([dyb, jnp.where(i == pl.num_programs(0) - 1, 0.0, dyn_ref[...])], axis=0)
        for g in range(B_GROUPS):
            cols = slice(g * Wg, (g + 1) * Wg)
            pb = _pool_fwd(X, g, Wg, i, tr).astype(_MM)
            ypre = _dot(pb, wp_ref[g])
            dsc_ref[:, cols] += jnp.sum(dyb[:, cols] * ypre, axis=0, keepdims=True)
            dyp = (dyb_full[:, cols] * sc_ref[:, cols]).astype(_MM)
            dwp_ref[g] += _dot(pb, dyp[0:tr], _TN)
            dp = _dot(dyp, wp_ref[g], _NT)
            s = dp * _pool_inv_count(i, tr, n, 2 ** (g + 1))
            for k in range(g + 1):
                s = s + pltpu.roll(s, n - 2 ** k, 0)
            dh_ref[2, :, cols] = (s[0:tr] - dp[0:tr]).astype(dh_ref.dtype)

    vec = pl.BlockSpec((1, W), lambda i: (0, 0))
    ws_spec = pl.BlockSpec((H, A_CHUNK, A_CHUNK), lambda i: (0, 0, 0))
    bias_spec = pl.BlockSpec((A_CHUNK, W), lambda i: (0, 0))
    wp_spec = pl.BlockSpec((B_GROUPS, Wg, Wg), lambda i: (0, 0, 0))
    return pl.pallas_call(
        body, name=name, grid=(T // tr,),
        in_specs=[pl.BlockSpec((3, tr, W), lambda i: (0, i, 0)),
                  pl.BlockSpec((None, POOL_HALO, W), lambda i: (2, jnp.maximum(i * nb - 1, 0), 0)),
                  pl.BlockSpec((tr, 2 * W), lambda i: (i, 0)),
                  pl.BlockSpec((POOL_HALO, W), lambda i: (jnp.minimum((i + 1) * nb, last_blk), 1)),
                  vec, vec, ws_spec, bias_spec, wp_spec, vec],
        out_specs=[pl.BlockSpec((3, tr, W), lambda i: (0, i, 0)), ws_spec, bias_spec, vec, vec, wp_spec, vec],
        out_shape=[_S((3, T, W), _MM), _S((H, A_CHUNK, A_CHUNK), _F32), _S((A_CHUNK, W), _F32), _S((1, W), _F32),
                   _S((1, W), _F32), _S((B_GROUPS, Wg, Wg), _F32), _S((1, W), _F32)],
        compiler_params=_params(("arbitrary",)),
    )(h, h, dy, dy, lng, lnb, ws, bias, wp, sc)


def _chunk_cumsum(x, rin):
    s = 1
    while s < C_CHUNK:
        x = x + jnp.where(rin >= s, pltpu.roll(x, s, 0), 0.0)
        s *= 2
    return x


def _chunk_revcumsum(x, rin):
    n = x.shape[0]
    s = 1
    while s < C_CHUNK:
        x = x + jnp.where(rin + s < C_CHUNK, pltpu.roll(x, n - s, 0), 0.0)
        s *= 2
    return x


def _hgrn_gates(q, fl, lb, tr, tc):
    nch = tr // C_CHUNK
    sq = _sigmoid(q)
    sf = _sigmoid_rel(fl)
    f = lb + (1.0 - lb) * sf
    logf = jnp.log(f)
    rin = _row_index(tr) % C_CHUNK
    b = _chunk_cumsum(logf, rin)
    tot3 = jnp.sum(logf.reshape(nch, C_CHUNK, tc), axis=1, keepdims=True)
    eb = jnp.exp(b)
    enb = jnp.exp(-b)
    ekb = jnp.exp(tot3 - b.reshape(nch, C_CHUNK, tc)).reshape(tr, tc)
    return sq, sf, f, rin, tot3, eb, enb, ekb


def _hgrn_prep_fwd(h, lb, name):
    _, T, D = h.shape
    tr = _tile(T, 512)
    tc = _tile(D, 512)
    nch = tr // C_CHUNK

    def body(q_ref, f_ref, v_ref, lb_ref, qd_ref, kd_ref, ke_ref, vb_ref, dec_ref):
        q = q_ref[...]
        sq, _, f, _, tot3, eb, enb, ekb = _hgrn_gates(q, f_ref[...], lb_ref[...], tr, tc)
        kk = 1.0 - f
        qd_ref[...] = (q * sq * eb).astype(qd_ref.dtype)
        kd_ref[...] = (kk * enb).astype(kd_ref.dtype)
        ke_ref[...] = (kk * ekb).astype(ke_ref.dtype)
        vb_ref[...] = v_ref[...].astype(vb_ref.dtype)
        dec_ref[...] = jnp.exp(tot3).reshape(nch, tc)

    def part(p):
        return pl.BlockSpec((None, tr, tc), lambda i, j: (p, i, j))

    blk = pl.BlockSpec((tr, tc), lambda i, j: (i, j))
    return pl.pallas_call(
        body, name=name, grid=(T // tr, D // tc),
        in_specs=[part(0), part(1), part(2), pl.BlockSpec((1, tc), lambda i, j: (0, j))],
        out_specs=[blk, blk, blk, blk, pl.BlockSpec((nch, tc), lambda i, j: (i, j))],
        out_shape=[_S((T, D), _MM)] * 4 + [_S((T // C_CHUNK, D), _F32)],
        compiler_params=_params(("parallel", "parallel")),
    )(h, h, h, lb)


def _tril_mask():
    rr = lax.broadcasted_iota(jnp.int32, (C_CHUNK, C_CHUNK), 0)
    cc = lax.broadcasted_iota(jnp.int32, (C_CHUNK, C_CHUNK), 1)
    return rr >= cc


def _hgrn_scan_fwd(qd, kd, ke, vb, dec, h, ng, name):
    T, D = qd.shape
    NH = D // C_HEAD
    N = T // C_CHUNK

    def body(qd_ref, kd_ref, ke_ref, vb_ref, dec_ref, g_ref, ng_ref, o_ref, y_ref, st_ref):
        mask = _tril_mask()

        def step(n, St):
            r = pl.ds(pl.multiple_of(n * C_CHUNK, C_CHUNK), C_CHUNK)
            Qd, Kd, Ke, V = qd_ref[r, :], kd_ref[r, :], ke_ref[r, :], vb_ref[r, :]
            att = jnp.where(mask, _dot(Qd, Kd, _NT), 0.0).astype(_MM)
            o_ref[r, :] = _dot(att, V) + _dot(Qd, St.astype(_MM), _NT)
            st_ref[n] = St
            return St * dec_ref[pl.ds(n, 1), :] + _dot(V, Ke, _TN)

        def trip(i, state):
            for u in range(SCAN_UNROLL):
                state = step(i * SCAN_UNROLL + u, state)
            return state

        lax.fori_loop(0, N // SCAN_UNROLL, trip, jnp.zeros((C_HEAD, C_HEAD), _F32))
        o = o_ref[...]
        r = lax.rsqrt(jnp.mean(o * o, axis=-1, keepdims=True) + LN_EPS)
        y_ref[...] = (o * r * ng_ref[...] * _sigmoid(g_ref[...])).astype(y_ref.dtype)

    col = pl.BlockSpec((T, C_HEAD), lambda j: (0, j))
    return pl.pallas_call(
        body, name=name, grid=(NH,),
        in_specs=[col, col, col, col, pl.BlockSpec((N, C_HEAD), lambda j: (0, j)),
                  pl.BlockSpec((None, T, C_HEAD), lambda j: (3, 0, j)), pl.BlockSpec((1, C_HEAD), lambda j: (0, j))],
        out_specs=[col, col, pl.BlockSpec((None, N, C_HEAD, C_HEAD), lambda j: (j, 0, 0, 0))],
        out_shape=[_S((T, D), _F32), _S((T, D), _MM), _S((NH, N, C_HEAD, C_HEAD), _F32)],
        compiler_params=_params(("parallel",)),
    )(qd, kd, ke, vb, dec, h, ng)


def _hgrn_scan_bwd(qd, kd, ke, vb, dec, st, o, h, ng, dy, name):
    T, D = qd.shape
    NH = D // C_HEAD
    N = T // C_CHUNK

    def body(qd_ref, kd_ref, ke_ref, vb_ref, dec_ref, st_ref, o_ref, g_ref, ng_ref, dy_ref,
             dqd_ref, dkd_ref, dke_ref, dv_ref, dgate_ref, ddec_ref, dng_ref, do_s):
        o = o_ref[...]
        r = lax.rsqrt(jnp.mean(o * o, axis=-1, keepdims=True) + LN_EPS)
        oh = o * r
        gn = ng_ref[...]
        sg = _sigmoid(g_ref[...])
        d = dy_ref[...]
        dyn = d * sg
        dgate_ref[...] = (d * oh * gn * sg * (1.0 - sg)).astype(dgate_ref.dtype)
        dng_ref[...] = jnp.sum(dyn * oh, axis=0, keepdims=True)
        doh = dyn * gn
        do_s[...] = (r * (doh - oh * jnp.mean(doh * oh, axis=-1, keepdims=True))).astype(do_s.dtype)
        mask = _tril_mask()

        def step(k, dSt):
            n = N - 1 - k
            rws = pl.ds(pl.multiple_of(n * C_CHUNK, C_CHUNK), C_CHUNK)
            Qd, Kd, Ke, V, dO = qd_ref[rws, :], kd_ref[rws, :], ke_ref[rws, :], vb_ref[rws, :], do_s[rws, :]
            St = st_ref[n]
            Stb = St.astype(_MM)
            dStb = dSt.astype(_MM)
            att = jnp.where(mask, _dot(Qd, Kd, _NT), 0.0).astype(_MM)
            dA = jnp.where(mask, _dot(dO, V, _NT), 0.0).astype(_MM)
            dv_ref[rws, :] = (_dot(att, dO, _TN) + _dot(Ke, dStb, _NT)).astype(dv_ref.dtype)
            dqd_ref[rws, :] = _dot(dA, Kd) + _dot(dO, Stb)
            dkd_ref[rws, :] = _dot(dA, Qd, _TN)
            dke_ref[rws, :] = _dot(V, dStb)
            ddec_ref[pl.ds(n, 1), :] = jnp.sum(dSt * St, axis=0, keepdims=True)
            return dSt * dec_ref[pl.ds(n, 1), :] + _dot(dO, Qd, _TN)

        def trip(i, state):
            for u in range(SCAN_UNROLL):
                state = step(i * SCAN_UNROLL + u, state)
            return state

        lax.fori_loop(0, N // SCAN_UNROLL, trip, jnp.zeros((C_HEAD, C_HEAD), _F32))

    col = pl.BlockSpec((T, C_HEAD), lambda j: (0, j))
    chk = pl.BlockSpec((N, C_HEAD), lambda j: (0, j))
    one = pl.BlockSpec((1, C_HEAD), lambda j: (0, j))
    return pl.pallas_call(
        body, name=name, grid=(NH,),
        in_specs=[col, col, col, col, chk, pl.BlockSpec((None, N, C_HEAD, C_HEAD), lambda j: (j, 0, 0, 0)), col,
                  pl.BlockSpec((None, T, C_HEAD), lambda j: (3, 0, j)), one, col],
        out_specs=[col, col, col, col, col, chk, one],
        out_shape=[_S((T, D), _F32)] * 3 + [_S((T, D), _MM)] * 2 + [_S((N, D), _F32), _S((1, D), _F32)],
        scratch_shapes=[pltpu.VMEM((T, C_HEAD), _MM)],
        compiler_params=_params(("parallel",)),
    )(qd, kd, ke, vb, dec, st, o, h, ng, dy)


def _hgrn_prep_bwd(h, lb, dqd, dkd, dke, dv, dgate, ddec, name):
    _, T, D = h.shape
    tr = _tile(T, 512)
    tc = _tile(D, 256)
    nch = tr // C_CHUNK

    def body(q_ref, f_ref, lb_ref, dqd_ref, dkd_ref, dke_ref, dv_ref, dgate_ref, ddec_ref, dh_ref, dlb_ref):
        @pl.when(pl.program_id(1) == 0)
        def _():
            dlb_ref[...] = jnp.zeros_like(dlb_ref)

        q = q_ref[...]
        lb = lb_ref[...]
        sq, sf, f, rin, tot3, eb, enb, ekb = _hgrn_gates(q, f_ref[...], lb, tr, tc)
        kk = 1.0 - f
        dQd, dKd, dKe = dqd_ref[...], dkd_ref[...], dke_ref[...]
        tq = dQd * eb
        tkd = dKd * enb
        tke = dKe * ekb
        ke_term = tke * kk
        db = tq * (q * sq) - tkd * kk - ke_term
        dtot3 = (jnp.sum(ke_term.reshape(nch, C_CHUNK, tc), axis=1, keepdims=True)
                 + (ddec_ref[...] * jnp.exp(tot3).reshape(nch, tc)).reshape(nch, 1, tc))
        dlogf = (_chunk_revcumsum(db, rin).reshape(nch, C_CHUNK, tc) + dtot3).reshape(tr, tc)
        df = dlogf / f - (tkd + tke)
        dh_ref[0] = (tq * sq * (1.0 + q * (1.0 - sq))).astype(dh_ref.dtype)
        dh_ref[1] = (df * (1.0 - lb) * sf * (1.0 - sf)).astype(dh_ref.dtype)
        dh_ref[2] = dv_ref[...]
        dh_ref[3] = dgate_ref[...]
        dlb_ref[...] += jnp.sum(df * (1.0 - sf), axis=0, keepdims=True)

    def part(p):
        return pl.BlockSpec((None, tr, tc), lambda j, i: (p, i, j))

    blk = pl.BlockSpec((tr, tc), lambda j, i: (i, j))
    vec = pl.BlockSpec((1, tc), lambda j, i: (0, j))
    return pl.pallas_call(
        body, name=name, grid=(D // tc, T // tr),
        in_specs=[part(0), part(1), vec, blk, blk, blk, blk, blk, pl.BlockSpec((nch, tc), lambda j, i: (i, j))],
        out_specs=[pl.BlockSpec((4, tr, tc), lambda j, i: (0, i, j)), vec],
        out_shape=[_S((4, T, D), _MM), _S((1, D), _F32)],
        compiler_params=_params(("parallel", "arbitrary")),
    )(h, h, lb, dqd, dkd, dke, dv, dgate, ddec)


def _sum_in_device_order(me1, own, land, name):
    R, C = own.shape
    tr = _tile(R, 256)

    def body(me_ref, own_ref, land_ref, o_ref):
        me = me_ref[0]
        g = None
        for j in range(N_DEV):
            slot = jnp.maximum(jnp.bitwise_xor(me, j) - 1, 0)
            p = jnp.where(me == j, own_ref[...], land_ref[slot])
            g = p if g is None else g + p
        o_ref[...] = g

    return pl.pallas_call(
        body, name=name,
        grid_spec=pltpu.PrefetchScalarGridSpec(
            num_scalar_prefetch=1, grid=(R // tr,),
            in_specs=[pl.BlockSpec((tr, C), lambda i, me: (i, 0)), pl.BlockSpec((N_DEV - 1, tr, C), lambda i, me: (0, i, 0))],
            out_specs=pl.BlockSpec((tr, C), lambda i, me: (i, 0))),
        out_shape=_S((R, C), _F32), compiler_params=_params(("parallel",)),
    )(me1, own, land)


def _adamw(parts, w, m, v, name):
    P, R, C = parts.shape
    tr = _tile(R, 128) if R % LANE == 0 else R

    def body(p_ref, w_ref, m_ref, v_ref, g_ref, d_ref, nm_ref, nv_ref):
        g = p_ref[0].astype(_F32)
        for s in range(1, P):
            g = g + p_ref[s].astype(_F32)
        nm = ADAM_B1 * m_ref[...] + (1.0 - ADAM_B1) * g
        nv = ADAM_B2 * v_ref[...] + (1.0 - ADAM_B2) * (g * g)
        m_hat = nm / (1.0 - ADAM_B1 ** ADAM_STEP)
        v_hat = nv / (1.0 - ADAM_B2 ** ADAM_STEP)
        g_ref[...] = g
        d_ref[...] = -ADAM_LR * (m_hat / (jnp.sqrt(v_hat) + ADAM_EPS) + ADAM_WD * w_ref[...])
        nm_ref[...] = nm
        nv_ref[...] = nv

    blk = pl.BlockSpec((tr, C), lambda i: (i, 0))
    return pl.pallas_call(
        body, name=name, grid=(R // tr,), in_specs=[pl.BlockSpec((P, tr, C), lambda i: (0, i, 0)), blk, blk, blk],
        out_specs=[blk] * 4, out_shape=[_S((R, C), _F32)] * 4, compiler_params=_params(("parallel",)),
    )(parts, w, m, v)


def _exchange(name, srcs, out_shapes, jobs, deps=()):
    ns, nj = len(srcs), len(jobs)

    nd = len(deps)

    def body(*refs):
        ins, outs = refs[:ns], refs[ns + nd:ns + nd + len(out_shapes)]
        send_sems, recv_sems, local_sems = refs[-3:]
        x, y, c = lax.axis_index("x"), lax.axis_index("y"), lax.axis_index("c")
        me = 4 * x + 2 * y + c
        local = []
        for ji, (si, src_fn, di, dst_fn) in enumerate(jobs):
            cp = pltpu.make_async_copy(src_fn(ins[si], me, me), dst_fn(outs[di], me), local_sems.at[ji])
            cp.start()
            local.append(cp)
        remote = []
        for k in range(1, N_DEV):
            px, py, pc = (x + (k >> 2)) % 2, (y + ((k >> 1) & 1)) % 2, (c + (k & 1)) % 2
            to = 4 * px + 2 * py + pc
            for ji, (si, src_fn, di, dst_fn) in enumerate(jobs):
                sem = (k - 1) * nj + ji
                cp = pltpu.make_async_remote_copy(
                    src_ref=src_fn(ins[si], me, to), dst_ref=dst_fn(outs[di], me),
                    send_sem=send_sems.at[sem], recv_sem=recv_sems.at[sem],
                    device_id=(px, py, pc), device_id_type=pl.DeviceIdType.MESH)
                cp.start()
                remote.append(cp)
        for cp in remote:
            cp.wait_recv()
        for cp in remote:
            cp.wait_send()
        for cp in local:
            cp.wait()

    hbm = pl.BlockSpec(memory_space=pltpu.HBM)
    return pl.pallas_call(
        body, name=name, in_specs=[hbm] * ns + [_ANY] * nd, out_specs=[hbm] * len(out_shapes), out_shape=list(out_shapes),
        scratch_shapes=[pltpu.SemaphoreType.DMA(((N_DEV - 1) * nj,)), pltpu.SemaphoreType.DMA(((N_DEV - 1) * nj,)),
                        pltpu.SemaphoreType.DMA((nj,))],
    )(*srcs, *deps)


def _whole(ref, me, to):
    return ref


def _slot_job(i, o):
    def dst(ref, me):
        return ref.at[me]
    return (i, _whole, o, dst)


_HBM = pl.BlockSpec(memory_space=pltpu.HBM)
_SEM = pl.BlockSpec(memory_space=pltpu.SEMAPHORE)
_ANY = pl.BlockSpec(memory_space=pl.ANY)
_N_PEER = N_DEV - 1


def _split_params():
    return pltpu.CompilerParams(has_side_effects=pltpu.SideEffectType.DATAFLOW_SIDE_EFFECTING)


def _blk(ref, axis, n, idx):
    if axis is None:
        return ref
    return ref.at[tuple([slice(None)] * axis + [pl.ds(pl.multiple_of(idx * n, n), n)])]


def _peer(k):
    x, y, c = lax.axis_index("x"), lax.axis_index("y"), lax.axis_index("c")
    px, py, pc = (x + (k >> 2)) % 2, (y + ((k >> 1) & 1)) % 2, (c + (k & 1)) % 2
    return (px, py, pc), 4 * px + 2 * py + pc, 4 * x + 2 * y + c


def _row_tile(rows, pref):
    best = None
    for d in range(16, min(rows, pref) + 1, 16):
        if rows % d == 0:
            best = d
    return best if best is not None else rows


def _place(w, me1, axis, name, layer=None, deps=()):
    R, C = w.shape[-2:]
    tr = _row_tile(R, 512)
    nb = R // tr
    lead = () if layer is None else (None,)
    pre = () if layer is None else (layer,)

    def body(me_ref, w_ref, *rest):
        rest[-1][...] = w_ref[...].astype(rest[-1].dtype)

    if axis == 1:
        out_spec = pl.BlockSpec((tr, C), lambda i, me: (i, me[0]))
        out_shape = _S((R, N_DEV * C), _MM)
    else:
        out_spec = pl.BlockSpec((tr, C), lambda i, me: (me[0] * nb + i, 0))
        out_shape = _S((N_DEV * R, C), _MM)
    return pl.pallas_call(
        body, name=name,
        grid_spec=pltpu.PrefetchScalarGridSpec(
            num_scalar_prefetch=1, grid=(nb,),
            in_specs=[pl.BlockSpec(lead + (tr, C), lambda i, me: pre + (i, 0))] + [_ANY] * len(deps), out_specs=out_spec),
        out_shape=out_shape, compiler_params=_params(("parallel",)),
    )(me1, w, *deps)


_SIBLING = 1
_CHIPS = (2, 4, 6)
_VMEM_TOKEN = pl.BlockSpec(memory_space=pltpu.VMEM)


def _remote(ref_blk, send_sem, recv_sem, dev):
    return pltpu.make_async_remote_copy(src_ref=ref_blk, dst_ref=ref_blk, send_sem=send_sem, recv_sem=recv_sem,
                                        device_id=dev, device_id_type=pl.DeviceIdType.MESH)


def _gather_start(name, full, axis, n):
    def body(f_ref, send, recv, f_out, token):
        for i, k in enumerate((_SIBLING,) + _CHIPS):
            dev, _, me = _peer(k)
            _remote(_blk(f_ref, axis, n, me), send.at[i], recv.at[i], dev).start()
        token[...] = jnp.zeros_like(token)

    return pl.pallas_call(
        body, name=name,
        out_shape=(pltpu.SemaphoreType.DMA((4,)), pltpu.SemaphoreType.DMA((4,)), pltpu.HBM(full.shape, full.dtype),
                   _S((8, LANE), _F32)),
        in_specs=(_HBM,), out_specs=(_SEM, _SEM, _HBM, _VMEM_TOKEN),
        input_output_aliases={0: 2}, compiler_params=_split_params(),
    )(pltpu.with_memory_space_constraint(full, pltpu.HBM))


def _gather_forward(name, full, axis, n, recv, after):
    after = tuple(after) if isinstance(after, (tuple, list)) else (after,)

    def body(f_ref, recv_r, *rest):
        send2, recv2, f_out, token = rest[-4:]
        sib, _, _ = _peer(_SIBLING)
        for i, k in enumerate(_CHIPS):
            dev, frm, _ = _peer(k)
            blk = _blk(f_ref, axis, n, frm)
            _remote(blk, send2.at[i], recv_r.at[1 + i], dev).wait_recv()
            _remote(blk, send2.at[i], recv2.at[i], sib).start()
        token[...] = jnp.zeros_like(token)

    return pl.pallas_call(
        body, name=name,
        out_shape=(pltpu.SemaphoreType.DMA((3,)), pltpu.SemaphoreType.DMA((3,)), pltpu.HBM(full.shape, full.dtype),
                   _S((8, LANE), _F32)),
        in_specs=(_HBM, _SEM) + (_ANY,) * len(after), out_specs=(_SEM, _SEM, _HBM, _VMEM_TOKEN),
        input_output_aliases={0: 2}, compiler_params=_split_params(),
    )(full, recv, *after)


def _gather_wait(name, full, axis, n, send, recv, send2, recv2, after):
    def body(f_ref, send_r, recv_r, send2_r, recv2_r, after_ref, f_out):
        sib, _, me = _peer(_SIBLING)
        blk = _blk(f_ref, axis, n, me)
        for i in range(4):
            _remote(blk, send_r.at[i], recv_r.at[0], sib).wait_send()
        _remote(blk, send_r.at[0], recv_r.at[0], sib).wait_recv()
        for i in range(3):
            cp = _remote(blk, send2_r.at[i], recv2_r.at[i], sib)
            cp.wait_send()
            cp.wait_recv()

    return pl.pallas_call(
        body, name=name, out_shape=pltpu.HBM(full.shape, full.dtype),
        in_specs=(_HBM, _SEM, _SEM, _SEM, _SEM, _ANY), out_specs=_HBM,
        input_output_aliases={0: 0}, compiler_params=_split_params(),
    )(full, send, recv, send2, recv2, after)


def _scatter_start(name, dw, axis, n):
    shard = tuple(n if a == axis else d for a, d in enumerate(dw.shape))
    land = lax.empty((_N_PEER,) + shard, dw.dtype)

    def body(dw_ref, land_ref, send, recv, dw_out, land_out, token):
        for k in range(1, N_DEV):
            dev, to, _ = _peer(k)
            pltpu.make_async_remote_copy(
                src_ref=_blk(dw_ref, axis, n, to), dst_ref=land_ref.at[k - 1], send_sem=send.at[k - 1],
                recv_sem=recv.at[k - 1], device_id=dev, device_id_type=pl.DeviceIdType.MESH).start()
        token[...] = jnp.zeros_like(token)

    return pl.pallas_call(
        body, name=name,
        out_shape=(pltpu.SemaphoreType.DMA((_N_PEER,)), pltpu.SemaphoreType.DMA((_N_PEER,)),
                   pltpu.HBM(dw.shape, dw.dtype), pltpu.HBM(land.shape, land.dtype), _S((8, LANE), _F32)),
        in_specs=(_HBM, _HBM), out_specs=(_SEM, _SEM, _HBM, _HBM, pl.BlockSpec(memory_space=pltpu.VMEM)),
        input_output_aliases={0: 2, 1: 3}, compiler_params=_split_params(),
    )(pltpu.with_memory_space_constraint(dw, pltpu.HBM), pltpu.with_memory_space_constraint(land, pltpu.HBM))


def _scatter_wait(name, items, after):
    ne = len(items)
    after = tuple(after) if isinstance(after, (tuple, list)) else (after,)

    def body(*refs):
        for e, (_, _, _, _, axis, n) in enumerate(items):
            dw_ref, land_ref, send_r, recv_r = refs[4 * e:4 * e + 4]
            for k in range(1, N_DEV):
                dev, to, _ = _peer(k)
                cp = pltpu.make_async_remote_copy(
                    src_ref=_blk(dw_ref, axis, n, to), dst_ref=land_ref.at[k - 1], send_sem=send_r.at[k - 1],
                    recv_sem=recv_r.at[k - 1], device_id=dev, device_id_type=pl.DeviceIdType.MESH)
                cp.wait_send()
                cp.wait_recv()

    args, out_shape = [], []
    for dw, land, send, recv, _, _ in items:
        args += [dw, land, send, recv]
        out_shape += [pltpu.HBM(dw.shape, dw.dtype), pltpu.HBM(land.shape, land.dtype)]
    res = pl.pallas_call(
        body, name=name, out_shape=tuple(out_shape),
        in_specs=(_HBM, _HBM, _SEM, _SEM) * ne + (_ANY,) * len(after), out_specs=(_HBM,) * (2 * ne),
        input_output_aliases={4 * e + j: 2 * e + j for e in range(ne) for j in range(2)},
        compiler_params=_split_params(),
    )(*args, *after)
    return [(res[2 * e], res[2 * e + 1]) for e in range(ne)]


def _adamw_big(me1, dw, land, w, m, v, axis, n, name, layer=None, into=None):
    R, C = land.shape[1:]
    tr = _row_tile(R, 128)
    nb = R // tr
    lead = () if layer is None else (None,)
    pre = () if layer is None else (layer,)

    def body(me_ref, own_ref, land_ref, w_ref, m_ref, v_ref, *rest):
        g_ref, d_ref, nm_ref, nv_ref = rest[-4:]
        g = own_ref[...].astype(_F32)
        for s in range(_N_PEER):
            g = g + land_ref[s].astype(_F32)
        nm = ADAM_B1 * m_ref[...] + (1.0 - ADAM_B1) * g
        nv = ADAM_B2 * v_ref[...] + (1.0 - ADAM_B2) * (g * g)
        m_hat = nm / (1.0 - ADAM_B1 ** ADAM_STEP)
        v_hat = nv / (1.0 - ADAM_B2 ** ADAM_STEP)
        g_ref[...] = g
        d_ref[...] = -ADAM_LR * (m_hat / (jnp.sqrt(v_hat) + ADAM_EPS) + ADAM_WD * w_ref[...])
        nm_ref[...] = nm
        nv_ref[...] = nv

    if axis == 1:
        own_spec = pl.BlockSpec((tr, C), lambda i, me: (i, me[0]))
    else:
        own_spec = pl.BlockSpec((tr, C), lambda i, me: (me[0] * nb + i, 0))
    blk = pl.BlockSpec(lead + (tr, C), lambda i, me: pre + (i, 0))
    in_specs = [own_spec, pl.BlockSpec((_N_PEER, tr, C), lambda i, me: (0, i, 0)), blk, blk, blk]
    args = [me1, dw, land, w, m, v]
    aliases = {}
    if into is not None:
        in_specs += [_ANY] * 4
        aliases = {6 + j: j for j in range(4)}
        args += list(into)
    return pl.pallas_call(
        body, name=name,
        grid_spec=pltpu.PrefetchScalarGridSpec(num_scalar_prefetch=1, grid=(nb,), in_specs=in_specs, out_specs=[blk] * 4),
        out_shape=[_S(w.shape, _F32)] * 4, input_output_aliases=aliases, compiler_params=_params(("parallel",)),
    )(*args)


def _pack(arrs):
    flat = jnp.concatenate([a.reshape(-1).astype(_F32) for a in arrs])
    pad = (-flat.shape[0]) % (LANE * LANE)
    return jnp.pad(flat, (0, pad)).reshape(-1, LANE)


def _unpack(mat, shapes):
    flat = mat.reshape(-1)
    out, off = [], 0
    for s in shapes:
        n = 1
        for d in s:
            n *= d
        out.append(flat[off:off + n].reshape(s))
        off += n
    return out


def _lb_of(lb_param):
    lb_all = jnp.cumsum(jax.nn.softmax(lb_param.astype(_F32), axis=0), axis=0)
    return (lb_all - lb_all[0])[1:2]


def kernel(x, ev_w_in, ev_ln_v_g, ev_ln_v_b, ev_w_s, ev_b_s, ev_w_pool, ev_pool_scale, ev_w_out, od_w_in, od_norm_g, od_w_out, lb_param, ffn_w_up, ffn_conv_w, ffn_conv_b, ffn_w_down, ln1_g, ln1_b, ln2_g, ln2_b, loss_target, m_ev_w_in, m_ev_ln_v_g, m_ev_ln_v_b, m_ev_w_s, m_ev_b_s, m_ev_w_pool, m_ev_pool_scale, m_ev_w_out, m_od_w_in, m_od_norm_g, m_od_w_out, m_lb_param, m_ffn_w_up, m_ffn_conv_w, m_ffn_conv_b, m_ffn_w_down, m_ln1_g, m_ln1_b, m_ln2_g, m_ln2_b, v_ev_w_in, v_ev_ln_v_g, v_ev_ln_v_b, v_ev_w_s, v_ev_b_s, v_ev_w_pool, v_ev_pool_scale, v_ev_w_out, v_od_w_in, v_od_norm_g, v_od_w_out, v_lb_param, v_ffn_w_up, v_ffn_conv_w, v_ffn_conv_b, v_ffn_w_down, v_ln1_g, v_ln1_b, v_ln2_g, v_ln2_b):
    me = 4 * lax.axis_index("x") + 2 * lax.axis_index("y") + lax.axis_index("c")
    T, D = x.shape[1], x.shape[2]
    W = ev_ln_v_g.shape[1]
    H = W // A_HEAD
    Wg = W // B_GROUPS
    F2 = ffn_conv_b.shape[1]
    F = F2 // 2
    n_in0, n_out0 = ev_w_in.shape[2], ev_w_out.shape[1]
    n_in1, n_out1 = od_w_in.shape[2], od_w_out.shape[1]
    n_up, n_dn = ffn_w_up.shape[2], ffn_w_down.shape[1]
    n_pool, n_ng, n_cw = ev_w_pool.shape[2], od_norm_g.shape[1], ffn_conv_w.shape[2]

    small_shards = [od_norm_g, ffn_conv_w, ev_w_pool]
    small_pack = _pack(small_shards)
    small_all = _exchange("gather_small_params", [small_pack], [_S((N_DEV,) + small_pack.shape, _F32)], [_slot_job(0, 0)])[0]

    me1 = me.astype(jnp.int32).reshape(1)
    weights = [
        ("w_in0", ev_w_in[0], None, 1, n_in0), ("w_out0", ev_w_out[0], None, 0, n_out0),
        ("w_up0", ffn_w_up, 0, 1, n_up), ("w_dn0", ffn_w_down, 0, 0, n_dn),
        ("w_in1", od_w_in[0], None, 1, n_in1), ("w_out1", od_w_out[0], None, 0, n_out1),
        ("w_up1", ffn_w_up, 1, 1, n_up), ("w_dn1", ffn_w_down, 1, 0, n_dn),
    ]
    started, tokens = {}, [small_all]
    for key, w, layer, axis, n in weights:
        full = _place(w, me1, axis, "place_" + key, layer, deps=tokens)
        send, recv, full, token = _gather_start("gather_start_" + key, full, axis, n)
        started[key] = (full, axis, n, send, recv)
        tokens = [token]

    def pass_on(key, after):
        full, axis, n, send, recv = started[key]
        send2, recv2, full, token = _gather_forward("gather_forward_" + key, full, axis, n, recv, after)
        started[key] = (full, axis, n, send, recv, send2, recv2)
        return token

    def gathered(key, after):
        return _gather_wait("gather_wait_" + key, *started[key], after)

    ng_parts, cw_parts, wp_parts = [], [], []
    for j in range(N_DEV):
        a, b, c = _unpack(small_all[j], [s.shape for s in small_shards])
        ng_parts.append(a)
        cw_parts.append(b)
        wp_parts.append(c)
    norm_g = jnp.concatenate(ng_parts, axis=1)
    conv_w = jnp.concatenate(cw_parts, axis=2)
    w_pool = jnp.concatenate(wp_parts, axis=2)[0]
    cw_l = [conv_w[l].reshape(3, 2, F).transpose(1, 0, 2) for l in range(DEPTH)]
    cb_l = [ffn_conv_b[l].reshape(2, 1, F) for l in range(DEPTH)]
    ws_tril = jnp.tril(ev_w_s[0]).astype(_MM)
    bias = jnp.repeat(ev_b_s[0].T, A_HEAD, axis=1)
    wp_b = w_pool.astype(_MM)
    lb, lb_vjp = jax.vjp(_lb_of, lb_param)

    small_names = ["ev_ln_v_g", "ev_ln_v_b", "ev_w_s", "ev_b_s", "ev_w_pool", "ev_pool_scale", "od_norm_g", "lb_param",
                   "ffn_conv_w", "ffn_conv_b", "ln1_g", "ln1_b", "ln2_g", "ln2_b"]
    given = dict(ev_ln_v_g=(ev_ln_v_g, m_ev_ln_v_g, v_ev_ln_v_g), ev_ln_v_b=(ev_ln_v_b, m_ev_ln_v_b, v_ev_ln_v_b),
                 ev_w_s=(ev_w_s, m_ev_w_s, v_ev_w_s), ev_b_s=(ev_b_s, m_ev_b_s, v_ev_b_s),
                 ev_w_pool=(ev_w_pool, m_ev_w_pool, v_ev_w_pool),
                 ev_pool_scale=(ev_pool_scale, m_ev_pool_scale, v_ev_pool_scale),
                 od_norm_g=(od_norm_g, m_od_norm_g, v_od_norm_g), lb_param=(lb_param, m_lb_param, v_lb_param),
                 ffn_conv_w=(ffn_conv_w, m_ffn_conv_w, v_ffn_conv_w), ffn_conv_b=(ffn_conv_b, m_ffn_conv_b, v_ffn_conv_b),
                 ln1_g=(ln1_g, m_ln1_g, v_ln1_g), ln1_b=(ln1_b, m_ln1_b, v_ln1_b), ln2_g=(ln2_g, m_ln2_g, v_ln2_g),
                 ln2_b=(ln2_b, m_ln2_b, v_ln2_b))
    shard_axis = dict(ev_w_pool=2, od_norm_g=1, ffn_conv_w=2)
    rep_names = [n for n in small_names if n not in shard_axis]
    shd_names = [n for n in small_names if n in shard_axis]
    small_packs = [_pack([given[n][j] for n in small_names]) for j in range(3)]

    x2 = x[0]
    xb = _cast(x2, _MM, "cast_x", deps=[pass_on("w_in0", tokens[0])])
    w_in0 = gathered("w_in0", xb)
    h0 = _mm(xb, w_in0, "nn", _F32, "ev_in", out_parts=3)
    tie = pass_on("w_out0", h0)
    yab = _ev_mid_fwd(h0, ev_ln_v_g + tie[0, 0], ev_ln_v_b, ws_tril, bias, wp_b, ev_pool_scale, "ev_mid_fwd")
    w_out0 = gathered("w_out0", yab)
    z1 = _mm(yab, w_out0, "nn", _F32, "ev_out", add=x2, add_scale=ALPHA)
    tie = pass_on("w_up0", (z1, *small_packs))
    x1, x1b = _ln_fwd(z1, ln1_g[0:1] + tie[0, 0], ln1_b[0:1], "ln1_0")
    w_up0 = gathered("w_up0", x1b)
    hf0 = _mm(x1b, w_up0, "nn", _F32, "ffn_up", out_parts=2)
    tie = pass_on("w_dn0", hf0)
    act0, hc0 = _ffn_mid_fwd(hf0, cw_l[0], cb_l[0] + tie[0, 0], "ffn_mid_fwd")
    w_dn0 = gathered("w_dn0", act0)
    z2 = _mm(act0, w_dn0, "nn", _F32, "ffn_down", add=x1, add_scale=ALPHA)
    tie = pass_on("w_in1", z2)
    x2_, x2b = _ln_fwd(z2, ln2_g[0:1] + tie[0, 0], ln2_b[0:1], "ln2_0")
    w_in1 = gathered("w_in1", x2b)
    h1 = _mm(x2b, w_in1, "nn", _F32, "od_in", out_parts=4)
    qd, kd, ke, vb, dec = _hgrn_prep_fwd(h1, lb, "hgrn_prep_fwd")
    tie = pass_on("w_out1", qd)
    o, yo, st = _hgrn_scan_fwd(qd, kd, ke, vb, dec, h1, norm_g + tie[0, 0], "hgrn_scan_fwd")
    w_out1 = gathered("w_out1", yo)
    z3 = _mm(yo, w_out1, "nn", _F32, "od_out", add=x2_, add_scale=ALPHA)
    tie = pass_on("w_up1", z3)
    x3, x3b = _ln_fwd(z3, ln1_g[1:2] + tie[0, 0], ln1_b[1:2], "ln1_1")
    w_up1 = gathered("w_up1", x3b)
    hf1 = _mm(x3b, w_up1, "nn", _F32, "ffn_up", out_parts=2)
    tie = pass_on("w_dn1", hf1)
    act1, hc1 = _ffn_mid_fwd(hf1, cw_l[1], cb_l[1] + tie[0, 0], "ffn_mid_fwd")
    w_dn1 = gathered("w_dn1", act1)
    z4 = _mm(act1, w_dn1, "nn", _F32, "ffn_down", add=x3, add_scale=ALPHA)

    scat = {}

    def scatter(key, dw, axis, n):
        send, recv, dw, land, token = _scatter_start("scatter_start_" + key, dw, axis, n)
        scat[key] = (dw, land, send, recv, axis, n)
        return [token]

    loss11, dz4, dz4b, g_ln2_1, b_ln2_1 = _ln_loss_bwd(z4, ln2_g[1:2], ln2_b[1:2], loss_target[0], "ln_loss_bwd")
    tok = scatter("dn1", _mm(act1, dz4b, "tn", _XCH, "ffn_down_dw"), 0, n_dn)
    dact1 = _mm(dz4b, w_dn1, "nt", _F32, "ffn_down_dx", deps=tok)
    dhf1, dcw1, dcb1 = _ffn_mid_bwd(hf1, hc1, dact1, cw_l[1], "ffn_mid_bwd")
    tok = scatter("up1", _mm(x3b, dhf1, "tn", _XCH, "ffn_up_dw", b_parts=2, deps=tok), 1, n_up)
    dx3 = _mm(dhf1, w_up1, "nt", _F32, "ffn_up_dx", a_parts=2, add=dz4, add_scale=ALPHA, deps=tok)
    dz3, dz3b, g_ln1_1, b_ln1_1 = _ln_bwd(z3, ln1_g[1:2], dx3, "ln_bwd")
    tok = scatter("out1", _mm(yo, dz3b, "tn", _XCH, "od_out_dw", deps=tok), 0, n_out1)
    dyo = _mm(dz3b, w_out1, "nt", _F32, "od_out_dx", deps=tok)
    dqd, dkd, dke, dv, dgate, ddec, dng = _hgrn_scan_bwd(qd, kd, ke, vb, dec, st, o, h1, norm_g, dyo, "hgrn_scan_bwd")
    dh1, dlb = _hgrn_prep_bwd(h1, lb, dqd, dkd, dke, dv, dgate, ddec, "hgrn_prep_bwd")
    tok = scatter("in1", _mm(x2b, dh1, "tn", _XCH, "od_in_dw", b_parts=4, deps=tok), 1, n_in1)
    dx2 = _mm(dh1, w_in1, "nt", _F32, "od_in_dx", a_parts=4, add=dz3, add_scale=ALPHA, deps=tok)
    dz2, dz2b, g_ln2_0, b_ln2_0 = _ln_bwd(z2, ln2_g[0:1], dx2, "ln_bwd")
    tok = scatter("dn0", _mm(act0, dz2b, "tn", _XCH, "ffn_down_dw", deps=tok), 0, n_dn)
    dact0 = _mm(dz2b, w_dn0, "nt", _F32, "ffn_down_dx", deps=tok)
    dhf0, dcw0, dcb0 = _ffn_mid_bwd(hf0, hc0, dact0, cw_l[0], "ffn_mid_bwd")
    tok = scatter("up0", _mm(x1b, dhf0, "tn", _XCH, "ffn_up_dw", b_parts=2, deps=tok), 1, n_up)
    dx1 = _mm(dhf0, w_up0, "nt", _F32, "ffn_up_dx", a_parts=2, add=dz2, add_scale=ALPHA, deps=tok)
    dz1, dz1b, g_ln1_0, b_ln1_0 = _ln_bwd(z1, ln1_g[0:1], dx1, "ln_bwd")
    tok = scatter("out0", _mm(yab, dz1b, "tn", _XCH, "ev_out_dw", deps=tok), 0, n_out0)
    dyab = _mm(dz1b, w_out0, "nt", _F32, "ev_out_dx", deps=tok)
    dh0, dws, dbias, dlng, dlnb, dwp, dsc = _ev_mid_bwd(h0, dyab, ev_ln_v_g, ev_ln_v_b, ws_tril, bias, wp_b,
                                                        ev_pool_scale, "ev_mid_bwd")

    g_b_s = dbias.reshape(A_CHUNK, H, A_HEAD).sum(axis=-1).T[None]
    g_conv_w = jnp.stack([d.transpose(1, 0, 2).reshape(3, F2) for d in (dcw0, dcw1)])
    g_conv_b = jnp.stack([d.reshape(F2) for d in (dcb0, dcb1)])
    small_grads = dict(zip(small_names, [
        dlng, dlnb, dws[None], g_b_s, dwp[None], dsc, dng, lb_vjp(dlb)[0], g_conv_w, g_conv_b,
        jnp.concatenate([g_ln1_0, g_ln1_1]), jnp.concatenate([b_ln1_0, b_ln1_1]),
        jnp.concatenate([g_ln2_0, g_ln2_1]), jnp.concatenate([b_ln2_0, b_ln2_1])]))

    def by_device(g, ax):
        g = g.reshape(g.shape[:ax] + (N_DEV, g.shape[ax] // N_DEV) + g.shape[ax + 1:])
        return jnp.moveaxis(g, ax, 0).reshape(N_DEV, -1)

    shd = jnp.concatenate([by_device(small_grads[n], shard_axis[n]) for n in shd_names], axis=1)
    shd_pack = jnp.pad(shd, ((0, 0), (0, (-shd.shape[1]) % (LANE * LANE)))).reshape(-1, LANE)
    shd_rows = shd_pack.shape[0] // N_DEV
    rep_pack = _pack([small_grads[n] for n in rep_names])
    tok = scatter("in0", _mm(xb, dh0, "tn", _XCH, "ev_in_dw", b_parts=3, deps=tok), 1, n_in0)
    tok = scatter("small_rep", rep_pack + tok[0][0, 0], None, None)
    tok = scatter("small_shd", shd_pack + tok[0][0, 0], 0, shd_rows)
    grad_x = _mm(dh0, w_in0, "nt", _F32, "ev_in_dx", a_parts=3, add=dz1, add_scale=ALPHA, deps=tok)

    def landed(name, keys, after):
        got = _scatter_wait(name, [scat[k] for k in keys], after)
        return {k: (me1, dw, land) for k, (dw, land) in zip(keys, got)}

    early = landed("scatter_wait_early", ["dn1", "up1", "out1", "in1", "dn0", "up0", "out0"], grad_x)
    big = {}
    r_dn = _adamw_big(*early["dn1"], ffn_w_down, m_ffn_w_down, v_ffn_w_down, 0, n_dn, "adamw_w_dn1", layer=1)
    r_up = _adamw_big(*early["up1"], ffn_w_up, m_ffn_w_up, v_ffn_w_up, 1, n_up, "adamw_w_up1", layer=1)
    big["od_w_out"] = _adamw_big(*early["out1"], od_w_out[0], m_od_w_out[0], v_od_w_out[0], 0, n_out1, "adamw_w_out1")
    big["od_w_in"] = _adamw_big(*early["in1"], od_w_in[0], m_od_w_in[0], v_od_w_in[0], 1, n_in1, "adamw_w_in1")
    big["ffn_w_down"] = _adamw_big(*early["dn0"], ffn_w_down, m_ffn_w_down, v_ffn_w_down, 0, n_dn, "adamw_w_dn0", layer=0, into=r_dn)
    big["ffn_w_up"] = _adamw_big(*early["up0"], ffn_w_up, m_ffn_w_up, v_ffn_w_up, 1, n_up, "adamw_w_up0", layer=0, into=r_up)
    big["ev_w_out"] = _adamw_big(*early["out0"], ev_w_out[0], m_ev_w_out[0], v_ev_w_out[0], 0, n_out0, "adamw_w_out0")
    late = landed("scatter_wait_late", ["in0", "small_rep", "small_shd"],
                  (big["ffn_w_down"][0], big["ffn_w_up"][0], big["od_w_in"][0], big["ev_w_out"][0]))
    big["ev_w_in"] = _adamw_big(*late["in0"], ev_w_in[0], m_ev_w_in[0], v_ev_w_in[0], 1, n_in0, "adamw_w_in0")

    rep_mat = _sum_in_device_order(*late["small_rep"], "sum_small_rep")
    local_g = dict(zip(rep_names, _unpack(rep_mat, [small_grads[n].shape for n in rep_names])))
    _, shd_all, shd_land = late["small_shd"]
    shd_own = lax.dynamic_slice_in_dim(shd_all, me * shd_rows, shd_rows, axis=0)
    shd_mat = _sum_in_device_order(me1, shd_own, shd_land, "sum_small_shd")
    local_g.update(zip(shd_names, _unpack(shd_mat, [given[n][0].shape for n in shd_names])))
    local_shapes = [given[n][0].shape for n in small_names]
    res = _adamw(_pack([local_g[n] for n in small_names])[None], *small_packs, "adamw_small")
    small = {n: [] for n in small_names}
    for r in res:
        for n, a in zip(small_names, _unpack(r, local_shapes)):
            small[n].append(a)

    loss = lax.psum(loss11[0, 0], ("x", "y", "c"))
    order = ["ev_w_in", "ev_ln_v_g", "ev_ln_v_b", "ev_w_s", "ev_b_s", "ev_w_pool", "ev_pool_scale", "ev_w_out", "od_w_in",
             "od_norm_g", "od_w_out", "lb_param", "ffn_w_up", "ffn_conv_w", "ffn_conv_b", "ffn_w_down", "ln1_g", "ln1_b",
             "ln2_g", "ln2_b"]
    shapes = dict(ev_w_in=ev_w_in.shape, ev_w_out=ev_w_out.shape, od_w_in=od_w_in.shape, od_w_out=od_w_out.shape,
                  ffn_w_up=ffn_w_up.shape, ffn_w_down=ffn_w_down.shape)
    outs = [loss, grad_x[None]]
    for kind in range(4):
        for n in order:
            outs.append(big[n][kind].reshape(shapes[n]) if n in big else small[n][kind])
    return tuple(outs)
```

```python
import functools
import math

import jax
import jax.numpy as jnp
from jax import lax
from jax.experimental import pallas as pl
from jax.experimental.pallas import tpu as pltpu

_MM = jnp.bfloat16
_XCH = jnp.bfloat16

DEPTH = 2
ALPHA = (2 * DEPTH) ** 0.25
LN_EPS = 1e-5
A_CHUNK = 128
A_HEAD = 128
B_GROUPS = 4
POOL_HALO = 16
C_CHUNK = 64
C_HEAD = 128
SCAN_UNROLL = 16
CONV_HALO = 8
PACKED_ROWS = 16
FFN_ROWS, FFN_FWD_COLS = 512, 1408
FFN_BWD_ROWS, FFN_BWD_COLS = 512, 512
FFN_CHUNK = 128
ADAM_LR, ADAM_B1, ADAM_B2, ADAM_EPS, ADAM_WD, ADAM_STEP = 0.001, 0.9, 0.999, 1e-08, 0.01, 10
N_DEV = 8
LANE = 128
VMEM_LIMIT = 56 * 1024 * 1024
MM_FULL_K = 2048
MM_FULL_K_TN = 4096
MM_DEEP_K = 2816

_F32 = jnp.float32
_NN = (((1,), (0,)), ((), ()))
_NT = (((1,), (1,)), ((), ()))
_TN = (((0,), (0,)), ((), ()))
_S = jax.ShapeDtypeStruct


def _dot(a, b, dims=_NN):
    return lax.dot_general(a, b, dims, preferred_element_type=_F32)


def _tile(dim, pref):
    best = None
    d = LANE
    while d <= min(dim, pref):
        if dim % d == 0:
            best = d
        d += LANE
    return best if best is not None else dim


def _params(sem):
    return pltpu.CompilerParams(dimension_semantics=sem, vmem_limit_bytes=VMEM_LIMIT)


def _sigmoid(x):
    return 0.5 * jnp.tanh(0.5 * x) + 0.5


def _sigmoid_rel(x):
    return 1.0 / (1.0 + jnp.exp(-x))


_GELU_C = 0.7978845608028654
_GELU_A = 0.044715


def _gelu_and_grad(x):
    t = jnp.tanh(_GELU_C * (x + _GELU_A * x * x * x))
    y = 0.5 * x * (1.0 + t)
    dy = 0.5 * (1.0 + t) + 0.5 * x * (1.0 - t * t) * _GELU_C * (1.0 + 3.0 * _GELU_A * x * x)
    return y, dy


def _row_index(n):
    return lax.broadcasted_iota(jnp.int32, (n, 1), 0)


def _mm_tiles(mode, M, N, K, with_add):
    if mode == "tn":
        return _tile(M, 1024), _tile(N, 1024), _tile(K, MM_FULL_K_TN)
    if K <= MM_FULL_K:
        return _tile(M, 1024 if with_add else 2048), _tile(N, 1024 if mode == "nn" else 512), K
    return _tile(M, 1024), _tile(N, 1024), _tile(K, MM_DEEP_K)


def _mm(a, b, mode, out_dtype, name, *, a_parts=1, b_parts=1, out_parts=1, add=None, add_scale=1.0, deps=(), tiles=None):
    if mode == "nn":
        M, K = a.shape
        N = b.shape[1]
    elif mode == "nt":
        if a_parts > 1:
            M, K = a.shape[1], a.shape[2] * a_parts
        else:
            M, K = a.shape
        N = b.shape[0]
    else:
        K, M = a.shape
        N = b.shape[-1] * b_parts
    tm, tn, tk = tiles if tiles is not None else _mm_tiles(mode, M, N // max(b_parts, out_parts), K // a_parts, add is not None)
    nk = K // tk
    npj = (N // max(b_parts, out_parts)) // tn
    nkp = (K // a_parts) // tk
    if mode == "nn":
        a_spec = pl.BlockSpec((tm, tk), lambda i, j, k: (i, k))
        b_spec = pl.BlockSpec((tk, tn), lambda i, j, k: (k, j))
        dims = _NN
    elif mode == "nt":
        if a_parts > 1:
            a_spec = pl.BlockSpec((None, tm, tk), lambda i, j, k: (k // nkp, i, k % nkp))
        else:
            a_spec = pl.BlockSpec((tm, tk), lambda i, j, k: (i, k))
        b_spec = pl.BlockSpec((tn, tk), lambda i, j, k: (j, k))
        dims = _NT
    else:
        a_spec = pl.BlockSpec((tk, tm), lambda i, j, k: (k, i))
        if b_parts > 1:
            b_spec = pl.BlockSpec((None, tk, tn), lambda i, j, k: (j // npj, k, j % npj))
        else:
            b_spec = pl.BlockSpec((tk, tn), lambda i, j, k: (k, j))
        dims = _TN
    if out_parts > 1:
        out_spec = pl.BlockSpec((None, tm, tn), lambda i, j, k: (j // npj, i, j % npj))
        out_shape = _S((out_parts, M, N // out_parts), out_dtype)
    else:
        out_spec = pl.BlockSpec((tm, tn), lambda i, j, k: (i, j))
        out_shape = _S((M, N), out_dtype)
    in_specs = [a_spec, b_spec]
    args = [a, b]
    if add is not None:
        in_specs.append(pl.BlockSpec((tm, tn), lambda i, j, k: (i, j)))
        args.append(add)
    in_specs += [_ANY] * len(deps)
    args += list(deps)

    def finish(r, refs, o_ref):
        if add is not None:
            r = r + add_scale * refs[2][...]
        o_ref[...] = r.astype(o_ref.dtype)

    def body_one(*refs):
        finish(_dot(refs[0][...], refs[1][...], dims), refs, refs[-1])

    def body_acc(*refs):
        o_ref, acc = refs[-2], refs[-1]
        k = pl.program_id(2)

        @pl.when(k == 0)
        def _():
            acc[...] = jnp.zeros_like(acc)

        acc[...] += _dot(refs[0][...], refs[1][...], dims)

        @pl.when(k == nk - 1)
        def _():
            finish(acc[...], refs, o_ref)

    return pl.pallas_call(
        body_one if nk == 1 else body_acc, name=name, grid=(M // tm, N // tn, nk), in_specs=in_specs,
        out_specs=out_spec, out_shape=out_shape,
        scratch_shapes=[] if nk == 1 else [pltpu.VMEM((tm, tn), _F32)],
        compiler_params=_params(("parallel", "parallel", "arbitrary")),
    )(*args)


def _cast(x2d, dtype, name, deps=()):
    R, C = x2d.shape
    tr = _tile(R, 512) if R % LANE == 0 else R

    def body(x_ref, *rest):
        rest[-1][...] = x_ref[...].astype(rest[-1].dtype)

    return pl.pallas_call(
        body, name=name, grid=(R // tr,), in_specs=[pl.BlockSpec((tr, C), lambda i: (i, 0))] + [_ANY] * len(deps),
        out_specs=pl.BlockSpec((tr, C), lambda i: (i, 0)), out_shape=_S((R, C), dtype),
        compiler_params=_params(("parallel",)),
    )(x2d, *deps)


def _ln_fwd(z, g, b, name):
    T, D = z.shape
    tr = _tile(T, 256)

    def body(z_ref, g_ref, b_ref, y_ref, yb_ref):
        zz = z_ref[...]
        mu = jnp.mean(zz, axis=-1, keepdims=True)
        zc = zz - mu
        var = jnp.mean(zc * zc, axis=-1, keepdims=True)
        y = zc * lax.rsqrt(var + LN_EPS) * g_ref[...] + b_ref[...]
        y_ref[...] = y
        yb_ref[...] = y.astype(yb_ref.dtype)

    row = pl.BlockSpec((tr, D), lambda i: (i, 0))
    vec = pl.BlockSpec((1, D), lambda i: (0, 0))
    return pl.pallas_call(
        body, name=name, grid=(T // tr,), in_specs=[row, vec, vec], out_specs=[row, row],
        out_shape=[_S((T, D), _F32), _S((T, D), _MM)], compiler_params=_params(("parallel",)),
    )(z, g, b)


def _ln_bwd(z, g, dy, name):
    T, D = z.shape
    tr = _tile(T, 256)

    def body(z_ref, g_ref, dy_ref, dz_ref, dzb_ref, dg_ref, db_ref):
        @pl.when(pl.program_id(0) == 0)
        def _():
            dg_ref[...] = jnp.zeros_like(dg_ref)
            db_ref[...] = jnp.zeros_like(db_ref)

        zz = z_ref[...]
        mu = jnp.mean(zz, axis=-1, keepdims=True)
        zc = zz - mu
        rstd = lax.rsqrt(jnp.mean(zc * zc, axis=-1, keepdims=True) + LN_EPS)
        xh = zc * rstd
        d = dy_ref[...]
        dg_ref[...] += jnp.sum(d * xh, axis=0, keepdims=True)
        db_ref[...] += jnp.sum(d, axis=0, keepdims=True)
        dxh = d * g_ref[...]
        dz = rstd * (dxh - jnp.mean(dxh, axis=-1, keepdims=True) - xh * jnp.mean(dxh * xh, axis=-1, keepdims=True))
        dz_ref[...] = dz
        dzb_ref[...] = dz.astype(dzb_ref.dtype)

    row = pl.BlockSpec((tr, D), lambda i: (i, 0))
    vec = pl.BlockSpec((1, D), lambda i: (0, 0))
    return pl.pallas_call(
        body, name=name, grid=(T // tr,), in_specs=[row, vec, row], out_specs=[row, row, vec, vec],
        out_shape=[_S((T, D), _F32), _S((T, D), _MM), _S((1, D), _F32), _S((1, D), _F32)],
        compiler_params=_params(("arbitrary",)),
    )(z, g, dy)


def _ln_loss_bwd(z, g, b, target, name):
    T, D = z.shape
    tr = _tile(T, 256)

    def body(z_ref, g_ref, b_ref, t_ref, loss_ref, dz_ref, dzb_ref, dg_ref, db_ref, lacc):
        i = pl.program_id(0)

        @pl.when(i == 0)
        def _():
            dg_ref[...] = jnp.zeros_like(dg_ref)
            db_ref[...] = jnp.zeros_like(db_ref)
            lacc[...] = jnp.zeros_like(lacc)

        zz = z_ref[...]
        mu = jnp.mean(zz, axis=-1, keepdims=True)
        zc = zz - mu
        rstd = lax.rsqrt(jnp.mean(zc * zc, axis=-1, keepdims=True) + LN_EPS)
        xh = zc * rstd
        err = xh * g_ref[...] + b_ref[...] - t_ref[...]
        lacc[...] += jnp.sum(err * err, axis=0, keepdims=True)
        d = err * (1.0 / D)
        dg_ref[...] += jnp.sum(d * xh, axis=0, keepdims=True)
        db_ref[...] += jnp.sum(d, axis=0, keepdims=True)
        dxh = d * g_ref[...]
        dz = rstd * (dxh - jnp.mean(dxh, axis=-1, keepdims=True) - xh * jnp.mean(dxh * xh, axis=-1, keepdims=True))
        dz_ref[...] = dz
        dzb_ref[...] = dz.astype(dzb_ref.dtype)

        @pl.when(i == pl.num_programs(0) - 1)
        def _():
            loss_ref[...] = jnp.sum(lacc[...], axis=-1, keepdims=True) * (0.5 / D)

    row = pl.BlockSpec((tr, D), lambda i: (i, 0))
    vec = pl.BlockSpec((1, D), lambda i: (0, 0))
    one = pl.BlockSpec((1, 1), lambda i: (0, 0))
    return pl.pallas_call(
        body, name=name, grid=(T // tr,), in_specs=[row, vec, vec, row], out_specs=[one, row, row, vec, vec],
        out_shape=[_S((1, 1), _F32), _S((T, D), _F32), _S((T, D), _MM), _S((1, D), _F32), _S((1, D), _F32)],
        scratch_shapes=[pltpu.VMEM((1, D), _F32)], compiler_params=_params(("arbitrary",)),
    )(z, g, b, target)


def _conv3(X, cw, cb):
    return cb + cw[2:3] * X + cw[1:2] * pltpu.roll(X, 1, 0) + cw[0:1] * pltpu.roll(X, 2, 0)


def _ffn_mid_fwd(h, cw, cb, name):
    _, T, F = h.shape
    tr = _tile(T, FFN_ROWS)
    tc = _tile(F, FFN_FWD_COLS)
    nb = tr // CONV_HALO

    rc = _tile(tr, FFN_CHUNK)
    lanes = [slice(cs * LANE, (cs + 1) * LANE) for cs in range(tc // LANE)]

    def body(h_ref, p_ref, cw_ref, cb_ref, o_ref, c_ref):
        i = pl.program_id(0)

        def work(r0, cols, X):
            hc = [_conv3(X[part], cw_ref[part, :, cols], cb_ref[part, :, cols])[CONV_HALO:] for part in range(2)]
            for part in range(2):
                c_ref[part, pl.ds(r0, rc), cols] = hc[part].astype(c_ref.dtype)
            a, v = hc
            o_ref[pl.ds(r0, rc), cols] = (a * _sigmoid(a) * v).astype(o_ref.dtype)

        for cols in lanes:
            work(0, cols, [jnp.concatenate([jnp.where(i == 0, 0.0, p_ref[part, :, cols]), h_ref[part, 0:rc, cols]], axis=0)
                           for part in range(2)])

        def chunk(c, carry):
            r0 = pl.multiple_of(c * rc, rc)
            for cols in lanes:
                work(r0, cols, [h_ref[part, pl.ds(r0 - CONV_HALO, rc + CONV_HALO), cols] for part in range(2)])
            return carry

        lax.fori_loop(1, tr // rc, chunk, 0)

    return pl.pallas_call(
        body, name=name, grid=(T // tr, F // tc),
        in_specs=[pl.BlockSpec((2, tr, tc), lambda i, j: (0, i, j)),
                  pl.BlockSpec((2, CONV_HALO, tc), lambda i, j: (0, jnp.maximum(i * nb - 1, 0), j)),
                  pl.BlockSpec((2, 3, tc), lambda i, j: (0, 0, j)),
                  pl.BlockSpec((2, 1, tc), lambda i, j: (0, 0, j))],
        out_specs=[pl.BlockSpec((tr, tc), lambda i, j: (i, j)), pl.BlockSpec((2, tr, tc), lambda i, j: (0, i, j))],
        out_shape=[_S((T, F), _MM), _S((2, T, F), _MM)],
        compiler_params=_params(("parallel", "parallel")),
    )(h, h, cw, cb)


def _ffn_mid_bwd(h, hc, dact, cw, name):
    _, T, F = h.shape
    tr = _tile(T, FFN_BWD_ROWS)
    tc = _tile(F, FFN_BWD_COLS)
    nb_c = tr // PACKED_ROWS
    rc = _tile(tr, FFN_CHUNK)
    n = rc + CONV_HALO

    def body(h_ref, c_ref, cn_ref, d_ref, dn_ref, cw_ref, dh_ref, dcw_ref, dcb_ref):
        i = pl.program_id(1)
        is_last = i == pl.num_programs(1) - 1

        @pl.when(i == 0)
        def _():
            dcw_ref[...] = jnp.zeros_like(dcw_ref)
            dcb_ref[...] = jnp.zeros_like(dcb_ref)

        def work(r0, cols, a, v, D):
            sg = _sigmoid(a)
            dhc = [D * v * sg * (1.0 + a * (1.0 - sg)), D * a * sg]
            for part in range(2):
                X = h_ref[part, pl.ds(r0, rc), cols]
                cwp = cw_ref[part, :, cols]
                dh = None
                for k in range(3):
                    g = (dhc[part] if k == 0 else pltpu.roll(dhc[part], n - k, 0))[0:rc]
                    term = cwp[2 - k:3 - k] * g
                    dh = term if dh is None else dh + term
                    dcw_ref[part, 2 - k:3 - k, cols] += jnp.sum(g * X, axis=0, keepdims=True)
                    if k == 0:
                        dcb_ref[part, :, cols] += jnp.sum(g, axis=0, keepdims=True)
                dh_ref[part, pl.ds(r0, rc), cols] = dh.astype(dh_ref.dtype)

        lanes = [slice(cs * LANE, (cs + 1) * LANE) for cs in range(tc // LANE)]

        def chunk(c, carry):
            r0 = pl.multiple_of(c * rc, rc)
            for cols in lanes:
                a, v = [c_ref[part, pl.ds(r0, rc + PACKED_ROWS), cols].astype(_F32)[0:n] for part in range(2)]
                work(r0, cols, a, v, d_ref[pl.ds(r0, rc + PACKED_ROWS), cols].astype(_F32)[0:n])
            return carry

        lax.fori_loop(0, tr // rc - 1, chunk, 0)
        r0 = tr - rc
        for cols in lanes:
            a, v = [jnp.concatenate([c_ref[part, r0:tr, cols].astype(_F32), cn_ref[part, :, cols].astype(_F32)[0:CONV_HALO]],
                                    axis=0) for part in range(2)]
            D = jnp.concatenate([d_ref[r0:tr, cols].astype(_F32),
                                 jnp.where(is_last, 0.0, dn_ref[:, cols].astype(_F32)[0:CONV_HALO])], axis=0)
            work(r0, cols, a, v, D)

    return pl.pallas_call(
        body, name=name, grid=(F // tc, T // tr),
        in_specs=[pl.BlockSpec((2, tr, tc), lambda j, i: (0, i, j)),
                  pl.BlockSpec((2, tr, tc), lambda j, i: (0, i, j)),
                  pl.BlockSpec((2, PACKED_ROWS, tc), lambda j, i: (0, jnp.minimum((i + 1) * nb_c, T // PACKED_ROWS - 1), j)),
                  pl.BlockSpec((tr, tc), lambda j, i: (i, j)),
                  pl.BlockSpec((PACKED_ROWS, tc), lambda j, i: (jnp.minimum((i + 1) * nb_c, T // PACKED_ROWS - 1), j)),
                  pl.BlockSpec((2, 3, tc), lambda j, i: (0, 0, j))],
        out_specs=[pl.BlockSpec((2, tr, tc), lambda j, i: (0, i, j)),
                   pl.BlockSpec((2, 3, tc), lambda j, i: (0, 0, j)),
                   pl.BlockSpec((2, 1, tc), lambda j, i: (0, 0, j))],
        out_shape=[_S((2, T, F), _MM), _S((2, 3, F), _F32), _S((2, 1, F), _F32)],
        compiler_params=_params(("parallel", "arbitrary")),
    )(h, hc, hc, dact, dact, cw)


def _ev_common(h_ref, hp_ref, lng_ref, lnb_ref, ws_ref, bias_ref, i, tr, W):
    H = W // A_HEAD
    u, gu = _gelu_and_grad(h_ref[0])
    v, gv = _gelu_and_grad(h_ref[1])
    mu = jnp.mean(v, axis=-1, keepdims=True)
    vc = v - mu
    rstd = lax.rsqrt(jnp.mean(vc * vc, axis=-1, keepdims=True) + LN_EPS)
    vhat = vc * rstd
    vb = (vhat * lng_ref[...] + lnb_ref[...]).astype(_MM)
    s_chunks = []
    for c in range(tr // A_CHUNK):
        r0 = c * A_CHUNK
        heads = [_dot(ws_ref[hd], vb[r0:r0 + A_CHUNK, hd * A_HEAD:(hd + 1) * A_HEAD]) for hd in range(H)]
        s_chunks.append(jnp.concatenate(heads, axis=1) + bias_ref[...])
    prev = jnp.where(i == 0, 0.0, hp_ref[...])
    X = jnp.concatenate([prev, h_ref[2]], axis=0)
    return u, gu, gv, rstd, vhat, vb, s_chunks, X


def _pool_inv_count(i, tr, rows, win):
    pos = i * tr + _row_index(rows) + 1
    return 1.0 / jnp.minimum(pos, win).astype(_F32)


def _pool_fwd(X, g, Wg, i, tr):
    xg = X[:, g * Wg:(g + 1) * Wg]
    s = xg
    for k in range(g + 1):
        s = s + pltpu.roll(s, 2 ** k, 0)
    return s[POOL_HALO:] * _pool_inv_count(i, tr, tr, 2 ** (g + 1)) - xg[POOL_HALO:]


def _ev_mid_fwd(h, lng, lnb, ws, bias, wp, sc, name):
    _, T, W = h.shape
    tr = _tile(T, 256)
    H = W // A_HEAD
    Wg = W // B_GROUPS
    nb = tr // POOL_HALO

    def body(h_ref, hp_ref, lng_ref, lnb_ref, ws_ref, bias_ref, wp_ref, sc_ref, o_ref):
        i = pl.program_id(0)
        u, _, _, _, _, _, s_chunks, X = _ev_common(h_ref, hp_ref, lng_ref, lnb_ref, ws_ref, bias_ref, i, tr, W)
        for c, s in enumerate(s_chunks):
            r0 = c * A_CHUNK
            o_ref[r0:r0 + A_CHUNK, 0:W] = (u[r0:r0 + A_CHUNK] * s).astype(o_ref.dtype)
        for g in range(B_GROUPS):
            p = _pool_fwd(X, g, Wg, i, tr)
            y = _dot(p.astype(_MM), wp_ref[g]) * sc_ref[:, g * Wg:(g + 1) * Wg]
            o_ref[:, W + g * Wg:W + (g + 1) * Wg] = y.astype(o_ref.dtype)

    vec = pl.BlockSpec((1, W), lambda i: (0, 0))
    return pl.pallas_call(
        body, name=name, grid=(T // tr,),
        in_specs=[pl.BlockSpec((3, tr, W), lambda i: (0, i, 0)),
                  pl.BlockSpec((None, POOL_HALO, W), lambda i: (2, jnp.maximum(i * nb - 1, 0), 0)),
                  vec, vec,
                  pl.BlockSpec((H, A_CHUNK, A_CHUNK), lambda i: (0, 0, 0)),
                  pl.BlockSpec((A_CHUNK, W), lambda i: (0, 0)),
                  pl.BlockSpec((B_GROUPS, Wg, Wg), lambda i: (0, 0, 0)),
                  vec],
        out_specs=pl.BlockSpec((tr, 2 * W), lambda i: (i, 0)), out_shape=_S((T, 2 * W), _MM),
        compiler_params=_params(("parallel",)),
    )(h, h, lng, lnb, ws, bias, wp, sc)


def _ev_mid_bwd(h, dy, lng, lnb, ws, bias, wp, sc, name):
    _, T, W = h.shape
    tr = _tile(T, 256)
    H = W // A_HEAD
    Wg = W // B_GROUPS
    nb = tr // POOL_HALO
    last_blk = T // POOL_HALO - 1
    n = tr + POOL_HALO

    def body(h_ref, hp_ref, dy_ref, dyn_ref, lng_ref, lnb_ref, ws_ref, bias_ref, wp_ref, sc_ref,
             dh_ref, dws_ref, dbias_ref, dlng_ref, dlnb_ref, dwp_ref, dsc_ref):
        i = pl.program_id(0)

        @pl.when(i == 0)
        def _():
            for r in (dws_ref, dbias_ref, dlng_ref, dlnb_ref, dwp_ref, dsc_ref):
                r[...] = jnp.zeros_like(r)

        u, gu, gv, rstd, vhat, vb, s_chunks, X = _ev_common(h_ref, hp_ref, lng_ref, lnb_ref, ws_ref, bias_ref, i, tr, W)
        rr = lax.broadcasted_iota(jnp.int32, (A_CHUNK, A_CHUNK), 0)
        cc = lax.broadcasted_iota(jnp.int32, (A_CHUNK, A_CHUNK), 1)
        tril = rr >= cc
        du_chunks, dvln_chunks = [], []
        for c, s in enumerate(s_chunks):
            r0 = c * A_CHUNK
            dya = dy_ref[r0:r0 + A_CHUNK, 0:W]
            du_chunks.append(dya * s)
            ds = dya * u[r0:r0 + A_CHUNK]
            dbias_ref[...] += ds
            dsb = ds.astype(_MM)
            heads = []
            for hd in range(H):
                cols = slice(hd * A_HEAD, (hd + 1) * A_HEAD)
                dws_ref[hd] += jnp.where(tril, _dot(dsb[:, cols], vb[r0:r0 + A_CHUNK, cols], _NT), 0.0)
                heads.append(_dot(ws_ref[hd], dsb[:, cols], _TN))
            dvln_chunks.append(jnp.concatenate(heads, axis=1))
        du = jnp.concatenate(du_chunks, axis=0)
        dvln = jnp.concatenate(dvln_chunks, axis=0)
        dlng_ref[...] += jnp.sum(dvln * vhat, axis=0, keepdims=True)
        dlnb_ref[...] += jnp.sum(dvln, axis=0, keepdims=True)
        dxh = dvln * lng_ref[...]
        dv = rstd * (dxh - jnp.mean(dxh, axis=-1, keepdims=True) - vhat * jnp.mean(dxh * vhat, axis=-1, keepdims=True))
        dh_ref[0] = (du * gu).astype(dh_ref.dtype)
        dh_ref[1] = (dv * gv).astype(dh_ref.dtype)

        dyb = dy_ref[:, W:2 * W]
        dyb_full = jnp.concatenate([dyb, jnp.where(i == pl.num_programs(0) - 1, 0.0, dyn_ref[...])], axis=0)
        for g in range(B_GROUPS):
            cols = slice(g * Wg, (g + 1) * Wg)
            pb = _pool_fwd(X, g, Wg, i, tr).astype(_MM)
            ypre = _dot(pb, wp_ref[g])
            dsc_ref[:, cols] += jnp.sum(dyb[:, cols] * ypre, axis=0, keepdims=True)
            dyp = (dyb_full[:, cols] * sc_ref[:, cols]).astype(_MM)
            dwp_ref[g] += _dot(pb, dyp[0:tr], _TN)
            dp = _dot(dyp, wp_ref[g], _NT)
            s = dp * _pool_inv_count(i, tr, n, 2 ** (g + 1))
            for k in range(g + 1):
                s = s + pltpu.roll(s, n - 2 ** k, 0)
            dh_ref[2, :, cols] = (s[0:tr] - dp[0:tr]).astype(dh_ref.dtype)

    vec = pl.BlockSpec((1, W), lambda i: (0, 0))
    ws_spec = pl.BlockSpec((H, A_CHUNK, A_CHUNK), lambda i: (0, 0, 0))
    bias_spec = pl.BlockSpec((A_CHUNK, W), lambda i: (0, 0))
    wp_spec = pl.BlockSpec((B_GROUPS, Wg, Wg), lambda i: (0, 0, 0))
    return pl.pallas_call(
        body, name=name, grid=(T // tr,),
        in_specs=[pl.BlockSpec((3, tr, W), lambda i: (0, i, 0)),
                  pl.BlockSpec((None, POOL_HALO, W), lambda i: (2, jnp.maximum(i * nb - 1, 0), 0)),
                  pl.BlockSpec((tr, 2 * W), lambda i: (i, 0)),
                  pl.BlockSpec((POOL_HALO, W), lambda i: (jnp.minimum((i + 1) * nb, last_blk), 1)),
                  vec, vec, ws_spec, bias_spec, wp_spec, vec],
        out_specs=[pl.BlockSpec((3, tr, W), lambda i: (0, i, 0)), ws_spec, bias_spec, vec, vec, wp_spec, vec],
        out_shape=[_S((3, T, W), _MM), _S((H, A_CHUNK, A_CHUNK), _F32), _S((A_CHUNK, W), _F32), _S((1, W), _F32),
                   _S((1, W), _F32), _S((B_GROUPS, Wg, Wg), _F32), _S((1, W), _F32)],
        compiler_params=_params(("arbitrary",)),
    )(h, h, dy, dy, lng, lnb, ws, bias, wp, sc)


def _chunk_cumsum(x, rin):
    s = 1
    while s < C_CHUNK:
        x = x + jnp.where(rin >= s, pltpu.roll(x, s, 0), 0.0)
        s *= 2
    return x


def _chunk_revcumsum(x, rin):
    n = x.shape[0]
    s = 1
    while s < C_CHUNK:
        x = x + jnp.where(rin + s < C_CHUNK, pltpu.roll(x, n - s, 0), 0.0)
        s *= 2
    return x


def _hgrn_gates(q, fl, lb, tr, tc):
    nch = tr // C_CHUNK
    sq = _sigmoid(q)
    sf = _sigmoid_rel(fl)
    f = lb + (1.0 - lb) * sf
    logf = jnp.log(f)
    rin = _row_index(tr) % C_CHUNK
    b = _chunk_cumsum(logf, rin)
    tot3 = jnp.sum(logf.reshape(nch, C_CHUNK, tc), axis=1, keepdims=True)
    eb = jnp.exp(b)
    enb = jnp.exp(-b)
    ekb = jnp.exp(tot3 - b.reshape(nch, C_CHUNK, tc)).reshape(tr, tc)
    return sq, sf, f, rin, tot3, eb, enb, ekb


def _hgrn_prep_fwd(h, lb, name):
    _, T, D = h.shape
    tr = _tile(T, 512)
    tc = _tile(D, 512)
    nch = tr // C_CHUNK

    def body(q_ref, f_ref, v_ref, lb_ref, qd_ref, kd_ref, ke_ref, vb_ref, dec_ref):
        q = q_ref[...]
        sq, _, f, _, tot3, eb, enb, ekb = _hgrn_gates(q, f_ref[...], lb_ref[...], tr, tc)
        kk = 1.0 - f
        qd_ref[...] = (q * sq * eb).astype(qd_ref.dtype)
        kd_ref[...] = (kk * enb).astype(kd_ref.dtype)
        ke_ref[...] = (kk * ekb).astype(ke_ref.dtype)
        vb_ref[...] = v_ref[...].astype(vb_ref.dtype)
        dec_ref[...] = jnp.exp(tot3).reshape(nch, tc)

    def part(p):
        return pl.BlockSpec((None, tr, tc), lambda i, j: (p, i, j))

    blk = pl.BlockSpec((tr, tc), lambda i, j: (i, j))
    return pl.pallas_call(
        body, name=name, grid=(T // tr, D // tc),
        in_specs=[part(0), part(1), part(2), pl.BlockSpec((1, tc), lambda i, j: (0, j))],
        out_specs=[blk, blk, blk, blk, pl.BlockSpec((nch, tc), lambda i, j: (i, j))],
        out_shape=[_S((T, D), _MM)] * 4 + [_S((T // C_CHUNK, D), _F32)],
        compiler_params=_params(("parallel", "parallel")),
    )(h, h, h, lb)


def _tril_mask():
    rr = lax.broadcasted_iota(jnp.int32, (C_CHUNK, C_CHUNK), 0)
    cc = lax.broadcasted_iota(jnp.int32, (C_CHUNK, C_CHUNK), 1)
    return rr >= cc


def _hgrn_scan_fwd(qd, kd, ke, vb, dec, h, ng, name):
    T, D = qd.shape
    NH = D // C_HEAD
    N = T // C_CHUNK

    def body(qd_ref, kd_ref, ke_ref, vb_ref, dec_ref, g_ref, ng_ref, o_ref, y_ref, st_ref):
        mask = _tril_mask()

        def step(n, St):
            r = pl.ds(pl.multiple_of(n * C_CHUNK, C_CHUNK), C_CHUNK)
            Qd, Kd, Ke, V = qd_ref[r, :], kd_ref[r, :], ke_ref[r, :], vb_ref[r, :]
            att = jnp.where(mask, _dot(Qd, Kd, _NT), 0.0).astype(_MM)
            o_ref[r, :] = _dot(att, V) + _dot(Qd, St.astype(_MM), _NT)
            st_ref[n] = St
            return St * dec_ref[pl.ds(n, 1), :] + _dot(V, Ke, _TN)

        per_trip = math.gcd(N, SCAN_UNROLL)

        def trip(i, state):
            for u in range(per_trip):
                state = step(i * per_trip + u, state)
            return state

        lax.fori_loop(0, N // per_trip, trip, jnp.zeros((C_HEAD, C_HEAD), _F32))
        o = o_ref[...]
        r = lax.rsqrt(jnp.mean(o * o, axis=-1, keepdims=True) + LN_EPS)
        y_ref[...] = (o * r * ng_ref[...] * _sigmoid(g_ref[...])).astype(y_ref.dtype)

    col = pl.BlockSpec((T, C_HEAD), lambda j: (0, j))
    return pl.pallas_call(
        body, name=name, grid=(NH,),
        in_specs=[col, col, col, col, pl.BlockSpec((N, C_HEAD), lambda j: (0, j)),
                  pl.BlockSpec((None, T, C_HEAD), lambda j: (3, 0, j)), pl.BlockSpec((1, C_HEAD), lambda j: (0, j))],
        out_specs=[col, col, pl.BlockSpec((None, N, C_HEAD, C_HEAD), lambda j: (j, 0, 0, 0))],
        out_shape=[_S((T, D), _F32), _S((T, D), _MM), _S((NH, N, C_HEAD, C_HEAD), _F32)],
        compiler_params=_params(("parallel",)),
    )(qd, kd, ke, vb, dec, h, ng)


def _hgrn_scan_bwd(qd, kd, ke, vb, dec, st, o, h, ng, dy, name):
    T, D = qd.shape
    NH = D // C_HEAD
    N = T // C_CHUNK

    def body(qd_ref, kd_ref, ke_ref, vb_ref, dec_ref, st_ref, o_ref, g_ref, ng_ref, dy_ref,
             dqd_ref, dkd_ref, dke_ref, dv_ref, dgate_ref, ddec_ref, dng_ref, do_s):
        o = o_ref[...]
        r = lax.rsqrt(jnp.mean(o * o, axis=-1, keepdims=True) + LN_EPS)
        oh = o * r
        gn = ng_ref[...]
        sg = _sigmoid(g_ref[...])
        d = dy_ref[...]
        dyn = d * sg
        dgate_ref[...] = (d * oh * gn * sg * (1.0 - sg)).astype(dgate_ref.dtype)
        dng_ref[...] = jnp.sum(dyn * oh, axis=0, keepdims=True)
        doh = dyn * gn
        do_s[...] = (r * (doh - oh * jnp.mean(doh * oh, axis=-1, keepdims=True))).astype(do_s.dtype)
        mask = _tril_mask()

        def step(k, dSt):
            n = N - 1 - k
            rws = pl.ds(pl.multiple_of(n * C_CHUNK, C_CHUNK), C_CHUNK)
            Qd, Kd, Ke, V, dO = qd_ref[rws, :], kd_ref[rws, :], ke_ref[rws, :], vb_ref[rws, :], do_s[rws, :]
            St = st_ref[n]
            Stb = St.astype(_MM)
            dStb = dSt.astype(_MM)
            att = jnp.where(mask, _dot(Qd, Kd, _NT), 0.0).astype(_MM)
            dA = jnp.where(mask, _dot(dO, V, _NT), 0.0).astype(_MM)
            dv_ref[rws, :] = (_dot(att, dO, _TN) + _dot(Ke, dStb, _NT)).astype(dv_ref.dtype)
            dqd_ref[rws, :] = _dot(dA, Kd) + _dot(dO, Stb)
            dkd_ref[rws, :] = _dot(dA, Qd, _TN)
            dke_ref[rws, :] = _dot(V, dStb)
            ddec_ref[pl.ds(n, 1), :] = jnp.sum(dSt * St, axis=0, keepdims=True)
            return dSt * dec_ref[pl.ds(n, 1), :] + _dot(dO, Qd, _TN)

        per_trip = math.gcd(N, SCAN_UNROLL)

        def trip(i, state):
            for u in range(per_trip):
                state = step(i * per_trip + u, state)
            return state

        lax.fori_loop(0, N // per_trip, trip, jnp.zeros((C_HEAD, C_HEAD), _F32))

    col = pl.BlockSpec((T, C_HEAD), lambda j: (0, j))
    chk = pl.BlockSpec((N, C_HEAD), lambda j: (0, j))
    one = pl.BlockSpec((1, C_HEAD), lambda j: (0, j))
    return pl.pallas_call(
        body, name=name, grid=(NH,),
        in_specs=[col, col, col, col, chk, pl.BlockSpec((None, N, C_HEAD, C_HEAD), lambda j: (j, 0, 0, 0)), col,
                  pl.BlockSpec((None, T, C_HEAD), lambda j: (3, 0, j)), one, col],
        out_specs=[col, col, col, col, col, chk, one],
        out_shape=[_S((T, D), _F32)] * 3 + [_S((T, D), _MM)] * 2 + [_S((N, D), _F32), _S((1, D), _F32)],
        scratch_shapes=[pltpu.VMEM((T, C_HEAD), _MM)],
        compiler_params=_params(("parallel",)),
    )(qd, kd, ke, vb, dec, st, o, h, ng, dy)


def _hgrn_prep_bwd(h, lb, dqd, dkd, dke, dv, dgate, ddec, name):
    _, T, D = h.shape
    tr = _tile(T, 512)
    tc = _tile(D, 256)
    nch = tr // C_CHUNK

    def body(q_ref, f_ref, lb_ref, dqd_ref, dkd_ref, dke_ref, dv_ref, dgate_ref, ddec_ref, dh_ref, dlb_ref):
        @pl.when(pl.program_id(1) == 0)
        def _():
            dlb_ref[...] = jnp.zeros_like(dlb_ref)

        q = q_ref[...]
        lb = lb_ref[...]
        sq, sf, f, rin, tot3, eb, enb, ekb = _hgrn_gates(q, f_ref[...], lb, tr, tc)
        kk = 1.0 - f
        dQd, dKd, dKe = dqd_ref[...], dkd_ref[...], dke_ref[...]
        tq = dQd * eb
        tkd = dKd * enb
        tke = dKe * ekb
        ke_term = tke * kk
        db = tq * (q * sq) - tkd * kk - ke_term
        dtot3 = (jnp.sum(ke_term.reshape(nch, C_CHUNK, tc), axis=1, keepdims=True)
                 + (ddec_ref[...] * jnp.exp(tot3).reshape(nch, tc)).reshape(nch, 1, tc))
        dlogf = (_chunk_revcumsum(db, rin).reshape(nch, C_CHUNK, tc) + dtot3).reshape(tr, tc)
        df = dlogf / f - (tkd + tke)
        dh_ref[0] = (tq * sq * (1.0 + q * (1.0 - sq))).astype(dh_ref.dtype)
        dh_ref[1] = (df * (1.0 - lb) * sf * (1.0 - sf)).astype(dh_ref.dtype)
        dh_ref[2] = dv_ref[...]
        dh_ref[3] = dgate_ref[...]
        dlb_ref[...] += jnp.sum(df * (1.0 - sf), axis=0, keepdims=True)

    def part(p):
        return pl.BlockSpec((None, tr, tc), lambda j, i: (p, i, j))

    blk = pl.BlockSpec((tr, tc), lambda j, i: (i, j))
    vec = pl.BlockSpec((1, tc), lambda j, i: (0, j))
    return pl.pallas_call(
        body, name=name, grid=(D // tc, T // tr),
        in_specs=[part(0), part(1), vec, blk, blk, blk, blk, blk, pl.BlockSpec((nch, tc), lambda j, i: (i, j))],
        out_specs=[pl.BlockSpec((4, tr, tc), lambda j, i: (0, i, j)), vec],
        out_shape=[_S((4, T, D), _MM), _S((1, D), _F32)],
        compiler_params=_params(("parallel", "arbitrary")),
    )(h, h, lb, dqd, dkd, dke, dv, dgate, ddec)


def _sum_in_device_order(me1, own, land, name):
    R, C = own.shape
    tr = _tile(R, 256)

    def body(me_ref, own_ref, land_ref, o_ref):
        me = me_ref[0]
        g = None
        for j in range(N_DEV):
            slot = jnp.maximum(jnp.bitwise_xor(me, j) - 1, 0)
            p = jnp.where(me == j, own_ref[...], land_ref[slot])
            g = p if g is None else g + p
        o_ref[...] = g

    return pl.pallas_call(
        body, name=name,
        grid_spec=pltpu.PrefetchScalarGridSpec(
            num_scalar_prefetch=1, grid=(R // tr,),
            in_specs=[pl.BlockSpec((tr, C), lambda i, me: (i, 0)), pl.BlockSpec((N_DEV - 1, tr, C), lambda i, me: (0, i, 0))],
            out_specs=pl.BlockSpec((tr, C), lambda i, me: (i, 0))),
        out_shape=_S((R, C), _F32), compiler_params=_params(("parallel",)),
    )(me1, own, land)


def _adamw(parts, w, m, v, name):
    P, R, C = parts.shape
    tr = _tile(R, 128) if R % LANE == 0 else R

    def body(p_ref, w_ref, m_ref, v_ref, g_ref, d_ref, nm_ref, nv_ref):
        g = p_ref[0].astype(_F32)
        for s in range(1, P):
            g = g + p_ref[s].astype(_F32)
        nm = ADAM_B1 * m_ref[...] + (1.0 - ADAM_B1) * g
        nv = ADAM_B2 * v_ref[...] + (1.0 - ADAM_B2) * (g * g)
        m_hat = nm / (1.0 - ADAM_B1 ** ADAM_STEP)
        v_hat = nv / (1.0 - ADAM_B2 ** ADAM_STEP)
        g_ref[...] = g
        d_ref[...] = -ADAM_LR * (m_hat / (jnp.sqrt(v_hat) + ADAM_EPS) + ADAM_WD * w_ref[...])
        nm_ref[...] = nm
        nv_ref[...] = nv

    blk = pl.BlockSpec((tr, C), lambda i: (i, 0))
    return pl.pallas_call(
        body, name=name, grid=(R // tr,), in_specs=[pl.BlockSpec((P, tr, C), lambda i: (0, i, 0)), blk, blk, blk],
        out_specs=[blk] * 4, out_shape=[_S((R, C), _F32)] * 4, compiler_params=_params(("parallel",)),
    )(parts, w, m, v)


def _exchange(name, srcs, out_shapes, jobs, deps=()):
    ns, nj = len(srcs), len(jobs)

    nd = len(deps)

    def body(*refs):
        ins, outs = refs[:ns], refs[ns + nd:ns + nd + len(out_shapes)]
        send_sems, recv_sems, local_sems = refs[-3:]
        x, y, c = lax.axis_index("x"), lax.axis_index("y"), lax.axis_index("c")
        me = 4 * x + 2 * y + c
        local = []
        for ji, (si, src_fn, di, dst_fn) in enumerate(jobs):
            cp = pltpu.make_async_copy(src_fn(ins[si], me, me), dst_fn(outs[di], me), local_sems.at[ji])
            cp.start()
            local.append(cp)
        remote = []
        for k in range(1, N_DEV):
            px, py, pc = (x + (k >> 2)) % 2, (y + ((k >> 1) & 1)) % 2, (c + (k & 1)) % 2
            to = 4 * px + 2 * py + pc
            for ji, (si, src_fn, di, dst_fn) in enumerate(jobs):
                sem = (k - 1) * nj + ji
                cp = pltpu.make_async_remote_copy(
                    src_ref=src_fn(ins[si], me, to), dst_ref=dst_fn(outs[di], me),
                    send_sem=send_sems.at[sem], recv_sem=recv_sems.at[sem],
                    device_id=(px, py, pc), device_id_type=pl.DeviceIdType.MESH)
                cp.start()
                remote.append(cp)
        for cp in remote:
            cp.wait_recv()
        for cp in remote:
            cp.wait_send()
        for cp in local:
            cp.wait()

    hbm = pl.BlockSpec(memory_space=pltpu.HBM)
    return pl.pallas_call(
        body, name=name, in_specs=[hbm] * ns + [_ANY] * nd, out_specs=[hbm] * len(out_shapes), out_shape=list(out_shapes),
        scratch_shapes=[pltpu.SemaphoreType.DMA(((N_DEV - 1) * nj,)), pltpu.SemaphoreType.DMA(((N_DEV - 1) * nj,)),
                        pltpu.SemaphoreType.DMA((nj,))],
    )(*srcs, *deps)


def _whole(ref, me, to):
    return ref


def _slot_job(i, o):
    def dst(ref, me):
        return ref.at[me]
    return (i, _whole, o, dst)


_HBM = pl.BlockSpec(memory_space=pltpu.HBM)
_SEM = pl.BlockSpec(memory_space=pltpu.SEMAPHORE)
_ANY = pl.BlockSpec(memory_space=pl.ANY)
_N_PEER = N_DEV - 1


def _split_params():
    return pltpu.CompilerParams(has_side_effects=pltpu.SideEffectType.DATAFLOW_SIDE_EFFECTING)


def _blk(ref, axis, n, idx):
    if axis is None:
        return ref
    return ref.at[tuple([slice(None)] * axis + [pl.ds(pl.multiple_of(idx * n, n), n)])]


def _peer(k):
    x, y, c = lax.axis_index("x"), lax.axis_index("y"), lax.axis_index("c")
    px, py, pc = (x + (k >> 2)) % 2, (y + ((k >> 1) & 1)) % 2, (c + (k & 1)) % 2
    return (px, py, pc), 4 * px + 2 * py + pc, 4 * x + 2 * y + c


def _row_tile(rows, pref):
    best = None
    for d in range(16, min(rows, pref) + 1, 16):
        if rows % d == 0:
            best = d
    return best if best is not None else rows


def _place(w, me1, axis, name, layer=None, deps=()):
    R, C = w.shape[-2:]
    tr = _row_tile(R, 512)
    nb = R // tr
    lead = () if layer is None else (None,)
    pre = () if layer is None else (layer,)

    def body(me_ref, w_ref, *rest):
        rest[-1][...] = w_ref[...].astype(rest[-1].dtype)

    if axis == 1:
        out_spec = pl.BlockSpec((tr, C), lambda i, me: (i, me[0]))
        out_shape = _S((R, N_DEV * C), _MM)
    else:
        out_spec = pl.BlockSpec((tr, C), lambda i, me: (me[0] * nb + i, 0))
        out_shape = _S((N_DEV * R, C), _MM)
    return pl.pallas_call(
        body, name=name,
        grid_spec=pltpu.PrefetchScalarGridSpec(
            num_scalar_prefetch=1, grid=(nb,),
            in_specs=[pl.BlockSpec(lead + (tr, C), lambda i, me: pre + (i, 0))] + [_ANY] * len(deps), out_specs=out_spec),
        out_shape=out_shape, compiler_params=_params(("parallel",)),
    )(me1, w, *deps)


_SIBLING = 1
_CHIPS = (2, 4, 6)
_VMEM_TOKEN = pl.BlockSpec(memory_space=pltpu.VMEM)


def _remote(ref_blk, send_sem, recv_sem, dev):
    return pltpu.make_async_remote_copy(src_ref=ref_blk, dst_ref=ref_blk, send_sem=send_sem, recv_sem=recv_sem,
                                        device_id=dev, device_id_type=pl.DeviceIdType.MESH)


def _gather_start(name, full, axis, n):
    def body(f_ref, send, recv, f_out, token):
        for i, k in enumerate((_SIBLING,) + _CHIPS):
            dev, _, me = _peer(k)
            _remote(_blk(f_ref, axis, n, me), send.at[i], recv.at[i], dev).start()
        token[...] = jnp.zeros_like(token)

    return pl.pallas_call(
        body, name=name,
        out_shape=(pltpu.SemaphoreType.DMA((4,)), pltpu.SemaphoreType.DMA((4,)), pltpu.HBM(full.shape, full.dtype),
                   _S((8, LANE), _F32)),
        in_specs=(_HBM,), out_specs=(_SEM, _SEM, _HBM, _VMEM_TOKEN),
        input_output_aliases={0: 2}, compiler_params=_split_params(),
    )(pltpu.with_memory_space_constraint(full, pltpu.HBM))


def _gather_forward(name, full, axis, n, recv, after):
    after = tuple(after) if isinstance(after, (tuple, list)) else (after,)

    def body(f_ref, recv_r, *rest):
        send2, recv2, f_out, token = rest[-4:]
        sib, _, _ = _peer(_SIBLING)
        for i, k in enumerate(_CHIPS):
            dev, frm, _ = _peer(k)
            blk = _blk(f_ref, axis, n, frm)
            _remote(blk, send2.at[i], recv_r.at[1 + i], dev).wait_recv()
            _remote(blk, send2.at[i], recv2.at[i], sib).start()
        token[...] = jnp.zeros_like(token)

    return pl.pallas_call(
        body, name=name,
        out_shape=(pltpu.SemaphoreType.DMA((3,)), pltpu.SemaphoreType.DMA((3,)), pltpu.HBM(full.shape, full.dtype),
                   _S((8, LANE), _F32)),
        in_specs=(_HBM, _SEM) + (_ANY,) * len(after), out_specs=(_SEM, _SEM, _HBM, _VMEM_TOKEN),
        input_output_aliases={0: 2}, compiler_params=_split_params(),
    )(full, recv, *after)


def _gather_wait(name, full, axis, n, send, recv, send2, recv2, after):
    def body(f_ref, send_r, recv_r, send2_r, recv2_r, after_ref, f_out):
        sib, _, me = _peer(_SIBLING)
        blk = _blk(f_ref, axis, n, me)
        for i in range(4):
            _remote(blk, send_r.at[i], recv_r.at[0], sib).wait_send()
        _remote(blk, send_r.at[0], recv_r.at[0], sib).wait_recv()
        for i in range(3):
            cp = _remote(blk, send2_r.at[i], recv2_r.at[i], sib)
            cp.wait_send()
            cp.wait_recv()

    return pl.pallas_call(
        body, name=name, out_shape=pltpu.HBM(full.shape, full.dtype),
        in_specs=(_HBM, _SEM, _SEM, _SEM, _SEM, _ANY), out_specs=_HBM,
        input_output_aliases={0: 0}, compiler_params=_split_params(),
    )(full, send, recv, send2, recv2, after)


def _scatter_start(name, dw, axis, n):
    shard = tuple(n if a == axis else d for a, d in enumerate(dw.shape))
    land = lax.empty((_N_PEER,) + shard, dw.dtype)

    def body(dw_ref, land_ref, send, recv, dw_out, land_out, token):
        for k in range(1, N_DEV):
            dev, to, _ = _peer(k)
            pltpu.make_async_remote_copy(
                src_ref=_blk(dw_ref, axis, n, to), dst_ref=land_ref.at[k - 1], send_sem=send.at[k - 1],
                recv_sem=recv.at[k - 1], device_id=dev, device_id_type=pl.DeviceIdType.MESH).start()
        token[...] = jnp.zeros_like(token)

    return pl.pallas_call(
        body, name=name,
        out_shape=(pltpu.SemaphoreType.DMA((_N_PEER,)), pltpu.SemaphoreType.DMA((_N_PEER,)),
                   pltpu.HBM(dw.shape, dw.dtype), pltpu.HBM(land.shape, land.dtype), _S((8, LANE), _F32)),
        in_specs=(_HBM, _HBM), out_specs=(_SEM, _SEM, _HBM, _HBM, pl.BlockSpec(memory_space=pltpu.VMEM)),
        input_output_aliases={0: 2, 1: 3}, compiler_params=_split_params(),
    )(pltpu.with_memory_space_constraint(dw, pltpu.HBM), pltpu.with_memory_space_constraint(land, pltpu.HBM))


def _scatter_wait(name, items, after):
    ne = len(items)
    after = tuple(after) if isinstance(after, (tuple, list)) else (after,)

    def body(*refs):
        for e, (_, _, _, _, axis, n) in enumerate(items):
            dw_ref, land_ref, send_r, recv_r = refs[4 * e:4 * e + 4]
            for k in range(1, N_DEV):
                dev, to, _ = _peer(k)
                cp = pltpu.make_async_remote_copy(
                    src_ref=_blk(dw_ref, axis, n, to), dst_ref=land_ref.at[k - 1], send_sem=send_r.at[k - 1],
                    recv_sem=recv_r.at[k - 1], device_id=dev, device_id_type=pl.DeviceIdType.MESH)
                cp.wait_send()
                cp.wait_recv()

    args, out_shape = [], []
    for dw, land, send, recv, _, _ in items:
        args += [dw, land, send, recv]
        out_shape += [pltpu.HBM(dw.shape, dw.dtype), pltpu.HBM(land.shape, land.dtype)]
    res = pl.pallas_call(
        body, name=name, out_shape=tuple(out_shape),
        in_specs=(_HBM, _HBM, _SEM, _SEM) * ne + (_ANY,) * len(after), out_specs=(_HBM,) * (2 * ne),
        input_output_aliases={4 * e + j: 2 * e + j for e in range(ne) for j in range(2)},
        compiler_params=_split_params(),
    )(*args, *after)
    return [(res[2 * e], res[2 * e + 1]) for e in range(ne)]


def _adamw_big(me1, dw, land, w, m, v, axis, n, name, layer=None, into=None):
    R, C = land.shape[1:]
    tr = _row_tile(R, 128)
    nb = R // tr
    lead = () if layer is None else (None,)
    pre = () if layer is None else (layer,)

    def body(me_ref, own_ref, land_ref, w_ref, m_ref, v_ref, *rest):
        g_ref, d_ref, nm_ref, nv_ref = rest[-4:]
        g = own_ref[...].astype(_F32)
        for s in range(_N_PEER):
            g = g + land_ref[s].astype(_F32)
        nm = ADAM_B1 * m_ref[...] + (1.0 - ADAM_B1) * g
        nv = ADAM_B2 * v_ref[...] + (1.0 - ADAM_B2) * (g * g)
        m_hat = nm / (1.0 - ADAM_B1 ** ADAM_STEP)
        v_hat = nv / (1.0 - ADAM_B2 ** ADAM_STEP)
        g_ref[...] = g
        d_ref[...] = -ADAM_LR * (m_hat / (jnp.sqrt(v_hat) + ADAM_EPS) + ADAM_WD * w_ref[...])
        nm_ref[...] = nm
        nv_ref[...] = nv

    if axis == 1:
        own_spec = pl.BlockSpec((tr, C), lambda i, me: (i, me[0]))
    else:
        own_spec = pl.BlockSpec((tr, C), lambda i, me: (me[0] * nb + i, 0))
    blk = pl.BlockSpec(lead + (tr, C), lambda i, me: pre + (i, 0))
    in_specs = [own_spec, pl.BlockSpec((_N_PEER, tr, C), lambda i, me: (0, i, 0)), blk, blk, blk]
    args = [me1, dw, land, w, m, v]
    aliases = {}
    if into is not None:
        in_specs += [_ANY] * 4
        aliases = {6 + j: j for j in range(4)}
        args += list(into)
    return pl.pallas_call(
        body, name=name,
        grid_spec=pltpu.PrefetchScalarGridSpec(num_scalar_prefetch=1, grid=(nb,), in_specs=in_specs, out_specs=[blk] * 4),
        out_shape=[_S(w.shape, _F32)] * 4, input_output_aliases=aliases, compiler_params=_params(("parallel",)),
    )(*args)


def _pack(arrs):
    flat = jnp.concatenate([a.reshape(-1).astype(_F32) for a in arrs])
    pad = (-flat.shape[0]) % (LANE * LANE)
    return jnp.pad(flat, (0, pad)).reshape(-1, LANE)


def _unpack(mat, shapes):
    flat = mat.reshape(-1)
    out, off = [], 0
    for s in shapes:
        n = 1
        for d in s:
            n *= d
        out.append(flat[off:off + n].reshape(s))
        off += n
    return out


def _lb_of(lb_param):
    lb_all = jnp.cumsum(jax.nn.softmax(lb_param.astype(_F32), axis=0), axis=0)
    return (lb_all - lb_all[0])[1:2]


def kernel(x, ev_w_in, ev_ln_v_g, ev_ln_v_b, ev_w_s, ev_b_s, ev_w_pool, ev_pool_scale, ev_w_out, od_w_in, od_norm_g, od_w_out, lb_param, ffn_w_up, ffn_conv_w, ffn_conv_b, ffn_w_down, ln1_g, ln1_b, ln2_g, ln2_b, loss_target, m_ev_w_in, m_ev_ln_v_g, m_ev_ln_v_b, m_ev_w_s, m_ev_b_s, m_ev_w_pool, m_ev_pool_scale, m_ev_w_out, m_od_w_in, m_od_norm_g, m_od_w_out, m_lb_param, m_ffn_w_up, m_ffn_conv_w, m_ffn_conv_b, m_ffn_w_down, m_ln1_g, m_ln1_b, m_ln2_g, m_ln2_b, v_ev_w_in, v_ev_ln_v_g, v_ev_ln_v_b, v_ev_w_s, v_ev_b_s, v_ev_w_pool, v_ev_pool_scale, v_ev_w_out, v_od_w_in, v_od_norm_g, v_od_w_out, v_lb_param, v_ffn_w_up, v_ffn_conv_w, v_ffn_conv_b, v_ffn_w_down, v_ln1_g, v_ln1_b, v_ln2_g, v_ln2_b):
    me = 4 * lax.axis_index("x") + 2 * lax.axis_index("y") + lax.axis_index("c")
    T, D = x.shape[1], x.shape[2]
    W = ev_ln_v_g.shape[1]
    H = W // A_HEAD
    Wg = W // B_GROUPS
    F2 = ffn_conv_b.shape[1]
    F = F2 // 2
    n_in0, n_out0 = ev_w_in.shape[2], ev_w_out.shape[1]
    n_in1, n_out1 = od_w_in.shape[2], od_w_out.shape[1]
    n_up, n_dn = ffn_w_up.shape[2], ffn_w_down.shape[1]
    n_pool, n_ng, n_cw = ev_w_pool.shape[2], od_norm_g.shape[1], ffn_conv_w.shape[2]

    small_shards = [od_norm_g, ffn_conv_w, ev_w_pool]
    small_pack = _pack(small_shards)
    small_all = _exchange("gather_small_params", [small_pack], [_S((N_DEV,) + small_pack.shape, _F32)], [_slot_job(0, 0)])[0]

    me1 = me.astype(jnp.int32).reshape(1)
    weights = [
        ("w_in0", ev_w_in[0], None, 1, n_in0), ("w_out0", ev_w_out[0], None, 0, n_out0),
        ("w_up0", ffn_w_up, 0, 1, n_up), ("w_dn0", ffn_w_down, 0, 0, n_dn),
        ("w_in1", od_w_in[0], None, 1, n_in1), ("w_out1", od_w_out[0], None, 0, n_out1),
        ("w_up1", ffn_w_up, 1, 1, n_up), ("w_dn1", ffn_w_down, 1, 0, n_dn),
    ]
    started, tokens = {}, [small_all]
    for key, w, layer, axis, n in weights:
        full = _place(w, me1, axis, "place_" + key, layer, deps=tokens)
        send, recv, full, token = _gather_start("gather_start_" + key, full, axis, n)
        started[key] = (full, axis, n, send, recv)
        tokens = [token]

    def pass_on(key, after):
        full, axis, n, send, recv = started[key]
        send2, recv2, full, token = _gather_forward("gather_forward_" + key, full, axis, n, recv, after)
        started[key] = (full, axis, n, send, recv, send2, recv2)
        return token

    def gathered(key, after):
        return _gather_wait("gather_wait_" + key, *started[key], after)

    ng_parts, cw_parts, wp_parts = [], [], []
    for j in range(N_DEV):
        a, b, c = _unpack(small_all[j], [s.shape for s in small_shards])
        ng_parts.append(a)
        cw_parts.append(b)
        wp_parts.append(c)
    norm_g = jnp.concatenate(ng_parts, axis=1)
    conv_w = jnp.concatenate(cw_parts, axis=2)
    w_pool = jnp.concatenate(wp_parts, axis=2)[0]
    cw_l = [conv_w[l].reshape(3, 2, F).transpose(1, 0, 2) for l in range(DEPTH)]
    cb_l = [ffn_conv_b[l].reshape(2, 1, F) for l in range(DEPTH)]
    ws_tril = jnp.tril(ev_w_s[0]).astype(_MM)
    bias = jnp.repeat(ev_b_s[0].T, A_HEAD, axis=1)
    wp_b = w_pool.astype(_MM)
    lb, lb_vjp = jax.vjp(_lb_of, lb_param)

    small_names = ["ev_ln_v_g", "ev_ln_v_b", "ev_w_s", "ev_b_s", "ev_w_pool", "ev_pool_scale", "od_norm_g", "lb_param",
                   "ffn_conv_w", "ffn_conv_b", "ln1_g", "ln1_b", "ln2_g", "ln2_b"]
    given = dict(ev_ln_v_g=(ev_ln_v_g, m_ev_ln_v_g, v_ev_ln_v_g), ev_ln_v_b=(ev_ln_v_b, m_ev_ln_v_b, v_ev_ln_v_b),
                 ev_w_s=(ev_w_s, m_ev_w_s, v_ev_w_s), ev_b_s=(ev_b_s, m_ev_b_s, v_ev_b_s),
                 ev_w_pool=(ev_w_pool, m_ev_w_pool, v_ev_w_pool),
                 ev_pool_scale=(ev_pool_scale, m_ev_pool_scale, v_ev_pool_scale),
                 od_norm_g=(od_norm_g, m_od_norm_g, v_od_norm_g), lb_param=(lb_param, m_lb_param, v_lb_param),
                 ffn_conv_w=(ffn_conv_w, m_ffn_conv_w, v_ffn_conv_w), ffn_conv_b=(ffn_conv_b, m_ffn_conv_b, v_ffn_conv_b),
                 ln1_g=(ln1_g, m_ln1_g, v_ln1_g), ln1_b=(ln1_b, m_ln1_b, v_ln1_b), ln2_g=(ln2_g, m_ln2_g, v_ln2_g),
                 ln2_b=(ln2_b, m_ln2_b, v_ln2_b))
    shard_axis = dict(ev_w_pool=2, od_norm_g=1, ffn_conv_w=2)
    rep_names = [n for n in small_names if n not in shard_axis]
    shd_names = [n for n in small_names if n in shard_axis]
    small_packs = [_pack([given[n][j] for n in small_names]) for j in range(3)]

    x2 = x[0]
    xb = _cast(x2, _MM, "cast_x", deps=[pass_on("w_in0", tokens[0])])
    w_in0 = gathered("w_in0", xb)
    h0 = _mm(xb, w_in0, "nn", _F32, "ev_in", out_parts=3)
    tie = pass_on("w_out0", h0)
    yab = _ev_mid_fwd(h0, ev_ln_v_g + tie[0, 0], ev_ln_v_b, ws_tril, bias, wp_b, ev_pool_scale, "ev_mid_fwd")
    w_out0 = gathered("w_out0", yab)
    z1 = _mm(yab, w_out0, "nn", _F32, "ev_out", add=x2, add_scale=ALPHA)
    tie = pass_on("w_up0", (z1, *small_packs))
    x1, x1b = _ln_fwd(z1, ln1_g[0:1] + tie[0, 0], ln1_b[0:1], "ln1_0")
    w_up0 = gathered("w_up0", x1b)
    hf0 = _mm(x1b, w_up0, "nn", _F32, "ffn_up", out_parts=2)
    tie = pass_on("w_dn0", hf0)
    act0, hc0 = _ffn_mid_fwd(hf0, cw_l[0], cb_l[0] + tie[0, 0], "ffn_mid_fwd")
    w_dn0 = gathered("w_dn0", act0)
    z2 = _mm(act0, w_dn0, "nn", _F32, "ffn_down", add=x1, add_scale=ALPHA)
    tie = pass_on("w_in1", z2)
    x2_, x2b = _ln_fwd(z2, ln2_g[0:1] + tie[0, 0], ln2_b[0:1], "ln2_0")
    w_in1 = gathered("w_in1", x2b)
    h1 = _mm(x2b, w_in1, "nn", _F32, "od_in", out_parts=4)
    qd, kd, ke, vb, dec = _hgrn_prep_fwd(h1, lb, "hgrn_prep_fwd")
    tie = pass_on("w_out1", qd)
    o, yo, st = _hgrn_scan_fwd(qd, kd, ke, vb, dec, h1, norm_g + tie[0, 0], "hgrn_scan_fwd")
    w_out1 = gathered("w_out1", yo)
    z3 = _mm(yo, w_out1, "nn", _F32, "od_out", add=x2_, add_scale=ALPHA)
    tie = pass_on("w_up1", z3)
    x3, x3b = _ln_fwd(z3, ln1_g[1:2] + tie[0, 0], ln1_b[1:2], "ln1_1")
    w_up1 = gathered("w_up1", x3b)
    hf1 = _mm(x3b, w_up1, "nn", _F32, "ffn_up", out_parts=2)
    tie = pass_on("w_dn1", hf1)
    act1, hc1 = _ffn_mid_fwd(hf1, cw_l[1], cb_l[1] + tie[0, 0], "ffn_mid_fwd")
    w_dn1 = gathered("w_dn1", act1)
    z4 = _mm(act1, w_dn1, "nn", _F32, "ffn_down", add=x3, add_scale=ALPHA)

    scat = {}

    def scatter(key, dw, axis, n):
        send, recv, dw, land, token = _scatter_start("scatter_start_" + key, dw, axis, n)
        scat[key] = (dw, land, send, recv, axis, n)
        return [token]

    loss11, dz4, dz4b, g_ln2_1, b_ln2_1 = _ln_loss_bwd(z4, ln2_g[1:2], ln2_b[1:2], loss_target[0], "ln_loss_bwd")
    tok = scatter("dn1", _mm(act1, dz4b, "tn", _XCH, "ffn_down_dw"), 0, n_dn)
    dact1 = _mm(dz4b, w_dn1, "nt", _MM, "ffn_down_dx", deps=tok)
    dhf1, dcw1, dcb1 = _ffn_mid_bwd(hf1, hc1, dact1, cw_l[1], "ffn_mid_bwd")
    tok = scatter("up1", _mm(x3b, dhf1, "tn", _XCH, "ffn_up_dw", b_parts=2, deps=tok), 1, n_up)
    dx3 = _mm(dhf1, w_up1, "nt", _F32, "ffn_up_dx", a_parts=2, add=dz4, add_scale=ALPHA, deps=tok)
    dz3, dz3b, g_ln1_1, b_ln1_1 = _ln_bwd(z3, ln1_g[1:2], dx3, "ln_bwd")
    tok = scatter("out1", _mm(yo, dz3b, "tn", _XCH, "od_out_dw", deps=tok), 0, n_out1)
    dyo = _mm(dz3b, w_out1, "nt", _F32, "od_out_dx", deps=tok)
    dqd, dkd, dke, dv, dgate, ddec, dng = _hgrn_scan_bwd(qd, kd, ke, vb, dec, st, o, h1, norm_g, dyo, "hgrn_scan_bwd")
    dh1, dlb = _hgrn_prep_bwd(h1, lb, dqd, dkd, dke, dv, dgate, ddec, "hgrn_prep_bwd")
    tok = scatter("in1", _mm(x2b, dh1, "tn", _XCH, "od_in_dw", b_parts=4, deps=tok), 1, n_in1)
    dx2 = _mm(dh1, w_in1, "nt", _F32, "od_in_dx", a_parts=4, add=dz3, add_scale=ALPHA, deps=tok)
    dz2, dz2b, g_ln2_0, b_ln2_0 = _ln_bwd(z2, ln2_g[0:1], dx2, "ln_bwd")
    tok = scatter("dn0", _mm(act0, dz2b, "tn", _XCH, "ffn_down_dw", deps=tok), 0, n_dn)
    dact0 = _mm(dz2b, w_dn0, "nt", _MM, "ffn_down_dx", deps=tok)
    dhf0, dcw0, dcb0 = _ffn_mid_bwd(hf0, hc0, dact0, cw_l[0], "ffn_mid_bwd")
    tok = scatter("up0", _mm(x1b, dhf0, "tn", _XCH, "ffn_up_dw", b_parts=2, deps=tok), 1, n_up)
    dx1 = _mm(dhf0, w_up0, "nt", _F32, "ffn_up_dx", a_parts=2, add=dz2, add_scale=ALPHA, deps=tok)
    dz1, dz1b, g_ln1_0, b_ln1_0 = _ln_bwd(z1, ln1_g[0:1], dx1, "ln_bwd")
    tok = scatter("out0", _mm(yab, dz1b, "tn", _XCH, "ev_out_dw", deps=tok), 0, n_out0)
    dyab = _mm(dz1b, w_out0, "nt", _F32, "ev_out_dx", deps=tok)
    dh0, dws, dbias, dlng, dlnb, dwp, dsc = _ev_mid_bwd(h0, dyab, ev_ln_v_g, ev_ln_v_b, ws_tril, bias, wp_b,
                                                        ev_pool_scale, "ev_mid_bwd")

    g_b_s = dbias.reshape(A_CHUNK, H, A_HEAD).sum(axis=-1).T[None]
    g_conv_w = jnp.stack([d.transpose(1, 0, 2).reshape(3, F2) for d in (dcw0, dcw1)])
    g_conv_b = jnp.stack([d.reshape(F2) for d in (dcb0, dcb1)])
    small_grads = dict(zip(small_names, [
        dlng, dlnb, dws[None], g_b_s, dwp[None], dsc, dng, lb_vjp(dlb)[0], g_conv_w, g_conv_b,
        jnp.concatenate([g_ln1_0, g_ln1_1]), jnp.concatenate([b_ln1_0, b_ln1_1]),
        jnp.concatenate([g_ln2_0, g_ln2_1]), jnp.concatenate([b_ln2_0, b_ln2_1])]))

    def by_device(g, ax):
        g = g.reshape(g.shape[:ax] + (N_DEV, g.shape[ax] // N_DEV) + g.shape[ax + 1:])
        return jnp.moveaxis(g, ax, 0).reshape(N_DEV, -1)

    shd = jnp.concatenate([by_device(small_grads[n], shard_axis[n]) for n in shd_names], axis=1)
    shd_pack = jnp.pad(shd, ((0, 0), (0, (-shd.shape[1]) % (LANE * LANE)))).reshape(-1, LANE)
    shd_rows = shd_pack.shape[0] // N_DEV
    rep_pack = _pack([small_grads[n] for n in rep_names])
    tok = scatter("in0", _mm(xb, dh0, "tn", _XCH, "ev_in_dw", b_parts=3, deps=tok), 1, n_in0)
    tok = scatter("small_rep", rep_pack + tok[0][0, 0], None, None)
    tok = scatter("small_shd", shd_pack + tok[0][0, 0], 0, shd_rows)
    grad_x = _mm(dh0, w_in0, "nt", _F32, "ev_in_dx", a_parts=3, add=dz1, add_scale=ALPHA, deps=tok)

    def landed(name, keys, after):
        got = _scatter_wait(name, [scat[k] for k in keys], after)
        return {k: (me1, dw, land) for k, (dw, land) in zip(keys, got)}

    early = landed("scatter_wait_early", ["dn1", "up1", "out1", "in1", "dn0", "up0", "out0"], grad_x)
    big = {}
    r_dn = _adamw_big(*early["dn1"], ffn_w_down, m_ffn_w_down, v_ffn_w_down, 0, n_dn, "adamw_w_dn1", layer=1)
    r_up = _adamw_big(*early["up1"], ffn_w_up, m_ffn_w_up, v_ffn_w_up, 1, n_up, "adamw_w_up1", layer=1)
    big["od_w_out"] = _adamw_big(*early["out1"], od_w_out[0], m_od_w_out[0], v_od_w_out[0], 0, n_out1, "adamw_w_out1")
    big["od_w_in"] = _adamw_big(*early["in1"], od_w_in[0], m_od_w_in[0], v_od_w_in[0], 1, n_in1, "adamw_w_in1")
    big["ffn_w_down"] = _adamw_big(*early["dn0"], ffn_w_down, m_ffn_w_down, v_ffn_w_down, 0, n_dn, "adamw_w_dn0", layer=0, into=r_dn)
    big["ffn_w_up"] = _adamw_big(*early["up0"], ffn_w_up, m_ffn_w_up, v_ffn_w_up, 1, n_up, "adamw_w_up0", layer=0, into=r_up)
    big["ev_w_out"] = _adamw_big(*early["out0"], ev_w_out[0], m_ev_w_out[0], v_ev_w_out[0], 0, n_out0, "adamw_w_out0")
    late = landed("scatter_wait_late", ["in0", "small_rep", "small_shd"],
                  (big["ffn_w_down"][0], big["ffn_w_up"][0], big["od_w_in"][0], big["ev_w_out"][0]))
    big["ev_w_in"] = _adamw_big(*late["in0"], ev_w_in[0], m_ev_w_in[0], v_ev_w_in[0], 1, n_in0, "adamw_w_in0")

    rep_mat = _sum_in_device_order(*late["small_rep"], "sum_small_rep")
    local_g = dict(zip(rep_names, _unpack(rep_mat, [small_grads[n].shape for n in rep_names])))
    _, shd_all, shd_land = late["small_shd"]
    shd_own = lax.dynamic_slice_in_dim(shd_all, me * shd_rows, shd_rows, axis=0)
    shd_mat = _sum_in_device_order(me1, shd_own, shd_land, "sum_small_shd")
    local_g.update(zip(shd_names, _unpack(shd_mat, [given[n][0].shape for n in shd_names])))
    local_shapes = [given[n][0].shape for n in small_names]
    res = _adamw(_pack([local_g[n] for n in small_names])[None], *small_packs, "adamw_small")
    small = {n: [] for n in small_names}
    for r in res:
        for n, a in zip(small_names, _unpack(r, local_shapes)):
            small[n].append(a)

    loss = lax.psum(loss11[0, 0], ("x", "y", "c"))
    order = ["ev_w_in", "ev_ln_v_g", "ev_ln_v_b", "ev_w_s", "ev_b_s", "ev_w_pool", "ev_pool_scale", "ev_w_out", "od_w_in",
             "od_norm_g", "od_w_out", "lb_param", "ffn_w_up", "ffn_conv_w", "ffn_conv_b", "ffn_w_down", "ln1_g", "ln1_b",
             "ln2_g", "ln2_b"]
    shapes = dict(ev_w_in=ev_w_in.shape, ev_w_out=ev_w_out.shape, od_w_in=od_w_in.shape, od_w_out=od_w_out.shape,
                  ffn_w_up=ffn_w_up.shape, ffn_w_down=ffn_w_down.shape)
    outs = [loss, grad_x[None]]
    for kind in range(4):
        for n in order:
            outs.append(big[n][kind].reshape(shapes[n]) if n in big else small[n][kind])
    return tuple(outs)
```

```python
import functools
import math

import jax
import jax.numpy as jnp
from jax import lax
from jax.experimental import pallas as pl
from jax.experimental.pallas import tpu as pltpu

_MM = jnp.bfloat16
_XCH = jnp.bfloat16

DEPTH = 2
ALPHA = (2 * DEPTH) ** 0.25
LN_EPS = 1e-5
A_CHUNK = 128
A_HEAD = 128
B_GROUPS = 4
POOL_HALO = 16
C_CHUNK = 64
C_HEAD = 128
SCAN_UNROLL = 32
CONV_HALO = 8
PACKED_ROWS = 16
FFN_ROWS, FFN_FWD_COLS = 512, 1408
FFN_BWD_ROWS, FFN_BWD_COLS = 512, 512
FFN_CHUNK = 128
ADAM_LR, ADAM_B1, ADAM_B2, ADAM_EPS, ADAM_WD, ADAM_STEP = 0.001, 0.9, 0.999, 1e-08, 0.01, 10
N_DEV = 8
LANE = 128
VMEM_LIMIT = 56 * 1024 * 1024
MM_FULL_K = 2048
MM_FULL_K_TN = 4096
MM_DEEP_K = 2816

_F32 = jnp.float32
_NN = (((1,), (0,)), ((), ()))
_NT = (((1,), (1,)), ((), ()))
_TN = (((0,), (0,)), ((), ()))
_S = jax.ShapeDtypeStruct


def _dot(a, b, dims=_NN):
    return lax.dot_general(a, b, dims, preferred_element_type=_F32)


def _tile(dim, pref):
    best = None
    d = LANE
    while d <= min(dim, pref):
        if dim % d == 0:
            best = d
        d += LANE
    return best if best is not None else dim


def _params(sem):
    return pltpu.CompilerParams(dimension_semantics=sem, vmem_limit_bytes=VMEM_LIMIT)


def _sigmoid(x):
    return 0.5 * jnp.tanh(0.5 * x) + 0.5


def _sigmoid_rel(x):
    return 1.0 / (1.0 + jnp.exp(-x))


_GELU_C = 0.7978845608028654
_GELU_A = 0.044715


def _gelu_and_grad(x):
    t = jnp.tanh(_GELU_C * (x + _GELU_A * x * x * x))
    y = 0.5 * x * (1.0 + t)
    dy = 0.5 * (1.0 + t) + 0.5 * x * (1.0 - t * t) * _GELU_C * (1.0 + 3.0 * _GELU_A * x * x)
    return y, dy


def _row_index(n):
    return lax.broadcasted_iota(jnp.int32, (n, 1), 0)


def _mm_tiles(mode, M, N, K, with_add):
    if mode == "tn":
        return _tile(M, 1024), _tile(N, 1024), _tile(K, MM_FULL_K_TN)
    if K <= MM_FULL_K:
        return _tile(M, 1024 if with_add else 2048), _tile(N, 1024 if mode == "nn" else 512), K
    return _tile(M, 1024), _tile(N, 1024), _tile(K, MM_DEEP_K)


def _mm(a, b, mode, out_dtype, name, *, a_parts=1, b_parts=1, out_parts=1, add=None, add_scale=1.0, deps=(), tiles=None):
    if mode == "nn":
        M, K = a.shape
        N = b.shape[1]
    elif mode == "nt":
        if a_parts > 1:
            M, K = a.shape[1], a.shape[2] * a_parts
        else:
            M, K = a.shape
        N = b.shape[0]
    else:
        K, M = a.shape
        N = b.shape[-1] * b_parts
    tm, tn, tk = tiles if tiles is not None else _mm_tiles(mode, M, N // max(b_parts, out_parts), K // a_parts, add is not None)
    nk = K // tk
    npj = (N // max(b_parts, out_parts)) // tn
    nkp = (K // a_parts) // tk
    if mode == "nn":
        a_spec = pl.BlockSpec((tm, tk), lambda i, j, k: (i, k))
        b_spec = pl.BlockSpec((tk, tn), lambda i, j, k: (k, j))
        dims = _NN
    elif mode == "nt":
        if a_parts > 1:
            a_spec = pl.BlockSpec((None, tm, tk), lambda i, j, k: (k // nkp, i, k % nkp))
        else:
            a_spec = pl.BlockSpec((tm, tk), lambda i, j, k: (i, k))
        b_spec = pl.BlockSpec((tn, tk), lambda i, j, k: (j, k))
        dims = _NT
    else:
        a_spec = pl.BlockSpec((tk, tm), lambda i, j, k: (k, i))
        if b_parts > 1:
            b_spec = pl.BlockSpec((None, tk, tn), lambda i, j, k: (j // npj, k, j % npj))
        else:
            b_spec = pl.BlockSpec((tk, tn), lambda i, j, k: (k, j))
        dims = _TN
    if out_parts > 1:
        out_spec = pl.BlockSpec((None, tm, tn), lambda i, j, k: (j // npj, i, j % npj))
        out_shape = _S((out_parts, M, N // out_parts), out_dtype)
    else:
        out_spec = pl.BlockSpec((tm, tn), lambda i, j, k: (i, j))
        out_shape = _S((M, N), out_dtype)
    in_specs = [a_spec, b_spec]
    args = [a, b]
    if add is not None:
        in_specs.append(pl.BlockSpec((tm, tn), lambda i, j, k: (i, j)))
        args.append(add)
    in_specs += [_ANY] * len(deps)
    args += list(deps)

    def finish(r, refs, o_ref):
        if add is not None:
            r = r + add_scale * refs[2][...]
        o_ref[...] = r.astype(o_ref.dtype)

    def body_one(*refs):
        finish(_dot(refs[0][...], refs[1][...], dims), refs, refs[-1])

    def body_acc(*refs):
        o_ref, acc = refs[-2], refs[-1]
        k = pl.program_id(2)

        @pl.when(k == 0)
        def _():
            acc[...] = jnp.zeros_like(acc)

        acc[...] += _dot(refs[0][...], refs[1][...], dims)

        @pl.when(k == nk - 1)
        def _():
            finish(acc[...], refs, o_ref)

    return pl.pallas_call(
        body_one if nk == 1 else body_acc, name=name, grid=(M // tm, N // tn, nk), in_specs=in_specs,
        out_specs=out_spec, out_shape=out_shape,
        scratch_shapes=[] if nk == 1 else [pltpu.VMEM((tm, tn), _F32)],
        compiler_params=_params(("parallel", "parallel", "arbitrary")),
    )(*args)


def _cast(x2d, dtype, name, deps=()):
    R, C = x2d.shape
    tr = _tile(R, 512) if R % LANE == 0 else R

    def body(x_ref, *rest):
        rest[-1][...] = x_ref[...].astype(rest[-1].dtype)

    return pl.pallas_call(
        body, name=name, grid=(R // tr,), in_specs=[pl.BlockSpec((tr, C), lambda i: (i, 0))] + [_ANY] * len(deps),
        out_specs=pl.BlockSpec((tr, C), lambda i: (i, 0)), out_shape=_S((R, C), dtype),
        compiler_params=_params(("parallel",)),
    )(x2d, *deps)


def _ln_fwd(z, g, b, name):
    T, D = z.shape
    tr = _tile(T, 256)

    def body(z_ref, g_ref, b_ref, y_ref, yb_ref):
        zz = z_ref[...]
        mu = jnp.mean(zz, axis=-1, keepdims=True)
        zc = zz - mu
        var = jnp.mean(zc * zc, axis=-1, keepdims=True)
        y = zc * lax.rsqrt(var + LN_EPS) * g_ref[...] + b_ref[...]
        y_ref[...] = y
        yb_ref[...] = y.astype(yb_ref.dtype)

    row = pl.BlockSpec((tr, D), lambda i: (i, 0))
    vec = pl.BlockSpec((1, D), lambda i: (0, 0))
    return pl.pallas_call(
        body, name=name, grid=(T // tr,), in_specs=[row, vec, vec], out_specs=[row, row],
        out_shape=[_S((T, D), _F32), _S((T, D), _MM)], compiler_params=_params(("parallel",)),
    )(z, g, b)


def _ln_bwd(z, g, dy, name):
    T, D = z.shape
    tr = _tile(T, 256)

    def body(z_ref, g_ref, dy_ref, dz_ref, dzb_ref, dg_ref, db_ref):
        @pl.when(pl.program_id(0) == 0)
        def _():
            dg_ref[...] = jnp.zeros_like(dg_ref)
            db_ref[...] = jnp.zeros_like(db_ref)

        zz = z_ref[...]
        mu = jnp.mean(zz, axis=-1, keepdims=True)
        zc = zz - mu
        rstd = lax.rsqrt(jnp.mean(zc * zc, axis=-1, keepdims=True) + LN_EPS)
        xh = zc * rstd
        d = dy_ref[...]
        dg_ref[...] += jnp.sum(d * xh, axis=0, keepdims=True)
        db_ref[...] += jnp.sum(d, axis=0, keepdims=True)
        dxh = d * g_ref[...]
        dz = rstd * (dxh - jnp.mean(dxh, axis=-1, keepdims=True) - xh * jnp.mean(dxh * xh, axis=-1, keepdims=True))
        dz_ref[...] = dz
        dzb_ref[...] = dz.astype(dzb_ref.dtype)

    row = pl.BlockSpec((tr, D), lambda i: (i, 0))
    vec = pl.BlockSpec((1, D), lambda i: (0, 0))
    return pl.pallas_call(
        body, name=name, grid=(T // tr,), in_specs=[row, vec, row], out_specs=[row, row, vec, vec],
        out_shape=[_S((T, D), _F32), _S((T, D), _MM), _S((1, D), _F32), _S((1, D), _F32)],
        compiler_params=_params(("arbitrary",)),
    )(z, g, dy)


def _ln_loss_bwd(z, g, b, target, name):
    T, D = z.shape
    tr = _tile(T, 256)

    def body(z_ref, g_ref, b_ref, t_ref, loss_ref, dz_ref, dzb_ref, dg_ref, db_ref, lacc):
        i = pl.program_id(0)

        @pl.when(i == 0)
        def _():
            dg_ref[...] = jnp.zeros_like(dg_ref)
            db_ref[...] = jnp.zeros_like(db_ref)
            lacc[...] = jnp.zeros_like(lacc)

        zz = z_ref[...]
        mu = jnp.mean(zz, axis=-1, keepdims=True)
        zc = zz - mu
        rstd = lax.rsqrt(jnp.mean(zc * zc, axis=-1, keepdims=True) + LN_EPS)
        xh = zc * rstd
        err = xh * g_ref[...] + b_ref[...] - t_ref[...]
        lacc[...] += jnp.sum(err * err, axis=0, keepdims=True)
        d = err * (1.0 / D)
        dg_ref[...] += jnp.sum(d * xh, axis=0, keepdims=True)
        db_ref[...] += jnp.sum(d, axis=0, keepdims=True)
        dxh = d * g_ref[...]
        dz = rstd * (dxh - jnp.mean(dxh, axis=-1, keepdims=True) - xh * jnp.mean(dxh * xh, axis=-1, keepdims=True))
        dz_ref[...] = dz
        dzb_ref[...] = dz.astype(dzb_ref.dtype)

        @pl.when(i == pl.num_programs(0) - 1)
        def _():
            loss_ref[...] = jnp.sum(lacc[...], axis=-1, keepdims=True) * (0.5 / D)

    row = pl.BlockSpec((tr, D), lambda i: (i, 0))
    vec = pl.BlockSpec((1, D), lambda i: (0, 0))
    one = pl.BlockSpec((1, 1), lambda i: (0, 0))
    return pl.pallas_call(
        body, name=name, grid=(T // tr,), in_specs=[row, vec, vec, row], out_specs=[one, row, row, vec, vec],
        out_shape=[_S((1, 1), _F32), _S((T, D), _F32), _S((T, D), _MM), _S((1, D), _F32), _S((1, D), _F32)],
        scratch_shapes=[pltpu.VMEM((1, D), _F32)], compiler_params=_params(("arbitrary",)),
    )(z, g, b, target)


def _conv3(X, cw, cb):
    return cb + cw[2:3] * X + cw[1:2] * pltpu.roll(X, 1, 0) + cw[0:1] * pltpu.roll(X, 2, 0)


def _ffn_mid_fwd(h, cw, cb, name):
    _, T, F = h.shape
    tr = _tile(T, FFN_ROWS)
    tc = _tile(F, FFN_FWD_COLS)
    nb = tr // CONV_HALO

    rc = _tile(tr, FFN_CHUNK)
    lanes = [slice(cs * LANE, (cs + 1) * LANE) for cs in range(tc // LANE)]

    def body(h_ref, p_ref, cw_ref, cb_ref, o_ref, c_ref):
        i = pl.program_id(0)

        def work(r0, cols, X):
            hc = [_conv3(X[part], cw_ref[part, :, cols], cb_ref[part, :, cols])[CONV_HALO:] for part in range(2)]
            for part in range(2):
                c_ref[part, pl.ds(r0, rc), cols] = hc[part].astype(c_ref.dtype)
            a, v = hc
            o_ref[pl.ds(r0, rc), cols] = (a * _sigmoid(a) * v).astype(o_ref.dtype)

        for cols in lanes:
            work(0, cols, [jnp.concatenate([jnp.where(i == 0, 0.0, p_ref[part, :, cols]), h_ref[part, 0:rc, cols]], axis=0)
                           for part in range(2)])

        def chunk(c, carry):
            r0 = pl.multiple_of(c * rc, rc)
            for cols in lanes:
                work(r0, cols, [h_ref[part, pl.ds(r0 - CONV_HALO, rc + CONV_HALO), cols] for part in range(2)])
            return carry

        lax.fori_loop(1, tr // rc, chunk, 0)

    return pl.pallas_call(
        body, name=name, grid=(T // tr, F // tc),
        in_specs=[pl.BlockSpec((2, tr, tc), lambda i, j: (0, i, j)),
                  pl.BlockSpec((2, CONV_HALO, tc), lambda i, j: (0, jnp.maximum(i * nb - 1, 0), j)),
                  pl.BlockSpec((2, 3, tc), lambda i, j: (0, 0, j)),
                  pl.BlockSpec((2, 1, tc), lambda i, j: (0, 0, j))],
        out_specs=[pl.BlockSpec((tr, tc), lambda i, j: (i, j)), pl.BlockSpec((2, tr, tc), lambda i, j: (0, i, j))],
        out_shape=[_S((T, F), _MM), _S((2, T, F), _MM)],
        compiler_params=_params(("parallel", "parallel")),
    )(h, h, cw, cb)


def _ffn_mid_bwd(h, hc, dact, cw, name):
    _, T, F = h.shape
    tr = _tile(T, FFN_BWD_ROWS)
    tc = _tile(F, FFN_BWD_COLS)
    nb_c = tr // PACKED_ROWS
    rc = _tile(tr, FFN_CHUNK)
    n = rc + CONV_HALO

    def body(h_ref, c_ref, cn_ref, d_ref, dn_ref, cw_ref, dh_ref, dcw_ref, dcb_ref):
        i = pl.program_id(1)
        is_last = i == pl.num_programs(1) - 1

        @pl.when(i == 0)
        def _():
            dcw_ref[...] = jnp.zeros_like(dcw_ref)
            dcb_ref[...] = jnp.zeros_like(dcb_ref)

        def work(r0, cols, a, v, D):
            sg = _sigmoid(a)
            dhc = [D * v * sg * (1.0 + a * (1.0 - sg)), D * a * sg]
            for part in range(2):
                X = h_ref[part, pl.ds(r0, rc), cols]
                cwp = cw_ref[part, :, cols]
                dh = None
                for k in range(3):
                    g = (dhc[part] if k == 0 else pltpu.roll(dhc[part], n - k, 0))[0:rc]
                    term = cwp[2 - k:3 - k] * g
                    dh = term if dh is None else dh + term
                    dcw_ref[part, 2 - k:3 - k, cols] += jnp.sum(g * X, axis=0, keepdims=True)
                    if k == 0:
                        dcb_ref[part, :, cols] += jnp.sum(g, axis=0, keepdims=True)
                dh_ref[part, pl.ds(r0, rc), cols] = dh.astype(dh_ref.dtype)

        lanes = [slice(cs * LANE, (cs + 1) * LANE) for cs in range(tc // LANE)]

        def chunk(c, carry):
            r0 = pl.multiple_of(c * rc, rc)
            for cols in lanes:
                a, v = [c_ref[part, pl.ds(r0, rc + PACKED_ROWS), cols].astype(_F32)[0:n] for part in range(2)]
                work(r0, cols, a, v, d_ref[pl.ds(r0, rc + PACKED_ROWS), cols].astype(_F32)[0:n])
            return carry

        lax.fori_loop(0, tr // rc - 1, chunk, 0)
        r0 = tr - rc
        for cols in lanes:
            a, v = [jnp.concatenate([c_ref[part, r0:tr, cols].astype(_F32), cn_ref[part, :, cols].astype(_F32)[0:CONV_HALO]],
                                    axis=0) for part in range(2)]
            D = jnp.concatenate([d_ref[r0:tr, cols].astype(_F32),
                                 jnp.where(is_last, 0.0, dn_ref[:, cols].astype(_F32)[0:CONV_HALO])], axis=0)
            work(r0, cols, a, v, D)

    return pl.pallas_call(
        body, name=name, grid=(F // tc, T // tr),
        in_specs=[pl.BlockSpec((2, tr, tc), lambda j, i: (0, i, j)),
                  pl.BlockSpec((2, tr, tc), lambda j, i: (0, i, j)),
                  pl.BlockSpec((2, PACKED_ROWS, tc), lambda j, i: (0, jnp.minimum((i + 1) * nb_c, T // PACKED_ROWS - 1), j)),
                  pl.BlockSpec((tr, tc), lambda j, i: (i, j)),
                  pl.BlockSpec((PACKED_ROWS, tc), lambda j, i: (jnp.minimum((i + 1) * nb_c, T // PACKED_ROWS - 1), j)),
                  pl.BlockSpec((2, 3, tc), lambda j, i: (0, 0, j))],
        out_specs=[pl.BlockSpec((2, tr, tc), lambda j, i: (0, i, j)),
                   pl.BlockSpec((2, 3, tc), lambda j, i: (0, 0, j)),
                   pl.BlockSpec((2, 1, tc), lambda j, i: (0, 0, j))],
        out_shape=[_S((2, T, F), _MM), _S((2, 3, F), _F32), _S((2, 1, F), _F32)],
        compiler_params=_params(("parallel", "arbitrary")),
    )(h, hc, hc, dact, dact, cw)


def _ev_common(h_ref, hp_ref, lng_ref, lnb_ref, ws_ref, bias_ref, i, tr, W):
    H = W // A_HEAD
    u, gu = _gelu_and_grad(h_ref[0])
    v, gv = _gelu_and_grad(h_ref[1])
    mu = jnp.mean(v, axis=-1, keepdims=True)
    vc = v - mu
    rstd = lax.rsqrt(jnp.mean(vc * vc, axis=-1, keepdims=True) + LN_EPS)
    vhat = vc * rstd
    vb = (vhat * lng_ref[...] + lnb_ref[...]).astype(_MM)
    s_chunks = []
    for c in range(tr // A_CHUNK):
        r0 = c * A_CHUNK
        heads = [_dot(ws_ref[hd], vb[r0:r0 + A_CHUNK, hd * A_HEAD:(hd + 1) * A_HEAD]) for hd in range(H)]
        s_chunks.append(jnp.concatenate(heads, axis=1) + bias_ref[...])
    prev = jnp.where(i == 0, 0.0, hp_ref[...])
    X = jnp.concatenate([prev, h_ref[2]], axis=0)
    return u, gu, gv, rstd, vhat, vb, s_chunks, X


def _pool_inv_count(i, tr, rows, win):
    pos = i * tr + _row_index(rows) + 1
    return 1.0 / jnp.minimum(pos, win).astype(_F32)


def _pool_fwd(X, g, Wg, i, tr):
    xg = X[:, g * Wg:(g + 1) * Wg]
    s = xg
    for k in range(g + 1):
        s = s + pltpu.roll(s, 2 ** k, 0)
    return s[POOL_HALO:] * _pool_inv_count(i, tr, tr, 2 ** (g + 1)) - xg[POOL_HALO:]


def _ev_mid_fwd(h, lng, lnb, ws, bias, wp, sc, name):
    _, T, W = h.shape
    tr = _tile(T, 256)
    H = W // A_HEAD
    Wg = W // B_GROUPS
    nb = tr // POOL_HALO

    def body(h_ref, hp_ref, lng_ref, lnb_ref, ws_ref, bias_ref, wp_ref, sc_ref, o_ref):
        i = pl.program_id(0)
        u, _, _, _, _, _, s_chunks, X = _ev_common(h_ref, hp_ref, lng_ref, lnb_ref, ws_ref, bias_ref, i, tr, W)
        for c, s in enumerate(s_chunks):
            r0 = c * A_CHUNK
            o_ref[r0:r0 + A_CHUNK, 0:W] = (u[r0:r0 + A_CHUNK] * s).astype(o_ref.dtype)
        for g in range(B_GROUPS):
            p = _pool_fwd(X, g, Wg, i, tr)
            y = _dot(p.astype(_MM), wp_ref[g]) * sc_ref[:, g * Wg:(g + 1) * Wg]
            o_ref[:, W + g * Wg:W + (g + 1) * Wg] = y.astype(o_ref.dtype)

    vec = pl.BlockSpec((1, W), lambda i: (0, 0))
    return pl.pallas_call(
        body, name=name, grid=(T // tr,),
        in_specs=[pl.BlockSpec((3, tr, W), lambda i: (0, i, 0)),
                  pl.BlockSpec((None, POOL_HALO, W), lambda i: (2, jnp.maximum(i * nb - 1, 0), 0)),
                  vec, vec,
                  pl.BlockSpec((H, A_CHUNK, A_CHUNK), lambda i: (0, 0, 0)),
                  pl.BlockSpec((A_CHUNK, W), lambda i: (0, 0)),
                  pl.BlockSpec((B_GROUPS, Wg, Wg), lambda i: (0, 0, 0)),
                  vec],
        out_specs=pl.BlockSpec((tr, 2 * W), lambda i: (i, 0)), out_shape=_S((T, 2 * W), _MM),
        compiler_params=_params(("parallel",)),
    )(h, h, lng, lnb, ws, bias, wp, sc)


def _ev_mid_bwd(h, dy, lng, lnb, ws, bias, wp, sc, name):
    _, T, W = h.shape
    tr = _tile(T, 256)
    H = W // A_HEAD
    Wg = W // B_GROUPS
    nb = tr // POOL_HALO
    last_blk = T // POOL_HALO - 1
    n = tr + POOL_HALO

    def body(h_ref, hp_ref, dy_ref, dyn_ref, lng_ref, lnb_ref, ws_ref, bias_ref, wp_ref, sc_ref,
             dh_ref, dws_ref, dbias_ref, dlng_ref, dlnb_ref, dwp_ref, dsc_ref):
        i = pl.program_id(0)

        @pl.when(i == 0)
        def _():
            for r in (dws_ref, dbias_ref, dlng_ref, dlnb_ref, dwp_ref, dsc_ref):
                r[...] = jnp.zeros_like(r)

        u, gu, gv, rstd, vhat, vb, s_chunks, X = _ev_common(h_ref, hp_ref, lng_ref, lnb_ref, ws_ref, bias_ref, i, tr, W)
        rr = lax.broadcasted_iota(jnp.int32, (A_CHUNK, A_CHUNK), 0)
        cc = lax.broadcasted_iota(jnp.int32, (A_CHUNK, A_CHUNK), 1)
        tril = rr >= cc
        du_chunks, dvln_chunks = [], []
        for c, s in enumerate(s_chunks):
            r0 = c * A_CHUNK
            dya = dy_ref[r0:r0 + A_CHUNK, 0:W]
            du_chunks.append(dya * s)
            ds = dya * u[r0:r0 + A_CHUNK]
            dbias_ref[...] += ds
            dsb = ds.astype(_MM)
            heads = []
            for hd in range(H):
                cols = slice(hd * A_HEAD, (hd + 1) * A_HEAD)
                dws_ref[hd] += jnp.where(tril, _dot(dsb[:, cols], vb[r0:r0 + A_CHUNK, cols], _NT), 0.0)
                heads.append(_dot(ws_ref[hd], dsb[:, cols], _TN))
            dvln_chunks.append(jnp.concatenate(heads, axis=1))
        du = jnp.concatenate(du_chunks, axis=0)
        dvln = jnp.concatenate(dvln_chunks, axis=0)
        dlng_ref[...] += jnp.sum(dvln * vhat, axis=0, keepdims=True)
        dlnb_ref[...] += jnp.sum(dvln, axis=0, keepdims=True)
        dxh = dvln * lng_ref[...]
        dv = rstd * (dxh - jnp.mean(dxh, axis=-1, keepdims=True) - vhat * jnp.mean(dxh * vhat, axis=-1, keepdims=True))
        dh_ref[0] = (du * gu).astype(dh_ref.dtype)
        dh_ref[1] = (dv * gv).astype(dh_ref.dtype)

        dyb = dy_ref[:, W:2 * W]
        dyb_full = jnp.concatenate([dyb, jnp.where(i == pl.num_programs(0) - 1, 0.0, dyn_ref[...])], axis=0)
        for g in range(B_GROUPS):
            cols = slice(g * Wg, (g + 1) * Wg)
            pb = _pool_fwd(X, g, Wg, i, tr).astype(_MM)
            ypre = _dot(pb, wp_ref[g])
            dsc_ref[:, cols] += jnp.sum(dyb[:, cols] * ypre, axis=0, keepdims=True)
            dyp = (dyb_full[:, cols] * sc_ref[:, cols]).astype(_MM)
            dwp_ref[g] += _dot(pb, dyp[0:tr], _TN)
            dp = _dot(dyp, wp_ref[g], _NT)
            s = dp * _pool_inv_count(i, tr, n, 2 ** (g + 1))
            for k in range(g + 1):
                s = s + pltpu.roll(s, n - 2 ** k, 0)
            dh_ref[2, :, cols] = (s[0:tr] - dp[0:tr]).astype(dh_ref.dtype)

    vec = pl.BlockSpec((1, W), lambda i: (0, 0))
    ws_spec = pl.BlockSpec((H, A_CHUNK, A_CHUNK), lambda i: (0, 0, 0))
    bias_spec = pl.BlockSpec((A_CHUNK, W), lambda i: (0, 0))
    wp_spec = pl.BlockSpec((B_GROUPS, Wg, Wg), lambda i: (0, 0, 0))
    return pl.pallas_call(
        body, name=name, grid=(T // tr,),
        in_specs=[pl.BlockSpec((3, tr, W), lambda i: (0, i, 0)),
                  pl.BlockSpec((None, POOL_HALO, W), lambda i: (2, jnp.maximum(i * nb - 1, 0), 0)),
                  pl.BlockSpec((tr, 2 * W), lambda i: (i, 0)),
                  pl.BlockSpec((POOL_HALO, W), lambda i: (jnp.minimum((i + 1) * nb, last_blk), 1)),
                  vec, vec, ws_spec, bias_spec, wp_spec, vec],
        out_specs=[pl.BlockSpec((3, tr, W), lambda i: (0, i, 0)), ws_spec, bias_spec, vec, vec, wp_spec, vec],
        out_shape=[_S((3, T, W), _MM), _S((H, A_CHUNK, A_CHUNK), _F32), _S((A_CHUNK, W), _F32), _S((1, W), _F32),
                   _S((1, W), _F32), _S((B_GROUPS, Wg, Wg), _F32), _S((1, W), _F32)],
        compiler_params=_params(("arbitrary",)),
    )(h, h, dy, dy, lng, lnb, ws, bias, wp, sc)


def _chunk_cumsum(x, rin):
    s = 1
    while s < C_CHUNK:
        x = x + jnp.where(rin >= s, pltpu.roll(x, s, 0), 0.0)
        s *= 2
    return x


def _chunk_revcumsum(x, rin):
    n = x.shape[0]
    s = 1
    while s < C_CHUNK:
        x = x + jnp.where(rin + s < C_CHUNK, pltpu.roll(x, n - s, 0), 0.0)
        s *= 2
    return x


def _hgrn_gates(q, fl, lb, tr, tc):
    nch = tr // C_CHUNK
    sq = _sigmoid(q)
    sf = _sigmoid_rel(fl)
    f = lb + (1.0 - lb) * sf
    logf = jnp.log(f)
    rin = _row_index(tr) % C_CHUNK
    b = _chunk_cumsum(logf, rin)
    tot3 = jnp.sum(logf.reshape(nch, C_CHUNK, tc), axis=1, keepdims=True)
    eb = jnp.exp(b)
    enb = jnp.exp(-b)
    ekb = jnp.exp(tot3 - b.reshape(nch, C_CHUNK, tc)).reshape(tr, tc)
    return sq, sf, f, rin, tot3, eb, enb, ekb


def _hgrn_prep_fwd(h, lb, name):
    _, T, D = h.shape
    tr = _tile(T, 512)
    tc = _tile(D, 512)
    nch = tr // C_CHUNK

    def body(q_ref, f_ref, v_ref, lb_ref, qd_ref, kd_ref, ke_ref, vb_ref, dec_ref):
        q = q_ref[...]
        sq, _, f, _, tot3, eb, enb, ekb = _hgrn_gates(q, f_ref[...], lb_ref[...], tr, tc)
        kk = 1.0 - f
        qd_ref[...] = (q * sq * eb).astype(qd_ref.dtype)
        kd_ref[...] = (kk * enb).astype(kd_ref.dtype)
        ke_ref[...] = (kk * ekb).astype(ke_ref.dtype)
        vb_ref[...] = v_ref[...].astype(vb_ref.dtype)
        dec_ref[...] = jnp.exp(tot3).reshape(nch, tc)

    def part(p):
        return pl.BlockSpec((None, tr, tc), lambda i, j: (p, i, j))

    blk = pl.BlockSpec((tr, tc), lambda i, j: (i, j))
    return pl.pallas_call(
        body, name=name, grid=(T // tr, D // tc),
        in_specs=[part(0), part(1), part(2), pl.BlockSpec((1, tc), lambda i, j: (0, j))],
        out_specs=[blk, blk, blk, blk, pl.BlockSpec((nch, tc), lambda i, j: (i, j))],
        out_shape=[_S((T, D), _MM)] * 4 + [_S((T // C_CHUNK, D), _F32)],
        compiler_params=_params(("parallel", "parallel")),
    )(h, h, h, lb)


def _tril_mask():
    rr = lax.broadcasted_iota(jnp.int32, (C_CHUNK, C_CHUNK), 0)
    cc = lax.broadcasted_iota(jnp.int32, (C_CHUNK, C_CHUNK), 1)
    return rr >= cc


def _hgrn_scan_fwd(qd, kd, ke, vb, dec, h, ng, name):
    T, D = qd.shape
    NH = D // C_HEAD
    N = T // C_CHUNK

    def body(qd_ref, kd_ref, ke_ref, vb_ref, dec_ref, g_ref, ng_ref, o_ref, y_ref, st_ref):
        mask = _tril_mask()

        per_trip = math.gcd(N, SCAN_UNROLL)

        def trip(i, St):
            ahead = []
            for u in range(per_trip):
                n = i * per_trip + u
                r = pl.ds(pl.multiple_of(n * C_CHUNK, C_CHUNK), C_CHUNK)
                Qd, Kd, Ke, V = qd_ref[r, :], kd_ref[r, :], ke_ref[r, :], vb_ref[r, :]
                att = jnp.where(mask, _dot(Qd, Kd, _NT), 0.0).astype(_MM)
                ahead.append((n, r, _dot(att, V), _dot(V, Ke, _TN)))
            for n, r, o_intra, update in ahead:
                o_ref[r, :] = o_intra + _dot(qd_ref[r, :], St.astype(_MM), _NT)
                st_ref[n] = St
                St = St * dec_ref[pl.ds(n, 1), :] + update
            return St

        lax.fori_loop(0, N // per_trip, trip, jnp.zeros((C_HEAD, C_HEAD), _F32))
        o = o_ref[...]
        r = lax.rsqrt(jnp.mean(o * o, axis=-1, keepdims=True) + LN_EPS)
        y_ref[...] = (o * r * ng_ref[...] * _sigmoid(g_ref[...])).astype(y_ref.dtype)

    col = pl.BlockSpec((T, C_HEAD), lambda j: (0, j))
    return pl.pallas_call(
        body, name=name, grid=(NH,),
        in_specs=[col, col, col, col, pl.BlockSpec((N, C_HEAD), lambda j: (0, j)),
                  pl.BlockSpec((None, T, C_HEAD), lambda j: (3, 0, j)), pl.BlockSpec((1, C_HEAD), lambda j: (0, j))],
        out_specs=[col, col, pl.BlockSpec((None, N, C_HEAD, C_HEAD), lambda j: (j, 0, 0, 0))],
        out_shape=[_S((T, D), _F32), _S((T, D), _MM), _S((NH, N, C_HEAD, C_HEAD), _F32)],
        compiler_params=_params(("parallel",)),
    )(qd, kd, ke, vb, dec, h, ng)


def _hgrn_scan_bwd(qd, kd, ke, vb, dec, st, o, h, ng, dy, name):
    T, D = qd.shape
    NH = D // C_HEAD
    N = T // C_CHUNK

    def body(qd_ref, kd_ref, ke_ref, vb_ref, dec_ref, st_ref, o_ref, g_ref, ng_ref, dy_ref,
             dqd_ref, dkd_ref, dke_ref, dv_ref, dgate_ref, ddec_ref, dng_ref, do_s):
        o = o_ref[...]
        r = lax.rsqrt(jnp.mean(o * o, axis=-1, keepdims=True) + LN_EPS)
        oh = o * r
        gn = ng_ref[...]
        sg = _sigmoid(g_ref[...])
        d = dy_ref[...]
        dyn = d * sg
        dgate_ref[...] = (d * oh * gn * sg * (1.0 - sg)).astype(dgate_ref.dtype)
        dng_ref[...] = jnp.sum(dyn * oh, axis=0, keepdims=True)
        doh = dyn * gn
        do_s[...] = (r * (doh - oh * jnp.mean(doh * oh, axis=-1, keepdims=True))).astype(do_s.dtype)
        mask = _tril_mask()

        per_trip = math.gcd(N, SCAN_UNROLL)

        def trip(i, dSt):
            ahead = []
            for u in range(per_trip):
                n = N - 1 - (i * per_trip + u)
                rws = pl.ds(pl.multiple_of(n * C_CHUNK, C_CHUNK), C_CHUNK)
                Qd, Kd, V, dO = qd_ref[rws, :], kd_ref[rws, :], vb_ref[rws, :], do_s[rws, :]
                att = jnp.where(mask, _dot(Qd, Kd, _NT), 0.0).astype(_MM)
                dA = jnp.where(mask, _dot(dO, V, _NT), 0.0).astype(_MM)
                dqd_ref[rws, :] = _dot(dA, Kd) + _dot(dO, st_ref[n].astype(_MM))
                dkd_ref[rws, :] = _dot(dA, Qd, _TN)
                ahead.append((n, rws, _dot(att, dO, _TN), _dot(dO, Qd, _TN)))
            for n, rws, dv_intra, d_state in ahead:
                dStb = dSt.astype(_MM)
                dv_ref[rws, :] = (dv_intra + _dot(ke_ref[rws, :], dStb, _NT)).astype(dv_ref.dtype)
                dke_ref[rws, :] = _dot(vb_ref[rws, :], dStb)
                ddec_ref[pl.ds(n, 1), :] = jnp.sum(dSt * st_ref[n], axis=0, keepdims=True)
                dSt = dSt * dec_ref[pl.ds(n, 1), :] + d_state
            return dSt

        lax.fori_loop(0, N // per_trip, trip, jnp.zeros((C_HEAD, C_HEAD), _F32))

    col = pl.BlockSpec((T, C_HEAD), lambda j: (0, j))
    chk = pl.BlockSpec((N, C_HEAD), lambda j: (0, j))
    one = pl.BlockSpec((1, C_HEAD), lambda j: (0, j))
    return pl.pallas_call(
        body, name=name, grid=(NH,),
        in_specs=[col, col, col, col, chk, pl.BlockSpec((None, N, C_HEAD, C_HEAD), lambda j: (j, 0, 0, 0)), col,
                  pl.BlockSpec((None, T, C_HEAD), lambda j: (3, 0, j)), one, col],
        out_specs=[col, col, col, col, col, chk, one],
        out_shape=[_S((T, D), _F32)] * 3 + [_S((T, D), _MM)] * 2 + [_S((N, D), _F32), _S((1, D), _F32)],
        scratch_shapes=[pltpu.VMEM((T, C_HEAD), _MM)],
        compiler_params=_params(("parallel",)),
    )(qd, kd, ke, vb, dec, st, o, h, ng, dy)


def _hgrn_prep_bwd(h, lb, dqd, dkd, dke, dv, dgate, ddec, name):
    _, T, D = h.shape
    tr = _tile(T, 512)
    tc = _tile(D, 256)
    nch = tr // C_CHUNK

    def body(q_ref, f_ref, lb_ref, dqd_ref, dkd_ref, dke_ref, dv_ref, dgate_ref, ddec_ref, dh_ref, dlb_ref):
        @pl.when(pl.program_id(1) == 0)
        def _():
            dlb_ref[...] = jnp.zeros_like(dlb_ref)

        q = q_ref[...]
        lb = lb_ref[...]
        sq, sf, f, rin, tot3, eb, enb, ekb = _hgrn_gates(q, f_ref[...], lb, tr, tc)
        kk = 1.0 - f
        dQd, dKd, dKe = dqd_ref[...], dkd_ref[...], dke_ref[...]
        tq = dQd * eb
        tkd = dKd * enb
        tke = dKe * ekb
        ke_term = tke * kk
        db = tq * (q * sq) - tkd * kk - ke_term
        dtot3 = (jnp.sum(ke_term.reshape(nch, C_CHUNK, tc), axis=1, keepdims=True)
                 + (ddec_ref[...] * jnp.exp(tot3).reshape(nch, tc)).reshape(nch, 1, tc))
        dlogf = (_chunk_revcumsum(db, rin).reshape(nch, C_CHUNK, tc) + dtot3).reshape(tr, tc)
        df = dlogf / f - (tkd + tke)
        dh_ref[0] = (tq * sq * (1.0 + q * (1.0 - sq))).astype(dh_ref.dtype)
        dh_ref[1] = (df * (1.0 - lb) * sf * (1.0 - sf)).astype(dh_ref.dtype)
        dh_ref[2] = dv_ref[...]
        dh_ref[3] = dgate_ref[...]
        dlb_ref[...] += jnp.sum(df * (1.0 - sf), axis=0, keepdims=True)

    def part(p):
        return pl.BlockSpec((None, tr, tc), lambda j, i: (p, i, j))

    blk = pl.BlockSpec((tr, tc), lambda j, i: (i, j))
    vec = pl.BlockSpec((1, tc), lambda j, i: (0, j))
    return pl.pallas_call(
        body, name=name, grid=(D // tc, T // tr),
        in_specs=[part(0), part(1), vec, blk, blk, blk, blk, blk, pl.BlockSpec((nch, tc), lambda j, i: (i, j))],
        out_specs=[pl.BlockSpec((4, tr, tc), lambda j, i: (0, i, j)), vec],
        out_shape=[_S((4, T, D), _MM), _S((1, D), _F32)],
        compiler_params=_params(("parallel", "arbitrary")),
    )(h, h, lb, dqd, dkd, dke, dv, dgate, ddec)


def _sum_in_device_order(me1, own, land, name):
    R, C = own.shape
    tr = _tile(R, 256)

    def body(me_ref, own_ref, land_ref, o_ref):
        me = me_ref[0]
        g = None
        for j in range(N_DEV):
            slot = jnp.maximum(jnp.bitwise_xor(me, j) - 1, 0)
            p = jnp.where(me == j, own_ref[...], land_ref[slot])
            g = p if g is None else g + p
        o_ref[...] = g

    return pl.pallas_call(
        body, name=name,
        grid_spec=pltpu.PrefetchScalarGridSpec(
            num_scalar_prefetch=1, grid=(R // tr,),
            in_specs=[pl.BlockSpec((tr, C), lambda i, me: (i, 0)), pl.BlockSpec((N_DEV - 1, tr, C), lambda i, me: (0, i, 0))],
            out_specs=pl.BlockSpec((tr, C), lambda i, me: (i, 0))),
        out_shape=_S((R, C), _F32), compiler_params=_params(("parallel",)),
    )(me1, own, land)


def _adamw(parts, w, m, v, name):
    P, R, C = parts.shape
    tr = _tile(R, 128) if R % LANE == 0 else R

    def body(p_ref, w_ref, m_ref, v_ref, g_ref, d_ref, nm_ref, nv_ref):
        g = p_ref[0].astype(_F32)
        for s in range(1, P):
            g = g + p_ref[s].astype(_F32)
        nm = ADAM_B1 * m_ref[...] + (1.0 - ADAM_B1) * g
        nv = ADAM_B2 * v_ref[...] + (1.0 - ADAM_B2) * (g * g)
        m_hat = nm / (1.0 - ADAM_B1 ** ADAM_STEP)
        v_hat = nv / (1.0 - ADAM_B2 ** ADAM_STEP)
        g_ref[...] = g
        d_ref[...] = -ADAM_LR * (m_hat / (jnp.sqrt(v_hat) + ADAM_EPS) + ADAM_WD * w_ref[...])
        nm_ref[...] = nm
        nv_ref[...] = nv

    blk = pl.BlockSpec((tr, C), lambda i: (i, 0))
    return pl.pallas_call(
        body, name=name, grid=(R // tr,), in_specs=[pl.BlockSpec((P, tr, C), lambda i: (0, i, 0)), blk, blk, blk],
        out_specs=[blk] * 4, out_shape=[_S((R, C), _F32)] * 4, compiler_params=_params(("parallel",)),
    )(parts, w, m, v)


def _exchange(name, srcs, out_shapes, jobs, deps=()):
    ns, nj = len(srcs), len(jobs)

    nd = len(deps)

    def body(*refs):
        ins, outs = refs[:ns], refs[ns + nd:ns + nd + len(out_shapes)]
        send_sems, recv_sems, local_sems = refs[-3:]
        x, y, c = lax.axis_index("x"), lax.axis_index("y"), lax.axis_index("c")
        me = 4 * x + 2 * y + c
        local = []
        for ji, (si, src_fn, di, dst_fn) in enumerate(jobs):
            cp = pltpu.make_async_copy(src_fn(ins[si], me, me), dst_fn(outs[di], me), local_sems.at[ji])
            cp.start()
            local.append(cp)
        remote = []
        for k in range(1, N_DEV):
            px, py, pc = (x + (k >> 2)) % 2, (y + ((k >> 1) & 1)) % 2, (c + (k & 1)) % 2
            to = 4 * px + 2 * py + pc
            for ji, (si, src_fn, di, dst_fn) in enumerate(jobs):
                sem = (k - 1) * nj + ji
                cp = pltpu.make_async_remote_copy(
                    src_ref=src_fn(ins[si], me, to), dst_ref=dst_fn(outs[di], me),
                    send_sem=send_sems.at[sem], recv_sem=recv_sems.at[sem],
                    device_id=(px, py, pc), device_id_type=pl.DeviceIdType.MESH)
                cp.start()
                remote.append(cp)
        for cp in remote:
            cp.wait_recv()
        for cp in remote:
            cp.wait_send()
        for cp in local:
            cp.wait()

    hbm = pl.BlockSpec(memory_space=pltpu.HBM)
    return pl.pallas_call(
        body, name=name, in_specs=[hbm] * ns + [_ANY] * nd, out_specs=[hbm] * len(out_shapes), out_shape=list(out_shapes),
        scratch_shapes=[pltpu.SemaphoreType.DMA(((N_DEV - 1) * nj,)), pltpu.SemaphoreType.DMA(((N_DEV - 1) * nj,)),
                        pltpu.SemaphoreType.DMA((nj,))],
    )(*srcs, *deps)


def _whole(ref, me, to):
    return ref


def _slot_job(i, o):
    def dst(ref, me):
        return ref.at[me]
    return (i, _whole, o, dst)


_HBM = pl.BlockSpec(memory_space=pltpu.HBM)
_SEM = pl.BlockSpec(memory_space=pltpu.SEMAPHORE)
_ANY = pl.BlockSpec(memory_space=pl.ANY)
_N_PEER = N_DEV - 1


def _split_params():
    return pltpu.CompilerParams(has_side_effects=pltpu.SideEffectType.DATAFLOW_SIDE_EFFECTING)


def _blk(ref, axis, n, idx):
    if axis is None:
        return ref
    return ref.at[tuple([slice(None)] * axis + [pl.ds(pl.multiple_of(idx * n, n), n)])]


def _peer(k):
    x, y, c = lax.axis_index("x"), lax.axis_index("y"), lax.axis_index("c")
    px, py, pc = (x + (k >> 2)) % 2, (y + ((k >> 1) & 1)) % 2, (c + (k & 1)) % 2
    return (px, py, pc), 4 * px + 2 * py + pc, 4 * x + 2 * y + c


def _row_tile(rows, pref):
    best = None
    for d in range(16, min(rows, pref) + 1, 16):
        if rows % d == 0:
            best = d
    return best if best is not None else rows


def _place(w, me1, axis, name, layer=None, deps=()):
    R, C = w.shape[-2:]
    tr = _row_tile(R, 512)
    nb = R // tr
    lead = () if layer is None else (None,)
    pre = () if layer is None else (layer,)

    def body(me_ref, w_ref, *rest):
        rest[-1][...] = w_ref[...].astype(rest[-1].dtype)

    if axis == 1:
        out_spec = pl.BlockSpec((tr, C), lambda i, me: (i, me[0]))
        out_shape = _S((R, N_DEV * C), _MM)
    else:
        out_spec = pl.BlockSpec((tr, C), lambda i, me: (me[0] * nb + i, 0))
        out_shape = _S((N_DEV * R, C), _MM)
    return pl.pallas_call(
        body, name=name,
        grid_spec=pltpu.PrefetchScalarGridSpec(
            num_scalar_prefetch=1, grid=(nb,),
            in_specs=[pl.BlockSpec(lead + (tr, C), lambda i, me: pre + (i, 0))] + [_ANY] * len(deps), out_specs=out_spec),
        out_shape=out_shape, compiler_params=_params(("parallel",)),
    )(me1, w, *deps)


_SIBLING = 1
_CHIPS = (2, 4, 6)
_VMEM_TOKEN = pl.BlockSpec(memory_space=pltpu.VMEM)


def _remote(ref_blk, send_sem, recv_sem, dev):
    return pltpu.make_async_remote_copy(src_ref=ref_blk, dst_ref=ref_blk, send_sem=send_sem, recv_sem=recv_sem,
                                        device_id=dev, device_id_type=pl.DeviceIdType.MESH)


def _gather_start(name, full, axis, n):
    def body(f_ref, send, recv, f_out, token):
        for i, k in enumerate((_SIBLING,) + _CHIPS):
            dev, _, me = _peer(k)
            _remote(_blk(f_ref, axis, n, me), send.at[i], recv.at[i], dev).start()
        token[...] = jnp.zeros_like(token)

    return pl.pallas_call(
        body, name=name,
        out_shape=(pltpu.SemaphoreType.DMA((4,)), pltpu.SemaphoreType.DMA((4,)), pltpu.HBM(full.shape, full.dtype),
                   _S((8, LANE), _F32)),
        in_specs=(_HBM,), out_specs=(_SEM, _SEM, _HBM, _VMEM_TOKEN),
        input_output_aliases={0: 2}, compiler_params=_split_params(),
    )(pltpu.with_memory_space_constraint(full, pltpu.HBM))


def _gather_forward(name, full, axis, n, recv, after):
    after = tuple(after) if isinstance(after, (tuple, list)) else (after,)

    def body(f_ref, recv_r, *rest):
        send2, recv2, f_out, token = rest[-4:]
        sib, _, _ = _peer(_SIBLING)
        for i, k in enumerate(_CHIPS):
            dev, frm, _ = _peer(k)
            blk = _blk(f_ref, axis, n, frm)
            _remote(blk, send2.at[i], recv_r.at[1 + i], dev).wait_recv()
            _remote(blk, send2.at[i], recv2.at[i], sib).start()
        token[...] = jnp.zeros_like(token)

    return pl.pallas_call(
        body, name=name,
        out_shape=(pltpu.SemaphoreType.DMA((3,)), pltpu.SemaphoreType.DMA((3,)), pltpu.HBM(full.shape, full.dtype),
                   _S((8, LANE), _F32)),
        in_specs=(_HBM, _SEM) + (_ANY,) * len(after), out_specs=(_SEM, _SEM, _HBM, _VMEM_TOKEN),
        input_output_aliases={0: 2}, compiler_params=_split_params(),
    )(full, recv, *after)


def _gather_wait(name, full, axis, n, send, recv, send2, recv2, after):
    def body(f_ref, send_r, recv_r, send2_r, recv2_r, after_ref, f_out):
        sib, _, me = _peer(_SIBLING)
        blk = _blk(f_ref, axis, n, me)
        for i in range(4):
            _remote(blk, send_r.at[i], recv_r.at[0], sib).wait_send()
        _remote(blk, send_r.at[0], recv_r.at[0], sib).wait_recv()
        for i in range(3):
            cp = _remote(blk, send2_r.at[i], recv2_r.at[i], sib)
            cp.wait_send()
            cp.wait_recv()

    return pl.pallas_call(
        body, name=name, out_shape=pltpu.HBM(full.shape, full.dtype),
        in_specs=(_HBM, _SEM, _SEM, _SEM, _SEM, _ANY), out_specs=_HBM,
        input_output_aliases={0: 0}, compiler_params=_split_params(),
    )(full, send, recv, send2, recv2, after)


def _scatter_start(name, dw, axis, n):
    shard = tuple(n if a == axis else d for a, d in enumerate(dw.shape))
    land = lax.empty((_N_PEER,) + shard, dw.dtype)

    def body(dw_ref, land_ref, send, recv, dw_out, land_out, token):
        for k in range(1, N_DEV):
            dev, to, _ = _peer(k)
            pltpu.make_async_remote_copy(
                src_ref=_blk(dw_ref, axis, n, to), dst_ref=land_ref.at[k - 1], send_sem=send.at[k - 1],
                recv_sem=recv.at[k - 1], device_id=dev, device_id_type=pl.DeviceIdType.MESH).start()
        token[...] = jnp.zeros_like(token)

    return pl.pallas_call(
        body, name=name,
        out_shape=(pltpu.SemaphoreType.DMA((_N_PEER,)), pltpu.SemaphoreType.DMA((_N_PEER,)),
                   pltpu.HBM(dw.shape, dw.dtype), pltpu.HBM(land.shape, land.dtype), _S((8, LANE), _F32)),
        in_specs=(_HBM, _HBM), out_specs=(_SEM, _SEM, _HBM, _HBM, pl.BlockSpec(memory_space=pltpu.VMEM)),
        input_output_aliases={0: 2, 1: 3}, compiler_params=_split_params(),
    )(pltpu.with_memory_space_constraint(dw, pltpu.HBM), pltpu.with_memory_space_constraint(land, pltpu.HBM))


def _scatter_wait(name, items, after):
    ne = len(items)
    after = tuple(after) if isinstance(after, (tuple, list)) else (after,)

    def body(*refs):
        for e, (_, _, _, _, axis, n) in enumerate(items):
            dw_ref, land_ref, send_r, recv_r = refs[4 * e:4 * e + 4]
            for k in range(1, N_DEV):
                dev, to, _ = _peer(k)
                cp = pltpu.make_async_remote_copy(
                    src_ref=_blk(dw_ref, axis, n, to), dst_ref=land_ref.at[k - 1], send_sem=send_r.at[k - 1],
                    recv_sem=recv_r.at[k - 1], device_id=dev, device_id_type=pl.DeviceIdType.MESH)
                cp.wait_send()
                cp.wait_recv()

    args, out_shape = [], []
    for dw, land, send, recv, _, _ in items:
        args += [dw, land, send, recv]
        out_shape += [pltpu.HBM(dw.shape, dw.dtype), pltpu.HBM(land.shape, land.dtype)]
    res = pl.pallas_call(
        body, name=name, out_shape=tuple(out_shape),
        in_specs=(_HBM, _HBM, _SEM, _SEM) * ne + (_ANY,) * len(after), out_specs=(_HBM,) * (2 * ne),
        input_output_aliases={4 * e + j: 2 * e + j for e in range(ne) for j in range(2)},
        compiler_params=_split_params(),
    )(*args, *after)
    return [(res[2 * e], res[2 * e + 1]) for e in range(ne)]


def _adamw_big(me1, dw, land, w, m, v, axis, n, name, layer=None, into=None):
    R, C = land.shape[1:]
    tr = _row_tile(R, 128)
    nb = R // tr
    lead = () if layer is None else (None,)
    pre = () if layer is None else (layer,)

    def body(me_ref, own_ref, land_ref, w_ref, m_ref, v_ref, *rest):
        g_ref, d_ref, nm_ref, nv_ref = rest[-4:]
        g = own_ref[...].astype(_F32)
        for s in range(_N_PEER):
            g = g + land_ref[s].astype(_F32)
        nm = ADAM_B1 * m_ref[...] + (1.0 - ADAM_B1) * g
        nv = ADAM_B2 * v_ref[...] + (1.0 - ADAM_B2) * (g * g)
        m_hat = nm / (1.0 - ADAM_B1 ** ADAM_STEP)
        v_hat = nv / (1.0 - ADAM_B2 ** ADAM_STEP)
        g_ref[...] = g
        d_ref[...] = -ADAM_LR * (m_hat / (jnp.sqrt(v_hat) + ADAM_EPS) + ADAM_WD * w_ref[...])
        nm_ref[...] = nm
        nv_ref[...] = nv

    if axis == 1:
        own_spec = pl.BlockSpec((tr, C), lambda i, me: (i, me[0]))
    else:
        own_spec = pl.BlockSpec((tr, C), lambda i, me: (me[0] * nb + i, 0))
    blk = pl.BlockSpec(lead + (tr, C), lambda i, me: pre + (i, 0))
    in_specs = [own_spec, pl.BlockSpec((_N_PEER, tr, C), lambda i, me: (0, i, 0)), blk, blk, blk]
    args = [me1, dw, land, w, m, v]
    aliases = {}
    if into is not None:
        in_specs += [_ANY] * 4
        aliases = {6 + j: j for j in range(4)}
        args += list(into)
    return pl.pallas_call(
        body, name=name,
        grid_spec=pltpu.PrefetchScalarGridSpec(num_scalar_prefetch=1, grid=(nb,), in_specs=in_specs, out_specs=[blk] * 4),
        out_shape=[_S(w.shape, _F32)] * 4, input_output_aliases=aliases, compiler_params=_params(("parallel",)),
    )(*args)


def _pack(arrs):
    flat = jnp.concatenate([a.reshape(-1).astype(_F32) for a in arrs])
    pad = (-flat.shape[0]) % (LANE * LANE)
    return jnp.pad(flat, (0, pad)).reshape(-1, LANE)


def _unpack(mat, shapes):
    flat = mat.reshape(-1)
    out, off = [], 0
    for s in shapes:
        n = 1
        for d in s:
            n *= d
        out.append(flat[off:off + n].reshape(s))
        off += n
    return out


def _lb_of(lb_param):
    lb_all = jnp.cumsum(jax.nn.softmax(lb_param.astype(_F32), axis=0), axis=0)
    return (lb_all - lb_all[0])[1:2]


def kernel(x, ev_w_in, ev_ln_v_g, ev_ln_v_b, ev_w_s, ev_b_s, ev_w_pool, ev_pool_scale, ev_w_out, od_w_in, od_norm_g, od_w_out, lb_param, ffn_w_up, ffn_conv_w, ffn_conv_b, ffn_w_down, ln1_g, ln1_b, ln2_g, ln2_b, loss_target, m_ev_w_in, m_ev_ln_v_g, m_ev_ln_v_b, m_ev_w_s, m_ev_b_s, m_ev_w_pool, m_ev_pool_scale, m_ev_w_out, m_od_w_in, m_od_norm_g, m_od_w_out, m_lb_param, m_ffn_w_up, m_ffn_conv_w, m_ffn_conv_b, m_ffn_w_down, m_ln1_g, m_ln1_b, m_ln2_g, m_ln2_b, v_ev_w_in, v_ev_ln_v_g, v_ev_ln_v_b, v_ev_w_s, v_ev_b_s, v_ev_w_pool, v_ev_pool_scale, v_ev_w_out, v_od_w_in, v_od_norm_g, v_od_w_out, v_lb_param, v_ffn_w_up, v_ffn_conv_w, v_ffn_conv_b, v_ffn_w_down, v_ln1_g, v_ln1_b, v_ln2_g, v_ln2_b):
    me = 4 * lax.axis_index("x") + 2 * lax.axis_index("y") + lax.axis_index("c")
    T, D = x.shape[1], x.shape[2]
    W = ev_ln_v_g.shape[1]
    H = W // A_HEAD
    Wg = W // B_GROUPS
    F2 = ffn_conv_b.shape[1]
    F = F2 // 2
    n_in0, n_out0 = ev_w_in.shape[2], ev_w_out.shape[1]
    n_in1, n_out1 = od_w_in.shape[2], od_w_out.shape[1]
    n_up, n_dn = ffn_w_up.shape[2], ffn_w_down.shape[1]
    n_pool, n_ng, n_cw = ev_w_pool.shape[2], od_norm_g.shape[1], ffn_conv_w.shape[2]

    small_shards = [od_norm_g, ffn_conv_w, ev_w_pool]
    small_pack = _pack(small_shards)
    small_all = _exchange("gather_small_params", [small_pack], [_S((N_DEV,) + small_pack.shape, _F32)], [_slot_job(0, 0)])[0]

    me1 = me.astype(jnp.int32).reshape(1)
    weights = [
        ("w_in0", ev_w_in[0], None, 1, n_in0), ("w_out0", ev_w_out[0], None, 0, n_out0),
        ("w_up0", ffn_w_up, 0, 1, n_up), ("w_dn0", ffn_w_down, 0, 0, n_dn),
        ("w_in1", od_w_in[0], None, 1, n_in1), ("w_out1", od_w_out[0], None, 0, n_out1),
        ("w_up1", ffn_w_up, 1, 1, n_up), ("w_dn1", ffn_w_down, 1, 0, n_dn),
    ]
    started, tokens = {}, [small_all]
    for key, w, layer, axis, n in weights:
        full = _place(w, me1, axis, "place_" + key, layer, deps=tokens)
        send, recv, full, token = _gather_start("gather_start_" + key, full, axis, n)
        started[key] = (full, axis, n, send, recv)
        tokens = [token]

    def pass_on(key, after):
        full, axis, n, send, recv = started[key]
        send2, recv2, full, token = _gather_forward("gather_forward_" + key, full, axis, n, recv, after)
        started[key] = (full, axis, n, send, recv, send2, recv2)
        return token

    def gathered(key, after):
        return _gather_wait("gather_wait_" + key, *started[key], after)

    ng_parts, cw_parts, wp_parts = [], [], []
    for j in range(N_DEV):
        a, b, c = _unpack(small_all[j], [s.shape for s in small_shards])
        ng_parts.append(a)
        cw_parts.append(b)
        wp_parts.append(c)
    norm_g = jnp.concatenate(ng_parts, axis=1)
    conv_w = jnp.concatenate(cw_parts, axis=2)
    w_pool = jnp.concatenate(wp_parts, axis=2)[0]
    cw_l = [conv_w[l].reshape(3, 2, F).transpose(1, 0, 2) for l in range(DEPTH)]
    cb_l = [ffn_conv_b[l].reshape(2, 1, F) for l in range(DEPTH)]
    ws_tril = jnp.tril(ev_w_s[0]).astype(_MM)
    bias = jnp.repeat(ev_b_s[0].T, A_HEAD, axis=1)
    wp_b = w_pool.astype(_MM)
    lb, lb_vjp = jax.vjp(_lb_of, lb_param)

    small_names = ["ev_ln_v_g", "ev_ln_v_b", "ev_w_s", "ev_b_s", "ev_w_pool", "ev_pool_scale", "od_norm_g", "lb_param",
                   "ffn_conv_w", "ffn_conv_b", "ln1_g", "ln1_b", "ln2_g", "ln2_b"]
    given = dict(ev_ln_v_g=(ev_ln_v_g, m_ev_ln_v_g, v_ev_ln_v_g), ev_ln_v_b=(ev_ln_v_b, m_ev_ln_v_b, v_ev_ln_v_b),
                 ev_w_s=(ev_w_s, m_ev_w_s, v_ev_w_s), ev_b_s=(ev_b_s, m_ev_b_s, v_ev_b_s),
                 ev_w_pool=(ev_w_pool, m_ev_w_pool, v_ev_w_pool),
                 ev_pool_scale=(ev_pool_scale, m_ev_pool_scale, v_ev_pool_scale),
                 od_norm_g=(od_norm_g, m_od_norm_g, v_od_norm_g), lb_param=(lb_param, m_lb_param, v_lb_param),
                 ffn_conv_w=(ffn_conv_w, m_ffn_conv_w, v_ffn_conv_w), ffn_conv_b=(ffn_conv_b, m_ffn_conv_b, v_ffn_conv_b),
                 ln1_g=(ln1_g, m_ln1_g, v_ln1_g), ln1_b=(ln1_b, m_ln1_b, v_ln1_b), ln2_g=(ln2_g, m_ln2_g, v_ln2_g),
                 ln2_b=(ln2_b, m_ln2_b, v_ln2_b))
    shard_axis = dict(ev_w_pool=2, od_norm_g=1, ffn_conv_w=2)
    rep_names = [n for n in small_names if n not in shard_axis]
    shd_names = [n for n in small_names if n in shard_axis]
    small_packs = [_pack([given[n][j] for n in small_names]) for j in range(3)]

    x2 = x[0]
    xb = _cast(x2, _MM, "cast_x", deps=[pass_on("w_in0", tokens[0])])
    w_in0 = gathered("w_in0", xb)
    h0 = _mm(xb, w_in0, "nn", _F32, "ev_in", out_parts=3)
    tie = pass_on("w_out0", h0)
    yab = _ev_mid_fwd(h0, ev_ln_v_g + tie[0, 0], ev_ln_v_b, ws_tril, bias, wp_b, ev_pool_scale, "ev_mid_fwd")
    w_out0 = gathered("w_out0", yab)
    z1 = _mm(yab, w_out0, "nn", _F32, "ev_out", add=x2, add_scale=ALPHA)
    tie = pass_on("w_up0", (z1, *small_packs))
    x1, x1b = _ln_fwd(z1, ln1_g[0:1] + tie[0, 0], ln1_b[0:1], "ln1_0")
    w_up0 = gathered("w_up0", x1b)
    hf0 = _mm(x1b, w_up0, "nn", _F32, "ffn_up", out_parts=2)
    tie = pass_on("w_dn0", hf0)
    act0, hc0 = _ffn_mid_fwd(hf0, cw_l[0], cb_l[0] + tie[0, 0], "ffn_mid_fwd")
    w_dn0 = gathered("w_dn0", act0)
    z2 = _mm(act0, w_dn0, "nn", _F32, "ffn_down", add=x1, add_scale=ALPHA)
    tie = pass_on("w_in1", z2)
    x2_, x2b = _ln_fwd(z2, ln2_g[0:1] + tie[0, 0], ln2_b[0:1], "ln2_0")
    w_in1 = gathered("w_in1", x2b)
    h1 = _mm(x2b, w_in1, "nn", _F32, "od_in", out_parts=4)
    qd, kd, ke, vb, dec = _hgrn_prep_fwd(h1, lb, "hgrn_prep_fwd")
    tie = pass_on("w_out1", qd)
    o, yo, st = _hgrn_scan_fwd(qd, kd, ke, vb, dec, h1, norm_g + tie[0, 0], "hgrn_scan_fwd")
    w_out1 = gathered("w_out1", yo)
    z3 = _mm(yo, w_out1, "nn", _F32, "od_out", add=x2_, add_scale=ALPHA)
    tie = pass_on("w_up1", z3)
    x3, x3b = _ln_fwd(z3, ln1_g[1:2] + tie[0, 0], ln1_b[1:2], "ln1_1")
    w_up1 = gathered("w_up1", x3b)
    hf1 = _mm(x3b, w_up1, "nn", _F32, "ffn_up", out_parts=2)
    tie = pass_on("w_dn1", hf1)
    act1, hc1 = _ffn_mid_fwd(hf1, cw_l[1], cb_l[1] + tie[0, 0], "ffn_mid_fwd")
    w_dn1 = gathered("w_dn1", act1)
    z4 = _mm(act1, w_dn1, "nn", _F32, "ffn_down", add=x3, add_scale=ALPHA)

    scat = {}

    def scatter(key, dw, axis, n):
        send, recv, dw, land, token = _scatter_start("scatter_start_" + key, dw, axis, n)
        scat[key] = (dw, land, send, recv, axis, n)
        return [token]

    loss11, dz4, dz4b, g_ln2_1, b_ln2_1 = _ln_loss_bwd(z4, ln2_g[1:2], ln2_b[1:2], loss_target[0], "ln_loss_bwd")
    tok = scatter("dn1", _mm(act1, dz4b, "tn", _XCH, "ffn_down_dw"), 0, n_dn)
    dact1 = _mm(dz4b, w_dn1, "nt", _MM, "ffn_down_dx", deps=tok)
    dhf1, dcw1, dcb1 = _ffn_mid_bwd(hf1, hc1, dact1, cw_l[1], "ffn_mid_bwd")
    tok = scatter("up1", _mm(x3b, dhf1, "tn", _XCH, "ffn_up_dw", b_parts=2, deps=tok), 1, n_up)
    dx3 = _mm(dhf1, w_up1, "nt", _F32, "ffn_up_dx", a_parts=2, add=dz4, add_scale=ALPHA, deps=tok)
    dz3, dz3b, g_ln1_1, b_ln1_1 = _ln_bwd(z3, ln1_g[1:2], dx3, "ln_bwd")
    tok = scatter("out1", _mm(yo, dz3b, "tn", _XCH, "od_out_dw", deps=tok), 0, n_out1)
    dyo = _mm(dz3b, w_out1, "nt", _F32, "od_out_dx", deps=tok)
    dqd, dkd, dke, dv, dgate, ddec, dng = _hgrn_scan_bwd(qd, kd, ke, vb, dec, st, o, h1, norm_g, dyo, "hgrn_scan_bwd")
    dh1, dlb = _hgrn_prep_bwd(h1, lb, dqd, dkd, dke, dv, dgate, ddec, "hgrn_prep_bwd")
    tok = scatter("in1", _mm(x2b, dh1, "tn", _XCH, "od_in_dw", b_parts=4, deps=tok), 1, n_in1)
    dx2 = _mm(dh1, w_in1, "nt", _F32, "od_in_dx", a_parts=4, add=dz3, add_scale=ALPHA, deps=tok)
    dz2, dz2b, g_ln2_0, b_ln2_0 = _ln_bwd(z2, ln2_g[0:1], dx2, "ln_bwd")
    tok = scatter("dn0", _mm(act0, dz2b, "tn", _XCH, "ffn_down_dw", deps=tok), 0, n_dn)
    dact0 = _mm(dz2b, w_dn0, "nt", _MM, "ffn_down_dx", deps=tok)
    dhf0, dcw0, dcb0 = _ffn_mid_bwd(hf0, hc0, dact0, cw_l[0], "ffn_mid_bwd")
    tok = scatter("up0", _mm(x1b, dhf0, "tn", _XCH, "ffn_up_dw", b_parts=2, deps=tok), 1, n_up)
    dx1 = _mm(dhf0, w_up0, "nt", _F32, "ffn_up_dx", a_parts=2, add=dz2, add_scale=ALPHA, deps=tok)
    dz1, dz1b, g_ln1_0, b_ln1_0 = _ln_bwd(z1, ln1_g[0:1], dx1, "ln_bwd")
    tok = scatter("out0", _mm(yab, dz1b, "tn", _XCH, "ev_out_dw", deps=tok), 0, n_out0)
    dyab = _mm(dz1b, w_out0, "nt", _F32, "ev_out_dx", deps=tok)
    dh0, dws, dbias, dlng, dlnb, dwp, dsc = _ev_mid_bwd(h0, dyab, ev_ln_v_g, ev_ln_v_b, ws_tril, bias, wp_b,
                                                        ev_pool_scale, "ev_mid_bwd")

    g_b_s = dbias.reshape(A_CHUNK, H, A_HEAD).sum(axis=-1).T[None]
    g_conv_w = jnp.stack([d.transpose(1, 0, 2).reshape(3, F2) for d in (dcw0, dcw1)])
    g_conv_b = jnp.stack([d.reshape(F2) for d in (dcb0, dcb1)])
    small_grads = dict(zip(small_names, [
        dlng, dlnb, dws[None], g_b_s, dwp[None], dsc, dng, lb_vjp(dlb)[0], g_conv_w, g_conv_b,
        jnp.concatenate([g_ln1_0, g_ln1_1]), jnp.concatenate([b_ln1_0, b_ln1_1]),
        jnp.concatenate([g_ln2_0, g_ln2_1]), jnp.concatenate([b_ln2_0, b_ln2_1])]))

    def by_device(g, ax):
        g = g.reshape(g.shape[:ax] + (N_DEV, g.shape[ax] // N_DEV) + g.shape[ax + 1:])
        return jnp.moveaxis(g, ax, 0).reshape(N_DEV, -1)

    shd = jnp.concatenate([by_device(small_grads[n], shard_axis[n]) for n in shd_names], axis=1)
    shd_pack = jnp.pad(shd, ((0, 0), (0, (-shd.shape[1]) % (LANE * LANE)))).reshape(-1, LANE)
    shd_rows = shd_pack.shape[0] // N_DEV
    rep_pack = _pack([small_grads[n] for n in rep_names])
    tok = scatter("in0", _mm(xb, dh0, "tn", _XCH, "ev_in_dw", b_parts=3, deps=tok), 1, n_in0)
    tok = scatter("small_rep", rep_pack + tok[0][0, 0], None, None)
    tok = scatter("small_shd", shd_pack + tok[0][0, 0], 0, shd_rows)
    grad_x = _mm(dh0, w_in0, "nt", _F32, "ev_in_dx", a_parts=3, add=dz1, add_scale=ALPHA, deps=tok)

    def landed(name, keys, after):
        got = _scatter_wait(name, [scat[k] for k in keys], after)
        return {k: (me1, dw, land) for k, (dw, land) in zip(keys, got)}

    early = landed("scatter_wait_early", ["dn1", "up1", "out1", "in1", "dn0", "up0", "out0"], grad_x)
    big = {}
    r_dn = _adamw_big(*early["dn1"], ffn_w_down, m_ffn_w_down, v_ffn_w_down, 0, n_dn, "adamw_w_dn1", layer=1)
    r_up = _adamw_big(*early["up1"], ffn_w_up, m_ffn_w_up, v_ffn_w_up, 1, n_up, "adamw_w_up1", layer=1)
    big["od_w_out"] = _adamw_big(*early["out1"], od_w_out[0], m_od_w_out[0], v_od_w_out[0], 0, n_out1, "adamw_w_out1")
    big["od_w_in"] = _adamw_big(*early["in1"], od_w_in[0], m_od_w_in[0], v_od_w_in[0], 1, n_in1, "adamw_w_in1")
    big["ffn_w_down"] = _adamw_big(*early["dn0"], ffn_w_down, m_ffn_w_down, v_ffn_w_down, 0, n_dn, "adamw_w_dn0", layer=0, into=r_dn)
    big["ffn_w_up"] = _adamw_big(*early["up0"], ffn_w_up, m_ffn_w_up, v_ffn_w_up, 1, n_up, "adamw_w_up0", layer=0, into=r_up)
    big["ev_w_out"] = _adamw_big(*early["out0"], ev_w_out[0], m_ev_w_out[0], v_ev_w_out[0], 0, n_out0, "adamw_w_out0")
    late = landed("scatter_wait_late", ["in0", "small_rep", "small_shd"],
                  (big["ffn_w_down"][0], big["ffn_w_up"][0], big["od_w_in"][0], big["ev_w_out"][0]))
    big["ev_w_in"] = _adamw_big(*late["in0"], ev_w_in[0], m_ev_w_in[0], v_ev_w_in[0], 1, n_in0, "adamw_w_in0")

    rep_mat = _sum_in_device_order(*late["small_rep"], "sum_small_rep")
    local_g = dict(zip(rep_names, _unpack(rep_mat, [small_grads[n].shape for n in rep_names])))
    _, shd_all, shd_land = late["small_shd"]
    shd_own = lax.dynamic_slice_in_dim(shd_all, me * shd_rows, shd_rows, axis=0)
    shd_mat = _sum_in_device_order(me1, shd_own, shd_land, "sum_small_shd")
    local_g.update(zip(shd_names, _unpack(shd_mat, [given[n][0].shape for n in shd_names])))
    local_shapes = [given[n][0].shape for n in small_names]
    res = _adamw(_pack([local_g[n] for n in small_names])[None], *small_packs, "adamw_small")
    small = {n: [] for n in small_names}
    for r in res:
        for n, a in zip(small_names, _unpack(r, local_shapes)):
            small[n].append(a)

    loss = lax.psum(loss11[0, 0], ("x", "y", "c"))
    order = ["ev_w_in", "ev_ln_v_g", "ev_ln_v_b", "ev_w_s", "ev_b_s", "ev_w_pool", "ev_pool_scale", "ev_w_out", "od_w_in",
             "od_norm_g", "od_w_out", "lb_param", "ffn_w_up", "ffn_conv_w", "ffn_conv_b", "ffn_w_down", "ln1_g", "ln1_b",
             "ln2_g", "ln2_b"]
    shapes = dict(ev_w_in=ev_w_in.shape, ev_w_out=ev_w_out.shape, od_w_in=od_w_in.shape, od_w_out=od_w_out.shape,
                  ffn_w_up=ffn_w_up.shape, ffn_w_down=ffn_w_down.shape)
    outs = [loss, grad_x[None]]
    for kind in range(4):
        for n in order:
            outs.append(big[n][kind].reshape(shapes[n]) if n in big else small[n][kind])
    return tuple(outs)
```

```python
import functools
import math

import jax
import jax.numpy as jnp
from jax import lax
from jax.experimental import pallas as pl
from jax.experimental.pallas import tpu as pltpu

_MM = jnp.bfloat16
_XCH = jnp.bfloat16

DEPTH = 2
ALPHA = (2 * DEPTH) ** 0.25
LN_EPS = 1e-5
A_CHUNK = 128
A_HEAD = 128
B_GROUPS = 4
POOL_HALO = 16
C_CHUNK = 64
C_HEAD = 128
SCAN_UNROLL = 32
CONV_HALO = 8
PACKED_ROWS = 16
FFN_ROWS, FFN_FWD_COLS = 512, 1408
FFN_BWD_ROWS, FFN_BWD_COLS = 512, 512
FFN_CHUNK = 128
ADAM_LR, ADAM_B1, ADAM_B2, ADAM_EPS, ADAM_WD, ADAM_STEP = 0.001, 0.9, 0.999, 1e-08, 0.01, 10
N_DEV = 8
LANE = 128
VMEM_LIMIT = 56 * 1024 * 1024
MM_FULL_K = 2048
MM_FULL_K_TN = 4096
MM_DEEP_K = 2816
MM_LN_ROWS, MM_LN_K = 512, 1408

_F32 = jnp.float32
_NN = (((1,), (0,)), ((), ()))
_NT = (((1,), (1,)), ((), ()))
_TN = (((0,), (0,)), ((), ()))
_S = jax.ShapeDtypeStruct


def _dot(a, b, dims=_NN):
    return lax.dot_general(a, b, dims, preferred_element_type=_F32)


def _tile(dim, pref):
    best = None
    d = LANE
    while d <= min(dim, pref):
        if dim % d == 0:
            best = d
        d += LANE
    return best if best is not None else dim


def _params(sem):
    return pltpu.CompilerParams(dimension_semantics=sem, vmem_limit_bytes=VMEM_LIMIT)


def _sigmoid(x):
    return 0.5 * jnp.tanh(0.5 * x) + 0.5


def _sigmoid_rel(x):
    return 1.0 / (1.0 + jnp.exp(-x))


_GELU_C = 0.7978845608028654
_GELU_A = 0.044715


def _gelu_and_grad(x):
    t = jnp.tanh(_GELU_C * (x + _GELU_A * x * x * x))
    y = 0.5 * x * (1.0 + t)
    dy = 0.5 * (1.0 + t) + 0.5 * x * (1.0 - t * t) * _GELU_C * (1.0 + 3.0 * _GELU_A * x * x)
    return y, dy


def _row_index(n):
    return lax.broadcasted_iota(jnp.int32, (n, 1), 0)


def _mm_tiles(mode, M, N, K, with_add):
    if mode == "tn":
        return _tile(M, 1024), _tile(N, 1024), _tile(K, MM_FULL_K_TN)
    if K <= MM_FULL_K:
        return _tile(M, 1024 if with_add else 2048), _tile(N, 1024 if mode == "nn" else 512), K
    return _tile(M, 1024), _tile(N, 1024), _tile(K, MM_DEEP_K)


def _mm(a, b, mode, out_dtype, name, *, a_parts=1, b_parts=1, out_parts=1, add=None, add_scale=1.0, deps=(), tiles=None,
        ln=None):
    if mode == "nn":
        M, K = a.shape
        N = b.shape[1]
    elif mode == "nt":
        if a_parts > 1:
            M, K = a.shape[1], a.shape[2] * a_parts
        else:
            M, K = a.shape
        N = b.shape[0]
    else:
        K, M = a.shape
        N = b.shape[-1] * b_parts
    tm, tn, tk = tiles if tiles is not None else _mm_tiles(mode, M, N // max(b_parts, out_parts), K // a_parts, add is not None)
    if ln is not None:
        tm, tn, tk = _tile(M, MM_LN_ROWS), N, (K if K <= MM_FULL_K else _tile(K, MM_LN_K))
    nk = K // tk
    npj = (N // max(b_parts, out_parts)) // tn
    nkp = (K // a_parts) // tk
    if mode == "nn":
        a_spec = pl.BlockSpec((tm, tk), lambda i, j, k: (i, k))
        b_spec = pl.BlockSpec((tk, tn), lambda i, j, k: (k, j))
        dims = _NN
    elif mode == "nt":
        if a_parts > 1:
            a_spec = pl.BlockSpec((None, tm, tk), lambda i, j, k: (k // nkp, i, k % nkp))
        else:
            a_spec = pl.BlockSpec((tm, tk), lambda i, j, k: (i, k))
        b_spec = pl.BlockSpec((tn, tk), lambda i, j, k: (j, k))
        dims = _NT
    else:
        a_spec = pl.BlockSpec((tk, tm), lambda i, j, k: (k, i))
        if b_parts > 1:
            b_spec = pl.BlockSpec((None, tk, tn), lambda i, j, k: (j // npj, k, j % npj))
        else:
            b_spec = pl.BlockSpec((tk, tn), lambda i, j, k: (k, j))
        dims = _TN
    if out_parts > 1:
        out_spec = pl.BlockSpec((None, tm, tn), lambda i, j, k: (j // npj, i, j % npj))
        out_shape = _S((out_parts, M, N // out_parts), out_dtype)
    else:
        out_spec = pl.BlockSpec((tm, tn), lambda i, j, k: (i, j))
        out_shape = _S((M, N), out_dtype)
    in_specs = [a_spec, b_spec]
    args = [a, b]
    if add is not None:
        in_specs.append(pl.BlockSpec((tm, tn), lambda i, j, k: (i, j)))
        args.append(add)
    if ln is not None:
        vec = pl.BlockSpec((1, N), lambda i, j, k: (0, 0))
        in_specs += [vec, vec]
        args += list(ln)
        out_spec = [out_spec] * 3
        out_shape = [_S((M, N), _F32), _S((M, N), _F32), _S((M, N), _MM)]
    n_ln = 2 + (add is not None)
    in_specs += [_ANY] * len(deps)
    args += list(deps)
    n_out = 1 if ln is None else 3

    def finish(r, refs, outs):
        if add is not None:
            r = r + add_scale * refs[2][...]
        if ln is None:
            outs[0][...] = r.astype(outs[0].dtype)
            return
        mu = jnp.mean(r, axis=-1, keepdims=True)
        rc = r - mu
        y = rc * lax.rsqrt(jnp.mean(rc * rc, axis=-1, keepdims=True) + LN_EPS) * refs[n_ln][...] + refs[n_ln + 1][...]
        outs[0][...] = r
        outs[1][...] = y
        outs[2][...] = y.astype(outs[2].dtype)

    def body_one(*refs):
        finish(_dot(refs[0][...], refs[1][...], dims), refs, refs[len(refs) - n_out:])

    def body_acc(*refs):
        acc = refs[-1]
        k = pl.program_id(2)

        @pl.when(k == 0)
        def _():
            acc[...] = jnp.zeros_like(acc)

        acc[...] += _dot(refs[0][...], refs[1][...], dims)

        @pl.when(k == nk - 1)
        def _():
            finish(acc[...], refs, refs[len(refs) - 1 - n_out:len(refs) - 1])

    return pl.pallas_call(
        body_one if nk == 1 else body_acc, name=name, grid=(M // tm, N // tn, nk), in_specs=in_specs,
        out_specs=out_spec, out_shape=out_shape,
        scratch_shapes=[] if nk == 1 else [pltpu.VMEM((tm, tn), _F32)],
        compiler_params=_params(("parallel", "parallel", "arbitrary")),
    )(*args)


def _cast(x2d, dtype, name, deps=()):
    R, C = x2d.shape
    tr = _tile(R, 512) if R % LANE == 0 else R

    def body(x_ref, *rest):
        rest[-1][...] = x_ref[...].astype(rest[-1].dtype)

    return pl.pallas_call(
        body, name=name, grid=(R // tr,), in_specs=[pl.BlockSpec((tr, C), lambda i: (i, 0))] + [_ANY] * len(deps),
        out_specs=pl.BlockSpec((tr, C), lambda i: (i, 0)), out_shape=_S((R, C), dtype),
        compiler_params=_params(("parallel",)),
    )(x2d, *deps)


def _ln_bwd(z, g, dy, name):
    T, D = z.shape
    tr = _tile(T, 256)

    def body(z_ref, g_ref, dy_ref, dz_ref, dzb_ref, dg_ref, db_ref):
        @pl.when(pl.program_id(0) == 0)
        def _():
            dg_ref[...] = jnp.zeros_like(dg_ref)
            db_ref[...] = jnp.zeros_like(db_ref)

        zz = z_ref[...]
        mu = jnp.mean(zz, axis=-1, keepdims=True)
        zc = zz - mu
        rstd = lax.rsqrt(jnp.mean(zc * zc, axis=-1, keepdims=True) + LN_EPS)
        xh = zc * rstd
        d = dy_ref[...]
        dg_ref[...] += jnp.sum(d * xh, axis=0, keepdims=True)
        db_ref[...] += jnp.sum(d, axis=0, keepdims=True)
        dxh = d * g_ref[...]
        dz = rstd * (dxh - jnp.mean(dxh, axis=-1, keepdims=True) - xh * jnp.mean(dxh * xh, axis=-1, keepdims=True))
        dz_ref[...] = dz
        dzb_ref[...] = dz.astype(dzb_ref.dtype)

    row = pl.BlockSpec((tr, D), lambda i: (i, 0))
    vec = pl.BlockSpec((1, D), lambda i: (0, 0))
    return pl.pallas_call(
        body, name=name, grid=(T // tr,), in_specs=[row, vec, row], out_specs=[row, row, vec, vec],
        out_shape=[_S((T, D), _F32), _S((T, D), _MM), _S((1, D), _F32), _S((1, D), _F32)],
        compiler_params=_params(("arbitrary",)),
    )(z, g, dy)


def _ln_loss_bwd(z, g, b, target, name):
    T, D = z.shape
    tr = _tile(T, 256)

    def body(z_ref, g_ref, b_ref, t_ref, loss_ref, dz_ref, dzb_ref, dg_ref, db_ref, lacc):
        i = pl.program_id(0)

        @pl.when(i == 0)
        def _():
            dg_ref[...] = jnp.zeros_like(dg_ref)
            db_ref[...] = jnp.zeros_like(db_ref)
            lacc[...] = jnp.zeros_like(lacc)

        zz = z_ref[...]
        mu = jnp.mean(zz, axis=-1, keepdims=True)
        zc = zz - mu
        rstd = lax.rsqrt(jnp.mean(zc * zc, axis=-1, keepdims=True) + LN_EPS)
        xh = zc * rstd
        err = xh * g_ref[...] + b_ref[...] - t_ref[...]
        lacc[...] += jnp.sum(err * err, axis=0, keepdims=True)
        d = err * (1.0 / D)
        dg_ref[...] += jnp.sum(d * xh, axis=0, keepdims=True)
        db_ref[...] += jnp.sum(d, axis=0, keepdims=True)
        dxh = d * g_ref[...]
        dz = rstd * (dxh - jnp.mean(dxh, axis=-1, keepdims=True) - xh * jnp.mean(dxh * xh, axis=-1, keepdims=True))
        dz_ref[...] = dz
        dzb_ref[...] = dz.astype(dzb_ref.dtype)

        @pl.when(i == pl.num_programs(0) - 1)
        def _():
            loss_ref[...] = jnp.sum(lacc[...], axis=-1, keepdims=True) * (0.5 / D)

    row = pl.BlockSpec((tr, D), lambda i: (i, 0))
    vec = pl.BlockSpec((1, D), lambda i: (0, 0))
    one = pl.BlockSpec((1, 1), lambda i: (0, 0))
    return pl.pallas_call(
        body, name=name, grid=(T // tr,), in_specs=[row, vec, vec, row], out_specs=[one, row, row, vec, vec],
        out_shape=[_S((1, 1), _F32), _S((T, D), _F32), _S((T, D), _MM), _S((1, D), _F32), _S((1, D), _F32)],
        scratch_shapes=[pltpu.VMEM((1, D), _F32)], compiler_params=_params(("arbitrary",)),
    )(z, g, b, target)


def _conv3(X, cw, cb):
    return cb + cw[2:3] * X + cw[1:2] * pltpu.roll(X, 1, 0) + cw[0:1] * pltpu.roll(X, 2, 0)


def _ffn_mid_fwd(h, cw, cb, name):
    _, T, F = h.shape
    tr = _tile(T, FFN_ROWS)
    tc = _tile(F, FFN_FWD_COLS)
    nb = tr // CONV_HALO

    rc = _tile(tr, FFN_CHUNK)
    lanes = [slice(cs * LANE, (cs + 1) * LANE) for cs in range(tc // LANE)]

    def body(h_ref, p_ref, cw_ref, cb_ref, o_ref, c_ref):
        i = pl.program_id(0)

        def work(r0, cols, X):
            hc = [_conv3(X[part], cw_ref[part, :, cols], cb_ref[part, :, cols])[CONV_HALO:] for part in range(2)]
            for part in range(2):
                c_ref[part, pl.ds(r0, rc), cols] = hc[part].astype(c_ref.dtype)
            a, v = hc
            o_ref[pl.ds(r0, rc), cols] = (a * _sigmoid(a) * v).astype(o_ref.dtype)

        for cols in lanes:
            work(0, cols, [jnp.concatenate([jnp.where(i == 0, 0.0, p_ref[part, :, cols]), h_ref[part, 0:rc, cols]], axis=0)
                           for part in range(2)])

        def chunk(c, carry):
            r0 = pl.multiple_of(c * rc, rc)
            for cols in lanes:
                work(r0, cols, [h_ref[part, pl.ds(r0 - CONV_HALO, rc + CONV_HALO), cols] for part in range(2)])
            return carry

        lax.fori_loop(1, tr // rc, chunk, 0)

    return pl.pallas_call(
        body, name=name, grid=(T // tr, F // tc),
        in_specs=[pl.BlockSpec((2, tr, tc), lambda i, j: (0, i, j)),
                  pl.BlockSpec((2, CONV_HALO, tc), lambda i, j: (0, jnp.maximum(i * nb - 1, 0), j)),
                  pl.BlockSpec((2, 3, tc), lambda i, j: (0, 0, j)),
                  pl.BlockSpec((2, 1, tc), lambda i, j: (0, 0, j))],
        out_specs=[pl.BlockSpec((tr, tc), lambda i, j: (i, j)), pl.BlockSpec((2, tr, tc), lambda i, j: (0, i, j))],
        out_shape=[_S((T, F), _MM), _S((2, T, F), _MM)],
        compiler_params=_params(("parallel", "parallel")),
    )(h, h, cw, cb)


def _ffn_mid_bwd(h, hc, dact, cw, name):
    _, T, F = h.shape
    tr = _tile(T, FFN_BWD_ROWS)
    tc = _tile(F, FFN_BWD_COLS)
    nb_c = tr // PACKED_ROWS
    rc = _tile(tr, FFN_CHUNK)
    n = rc + CONV_HALO

    def body(h_ref, c_ref, cn_ref, d_ref, dn_ref, cw_ref, dh_ref, dcw_ref, dcb_ref):
        i = pl.program_id(1)
        is_last = i == pl.num_programs(1) - 1

        @pl.when(i == 0)
        def _():
            dcw_ref[...] = jnp.zeros_like(dcw_ref)
            dcb_ref[...] = jnp.zeros_like(dcb_ref)

        def work(r0, cols, a, v, D):
            sg = _sigmoid(a)
            dhc = [D * v * sg * (1.0 + a * (1.0 - sg)), D * a * sg]
            for part in range(2):
                X = h_ref[part, pl.ds(r0, rc), cols]
                cwp = cw_ref[part, :, cols]
                dh = None
                for k in range(3):
                    g = (dhc[part] if k == 0 else pltpu.roll(dhc[part], n - k, 0))[0:rc]
                    term = cwp[2 - k:3 - k] * g
                    dh = term if dh is None else dh + term
                    dcw_ref[part, 2 - k:3 - k, cols] += jnp.sum(g * X, axis=0, keepdims=True)
                    if k == 0:
                        dcb_ref[part, :, cols] += jnp.sum(g, axis=0, keepdims=True)
                dh_ref[part, pl.ds(r0, rc), cols] = dh.astype(dh_ref.dtype)

        lanes = [slice(cs * LANE, (cs + 1) * LANE) for cs in range(tc // LANE)]

        def chunk(c, carry):
            r0 = pl.multiple_of(c * rc, rc)
            for cols in lanes:
                a, v = [c_ref[part, pl.ds(r0, rc + PACKED_ROWS), cols].astype(_F32)[0:n] for part in range(2)]
                work(r0, cols, a, v, d_ref[pl.ds(r0, rc + PACKED_ROWS), cols].astype(_F32)[0:n])
            return carry

        lax.fori_loop(0, tr // rc - 1, chunk, 0)
        r0 = tr - rc
        for cols in lanes:
            a, v = [jnp.concatenate([c_ref[part, r0:tr, cols].astype(_F32), cn_ref[part, :, cols].astype(_F32)[0:CONV_HALO]],
                                    axis=0) for part in range(2)]
            D = jnp.concatenate([d_ref[r0:tr, cols].astype(_F32),
                                 jnp.where(is_last, 0.0, dn_ref[:, cols].astype(_F32)[0:CONV_HALO])], axis=0)
            work(r0, cols, a, v, D)

    return pl.pallas_call(
        body, name=name, grid=(F // tc, T // tr),
        in_specs=[pl.BlockSpec((2, tr, tc), lambda j, i: (0, i, j)),
                  pl.BlockSpec((2, tr, tc), lambda j, i: (0, i, j)),
                  pl.BlockSpec((2, PACKED_ROWS, tc), lambda j, i: (0, jnp.minimum((i + 1) * nb_c, T // PACKED_ROWS - 1), j)),
                  pl.BlockSpec((tr, tc), lambda j, i: (i, j)),
                  pl.BlockSpec((PACKED_ROWS, tc), lambda j, i: (jnp.minimum((i + 1) * nb_c, T // PACKED_ROWS - 1), j)),
                  pl.BlockSpec((2, 3, tc), lambda j, i: (0, 0, j))],
        out_specs=[pl.BlockSpec((2, tr, tc), lambda j, i: (0, i, j)),
                   pl.BlockSpec((2, 3, tc), lambda j, i: (0, 0, j)),
                   pl.BlockSpec((2, 1, tc), lambda j, i: (0, 0, j))],
        out_shape=[_S((2, T, F), _MM), _S((2, 3, F), _F32), _S((2, 1, F), _F32)],
        compiler_params=_params(("parallel", "arbitrary")),
    )(h, hc, hc, dact, dact, cw)


def _ev_common(h_ref, hp_ref, lng_ref, lnb_ref, ws_ref, bias_ref, i, tr, W):
    H = W // A_HEAD
    u, gu = _gelu_and_grad(h_ref[0])
    v, gv = _gelu_and_grad(h_ref[1])
    mu = jnp.mean(v, axis=-1, keepdims=True)
    vc = v - mu
    rstd = lax.rsqrt(jnp.mean(vc * vc, axis=-1, keepdims=True) + LN_EPS)
    vhat = vc * rstd
    vb = (vhat * lng_ref[...] + lnb_ref[...]).astype(_MM)
    s_chunks = []
    for c in range(tr // A_CHUNK):
        r0 = c * A_CHUNK
        heads = [_dot(ws_ref[hd], vb[r0:r0 + A_CHUNK, hd * A_HEAD:(hd + 1) * A_HEAD]) for hd in range(H)]
        s_chunks.append(jnp.concatenate(heads, axis=1) + bias_ref[...])
    prev = jnp.where(i == 0, 0.0, hp_ref[...])
    X = jnp.concatenate([prev, h_ref[2]], axis=0)
    return u, gu, gv, rstd, vhat, vb, s_chunks, X


def _pool_inv_count(i, tr, rows, win):
    pos = i * tr + _row_index(rows) + 1
    return 1.0 / jnp.minimum(pos, win).astype(_F32)


def _pool_fwd(X, g, Wg, i, tr):
    xg = X[:, g * Wg:(g + 1) * Wg]
    s = xg
    for k in range(g + 1):
        s = s + pltpu.roll(s, 2 ** k, 0)
    return s[POOL_HALO:] * _pool_inv_count(i, tr, tr, 2 ** (g + 1)) - xg[POOL_HALO:]


def _ev_mid_fwd(h, lng, lnb, ws, bias, wp, sc, name):
    _, T, W = h.shape
    tr = _tile(T, 256)
    H = W // A_HEAD
    Wg = W // B_GROUPS
    nb = tr // POOL_HALO

    def body(h_ref, hp_ref, lng_ref, lnb_ref, ws_ref, bias_ref, wp_ref, sc_ref, o_ref):
        i = pl.program_id(0)
        u, _, _, _, _, _, s_chunks, X = _ev_common(h_ref, hp_ref, lng_ref, lnb_ref, ws_ref, bias_ref, i, tr, W)
        for c, s in enumerate(s_chunks):
            r0 = c * A_CHUNK
            o_ref[r0:r0 + A_CHUNK, 0:W] = (u[r0:r0 + A_CHUNK] * s).astype(o_ref.dtype)
        for g in range(B_GROUPS):
            p = _pool_fwd(X, g, Wg, i, tr)
            y = _dot(p.astype(_MM), wp_ref[g]) * sc_ref[:, g * Wg:(g + 1) * Wg]
            o_ref[:, W + g * Wg:W + (g + 1) * Wg] = y.astype(o_ref.dtype)

    vec = pl.BlockSpec((1, W), lambda i: (0, 0))
    return pl.pallas_call(
        body, name=name, grid=(T // tr,),
        in_specs=[pl.BlockSpec((3, tr, W), lambda i: (0, i, 0)),
                  pl.BlockSpec((None, POOL_HALO, W), lambda i: (2, jnp.maximum(i * nb - 1, 0), 0)),
                  vec, vec,
                  pl.BlockSpec((H, A_CHUNK, A_CHUNK), lambda i: (0, 0, 0)),
                  pl.BlockSpec((A_CHUNK, W), lambda i: (0, 0)),
                  pl.BlockSpec((B_GROUPS, Wg, Wg), lambda i: (0, 0, 0)),
                  vec],
        out_specs=pl.BlockSpec((tr, 2 * W), lambda i: (i, 0)), out_shape=_S((T, 2 * W), _MM),
        compiler_params=_params(("parallel",)),
    )(h, h, lng, lnb, ws, bias, wp, sc)


def _ev_mid_bwd(h, dy, lng, lnb, ws, bias, wp, sc, name):
    _, T, W = h.shape
    tr = _tile(T, 256)
    H = W // A_HEAD
    Wg = W // B_GROUPS
    nb = tr // POOL_HALO
    last_blk = T // POOL_HALO - 1
    n = tr + POOL_HALO

    def body(h_ref, hp_ref, dy_ref, dyn_ref, lng_ref, lnb_ref, ws_ref, bias_ref, wp_ref, sc_ref,
             dh_ref, dws_ref, dbias_ref, dlng_ref, dlnb_ref, dwp_ref, dsc_ref):
        i = pl.program_id(0)

        @pl.when(i == 0)
        def _():
            for r in (dws_ref, dbias_ref, dlng_ref, dlnb_ref, dwp_ref, dsc_ref):
                r[...] = jnp.zeros_like(r)

        u, gu, gv, rstd, vhat, vb, s_chunks, X = _ev_common(h_ref, hp_ref, lng_ref, lnb_ref, ws_ref, bias_ref, i, tr, W)
        rr = lax.broadcasted_iota(jnp.int32, (A_CHUNK, A_CHUNK), 0)
        cc = lax.broadcasted_iota(jnp.int32, (A_CHUNK, A_CHUNK), 1)
        tril = rr >= cc
        du_chunks, dvln_chunks = [], []
        for c, s in enumerate(s_chunks):
            r0 = c * A_CHUNK
            dya = dy_ref[r0:r0 + A_CHUNK, 0:W]
            du_chunks.append(dya * s)
            ds = dya * u[r0:r0 + A_CHUNK]
            dbias_ref[...] += ds
            dsb = ds.astype(_MM)
            heads = []
            for hd in range(H):
                cols = slice(hd * A_HEAD, (hd + 1) * A_HEAD)
                dws_ref[hd] += jnp.where(tril, _dot(dsb[:, cols], vb[r0:r0 + A_CHUNK, cols], _NT), 0.0)
                heads.append(_dot(ws_ref[hd], dsb[:, cols], _TN))
            dvln_chunks.append(jnp.concatenate(heads, axis=1))
        du = jnp.concatenate(du_chunks, axis=0)
        dvln = jnp.concatenate(dvln_chunks, axis=0)
        dlng_ref[...] += jnp.sum(dvln * vhat, axis=0, keepdims=True)
        dlnb_ref[...] += jnp.sum(dvln, axis=0, keepdims=True)
        dxh = dvln * lng_ref[...]
        dv = rstd * (dxh - jnp.mean(dxh, axis=-1, keepdims=True) - vhat * jnp.mean(dxh * vhat, axis=-1, keepdims=True))
        dh_ref[0] = (du * gu).astype(dh_ref.dtype)
        dh_ref[1] = (dv * gv).astype(dh_ref.dtype)

        dyb = dy_ref[:, W:2 * W]
        dyb_full = jnp.concatenate([dyb, jnp.where(i == pl.num_programs(0) - 1, 0.0, dyn_ref[...])], axis=0)
        for g in range(B_GROUPS):
            cols = slice(g * Wg, (g + 1) * Wg)
            pb = _pool_fwd(X, g, Wg, i, tr).astype(_MM)
            ypre = _dot(pb, wp_ref[g])
            dsc_ref[:, cols] += jnp.sum(dyb[:, cols] * ypre, axis=0, keepdims=True)
            dyp = (dyb_full[:, cols] * sc_ref[:, cols]).astype(_MM)
            dwp_ref[g] += _dot(pb, dyp[0:tr], _TN)
            dp = _dot(dyp, wp_ref[g], _NT)
            s = dp * _pool_inv_count(i, tr, n, 2 ** (g + 1))
            for k in range(g + 1):
                s = s + pltpu.roll(s, n - 2 ** k, 0)
            dh_ref[2, :, cols] = (s[0:tr] - dp[0:tr]).astype(dh_ref.dtype)

    vec = pl.BlockSpec((1, W), lambda i: (0, 0))
    ws_spec = pl.BlockSpec((H, A_CHUNK, A_CHUNK), lambda i: (0, 0, 0))
    bias_spec = pl.BlockSpec((A_CHUNK, W), lambda i: (0, 0))
    wp_spec = pl.BlockSpec((B_GROUPS, Wg, Wg), lambda i: (0, 0, 0))
    return pl.pallas_call(
        body, name=name, grid=(T // tr,),
        in_specs=[pl.BlockSpec((3, tr, W), lambda i: (0, i, 0)),
                  pl.BlockSpec((None, POOL_HALO, W), lambda i: (2, jnp.maximum(i * nb - 1, 0), 0)),
                  pl.BlockSpec((tr, 2 * W), lambda i: (i, 0)),
                  pl.BlockSpec((POOL_HALO, W), lambda i: (jnp.minimum((i + 1) * nb, last_blk), 1)),
                  vec, vec, ws_spec, bias_spec, wp_spec, vec],
        out_specs=[pl.BlockSpec((3, tr, W), lambda i: (0, i, 0)), ws_spec, bias_spec, vec, vec, wp_spec, vec],
        out_shape=[_S((3, T, W), _MM), _S((H, A_CHUNK, A_CHUNK), _F32), _S((A_CHUNK, W), _F32), _S((1, W), _F32),
                   _S((1, W), _F32), _S((B_GROUPS, Wg, Wg), _F32), _S((1, W), _F32)],
        compiler_params=_params(("arbitrary",)),
    )(h, h, dy, dy, lng, lnb, ws, bias, wp, sc)


def _chunk_cumsum(x, rin):
    s = 1
    while s < C_CHUNK:
        x = x + jnp.where(rin >= s, pltpu.roll(x, s, 0), 0.0)
        s *= 2
    return x


def _chunk_revcumsum(x, rin):
    n = x.shape[0]
    s = 1
    while s < C_CHUNK:
        x = x + jnp.where(rin + s < C_CHUNK, pltpu.roll(x, n - s, 0), 0.0)
        s *= 2
    return x


def _hgrn_gates(q, fl, lb, tr, tc):
    nch = tr // C_CHUNK
    sq = _sigmoid(q)
    sf = _sigmoid_rel(fl)
    f = lb + (1.0 - lb) * sf
    logf = jnp.log(f)
    rin = _row_index(tr) % C_CHUNK
    b = _chunk_cumsum(logf, rin)
    tot3 = jnp.sum(logf.reshape(nch, C_CHUNK, tc), axis=1, keepdims=True)
    eb = jnp.exp(b)
    enb = jnp.exp(-b)
    ekb = jnp.exp(tot3 - b.reshape(nch, C_CHUNK, tc)).reshape(tr, tc)
    return sq, sf, f, rin, tot3, eb, enb, ekb


def _hgrn_prep_fwd(h, lb, name):
    _, T, D = h.shape
    tr = _tile(T, 512)
    tc = _tile(D, 512)
    nch = tr // C_CHUNK

    def body(q_ref, f_ref, v_ref, lb_ref, qd_ref, kd_ref, ke_ref, vb_ref, dec_ref):
        q = q_ref[...]
        sq, _, f, _, tot3, eb, enb, ekb = _hgrn_gates(q, f_ref[...], lb_ref[...], tr, tc)
        kk = 1.0 - f
        qd_ref[...] = (q * sq * eb).astype(qd_ref.dtype)
        kd_ref[...] = (kk * enb).astype(kd_ref.dtype)
        ke_ref[...] = (kk * ekb).astype(ke_ref.dtype)
        vb_ref[...] = v_ref[...].astype(vb_ref.dtype)
        dec_ref[...] = jnp.exp(tot3).reshape(nch, tc)

    def part(p):
        return pl.BlockSpec((None, tr, tc), lambda i, j: (p, i, j))

    blk = pl.BlockSpec((tr, tc), lambda i, j: (i, j))
    return pl.pallas_call(
        body, name=name, grid=(T // tr, D // tc),
        in_specs=[part(0), part(1), part(2), pl.BlockSpec((1, tc), lambda i, j: (0, j))],
        out_specs=[blk, blk, blk, blk, pl.BlockSpec((nch, tc), lambda i, j: (i, j))],
        out_shape=[_S((T, D), _MM)] * 4 + [_S((T // C_CHUNK, D), _F32)],
        compiler_params=_params(("parallel", "parallel")),
    )(h, h, h, lb)


def _tril_mask():
    rr = lax.broadcasted_iota(jnp.int32, (C_CHUNK, C_CHUNK), 0)
    cc = lax.broadcasted_iota(jnp.int32, (C_CHUNK, C_CHUNK), 1)
    return rr >= cc


def _hgrn_scan_fwd(qd, kd, ke, vb, dec, h, ng, name):
    T, D = qd.shape
    NH = D // C_HEAD
    N = T // C_CHUNK

    def body(qd_ref, kd_ref, ke_ref, vb_ref, dec_ref, g_ref, ng_ref, o_ref, y_ref, st_ref):
        mask = _tril_mask()

        per_trip = math.gcd(N, SCAN_UNROLL)

        def trip(i, St):
            ahead = []
            for u in range(per_trip):
                n = i * per_trip + u
                r = pl.ds(pl.multiple_of(n * C_CHUNK, C_CHUNK), C_CHUNK)
                Qd, Kd, Ke, V = qd_ref[r, :], kd_ref[r, :], ke_ref[r, :], vb_ref[r, :]
                att = jnp.where(mask, _dot(Qd, Kd, _NT), 0.0).astype(_MM)
                ahead.append((n, r, _dot(att, V), _dot(V, Ke, _TN)))
            for n, r, o_intra, update in ahead:
                o_ref[r, :] = o_intra + _dot(qd_ref[r, :], St.astype(_MM), _NT)
                st_ref[n] = St
                St = St * dec_ref[pl.ds(n, 1), :] + update
            return St

        lax.fori_loop(0, N // per_trip, trip, jnp.zeros((C_HEAD, C_HEAD), _F32))
        o = o_ref[...]
        r = lax.rsqrt(jnp.mean(o * o, axis=-1, keepdims=True) + LN_EPS)
        y_ref[...] = (o * r * ng_ref[...] * _sigmoid(g_ref[...])).astype(y_ref.dtype)

    col = pl.BlockSpec((T, C_HEAD), lambda j: (0, j))
    return pl.pallas_call(
        body, name=name, grid=(NH,),
        in_specs=[col, col, col, col, pl.BlockSpec((N, C_HEAD), lambda j: (0, j)),
                  pl.BlockSpec((None, T, C_HEAD), lambda j: (3, 0, j)), pl.BlockSpec((1, C_HEAD), lambda j: (0, j))],
        out_specs=[col, col, pl.BlockSpec((None, N, C_HEAD, C_HEAD), lambda j: (j, 0, 0, 0))],
        out_shape=[_S((T, D), _F32), _S((T, D), _MM), _S((NH, N, C_HEAD, C_HEAD), _F32)],
        compiler_params=_params(("parallel",)),
    )(qd, kd, ke, vb, dec, h, ng)


def _hgrn_scan_bwd(qd, kd, ke, vb, dec, st, o, h, ng, dy, name):
    T, D = qd.shape
    NH = D // C_HEAD
    N = T // C_CHUNK

    def body(qd_ref, kd_ref, ke_ref, vb_ref, dec_ref, st_ref, o_ref, g_ref, ng_ref, dy_ref,
             dqd_ref, dkd_ref, dke_ref, dv_ref, dgate_ref, ddec_ref, dng_ref, do_s):
        o = o_ref[...]
        r = lax.rsqrt(jnp.mean(o * o, axis=-1, keepdims=True) + LN_EPS)
        oh = o * r
        gn = ng_ref[...]
        sg = _sigmoid(g_ref[...])
        d = dy_ref[...]
        dyn = d * sg
        dgate_ref[...] = (d * oh * gn * sg * (1.0 - sg)).astype(dgate_ref.dtype)
        dng_ref[...] = jnp.sum(dyn * oh, axis=0, keepdims=True)
        doh = dyn * gn
        do_s[...] = (r * (doh - oh * jnp.mean(doh * oh, axis=-1, keepdims=True))).astype(do_s.dtype)
        mask = _tril_mask()

        per_trip = math.gcd(N, SCAN_UNROLL)

        def trip(i, dSt):
            ahead = []
            for u in range(per_trip):
                n = N - 1 - (i * per_trip + u)
                rws = pl.ds(pl.multiple_of(n * C_CHUNK, C_CHUNK), C_CHUNK)
                Qd, Kd, V, dO = qd_ref[rws, :], kd_ref[rws, :], vb_ref[rws, :], do_s[rws, :]
                att = jnp.where(mask, _dot(Qd, Kd, _NT), 0.0).astype(_MM)
                dA = jnp.where(mask, _dot(dO, V, _NT), 0.0).astype(_MM)
                dqd_ref[rws, :] = _dot(dA, Kd) + _dot(dO, st_ref[n].astype(_MM))
                dkd_ref[rws, :] = _dot(dA, Qd, _TN)
                ahead.append((n, rws, _dot(att, dO, _TN), _dot(dO, Qd, _TN)))
            for n, rws, dv_intra, d_state in ahead:
                dStb = dSt.astype(_MM)
                dv_ref[rws, :] = (dv_intra + _dot(ke_ref[rws, :], dStb, _NT)).astype(dv_ref.dtype)
                dke_ref[rws, :] = _dot(vb_ref[rws, :], dStb)
                ddec_ref[pl.ds(n, 1), :] = jnp.sum(dSt * st_ref[n], axis=0, keepdims=True)
                dSt = dSt * dec_ref[pl.ds(n, 1), :] + d_state
            return dSt

        lax.fori_loop(0, N // per_trip, trip, jnp.zeros((C_HEAD, C_HEAD), _F32))

    col = pl.BlockSpec((T, C_HEAD), lambda j: (0, j))
    chk = pl.BlockSpec((N, C_HEAD), lambda j: (0, j))
    one = pl.BlockSpec((1, C_HEAD), lambda j: (0, j))
    return pl.pallas_call(
        body, name=name, grid=(NH,),
        in_specs=[col, col, col, col, chk, pl.BlockSpec((None, N, C_HEAD, C_HEAD), lambda j: (j, 0, 0, 0)), col,
                  pl.BlockSpec((None, T, C_HEAD), lambda j: (3, 0, j)), one, col],
        out_specs=[col, col, col, col, col, chk, one],
        out_shape=[_S((T, D), _F32)] * 3 + [_S((T, D), _MM)] * 2 + [_S((N, D), _F32), _S((1, D), _F32)],
        scratch_shapes=[pltpu.VMEM((T, C_HEAD), _MM)],
        compiler_params=_params(("parallel",)),
    )(qd, kd, ke, vb, dec, st, o, h, ng, dy)


def _hgrn_prep_bwd(h, lb, dqd, dkd, dke, dv, dgate, ddec, name):
    _, T, D = h.shape
    tr = _tile(T, 512)
    tc = _tile(D, 256)
    nch = tr // C_CHUNK

    def body(q_ref, f_ref, lb_ref, dqd_ref, dkd_ref, dke_ref, dv_ref, dgate_ref, ddec_ref, dh_ref, dlb_ref):
        @pl.when(pl.program_id(1) == 0)
        def _():
            dlb_ref[...] = jnp.zeros_like(dlb_ref)

        q = q_ref[...]
        lb = lb_ref[...]
        sq, sf, f, rin, tot3, eb, enb, ekb = _hgrn_gates(q, f_ref[...], lb, tr, tc)
        kk = 1.0 - f
        dQd, dKd, dKe = dqd_ref[...], dkd_ref[...], dke_ref[...]
        tq = dQd * eb
        tkd = dKd * enb
        tke = dKe * ekb
        ke_term = tke * kk
        db = tq * (q * sq) - tkd * kk - ke_term
        dtot3 = (jnp.sum(ke_term.reshape(nch, C_CHUNK, tc), axis=1, keepdims=True)
                 + (ddec_ref[...] * jnp.exp(tot3).reshape(nch, tc)).reshape(nch, 1, tc))
        dlogf = (_chunk_revcumsum(db, rin).reshape(nch, C_CHUNK, tc) + dtot3).reshape(tr, tc)
        df = dlogf / f - (tkd + tke)
        dh_ref[0] = (tq * sq * (1.0 + q * (1.0 - sq))).astype(dh_ref.dtype)
        dh_ref[1] = (df * (1.0 - lb) * sf * (1.0 - sf)).astype(dh_ref.dtype)
        dh_ref[2] = dv_ref[...]
        dh_ref[3] = dgate_ref[...]
        dlb_ref[...] += jnp.sum(df * (1.0 - sf), axis=0, keepdims=True)

    def part(p):
        return pl.BlockSpec((None, tr, tc), lambda j, i: (p, i, j))

    blk = pl.BlockSpec((tr, tc), lambda j, i: (i, j))
    vec = pl.BlockSpec((1, tc), lambda j, i: (0, j))
    return pl.pallas_call(
        body, name=name, grid=(D // tc, T // tr),
        in_specs=[part(0), part(1), vec, blk, blk, blk, blk, blk, pl.BlockSpec((nch, tc), lambda j, i: (i, j))],
        out_specs=[pl.BlockSpec((4, tr, tc), lambda j, i: (0, i, j)), vec],
        out_shape=[_S((4, T, D), _MM), _S((1, D), _F32)],
        compiler_params=_params(("parallel", "arbitrary")),
    )(h, h, lb, dqd, dkd, dke, dv, dgate, ddec)


def _sum_in_device_order(me1, own, land, name):
    R, C = own.shape
    tr = _tile(R, 256)

    def body(me_ref, own_ref, land_ref, o_ref):
        me = me_ref[0]
        g = None
        for j in range(N_DEV):
            slot = jnp.maximum(jnp.bitwise_xor(me, j) - 1, 0)
            p = jnp.where(me == j, own_ref[...], land_ref[slot])
            g = p if g is None else g + p
        o_ref[...] = g

    return pl.pallas_call(
        body, name=name,
        grid_spec=pltpu.PrefetchScalarGridSpec(
            num_scalar_prefetch=1, grid=(R // tr,),
            in_specs=[pl.BlockSpec((tr, C), lambda i, me: (i, 0)), pl.BlockSpec((N_DEV - 1, tr, C), lambda i, me: (0, i, 0))],
            out_specs=pl.BlockSpec((tr, C), lambda i, me: (i, 0))),
        out_shape=_S((R, C), _F32), compiler_params=_params(("parallel",)),
    )(me1, own, land)


def _adamw(parts, w, m, v, name):
    P, R, C = parts.shape
    tr = _tile(R, 128) if R % LANE == 0 else R

    def body(p_ref, w_ref, m_ref, v_ref, g_ref, d_ref, nm_ref, nv_ref):
        g = p_ref[0].astype(_F32)
        for s in range(1, P):
            g = g + p_ref[s].astype(_F32)
        nm = ADAM_B1 * m_ref[...] + (1.0 - ADAM_B1) * g
        nv = ADAM_B2 * v_ref[...] + (1.0 - ADAM_B2) * (g * g)
        m_hat = nm / (1.0 - ADAM_B1 ** ADAM_STEP)
        v_hat = nv / (1.0 - ADAM_B2 ** ADAM_STEP)
        g_ref[...] = g
        d_ref[...] = -ADAM_LR * (m_hat / (jnp.sqrt(v_hat) + ADAM_EPS) + ADAM_WD * w_ref[...])
        nm_ref[...] = nm
        nv_ref[...] = nv

    blk = pl.BlockSpec((tr, C), lambda i: (i, 0))
    return pl.pallas_call(
        body, name=name, grid=(R // tr,), in_specs=[pl.BlockSpec((P, tr, C), lambda i: (0, i, 0)), blk, blk, blk],
        out_specs=[blk] * 4, out_shape=[_S((R, C), _F32)] * 4, compiler_params=_params(("parallel",)),
    )(parts, w, m, v)


def _exchange(name, srcs, out_shapes, jobs, deps=()):
    ns, nj = len(srcs), len(jobs)

    nd = len(deps)

    def body(*refs):
        ins, outs = refs[:ns], refs[ns + nd:ns + nd + len(out_shapes)]
        send_sems, recv_sems, local_sems = refs[-3:]
        x, y, c = lax.axis_index("x"), lax.axis_index("y"), lax.axis_index("c")
        me = 4 * x + 2 * y + c
        local = []
        for ji, (si, src_fn, di, dst_fn) in enumerate(jobs):
            cp = pltpu.make_async_copy(src_fn(ins[si], me, me), dst_fn(outs[di], me), local_sems.at[ji])
            cp.start()
            local.append(cp)
        remote = []
        for k in range(1, N_DEV):
            px, py, pc = (x + (k >> 2)) % 2, (y + ((k >> 1) & 1)) % 2, (c + (k & 1)) % 2
            to = 4 * px + 2 * py + pc
            for ji, (si, src_fn, di, dst_fn) in enumerate(jobs):
                sem = (k - 1) * nj + ji
                cp = pltpu.make_async_remote_copy(
                    src_ref=src_fn(ins[si], me, to), dst_ref=dst_fn(outs[di], me),
                    send_sem=send_sems.at[sem], recv_sem=recv_sems.at[sem],
                    device_id=(px, py, pc), device_id_type=pl.DeviceIdType.MESH)
                cp.start()
                remote.append(cp)
        for cp in remote:
            cp.wait_recv()
        for cp in remote:
            cp.wait_send()
        for cp in local:
            cp.wait()

    hbm = pl.BlockSpec(memory_space=pltpu.HBM)
    return pl.pallas_call(
        body, name=name, in_specs=[hbm] * ns + [_ANY] * nd, out_specs=[hbm] * len(out_shapes), out_shape=list(out_shapes),
        scratch_shapes=[pltpu.SemaphoreType.DMA(((N_DEV - 1) * nj,)), pltpu.SemaphoreType.DMA(((N_DEV - 1) * nj,)),
                        pltpu.SemaphoreType.DMA((nj,))],
    )(*srcs, *deps)


def _whole(ref, me, to):
    return ref


def _slot_job(i, o):
    def dst(ref, me):
        return ref.at[me]
    return (i, _whole, o, dst)


_HBM = pl.BlockSpec(memory_space=pltpu.HBM)
_SEM = pl.BlockSpec(memory_space=pltpu.SEMAPHORE)
_ANY = pl.BlockSpec(memory_space=pl.ANY)
_N_PEER = N_DEV - 1


def _split_params():
    return pltpu.CompilerParams(has_side_effects=pltpu.SideEffectType.DATAFLOW_SIDE_EFFECTING)


def _blk(ref, axis, n, idx):
    if axis is None:
        return ref
    return ref.at[tuple([slice(None)] * axis + [pl.ds(pl.multiple_of(idx * n, n), n)])]


def _peer(k):
    x, y, c = lax.axis_index("x"), lax.axis_index("y"), lax.axis_index("c")
    px, py, pc = (x + (k >> 2)) % 2, (y + ((k >> 1) & 1)) % 2, (c + (k & 1)) % 2
    return (px, py, pc), 4 * px + 2 * py + pc, 4 * x + 2 * y + c


def _row_tile(rows, pref):
    best = None
    for d in range(16, min(rows, pref) + 1, 16):
        if rows % d == 0:
            best = d
    return best if best is not None else rows


def _place(w, me1, axis, name, layer=None, deps=()):
    R, C = w.shape[-2:]
    tr = _row_tile(R, 512)
    nb = R // tr
    lead = () if layer is None else (None,)
    pre = () if layer is None else (layer,)

    def body(me_ref, w_ref, *rest):
        rest[-1][...] = w_ref[...].astype(rest[-1].dtype)

    if axis == 1:
        out_spec = pl.BlockSpec((tr, C), lambda i, me: (i, me[0]))
        out_shape = _S((R, N_DEV * C), _MM)
    else:
        out_spec = pl.BlockSpec((tr, C), lambda i, me: (me[0] * nb + i, 0))
        out_shape = _S((N_DEV * R, C), _MM)
    return pl.pallas_call(
        body, name=name,
        grid_spec=pltpu.PrefetchScalarGridSpec(
            num_scalar_prefetch=1, grid=(nb,),
            in_specs=[pl.BlockSpec(lead + (tr, C), lambda i, me: pre + (i, 0))] + [_ANY] * len(deps), out_specs=out_spec),
        out_shape=out_shape, compiler_params=_params(("parallel",)),
    )(me1, w, *deps)


_SIBLING = 1
_CHIPS = (2, 4, 6)
_VMEM_TOKEN = pl.BlockSpec(memory_space=pltpu.VMEM)


def _remote(ref_blk, send_sem, recv_sem, dev):
    return pltpu.make_async_remote_copy(src_ref=ref_blk, dst_ref=ref_blk, send_sem=send_sem, recv_sem=recv_sem,
                                        device_id=dev, device_id_type=pl.DeviceIdType.MESH)


def _gather_start(name, full, axis, n):
    def body(f_ref, send, recv, f_out, token):
        for i, k in enumerate((_SIBLING,) + _CHIPS):
            dev, _, me = _peer(k)
            _remote(_blk(f_ref, axis, n, me), send.at[i], recv.at[i], dev).start()
        token[...] = jnp.zeros_like(token)

    return pl.pallas_call(
        body, name=name,
        out_shape=(pltpu.SemaphoreType.DMA((4,)), pltpu.SemaphoreType.DMA((4,)), pltpu.HBM(full.shape, full.dtype),
                   _S((8, LANE), _F32)),
        in_specs=(_HBM,), out_specs=(_SEM, _SEM, _HBM, _VMEM_TOKEN),
        input_output_aliases={0: 2}, compiler_params=_split_params(),
    )(pltpu.with_memory_space_constraint(full, pltpu.HBM))


def _gather_forward(name, full, axis, n, recv, after):
    after = tuple(after) if isinstance(after, (tuple, list)) else (after,)

    def body(f_ref, recv_r, *rest):
        send2, recv2, f_out, token = rest[-4:]
        sib, _, _ = _peer(_SIBLING)
        for i, k in enumerate(_CHIPS):
            dev, frm, _ = _peer(k)
            blk = _blk(f_ref, axis, n, frm)
            _remote(blk, send2.at[i], recv_r.at[1 + i], dev).wait_recv()
            _remote(blk, send2.at[i], recv2.at[i], sib).start()
        token[...] = jnp.zeros_like(token)

    return pl.pallas_call(
        body, name=name,
        out_shape=(pltpu.SemaphoreType.DMA((3,)), pltpu.SemaphoreType.DMA((3,)), pltpu.HBM(full.shape, full.dtype),
                   _S((8, LANE), _F32)),
        in_specs=(_HBM, _SEM) + (_ANY,) * len(after), out_specs=(_SEM, _SEM, _HBM, _VMEM_TOKEN),
        input_output_aliases={0: 2}, compiler_params=_split_params(),
    )(full, recv, *after)


def _gather_wait(name, full, axis, n, send, recv, send2, recv2, after):
    def body(f_ref, send_r, recv_r, send2_r, recv2_r, after_ref, f_out):
        sib, _, me = _peer(_SIBLING)
        blk = _blk(f_ref, axis, n, me)
        for i in range(4):
            _remote(blk, send_r.at[i], recv_r.at[0], sib).wait_send()
        _remote(blk, send_r.at[0], recv_r.at[0], sib).wait_recv()
        for i in range(3):
            cp = _remote(blk, send2_r.at[i], recv2_r.at[i], sib)
            cp.wait_send()
            cp.wait_recv()

    return pl.pallas_call(
        body, name=name, out_shape=pltpu.HBM(full.shape, full.dtype),
        in_specs=(_HBM, _SEM, _SEM, _SEM, _SEM, _ANY), out_specs=_HBM,
        input_output_aliases={0: 0}, compiler_params=_split_params(),
    )(full, send, recv, send2, recv2, after)


def _scatter_start(name, dw, axis, n):
    shard = tuple(n if a == axis else d for a, d in enumerate(dw.shape))
    land = lax.empty((_N_PEER,) + shard, dw.dtype)

    def body(dw_ref, land_ref, send, recv, dw_out, land_out, token):
        for k in range(1, N_DEV):
            dev, to, _ = _peer(k)
            pltpu.make_async_remote_copy(
                src_ref=_blk(dw_ref, axis, n, to), dst_ref=land_ref.at[k - 1], send_sem=send.at[k - 1],
                recv_sem=recv.at[k - 1], device_id=dev, device_id_type=pl.DeviceIdType.MESH).start()
        token[...] = jnp.zeros_like(token)

    return pl.pallas_call(
        body, name=name,
        out_shape=(pltpu.SemaphoreType.DMA((_N_PEER,)), pltpu.SemaphoreType.DMA((_N_PEER,)),
                   pltpu.HBM(dw.shape, dw.dtype), pltpu.HBM(land.shape, land.dtype), _S((8, LANE), _F32)),
        in_specs=(_HBM, _HBM), out_specs=(_SEM, _SEM, _HBM, _HBM, pl.BlockSpec(memory_space=pltpu.VMEM)),
        input_output_aliases={0: 2, 1: 3}, compiler_params=_split_params(),
    )(pltpu.with_memory_space_constraint(dw, pltpu.HBM), pltpu.with_memory_space_constraint(land, pltpu.HBM))


def _scatter_wait(name, items, after):
    ne = len(items)
    after = tuple(after) if isinstance(after, (tuple, list)) else (after,)

    def body(*refs):
        for e, (_, _, _, _, axis, n) in enumerate(items):
            dw_ref, land_ref, send_r, recv_r = refs[4 * e:4 * e + 4]
            for k in range(1, N_DEV):
                dev, to, _ = _peer(k)
                cp = pltpu.make_async_remote_copy(
                    src_ref=_blk(dw_ref, axis, n, to), dst_ref=land_ref.at[k - 1], send_sem=send_r.at[k - 1],
                    recv_sem=recv_r.at[k - 1], device_id=dev, device_id_type=pl.DeviceIdType.MESH)
                cp.wait_send()
                cp.wait_recv()

    args, out_shape = [], []
    for dw, land, send, recv, _, _ in items:
        args += [dw, land, send, recv]
        out_shape += [pltpu.HBM(dw.shape, dw.dtype), pltpu.HBM(land.shape, land.dtype)]
    res = pl.pallas_call(
        body, name=name, out_shape=tuple(out_shape),
        in_specs=(_HBM, _HBM, _SEM, _SEM) * ne + (_ANY,) * len(after), out_specs=(_HBM,) * (2 * ne),
        input_output_aliases={4 * e + j: 2 * e + j for e in range(ne) for j in range(2)},
        compiler_params=_split_params(),
    )(*args, *after)
    return [(res[2 * e], res[2 * e + 1]) for e in range(ne)]


def _adamw_big(me1, dw, land, w, m, v, axis, n, name, layer=None, into=None):
    R, C = land.shape[1:]
    tr = _row_tile(R, 128)
    nb = R // tr
    lead = () if layer is None else (None,)
    pre = () if layer is None else (layer,)

    def body(me_ref, own_ref, land_ref, w_ref, m_ref, v_ref, *rest):
        g_ref, d_ref, nm_ref, nv_ref = rest[-4:]
        g = own_ref[...].astype(_F32)
        for s in range(_N_PEER):
            g = g + land_ref[s].astype(_F32)
        nm = ADAM_B1 * m_ref[...] + (1.0 - ADAM_B1) * g
        nv = ADAM_B2 * v_ref[...] + (1.0 - ADAM_B2) * (g * g)
        m_hat = nm / (1.0 - ADAM_B1 ** ADAM_STEP)
        v_hat = nv / (1.0 - ADAM_B2 ** ADAM_STEP)
        g_ref[...] = g
        d_ref[...] = -ADAM_LR * (m_hat / (jnp.sqrt(v_hat) + ADAM_EPS) + ADAM_WD * w_ref[...])
        nm_ref[...] = nm
        nv_ref[...] = nv

    if axis == 1:
        own_spec = pl.BlockSpec((tr, C), lambda i, me: (i, me[0]))
    else:
        own_spec = pl.BlockSpec((tr, C), lambda i, me: (me[0] * nb + i, 0))
    blk = pl.BlockSpec(lead + (tr, C), lambda i, me: pre + (i, 0))
    in_specs = [own_spec, pl.BlockSpec((_N_PEER, tr, C), lambda i, me: (0, i, 0)), blk, blk, blk]
    args = [me1, dw, land, w, m, v]
    aliases = {}
    if into is not None:
        in_specs += [_ANY] * 4
        aliases = {6 + j: j for j in range(4)}
        args += list(into)
    return pl.pallas_call(
        body, name=name,
        grid_spec=pltpu.PrefetchScalarGridSpec(num_scalar_prefetch=1, grid=(nb,), in_specs=in_specs, out_specs=[blk] * 4),
        out_shape=[_S(w.shape, _F32)] * 4, input_output_aliases=aliases, compiler_params=_params(("parallel",)),
    )(*args)


def _pack(arrs):
    flat = jnp.concatenate([a.reshape(-1).astype(_F32) for a in arrs])
    pad = (-flat.shape[0]) % (LANE * LANE)
    return jnp.pad(flat, (0, pad)).reshape(-1, LANE)


def _unpack(mat, shapes):
    flat = mat.reshape(-1)
    out, off = [], 0
    for s in shapes:
        n = 1
        for d in s:
            n *= d
        out.append(flat[off:off + n].reshape(s))
        off += n
    return out


def _lb_of(lb_param):
    lb_all = jnp.cumsum(jax.nn.softmax(lb_param.astype(_F32), axis=0), axis=0)
    return (lb_all - lb_all[0])[1:2]


def kernel(x, ev_w_in, ev_ln_v_g, ev_ln_v_b, ev_w_s, ev_b_s, ev_w_pool, ev_pool_scale, ev_w_out, od_w_in, od_norm_g, od_w_out, lb_param, ffn_w_up, ffn_conv_w, ffn_conv_b, ffn_w_down, ln1_g, ln1_b, ln2_g, ln2_b, loss_target, m_ev_w_in, m_ev_ln_v_g, m_ev_ln_v_b, m_ev_w_s, m_ev_b_s, m_ev_w_pool, m_ev_pool_scale, m_ev_w_out, m_od_w_in, m_od_norm_g, m_od_w_out, m_lb_param, m_ffn_w_up, m_ffn_conv_w, m_ffn_conv_b, m_ffn_w_down, m_ln1_g, m_ln1_b, m_ln2_g, m_ln2_b, v_ev_w_in, v_ev_ln_v_g, v_ev_ln_v_b, v_ev_w_s, v_ev_b_s, v_ev_w_pool, v_ev_pool_scale, v_ev_w_out, v_od_w_in, v_od_norm_g, v_od_w_out, v_lb_param, v_ffn_w_up, v_ffn_conv_w, v_ffn_conv_b, v_ffn_w_down, v_ln1_g, v_ln1_b, v_ln2_g, v_ln2_b):
    me = 4 * lax.axis_index("x") + 2 * lax.axis_index("y") + lax.axis_index("c")
    T, D = x.shape[1], x.shape[2]
    W = ev_ln_v_g.shape[1]
    H = W // A_HEAD
    Wg = W // B_GROUPS
    F2 = ffn_conv_b.shape[1]
    F = F2 // 2
    n_in0, n_out0 = ev_w_in.shape[2], ev_w_out.shape[1]
    n_in1, n_out1 = od_w_in.shape[2], od_w_out.shape[1]
    n_up, n_dn = ffn_w_up.shape[2], ffn_w_down.shape[1]
    n_pool, n_ng, n_cw = ev_w_pool.shape[2], od_norm_g.shape[1], ffn_conv_w.shape[2]

    small_shards = [od_norm_g, ffn_conv_w, ev_w_pool]
    small_pack = _pack(small_shards)
    small_all = _exchange("gather_small_params", [small_pack], [_S((N_DEV,) + small_pack.shape, _F32)], [_slot_job(0, 0)])[0]

    me1 = me.astype(jnp.int32).reshape(1)
    weights = [
        ("w_in0", ev_w_in[0], None, 1, n_in0), ("w_out0", ev_w_out[0], None, 0, n_out0),
        ("w_up0", ffn_w_up, 0, 1, n_up), ("w_dn0", ffn_w_down, 0, 0, n_dn),
        ("w_in1", od_w_in[0], None, 1, n_in1), ("w_out1", od_w_out[0], None, 0, n_out1),
        ("w_up1", ffn_w_up, 1, 1, n_up), ("w_dn1", ffn_w_down, 1, 0, n_dn),
    ]
    started, tokens = {}, [small_all]
    for key, w, layer, axis, n in weights:
        full = _place(w, me1, axis, "place_" + key, layer, deps=tokens)
        send, recv, full, token = _gather_start("gather_start_" + key, full, axis, n)
        started[key] = (full, axis, n, send, recv)
        tokens = [token]

    def pass_on(key, after):
        full, axis, n, send, recv = started[key]
        send2, recv2, full, token = _gather_forward("gather_forward_" + key, full, axis, n, recv, after)
        started[key] = (full, axis, n, send, recv, send2, recv2)
        return token

    def gathered(key, after):
        return _gather_wait("gather_wait_" + key, *started[key], after)

    ng_parts, cw_parts, wp_parts = [], [], []
    for j in range(N_DEV):
        a, b, c = _unpack(small_all[j], [s.shape for s in small_shards])
        ng_parts.append(a)
        cw_parts.append(b)
        wp_parts.append(c)
    norm_g = jnp.concatenate(ng_parts, axis=1)
    conv_w = jnp.concatenate(cw_parts, axis=2)
    w_pool = jnp.concatenate(wp_parts, axis=2)[0]
    cw_l = [conv_w[l].reshape(3, 2, F).transpose(1, 0, 2) for l in range(DEPTH)]
    cb_l = [ffn_conv_b[l].reshape(2, 1, F) for l in range(DEPTH)]
    ws_tril = jnp.tril(ev_w_s[0]).astype(_MM)
    bias = jnp.repeat(ev_b_s[0].T, A_HEAD, axis=1)
    wp_b = w_pool.astype(_MM)
    lb, lb_vjp = jax.vjp(_lb_of, lb_param)

    small_names = ["ev_ln_v_g", "ev_ln_v_b", "ev_w_s", "ev_b_s", "ev_w_pool", "ev_pool_scale", "od_norm_g", "lb_param",
                   "ffn_conv_w", "ffn_conv_b", "ln1_g", "ln1_b", "ln2_g", "ln2_b"]
    given = dict(ev_ln_v_g=(ev_ln_v_g, m_ev_ln_v_g, v_ev_ln_v_g), ev_ln_v_b=(ev_ln_v_b, m_ev_ln_v_b, v_ev_ln_v_b),
                 ev_w_s=(ev_w_s, m_ev_w_s, v_ev_w_s), ev_b_s=(ev_b_s, m_ev_b_s, v_ev_b_s),
                 ev_w_pool=(ev_w_pool, m_ev_w_pool, v_ev_w_pool),
                 ev_pool_scale=(ev_pool_scale, m_ev_pool_scale, v_ev_pool_scale),
                 od_norm_g=(od_norm_g, m_od_norm_g, v_od_norm_g), lb_param=(lb_param, m_lb_param, v_lb_param),
                 ffn_conv_w=(ffn_conv_w, m_ffn_conv_w, v_ffn_conv_w), ffn_conv_b=(ffn_conv_b, m_ffn_conv_b, v_ffn_conv_b),
                 ln1_g=(ln1_g, m_ln1_g, v_ln1_g), ln1_b=(ln1_b, m_ln1_b, v_ln1_b), ln2_g=(ln2_g, m_ln2_g, v_ln2_g),
                 ln2_b=(ln2_b, m_ln2_b, v_ln2_b))
    shard_axis = dict(ev_w_pool=2, od_norm_g=1, ffn_conv_w=2)
    rep_names = [n for n in small_names if n not in shard_axis]
    shd_names = [n for n in small_names if n in shard_axis]
    small_packs = [_pack([given[n][j] for n in small_names]) for j in range(3)]

    x2 = x[0]
    xb = _cast(x2, _MM, "cast_x", deps=[pass_on("w_in0", tokens[0])])
    w_in0 = gathered("w_in0", xb)
    h0 = _mm(xb, w_in0, "nn", _F32, "ev_in", out_parts=3)
    tie = pass_on("w_out0", h0)
    yab = _ev_mid_fwd(h0, ev_ln_v_g + tie[0, 0], ev_ln_v_b, ws_tril, bias, wp_b, ev_pool_scale, "ev_mid_fwd")
    w_out0 = gathered("w_out0", yab)
    tie = pass_on("w_up0", (yab, *small_packs))
    z1, x1, x1b = _mm(yab, w_out0, "nn", _F32, "ev_out", add=x2, add_scale=ALPHA, ln=(ln1_g[0:1], ln1_b[0:1]), deps=[tie])
    w_up0 = gathered("w_up0", x1b)
    hf0 = _mm(x1b, w_up0, "nn", _F32, "ffn_up", out_parts=2)
    tie = pass_on("w_dn0", hf0)
    act0, hc0 = _ffn_mid_fwd(hf0, cw_l[0], cb_l[0] + tie[0, 0], "ffn_mid_fwd")
    w_dn0 = gathered("w_dn0", act0)
    tie = pass_on("w_in1", act0)
    z2, x2_, x2b = _mm(act0, w_dn0, "nn", _F32, "ffn_down_ln", add=x1, add_scale=ALPHA, ln=(ln2_g[0:1], ln2_b[0:1]),
                       deps=[tie])
    w_in1 = gathered("w_in1", x2b)
    h1 = _mm(x2b, w_in1, "nn", _F32, "od_in", out_parts=4)
    qd, kd, ke, vb, dec = _hgrn_prep_fwd(h1, lb, "hgrn_prep_fwd")
    tie = pass_on("w_out1", qd)
    o, yo, st = _hgrn_scan_fwd(qd, kd, ke, vb, dec, h1, norm_g + tie[0, 0], "hgrn_scan_fwd")
    w_out1 = gathered("w_out1", yo)
    tie = pass_on("w_up1", yo)
    z3, x3, x3b = _mm(yo, w_out1, "nn", _F32, "od_out", add=x2_, add_scale=ALPHA, ln=(ln1_g[1:2], ln1_b[1:2]), deps=[tie])
    w_up1 = gathered("w_up1", x3b)
    hf1 = _mm(x3b, w_up1, "nn", _F32, "ffn_up", out_parts=2)
    tie = pass_on("w_dn1", hf1)
    act1, hc1 = _ffn_mid_fwd(hf1, cw_l[1], cb_l[1] + tie[0, 0], "ffn_mid_fwd")
    w_dn1 = gathered("w_dn1", act1)
    z4 = _mm(act1, w_dn1, "nn", _F32, "ffn_down", add=x3, add_scale=ALPHA)

    scat = {}

    def scatter(key, dw, axis, n):
        send, recv, dw, land, token = _scatter_start("scatter_start_" + key, dw, axis, n)
        scat[key] = (dw, land, send, recv, axis, n)
        return [token]

    loss11, dz4, dz4b, g_ln2_1, b_ln2_1 = _ln_loss_bwd(z4, ln2_g[1:2], ln2_b[1:2], loss_target[0], "ln_loss_bwd")
    tok = scatter("dn1", _mm(act1, dz4b, "tn", _XCH, "ffn_down_dw"), 0, n_dn)
    dact1 = _mm(dz4b, w_dn1, "nt", _MM, "ffn_down_dx", deps=tok)
    dhf1, dcw1, dcb1 = _ffn_mid_bwd(hf1, hc1, dact1, cw_l[1], "ffn_mid_bwd")
    tok = scatter("up1", _mm(x3b, dhf1, "tn", _XCH, "ffn_up_dw", b_parts=2, deps=tok), 1, n_up)
    dx3 = _mm(dhf1, w_up1, "nt", _F32, "ffn_up_dx", a_parts=2, add=dz4, add_scale=ALPHA, deps=tok)
    dz3, dz3b, g_ln1_1, b_ln1_1 = _ln_bwd(z3, ln1_g[1:2], dx3, "ln_bwd")
    tok = scatter("out1", _mm(yo, dz3b, "tn", _XCH, "od_out_dw", deps=tok), 0, n_out1)
    dyo = _mm(dz3b, w_out1, "nt", _F32, "od_out_dx", deps=tok)
    dqd, dkd, dke, dv, dgate, ddec, dng = _hgrn_scan_bwd(qd, kd, ke, vb, dec, st, o, h1, norm_g, dyo, "hgrn_scan_bwd")
    dh1, dlb = _hgrn_prep_bwd(h1, lb, dqd, dkd, dke, dv, dgate, ddec, "hgrn_prep_bwd")
    tok = scatter("in1", _mm(x2b, dh1, "tn", _XCH, "od_in_dw", b_parts=4, deps=tok), 1, n_in1)
    dx2 = _mm(dh1, w_in1, "nt", _F32, "od_in_dx", a_parts=4, add=dz3, add_scale=ALPHA, deps=tok)
    dz2, dz2b, g_ln2_0, b_ln2_0 = _ln_bwd(z2, ln2_g[0:1], dx2, "ln_bwd")
    tok = scatter("dn0", _mm(act0, dz2b, "tn", _XCH, "ffn_down_dw", deps=tok), 0, n_dn)
    dact0 = _mm(dz2b, w_dn0, "nt", _MM, "ffn_down_dx", deps=tok)
    dhf0, dcw0, dcb0 = _ffn_mid_bwd(hf0, hc0, dact0, cw_l[0], "ffn_mid_bwd")
    tok = scatter("up0", _mm(x1b, dhf0, "tn", _XCH, "ffn_up_dw", b_parts=2, deps=tok), 1, n_up)
    dx1 = _mm(dhf0, w_up0, "nt", _F32, "ffn_up_dx", a_parts=2, add=dz2, add_scale=ALPHA, deps=tok)
    dz1, dz1b, g_ln1_0, b_ln1_0 = _ln_bwd(z1, ln1_g[0:1], dx1, "ln_bwd")
    tok = scatter("out0", _mm(yab, dz1b, "tn", _XCH, "ev_out_dw", deps=tok), 0, n_out0)
    dyab = _mm(dz1b, w_out0, "nt", _F32, "ev_out_dx", deps=tok)
    dh0, dws, dbias, dlng, dlnb, dwp, dsc = _ev_mid_bwd(h0, dyab, ev_ln_v_g, ev_ln_v_b, ws_tril, bias, wp_b,
                                                        ev_pool_scale, "ev_mid_bwd")

    g_b_s = dbias.reshape(A_CHUNK, H, A_HEAD).sum(axis=-1).T[None]
    g_conv_w = jnp.stack([d.transpose(1, 0, 2).reshape(3, F2) for d in (dcw0, dcw1)])
    g_conv_b = jnp.stack([d.reshape(F2) for d in (dcb0, dcb1)])
    small_grads = dict(zip(small_names, [
        dlng, dlnb, dws[None], g_b_s, dwp[None], dsc, dng, lb_vjp(dlb)[0], g_conv_w, g_conv_b,
        jnp.concatenate([g_ln1_0, g_ln1_1]), jnp.concatenate([b_ln1_0, b_ln1_1]),
        jnp.concatenate([g_ln2_0, g_ln2_1]), jnp.concatenate([b_ln2_0, b_ln2_1])]))

    def by_device(g, ax):
        g = g.reshape(g.shape[:ax] + (N_DEV, g.shape[ax] // N_DEV) + g.shape[ax + 1:])
        return jnp.moveaxis(g, ax, 0).reshape(N_DEV, -1)

    shd = jnp.concatenate([by_device(small_grads[n], shard_axis[n]) for n in shd_names], axis=1)
    shd_pack = jnp.pad(shd, ((0, 0), (0, (-shd.shape[1]) % (LANE * LANE)))).reshape(-1, LANE)
    shd_rows = shd_pack.shape[0] // N_DEV
    rep_pack = _pack([small_grads[n] for n in rep_names])
    tok = scatter("in0", _mm(xb, dh0, "tn", _XCH, "ev_in_dw", b_parts=3, deps=tok), 1, n_in0)
    tok = scatter("small_rep", rep_pack + tok[0][0, 0], None, None)
    tok = scatter("small_shd", shd_pack + tok[0][0, 0], 0, shd_rows)
    grad_x = _mm(dh0, w_in0, "nt", _F32, "ev_in_dx", a_parts=3, add=dz1, add_scale=ALPHA, deps=tok)

    def landed(name, keys, after):
        got = _scatter_wait(name, [scat[k] for k in keys], after)
        return {k: (me1, dw, land) for k, (dw, land) in zip(keys, got)}

    early = landed("scatter_wait_early", ["dn1", "up1", "out1", "in1", "dn0", "up0", "out0"], grad_x)
    big = {}
    r_dn = _adamw_big(*early["dn1"], ffn_w_down, m_ffn_w_down, v_ffn_w_down, 0, n_dn, "adamw_w_dn1", layer=1)
    r_up = _adamw_big(*early["up1"], ffn_w_up, m_ffn_w_up, v_ffn_w_up, 1, n_up, "adamw_w_up1", layer=1)
    big["od_w_out"] = _adamw_big(*early["out1"], od_w_out[0], m_od_w_out[0], v_od_w_out[0], 0, n_out1, "adamw_w_out1")
    big["od_w_in"] = _adamw_big(*early["in1"], od_w_in[0], m_od_w_in[0], v_od_w_in[0], 1, n_in1, "adamw_w_in1")
    big["ffn_w_down"] = _adamw_big(*early["dn0"], ffn_w_down, m_ffn_w_down, v_ffn_w_down, 0, n_dn, "adamw_w_dn0", layer=0, into=r_dn)
    big["ffn_w_up"] = _adamw_big(*early["up0"], ffn_w_up, m_ffn_w_up, v_ffn_w_up, 1, n_up, "adamw_w_up0", layer=0, into=r_up)
    big["ev_w_out"] = _adamw_big(*early["out0"], ev_w_out[0], m_ev_w_out[0], v_ev_w_out[0], 0, n_out0, "adamw_w_out0")
    late = landed("scatter_wait_late", ["in0", "small_rep", "small_shd"],
                  (big["ffn_w_down"][0], big["ffn_w_up"][0], big["od_w_in"][0], big["ev_w_out"][0]))
    big["ev_w_in"] = _adamw_big(*late["in0"], ev_w_in[0], m_ev_w_in[0], v_ev_w_in[0], 1, n_in0, "adamw_w_in0")

    rep_mat = _sum_in_device_order(*late["small_rep"], "sum_small_rep")
    local_g = dict(zip(rep_names, _unpack(rep_mat, [small_grads[n].shape for n in rep_names])))
    _, shd_all, shd_land = late["small_shd"]
    shd_own = lax.dynamic_slice_in_dim(shd_all, me * shd_rows, shd_rows, axis=0)
    shd_mat = _sum_in_device_order(me1, shd_own, shd_land, "sum_small_shd")
    local_g.update(zip(shd_names, _unpack(shd_mat, [given[n][0].shape for n in shd_names])))
    local_shapes = [given[n][0].shape for n in small_names]
    res = _adamw(_pack([local_g[n] for n in small_names])[None], *small_packs, "adamw_small")
    small = {n: [] for n in small_names}
    for r in res:
        for n, a in zip(small_names, _unpack(r, local_shapes)):
            small[n].append(a)

    loss = lax.psum(loss11[0, 0], ("x", "y", "c"))
    order = ["ev_w_in", "ev_ln_v_g", "ev_ln_v_b", "ev_w_s", "ev_b_s", "ev_w_pool", "ev_pool_scale", "ev_w_out", "od_w_in",
             "od_norm_g", "od_w_out", "lb_param", "ffn_w_up", "ffn_conv_w", "ffn_conv_b", "ffn_w_down", "ln1_g", "ln1_b",
             "ln2_g", "ln2_b"]
    shapes = dict(ev_w_in=ev_w_in.shape, ev_w_out=ev_w_out.shape, od_w_in=od_w_in.shape, od_w_out=od_w_out.shape,
                  ffn_w_up=ffn_w_up.shape, ffn_w_down=ffn_w_down.shape)
    outs = [loss, grad_x[None]]
    for kind in range(4):
        for n in order:
            outs.append(big[n][kind].reshape(shapes[n]) if n in big else small[n][kind])
    return tuple(outs)
```

```python
import functools
import math

import jax
import jax.numpy as jnp
from jax import lax
from jax.experimental import pallas as pl
from jax.experimental.pallas import tpu as pltpu

_MM = jnp.bfloat16
_XCH = jnp.bfloat16

DEPTH = 2
ALPHA = (2 * DEPTH) ** 0.25
LN_EPS = 1e-5
A_CHUNK = 128
A_HEAD = 128
B_GROUPS = 4
POOL_HALO = 16
C_CHUNK = 64
C_HEAD = 128
SCAN_UNROLL = 32
CONV_HALO = 8
PACKED_ROWS = 16
FFN_ROWS, FFN_FWD_COLS = 512, 1408
FFN_BWD_ROWS, FFN_BWD_COLS = 512, 512
FFN_CHUNK = 128
ADAM_LR, ADAM_B1, ADAM_B2, ADAM_EPS, ADAM_WD, ADAM_STEP = 0.001, 0.9, 0.999, 1e-08, 0.01, 10
N_DEV = 8
LANE = 128
VMEM_LIMIT = 56 * 1024 * 1024
MM_FULL_K = 3072
MM_FULL_K_TN = 4096
MM_DEEP_K = 2816
MM_LN_ROWS, MM_LN_K = 512, 1408

_F32 = jnp.float32
_NN = (((1,), (0,)), ((), ()))
_NT = (((1,), (1,)), ((), ()))
_TN = (((0,), (0,)), ((), ()))
_S = jax.ShapeDtypeStruct


def _dot(a, b, dims=_NN):
    return lax.dot_general(a, b, dims, preferred_element_type=_F32)


def _tile(dim, pref):
    best = None
    d = LANE
    while d <= min(dim, pref):
        if dim % d == 0:
            best = d
        d += LANE
    return best if best is not None else dim


def _params(sem):
    return pltpu.CompilerParams(dimension_semantics=sem, vmem_limit_bytes=VMEM_LIMIT)


def _sigmoid(x):
    return 0.5 * jnp.tanh(0.5 * x) + 0.5


def _sigmoid_rel(x):
    return 1.0 / (1.0 + jnp.exp(-x))


_GELU_C = 0.7978845608028654
_GELU_A = 0.044715


def _gelu_and_grad(x):
    t = jnp.tanh(_GELU_C * (x + _GELU_A * x * x * x))
    y = 0.5 * x * (1.0 + t)
    dy = 0.5 * (1.0 + t) + 0.5 * x * (1.0 - t * t) * _GELU_C * (1.0 + 3.0 * _GELU_A * x * x)
    return y, dy


def _row_index(n):
    return lax.broadcasted_iota(jnp.int32, (n, 1), 0)


def _mm_tiles(mode, M, N, K, with_add):
    if mode == "tn":
        return _tile(M, 1024), _tile(N, 1024), _tile(K, MM_FULL_K_TN)
    if K <= MM_FULL_K:
        return _tile(M, 1024 if with_add else 2048), _tile(N, 1024 if mode == "nn" else 512), K
    return _tile(M, 1024), _tile(N, 1024), _tile(K, MM_DEEP_K)


def _mm(a, b, mode, out_dtype, name, *, a_parts=1, b_parts=1, out_parts=1, add=None, add_scale=1.0, deps=(), tiles=None,
        ln=None):
    if mode == "nn":
        M, K = a.shape
        N = b.shape[1]
    elif mode == "nt":
        if a_parts > 1:
            M, K = a.shape[1], a.shape[2] * a_parts
        else:
            M, K = a.shape
        N = b.shape[0]
    else:
        K, M = a.shape
        N = b.shape[-1] * b_parts
    tm, tn, tk = tiles if tiles is not None else _mm_tiles(mode, M, N // max(b_parts, out_parts), K // a_parts, add is not None)
    if ln is not None:
        tm, tn, tk = _tile(M, MM_LN_ROWS), N, (K if K <= MM_FULL_K else _tile(K, MM_LN_K))
    nk = K // tk
    npj = (N // max(b_parts, out_parts)) // tn
    nkp = (K // a_parts) // tk
    if mode == "nn":
        a_spec = pl.BlockSpec((tm, tk), lambda i, j, k: (i, k))
        b_spec = pl.BlockSpec((tk, tn), lambda i, j, k: (k, j))
        dims = _NN
    elif mode == "nt":
        if a_parts > 1:
            a_spec = pl.BlockSpec((None, tm, tk), lambda i, j, k: (k // nkp, i, k % nkp))
        else:
            a_spec = pl.BlockSpec((tm, tk), lambda i, j, k: (i, k))
        b_spec = pl.BlockSpec((tn, tk), lambda i, j, k: (j, k))
        dims = _NT
    else:
        a_spec = pl.BlockSpec((tk, tm), lambda i, j, k: (k, i))
        if b_parts > 1:
            b_spec = pl.BlockSpec((None, tk, tn), lambda i, j, k: (j // npj, k, j % npj))
        else:
            b_spec = pl.BlockSpec((tk, tn), lambda i, j, k: (k, j))
        dims = _TN
    if out_parts > 1:
        out_spec = pl.BlockSpec((None, tm, tn), lambda i, j, k: (j // npj, i, j % npj))
        out_shape = _S((out_parts, M, N // out_parts), out_dtype)
    else:
        out_spec = pl.BlockSpec((tm, tn), lambda i, j, k: (i, j))
        out_shape = _S((M, N), out_dtype)
    in_specs = [a_spec, b_spec]
    args = [a, b]
    if add is not None:
        in_specs.append(pl.BlockSpec((tm, tn), lambda i, j, k: (i, j)))
        args.append(add)
    if ln is not None:
        vec = pl.BlockSpec((1, N), lambda i, j, k: (0, 0))
        in_specs += [vec, vec]
        args += list(ln)
        out_spec = [out_spec] * 3
        out_shape = [_S((M, N), _F32), _S((M, N), _F32), _S((M, N), _MM)]
    n_ln = 2 + (add is not None)
    in_specs += [_ANY] * len(deps)
    args += list(deps)
    n_out = 1 if ln is None else 3

    def finish(r, refs, outs):
        if add is not None:
            r = r + add_scale * refs[2][...]
        if ln is None:
            outs[0][...] = r.astype(outs[0].dtype)
            return
        mu = jnp.mean(r, axis=-1, keepdims=True)
        rc = r - mu
        y = rc * lax.rsqrt(jnp.mean(rc * rc, axis=-1, keepdims=True) + LN_EPS) * refs[n_ln][...] + refs[n_ln + 1][...]
        outs[0][...] = r
        outs[1][...] = y
        outs[2][...] = y.astype(outs[2].dtype)

    def body_one(*refs):
        finish(_dot(refs[0][...], refs[1][...], dims), refs, refs[len(refs) - n_out:])

    def body_acc(*refs):
        acc = refs[-1]
        k = pl.program_id(2)

        @pl.when(k == 0)
        def _():
            acc[...] = jnp.zeros_like(acc)

        acc[...] += _dot(refs[0][...], refs[1][...], dims)

        @pl.when(k == nk - 1)
        def _():
            finish(acc[...], refs, refs[len(refs) - 1 - n_out:len(refs) - 1])

    return pl.pallas_call(
        body_one if nk == 1 else body_acc, name=name, grid=(M // tm, N // tn, nk), in_specs=in_specs,
        out_specs=out_spec, out_shape=out_shape,
        scratch_shapes=[] if nk == 1 else [pltpu.VMEM((tm, tn), _F32)],
        compiler_params=_params(("parallel", "parallel", "arbitrary")),
    )(*args)


def _cast(x2d, dtype, name, deps=()):
    R, C = x2d.shape
    tr = _tile(R, 512) if R % LANE == 0 else R

    def body(x_ref, *rest):
        rest[-1][...] = x_ref[...].astype(rest[-1].dtype)

    return pl.pallas_call(
        body, name=name, grid=(R // tr,), in_specs=[pl.BlockSpec((tr, C), lambda i: (i, 0))] + [_ANY] * len(deps),
        out_specs=pl.BlockSpec((tr, C), lambda i: (i, 0)), out_shape=_S((R, C), dtype),
        compiler_params=_params(("parallel",)),
    )(x2d, *deps)


def _ln_fwd(z, g, b, name):
    T, D = z.shape
    tr = _tile(T, 256)

    def body(z_ref, g_ref, b_ref, y_ref, yb_ref):
        zz = z_ref[...]
        mu = jnp.mean(zz, axis=-1, keepdims=True)
        zc = zz - mu
        var = jnp.mean(zc * zc, axis=-1, keepdims=True)
        y = zc * lax.rsqrt(var + LN_EPS) * g_ref[...] + b_ref[...]
        y_ref[...] = y
        yb_ref[...] = y.astype(yb_ref.dtype)

    row = pl.BlockSpec((tr, D), lambda i: (i, 0))
    vec = pl.BlockSpec((1, D), lambda i: (0, 0))
    return pl.pallas_call(
        body, name=name, grid=(T // tr,), in_specs=[row, vec, vec], out_specs=[row, row],
        out_shape=[_S((T, D), _F32), _S((T, D), _MM)], compiler_params=_params(("parallel",)),
    )(z, g, b)


def _ln_bwd(z, g, dy, name):
    T, D = z.shape
    tr = _tile(T, 256)

    def body(z_ref, g_ref, dy_ref, dz_ref, dzb_ref, dg_ref, db_ref):
        @pl.when(pl.program_id(0) == 0)
        def _():
            dg_ref[...] = jnp.zeros_like(dg_ref)
            db_ref[...] = jnp.zeros_like(db_ref)

        zz = z_ref[...]
        mu = jnp.mean(zz, axis=-1, keepdims=True)
        zc = zz - mu
        rstd = lax.rsqrt(jnp.mean(zc * zc, axis=-1, keepdims=True) + LN_EPS)
        xh = zc * rstd
        d = dy_ref[...]
        dg_ref[...] += jnp.sum(d * xh, axis=0, keepdims=True)
        db_ref[...] += jnp.sum(d, axis=0, keepdims=True)
        dxh = d * g_ref[...]
        dz = rstd * (dxh - jnp.mean(dxh, axis=-1, keepdims=True) - xh * jnp.mean(dxh * xh, axis=-1, keepdims=True))
        dz_ref[...] = dz
        dzb_ref[...] = dz.astype(dzb_ref.dtype)

    row = pl.BlockSpec((tr, D), lambda i: (i, 0))
    vec = pl.BlockSpec((1, D), lambda i: (0, 0))
    return pl.pallas_call(
        body, name=name, grid=(T // tr,), in_specs=[row, vec, row], out_specs=[row, row, vec, vec],
        out_shape=[_S((T, D), _F32), _S((T, D), _MM), _S((1, D), _F32), _S((1, D), _F32)],
        compiler_params=_params(("arbitrary",)),
    )(z, g, dy)


def _ln_loss_bwd(z, g, b, target, name):
    T, D = z.shape
    tr = _tile(T, 256)

    def body(z_ref, g_ref, b_ref, t_ref, loss_ref, dz_ref, dzb_ref, dg_ref, db_ref, lacc):
        i = pl.program_id(0)

        @pl.when(i == 0)
        def _():
            dg_ref[...] = jnp.zeros_like(dg_ref)
            db_ref[...] = jnp.zeros_like(db_ref)
            lacc[...] = jnp.zeros_like(lacc)

        zz = z_ref[...]
        mu = jnp.mean(zz, axis=-1, keepdims=True)
        zc = zz - mu
        rstd = lax.rsqrt(jnp.mean(zc * zc, axis=-1, keepdims=True) + LN_EPS)
        xh = zc * rstd
        err = xh * g_ref[...] + b_ref[...] - t_ref[...]
        lacc[...] += jnp.sum(err * err, axis=0, keepdims=True)
        d = err * (1.0 / D)
        dg_ref[...] += jnp.sum(d * xh, axis=0, keepdims=True)
        db_ref[...] += jnp.sum(d, axis=0, keepdims=True)
        dxh = d * g_ref[...]
        dz = rstd * (dxh - jnp.mean(dxh, axis=-1, keepdims=True) - xh * jnp.mean(dxh * xh, axis=-1, keepdims=True))
        dz_ref[...] = dz
        dzb_ref[...] = dz.astype(dzb_ref.dtype)

        @pl.when(i == pl.num_programs(0) - 1)
        def _():
            loss_ref[...] = jnp.sum(lacc[...], axis=-1, keepdims=True) * (0.5 / D)

    row = pl.BlockSpec((tr, D), lambda i: (i, 0))
    vec = pl.BlockSpec((1, D), lambda i: (0, 0))
    one = pl.BlockSpec((1, 1), lambda i: (0, 0))
    return pl.pallas_call(
        body, name=name, grid=(T // tr,), in_specs=[row, vec, vec, row], out_specs=[one, row, row, vec, vec],
        out_shape=[_S((1, 1), _F32), _S((T, D), _F32), _S((T, D), _MM), _S((1, D), _F32), _S((1, D), _F32)],
        scratch_shapes=[pltpu.VMEM((1, D), _F32)], compiler_params=_params(("arbitrary",)),
    )(z, g, b, target)


def _conv3(X, cw, cb):
    return cb + cw[2:3] * X + cw[1:2] * pltpu.roll(X, 1, 0) + cw[0:1] * pltpu.roll(X, 2, 0)


def _ffn_mid_fwd(h, cw, cb, name):
    _, T, F = h.shape
    tr = _tile(T, FFN_ROWS)
    tc = _tile(F, FFN_FWD_COLS)
    nb = tr // CONV_HALO

    rc = _tile(tr, FFN_CHUNK)
    lanes = [slice(cs * LANE, (cs + 1) * LANE) for cs in range(tc // LANE)]

    def body(h_ref, p_ref, cw_ref, cb_ref, o_ref, c_ref):
        i = pl.program_id(0)

        def work(r0, cols, X):
            hc = [_conv3(X[part], cw_ref[part, :, cols], cb_ref[part, :, cols])[CONV_HALO:] for part in range(2)]
            for part in range(2):
                c_ref[part, pl.ds(r0, rc), cols] = hc[part].astype(c_ref.dtype)
            a, v = hc
            o_ref[pl.ds(r0, rc), cols] = (a * _sigmoid(a) * v).astype(o_ref.dtype)

        for cols in lanes:
            work(0, cols, [jnp.concatenate([jnp.where(i == 0, 0.0, p_ref[part, :, cols]), h_ref[part, 0:rc, cols]], axis=0)
                           for part in range(2)])

        def chunk(c, carry):
            r0 = pl.multiple_of(c * rc, rc)
            for cols in lanes:
                work(r0, cols, [h_ref[part, pl.ds(r0 - CONV_HALO, rc + CONV_HALO), cols] for part in range(2)])
            return carry

        lax.fori_loop(1, tr // rc, chunk, 0)

    return pl.pallas_call(
        body, name=name, grid=(T // tr, F // tc),
        in_specs=[pl.BlockSpec((2, tr, tc), lambda i, j: (0, i, j)),
                  pl.BlockSpec((2, CONV_HALO, tc), lambda i, j: (0, jnp.maximum(i * nb - 1, 0), j)),
                  pl.BlockSpec((2, 3, tc), lambda i, j: (0, 0, j)),
                  pl.BlockSpec((2, 1, tc), lambda i, j: (0, 0, j))],
        out_specs=[pl.BlockSpec((tr, tc), lambda i, j: (i, j)), pl.BlockSpec((2, tr, tc), lambda i, j: (0, i, j))],
        out_shape=[_S((T, F), _MM), _S((2, T, F), _MM)],
        compiler_params=_params(("parallel", "parallel")),
    )(h, h, cw, cb)


def _ffn_mid_bwd(h, hc, dact, cw, name):
    _, T, F = h.shape
    tr = _tile(T, FFN_BWD_ROWS)
    tc = _tile(F, FFN_BWD_COLS)
    nb_c = tr // PACKED_ROWS
    rc = _tile(tr, FFN_CHUNK)
    n = rc + CONV_HALO

    def body(h_ref, c_ref, cn_ref, d_ref, dn_ref, cw_ref, dh_ref, dcw_ref, dcb_ref):
        i = pl.program_id(1)
        is_last = i == pl.num_programs(1) - 1

        @pl.when(i == 0)
        def _():
            dcw_ref[...] = jnp.zeros_like(dcw_ref)
            dcb_ref[...] = jnp.zeros_like(dcb_ref)

        def work(r0, cols, a, v, D):
            sg = _sigmoid(a)
            dhc = [D * v * sg * (1.0 + a * (1.0 - sg)), D * a * sg]
            for part in range(2):
                X = h_ref[part, pl.ds(r0, rc), cols]
                cwp = cw_ref[part, :, cols]
                dh = None
                for k in range(3):
                    g = (dhc[part] if k == 0 else pltpu.roll(dhc[part], n - k, 0))[0:rc]
                    term = cwp[2 - k:3 - k] * g
                    dh = term if dh is None else dh + term
                    dcw_ref[part, 2 - k:3 - k, cols] += jnp.sum(g * X, axis=0, keepdims=True)
                    if k == 0:
                        dcb_ref[part, :, cols] += jnp.sum(g, axis=0, keepdims=True)
                dh_ref[part, pl.ds(r0, rc), cols] = dh.astype(dh_ref.dtype)

        lanes = [slice(cs * LANE, (cs + 1) * LANE) for cs in range(tc // LANE)]

        def chunk(c, carry):
            r0 = pl.multiple_of(c * rc, rc)
            for cols in lanes:
                a, v = [c_ref[part, pl.ds(r0, rc + PACKED_ROWS), cols].astype(_F32)[0:n] for part in range(2)]
                work(r0, cols, a, v, d_ref[pl.ds(r0, rc + PACKED_ROWS), cols].astype(_F32)[0:n])
            return carry

        lax.fori_loop(0, tr // rc - 1, chunk, 0)
        r0 = tr - rc
        for cols in lanes:
            a, v = [jnp.concatenate([c_ref[part, r0:tr, cols].astype(_F32), cn_ref[part, :, cols].astype(_F32)[0:CONV_HALO]],
                                    axis=0) for part in range(2)]
            D = jnp.concatenate([d_ref[r0:tr, cols].astype(_F32),
                                 jnp.where(is_last, 0.0, dn_ref[:, cols].astype(_F32)[0:CONV_HALO])], axis=0)
            work(r0, cols, a, v, D)

    return pl.pallas_call(
        body, name=name, grid=(F // tc, T // tr),
        in_specs=[pl.BlockSpec((2, tr, tc), lambda j, i: (0, i, j)),
                  pl.BlockSpec((2, tr, tc), lambda j, i: (0, i, j)),
                  pl.BlockSpec((2, PACKED_ROWS, tc), lambda j, i: (0, jnp.minimum((i + 1) * nb_c, T // PACKED_ROWS - 1), j)),
                  pl.BlockSpec((tr, tc), lambda j, i: (i, j)),
                  pl.BlockSpec((PACKED_ROWS, tc), lambda j, i: (jnp.minimum((i + 1) * nb_c, T // PACKED_ROWS - 1), j)),
                  pl.BlockSpec((2, 3, tc), lambda j, i: (0, 0, j))],
        out_specs=[pl.BlockSpec((2, tr, tc), lambda j, i: (0, i, j)),
                   pl.BlockSpec((2, 3, tc), lambda j, i: (0, 0, j)),
                   pl.BlockSpec((2, 1, tc), lambda j, i: (0, 0, j))],
        out_shape=[_S((2, T, F), _MM), _S((2, 3, F), _F32), _S((2, 1, F), _F32)],
        compiler_params=_params(("parallel", "arbitrary")),
    )(h, hc, hc, dact, dact, cw)


def _ev_common(h_ref, hp_ref, lng_ref, lnb_ref, ws_ref, bias_ref, i, tr, W):
    H = W // A_HEAD
    u, gu = _gelu_and_grad(h_ref[0])
    v, gv = _gelu_and_grad(h_ref[1])
    mu = jnp.mean(v, axis=-1, keepdims=True)
    vc = v - mu
    rstd = lax.rsqrt(jnp.mean(vc * vc, axis=-1, keepdims=True) + LN_EPS)
    vhat = vc * rstd
    vb = (vhat * lng_ref[...] + lnb_ref[...]).astype(_MM)
    s_chunks = []
    for c in range(tr // A_CHUNK):
        r0 = c * A_CHUNK
        heads = [_dot(ws_ref[hd], vb[r0:r0 + A_CHUNK, hd * A_HEAD:(hd + 1) * A_HEAD]) for hd in range(H)]
        s_chunks.append(jnp.concatenate(heads, axis=1) + bias_ref[...])
    prev = jnp.where(i == 0, 0.0, hp_ref[...])
    X = jnp.concatenate([prev, h_ref[2]], axis=0)
    return u, gu, gv, rstd, vhat, vb, s_chunks, X


def _pool_inv_count(i, tr, rows, win):
    pos = i * tr + _row_index(rows) + 1
    return 1.0 / jnp.minimum(pos, win).astype(_F32)


def _pool_fwd(X, g, Wg, i, tr):
    xg = X[:, g * Wg:(g + 1) * Wg]
    s = xg
    for k in range(g + 1):
        s = s + pltpu.roll(s, 2 ** k, 0)
    return s[POOL_HALO:] * _pool_inv_count(i, tr, tr, 2 ** (g + 1)) - xg[POOL_HALO:]


def _ev_mid_fwd(h, lng, lnb, ws, bias, wp, sc, name):
    _, T, W = h.shape
    tr = _tile(T, 256)
    H = W // A_HEAD
    Wg = W // B_GROUPS
    nb = tr // POOL_HALO

    def body(h_ref, hp_ref, lng_ref, lnb_ref, ws_ref, bias_ref, wp_ref, sc_ref, o_ref):
        i = pl.program_id(0)
        u, _, _, _, _, _, s_chunks, X = _ev_common(h_ref, hp_ref, lng_ref, lnb_ref, ws_ref, bias_ref, i, tr, W)
        for c, s in enumerate(s_chunks):
            r0 = c * A_CHUNK
            o_ref[r0:r0 + A_CHUNK, 0:W] = (u[r0:r0 + A_CHUNK] * s).astype(o_ref.dtype)
        for g in range(B_GROUPS):
            p = _pool_fwd(X, g, Wg, i, tr)
            y = _dot(p.astype(_MM), wp_ref[g]) * sc_ref[:, g * Wg:(g + 1) * Wg]
            o_ref[:, W + g * Wg:W + (g + 1) * Wg] = y.astype(o_ref.dtype)

    vec = pl.BlockSpec((1, W), lambda i: (0, 0))
    return pl.pallas_call(
        body, name=name, grid=(T // tr,),
        in_specs=[pl.BlockSpec((3, tr, W), lambda i: (0, i, 0)),
                  pl.BlockSpec((None, POOL_HALO, W), lambda i: (2, jnp.maximum(i * nb - 1, 0), 0)),
                  vec, vec,
                  pl.BlockSpec((H, A_CHUNK, A_CHUNK), lambda i: (0, 0, 0)),
                  pl.BlockSpec((A_CHUNK, W), lambda i: (0, 0)),
                  pl.BlockSpec((B_GROUPS, Wg, Wg), lambda i: (0, 0, 0)),
                  vec],
        out_specs=pl.BlockSpec((tr, 2 * W), lambda i: (i, 0)), out_shape=_S((T, 2 * W), _MM),
        compiler_params=_params(("parallel",)),
    )(h, h, lng, lnb, ws, bias, wp, sc)


def _ev_mid_bwd(h, dy, lng, lnb, ws, bias, wp, sc, name):
    _, T, W = h.shape
    tr = _tile(T, 256)
    H = W // A_HEAD
    Wg = W // B_GROUPS
    nb = tr // POOL_HALO
    last_blk = T // POOL_HALO - 1
    n = tr + POOL_HALO

    def body(h_ref, hp_ref, dy_ref, dyn_ref, lng_ref, lnb_ref, ws_ref, bias_ref, wp_ref, sc_ref,
             dh_ref, dws_ref, dbias_ref, dlng_ref, dlnb_ref, dwp_ref, dsc_ref):
        i = pl.program_id(0)

        @pl.when(i == 0)
        def _():
            for r in (dws_ref, dbias_ref, dlng_ref, dlnb_ref, dwp_ref, dsc_ref):
                r[...] = jnp.zeros_like(r)

        u, gu, gv, rstd, vhat, vb, s_chunks, X = _ev_common(h_ref, hp_ref, lng_ref, lnb_ref, ws_ref, bias_ref, i, tr, W)
        rr = lax.broadcasted_iota(jnp.int32, (A_CHUNK, A_CHUNK), 0)
        cc = lax.broadcasted_iota(jnp.int32, (A_CHUNK, A_CHUNK), 1)
        tril = rr >= cc
        du_chunks, dvln_chunks = [], []
        for c, s in enumerate(s_chunks):
            r0 = c * A_CHUNK
            dya = dy_ref[r0:r0 + A_CHUNK, 0:W]
            du_chunks.append(dya * s)
            ds = dya * u[r0:r0 + A_CHUNK]
            dbias_ref[...] += ds
            dsb = ds.astype(_MM)
            heads = []
            for hd in range(H):
                cols = slice(hd * A_HEAD, (hd + 1) * A_HEAD)
                dws_ref[hd] += jnp.where(tril, _dot(dsb[:, cols], vb[r0:r0 + A_CHUNK, cols], _NT), 0.0)
                heads.append(_dot(ws_ref[hd], dsb[:, cols], _TN))
            dvln_chunks.append(jnp.concatenate(heads, axis=1))
        du = jnp.concatenate(du_chunks, axis=0)
        dvln = jnp.concatenate(dvln_chunks, axis=0)
        dlng_ref[...] += jnp.sum(dvln * vhat, axis=0, keepdims=True)
        dlnb_ref[...] += jnp.sum(dvln, axis=0, keepdims=True)
        dxh = dvln * lng_ref[...]
        dv = rstd * (dxh - jnp.mean(dxh, axis=-1, keepdims=True) - vhat * jnp.mean(dxh * vhat, axis=-1, keepdims=True))
        dh_ref[:, 0:W] = (du * gu).astype(dh_ref.dtype)
        dh_ref[:, W:2 * W] = (dv * gv).astype(dh_ref.dtype)

        dyb = dy_ref[:, W:2 * W]
        dyb_full = jnp.concatenate([dyb, jnp.where(i == pl.num_programs(0) - 1, 0.0, dyn_ref[...])], axis=0)
        for g in range(B_GROUPS):
            cols = slice(g * Wg, (g + 1) * Wg)
            pb = _pool_fwd(X, g, Wg, i, tr).astype(_MM)
            ypre = _dot(pb, wp_ref[g])
            dsc_ref[:, cols] += jnp.sum(dyb[:, cols] * ypre, axis=0, keepdims=True)
            dyp = (dyb_full[:, cols] * sc_ref[:, cols]).astype(_MM)
            dwp_ref[g] += _dot(pb, dyp[0:tr], _TN)
            dp = _dot(dyp, wp_ref[g], _NT)
            s = dp * _pool_inv_count(i, tr, n, 2 ** (g + 1))
            for k in range(g + 1):
                s = s + pltpu.roll(s, n - 2 ** k, 0)
            dh_ref[:, 2 * W + g * Wg:2 * W + (g + 1) * Wg] = (s[0:tr] - dp[0:tr]).astype(dh_ref.dtype)

    vec = pl.BlockSpec((1, W), lambda i: (0, 0))
    ws_spec = pl.BlockSpec((H, A_CHUNK, A_CHUNK), lambda i: (0, 0, 0))
    bias_spec = pl.BlockSpec((A_CHUNK, W), lambda i: (0, 0))
    wp_spec = pl.BlockSpec((B_GROUPS, Wg, Wg), lambda i: (0, 0, 0))
    return pl.pallas_call(
        body, name=name, grid=(T // tr,),
        in_specs=[pl.BlockSpec((3, tr, W), lambda i: (0, i, 0)),
                  pl.BlockSpec((None, POOL_HALO, W), lambda i: (2, jnp.maximum(i * nb - 1, 0), 0)),
                  pl.BlockSpec((tr, 2 * W), lambda i: (i, 0)),
                  pl.BlockSpec((POOL_HALO, W), lambda i: (jnp.minimum((i + 1) * nb, last_blk), 1)),
                  vec, vec, ws_spec, bias_spec, wp_spec, vec],
        out_specs=[pl.BlockSpec((tr, 3 * W), lambda i: (i, 0)), ws_spec, bias_spec, vec, vec, wp_spec, vec],
        out_shape=[_S((T, 3 * W), _MM), _S((H, A_CHUNK, A_CHUNK), _F32), _S((A_CHUNK, W), _F32), _S((1, W), _F32),
                   _S((1, W), _F32), _S((B_GROUPS, Wg, Wg), _F32), _S((1, W), _F32)],
        compiler_params=_params(("arbitrary",)),
    )(h, h, dy, dy, lng, lnb, ws, bias, wp, sc)


def _chunk_cumsum(x, rin):
    s = 1
    while s < C_CHUNK:
        x = x + jnp.where(rin >= s, pltpu.roll(x, s, 0), 0.0)
        s *= 2
    return x


def _chunk_revcumsum(x, rin):
    n = x.shape[0]
    s = 1
    while s < C_CHUNK:
        x = x + jnp.where(rin + s < C_CHUNK, pltpu.roll(x, n - s, 0), 0.0)
        s *= 2
    return x


def _hgrn_gates(q, fl, lb, tr, tc):
    nch = tr // C_CHUNK
    sq = _sigmoid(q)
    sf = _sigmoid_rel(fl)
    f = lb + (1.0 - lb) * sf
    logf = jnp.log(f)
    rin = _row_index(tr) % C_CHUNK
    b = _chunk_cumsum(logf, rin)
    tot3 = jnp.sum(logf.reshape(nch, C_CHUNK, tc), axis=1, keepdims=True)
    eb = jnp.exp(b)
    enb = jnp.exp(-b)
    ekb = jnp.exp(tot3 - b.reshape(nch, C_CHUNK, tc)).reshape(tr, tc)
    return sq, sf, f, rin, tot3, eb, enb, ekb


def _hgrn_prep_fwd(h, lb, name):
    _, T, D = h.shape
    tr = _tile(T, 512)
    tc = _tile(D, 512)
    nch = tr // C_CHUNK

    def body(q_ref, f_ref, v_ref, lb_ref, qd_ref, kd_ref, ke_ref, vb_ref, dec_ref):
        q = q_ref[...]
        sq, _, f, _, tot3, eb, enb, ekb = _hgrn_gates(q, f_ref[...], lb_ref[...], tr, tc)
        kk = 1.0 - f
        qd_ref[...] = (q * sq * eb).astype(qd_ref.dtype)
        kd_ref[...] = (kk * enb).astype(kd_ref.dtype)
        ke_ref[...] = (kk * ekb).astype(ke_ref.dtype)
        vb_ref[...] = v_ref[...].astype(vb_ref.dtype)
        dec_ref[...] = jnp.exp(tot3).reshape(nch, tc)

    def part(p):
        return pl.BlockSpec((None, tr, tc), lambda i, j: (p, i, j))

    blk = pl.BlockSpec((tr, tc), lambda i, j: (i, j))
    return pl.pallas_call(
        body, name=name, grid=(T // tr, D // tc),
        in_specs=[part(0), part(1), part(2), pl.BlockSpec((1, tc), lambda i, j: (0, j))],
        out_specs=[blk, blk, blk, blk, pl.BlockSpec((nch, tc), lambda i, j: (i, j))],
        out_shape=[_S((T, D), _MM)] * 4 + [_S((T // C_CHUNK, D), _F32)],
        compiler_params=_params(("parallel", "parallel")),
    )(h, h, h, lb)


def _tril_mask():
    rr = lax.broadcasted_iota(jnp.int32, (C_CHUNK, C_CHUNK), 0)
    cc = lax.broadcasted_iota(jnp.int32, (C_CHUNK, C_CHUNK), 1)
    return rr >= cc


def _hgrn_scan_fwd(qd, kd, ke, vb, dec, h, ng, name):
    T, D = qd.shape
    NH = D // C_HEAD
    N = T // C_CHUNK

    def body(qd_ref, kd_ref, ke_ref, vb_ref, dec_ref, g_ref, ng_ref, o_ref, y_ref, st_ref):
        mask = _tril_mask()

        per_trip = math.gcd(N, SCAN_UNROLL)

        def trip(i, St):
            ahead = []
            for u in range(per_trip):
                n = i * per_trip + u
                r = pl.ds(pl.multiple_of(n * C_CHUNK, C_CHUNK), C_CHUNK)
                Qd, Kd, Ke, V = qd_ref[r, :], kd_ref[r, :], ke_ref[r, :], vb_ref[r, :]
                att = jnp.where(mask, _dot(Qd, Kd, _NT), 0.0).astype(_MM)
                ahead.append((n, r, _dot(att, V), _dot(V, Ke, _TN)))
            for n, r, o_intra, update in ahead:
                o_ref[r, :] = o_intra + _dot(qd_ref[r, :], St.astype(_MM), _NT)
                st_ref[n] = St
                St = St * dec_ref[pl.ds(n, 1), :] + update
            return St

        lax.fori_loop(0, N // per_trip, trip, jnp.zeros((C_HEAD, C_HEAD), _F32))
        o = o_ref[...]
        r = lax.rsqrt(jnp.mean(o * o, axis=-1, keepdims=True) + LN_EPS)
        y_ref[...] = (o * r * ng_ref[...] * _sigmoid(g_ref[...])).astype(y_ref.dtype)

    col = pl.BlockSpec((T, C_HEAD), lambda j: (0, j))
    return pl.pallas_call(
        body, name=name, grid=(NH,),
        in_specs=[col, col, col, col, pl.BlockSpec((N, C_HEAD), lambda j: (0, j)),
                  pl.BlockSpec((None, T, C_HEAD), lambda j: (3, 0, j)), pl.BlockSpec((1, C_HEAD), lambda j: (0, j))],
        out_specs=[col, col, pl.BlockSpec((None, N, C_HEAD, C_HEAD), lambda j: (j, 0, 0, 0))],
        out_shape=[_S((T, D), _F32), _S((T, D), _MM), _S((NH, N, C_HEAD, C_HEAD), _F32)],
        compiler_params=_params(("parallel",)),
    )(qd, kd, ke, vb, dec, h, ng)


def _hgrn_scan_bwd(qd, kd, ke, vb, dec, st, o, h, ng, dy, name):
    T, D = qd.shape
    NH = D // C_HEAD
    N = T // C_CHUNK

    def body(qd_ref, kd_ref, ke_ref, vb_ref, dec_ref, st_ref, o_ref, g_ref, ng_ref, dy_ref,
             dqd_ref, dkd_ref, dke_ref, dv_ref, dgate_ref, ddec_ref, dng_ref, do_s):
        o = o_ref[...]
        r = lax.rsqrt(jnp.mean(o * o, axis=-1, keepdims=True) + LN_EPS)
        oh = o * r
        gn = ng_ref[...]
        sg = _sigmoid(g_ref[...])
        d = dy_ref[...]
        dyn = d * sg
        dgate_ref[...] = (d * oh * gn * sg * (1.0 - sg)).astype(dgate_ref.dtype)
        dng_ref[...] = jnp.sum(dyn * oh, axis=0, keepdims=True)
        doh = dyn * gn
        do_s[...] = (r * (doh - oh * jnp.mean(doh * oh, axis=-1, keepdims=True))).astype(do_s.dtype)
        mask = _tril_mask()

        per_trip = math.gcd(N, SCAN_UNROLL)

        def trip(i, dSt):
            ahead = []
            for u in range(per_trip):
                n = N - 1 - (i * per_trip + u)
                rws = pl.ds(pl.multiple_of(n * C_CHUNK, C_CHUNK), C_CHUNK)
                Qd, Kd, V, dO = qd_ref[rws, :], kd_ref[rws, :], vb_ref[rws, :], do_s[rws, :]
                att = jnp.where(mask, _dot(Qd, Kd, _NT), 0.0).astype(_MM)
                dA = jnp.where(mask, _dot(dO, V, _NT), 0.0).astype(_MM)
                dqd_ref[rws, :] = _dot(dA, Kd) + _dot(dO, st_ref[n].astype(_MM))
                dkd_ref[rws, :] = _dot(dA, Qd, _TN)
                ahead.append((n, rws, _dot(att, dO, _TN), _dot(dO, Qd, _TN)))
            for n, rws, dv_intra, d_state in ahead:
                dStb = dSt.astype(_MM)
                dv_ref[rws, :] = (dv_intra + _dot(ke_ref[rws, :], dStb, _NT)).astype(dv_ref.dtype)
                dke_ref[rws, :] = _dot(vb_ref[rws, :], dStb)
                ddec_ref[pl.ds(n, 1), :] = jnp.sum(dSt * st_ref[n], axis=0, keepdims=True)
                dSt = dSt * dec_ref[pl.ds(n, 1), :] + d_state
            return dSt

        lax.fori_loop(0, N // per_trip, trip, jnp.zeros((C_HEAD, C_HEAD), _F32))

    col = pl.BlockSpec((T, C_HEAD), lambda j: (0, j))
    chk = pl.BlockSpec((N, C_HEAD), lambda j: (0, j))
    one = pl.BlockSpec((1, C_HEAD), lambda j: (0, j))
    return pl.pallas_call(
        body, name=name, grid=(NH,),
        in_specs=[col, col, col, col, chk, pl.BlockSpec((None, N, C_HEAD, C_HEAD), lambda j: (j, 0, 0, 0)), col,
                  pl.BlockSpec((None, T, C_HEAD), lambda j: (3, 0, j)), one, col],
        out_specs=[col, col, col, col, col, chk, one],
        out_shape=[_S((T, D), _F32)] * 3 + [_S((T, D), _MM)] * 2 + [_S((N, D), _F32), _S((1, D), _F32)],
        scratch_shapes=[pltpu.VMEM((T, C_HEAD), _MM)],
        compiler_params=_params(("parallel",)),
    )(qd, kd, ke, vb, dec, st, o, h, ng, dy)


def _hgrn_prep_bwd(h, lb, dqd, dkd, dke, dv, dgate, ddec, name):
    _, T, D = h.shape
    tr = _tile(T, 512)
    tc = _tile(D, 256)
    nch = tr // C_CHUNK

    def body(q_ref, f_ref, lb_ref, dqd_ref, dkd_ref, dke_ref, dv_ref, dgate_ref, ddec_ref, dh_ref, dlb_ref):
        @pl.when(pl.program_id(1) == 0)
        def _():
            dlb_ref[...] = jnp.zeros_like(dlb_ref)

        q = q_ref[...]
        lb = lb_ref[...]
        sq, sf, f, rin, tot3, eb, enb, ekb = _hgrn_gates(q, f_ref[...], lb, tr, tc)
        kk = 1.0 - f
        dQd, dKd, dKe = dqd_ref[...], dkd_ref[...], dke_ref[...]
        tq = dQd * eb
        tkd = dKd * enb
        tke = dKe * ekb
        ke_term = tke * kk
        db = tq * (q * sq) - tkd * kk - ke_term
        dtot3 = (jnp.sum(ke_term.reshape(nch, C_CHUNK, tc), axis=1, keepdims=True)
                 + (ddec_ref[...] * jnp.exp(tot3).reshape(nch, tc)).reshape(nch, 1, tc))
        dlogf = (_chunk_revcumsum(db, rin).reshape(nch, C_CHUNK, tc) + dtot3).reshape(tr, tc)
        df = dlogf / f - (tkd + tke)
        dh_ref[0] = (tq * sq * (1.0 + q * (1.0 - sq))).astype(dh_ref.dtype)
        dh_ref[1] = (df * (1.0 - lb) * sf * (1.0 - sf)).astype(dh_ref.dtype)
        dh_ref[2] = dv_ref[...]
        dh_ref[3] = dgate_ref[...]
        dlb_ref[...] += jnp.sum(df * (1.0 - sf), axis=0, keepdims=True)

    def part(p):
        return pl.BlockSpec((None, tr, tc), lambda j, i: (p, i, j))

    blk = pl.BlockSpec((tr, tc), lambda j, i: (i, j))
    vec = pl.BlockSpec((1, tc), lambda j, i: (0, j))
    return pl.pallas_call(
        body, name=name, grid=(D // tc, T // tr),
        in_specs=[part(0), part(1), vec, blk, blk, blk, blk, blk, pl.BlockSpec((nch, tc), lambda j, i: (i, j))],
        out_specs=[pl.BlockSpec((4, tr, tc), lambda j, i: (0, i, j)), vec],
        out_shape=[_S((4, T, D), _MM), _S((1, D), _F32)],
        compiler_params=_params(("parallel", "arbitrary")),
    )(h, h, lb, dqd, dkd, dke, dv, dgate, ddec)


def _sum_in_device_order(me1, own, land, name):
    R, C = own.shape
    tr = _tile(R, 256)

    def body(me_ref, own_ref, land_ref, o_ref):
        me = me_ref[0]
        g = None
        for j in range(N_DEV):
            slot = jnp.maximum(jnp.bitwise_xor(me, j) - 1, 0)
            p = jnp.where(me == j, own_ref[...], land_ref[slot])
            g = p if g is None else g + p
        o_ref[...] = g

    return pl.pallas_call(
        body, name=name,
        grid_spec=pltpu.PrefetchScalarGridSpec(
            num_scalar_prefetch=1, grid=(R // tr,),
            in_specs=[pl.BlockSpec((tr, C), lambda i, me: (i, 0)), pl.BlockSpec((N_DEV - 1, tr, C), lambda i, me: (0, i, 0))],
            out_specs=pl.BlockSpec((tr, C), lambda i, me: (i, 0))),
        out_shape=_S((R, C), _F32), compiler_params=_params(("parallel",)),
    )(me1, own, land)


def _adamw(parts, w, m, v, name):
    P, R, C = parts.shape
    tr = _tile(R, 128) if R % LANE == 0 else R

    def body(p_ref, w_ref, m_ref, v_ref, g_ref, d_ref, nm_ref, nv_ref):
        g = p_ref[0].astype(_F32)
        for s in range(1, P):
            g = g + p_ref[s].astype(_F32)
        nm = ADAM_B1 * m_ref[...] + (1.0 - ADAM_B1) * g
        nv = ADAM_B2 * v_ref[...] + (1.0 - ADAM_B2) * (g * g)
        m_hat = nm / (1.0 - ADAM_B1 ** ADAM_STEP)
        v_hat = nv / (1.0 - ADAM_B2 ** ADAM_STEP)
        g_ref[...] = g
        d_ref[...] = -ADAM_LR * (m_hat / (jnp.sqrt(v_hat) + ADAM_EPS) + ADAM_WD * w_ref[...])
        nm_ref[...] = nm
        nv_ref[...] = nv

    blk = pl.BlockSpec((tr, C), lambda i: (i, 0))
    return pl.pallas_call(
        body, name=name, grid=(R // tr,), in_specs=[pl.BlockSpec((P, tr, C), lambda i: (0, i, 0)), blk, blk, blk],
        out_specs=[blk] * 4, out_shape=[_S((R, C), _F32)] * 4, compiler_params=_params(("parallel",)),
    )(parts, w, m, v)


def _exchange(name, srcs, out_shapes, jobs, deps=()):
    ns, nj = len(srcs), len(jobs)

    nd = len(deps)

    def body(*refs):
        ins, outs = refs[:ns], refs[ns + nd:ns + nd + len(out_shapes)]
        send_sems, recv_sems, local_sems = refs[-3:]
        x, y, c = lax.axis_index("x"), lax.axis_index("y"), lax.axis_index("c")
        me = 4 * x + 2 * y + c
        local = []
        for ji, (si, src_fn, di, dst_fn) in enumerate(jobs):
            cp = pltpu.make_async_copy(src_fn(ins[si], me, me), dst_fn(outs[di], me), local_sems.at[ji])
            cp.start()
            local.append(cp)
        remote = []
        for k in range(1, N_DEV):
            px, py, pc = (x + (k >> 2)) % 2, (y + ((k >> 1) & 1)) % 2, (c + (k & 1)) % 2
            to = 4 * px + 2 * py + pc
            for ji, (si, src_fn, di, dst_fn) in enumerate(jobs):
                sem = (k - 1) * nj + ji
                cp = pltpu.make_async_remote_copy(
                    src_ref=src_fn(ins[si], me, to), dst_ref=dst_fn(outs[di], me),
                    send_sem=send_sems.at[sem], recv_sem=recv_sems.at[sem],
                    device_id=(px, py, pc), device_id_type=pl.DeviceIdType.MESH)
                cp.start()
                remote.append(cp)
        for cp in remote:
            cp.wait_recv()
        for cp in remote:
            cp.wait_send()
        for cp in local:
            cp.wait()

    hbm = pl.BlockSpec(memory_space=pltpu.HBM)
    return pl.pallas_call(
        body, name=name, in_specs=[hbm] * ns + [_ANY] * nd, out_specs=[hbm] * len(out_shapes), out_shape=list(out_shapes),
        scratch_shapes=[pltpu.SemaphoreType.DMA(((N_DEV - 1) * nj,)), pltpu.SemaphoreType.DMA(((N_DEV - 1) * nj,)),
                        pltpu.SemaphoreType.DMA((nj,))],
    )(*srcs, *deps)


def _whole(ref, me, to):
    return ref


def _slot_job(i, o):
    def dst(ref, me):
        return ref.at[me]
    return (i, _whole, o, dst)


_HBM = pl.BlockSpec(memory_space=pltpu.HBM)
_SEM = pl.BlockSpec(memory_space=pltpu.SEMAPHORE)
_ANY = pl.BlockSpec(memory_space=pl.ANY)
_N_PEER = N_DEV - 1


def _split_params():
    return pltpu.CompilerParams(has_side_effects=pltpu.SideEffectType.DATAFLOW_SIDE_EFFECTING)


def _blk(ref, axis, n, idx):
    if axis is None:
        return ref
    return ref.at[tuple([slice(None)] * axis + [pl.ds(pl.multiple_of(idx * n, n), n)])]


def _peer(k):
    x, y, c = lax.axis_index("x"), lax.axis_index("y"), lax.axis_index("c")
    px, py, pc = (x + (k >> 2)) % 2, (y + ((k >> 1) & 1)) % 2, (c + (k & 1)) % 2
    return (px, py, pc), 4 * px + 2 * py + pc, 4 * x + 2 * y + c


def _row_tile(rows, pref):
    best = None
    for d in range(16, min(rows, pref) + 1, 16):
        if rows % d == 0:
            best = d
    return best if best is not None else rows


def _place(w, me1, axis, name, layer=None, deps=()):
    R, C = w.shape[-2:]
    tr = _row_tile(R, 512)
    nb = R // tr
    lead = () if layer is None else (None,)
    pre = () if layer is None else (layer,)

    def body(me_ref, w_ref, *rest):
        rest[-1][...] = w_ref[...].astype(rest[-1].dtype)

    if axis == 1:
        out_spec = pl.BlockSpec((tr, C), lambda i, me: (i, me[0]))
        out_shape = _S((R, N_DEV * C), _MM)
    else:
        out_spec = pl.BlockSpec((tr, C), lambda i, me: (me[0] * nb + i, 0))
        out_shape = _S((N_DEV * R, C), _MM)
    return pl.pallas_call(
        body, name=name,
        grid_spec=pltpu.PrefetchScalarGridSpec(
            num_scalar_prefetch=1, grid=(nb,),
            in_specs=[pl.BlockSpec(lead + (tr, C), lambda i, me: pre + (i, 0))] + [_ANY] * len(deps), out_specs=out_spec),
        out_shape=out_shape, compiler_params=_params(("parallel",)),
    )(me1, w, *deps)


_SIBLING = 1
_CHIPS = (2, 4, 6)
_VMEM_TOKEN = pl.BlockSpec(memory_space=pltpu.VMEM)


def _remote(ref_blk, send_sem, recv_sem, dev):
    return pltpu.make_async_remote_copy(src_ref=ref_blk, dst_ref=ref_blk, send_sem=send_sem, recv_sem=recv_sem,
                                        device_id=dev, device_id_type=pl.DeviceIdType.MESH)


def _gather_start(name, full, axis, n):
    def body(f_ref, send, recv, f_out, token):
        for i, k in enumerate((_SIBLING,) + _CHIPS):
            dev, _, me = _peer(k)
            _remote(_blk(f_ref, axis, n, me), send.at[i], recv.at[i], dev).start()
        token[...] = jnp.zeros_like(token)

    return pl.pallas_call(
        body, name=name,
        out_shape=(pltpu.SemaphoreType.DMA((4,)), pltpu.SemaphoreType.DMA((4,)), pltpu.HBM(full.shape, full.dtype),
                   _S((8, LANE), _F32)),
        in_specs=(_HBM,), out_specs=(_SEM, _SEM, _HBM, _VMEM_TOKEN),
        input_output_aliases={0: 2}, compiler_params=_split_params(),
    )(pltpu.with_memory_space_constraint(full, pltpu.HBM))


def _gather_forward(name, full, axis, n, recv, after):
    after = tuple(after) if isinstance(after, (tuple, list)) else (after,)

    def body(f_ref, recv_r, *rest):
        send2, recv2, f_out, token = rest[-4:]
        sib, _, _ = _peer(_SIBLING)
        for i, k in enumerate(_CHIPS):
            dev, frm, _ = _peer(k)
            blk = _blk(f_ref, axis, n, frm)
            _remote(blk, send2.at[i], recv_r.at[1 + i], dev).wait_recv()
            _remote(blk, send2.at[i], recv2.at[i], sib).start()
        token[...] = jnp.zeros_like(token)

    return pl.pallas_call(
        body, name=name,
        out_shape=(pltpu.SemaphoreType.DMA((3,)), pltpu.SemaphoreType.DMA((3,)), pltpu.HBM(full.shape, full.dtype),
                   _S((8, LANE), _F32)),
        in_specs=(_HBM, _SEM) + (_ANY,) * len(after), out_specs=(_SEM, _SEM, _HBM, _VMEM_TOKEN),
        input_output_aliases={0: 2}, compiler_params=_split_params(),
    )(full, recv, *after)


def _gather_wait(name, full, axis, n, send, recv, send2, recv2, after):
    def body(f_ref, send_r, recv_r, send2_r, recv2_r, after_ref, f_out):
        sib, _, me = _peer(_SIBLING)
        blk = _blk(f_ref, axis, n, me)
        for i in range(4):
            _remote(blk, send_r.at[i], recv_r.at[0], sib).wait_send()
        _remote(blk, send_r.at[0], recv_r.at[0], sib).wait_recv()
        for i in range(3):
            cp = _remote(blk, send2_r.at[i], recv2_r.at[i], sib)
            cp.wait_send()
            cp.wait_recv()

    return pl.pallas_call(
        body, name=name, out_shape=pltpu.HBM(full.shape, full.dtype),
        in_specs=(_HBM, _SEM, _SEM, _SEM, _SEM, _ANY), out_specs=_HBM,
        input_output_aliases={0: 0}, compiler_params=_split_params(),
    )(full, send, recv, send2, recv2, after)


def _scatter_start(name, dw, axis, n):
    shard = tuple(n if a == axis else d for a, d in enumerate(dw.shape))
    land = lax.empty((_N_PEER,) + shard, dw.dtype)

    def body(dw_ref, land_ref, send, recv, dw_out, land_out, token):
        for k in range(1, N_DEV):
            dev, to, _ = _peer(k)
            pltpu.make_async_remote_copy(
                src_ref=_blk(dw_ref, axis, n, to), dst_ref=land_ref.at[k - 1], send_sem=send.at[k - 1],
                recv_sem=recv.at[k - 1], device_id=dev, device_id_type=pl.DeviceIdType.MESH).start()
        token[...] = jnp.zeros_like(token)

    return pl.pallas_call(
        body, name=name,
        out_shape=(pltpu.SemaphoreType.DMA((_N_PEER,)), pltpu.SemaphoreType.DMA((_N_PEER,)),
                   pltpu.HBM(dw.shape, dw.dtype), pltpu.HBM(land.shape, land.dtype), _S((8, LANE), _F32)),
        in_specs=(_HBM, _HBM), out_specs=(_SEM, _SEM, _HBM, _HBM, pl.BlockSpec(memory_space=pltpu.VMEM)),
        input_output_aliases={0: 2, 1: 3}, compiler_params=_split_params(),
    )(pltpu.with_memory_space_constraint(dw, pltpu.HBM), pltpu.with_memory_space_constraint(land, pltpu.HBM))


def _scatter_wait(name, items, after):
    ne = len(items)
    after = tuple(after) if isinstance(after, (tuple, list)) else (after,)

    def body(*refs):
        for e, (_, _, _, _, axis, n) in enumerate(items):
            dw_ref, land_ref, send_r, recv_r = refs[4 * e:4 * e + 4]
            for k in range(1, N_DEV):
                dev, to, _ = _peer(k)
                cp = pltpu.make_async_remote_copy(
                    src_ref=_blk(dw_ref, axis, n, to), dst_ref=land_ref.at[k - 1], send_sem=send_r.at[k - 1],
                    recv_sem=recv_r.at[k - 1], device_id=dev, device_id_type=pl.DeviceIdType.MESH)
                cp.wait_send()
                cp.wait_recv()

    args, out_shape = [], []
    for dw, land, send, recv, _, _ in items:
        args += [dw, land, send, recv]
        out_shape += [pltpu.HBM(dw.shape, dw.dtype), pltpu.HBM(land.shape, land.dtype)]
    res = pl.pallas_call(
        body, name=name, out_shape=tuple(out_shape),
        in_specs=(_HBM, _HBM, _SEM, _SEM) * ne + (_ANY,) * len(after), out_specs=(_HBM,) * (2 * ne),
        input_output_aliases={4 * e + j: 2 * e + j for e in range(ne) for j in range(2)},
        compiler_params=_split_params(),
    )(*args, *after)
    return [(res[2 * e], res[2 * e + 1]) for e in range(ne)]


def _adamw_big(me1, dw, land, w, m, v, axis, n, name, layer=None, into=None):
    R, C = land.shape[1:]
    tr = _row_tile(R, 128)
    nb = R // tr
    lead = () if layer is None else (None,)
    pre = () if layer is None else (layer,)

    def body(me_ref, own_ref, land_ref, w_ref, m_ref, v_ref, *rest):
        g_ref, d_ref, nm_ref, nv_ref = rest[-4:]
        g = own_ref[...].astype(_F32)
        for s in range(_N_PEER):
            g = g + land_ref[s].astype(_F32)
        nm = ADAM_B1 * m_ref[...] + (1.0 - ADAM_B1) * g
        nv = ADAM_B2 * v_ref[...] + (1.0 - ADAM_B2) * (g * g)
        m_hat = nm / (1.0 - ADAM_B1 ** ADAM_STEP)
        v_hat = nv / (1.0 - ADAM_B2 ** ADAM_STEP)
        g_ref[...] = g
        d_ref[...] = -ADAM_LR * (m_hat / (jnp.sqrt(v_hat) + ADAM_EPS) + ADAM_WD * w_ref[...])
        nm_ref[...] = nm
        nv_ref[...] = nv

    if axis == 1:
        own_spec = pl.BlockSpec((tr, C), lambda i, me: (i, me[0]))
    else:
        own_spec = pl.BlockSpec((tr, C), lambda i, me: (me[0] * nb + i, 0))
    blk = pl.BlockSpec(lead + (tr, C), lambda i, me: pre + (i, 0))
    in_specs = [own_spec, pl.BlockSpec((_N_PEER, tr, C), lambda i, me: (0, i, 0)), blk, blk, blk]
    args = [me1, dw, land, w, m, v]
    aliases = {}
    if into is not None:
        in_specs += [_ANY] * 4
        aliases = {6 + j: j for j in range(4)}
        args += list(into)
    return pl.pallas_call(
        body, name=name,
        grid_spec=pltpu.PrefetchScalarGridSpec(num_scalar_prefetch=1, grid=(nb,), in_specs=in_specs, out_specs=[blk] * 4),
        out_shape=[_S(w.shape, _F32)] * 4, input_output_aliases=aliases, compiler_params=_params(("parallel",)),
    )(*args)


def _pack(arrs):
    flat = jnp.concatenate([a.reshape(-1).astype(_F32) for a in arrs])
    pad = (-flat.shape[0]) % (LANE * LANE)
    return jnp.pad(flat, (0, pad)).reshape(-1, LANE)


def _unpack(mat, shapes):
    flat = mat.reshape(-1)
    out, off = [], 0
    for s in shapes:
        n = 1
        for d in s:
            n *= d
        out.append(flat[off:off + n].reshape(s))
        off += n
    return out


def _lb_of(lb_param):
    lb_all = jnp.cumsum(jax.nn.softmax(lb_param.astype(_F32), axis=0), axis=0)
    return (lb_all - lb_all[0])[1:2]


def kernel(x, ev_w_in, ev_ln_v_g, ev_ln_v_b, ev_w_s, ev_b_s, ev_w_pool, ev_pool_scale, ev_w_out, od_w_in, od_norm_g, od_w_out, lb_param, ffn_w_up, ffn_conv_w, ffn_conv_b, ffn_w_down, ln1_g, ln1_b, ln2_g, ln2_b, loss_target, m_ev_w_in, m_ev_ln_v_g, m_ev_ln_v_b, m_ev_w_s, m_ev_b_s, m_ev_w_pool, m_ev_pool_scale, m_ev_w_out, m_od_w_in, m_od_norm_g, m_od_w_out, m_lb_param, m_ffn_w_up, m_ffn_conv_w, m_ffn_conv_b, m_ffn_w_down, m_ln1_g, m_ln1_b, m_ln2_g, m_ln2_b, v_ev_w_in, v_ev_ln_v_g, v_ev_ln_v_b, v_ev_w_s, v_ev_b_s, v_ev_w_pool, v_ev_pool_scale, v_ev_w_out, v_od_w_in, v_od_norm_g, v_od_w_out, v_lb_param, v_ffn_w_up, v_ffn_conv_w, v_ffn_conv_b, v_ffn_w_down, v_ln1_g, v_ln1_b, v_ln2_g, v_ln2_b):
    me = 4 * lax.axis_index("x") + 2 * lax.axis_index("y") + lax.axis_index("c")
    T, D = x.shape[1], x.shape[2]
    W = ev_ln_v_g.shape[1]
    H = W // A_HEAD
    Wg = W // B_GROUPS
    F2 = ffn_conv_b.shape[1]
    F = F2 // 2
    n_in0, n_out0 = ev_w_in.shape[2], ev_w_out.shape[1]
    n_in1, n_out1 = od_w_in.shape[2], od_w_out.shape[1]
    n_up, n_dn = ffn_w_up.shape[2], ffn_w_down.shape[1]
    n_pool, n_ng, n_cw = ev_w_pool.shape[2], od_norm_g.shape[1], ffn_conv_w.shape[2]

    small_shards = [od_norm_g, ffn_conv_w, ev_w_pool]
    small_pack = _pack(small_shards)
    small_all = _exchange("gather_small_params", [small_pack], [_S((N_DEV,) + small_pack.shape, _F32)], [_slot_job(0, 0)])[0]

    me1 = me.astype(jnp.int32).reshape(1)
    weights = [
        ("w_in0", ev_w_in[0], None, 1, n_in0), ("w_out0", ev_w_out[0], None, 0, n_out0),
        ("w_up0", ffn_w_up, 0, 1, n_up), ("w_dn0", ffn_w_down, 0, 0, n_dn),
        ("w_in1", od_w_in[0], None, 1, n_in1), ("w_out1", od_w_out[0], None, 0, n_out1),
        ("w_up1", ffn_w_up, 1, 1, n_up), ("w_dn1", ffn_w_down, 1, 0, n_dn),
    ]
    started, tokens = {}, [small_all]
    for key, w, layer, axis, n in weights:
        full = _place(w, me1, axis, "place_" + key, layer, deps=tokens)
        send, recv, full, token = _gather_start("gather_start_" + key, full, axis, n)
        started[key] = (full, axis, n, send, recv)
        tokens = [token]

    def pass_on(key, after):
        full, axis, n, send, recv = started[key]
        send2, recv2, full, token = _gather_forward("gather_forward_" + key, full, axis, n, recv, after)
        started[key] = (full, axis, n, send, recv, send2, recv2)
        return token

    def gathered(key, after):
        return _gather_wait("gather_wait_" + key, *started[key], after)

    ng_parts, cw_parts, wp_parts = [], [], []
    for j in range(N_DEV):
        a, b, c = _unpack(small_all[j], [s.shape for s in small_shards])
        ng_parts.append(a)
        cw_parts.append(b)
        wp_parts.append(c)
    norm_g = jnp.concatenate(ng_parts, axis=1)
    conv_w = jnp.concatenate(cw_parts, axis=2)
    w_pool = jnp.concatenate(wp_parts, axis=2)[0]
    cw_l = [conv_w[l].reshape(3, 2, F).transpose(1, 0, 2) for l in range(DEPTH)]
    cb_l = [ffn_conv_b[l].reshape(2, 1, F) for l in range(DEPTH)]
    ws_tril = jnp.tril(ev_w_s[0]).astype(_MM)
    bias = jnp.repeat(ev_b_s[0].T, A_HEAD, axis=1)
    wp_b = w_pool.astype(_MM)
    lb, lb_vjp = jax.vjp(_lb_of, lb_param)

    small_names = ["ev_ln_v_g", "ev_ln_v_b", "ev_w_s", "ev_b_s", "ev_w_pool", "ev_pool_scale", "od_norm_g", "lb_param",
                   "ffn_conv_w", "ffn_conv_b", "ln1_g", "ln1_b", "ln2_g", "ln2_b"]
    given = dict(ev_ln_v_g=(ev_ln_v_g, m_ev_ln_v_g, v_ev_ln_v_g), ev_ln_v_b=(ev_ln_v_b, m_ev_ln_v_b, v_ev_ln_v_b),
                 ev_w_s=(ev_w_s, m_ev_w_s, v_ev_w_s), ev_b_s=(ev_b_s, m_ev_b_s, v_ev_b_s),
                 ev_w_pool=(ev_w_pool, m_ev_w_pool, v_ev_w_pool),
                 ev_pool_scale=(ev_pool_scale, m_ev_pool_scale, v_ev_pool_scale),
                 od_norm_g=(od_norm_g, m_od_norm_g, v_od_norm_g), lb_param=(lb_param, m_lb_param, v_lb_param),
                 ffn_conv_w=(ffn_conv_w, m_ffn_conv_w, v_ffn_conv_w), ffn_conv_b=(ffn_conv_b, m_ffn_conv_b, v_ffn_conv_b),
                 ln1_g=(ln1_g, m_ln1_g, v_ln1_g), ln1_b=(ln1_b, m_ln1_b, v_ln1_b), ln2_g=(ln2_g, m_ln2_g, v_ln2_g),
                 ln2_b=(ln2_b, m_ln2_b, v_ln2_b))
    shard_axis = dict(ev_w_pool=2, od_norm_g=1, ffn_conv_w=2)
    rep_names = [n for n in small_names if n not in shard_axis]
    shd_names = [n for n in small_names if n in shard_axis]
    small_packs = [_pack([given[n][j] for n in small_names]) for j in range(3)]

    x2 = x[0]
    xb = _cast(x2, _MM, "cast_x", deps=[pass_on("w_in0", tokens[0])])
    w_in0 = gathered("w_in0", xb)
    h0 = _mm(xb, w_in0, "nn", _F32, "ev_in", out_parts=3)
    tie = pass_on("w_out0", h0)
    yab = _ev_mid_fwd(h0, ev_ln_v_g + tie[0, 0], ev_ln_v_b, ws_tril, bias, wp_b, ev_pool_scale, "ev_mid_fwd")
    w_out0 = gathered("w_out0", yab)
    z1 = _mm(yab, w_out0, "nn", _F32, "ev_out", add=x2, add_scale=ALPHA)
    tie = pass_on("w_up0", (z1, *small_packs))
    x1, x1b = _ln_fwd(z1, ln1_g[0:1] + tie[0, 0], ln1_b[0:1], "ln1_0")
    w_up0 = gathered("w_up0", x1b)
    hf0 = _mm(x1b, w_up0, "nn", _F32, "ffn_up", out_parts=2)
    tie = pass_on("w_dn0", hf0)
    act0, hc0 = _ffn_mid_fwd(hf0, cw_l[0], cb_l[0] + tie[0, 0], "ffn_mid_fwd")
    w_dn0 = gathered("w_dn0", act0)
    z2 = _mm(act0, w_dn0, "nn", _F32, "ffn_down", add=x1, add_scale=ALPHA)
    tie = pass_on("w_in1", z2)
    x2_, x2b = _ln_fwd(z2, ln2_g[0:1] + tie[0, 0], ln2_b[0:1], "ln2_0")
    w_in1 = gathered("w_in1", x2b)
    h1 = _mm(x2b, w_in1, "nn", _F32, "od_in", out_parts=4)
    qd, kd, ke, vb, dec = _hgrn_prep_fwd(h1, lb, "hgrn_prep_fwd")
    tie = pass_on("w_out1", qd)
    o, yo, st = _hgrn_scan_fwd(qd, kd, ke, vb, dec, h1, norm_g + tie[0, 0], "hgrn_scan_fwd")
    w_out1 = gathered("w_out1", yo)
    tie = pass_on("w_up1", yo)
    z3, x3, x3b = _mm(yo, w_out1, "nn", _F32, "od_out", add=x2_, add_scale=ALPHA, ln=(ln1_g[1:2], ln1_b[1:2]), deps=[tie])
    w_up1 = gathered("w_up1", x3b)
    hf1 = _mm(x3b, w_up1, "nn", _F32, "ffn_up", out_parts=2)
    tie = pass_on("w_dn1", hf1)
    act1, hc1 = _ffn_mid_fwd(hf1, cw_l[1], cb_l[1] + tie[0, 0], "ffn_mid_fwd")
    w_dn1 = gathered("w_dn1", act1)
    z4 = _mm(act1, w_dn1, "nn", _F32, "ffn_down", add=x3, add_scale=ALPHA)

    scat = {}

    def scatter(key, dw, axis, n):
        send, recv, dw, land, token = _scatter_start("scatter_start_" + key, dw, axis, n)
        scat[key] = (dw, land, send, recv, axis, n)
        return [token]

    loss11, dz4, dz4b, g_ln2_1, b_ln2_1 = _ln_loss_bwd(z4, ln2_g[1:2], ln2_b[1:2], loss_target[0], "ln_loss_bwd")
    tok = scatter("dn1", _mm(act1, dz4b, "tn", _XCH, "ffn_down_dw"), 0, n_dn)
    dact1 = _mm(dz4b, w_dn1, "nt", _MM, "ffn_down_dx", deps=tok)
    dhf1, dcw1, dcb1 = _ffn_mid_bwd(hf1, hc1, dact1, cw_l[1], "ffn_mid_bwd")
    tok = scatter("up1", _mm(x3b, dhf1, "tn", _XCH, "ffn_up_dw", b_parts=2, deps=tok), 1, n_up)
    dx3 = _mm(dhf1, w_up1, "nt", _F32, "ffn_up_dx", a_parts=2, add=dz4, add_scale=ALPHA, deps=tok)
    dz3, dz3b, g_ln1_1, b_ln1_1 = _ln_bwd(z3, ln1_g[1:2], dx3, "ln_bwd")
    tok = scatter("out1", _mm(yo, dz3b, "tn", _XCH, "od_out_dw", deps=tok), 0, n_out1)
    dyo = _mm(dz3b, w_out1, "nt", _F32, "od_out_dx", deps=tok)
    dqd, dkd, dke, dv, dgate, ddec, dng = _hgrn_scan_bwd(qd, kd, ke, vb, dec, st, o, h1, norm_g, dyo, "hgrn_scan_bwd")
    dh1, dlb = _hgrn_prep_bwd(h1, lb, dqd, dkd, dke, dv, dgate, ddec, "hgrn_prep_bwd")
    tok = scatter("in1", _mm(x2b, dh1, "tn", _XCH, "od_in_dw", b_parts=4, deps=tok), 1, n_in1)
    dx2 = _mm(dh1, w_in1, "nt", _F32, "od_in_dx", a_parts=4, add=dz3, add_scale=ALPHA, deps=tok)
    dz2, dz2b, g_ln2_0, b_ln2_0 = _ln_bwd(z2, ln2_g[0:1], dx2, "ln_bwd")
    tok = scatter("dn0", _mm(act0, dz2b, "tn", _XCH, "ffn_down_dw", deps=tok), 0, n_dn)
    dact0 = _mm(dz2b, w_dn0, "nt", _MM, "ffn_down_dx", deps=tok)
    dhf0, dcw0, dcb0 = _ffn_mid_bwd(hf0, hc0, dact0, cw_l[0], "ffn_mid_bwd")
    tok = scatter("up0", _mm(x1b, dhf0, "tn", _XCH, "ffn_up_dw", b_parts=2, deps=tok), 1, n_up)
    dx1 = _mm(dhf0, w_up0, "nt", _F32, "ffn_up_dx", a_parts=2, add=dz2, add_scale=ALPHA, deps=tok)
    dz1, dz1b, g_ln1_0, b_ln1_0 = _ln_bwd(z1, ln1_g[0:1], dx1, "ln_bwd")
    tok = scatter("out0", _mm(yab, dz1b, "tn", _XCH, "ev_out_dw", deps=tok), 0, n_out0)
    dyab = _mm(dz1b, w_out0, "nt", _F32, "ev_out_dx", deps=tok)
    dh0, dws, dbias, dlng, dlnb, dwp, dsc = _ev_mid_bwd(h0, dyab, ev_ln_v_g, ev_ln_v_b, ws_tril, bias, wp_b,
                                                        ev_pool_scale, "ev_mid_bwd")

    g_b_s = dbias.reshape(A_CHUNK, H, A_HEAD).sum(axis=-1).T[None]
    g_conv_w = jnp.stack([d.transpose(1, 0, 2).reshape(3, F2) for d in (dcw0, dcw1)])
    g_conv_b = jnp.stack([d.reshape(F2) for d in (dcb0, dcb1)])
    small_grads = dict(zip(small_names, [
        dlng, dlnb, dws[None], g_b_s, dwp[None], dsc, dng, lb_vjp(dlb)[0], g_conv_w, g_conv_b,
        jnp.concatenate([g_ln1_0, g_ln1_1]), jnp.concatenate([b_ln1_0, b_ln1_1]),
        jnp.concatenate([g_ln2_0, g_ln2_1]), jnp.concatenate([b_ln2_0, b_ln2_1])]))

    def by_device(g, ax):
        g = g.reshape(g.shape[:ax] + (N_DEV, g.shape[ax] // N_DEV) + g.shape[ax + 1:])
        return jnp.moveaxis(g, ax, 0).reshape(N_DEV, -1)

    shd = jnp.concatenate([by_device(small_grads[n], shard_axis[n]) for n in shd_names], axis=1)
    shd_pack = jnp.pad(shd, ((0, 0), (0, (-shd.shape[1]) % (LANE * LANE)))).reshape(-1, LANE)
    shd_rows = shd_pack.shape[0] // N_DEV
    rep_pack = _pack([small_grads[n] for n in rep_names])
    tok = scatter("in0", _mm(xb, dh0, "tn", _XCH, "ev_in_dw", deps=tok), 1, n_in0)
    tok = scatter("small_rep", rep_pack + tok[0][0, 0], None, None)
    tok = scatter("small_shd", shd_pack + tok[0][0, 0], 0, shd_rows)
    grad_x = _mm(dh0, w_in0, "nt", _F32, "ev_in_dx", add=dz1, add_scale=ALPHA, deps=tok)

    def landed(name, keys, after):
        got = _scatter_wait(name, [scat[k] for k in keys], after)
        return {k: (me1, dw, land) for k, (dw, land) in zip(keys, got)}

    early = landed("scatter_wait_early", ["dn1", "up1", "out1", "in1", "dn0", "up0", "out0"], grad_x)
    big = {}
    r_dn = _adamw_big(*early["dn1"], ffn_w_down, m_ffn_w_down, v_ffn_w_down, 0, n_dn, "adamw_w_dn1", layer=1)
    r_up = _adamw_big(*early["up1"], ffn_w_up, m_ffn_w_up, v_ffn_w_up, 1, n_up, "adamw_w_up1", layer=1)
    big["od_w_out"] = _adamw_big(*early["out1"], od_w_out[0], m_od_w_out[0], v_od_w_out[0], 0, n_out1, "adamw_w_out1")
    big["od_w_in"] = _adamw_big(*early["in1"], od_w_in[0], m_od_w_in[0], v_od_w_in[0], 1, n_in1, "adamw_w_in1")
    big["ffn_w_down"] = _adamw_big(*early["dn0"], ffn_w_down, m_ffn_w_down, v_ffn_w_down, 0, n_dn, "adamw_w_dn0", layer=0, into=r_dn)
    big["ffn_w_up"] = _adamw_big(*early["up0"], ffn_w_up, m_ffn_w_up, v_ffn_w_up, 1, n_up, "adamw_w_up0", layer=0, into=r_up)
    big["ev_w_out"] = _adamw_big(*early["out0"], ev_w_out[0], m_ev_w_out[0], v_ev_w_out[0], 0, n_out0, "adamw_w_out0")
    late = landed("scatter_wait_late", ["in0", "small_rep", "small_shd"],
                  (big["ffn_w_down"][0], big["ffn_w_up"][0], big["od_w_in"][0], big["ev_w_out"][0]))
    big["ev_w_in"] = _adamw_big(*late["in0"], ev_w_in[0], m_ev_w_in[0], v_ev_w_in[0], 1, n_in0, "adamw_w_in0")

    rep_mat = _sum_in_device_order(*late["small_rep"], "sum_small_rep")
    local_g = dict(zip(rep_names, _unpack(rep_mat, [small_grads[n].shape for n in rep_names])))
    _, shd_all, shd_land = late["small_shd"]
    shd_own = lax.dynamic_slice_in_dim(shd_all, me * shd_rows, shd_rows, axis=0)
    shd_mat = _sum_in_device_order(me1, shd_own, shd_land, "sum_small_shd")
    local_g.update(zip(shd_names, _unpack(shd_mat, [given[n][0].shape for n in shd_names])))
    local_shapes = [given[n][0].shape for n in small_names]
    res = _adamw(_pack([local_g[n] for n in small_names])[None], *small_packs, "adamw_small")
    small = {n: [] for n in small_names}
    for r in res:
        for n, a in zip(small_names, _unpack(r, local_shapes)):
            small[n].append(a)

    loss = lax.psum(loss11[0, 0], ("x", "y", "c"))
    order = ["ev_w_in", "ev_ln_v_g", "ev_ln_v_b", "ev_w_s", "ev_b_s", "ev_w_pool", "ev_pool_scale", "ev_w_out", "od_w_in",
             "od_norm_g", "od_w_out", "lb_param", "ffn_w_up", "ffn_conv_w", "ffn_conv_b", "ffn_w_down", "ln1_g", "ln1_b",
             "ln2_g", "ln2_b"]
    shapes = dict(ev_w_in=ev_w_in.shape, ev_w_out=ev_w_out.shape, od_w_in=od_w_in.shape, od_w_out=od_w_out.shape,
                  ffn_w_up=ffn_w_up.shape, ffn_w_down=ffn_w_down.shape)
    outs = [loss, grad_x[None]]
    for kind in range(4):
        for n in order:
            outs.append(big[n][kind].reshape(shapes[n]) if n in big else small[n][kind])
    return tuple(outs)
```

```python
import functools
import math

import jax
import jax.numpy as jnp
from jax import lax
from jax.experimental import pallas as pl
from jax.experimental.pallas import tpu as pltpu

_MM = jnp.bfloat16
_XCH = jnp.bfloat16

DEPTH = 2
ALPHA = (2 * DEPTH) ** 0.25
LN_EPS = 1e-5
A_CHUNK = 128
A_HEAD = 128
B_GROUPS = 4
POOL_HALO = 16
C_CHUNK = 64
C_HEAD = 128
SCAN_UNROLL = 64
CONV_HALO = 8
PACKED_ROWS = 16
FFN_ROWS, FFN_FWD_COLS = 512, 1408
FFN_BWD_ROWS, FFN_BWD_COLS = 512, 512
FFN_CHUNK = 128
ADAM_LR, ADAM_B1, ADAM_B2, ADAM_EPS, ADAM_WD, ADAM_STEP = 0.001, 0.9, 0.999, 1e-08, 0.01, 10
N_DEV = 8
LANE = 128
VMEM_LIMIT = 56 * 1024 * 1024
MM_FULL_K = 3072
MM_FULL_K_TN = 4096
MM_DEEP_K = 2816
MM_LN_ROWS, MM_LN_K = 512, 1408

_F32 = jnp.float32
_NN = (((1,), (0,)), ((), ()))
_NT = (((1,), (1,)), ((), ()))
_TN = (((0,), (0,)), ((), ()))
_S = jax.ShapeDtypeStruct


def _dot(a, b, dims=_NN):
    return lax.dot_general(a, b, dims, preferred_element_type=_F32)


def _tile(dim, pref):
    best = None
    d = LANE
    while d <= min(dim, pref):
        if dim % d == 0:
            best = d
        d += LANE
    return best if best is not None else dim


def _params(sem):
    return pltpu.CompilerParams(dimension_semantics=sem, vmem_limit_bytes=VMEM_LIMIT)


def _sigmoid(x):
    return 0.5 * jnp.tanh(0.5 * x) + 0.5


def _sigmoid_rel(x):
    return 1.0 / (1.0 + jnp.exp(-x))


_GELU_C = 0.7978845608028654
_GELU_A = 0.044715


def _gelu_and_grad(x):
    t = jnp.tanh(_GELU_C * (x + _GELU_A * x * x * x))
    y = 0.5 * x * (1.0 + t)
    dy = 0.5 * (1.0 + t) + 0.5 * x * (1.0 - t * t) * _GELU_C * (1.0 + 3.0 * _GELU_A * x * x)
    return y, dy


def _row_index(n):
    return lax.broadcasted_iota(jnp.int32, (n, 1), 0)


def _mm_tiles(mode, M, N, K, with_add):
    if mode == "tn":
        return _tile(M, 1024), _tile(N, 1024), _tile(K, MM_FULL_K_TN)
    if K <= MM_FULL_K:
        return _tile(M, 1024 if with_add else 2048), _tile(N, 1024 if mode == "nn" else 512), K
    return _tile(M, 1024), _tile(N, 1024), _tile(K, MM_DEEP_K)


def _mm(a, b, mode, out_dtype, name, *, a_parts=1, b_parts=1, out_parts=1, add=None, add_scale=1.0, deps=(), tiles=None,
        ln=None):
    if mode == "nn":
        M, K = a.shape
        N = b.shape[1]
    elif mode == "nt":
        if a_parts > 1:
            M, K = a.shape[1], a.shape[2] * a_parts
        else:
            M, K = a.shape
        N = b.shape[0]
    else:
        K, M = a.shape
        N = b.shape[-1] * b_parts
    tm, tn, tk = tiles if tiles is not None else _mm_tiles(mode, M, N // max(b_parts, out_parts), K // a_parts, add is not None)
    if ln is not None:
        tm, tn, tk = _tile(M, MM_LN_ROWS), N, (K if K <= MM_FULL_K else _tile(K, MM_LN_K))
    nk = K // tk
    npj = (N // max(b_parts, out_parts)) // tn
    nkp = (K // a_parts) // tk
    if mode == "nn":
        a_spec = pl.BlockSpec((tm, tk), lambda i, j, k: (i, k))
        b_spec = pl.BlockSpec((tk, tn), lambda i, j, k: (k, j))
        dims = _NN
    elif mode == "nt":
        if a_parts > 1:
            a_spec = pl.BlockSpec((None, tm, tk), lambda i, j, k: (k // nkp, i, k % nkp))
        else:
            a_spec = pl.BlockSpec((tm, tk), lambda i, j, k: (i, k))
        b_spec = pl.BlockSpec((tn, tk), lambda i, j, k: (j, k))
        dims = _NT
    else:
        a_spec = pl.BlockSpec((tk, tm), lambda i, j, k: (k, i))
        if b_parts > 1:
            b_spec = pl.BlockSpec((None, tk, tn), lambda i, j, k: (j // npj, k, j % npj))
        else:
            b_spec = pl.BlockSpec((tk, tn), lambda i, j, k: (k, j))
        dims = _TN
    if out_parts > 1:
        out_spec = pl.BlockSpec((None, tm, tn), lambda i, j, k: (j // npj, i, j % npj))
        out_shape = _S((out_parts, M, N // out_parts), out_dtype)
    else:
        out_spec = pl.BlockSpec((tm, tn), lambda i, j, k: (i, j))
        out_shape = _S((M, N), out_dtype)
    in_specs = [a_spec, b_spec]
    args = [a, b]
    if add is not None:
        in_specs.append(pl.BlockSpec((tm, tn), lambda i, j, k: (i, j)))
        args.append(add)
    if ln is not None:
        vec = pl.BlockSpec((1, N), lambda i, j, k: (0, 0))
        in_specs += [vec, vec]
        args += list(ln)
        out_spec = [out_spec] * 3
        out_shape = [_S((M, N), _F32), _S((M, N), _F32), _S((M, N), _MM)]
    n_ln = 2 + (add is not None)
    in_specs += [_ANY] * len(deps)
    args += list(deps)
    n_out = 1 if ln is None else 3

    def finish(r, refs, outs):
        if add is not None:
            r = r + add_scale * refs[2][...]
        if ln is None:
            outs[0][...] = r.astype(outs[0].dtype)
            return
        mu = jnp.mean(r, axis=-1, keepdims=True)
        rc = r - mu
        y = rc * lax.rsqrt(jnp.mean(rc * rc, axis=-1, keepdims=True) + LN_EPS) * refs[n_ln][...] + refs[n_ln + 1][...]
        outs[0][...] = r
        outs[1][...] = y
        outs[2][...] = y.astype(outs[2].dtype)

    def body_one(*refs):
        finish(_dot(refs[0][...], refs[1][...], dims), refs, refs[len(refs) - n_out:])

    def body_acc(*refs):
        acc = refs[-1]
        k = pl.program_id(2)

        @pl.when(k == 0)
        def _():
            acc[...] = jnp.zeros_like(acc)

        acc[...] += _dot(refs[0][...], refs[1][...], dims)

        @pl.when(k == nk - 1)
        def _():
            finish(acc[...], refs, refs[len(refs) - 1 - n_out:len(refs) - 1])

    return pl.pallas_call(
        body_one if nk == 1 else body_acc, name=name, grid=(M // tm, N // tn, nk), in_specs=in_specs,
        out_specs=out_spec, out_shape=out_shape,
        scratch_shapes=[] if nk == 1 else [pltpu.VMEM((tm, tn), _F32)],
        compiler_params=_params(("parallel", "parallel", "arbitrary")),
    )(*args)


def _cast(x2d, dtype, name, deps=()):
    R, C = x2d.shape
    tr = _tile(R, 512) if R % LANE == 0 else R

    def body(x_ref, *rest):
        rest[-1][...] = x_ref[...].astype(rest[-1].dtype)

    return pl.pallas_call(
        body, name=name, grid=(R // tr,), in_specs=[pl.BlockSpec((tr, C), lambda i: (i, 0))] + [_ANY] * len(deps),
        out_specs=pl.BlockSpec((tr, C), lambda i: (i, 0)), out_shape=_S((R, C), dtype),
        compiler_params=_params(("parallel",)),
    )(x2d, *deps)


def _ln_fwd(z, g, b, name):
    T, D = z.shape
    tr = _tile(T, 256)

    def body(z_ref, g_ref, b_ref, y_ref, yb_ref):
        zz = z_ref[...]
        mu = jnp.mean(zz, axis=-1, keepdims=True)
        zc = zz - mu
        var = jnp.mean(zc * zc, axis=-1, keepdims=True)
        y = zc * lax.rsqrt(var + LN_EPS) * g_ref[...] + b_ref[...]
        y_ref[...] = y
        yb_ref[...] = y.astype(yb_ref.dtype)

    row = pl.BlockSpec((tr, D), lambda i: (i, 0))
    vec = pl.BlockSpec((1, D), lambda i: (0, 0))
    return pl.pallas_call(
        body, name=name, grid=(T // tr,), in_specs=[row, vec, vec], out_specs=[row, row],
        out_shape=[_S((T, D), _F32), _S((T, D), _MM)], compiler_params=_params(("parallel",)),
    )(z, g, b)


def _ln_bwd(z, g, dy, name):
    T, D = z.shape
    tr = _tile(T, 256)

    def body(z_ref, g_ref, dy_ref, dz_ref, dzb_ref, dg_ref, db_ref):
        @pl.when(pl.program_id(0) == 0)
        def _():
            dg_ref[...] = jnp.zeros_like(dg_ref)
            db_ref[...] = jnp.zeros_like(db_ref)

        zz = z_ref[...]
        mu = jnp.mean(zz, axis=-1, keepdims=True)
        zc = zz - mu
        rstd = lax.rsqrt(jnp.mean(zc * zc, axis=-1, keepdims=True) + LN_EPS)
        xh = zc * rstd
        d = dy_ref[...]
        dg_ref[...] += jnp.sum(d * xh, axis=0, keepdims=True)
        db_ref[...] += jnp.sum(d, axis=0, keepdims=True)
        dxh = d * g_ref[...]
        dz = rstd * (dxh - jnp.mean(dxh, axis=-1, keepdims=True) - xh * jnp.mean(dxh * xh, axis=-1, keepdims=True))
        dz_ref[...] = dz
        dzb_ref[...] = dz.astype(dzb_ref.dtype)

    row = pl.BlockSpec((tr, D), lambda i: (i, 0))
    vec = pl.BlockSpec((1, D), lambda i: (0, 0))
    return pl.pallas_call(
        body, name=name, grid=(T // tr,), in_specs=[row, vec, row], out_specs=[row, row, vec, vec],
        out_shape=[_S((T, D), _F32), _S((T, D), _MM), _S((1, D), _F32), _S((1, D), _F32)],
        compiler_params=_params(("arbitrary",)),
    )(z, g, dy)


def _ln_loss_bwd(z, g, b, target, name):
    T, D = z.shape
    tr = _tile(T, 256)

    def body(z_ref, g_ref, b_ref, t_ref, loss_ref, dz_ref, dzb_ref, dg_ref, db_ref, lacc):
        i = pl.program_id(0)

        @pl.when(i == 0)
        def _():
            dg_ref[...] = jnp.zeros_like(dg_ref)
            db_ref[...] = jnp.zeros_like(db_ref)
            lacc[...] = jnp.zeros_like(lacc)

        zz = z_ref[...]
        mu = jnp.mean(zz, axis=-1, keepdims=True)
        zc = zz - mu
        rstd = lax.rsqrt(jnp.mean(zc * zc, axis=-1, keepdims=True) + LN_EPS)
        xh = zc * rstd
        err = xh * g_ref[...] + b_ref[...] - t_ref[...]
        lacc[...] += jnp.sum(err * err, axis=0, keepdims=True)
        d = err * (1.0 / D)
        dg_ref[...] += jnp.sum(d * xh, axis=0, keepdims=True)
        db_ref[...] += jnp.sum(d, axis=0, keepdims=True)
        dxh = d * g_ref[...]
        dz = rstd * (dxh - jnp.mean(dxh, axis=-1, keepdims=True) - xh * jnp.mean(dxh * xh, axis=-1, keepdims=True))
        dz_ref[...] = dz
        dzb_ref[...] = dz.astype(dzb_ref.dtype)

        @pl.when(i == pl.num_programs(0) - 1)
        def _():
            loss_ref[...] = jnp.sum(lacc[...], axis=-1, keepdims=True) * (0.5 / D)

    row = pl.BlockSpec((tr, D), lambda i: (i, 0))
    vec = pl.BlockSpec((1, D), lambda i: (0, 0))
    one = pl.BlockSpec((1, 1), lambda i: (0, 0))
    return pl.pallas_call(
        body, name=name, grid=(T // tr,), in_specs=[row, vec, vec, row], out_specs=[one, row, row, vec, vec],
        out_shape=[_S((1, 1), _F32), _S((T, D), _F32), _S((T, D), _MM), _S((1, D), _F32), _S((1, D), _F32)],
        scratch_shapes=[pltpu.VMEM((1, D), _F32)], compiler_params=_params(("arbitrary",)),
    )(z, g, b, target)


def _conv3(X, cw, cb):
    return cb + cw[2:3] * X + cw[1:2] * pltpu.roll(X, 1, 0) + cw[0:1] * pltpu.roll(X, 2, 0)


def _ffn_mid_fwd(h, cw, cb, name):
    _, T, F = h.shape
    tr = _tile(T, FFN_ROWS)
    tc = _tile(F, FFN_FWD_COLS)
    nb = tr // CONV_HALO

    rc = _tile(tr, FFN_CHUNK)
    lanes = [slice(cs * LANE, (cs + 1) * LANE) for cs in range(tc // LANE)]

    def body(h_ref, p_ref, cw_ref, cb_ref, o_ref, c_ref):
        i = pl.program_id(0)

        def work(r0, cols, X):
            hc = [_conv3(X[part], cw_ref[part, :, cols], cb_ref[part, :, cols])[CONV_HALO:] for part in range(2)]
            for part in range(2):
                c_ref[part, pl.ds(r0, rc), cols] = hc[part].astype(c_ref.dtype)
            a, v = hc
            o_ref[pl.ds(r0, rc), cols] = (a * _sigmoid(a) * v).astype(o_ref.dtype)

        for cols in lanes:
            work(0, cols, [jnp.concatenate([jnp.where(i == 0, 0.0, p_ref[part, :, cols]), h_ref[part, 0:rc, cols]], axis=0)
                           for part in range(2)])

        def chunk(c, carry):
            r0 = pl.multiple_of(c * rc, rc)
            for cols in lanes:
                work(r0, cols, [h_ref[part, pl.ds(r0 - CONV_HALO, rc + CONV_HALO), cols] for part in range(2)])
            return carry

        lax.fori_loop(1, tr // rc, chunk, 0)

    return pl.pallas_call(
        body, name=name, grid=(T // tr, F // tc),
        in_specs=[pl.BlockSpec((2, tr, tc), lambda i, j: (0, i, j)),
                  pl.BlockSpec((2, CONV_HALO, tc), lambda i, j: (0, jnp.maximum(i * nb - 1, 0), j)),
                  pl.BlockSpec((2, 3, tc), lambda i, j: (0, 0, j)),
                  pl.BlockSpec((2, 1, tc), lambda i, j: (0, 0, j))],
        out_specs=[pl.BlockSpec((tr, tc), lambda i, j: (i, j)), pl.BlockSpec((2, tr, tc), lambda i, j: (0, i, j))],
        out_shape=[_S((T, F), _MM), _S((2, T, F), _MM)],
        compiler_params=_params(("parallel", "parallel")),
    )(h, h, cw, cb)


def _ffn_mid_bwd(h, hc, dact, cw, name):
    _, T, F = h.shape
    tr = _tile(T, FFN_BWD_ROWS)
    tc = _tile(F, FFN_BWD_COLS)
    nb_c = tr // PACKED_ROWS
    rc = _tile(tr, FFN_CHUNK)
    n = rc + CONV_HALO

    def body(h_ref, c_ref, cn_ref, d_ref, dn_ref, cw_ref, dh_ref, dcw_ref, dcb_ref):
        i = pl.program_id(1)
        is_last = i == pl.num_programs(1) - 1

        @pl.when(i == 0)
        def _():
            dcw_ref[...] = jnp.zeros_like(dcw_ref)
            dcb_ref[...] = jnp.zeros_like(dcb_ref)

        def work(r0, cols, a, v, D):
            sg = _sigmoid(a)
            dhc = [D * v * sg * (1.0 + a * (1.0 - sg)), D * a * sg]
            for part in range(2):
                X = h_ref[part, pl.ds(r0, rc), cols]
                cwp = cw_ref[part, :, cols]
                dh = None
                for k in range(3):
                    g = (dhc[part] if k == 0 else pltpu.roll(dhc[part], n - k, 0))[0:rc]
                    term = cwp[2 - k:3 - k] * g
                    dh = term if dh is None else dh + term
                    dcw_ref[part, 2 - k:3 - k, cols] += jnp.sum(g * X, axis=0, keepdims=True)
                    if k == 0:
                        dcb_ref[part, :, cols] += jnp.sum(g, axis=0, keepdims=True)
                dh_ref[part, pl.ds(r0, rc), cols] = dh.astype(dh_ref.dtype)

        lanes = [slice(cs * LANE, (cs + 1) * LANE) for cs in range(tc // LANE)]

        def chunk(c, carry):
            r0 = pl.multiple_of(c * rc, rc)
            for cols in lanes:
                a, v = [c_ref[part, pl.ds(r0, rc + PACKED_ROWS), cols].astype(_F32)[0:n] for part in range(2)]
                work(r0, cols, a, v, d_ref[pl.ds(r0, rc + PACKED_ROWS), cols].astype(_F32)[0:n])
            return carry

        lax.fori_loop(0, tr // rc - 1, chunk, 0)
        r0 = tr - rc
        for cols in lanes:
            a, v = [jnp.concatenate([c_ref[part, r0:tr, cols].astype(_F32), cn_ref[part, :, cols].astype(_F32)[0:CONV_HALO]],
                                    axis=0) for part in range(2)]
            D = jnp.concatenate([d_ref[r0:tr, cols].astype(_F32),
                                 jnp.where(is_last, 0.0, dn_ref[:, cols].astype(_F32)[0:CONV_HALO])], axis=0)
            work(r0, cols, a, v, D)

    return pl.pallas_call(
        body, name=name, grid=(F // tc, T // tr),
        in_specs=[pl.BlockSpec((2, tr, tc), lambda j, i: (0, i, j)),
                  pl.BlockSpec((2, tr, tc), lambda j, i: (0, i, j)),
                  pl.BlockSpec((2, PACKED_ROWS, tc), lambda j, i: (0, jnp.minimum((i + 1) * nb_c, T // PACKED_ROWS - 1), j)),
                  pl.BlockSpec((tr, tc), lambda j, i: (i, j)),
                  pl.BlockSpec((PACKED_ROWS, tc), lambda j, i: (jnp.minimum((i + 1) * nb_c, T // PACKED_ROWS - 1), j)),
                  pl.BlockSpec((2, 3, tc), lambda j, i: (0, 0, j))],
        out_specs=[pl.BlockSpec((2, tr, tc), lambda j, i: (0, i, j)),
                   pl.BlockSpec((2, 3, tc), lambda j, i: (0, 0, j)),
                   pl.BlockSpec((2, 1, tc), lambda j, i: (0, 0, j))],
        out_shape=[_S((2, T, F), _MM), _S((2, 3, F), _F32), _S((2, 1, F), _F32)],
        compiler_params=_params(("parallel", "arbitrary")),
    )(h, hc, hc, dact, dact, cw)


def _ev_common(h_ref, hp_ref, lng_ref, lnb_ref, ws_ref, bias_ref, i, tr, W):
    H = W // A_HEAD
    u, gu = _gelu_and_grad(h_ref[0])
    v, gv = _gelu_and_grad(h_ref[1])
    mu = jnp.mean(v, axis=-1, keepdims=True)
    vc = v - mu
    rstd = lax.rsqrt(jnp.mean(vc * vc, axis=-1, keepdims=True) + LN_EPS)
    vhat = vc * rstd
    vb = (vhat * lng_ref[...] + lnb_ref[...]).astype(_MM)
    s_chunks = []
    for c in range(tr // A_CHUNK):
        r0 = c * A_CHUNK
        heads = [_dot(ws_ref[hd], vb[r0:r0 + A_CHUNK, hd * A_HEAD:(hd + 1) * A_HEAD]) for hd in range(H)]
        s_chunks.append(jnp.concatenate(heads, axis=1) + bias_ref[...])
    prev = jnp.where(i == 0, 0.0, hp_ref[...])
    X = jnp.concatenate([prev, h_ref[2]], axis=0)
    return u, gu, gv, rstd, vhat, vb, s_chunks, X


def _pool_inv_count(i, tr, rows, win):
    pos = i * tr + _row_index(rows) + 1
    return 1.0 / jnp.minimum(pos, win).astype(_F32)


def _pool_fwd(X, g, Wg, i, tr):
    xg = X[:, g * Wg:(g + 1) * Wg]
    s = xg
    for k in range(g + 1):
        s = s + pltpu.roll(s, 2 ** k, 0)
    return s[POOL_HALO:] * _pool_inv_count(i, tr, tr, 2 ** (g + 1)) - xg[POOL_HALO:]


def _ev_mid_fwd(h, lng, lnb, ws, bias, wp, sc, name):
    _, T, W = h.shape
    tr = _tile(T, 256)
    H = W // A_HEAD
    Wg = W // B_GROUPS
    nb = tr // POOL_HALO

    def body(h_ref, hp_ref, lng_ref, lnb_ref, ws_ref, bias_ref, wp_ref, sc_ref, o_ref):
        i = pl.program_id(0)
        u, _, _, _, _, _, s_chunks, X = _ev_common(h_ref, hp_ref, lng_ref, lnb_ref, ws_ref, bias_ref, i, tr, W)
        for c, s in enumerate(s_chunks):
            r0 = c * A_CHUNK
            o_ref[r0:r0 + A_CHUNK, 0:W] = (u[r0:r0 + A_CHUNK] * s).astype(o_ref.dtype)
        for g in range(B_GROUPS):
            p = _pool_fwd(X, g, Wg, i, tr)
            y = _dot(p.astype(_MM), wp_ref[g]) * sc_ref[:, g * Wg:(g + 1) * Wg]
            o_ref[:, W + g * Wg:W + (g + 1) * Wg] = y.astype(o_ref.dtype)

    vec = pl.BlockSpec((1, W), lambda i: (0, 0))
    return pl.pallas_call(
        body, name=name, grid=(T // tr,),
        in_specs=[pl.BlockSpec((3, tr, W), lambda i: (0, i, 0)),
                  pl.BlockSpec((None, POOL_HALO, W), lambda i: (2, jnp.maximum(i * nb - 1, 0), 0)),
                  vec, vec,
                  pl.BlockSpec((H, A_CHUNK, A_CHUNK), lambda i: (0, 0, 0)),
                  pl.BlockSpec((A_CHUNK, W), lambda i: (0, 0)),
                  pl.BlockSpec((B_GROUPS, Wg, Wg), lambda i: (0, 0, 0)),
                  vec],
        out_specs=pl.BlockSpec((tr, 2 * W), lambda i: (i, 0)), out_shape=_S((T, 2 * W), _MM),
        compiler_params=_params(("parallel",)),
    )(h, h, lng, lnb, ws, bias, wp, sc)


def _ev_mid_bwd(h, dy, lng, lnb, ws, bias, wp, sc, name):
    _, T, W = h.shape
    tr = _tile(T, 256)
    H = W // A_HEAD
    Wg = W // B_GROUPS
    nb = tr // POOL_HALO
    last_blk = T // POOL_HALO - 1
    n = tr + POOL_HALO

    def body(h_ref, hp_ref, dy_ref, dyn_ref, lng_ref, lnb_ref, ws_ref, bias_ref, wp_ref, sc_ref,
             dh_ref, dws_ref, dbias_ref, dlng_ref, dlnb_ref, dwp_ref, dsc_ref):
        i = pl.program_id(0)

        @pl.when(i == 0)
        def _():
            for r in (dws_ref, dbias_ref, dlng_ref, dlnb_ref, dwp_ref, dsc_ref):
                r[...] = jnp.zeros_like(r)

        u, gu, gv, rstd, vhat, vb, s_chunks, X = _ev_common(h_ref, hp_ref, lng_ref, lnb_ref, ws_ref, bias_ref, i, tr, W)
        rr = lax.broadcasted_iota(jnp.int32, (A_CHUNK, A_CHUNK), 0)
        cc = lax.broadcasted_iota(jnp.int32, (A_CHUNK, A_CHUNK), 1)
        tril = rr >= cc
        du_chunks, dvln_chunks = [], []
        for c, s in enumerate(s_chunks):
            r0 = c * A_CHUNK
            dya = dy_ref[r0:r0 + A_CHUNK, 0:W]
            du_chunks.append(dya * s)
            ds = dya * u[r0:r0 + A_CHUNK]
            dbias_ref[...] += ds
            dsb = ds.astype(_MM)
            heads = []
            for hd in range(H):
                cols = slice(hd * A_HEAD, (hd + 1) * A_HEAD)
                dws_ref[hd] += jnp.where(tril, _dot(dsb[:, cols], vb[r0:r0 + A_CHUNK, cols], _NT), 0.0)
                heads.append(_dot(ws_ref[hd], dsb[:, cols], _TN))
            dvln_chunks.append(jnp.concatenate(heads, axis=1))
        du = jnp.concatenate(du_chunks, axis=0)
        dvln = jnp.concatenate(dvln_chunks, axis=0)
        dlng_ref[...] += jnp.sum(dvln * vhat, axis=0, keepdims=True)
        dlnb_ref[...] += jnp.sum(dvln, axis=0, keepdims=True)
        dxh = dvln * lng_ref[...]
        dv = rstd * (dxh - jnp.mean(dxh, axis=-1, keepdims=True) - vhat * jnp.mean(dxh * vhat, axis=-1, keepdims=True))
        dh_ref[:, 0:W] = (du * gu).astype(dh_ref.dtype)
        dh_ref[:, W:2 * W] = (dv * gv).astype(dh_ref.dtype)

        dyb = dy_ref[:, W:2 * W]
        dyb_full = jnp.concatenate([dyb, jnp.where(i == pl.num_programs(0) - 1, 0.0, dyn_ref[...])], axis=0)
        for g in range(B_GROUPS):
            cols = slice(g * Wg, (g + 1) * Wg)
            pb = _pool_fwd(X, g, Wg, i, tr).astype(_MM)
            ypre = _dot(pb, wp_ref[g])
            dsc_ref[:, cols] += jnp.sum(dyb[:, cols] * ypre, axis=0, keepdims=True)
            dyp = (dyb_full[:, cols] * sc_ref[:, cols]).astype(_MM)
            dwp_ref[g] += _dot(pb, dyp[0:tr], _TN)
            dp = _dot(dyp, wp_ref[g], _NT)
            s = dp * _pool_inv_count(i, tr, n, 2 ** (g + 1))
            for k in range(g + 1):
                s = s + pltpu.roll(s, n - 2 ** k, 0)
            dh_ref[:, 2 * W + g * Wg:2 * W + (g + 1) * Wg] = (s[0:tr] - dp[0:tr]).astype(dh_ref.dtype)

    vec = pl.BlockSpec((1, W), lambda i: (0, 0))
    ws_spec = pl.BlockSpec((H, A_CHUNK, A_CHUNK), lambda i: (0, 0, 0))
    bias_spec = pl.BlockSpec((A_CHUNK, W), lambda i: (0, 0))
    wp_spec = pl.BlockSpec((B_GROUPS, Wg, Wg), lambda i: (0, 0, 0))
    return pl.pallas_call(
        body, name=name, grid=(T // tr,),
        in_specs=[pl.BlockSpec((3, tr, W), lambda i: (0, i, 0)),
                  pl.BlockSpec((None, POOL_HALO, W), lambda i: (2, jnp.maximum(i * nb - 1, 0), 0)),
                  pl.BlockSpec((tr, 2 * W), lambda i: (i, 0)),
                  pl.BlockSpec((POOL_HALO, W), lambda i: (jnp.minimum((i + 1) * nb, last_blk), 1)),
                  vec, vec, ws_spec, bias_spec, wp_spec, vec],
        out_specs=[pl.BlockSpec((tr, 3 * W), lambda i: (i, 0)), ws_spec, bias_spec, vec, vec, wp_spec, vec],
        out_shape=[_S((T, 3 * W), _MM), _S((H, A_CHUNK, A_CHUNK), _F32), _S((A_CHUNK, W), _F32), _S((1, W), _F32),
                   _S((1, W), _F32), _S((B_GROUPS, Wg, Wg), _F32), _S((1, W), _F32)],
        compiler_params=_params(("arbitrary",)),
    )(h, h, dy, dy, lng, lnb, ws, bias, wp, sc)


def _chunk_cumsum(x, rin):
    s = 1
    while s < C_CHUNK:
        x = x + jnp.where(rin >= s, pltpu.roll(x, s, 0), 0.0)
        s *= 2
    return x


def _chunk_revcumsum(x, rin):
    n = x.shape[0]
    s = 1
    while s < C_CHUNK:
        x = x + jnp.where(rin + s < C_CHUNK, pltpu.roll(x, n - s, 0), 0.0)
        s *= 2
    return x


def _hgrn_gates(q, fl, lb, tr, tc):
    nch = tr // C_CHUNK
    sq = _sigmoid(q)
    sf = _sigmoid_rel(fl)
    f = lb + (1.0 - lb) * sf
    logf = jnp.log(f)
    rin = _row_index(tr) % C_CHUNK
    b = _chunk_cumsum(logf, rin)
    tot3 = jnp.sum(logf.reshape(nch, C_CHUNK, tc), axis=1, keepdims=True)
    eb = jnp.exp(b)
    enb = jnp.exp(-b)
    ekb = jnp.exp(tot3 - b.reshape(nch, C_CHUNK, tc)).reshape(tr, tc)
    return sq, sf, f, rin, tot3, eb, enb, ekb


def _hgrn_prep_fwd(h, lb, name):
    _, T, D = h.shape
    tr = _tile(T, 512)
    tc = _tile(D, 512)
    nch = tr // C_CHUNK

    def body(q_ref, f_ref, v_ref, lb_ref, qd_ref, kd_ref, ke_ref, vb_ref, dec_ref):
        q = q_ref[...]
        sq, _, f, _, tot3, eb, enb, ekb = _hgrn_gates(q, f_ref[...], lb_ref[...], tr, tc)
        kk = 1.0 - f
        qd_ref[...] = (q * sq * eb).astype(qd_ref.dtype)
        kd_ref[...] = (kk * enb).astype(kd_ref.dtype)
        ke_ref[...] = (kk * ekb).astype(ke_ref.dtype)
        vb_ref[...] = v_ref[...].astype(vb_ref.dtype)
        dec_ref[...] = jnp.exp(tot3).reshape(nch, tc)

    def part(p):
        return pl.BlockSpec((None, tr, tc), lambda i, j: (p, i, j))

    blk = pl.BlockSpec((tr, tc), lambda i, j: (i, j))
    return pl.pallas_call(
        body, name=name, grid=(T // tr, D // tc),
        in_specs=[part(0), part(1), part(2), pl.BlockSpec((1, tc), lambda i, j: (0, j))],
        out_specs=[blk, blk, blk, blk, pl.BlockSpec((nch, tc), lambda i, j: (i, j))],
        out_shape=[_S((T, D), _MM)] * 4 + [_S((T // C_CHUNK, D), _F32)],
        compiler_params=_params(("parallel", "parallel")),
    )(h, h, h, lb)


def _tril_mask():
    rr = lax.broadcasted_iota(jnp.int32, (C_CHUNK, C_CHUNK), 0)
    cc = lax.broadcasted_iota(jnp.int32, (C_CHUNK, C_CHUNK), 1)
    return rr >= cc


def _hgrn_scan_fwd(qd, kd, ke, vb, dec, h, ng, name):
    T, D = qd.shape
    NH = D // C_HEAD
    N = T // C_CHUNK

    def body(qd_ref, kd_ref, ke_ref, vb_ref, dec_ref, g_ref, ng_ref, o_ref, y_ref, st_ref):
        mask = _tril_mask()

        per_trip = math.gcd(N, SCAN_UNROLL)

        def trip(i, St):
            ahead = []
            for u in range(per_trip):
                n = i * per_trip + u
                r = pl.ds(pl.multiple_of(n * C_CHUNK, C_CHUNK), C_CHUNK)
                Qd, Kd, Ke, V = qd_ref[r, :], kd_ref[r, :], ke_ref[r, :], vb_ref[r, :]
                att = jnp.where(mask, _dot(Qd, Kd, _NT), 0.0).astype(_MM)
                ahead.append((n, r, _dot(att, V), _dot(V, Ke, _TN)))
            for n, r, o_intra, update in ahead:
                o_ref[r, :] = o_intra + _dot(qd_ref[r, :], St.astype(_MM), _NT)
                st_ref[n] = St
                St = St * dec_ref[pl.ds(n, 1), :] + update
            return St

        lax.fori_loop(0, N // per_trip, trip, jnp.zeros((C_HEAD, C_HEAD), _F32))
        o = o_ref[...]
        r = lax.rsqrt(jnp.mean(o * o, axis=-1, keepdims=True) + LN_EPS)
        y_ref[...] = (o * r * ng_ref[...] * _sigmoid(g_ref[...])).astype(y_ref.dtype)

    col = pl.BlockSpec((T, C_HEAD), lambda j: (0, j))
    return pl.pallas_call(
        body, name=name, grid=(NH,),
        in_specs=[col, col, col, col, pl.BlockSpec((N, C_HEAD), lambda j: (0, j)),
                  pl.BlockSpec((None, T, C_HEAD), lambda j: (3, 0, j)), pl.BlockSpec((1, C_HEAD), lambda j: (0, j))],
        out_specs=[col, col, pl.BlockSpec((None, N, C_HEAD, C_HEAD), lambda j: (j, 0, 0, 0))],
        out_shape=[_S((T, D), _F32), _S((T, D), _MM), _S((NH, N, C_HEAD, C_HEAD), _F32)],
        compiler_params=_params(("parallel",)),
    )(qd, kd, ke, vb, dec, h, ng)


def _hgrn_scan_bwd(qd, kd, ke, vb, dec, st, o, h, ng, dy, name):
    T, D = qd.shape
    NH = D // C_HEAD
    N = T // C_CHUNK

    def body(qd_ref, kd_ref, ke_ref, vb_ref, dec_ref, st_ref, o_ref, g_ref, ng_ref, dy_ref,
             dqd_ref, dkd_ref, dke_ref, dv_ref, dgate_ref, ddec_ref, dng_ref, do_s):
        o = o_ref[...]
        r = lax.rsqrt(jnp.mean(o * o, axis=-1, keepdims=True) + LN_EPS)
        oh = o * r
        gn = ng_ref[...]
        sg = _sigmoid(g_ref[...])
        d = dy_ref[...]
        dyn = d * sg
        dgate_ref[...] = (d * oh * gn * sg * (1.0 - sg)).astype(dgate_ref.dtype)
        dng_ref[...] = jnp.sum(dyn * oh, axis=0, keepdims=True)
        doh = dyn * gn
        do_s[...] = (r * (doh - oh * jnp.mean(doh * oh, axis=-1, keepdims=True))).astype(do_s.dtype)
        mask = _tril_mask()

        per_trip = math.gcd(N, SCAN_UNROLL)

        def trip(i, dSt):
            ahead = []
            for u in range(per_trip):
                n = N - 1 - (i * per_trip + u)
                rws = pl.ds(pl.multiple_of(n * C_CHUNK, C_CHUNK), C_CHUNK)
                Qd, Kd, V, dO = qd_ref[rws, :], kd_ref[rws, :], vb_ref[rws, :], do_s[rws, :]
                att = jnp.where(mask, _dot(Qd, Kd, _NT), 0.0).astype(_MM)
                dA = jnp.where(mask, _dot(dO, V, _NT), 0.0).astype(_MM)
                dqd_ref[rws, :] = _dot(dA, Kd) + _dot(dO, st_ref[n].astype(_MM))
                dkd_ref[rws, :] = _dot(dA, Qd, _TN)
                ahead.append((n, rws, _dot(att, dO, _TN), _dot(dO, Qd, _TN)))
            for n, rws, dv_intra, d_state in ahead:
                dStb = dSt.astype(_MM)
                dv_ref[rws, :] = (dv_intra + _dot(ke_ref[rws, :], dStb, _NT)).astype(dv_ref.dtype)
                dke_ref[rws, :] = _dot(vb_ref[rws, :], dStb)
                ddec_ref[pl.ds(n, 1), :] = jnp.sum(dSt * st_ref[n], axis=0, keepdims=True)
                dSt = dSt * dec_ref[pl.ds(n, 1), :] + d_state
            return dSt

        lax.fori_loop(0, N // per_trip, trip, jnp.zeros((C_HEAD, C_HEAD), _F32))

    col = pl.BlockSpec((T, C_HEAD), lambda j: (0, j))
    chk = pl.BlockSpec((N, C_HEAD), lambda j: (0, j))
    one = pl.BlockSpec((1, C_HEAD), lambda j: (0, j))
    return pl.pallas_call(
        body, name=name, grid=(NH,),
        in_specs=[col, col, col, col, chk, pl.BlockSpec((None, N, C_HEAD, C_HEAD), lambda j: (j, 0, 0, 0)), col,
                  pl.BlockSpec((None, T, C_HEAD), lambda j: (3, 0, j)), one, col],
        out_specs=[col, col, col, col, col, chk, one],
        out_shape=[_S((T, D), _F32)] * 3 + [_S((T, D), _MM)] * 2 + [_S((N, D), _F32), _S((1, D), _F32)],
        scratch_shapes=[pltpu.VMEM((T, C_HEAD), _MM)],
        compiler_params=_params(("parallel",)),
    )(qd, kd, ke, vb, dec, st, o, h, ng, dy)


def _hgrn_prep_bwd(h, lb, dqd, dkd, dke, dv, dgate, ddec, name):
    _, T, D = h.shape
    tr = _tile(T, 512)
    tc = _tile(D, 256)
    nch = tr // C_CHUNK

    def body(q_ref, f_ref, lb_ref, dqd_ref, dkd_ref, dke_ref, dv_ref, dgate_ref, ddec_ref, dh_ref, dlb_ref):
        @pl.when(pl.program_id(1) == 0)
        def _():
            dlb_ref[...] = jnp.zeros_like(dlb_ref)

        q = q_ref[...]
        lb = lb_ref[...]
        sq, sf, f, rin, tot3, eb, enb, ekb = _hgrn_gates(q, f_ref[...], lb, tr, tc)
        kk = 1.0 - f
        dQd, dKd, dKe = dqd_ref[...], dkd_ref[...], dke_ref[...]
        tq = dQd * eb
        tkd = dKd * enb
        tke = dKe * ekb
        ke_term = tke * kk
        db = tq * (q * sq) - tkd * kk - ke_term
        dtot3 = (jnp.sum(ke_term.reshape(nch, C_CHUNK, tc), axis=1, keepdims=True)
                 + (ddec_ref[...] * jnp.exp(tot3).reshape(nch, tc)).reshape(nch, 1, tc))
        dlogf = (_chunk_revcumsum(db, rin).reshape(nch, C_CHUNK, tc) + dtot3).reshape(tr, tc)
        df = dlogf / f - (tkd + tke)
        dh_ref[0] = (tq * sq * (1.0 + q * (1.0 - sq))).astype(dh_ref.dtype)
        dh_ref[1] = (df * (1.0 - lb) * sf * (1.0 - sf)).astype(dh_ref.dtype)
        dh_ref[2] = dv_ref[...]
        dh_ref[3] = dgate_ref[...]
        dlb_ref[...] += jnp.sum(df * (1.0 - sf), axis=0, keepdims=True)

    def part(p):
        return pl.BlockSpec((None, tr, tc), lambda j, i: (p, i, j))

    blk = pl.BlockSpec((tr, tc), lambda j, i: (i, j))
    vec = pl.BlockSpec((1, tc), lambda j, i: (0, j))
    return pl.pallas_call(
        body, name=name, grid=(D // tc, T // tr),
        in_specs=[part(0), part(1), vec, blk, blk, blk, blk, blk, pl.BlockSpec((nch, tc), lambda j, i: (i, j))],
        out_specs=[pl.BlockSpec((4, tr, tc), lambda j, i: (0, i, j)), vec],
        out_shape=[_S((4, T, D), _MM), _S((1, D), _F32)],
        compiler_params=_params(("parallel", "arbitrary")),
    )(h, h, lb, dqd, dkd, dke, dv, dgate, ddec)


def _sum_in_device_order(me1, own, land, name):
    R, C = own.shape
    tr = _tile(R, 256)

    def body(me_ref, own_ref, land_ref, o_ref):
        me = me_ref[0]
        g = None
        for j in range(N_DEV):
            slot = jnp.maximum(jnp.bitwise_xor(me, j) - 1, 0)
            p = jnp.where(me == j, own_ref[...], land_ref[slot])
            g = p if g is None else g + p
        o_ref[...] = g

    return pl.pallas_call(
        body, name=name,
        grid_spec=pltpu.PrefetchScalarGridSpec(
            num_scalar_prefetch=1, grid=(R // tr,),
            in_specs=[pl.BlockSpec((tr, C), lambda i, me: (i, 0)), pl.BlockSpec((N_DEV - 1, tr, C), lambda i, me: (0, i, 0))],
            out_specs=pl.BlockSpec((tr, C), lambda i, me: (i, 0))),
        out_shape=_S((R, C), _F32), compiler_params=_params(("parallel",)),
    )(me1, own, land)


def _adamw(parts, w, m, v, name):
    P, R, C = parts.shape
    tr = _tile(R, 128) if R % LANE == 0 else R

    def body(p_ref, w_ref, m_ref, v_ref, g_ref, d_ref, nm_ref, nv_ref):
        g = p_ref[0].astype(_F32)
        for s in range(1, P):
            g = g + p_ref[s].astype(_F32)
        nm = ADAM_B1 * m_ref[...] + (1.0 - ADAM_B1) * g
        nv = ADAM_B2 * v_ref[...] + (1.0 - ADAM_B2) * (g * g)
        m_hat = nm / (1.0 - ADAM_B1 ** ADAM_STEP)
        v_hat = nv / (1.0 - ADAM_B2 ** ADAM_STEP)
        g_ref[...] = g
        d_ref[...] = -ADAM_LR * (m_hat / (jnp.sqrt(v_hat) + ADAM_EPS) + ADAM_WD * w_ref[...])
        nm_ref[...] = nm
        nv_ref[...] = nv

    blk = pl.BlockSpec((tr, C), lambda i: (i, 0))
    return pl.pallas_call(
        body, name=name, grid=(R // tr,), in_specs=[pl.BlockSpec((P, tr, C), lambda i: (0, i, 0)), blk, blk, blk],
        out_specs=[blk] * 4, out_shape=[_S((R, C), _F32)] * 4, compiler_params=_params(("parallel",)),
    )(parts, w, m, v)


def _exchange(name, srcs, out_shapes, jobs, deps=()):
    ns, nj = len(srcs), len(jobs)

    nd = len(deps)

    def body(*refs):
        ins, outs = refs[:ns], refs[ns + nd:ns + nd + len(out_shapes)]
        send_sems, recv_sems, local_sems = refs[-3:]
        x, y, c = lax.axis_index("x"), lax.axis_index("y"), lax.axis_index("c")
        me = 4 * x + 2 * y + c
        local = []
        for ji, (si, src_fn, di, dst_fn) in enumerate(jobs):
            cp = pltpu.make_async_copy(src_fn(ins[si], me, me), dst_fn(outs[di], me), local_sems.at[ji])
            cp.start()
            local.append(cp)
        remote = []
        for k in range(1, N_DEV):
            px, py, pc = (x + (k >> 2)) % 2, (y + ((k >> 1) & 1)) % 2, (c + (k & 1)) % 2
            to = 4 * px + 2 * py + pc
            for ji, (si, src_fn, di, dst_fn) in enumerate(jobs):
                sem = (k - 1) * nj + ji
                cp = pltpu.make_async_remote_copy(
                    src_ref=src_fn(ins[si], me, to), dst_ref=dst_fn(outs[di], me),
                    send_sem=send_sems.at[sem], recv_sem=recv_sems.at[sem],
                    device_id=(px, py, pc), device_id_type=pl.DeviceIdType.MESH)
                cp.start()
                remote.append(cp)
        for cp in remote:
            cp.wait_recv()
        for cp in remote:
            cp.wait_send()
        for cp in local:
            cp.wait()

    hbm = pl.BlockSpec(memory_space=pltpu.HBM)
    return pl.pallas_call(
        body, name=name, in_specs=[hbm] * ns + [_ANY] * nd, out_specs=[hbm] * len(out_shapes), out_shape=list(out_shapes),
        scratch_shapes=[pltpu.SemaphoreType.DMA(((N_DEV - 1) * nj,)), pltpu.SemaphoreType.DMA(((N_DEV - 1) * nj,)),
                        pltpu.SemaphoreType.DMA((nj,))],
    )(*srcs, *deps)


def _whole(ref, me, to):
    return ref


def _slot_job(i, o):
    def dst(ref, me):
        return ref.at[me]
    return (i, _whole, o, dst)


_HBM = pl.BlockSpec(memory_space=pltpu.HBM)
_SEM = pl.BlockSpec(memory_space=pltpu.SEMAPHORE)
_ANY = pl.BlockSpec(memory_space=pl.ANY)
_N_PEER = N_DEV - 1


def _split_params():
    return pltpu.CompilerParams(has_side_effects=pltpu.SideEffectType.DATAFLOW_SIDE_EFFECTING)


def _blk(ref, axis, n, idx):
    if axis is None:
        return ref
    return ref.at[tuple([slice(None)] * axis + [pl.ds(pl.multiple_of(idx * n, n), n)])]


def _peer(k):
    x, y, c = lax.axis_index("x"), lax.axis_index("y"), lax.axis_index("c")
    px, py, pc = (x + (k >> 2)) % 2, (y + ((k >> 1) & 1)) % 2, (c + (k & 1)) % 2
    return (px, py, pc), 4 * px + 2 * py + pc, 4 * x + 2 * y + c


def _row_tile(rows, pref):
    best = None
    for d in range(16, min(rows, pref) + 1, 16):
        if rows % d == 0:
            best = d
    return best if best is not None else rows


def _place(w, me1, axis, name, layer=None, deps=()):
    R, C = w.shape[-2:]
    tr = _row_tile(R, 512)
    nb = R // tr
    lead = () if layer is None else (None,)
    pre = () if layer is None else (layer,)

    def body(me_ref, w_ref, *rest):
        rest[-1][...] = w_ref[...].astype(rest[-1].dtype)

    if axis == 1:
        out_spec = pl.BlockSpec((tr, C), lambda i, me: (i, me[0]))
        out_shape = _S((R, N_DEV * C), _MM)
    else:
        out_spec = pl.BlockSpec((tr, C), lambda i, me: (me[0] * nb + i, 0))
        out_shape = _S((N_DEV * R, C), _MM)
    return pl.pallas_call(
        body, name=name,
        grid_spec=pltpu.PrefetchScalarGridSpec(
            num_scalar_prefetch=1, grid=(nb,),
            in_specs=[pl.BlockSpec(lead + (tr, C), lambda i, me: pre + (i, 0))] + [_ANY] * len(deps), out_specs=out_spec),
        out_shape=out_shape, compiler_params=_params(("parallel",)),
    )(me1, w, *deps)


_SIBLING = 1
_CHIPS = (2, 4, 6)
_VMEM_TOKEN = pl.BlockSpec(memory_space=pltpu.VMEM)


def _remote(ref_blk, send_sem, recv_sem, dev):
    return pltpu.make_async_remote_copy(src_ref=ref_blk, dst_ref=ref_blk, send_sem=send_sem, recv_sem=recv_sem,
                                        device_id=dev, device_id_type=pl.DeviceIdType.MESH)


def _gather_start(name, full, axis, n):
    def body(f_ref, send, recv, f_out, token):
        for i, k in enumerate((_SIBLING,) + _CHIPS):
            dev, _, me = _peer(k)
            _remote(_blk(f_ref, axis, n, me), send.at[i], recv.at[i], dev).start()
        token[...] = jnp.zeros_like(token)

    return pl.pallas_call(
        body, name=name,
        out_shape=(pltpu.SemaphoreType.DMA((4,)), pltpu.SemaphoreType.DMA((4,)), pltpu.HBM(full.shape, full.dtype),
                   _S((8, LANE), _F32)),
        in_specs=(_HBM,), out_specs=(_SEM, _SEM, _HBM, _VMEM_TOKEN),
        input_output_aliases={0: 2}, compiler_params=_split_params(),
    )(pltpu.with_memory_space_constraint(full, pltpu.HBM))


def _gather_forward(name, full, axis, n, recv, after):
    after = tuple(after) if isinstance(after, (tuple, list)) else (after,)

    def body(f_ref, recv_r, *rest):
        send2, recv2, f_out, token = rest[-4:]
        sib, _, _ = _peer(_SIBLING)
        for i, k in enumerate(_CHIPS):
            dev, frm, _ = _peer(k)
            blk = _blk(f_ref, axis, n, frm)
            _remote(blk, send2.at[i], recv_r.at[1 + i], dev).wait_recv()
            _remote(blk, send2.at[i], recv2.at[i], sib).start()
        token[...] = jnp.zeros_like(token)

    return pl.pallas_call(
        body, name=name,
        out_shape=(pltpu.SemaphoreType.DMA((3,)), pltpu.SemaphoreType.DMA((3,)), pltpu.HBM(full.shape, full.dtype),
                   _S((8, LANE), _F32)),
        in_specs=(_HBM, _SEM) + (_ANY,) * len(after), out_specs=(_SEM, _SEM, _HBM, _VMEM_TOKEN),
        input_output_aliases={0: 2}, compiler_params=_split_params(),
    )(full, recv, *after)


def _gather_wait(name, full, axis, n, send, recv, send2, recv2, after):
    def body(f_ref, send_r, recv_r, send2_r, recv2_r, after_ref, f_out):
        sib, _, me = _peer(_SIBLING)
        blk = _blk(f_ref, axis, n, me)
        for i in range(4):
            _remote(blk, send_r.at[i], recv_r.at[0], sib).wait_send()
        _remote(blk, send_r.at[0], recv_r.at[0], sib).wait_recv()
        for i in range(3):
            cp = _remote(blk, send2_r.at[i], recv2_r.at[i], sib)
            cp.wait_send()
            cp.wait_recv()

    return pl.pallas_call(
        body, name=name, out_shape=pltpu.HBM(full.shape, full.dtype),
        in_specs=(_HBM, _SEM, _SEM, _SEM, _SEM, _ANY), out_specs=_HBM,
        input_output_aliases={0: 0}, compiler_params=_split_params(),
    )(full, send, recv, send2, recv2, after)


def _scatter_start(name, dw, axis, n):
    shard = tuple(n if a == axis else d for a, d in enumerate(dw.shape))
    land = lax.empty((_N_PEER,) + shard, dw.dtype)

    def body(dw_ref, land_ref, send, recv, dw_out, land_out, token):
        for k in range(1, N_DEV):
            dev, to, _ = _peer(k)
            pltpu.make_async_remote_copy(
                src_ref=_blk(dw_ref, axis, n, to), dst_ref=land_ref.at[k - 1], send_sem=send.at[k - 1],
                recv_sem=recv.at[k - 1], device_id=dev, device_id_type=pl.DeviceIdType.MESH).start()
        token[...] = jnp.zeros_like(token)

    return pl.pallas_call(
        body, name=name,
        out_shape=(pltpu.SemaphoreType.DMA((_N_PEER,)), pltpu.SemaphoreType.DMA((_N_PEER,)),
                   pltpu.HBM(dw.shape, dw.dtype), pltpu.HBM(land.shape, land.dtype), _S((8, LANE), _F32)),
        in_specs=(_HBM, _HBM), out_specs=(_SEM, _SEM, _HBM, _HBM, pl.BlockSpec(memory_space=pltpu.VMEM)),
        input_output_aliases={0: 2, 1: 3}, compiler_params=_split_params(),
    )(pltpu.with_memory_space_constraint(dw, pltpu.HBM), pltpu.with_memory_space_constraint(land, pltpu.HBM))


def _scatter_wait(name, items, after):
    ne = len(items)
    after = tuple(after) if isinstance(after, (tuple, list)) else (after,)

    def body(*refs):
        for e, (_, _, _, _, axis, n) in enumerate(items):
            dw_ref, land_ref, send_r, recv_r = refs[4 * e:4 * e + 4]
            for k in range(1, N_DEV):
                dev, to, _ = _peer(k)
                cp = pltpu.make_async_remote_copy(
                    src_ref=_blk(dw_ref, axis, n, to), dst_ref=land_ref.at[k - 1], send_sem=send_r.at[k - 1],
                    recv_sem=recv_r.at[k - 1], device_id=dev, device_id_type=pl.DeviceIdType.MESH)
                cp.wait_send()
                cp.wait_recv()

    args, out_shape = [], []
    for dw, land, send, recv, _, _ in items:
        args += [dw, land, send, recv]
        out_shape += [pltpu.HBM(dw.shape, dw.dtype), pltpu.HBM(land.shape, land.dtype)]
    res = pl.pallas_call(
        body, name=name, out_shape=tuple(out_shape),
        in_specs=(_HBM, _HBM, _SEM, _SEM) * ne + (_ANY,) * len(after), out_specs=(_HBM,) * (2 * ne),
        input_output_aliases={4 * e + j: 2 * e + j for e in range(ne) for j in range(2)},
        compiler_params=_split_params(),
    )(*args, *after)
    return [(res[2 * e], res[2 * e + 1]) for e in range(ne)]


def _adamw_big(me1, dw, land, w, m, v, axis, n, name, layer=None, into=None):
    R, C = land.shape[1:]
    tr = _row_tile(R, 128)
    nb = R // tr
    lead = () if layer is None else (None,)
    pre = () if layer is None else (layer,)

    def body(me_ref, own_ref, land_ref, w_ref, m_ref, v_ref, *rest):
        g_ref, d_ref, nm_ref, nv_ref = rest[-4:]
        g = own_ref[...].astype(_F32)
        for s in range(_N_PEER):
            g = g + land_ref[s].astype(_F32)
        nm = ADAM_B1 * m_ref[...] + (1.0 - ADAM_B1) * g
        nv = ADAM_B2 * v_ref[...] + (1.0 - ADAM_B2) * (g * g)
        m_hat = nm / (1.0 - ADAM_B1 ** ADAM_STEP)
        v_hat = nv / (1.0 - ADAM_B2 ** ADAM_STEP)
        g_ref[...] = g
        d_ref[...] = -ADAM_LR * (m_hat / (jnp.sqrt(v_hat) + ADAM_EPS) + ADAM_WD * w_ref[...])
        nm_ref[...] = nm
        nv_ref[...] = nv

    if axis == 1:
        own_spec = pl.BlockSpec((tr, C), lambda i, me: (i, me[0]))
    else:
        own_spec = pl.BlockSpec((tr, C), lambda i, me: (me[0] * nb + i, 0))
    blk = pl.BlockSpec(lead + (tr, C), lambda i, me: pre + (i, 0))
    in_specs = [own_spec, pl.BlockSpec((_N_PEER, tr, C), lambda i, me: (0, i, 0)), blk, blk, blk]
    args = [me1, dw, land, w, m, v]
    aliases = {}
    if into is not None:
        in_specs += [_ANY] * 4
        aliases = {6 + j: j for j in range(4)}
        args += list(into)
    return pl.pallas_call(
        body, name=name,
        grid_spec=pltpu.PrefetchScalarGridSpec(num_scalar_prefetch=1, grid=(nb,), in_specs=in_specs, out_specs=[blk] * 4),
        out_shape=[_S(w.shape, _F32)] * 4, input_output_aliases=aliases, compiler_params=_params(("parallel",)),
    )(*args)


def _pack(arrs):
    flat = jnp.concatenate([a.reshape(-1).astype(_F32) for a in arrs])
    pad = (-flat.shape[0]) % (LANE * LANE)
    return jnp.pad(flat, (0, pad)).reshape(-1, LANE)


def _unpack(mat, shapes):
    flat = mat.reshape(-1)
    out, off = [], 0
    for s in shapes:
        n = 1
        for d in s:
            n *= d
        out.append(flat[off:off + n].reshape(s))
        off += n
    return out


def _lb_of(lb_param):
    lb_all = jnp.cumsum(jax.nn.softmax(lb_param.astype(_F32), axis=0), axis=0)
    return (lb_all - lb_all[0])[1:2]


def kernel(x, ev_w_in, ev_ln_v_g, ev_ln_v_b, ev_w_s, ev_b_s, ev_w_pool, ev_pool_scale, ev_w_out, od_w_in, od_norm_g, od_w_out, lb_param, ffn_w_up, ffn_conv_w, ffn_conv_b, ffn_w_down, ln1_g, ln1_b, ln2_g, ln2_b, loss_target, m_ev_w_in, m_ev_ln_v_g, m_ev_ln_v_b, m_ev_w_s, m_ev_b_s, m_ev_w_pool, m_ev_pool_scale, m_ev_w_out, m_od_w_in, m_od_norm_g, m_od_w_out, m_lb_param, m_ffn_w_up, m_ffn_conv_w, m_ffn_conv_b, m_ffn_w_down, m_ln1_g, m_ln1_b, m_ln2_g, m_ln2_b, v_ev_w_in, v_ev_ln_v_g, v_ev_ln_v_b, v_ev_w_s, v_ev_b_s, v_ev_w_pool, v_ev_pool_scale, v_ev_w_out, v_od_w_in, v_od_norm_g, v_od_w_out, v_lb_param, v_ffn_w_up, v_ffn_conv_w, v_ffn_conv_b, v_ffn_w_down, v_ln1_g, v_ln1_b, v_ln2_g, v_ln2_b):
    me = 4 * lax.axis_index("x") + 2 * lax.axis_index("y") + lax.axis_index("c")
    T, D = x.shape[1], x.shape[2]
    W = ev_ln_v_g.shape[1]
    H = W // A_HEAD
    Wg = W // B_GROUPS
    F2 = ffn_conv_b.shape[1]
    F = F2 // 2
    n_in0, n_out0 = ev_w_in.shape[2], ev_w_out.shape[1]
    n_in1, n_out1 = od_w_in.shape[2], od_w_out.shape[1]
    n_up, n_dn = ffn_w_up.shape[2], ffn_w_down.shape[1]
    n_pool, n_ng, n_cw = ev_w_pool.shape[2], od_norm_g.shape[1], ffn_conv_w.shape[2]

    small_shards = [od_norm_g, ffn_conv_w, ev_w_pool]
    small_pack = _pack(small_shards)
    small_all = _exchange("gather_small_params", [small_pack], [_S((N_DEV,) + small_pack.shape, _F32)], [_slot_job(0, 0)])[0]

    me1 = me.astype(jnp.int32).reshape(1)
    weights = [
        ("w_in0", ev_w_in[0], None, 1, n_in0), ("w_out0", ev_w_out[0], None, 0, n_out0),
        ("w_up0", ffn_w_up, 0, 1, n_up), ("w_dn0", ffn_w_down, 0, 0, n_dn),
        ("w_in1", od_w_in[0], None, 1, n_in1), ("w_out1", od_w_out[0], None, 0, n_out1),
        ("w_up1", ffn_w_up, 1, 1, n_up), ("w_dn1", ffn_w_down, 1, 0, n_dn),
    ]
    started, tokens = {}, [small_all]
    for key, w, layer, axis, n in weights:
        full = _place(w, me1, axis, "place_" + key, layer, deps=tokens)
        send, recv, full, token = _gather_start("gather_start_" + key, full, axis, n)
        started[key] = (full, axis, n, send, recv)
        tokens = [token]

    def pass_on(key, after):
        full, axis, n, send, recv = started[key]
        send2, recv2, full, token = _gather_forward("gather_forward_" + key, full, axis, n, recv, after)
        started[key] = (full, axis, n, send, recv, send2, recv2)
        return token

    def gathered(key, after):
        return _gather_wait("gather_wait_" + key, *started[key], after)

    ng_parts, cw_parts, wp_parts = [], [], []
    for j in range(N_DEV):
        a, b, c = _unpack(small_all[j], [s.shape for s in small_shards])
        ng_parts.append(a)
        cw_parts.append(b)
        wp_parts.append(c)
    norm_g = jnp.concatenate(ng_parts, axis=1)
    conv_w = jnp.concatenate(cw_parts, axis=2)
    w_pool = jnp.concatenate(wp_parts, axis=2)[0]
    cw_l = [conv_w[l].reshape(3, 2, F).transpose(1, 0, 2) for l in range(DEPTH)]
    cb_l = [ffn_conv_b[l].reshape(2, 1, F) for l in range(DEPTH)]
    ws_tril = jnp.tril(ev_w_s[0]).astype(_MM)
    bias = jnp.repeat(ev_b_s[0].T, A_HEAD, axis=1)
    wp_b = w_pool.astype(_MM)
    lb, lb_vjp = jax.vjp(_lb_of, lb_param)

    small_names = ["ev_ln_v_g", "ev_ln_v_b", "ev_w_s", "ev_b_s", "ev_w_pool", "ev_pool_scale", "od_norm_g", "lb_param",
                   "ffn_conv_w", "ffn_conv_b", "ln1_g", "ln1_b", "ln2_g", "ln2_b"]
    given = dict(ev_ln_v_g=(ev_ln_v_g, m_ev_ln_v_g, v_ev_ln_v_g), ev_ln_v_b=(ev_ln_v_b, m_ev_ln_v_b, v_ev_ln_v_b),
                 ev_w_s=(ev_w_s, m_ev_w_s, v_ev_w_s), ev_b_s=(ev_b_s, m_ev_b_s, v_ev_b_s),
                 ev_w_pool=(ev_w_pool, m_ev_w_pool, v_ev_w_pool),
                 ev_pool_scale=(ev_pool_scale, m_ev_pool_scale, v_ev_pool_scale),
                 od_norm_g=(od_norm_g, m_od_norm_g, v_od_norm_g), lb_param=(lb_param, m_lb_param, v_lb_param),
                 ffn_conv_w=(ffn_conv_w, m_ffn_conv_w, v_ffn_conv_w), ffn_conv_b=(ffn_conv_b, m_ffn_conv_b, v_ffn_conv_b),
                 ln1_g=(ln1_g, m_ln1_g, v_ln1_g), ln1_b=(ln1_b, m_ln1_b, v_ln1_b), ln2_g=(ln2_g, m_ln2_g, v_ln2_g),
                 ln2_b=(ln2_b, m_ln2_b, v_ln2_b))
    shard_axis = dict(ev_w_pool=2, od_norm_g=1, ffn_conv_w=2)
    rep_names = [n for n in small_names if n not in shard_axis]
    shd_names = [n for n in small_names if n in shard_axis]
    small_packs = [_pack([given[n][j] for n in small_names]) for j in range(3)]

    x2 = x[0]
    xb = _cast(x2, _MM, "cast_x", deps=[pass_on("w_in0", tokens[0])])
    w_in0 = gathered("w_in0", xb)
    h0 = _mm(xb, w_in0, "nn", _F32, "ev_in", out_parts=3)
    tie = pass_on("w_out0", h0)
    yab = _ev_mid_fwd(h0, ev_ln_v_g + tie[0, 0], ev_ln_v_b, ws_tril, bias, wp_b, ev_pool_scale, "ev_mid_fwd")
    w_out0 = gathered("w_out0", yab)
    z1 = _mm(yab, w_out0, "nn", _F32, "ev_out", add=x2, add_scale=ALPHA)
    tie = pass_on("w_up0", (z1, *small_packs))
    x1, x1b = _ln_fwd(z1, ln1_g[0:1] + tie[0, 0], ln1_b[0:1], "ln1_0")
    w_up0 = gathered("w_up0", x1b)
    hf0 = _mm(x1b, w_up0, "nn", _F32, "ffn_up", out_parts=2)
    tie = pass_on("w_dn0", hf0)
    act0, hc0 = _ffn_mid_fwd(hf0, cw_l[0], cb_l[0] + tie[0, 0], "ffn_mid_fwd")
    w_dn0 = gathered("w_dn0", act0)
    z2 = _mm(act0, w_dn0, "nn", _F32, "ffn_down", add=x1, add_scale=ALPHA)
    tie = pass_on("w_in1", z2)
    x2_, x2b = _ln_fwd(z2, ln2_g[0:1] + tie[0, 0], ln2_b[0:1], "ln2_0")
    w_in1 = gathered("w_in1", x2b)
    h1 = _mm(x2b, w_in1, "nn", _F32, "od_in", out_parts=4)
    qd, kd, ke, vb, dec = _hgrn_prep_fwd(h1, lb, "hgrn_prep_fwd")
    tie = pass_on("w_out1", qd)
    o, yo, st = _hgrn_scan_fwd(qd, kd, ke, vb, dec, h1, norm_g + tie[0, 0], "hgrn_scan_fwd")
    w_out1 = gathered("w_out1", yo)
    tie = pass_on("w_up1", yo)
    z3, x3, x3b = _mm(yo, w_out1, "nn", _F32, "od_out", add=x2_, add_scale=ALPHA, ln=(ln1_g[1:2], ln1_b[1:2]), deps=[tie])
    w_up1 = gathered("w_up1", x3b)
    hf1 = _mm(x3b, w_up1, "nn", _F32, "ffn_up", out_parts=2)
    tie = pass_on("w_dn1", hf1)
    act1, hc1 = _ffn_mid_fwd(hf1, cw_l[1], cb_l[1] + tie[0, 0], "ffn_mid_fwd")
    w_dn1 = gathered("w_dn1", act1)
    z4 = _mm(act1, w_dn1, "nn", _F32, "ffn_down", add=x3, add_scale=ALPHA)

    scat = {}

    def scatter(key, dw, axis, n):
        send, recv, dw, land, token = _scatter_start("scatter_start_" + key, dw, axis, n)
        scat[key] = (dw, land, send, recv, axis, n)
        return [token]

    loss11, dz4, dz4b, g_ln2_1, b_ln2_1 = _ln_loss_bwd(z4, ln2_g[1:2], ln2_b[1:2], loss_target[0], "ln_loss_bwd")
    tok = scatter("dn1", _mm(act1, dz4b, "tn", _XCH, "ffn_down_dw"), 0, n_dn)
    dact1 = _mm(dz4b, w_dn1, "nt", _MM, "ffn_down_dx", deps=tok)
    dhf1, dcw1, dcb1 = _ffn_mid_bwd(hf1, hc1, dact1, cw_l[1], "ffn_mid_bwd")
    tok = scatter("up1", _mm(x3b, dhf1, "tn", _XCH, "ffn_up_dw", b_parts=2, deps=tok), 1, n_up)
    dx3 = _mm(dhf1, w_up1, "nt", _F32, "ffn_up_dx", a_parts=2, add=dz4, add_scale=ALPHA, deps=tok)
    dz3, dz3b, g_ln1_1, b_ln1_1 = _ln_bwd(z3, ln1_g[1:2], dx3, "ln_bwd")
    tok = scatter("out1", _mm(yo, dz3b, "tn", _XCH, "od_out_dw", deps=tok), 0, n_out1)
    dyo = _mm(dz3b, w_out1, "nt", _F32, "od_out_dx", deps=tok)
    dqd, dkd, dke, dv, dgate, ddec, dng = _hgrn_scan_bwd(qd, kd, ke, vb, dec, st, o, h1, norm_g, dyo, "hgrn_scan_bwd")
    dh1, dlb = _hgrn_prep_bwd(h1, lb, dqd, dkd, dke, dv, dgate, ddec, "hgrn_prep_bwd")
    tok = scatter("in1", _mm(x2b, dh1, "tn", _XCH, "od_in_dw", b_parts=4, deps=tok), 1, n_in1)
    dx2 = _mm(dh1, w_in1, "nt", _F32, "od_in_dx", a_parts=4, add=dz3, add_scale=ALPHA, deps=tok)
    dz2, dz2b, g_ln2_0, b_ln2_0 = _ln_bwd(z2, ln2_g[0:1], dx2, "ln_bwd")
    tok = scatter("dn0", _mm(act0, dz2b, "tn", _XCH, "ffn_down_dw", deps=tok), 0, n_dn)
    dact0 = _mm(dz2b, w_dn0, "nt", _MM, "ffn_down_dx", deps=tok)
    dhf0, dcw0, dcb0 = _ffn_mid_bwd(hf0, hc0, dact0, cw_l[0], "ffn_mid_bwd")
    tok = scatter("up0", _mm(x1b, dhf0, "tn", _XCH, "ffn_up_dw", b_parts=2, deps=tok), 1, n_up)
    dx1 = _mm(dhf0, w_up0, "nt", _F32, "ffn_up_dx", a_parts=2, add=dz2, add_scale=ALPHA, deps=tok)
    dz1, dz1b, g_ln1_0, b_ln1_0 = _ln_bwd(z1, ln1_g[0:1], dx1, "ln_bwd")
    tok = scatter("out0", _mm(yab, dz1b, "tn", _XCH, "ev_out_dw", deps=tok), 0, n_out0)
    dyab = _mm(dz1b, w_out0, "nt", _F32, "ev_out_dx", deps=tok)
    dh0, dws, dbias, dlng, dlnb, dwp, dsc = _ev_mid_bwd(h0, dyab, ev_ln_v_g, ev_ln_v_b, ws_tril, bias, wp_b,
                                                        ev_pool_scale, "ev_mid_bwd")

    g_b_s = dbias.reshape(A_CHUNK, H, A_HEAD).sum(axis=-1).T[None]
    g_conv_w = jnp.stack([d.transpose(1, 0, 2).reshape(3, F2) for d in (dcw0, dcw1)])
    g_conv_b = jnp.stack([d.reshape(F2) for d in (dcb0, dcb1)])
    small_grads = dict(zip(small_names, [
        dlng, dlnb, dws[None], g_b_s, dwp[None], dsc, dng, lb_vjp(dlb)[0], g_conv_w, g_conv_b,
        jnp.concatenate([g_ln1_0, g_ln1_1]), jnp.concatenate([b_ln1_0, b_ln1_1]),
        jnp.concatenate([g_ln2_0, g_ln2_1]), jnp.concatenate([b_ln2_0, b_ln2_1])]))

    def by_device(g, ax):
        g = g.reshape(g.shape[:ax] + (N_DEV, g.shape[ax] // N_DEV) + g.shape[ax + 1:])
        return jnp.moveaxis(g, ax, 0).reshape(N_DEV, -1)

    shd = jnp.concatenate([by_device(small_grads[n], shard_axis[n]) for n in shd_names], axis=1)
    shd_pack = jnp.pad(shd, ((0, 0), (0, (-shd.shape[1]) % (LANE * LANE)))).reshape(-1, LANE)
    shd_rows = shd_pack.shape[0] // N_DEV
    rep_pack = _pack([small_grads[n] for n in rep_names])
    tok = scatter("in0", _mm(xb, dh0, "tn", _XCH, "ev_in_dw", deps=tok), 1, n_in0)
    tok = scatter("small_rep", rep_pack + tok[0][0, 0], None, None)
    tok = scatter("small_shd", shd_pack + tok[0][0, 0], 0, shd_rows)
    grad_x = _mm(dh0, w_in0, "nt", _F32, "ev_in_dx", add=dz1, add_scale=ALPHA, deps=tok)

    def landed(name, keys, after):
        got = _scatter_wait(name, [scat[k] for k in keys], after)
        return {k: (me1, dw, land) for k, (dw, land) in zip(keys, got)}

    early = landed("scatter_wait_early", ["dn1", "up1", "out1", "in1", "dn0", "up0", "out0"], grad_x)
    big = {}
    r_dn = _adamw_big(*early["dn1"], ffn_w_down, m_ffn_w_down, v_ffn_w_down, 0, n_dn, "adamw_w_dn1", layer=1)
    r_up = _adamw_big(*early["up1"], ffn_w_up, m_ffn_w_up, v_ffn_w_up, 1, n_up, "adamw_w_up1", layer=1)
    big["od_w_out"] = _adamw_big(*early["out1"], od_w_out[0], m_od_w_out[0], v_od_w_out[0], 0, n_out1, "adamw_w_out1")
    big["od_w_in"] = _adamw_big(*early["in1"], od_w_in[0], m_od_w_in[0], v_od_w_in[0], 1, n_in1, "adamw_w_in1")
    big["ffn_w_down"] = _adamw_big(*early["dn0"], ffn_w_down, m_ffn_w_down, v_ffn_w_down, 0, n_dn, "adamw_w_dn0", layer=0, into=r_dn)
    big["ffn_w_up"] = _adamw_big(*early["up0"], ffn_w_up, m_ffn_w_up, v_ffn_w_up, 1, n_up, "adamw_w_up0", layer=0, into=r_up)
    big["ev_w_out"] = _adamw_big(*early["out0"], ev_w_out[0], m_ev_w_out[0], v_ev_w_out[0], 0, n_out0, "adamw_w_out0")
    late = landed("scatter_wait_late", ["in0", "small_rep", "small_shd"],
                  (big["ffn_w_down"][0], big["ffn_w_up"][0], big["od_w_in"][0], big["ev_w_out"][0]))
    big["ev_w_in"] = _adamw_big(*late["in0"], ev_w_in[0], m_ev_w_in[0], v_ev_w_in[0], 1, n_in0, "adamw_w_in0")

    rep_mat = _sum_in_device_order(*late["small_rep"], "sum_small_rep")
    local_g = dict(zip(rep_names, _unpack(rep_mat, [small_grads[n].shape for n in rep_names])))
    _, shd_all, shd_land = late["small_shd"]
    shd_own = lax.dynamic_slice_in_dim(shd_all, me * shd_rows, shd_rows, axis=0)
    shd_mat = _sum_in_device_order(me1, shd_own, shd_land, "sum_small_shd")
    local_g.update(zip(shd_names, _unpack(shd_mat, [given[n][0].shape for n in shd_names])))
    local_shapes = [given[n][0].shape for n in small_names]
    res = _adamw(_pack([local_g[n] for n in small_names])[None], *small_packs, "adamw_small")
    small = {n: [] for n in small_names}
    for r in res:
        for n, a in zip(small_names, _unpack(r, local_shapes)):
            small[n].append(a)

    loss = lax.psum(loss11[0, 0], ("x", "y", "c"))
    order = ["ev_w_in", "ev_ln_v_g", "ev_ln_v_b", "ev_w_s", "ev_b_s", "ev_w_pool", "ev_pool_scale", "ev_w_out", "od_w_in",
             "od_norm_g", "od_w_out", "lb_param", "ffn_w_up", "ffn_conv_w", "ffn_conv_b", "ffn_w_down", "ln1_g", "ln1_b",
             "ln2_g", "ln2_b"]
    shapes = dict(ev_w_in=ev_w_in.shape, ev_w_out=ev_w_out.shape, od_w_in=od_w_in.shape, od_w_out=od_w_out.shape,
                  ffn_w_up=ffn_w_up.shape, ffn_w_down=ffn_w_down.shape)
    outs = [loss, grad_x[None]]
    for kind in range(4):
        for n in order:
            outs.append(big[n][kind].reshape(shapes[n]) if n in big else small[n][kind])
    return tuple(outs)
```

```python
import functools
import math

import jax
import jax.numpy as jnp
from jax import lax
from jax.experimental import pallas as pl
from jax.experimental.pallas import tpu as pltpu

_MM = jnp.bfloat16
_XCH = jnp.bfloat16

DEPTH = 2
ALPHA = (2 * DEPTH) ** 0.25
LN_EPS = 1e-5
A_CHUNK = 128
A_HEAD = 128
B_GROUPS = 4
POOL_HALO = 16
C_CHUNK = 64
C_HEAD = 128
SCAN_UNROLL = 64
CONV_HALO = 8
PACKED_ROWS = 16
FFN_ROWS, FFN_FWD_COLS = 512, 1408
FFN_BWD_ROWS, FFN_BWD_COLS = 512, 512
FFN_CHUNK = 128
ADAM_LR, ADAM_B1, ADAM_B2, ADAM_EPS, ADAM_WD, ADAM_STEP = 0.001, 0.9, 0.999, 1e-08, 0.01, 10
N_DEV = 8
LANE = 128
VMEM_LIMIT = 56 * 1024 * 1024
MM_FULL_K = 3072
MM_FULL_K_TN = 4096
MM_DEEP_K = 2816
MM_LN_ROWS, MM_LN_K = 512, 1408

_F32 = jnp.float32
_NN = (((1,), (0,)), ((), ()))
_NT = (((1,), (1,)), ((), ()))
_TN = (((0,), (0,)), ((), ()))
_S = jax.ShapeDtypeStruct


def _dot(a, b, dims=_NN):
    return lax.dot_general(a, b, dims, preferred_element_type=_F32)


def _tile(dim, pref):
    best = None
    d = LANE
    while d <= min(dim, pref):
        if dim % d == 0:
            best = d
        d += LANE
    return best if best is not None else dim


def _params(sem):
    return pltpu.CompilerParams(dimension_semantics=sem, vmem_limit_bytes=VMEM_LIMIT)


def _sigmoid(x):
    return 0.5 * jnp.tanh(0.5 * x) + 0.5


def _sigmoid_rel(x):
    return 1.0 / (1.0 + jnp.exp(-x))


_GELU_C = 0.7978845608028654
_GELU_A = 0.044715


def _gelu_and_grad(x):
    t = jnp.tanh(_GELU_C * (x + _GELU_A * x * x * x))
    y = 0.5 * x * (1.0 + t)
    dy = 0.5 * (1.0 + t) + 0.5 * x * (1.0 - t * t) * _GELU_C * (1.0 + 3.0 * _GELU_A * x * x)
    return y, dy


def _row_index(n):
    return lax.broadcasted_iota(jnp.int32, (n, 1), 0)


def _mm_tiles(mode, M, N, K, with_add):
    if mode == "tn":
        return _tile(M, 1024), _tile(N, 1024), _tile(K, MM_FULL_K_TN)
    if K <= MM_FULL_K:
        return _tile(M, 1024 if with_add else 2048), _tile(N, 1024 if mode == "nn" else 512), K
    return _tile(M, 1024), _tile(N, 1024), _tile(K, MM_DEEP_K)


def _mm(a, b, mode, out_dtype, name, *, a_parts=1, b_parts=1, out_parts=1, add=None, add_scale=1.0, deps=(), tiles=None,
        ln=None):
    if mode == "nn":
        M, K = a.shape
        N = b.shape[1]
    elif mode == "nt":
        if a_parts > 1:
            M, K = a.shape[1], a.shape[2] * a_parts
        else:
            M, K = a.shape
        N = b.shape[0]
    else:
        K, M = a.shape
        N = b.shape[-1] * b_parts
    tm, tn, tk = tiles if tiles is not None else _mm_tiles(mode, M, N // max(b_parts, out_parts), K // a_parts, add is not None)
    if ln is not None:
        tm, tn, tk = _tile(M, MM_LN_ROWS), N, (K if K <= MM_FULL_K else _tile(K, MM_LN_K))
    nk = K // tk
    npj = (N // max(b_parts, out_parts)) // tn
    nkp = (K // a_parts) // tk
    if mode == "nn":
        a_spec = pl.BlockSpec((tm, tk), lambda i, j, k: (i, k))
        b_spec = pl.BlockSpec((tk, tn), lambda i, j, k: (k, j))
        dims = _NN
    elif mode == "nt":
        if a_parts > 1:
            a_spec = pl.BlockSpec((None, tm, tk), lambda i, j, k: (k // nkp, i, k % nkp))
        else:
            a_spec = pl.BlockSpec((tm, tk), lambda i, j, k: (i, k))
        b_spec = pl.BlockSpec((tn, tk), lambda i, j, k: (j, k))
        dims = _NT
    else:
        a_spec = pl.BlockSpec((tk, tm), lambda i, j, k: (k, i))
        if b_parts > 1:
            b_spec = pl.BlockSpec((None, tk, tn), lambda i, j, k: (j // npj, k, j % npj))
        else:
            b_spec = pl.BlockSpec((tk, tn), lambda i, j, k: (k, j))
        dims = _TN
    if out_parts > 1:
        out_spec = pl.BlockSpec((None, tm, tn), lambda i, j, k: (j // npj, i, j % npj))
        out_shape = _S((out_parts, M, N // out_parts), out_dtype)
    else:
        out_spec = pl.BlockSpec((tm, tn), lambda i, j, k: (i, j))
        out_shape = _S((M, N), out_dtype)
    in_specs = [a_spec, b_spec]
    args = [a, b]
    if add is not None:
        in_specs.append(pl.BlockSpec((tm, tn), lambda i, j, k: (i, j)))
        args.append(add)
    if ln is not None:
        vec = pl.BlockSpec((1, N), lambda i, j, k: (0, 0))
        in_specs += [vec, vec]
        args += list(ln)
        out_spec = [out_spec] * 3
        out_shape = [_S((M, N), _F32), _S((M, N), _F32), _S((M, N), _MM)]
    n_ln = 2 + (add is not None)
    in_specs += [_ANY] * len(deps)
    args += list(deps)
    n_out = 1 if ln is None else 3

    def finish(r, refs, outs):
        if add is not None:
            r = r + add_scale * refs[2][...]
        if ln is None:
            outs[0][...] = r.astype(outs[0].dtype)
            return
        mu = jnp.mean(r, axis=-1, keepdims=True)
        rc = r - mu
        y = rc * lax.rsqrt(jnp.mean(rc * rc, axis=-1, keepdims=True) + LN_EPS) * refs[n_ln][...] + refs[n_ln + 1][...]
        outs[0][...] = r
        outs[1][...] = y
        outs[2][...] = y.astype(outs[2].dtype)

    def body_one(*refs):
        finish(_dot(refs[0][...], refs[1][...], dims), refs, refs[len(refs) - n_out:])

    def body_acc(*refs):
        acc = refs[-1]
        k = pl.program_id(2)

        @pl.when(k == 0)
        def _():
            acc[...] = jnp.zeros_like(acc)

        acc[...] += _dot(refs[0][...], refs[1][...], dims)

        @pl.when(k == nk - 1)
        def _():
            finish(acc[...], refs, refs[len(refs) - 1 - n_out:len(refs) - 1])

    return pl.pallas_call(
        body_one if nk == 1 else body_acc, name=name, grid=(M // tm, N // tn, nk), in_specs=in_specs,
        out_specs=out_spec, out_shape=out_shape,
        scratch_shapes=[] if nk == 1 else [pltpu.VMEM((tm, tn), _F32)],
        compiler_params=_params(("parallel", "parallel", "arbitrary")),
    )(*args)


def _cast(x2d, dtype, name, deps=()):
    R, C = x2d.shape
    tr = _tile(R, 512) if R % LANE == 0 else R

    def body(x_ref, *rest):
        rest[-1][...] = x_ref[...].astype(rest[-1].dtype)

    return pl.pallas_call(
        body, name=name, grid=(R // tr,), in_specs=[pl.BlockSpec((tr, C), lambda i: (i, 0))] + [_ANY] * len(deps),
        out_specs=pl.BlockSpec((tr, C), lambda i: (i, 0)), out_shape=_S((R, C), dtype),
        compiler_params=_params(("parallel",)),
    )(x2d, *deps)


def _ln_fwd(z, g, b, name):
    T, D = z.shape
    tr = _tile(T, 256)

    def body(z_ref, g_ref, b_ref, y_ref, yb_ref):
        zz = z_ref[...]
        mu = jnp.mean(zz, axis=-1, keepdims=True)
        zc = zz - mu
        var = jnp.mean(zc * zc, axis=-1, keepdims=True)
        y = zc * lax.rsqrt(var + LN_EPS) * g_ref[...] + b_ref[...]
        y_ref[...] = y
        yb_ref[...] = y.astype(yb_ref.dtype)

    row = pl.BlockSpec((tr, D), lambda i: (i, 0))
    vec = pl.BlockSpec((1, D), lambda i: (0, 0))
    return pl.pallas_call(
        body, name=name, grid=(T // tr,), in_specs=[row, vec, vec], out_specs=[row, row],
        out_shape=[_S((T, D), _F32), _S((T, D), _MM)], compiler_params=_params(("parallel",)),
    )(z, g, b)


def _ln_bwd(z, g, dy, name):
    T, D = z.shape
    tr = _tile(T, 256)

    def body(z_ref, g_ref, dy_ref, dz_ref, dzb_ref, dg_ref, db_ref):
        @pl.when(pl.program_id(0) == 0)
        def _():
            dg_ref[...] = jnp.zeros_like(dg_ref)
            db_ref[...] = jnp.zeros_like(db_ref)

        zz = z_ref[...]
        mu = jnp.mean(zz, axis=-1, keepdims=True)
        zc = zz - mu
        rstd = lax.rsqrt(jnp.mean(zc * zc, axis=-1, keepdims=True) + LN_EPS)
        xh = zc * rstd
        d = dy_ref[...]
        dg_ref[...] += jnp.sum(d * xh, axis=0, keepdims=True)
        db_ref[...] += jnp.sum(d, axis=0, keepdims=True)
        dxh = d * g_ref[...]
        dz = rstd * (dxh - jnp.mean(dxh, axis=-1, keepdims=True) - xh * jnp.mean(dxh * xh, axis=-1, keepdims=True))
        dz_ref[...] = dz
        dzb_ref[...] = dz.astype(dzb_ref.dtype)

    row = pl.BlockSpec((tr, D), lambda i: (i, 0))
    vec = pl.BlockSpec((1, D), lambda i: (0, 0))
    return pl.pallas_call(
        body, name=name, grid=(T // tr,), in_specs=[row, vec, row], out_specs=[row, row, vec, vec],
        out_shape=[_S((T, D), _F32), _S((T, D), _MM), _S((1, D), _F32), _S((1, D), _F32)],
        compiler_params=_params(("arbitrary",)),
    )(z, g, dy)


def _ln_loss_bwd(z, g, b, target, name):
    T, D = z.shape
    tr = _tile(T, 256)

    def body(z_ref, g_ref, b_ref, t_ref, loss_ref, dz_ref, dzb_ref, dg_ref, db_ref, lacc):
        i = pl.program_id(0)

        @pl.when(i == 0)
        def _():
            dg_ref[...] = jnp.zeros_like(dg_ref)
            db_ref[...] = jnp.zeros_like(db_ref)
            lacc[...] = jnp.zeros_like(lacc)

        zz = z_ref[...]
        mu = jnp.mean(zz, axis=-1, keepdims=True)
        zc = zz - mu
        rstd = lax.rsqrt(jnp.mean(zc * zc, axis=-1, keepdims=True) + LN_EPS)
        xh = zc * rstd
        err = xh * g_ref[...] + b_ref[...] - t_ref[...]
        lacc[...] += jnp.sum(err * err, axis=0, keepdims=True)
        d = err * (1.0 / D)
        dg_ref[...] += jnp.sum(d * xh, axis=0, keepdims=True)
        db_ref[...] += jnp.sum(d, axis=0, keepdims=True)
        dxh = d * g_ref[...]
        dz = rstd * (dxh - jnp.mean(dxh, axis=-1, keepdims=True) - xh * jnp.mean(dxh * xh, axis=-1, keepdims=True))
        dz_ref[...] = dz
        dzb_ref[...] = dz.astype(dzb_ref.dtype)

        @pl.when(i == pl.num_programs(0) - 1)
        def _():
            loss_ref[...] = jnp.sum(lacc[...], axis=-1, keepdims=True) * (0.5 / D)

    row = pl.BlockSpec((tr, D), lambda i: (i, 0))
    vec = pl.BlockSpec((1, D), lambda i: (0, 0))
    one = pl.BlockSpec((1, 1), lambda i: (0, 0))
    return pl.pallas_call(
        body, name=name, grid=(T // tr,), in_specs=[row, vec, vec, row], out_specs=[one, row, row, vec, vec],
        out_shape=[_S((1, 1), _F32), _S((T, D), _F32), _S((T, D), _MM), _S((1, D), _F32), _S((1, D), _F32)],
        scratch_shapes=[pltpu.VMEM((1, D), _F32)], compiler_params=_params(("arbitrary",)),
    )(z, g, b, target)


def _conv3(X, cw, cb):
    return cb + cw[2:3] * X + cw[1:2] * pltpu.roll(X, 1, 0) + cw[0:1] * pltpu.roll(X, 2, 0)


def _ffn_mid_fwd(h, cw, cb, name):
    _, T, F = h.shape
    tr = _tile(T, FFN_ROWS)
    tc = _tile(F, FFN_FWD_COLS)
    nb = tr // CONV_HALO

    rc = _tile(tr, FFN_CHUNK)
    lanes = [slice(cs * LANE, (cs + 1) * LANE) for cs in range(tc // LANE)]

    def body(h_ref, p_ref, cw_ref, cb_ref, o_ref, c_ref):
        i = pl.program_id(0)

        def work(r0, cols, X):
            hc = [_conv3(X[part], cw_ref[part, :, cols], cb_ref[part, :, cols])[CONV_HALO:] for part in range(2)]
            for part in range(2):
                c_ref[part, pl.ds(r0, rc), cols] = hc[part].astype(c_ref.dtype)
            a, v = hc
            o_ref[pl.ds(r0, rc), cols] = (a * _sigmoid(a) * v).astype(o_ref.dtype)

        for cols in lanes:
            work(0, cols, [jnp.concatenate([jnp.where(i == 0, 0.0, p_ref[part, :, cols]), h_ref[part, 0:rc, cols]], axis=0)
                           for part in range(2)])

        def chunk(c, carry):
            r0 = pl.multiple_of(c * rc, rc)
            for cols in lanes:
                work(r0, cols, [h_ref[part, pl.ds(r0 - CONV_HALO, rc + CONV_HALO), cols] for part in range(2)])
            return carry

        lax.fori_loop(1, tr // rc, chunk, 0)

    return pl.pallas_call(
        body, name=name, grid=(T // tr, F // tc),
        in_specs=[pl.BlockSpec((2, tr, tc), lambda i, j: (0, i, j)),
                  pl.BlockSpec((2, CONV_HALO, tc), lambda i, j: (0, jnp.maximum(i * nb - 1, 0), j)),
                  pl.BlockSpec((2, 3, tc), lambda i, j: (0, 0, j)),
                  pl.BlockSpec((2, 1, tc), lambda i, j: (0, 0, j))],
        out_specs=[pl.BlockSpec((tr, tc), lambda i, j: (i, j)), pl.BlockSpec((2, tr, tc), lambda i, j: (0, i, j))],
        out_shape=[_S((T, F), _MM), _S((2, T, F), _MM)],
        compiler_params=_params(("parallel", "parallel")),
    )(h, h, cw, cb)


def _ffn_mid_bwd(h, hc, dact, cw, name):
    _, T, F = h.shape
    tr = _tile(T, FFN_BWD_ROWS)
    tc = _tile(F, FFN_BWD_COLS)
    nb_c = tr // PACKED_ROWS
    rc = _tile(tr, FFN_CHUNK)
    n = rc + CONV_HALO

    def body(h_ref, c_ref, cn_ref, d_ref, dn_ref, cw_ref, dh_ref, dcw_ref, dcb_ref):
        i = pl.program_id(1)
        is_last = i == pl.num_programs(1) - 1

        @pl.when(i == 0)
        def _():
            dcw_ref[...] = jnp.zeros_like(dcw_ref)
            dcb_ref[...] = jnp.zeros_like(dcb_ref)

        def work(r0, cols, a, v, D):
            sg = _sigmoid(a)
            dhc = [D * v * sg * (1.0 + a * (1.0 - sg)), D * a * sg]
            for part in range(2):
                X = h_ref[part, pl.ds(r0, rc), cols]
                cwp = cw_ref[part, :, cols]
                dh = None
                for k in range(3):
                    g = (dhc[part] if k == 0 else pltpu.roll(dhc[part], n - k, 0))[0:rc]
                    term = cwp[2 - k:3 - k] * g
                    dh = term if dh is None else dh + term
                    dcw_ref[part, 2 - k:3 - k, cols] += jnp.sum(g * X, axis=0, keepdims=True)
                    if k == 0:
                        dcb_ref[part, :, cols] += jnp.sum(g, axis=0, keepdims=True)
                dh_ref[part, pl.ds(r0, rc), cols] = dh.astype(dh_ref.dtype)

        lanes = [slice(cs * LANE, (cs + 1) * LANE) for cs in range(tc // LANE)]

        def chunk(c, carry):
            r0 = pl.multiple_of(c * rc, rc)
            for cols in lanes:
                a, v = [c_ref[part, pl.ds(r0, rc + PACKED_ROWS), cols].astype(_F32)[0:n] for part in range(2)]
                work(r0, cols, a, v, d_ref[pl.ds(r0, rc + PACKED_ROWS), cols].astype(_F32)[0:n])
            return carry

        lax.fori_loop(0, tr // rc - 1, chunk, 0)
        r0 = tr - rc
        for cols in lanes:
            a, v = [jnp.concatenate([c_ref[part, r0:tr, cols].astype(_F32), cn_ref[part, :, cols].astype(_F32)[0:CONV_HALO]],
                                    axis=0) for part in range(2)]
            D = jnp.concatenate([d_ref[r0:tr, cols].astype(_F32),
                                 jnp.where(is_last, 0.0, dn_ref[:, cols].astype(_F32)[0:CONV_HALO])], axis=0)
            work(r0, cols, a, v, D)

    return pl.pallas_call(
        body, name=name, grid=(F // tc, T // tr),
        in_specs=[pl.BlockSpec((2, tr, tc), lambda j, i: (0, i, j)),
                  pl.BlockSpec((2, tr, tc), lambda j, i: (0, i, j)),
                  pl.BlockSpec((2, PACKED_ROWS, tc), lambda j, i: (0, jnp.minimum((i + 1) * nb_c, T // PACKED_ROWS - 1), j)),
                  pl.BlockSpec((tr, tc), lambda j, i: (i, j)),
                  pl.BlockSpec((PACKED_ROWS, tc), lambda j, i: (jnp.minimum((i + 1) * nb_c, T // PACKED_ROWS - 1), j)),
                  pl.BlockSpec((2, 3, tc), lambda j, i: (0, 0, j))],
        out_specs=[pl.BlockSpec((2, tr, tc), lambda j, i: (0, i, j)),
                   pl.BlockSpec((2, 3, tc), lambda j, i: (0, 0, j)),
                   pl.BlockSpec((2, 1, tc), lambda j, i: (0, 0, j))],
        out_shape=[_S((2, T, F), _MM), _S((2, 3, F), _F32), _S((2, 1, F), _F32)],
        compiler_params=_params(("parallel", "arbitrary")),
    )(h, hc, hc, dact, dact, cw)


def _ev_common(h_ref, hp_ref, lng_ref, lnb_ref, ws_ref, bias_ref, i, tr, W):
    H = W // A_HEAD
    u, gu = _gelu_and_grad(h_ref[0])
    v, gv = _gelu_and_grad(h_ref[1])
    mu = jnp.mean(v, axis=-1, keepdims=True)
    vc = v - mu
    rstd = lax.rsqrt(jnp.mean(vc * vc, axis=-1, keepdims=True) + LN_EPS)
    vhat = vc * rstd
    vb = (vhat * lng_ref[...] + lnb_ref[...]).astype(_MM)
    s_chunks = []
    for c in range(tr // A_CHUNK):
        r0 = c * A_CHUNK
        heads = [_dot(ws_ref[hd], vb[r0:r0 + A_CHUNK, hd * A_HEAD:(hd + 1) * A_HEAD]) for hd in range(H)]
        s_chunks.append(jnp.concatenate(heads, axis=1) + bias_ref[...])
    prev = jnp.where(i == 0, 0.0, hp_ref[...])
    X = jnp.concatenate([prev, h_ref[2]], axis=0)
    return u, gu, gv, rstd, vhat, vb, s_chunks, X


def _pool_inv_count(i, tr, rows, win):
    pos = i * tr + _row_index(rows) + 1
    return 1.0 / jnp.minimum(pos, win).astype(_F32)


def _pool_fwd(X, g, Wg, i, tr):
    xg = X[:, g * Wg:(g + 1) * Wg]
    s = xg
    for k in range(g + 1):
        s = s + pltpu.roll(s, 2 ** k, 0)
    return s[POOL_HALO:] * _pool_inv_count(i, tr, tr, 2 ** (g + 1)) - xg[POOL_HALO:]


def _ev_mid_fwd(h, lng, lnb, ws, bias, wp, sc, name):
    _, T, W = h.shape
    tr = _tile(T, 256)
    H = W // A_HEAD
    Wg = W // B_GROUPS
    nb = tr // POOL_HALO

    def body(h_ref, hp_ref, lng_ref, lnb_ref, ws_ref, bias_ref, wp_ref, sc_ref, o_ref):
        i = pl.program_id(0)
        u, _, _, _, _, _, s_chunks, X = _ev_common(h_ref, hp_ref, lng_ref, lnb_ref, ws_ref, bias_ref, i, tr, W)
        for c, s in enumerate(s_chunks):
            r0 = c * A_CHUNK
            o_ref[r0:r0 + A_CHUNK, 0:W] = (u[r0:r0 + A_CHUNK] * s).astype(o_ref.dtype)
        for g in range(B_GROUPS):
            p = _pool_fwd(X, g, Wg, i, tr)
            y = _dot(p.astype(_MM), wp_ref[g]) * sc_ref[:, g * Wg:(g + 1) * Wg]
            o_ref[:, W + g * Wg:W + (g + 1) * Wg] = y.astype(o_ref.dtype)

    vec = pl.BlockSpec((1, W), lambda i: (0, 0))
    return pl.pallas_call(
        body, name=name, grid=(T // tr,),
        in_specs=[pl.BlockSpec((3, tr, W), lambda i: (0, i, 0)),
                  pl.BlockSpec((None, POOL_HALO, W), lambda i: (2, jnp.maximum(i * nb - 1, 0), 0)),
                  vec, vec,
                  pl.BlockSpec((H, A_CHUNK, A_CHUNK), lambda i: (0, 0, 0)),
                  pl.BlockSpec((A_CHUNK, W), lambda i: (0, 0)),
                  pl.BlockSpec((B_GROUPS, Wg, Wg), lambda i: (0, 0, 0)),
                  vec],
        out_specs=pl.BlockSpec((tr, 2 * W), lambda i: (i, 0)), out_shape=_S((T, 2 * W), _MM),
        compiler_params=_params(("parallel",)),
    )(h, h, lng, lnb, ws, bias, wp, sc)


def _ev_mid_bwd(h, dy, lng, lnb, ws, bias, wp, sc, name):
    _, T, W = h.shape
    tr = _tile(T, 256)
    H = W // A_HEAD
    Wg = W // B_GROUPS
    nb = tr // POOL_HALO
    last_blk = T // POOL_HALO - 1
    n = tr + POOL_HALO

    def body(h_ref, hp_ref, dy_ref, dyn_ref, lng_ref, lnb_ref, ws_ref, bias_ref, wp_ref, sc_ref,
             dh_ref, dws_ref, dbias_ref, dlng_ref, dlnb_ref, dwp_ref, dsc_ref):
        i = pl.program_id(0)

        @pl.when(i == 0)
        def _():
            for r in (dws_ref, dbias_ref, dlng_ref, dlnb_ref, dwp_ref, dsc_ref):
                r[...] = jnp.zeros_like(r)

        u, gu, gv, rstd, vhat, vb, s_chunks, X = _ev_common(h_ref, hp_ref, lng_ref, lnb_ref, ws_ref, bias_ref, i, tr, W)
        rr = lax.broadcasted_iota(jnp.int32, (A_CHUNK, A_CHUNK), 0)
        cc = lax.broadcasted_iota(jnp.int32, (A_CHUNK, A_CHUNK), 1)
        tril = rr >= cc
        du_chunks, dvln_chunks = [], []
        for c, s in enumerate(s_chunks):
            r0 = c * A_CHUNK
            dya = dy_ref[r0:r0 + A_CHUNK, 0:W]
            du_chunks.append(dya * s)
            ds = dya * u[r0:r0 + A_CHUNK]
            dbias_ref[...] += ds
            dsb = ds.astype(_MM)
            heads = []
            for hd in range(H):
                cols = slice(hd * A_HEAD, (hd + 1) * A_HEAD)
                dws_ref[hd] += jnp.where(tril, _dot(dsb[:, cols], vb[r0:r0 + A_CHUNK, cols], _NT), 0.0)
                heads.append(_dot(ws_ref[hd], dsb[:, cols], _TN))
            dvln_chunks.append(jnp.concatenate(heads, axis=1))
        du = jnp.concatenate(du_chunks, axis=0)
        dvln = jnp.concatenate(dvln_chunks, axis=0)
        dlng_ref[...] += jnp.sum(dvln * vhat, axis=0, keepdims=True)
        dlnb_ref[...] += jnp.sum(dvln, axis=0, keepdims=True)
        dxh = dvln * lng_ref[...]
        dv = rstd * (dxh - jnp.mean(dxh, axis=-1, keepdims=True) - vhat * jnp.mean(dxh * vhat, axis=-1, keepdims=True))
        dh_ref[:, 0:W] = (du * gu).astype(dh_ref.dtype)
        dh_ref[:, W:2 * W] = (dv * gv).astype(dh_ref.dtype)

        dyb = dy_ref[:, W:2 * W]
        dyb_full = jnp.concatenate([dyb, jnp.where(i == pl.num_programs(0) - 1, 0.0, dyn_ref[...])], axis=0)
        for g in range(B_GROUPS):
            cols = slice(g * Wg, (g + 1) * Wg)
            pb = _pool_fwd(X, g, Wg, i, tr).astype(_MM)
            ypre = _dot(pb, wp_ref[g])
            dsc_ref[:, cols] += jnp.sum(dyb[:, cols] * ypre, axis=0, keepdims=True)
            dyp = (dyb_full[:, cols] * sc_ref[:, cols]).astype(_MM)
            dwp_ref[g] += _dot(pb, dyp[0:tr], _TN)
            dp = _dot(dyp, wp_ref[g], _NT)
            s = dp * _pool_inv_count(i, tr, n, 2 ** (g + 1))
            for k in range(g + 1):
                s = s + pltpu.roll(s, n - 2 ** k, 0)
            dh_ref[:, 2 * W + g * Wg:2 * W + (g + 1) * Wg] = (s[0:tr] - dp[0:tr]).astype(dh_ref.dtype)

    vec = pl.BlockSpec((1, W), lambda i: (0, 0))
    ws_spec = pl.BlockSpec((H, A_CHUNK, A_CHUNK), lambda i: (0, 0, 0))
    bias_spec = pl.BlockSpec((A_CHUNK, W), lambda i: (0, 0))
    wp_spec = pl.BlockSpec((B_GROUPS, Wg, Wg), lambda i: (0, 0, 0))
    return pl.pallas_call(
        body, name=name, grid=(T // tr,),
        in_specs=[pl.BlockSpec((3, tr, W), lambda i: (0, i, 0)),
                  pl.BlockSpec((None, POOL_HALO, W), lambda i: (2, jnp.maximum(i * nb - 1, 0), 0)),
                  pl.BlockSpec((tr, 2 * W), lambda i: (i, 0)),
                  pl.BlockSpec((POOL_HALO, W), lambda i: (jnp.minimum((i + 1) * nb, last_blk), 1)),
                  vec, vec, ws_spec, bias_spec, wp_spec, vec],
        out_specs=[pl.BlockSpec((tr, 3 * W), lambda i: (i, 0)), ws_spec, bias_spec, vec, vec, wp_spec, vec],
        out_shape=[_S((T, 3 * W), _MM), _S((H, A_CHUNK, A_CHUNK), _F32), _S((A_CHUNK, W), _F32), _S((1, W), _F32),
                   _S((1, W), _F32), _S((B_GROUPS, Wg, Wg), _F32), _S((1, W), _F32)],
        compiler_params=_params(("arbitrary",)),
    )(h, h, dy, dy, lng, lnb, ws, bias, wp, sc)


def _chunk_cumsum(x, rin):
    s = 1
    while s < C_CHUNK:
        x = x + jnp.where(rin >= s, pltpu.roll(x, s, 0), 0.0)
        s *= 2
    return x


def _chunk_revcumsum(x, rin):
    n = x.shape[0]
    s = 1
    while s < C_CHUNK:
        x = x + jnp.where(rin + s < C_CHUNK, pltpu.roll(x, n - s, 0), 0.0)
        s *= 2
    return x


def _hgrn_gates(q, fl, lb, tr, tc):
    nch = tr // C_CHUNK
    sq = _sigmoid(q)
    sf = _sigmoid_rel(fl)
    f = lb + (1.0 - lb) * sf
    logf = jnp.log(f)
    rin = _row_index(tr) % C_CHUNK
    b = _chunk_cumsum(logf, rin)
    tot3 = jnp.sum(logf.reshape(nch, C_CHUNK, tc), axis=1, keepdims=True)
    eb = jnp.exp(b)
    enb = jnp.exp(-b)
    ekb = jnp.exp(tot3 - b.reshape(nch, C_CHUNK, tc)).reshape(tr, tc)
    return sq, sf, f, rin, tot3, eb, enb, ekb


def _hgrn_prep_fwd(h, lb, name):
    _, T, D = h.shape
    tr = _tile(T, 512)
    tc = _tile(D, 512)
    nch = tr // C_CHUNK

    def body(q_ref, f_ref, v_ref, lb_ref, qd_ref, kd_ref, ke_ref, vb_ref, dec_ref):
        q = q_ref[...]
        sq, _, f, _, tot3, eb, enb, ekb = _hgrn_gates(q, f_ref[...], lb_ref[...], tr, tc)
        kk = 1.0 - f
        qd_ref[...] = (q * sq * eb).astype(qd_ref.dtype)
        kd_ref[...] = (kk * enb).astype(kd_ref.dtype)
        ke_ref[...] = (kk * ekb).astype(ke_ref.dtype)
        vb_ref[...] = v_ref[...].astype(vb_ref.dtype)
        dec_ref[...] = jnp.exp(tot3).reshape(nch, tc)

    def part(p):
        return pl.BlockSpec((None, tr, tc), lambda i, j: (p, i, j))

    blk = pl.BlockSpec((tr, tc), lambda i, j: (i, j))
    return pl.pallas_call(
        body, name=name, grid=(T // tr, D // tc),
        in_specs=[part(0), part(1), part(2), pl.BlockSpec((1, tc), lambda i, j: (0, j))],
        out_specs=[blk, blk, blk, blk, pl.BlockSpec((nch, tc), lambda i, j: (i, j))],
        out_shape=[_S((T, D), _MM)] * 4 + [_S((T // C_CHUNK, D), _F32)],
        compiler_params=_params(("parallel", "parallel")),
    )(h, h, h, lb)


def _tril_mask():
    rr = lax.broadcasted_iota(jnp.int32, (C_CHUNK, C_CHUNK), 0)
    cc = lax.broadcasted_iota(jnp.int32, (C_CHUNK, C_CHUNK), 1)
    return rr >= cc


def _hgrn_scan_fwd(qd, kd, ke, vb, dec, h, ng, name):
    T, D = qd.shape
    NH = D // C_HEAD
    N = T // C_CHUNK

    def body(qd_ref, kd_ref, ke_ref, vb_ref, dec_ref, g_ref, ng_ref, o_ref, y_ref, st_ref):
        mask = _tril_mask()

        per_trip = math.gcd(N, SCAN_UNROLL)

        def trip(i, St):
            ahead = []
            for u in range(per_trip):
                n = i * per_trip + u
                r = pl.ds(pl.multiple_of(n * C_CHUNK, C_CHUNK), C_CHUNK)
                Qd, Kd, Ke, V = qd_ref[r, :], kd_ref[r, :], ke_ref[r, :], vb_ref[r, :]
                att = jnp.where(mask, _dot(Qd, Kd, _NT), 0.0).astype(_MM)
                ahead.append((n, r, _dot(att, V), _dot(V, Ke, _TN)))
            for n, r, o_intra, update in ahead:
                o_ref[r, :] = o_intra + _dot(qd_ref[r, :], St.astype(_MM), _NT)
                st_ref[n] = St
                St = St * dec_ref[pl.ds(n, 1), :] + update
            return St

        lax.fori_loop(0, N // per_trip, trip, jnp.zeros((C_HEAD, C_HEAD), _F32))
        o = o_ref[...]
        r = lax.rsqrt(jnp.mean(o * o, axis=-1, keepdims=True) + LN_EPS)
        y_ref[...] = (o * r * ng_ref[...] * _sigmoid(g_ref[...])).astype(y_ref.dtype)

    col = pl.BlockSpec((T, C_HEAD), lambda j: (0, j))
    return pl.pallas_call(
        body, name=name, grid=(NH,),
        in_specs=[col, col, col, col, pl.BlockSpec((N, C_HEAD), lambda j: (0, j)),
                  pl.BlockSpec((None, T, C_HEAD), lambda j: (3, 0, j)), pl.BlockSpec((1, C_HEAD), lambda j: (0, j))],
        out_specs=[col, col, pl.BlockSpec((None, N, C_HEAD, C_HEAD), lambda j: (j, 0, 0, 0))],
        out_shape=[_S((T, D), _F32), _S((T, D), _MM), _S((NH, N, C_HEAD, C_HEAD), _F32)],
        compiler_params=_params(("parallel",)),
    )(qd, kd, ke, vb, dec, h, ng)


def _hgrn_scan_bwd(qd, kd, ke, vb, dec, st, o, h, ng, dy, name):
    T, D = qd.shape
    NH = D // C_HEAD
    N = T // C_CHUNK

    def body(qd_ref, kd_ref, ke_ref, vb_ref, dec_ref, st_ref, o_ref, g_ref, ng_ref, dy_ref,
             dqd_ref, dkd_ref, dke_ref, dv_ref, dgate_ref, ddec_ref, dng_ref, do_s):
        o = o_ref[...]
        r = lax.rsqrt(jnp.mean(o * o, axis=-1, keepdims=True) + LN_EPS)
        oh = o * r
        gn = ng_ref[...]
        sg = _sigmoid(g_ref[...])
        d = dy_ref[...]
        dyn = d * sg
        dgate_ref[...] = (d * oh * gn * sg * (1.0 - sg)).astype(dgate_ref.dtype)
        dng_ref[...] = jnp.sum(dyn * oh, axis=0, keepdims=True)
        doh = dyn * gn
        do_s[...] = (r * (doh - oh * jnp.mean(doh * oh, axis=-1, keepdims=True))).astype(do_s.dtype)
        mask = _tril_mask()

        per_trip = math.gcd(N, SCAN_UNROLL)

        def trip(i, dSt):
            ahead = []
            for u in range(per_trip):
                n = N - 1 - (i * per_trip + u)
                rws = pl.ds(pl.multiple_of(n * C_CHUNK, C_CHUNK), C_CHUNK)
                Qd, Kd, V, dO = qd_ref[rws, :], kd_ref[rws, :], vb_ref[rws, :], do_s[rws, :]
                att = jnp.where(mask, _dot(Qd, Kd, _NT), 0.0).astype(_MM)
                dA = jnp.where(mask, _dot(dO, V, _NT), 0.0).astype(_MM)
                dqd_ref[rws, :] = _dot(dA, Kd) + _dot(dO, st_ref[n].astype(_MM))
                dkd_ref[rws, :] = _dot(dA, Qd, _TN)
                ahead.append((n, rws, _dot(att, dO, _TN), _dot(dO, Qd, _TN)))
            for n, rws, dv_intra, d_state in ahead:
                dStb = dSt.astype(_MM)
                dv_ref[rws, :] = (dv_intra + _dot(ke_ref[rws, :], dStb, _NT)).astype(dv_ref.dtype)
                dke_ref[rws, :] = _dot(vb_ref[rws, :], dStb)
                ddec_ref[pl.ds(n, 1), :] = jnp.sum(dSt * st_ref[n], axis=0, keepdims=True)
                dSt = dSt * dec_ref[pl.ds(n, 1), :] + d_state
            return dSt

        lax.fori_loop(0, N // per_trip, trip, jnp.zeros((C_HEAD, C_HEAD), _F32))

    col = pl.BlockSpec((T, C_HEAD), lambda j: (0, j))
    chk = pl.BlockSpec((N, C_HEAD), lambda j: (0, j))
    one = pl.BlockSpec((1, C_HEAD), lambda j: (0, j))
    return pl.pallas_call(
        body, name=name, grid=(NH,),
        in_specs=[col, col, col, col, chk, pl.BlockSpec((None, N, C_HEAD, C_HEAD), lambda j: (j, 0, 0, 0)), col,
                  pl.BlockSpec((None, T, C_HEAD), lambda j: (3, 0, j)), one, col],
        out_specs=[col, col, col, col, col, chk, one],
        out_shape=[_S((T, D), _F32)] * 3 + [_S((T, D), _MM)] * 2 + [_S((N, D), _F32), _S((1, D), _F32)],
        scratch_shapes=[pltpu.VMEM((T, C_HEAD), _MM)],
        compiler_params=_params(("parallel",)),
    )(qd, kd, ke, vb, dec, st, o, h, ng, dy)


def _hgrn_prep_bwd(h, lb, dqd, dkd, dke, dv, dgate, ddec, name):
    _, T, D = h.shape
    tr = _tile(T, 512)
    tc = _tile(D, 256)
    nch = tr // C_CHUNK

    def body(q_ref, f_ref, lb_ref, dqd_ref, dkd_ref, dke_ref, dv_ref, dgate_ref, ddec_ref, dh_ref, dlb_ref):
        @pl.when(pl.program_id(1) == 0)
        def _():
            dlb_ref[...] = jnp.zeros_like(dlb_ref)

        q = q_ref[...]
        lb = lb_ref[...]
        sq, sf, f, rin, tot3, eb, enb, ekb = _hgrn_gates(q, f_ref[...], lb, tr, tc)
        kk = 1.0 - f
        dQd, dKd, dKe = dqd_ref[...], dkd_ref[...], dke_ref[...]
        tq = dQd * eb
        tkd = dKd * enb
        tke = dKe * ekb
        ke_term = tke * kk
        db = tq * (q * sq) - tkd * kk - ke_term
        dtot3 = (jnp.sum(ke_term.reshape(nch, C_CHUNK, tc), axis=1, keepdims=True)
                 + (ddec_ref[...] * jnp.exp(tot3).reshape(nch, tc)).reshape(nch, 1, tc))
        dlogf = (_chunk_revcumsum(db, rin).reshape(nch, C_CHUNK, tc) + dtot3).reshape(tr, tc)
        df = dlogf / f - (tkd + tke)
        dh_ref[0] = (tq * sq * (1.0 + q * (1.0 - sq))).astype(dh_ref.dtype)
        dh_ref[1] = (df * (1.0 - lb) * sf * (1.0 - sf)).astype(dh_ref.dtype)
        dh_ref[2] = dv_ref[...]
        dh_ref[3] = dgate_ref[...]
        dlb_ref[...] += jnp.sum(df * (1.0 - sf), axis=0, keepdims=True)

    def part(p):
        return pl.BlockSpec((None, tr, tc), lambda j, i: (p, i, j))

    blk = pl.BlockSpec((tr, tc), lambda j, i: (i, j))
    vec = pl.BlockSpec((1, tc), lambda j, i: (0, j))
    return pl.pallas_call(
        body, name=name, grid=(D // tc, T // tr),
        in_specs=[part(0), part(1), vec, blk, blk, blk, blk, blk, pl.BlockSpec((nch, tc), lambda j, i: (i, j))],
        out_specs=[pl.BlockSpec((4, tr, tc), lambda j, i: (0, i, j)), vec],
        out_shape=[_S((4, T, D), _MM), _S((1, D), _F32)],
        compiler_params=_params(("parallel", "arbitrary")),
    )(h, h, lb, dqd, dkd, dke, dv, dgate, ddec)


def _sum_in_device_order(me1, own, land, name):
    R, C = own.shape
    tr = _tile(R, 256)

    def body(me_ref, own_ref, land_ref, o_ref):
        me = me_ref[0]
        g = None
        for j in range(N_DEV):
            slot = jnp.maximum(jnp.bitwise_xor(me, j) - 1, 0)
            p = jnp.where(me == j, own_ref[...], land_ref[slot])
            g = p if g is None else g + p
        o_ref[...] = g

    return pl.pallas_call(
        body, name=name,
        grid_spec=pltpu.PrefetchScalarGridSpec(
            num_scalar_prefetch=1, grid=(R // tr,),
            in_specs=[pl.BlockSpec((tr, C), lambda i, me: (i, 0)), pl.BlockSpec((N_DEV - 1, tr, C), lambda i, me: (0, i, 0))],
            out_specs=pl.BlockSpec((tr, C), lambda i, me: (i, 0))),
        out_shape=_S((R, C), _F32), compiler_params=_params(("parallel",)),
    )(me1, own, land)


def _adamw(parts, w, m, v, name):
    P, R, C = parts.shape
    tr = _tile(R, 128) if R % LANE == 0 else R

    def body(p_ref, w_ref, m_ref, v_ref, g_ref, d_ref, nm_ref, nv_ref):
        g = p_ref[0].astype(_F32)
        for s in range(1, P):
            g = g + p_ref[s].astype(_F32)
        nm = ADAM_B1 * m_ref[...] + (1.0 - ADAM_B1) * g
        nv = ADAM_B2 * v_ref[...] + (1.0 - ADAM_B2) * (g * g)
        m_hat = nm / (1.0 - ADAM_B1 ** ADAM_STEP)
        v_hat = nv / (1.0 - ADAM_B2 ** ADAM_STEP)
        g_ref[...] = g
        d_ref[...] = -ADAM_LR * (m_hat / (jnp.sqrt(v_hat) + ADAM_EPS) + ADAM_WD * w_ref[...])
        nm_ref[...] = nm
        nv_ref[...] = nv

    blk = pl.BlockSpec((tr, C), lambda i: (i, 0))
    return pl.pallas_call(
        body, name=name, grid=(R // tr,), in_specs=[pl.BlockSpec((P, tr, C), lambda i: (0, i, 0)), blk, blk, blk],
        out_specs=[blk] * 4, out_shape=[_S((R, C), _F32)] * 4, compiler_params=_params(("parallel",)),
    )(parts, w, m, v)


_HBM = pl.BlockSpec(memory_space=pltpu.HBM)
_SEM = pl.BlockSpec(memory_space=pltpu.SEMAPHORE)
_ANY = pl.BlockSpec(memory_space=pl.ANY)
_N_PEER = N_DEV - 1


def _split_params():
    return pltpu.CompilerParams(has_side_effects=pltpu.SideEffectType.DATAFLOW_SIDE_EFFECTING)


def _blk(ref, axis, n, idx):
    if axis is None:
        return ref
    return ref.at[tuple([slice(None)] * axis + [pl.ds(pl.multiple_of(idx * n, n), n)])]


def _peer(k):
    x, y, c = lax.axis_index("x"), lax.axis_index("y"), lax.axis_index("c")
    px, py, pc = (x + (k >> 2)) % 2, (y + ((k >> 1) & 1)) % 2, (c + (k & 1)) % 2
    return (px, py, pc), 4 * px + 2 * py + pc, 4 * x + 2 * y + c


def _row_tile(rows, pref):
    best = None
    for d in range(16, min(rows, pref) + 1, 16):
        if rows % d == 0:
            best = d
    return best if best is not None else rows


def _place(w, me1, axis, name, layer=None, deps=()):
    R, C = w.shape[-2:]
    tr = _row_tile(R, 512)
    nb = R // tr
    lead = () if layer is None else (None,)
    pre = () if layer is None else (layer,)

    def body(me_ref, w_ref, *rest):
        rest[-1][...] = w_ref[...].astype(rest[-1].dtype)

    if axis == 1:
        out_spec = pl.BlockSpec((tr, C), lambda i, me: (i, me[0]))
        out_shape = _S((R, N_DEV * C), _MM)
    else:
        out_spec = pl.BlockSpec((tr, C), lambda i, me: (me[0] * nb + i, 0))
        out_shape = _S((N_DEV * R, C), _MM)
    return pl.pallas_call(
        body, name=name,
        grid_spec=pltpu.PrefetchScalarGridSpec(
            num_scalar_prefetch=1, grid=(nb,),
            in_specs=[pl.BlockSpec(lead + (tr, C), lambda i, me: pre + (i, 0))] + [_ANY] * len(deps), out_specs=out_spec),
        out_shape=out_shape, compiler_params=_params(("parallel",)),
    )(me1, w, *deps)


_SIBLING = 1
_CHIPS = (2, 4, 6)
_VMEM_TOKEN = pl.BlockSpec(memory_space=pltpu.VMEM)


def _remote(ref_blk, send_sem, recv_sem, dev):
    return pltpu.make_async_remote_copy(src_ref=ref_blk, dst_ref=ref_blk, send_sem=send_sem, recv_sem=recv_sem,
                                        device_id=dev, device_id_type=pl.DeviceIdType.MESH)


def _gather_start(name, full, axis, n):
    def body(f_ref, send, recv, f_out, token):
        for i, k in enumerate((_SIBLING,) + _CHIPS):
            dev, _, me = _peer(k)
            _remote(_blk(f_ref, axis, n, me), send.at[i], recv.at[i], dev).start()
        token[...] = jnp.zeros_like(token)

    return pl.pallas_call(
        body, name=name,
        out_shape=(pltpu.SemaphoreType.DMA((4,)), pltpu.SemaphoreType.DMA((4,)), pltpu.HBM(full.shape, full.dtype),
                   _S((8, LANE), _F32)),
        in_specs=(_HBM,), out_specs=(_SEM, _SEM, _HBM, _VMEM_TOKEN),
        input_output_aliases={0: 2}, compiler_params=_split_params(),
    )(pltpu.with_memory_space_constraint(full, pltpu.HBM))


def _gather_forward(name, full, axis, n, recv, after):
    after = tuple(after) if isinstance(after, (tuple, list)) else (after,)

    def body(f_ref, recv_r, *rest):
        send2, recv2, f_out, token = rest[-4:]
        sib, _, _ = _peer(_SIBLING)
        for i, k in enumerate(_CHIPS):
            dev, frm, _ = _peer(k)
            blk = _blk(f_ref, axis, n, frm)
            _remote(blk, send2.at[i], recv_r.at[1 + i], dev).wait_recv()
            _remote(blk, send2.at[i], recv2.at[i], sib).start()
        token[...] = jnp.zeros_like(token)

    return pl.pallas_call(
        body, name=name,
        out_shape=(pltpu.SemaphoreType.DMA((3,)), pltpu.SemaphoreType.DMA((3,)), pltpu.HBM(full.shape, full.dtype),
                   _S((8, LANE), _F32)),
        in_specs=(_HBM, _SEM) + (_ANY,) * len(after), out_specs=(_SEM, _SEM, _HBM, _VMEM_TOKEN),
        input_output_aliases={0: 2}, compiler_params=_split_params(),
    )(full, recv, *after)


def _gather_wait(name, full, axis, n, send, recv, send2, recv2, after):
    def body(f_ref, send_r, recv_r, send2_r, recv2_r, after_ref, f_out):
        sib, _, me = _peer(_SIBLING)
        blk = _blk(f_ref, axis, n, me)
        for i in range(4):
            _remote(blk, send_r.at[i], recv_r.at[0], sib).wait_send()
        _remote(blk, send_r.at[0], recv_r.at[0], sib).wait_recv()
        for i in range(3):
            cp = _remote(blk, send2_r.at[i], recv2_r.at[i], sib)
            cp.wait_send()
            cp.wait_recv()

    return pl.pallas_call(
        body, name=name, out_shape=pltpu.HBM(full.shape, full.dtype),
        in_specs=(_HBM, _SEM, _SEM, _SEM, _SEM, _ANY), out_specs=_HBM,
        input_output_aliases={0: 0}, compiler_params=_split_params(),
    )(full, send, recv, send2, recv2, after)


def _scatter_start(name, dw, axis, n):
    shard = tuple(n if a == axis else d for a, d in enumerate(dw.shape))
    land = lax.empty((_N_PEER,) + shard, dw.dtype)

    def body(dw_ref, land_ref, send, recv, dw_out, land_out, token):
        for k in range(1, N_DEV):
            dev, to, _ = _peer(k)
            pltpu.make_async_remote_copy(
                src_ref=_blk(dw_ref, axis, n, to), dst_ref=land_ref.at[k - 1], send_sem=send.at[k - 1],
                recv_sem=recv.at[k - 1], device_id=dev, device_id_type=pl.DeviceIdType.MESH).start()
        token[...] = jnp.zeros_like(token)

    return pl.pallas_call(
        body, name=name,
        out_shape=(pltpu.SemaphoreType.DMA((_N_PEER,)), pltpu.SemaphoreType.DMA((_N_PEER,)),
                   pltpu.HBM(dw.shape, dw.dtype), pltpu.HBM(land.shape, land.dtype), _S((8, LANE), _F32)),
        in_specs=(_HBM, _HBM), out_specs=(_SEM, _SEM, _HBM, _HBM, pl.BlockSpec(memory_space=pltpu.VMEM)),
        input_output_aliases={0: 2, 1: 3}, compiler_params=_split_params(),
    )(pltpu.with_memory_space_constraint(dw, pltpu.HBM), pltpu.with_memory_space_constraint(land, pltpu.HBM))


def _scatter_wait(name, items, after):
    ne = len(items)
    after = tuple(after) if isinstance(after, (tuple, list)) else (after,)

    def body(*refs):
        for e, (_, _, _, _, axis, n) in enumerate(items):
            dw_ref, land_ref, send_r, recv_r = refs[4 * e:4 * e + 4]
            for k in range(1, N_DEV):
                dev, to, _ = _peer(k)
                cp = pltpu.make_async_remote_copy(
                    src_ref=_blk(dw_ref, axis, n, to), dst_ref=land_ref.at[k - 1], send_sem=send_r.at[k - 1],
                    recv_sem=recv_r.at[k - 1], device_id=dev, device_id_type=pl.DeviceIdType.MESH)
                cp.wait_send()
                cp.wait_recv()

    args, out_shape = [], []
    for dw, land, send, recv, _, _ in items:
        args += [dw, land, send, recv]
        out_shape += [pltpu.HBM(dw.shape, dw.dtype), pltpu.HBM(land.shape, land.dtype)]
    res = pl.pallas_call(
        body, name=name, out_shape=tuple(out_shape),
        in_specs=(_HBM, _HBM, _SEM, _SEM) * ne + (_ANY,) * len(after), out_specs=(_HBM,) * (2 * ne),
        input_output_aliases={4 * e + j: 2 * e + j for e in range(ne) for j in range(2)},
        compiler_params=_split_params(),
    )(*args, *after)
    return [(res[2 * e], res[2 * e + 1]) for e in range(ne)]


def _adamw_big(me1, dw, land, w, m, v, axis, n, name, layer=None, into=None):
    R, C = land.shape[1:]
    tr = _row_tile(R, 128)
    nb = R // tr
    lead = () if layer is None else (None,)
    pre = () if layer is None else (layer,)

    def body(me_ref, own_ref, land_ref, w_ref, m_ref, v_ref, *rest):
        g_ref, d_ref, nm_ref, nv_ref = rest[-4:]
        g = own_ref[...].astype(_F32)
        for s in range(_N_PEER):
            g = g + land_ref[s].astype(_F32)
        nm = ADAM_B1 * m_ref[...] + (1.0 - ADAM_B1) * g
        nv = ADAM_B2 * v_ref[...] + (1.0 - ADAM_B2) * (g * g)
        m_hat = nm / (1.0 - ADAM_B1 ** ADAM_STEP)
        v_hat = nv / (1.0 - ADAM_B2 ** ADAM_STEP)
        g_ref[...] = g
        d_ref[...] = -ADAM_LR * (m_hat / (jnp.sqrt(v_hat) + ADAM_EPS) + ADAM_WD * w_ref[...])
        nm_ref[...] = nm
        nv_ref[...] = nv

    if axis == 1:
        own_spec = pl.BlockSpec((tr, C), lambda i, me: (i, me[0]))
    else:
        own_spec = pl.BlockSpec((tr, C), lambda i, me: (me[0] * nb + i, 0))
    blk = pl.BlockSpec(lead + (tr, C), lambda i, me: pre + (i, 0))
    in_specs = [own_spec, pl.BlockSpec((_N_PEER, tr, C), lambda i, me: (0, i, 0)), blk, blk, blk]
    args = [me1, dw, land, w, m, v]
    aliases = {}
    if into is not None:
        in_specs += [_ANY] * 4
        aliases = {6 + j: j for j in range(4)}
        args += list(into)
    return pl.pallas_call(
        body, name=name,
        grid_spec=pltpu.PrefetchScalarGridSpec(num_scalar_prefetch=1, grid=(nb,), in_specs=in_specs, out_specs=[blk] * 4),
        out_shape=[_S(w.shape, _F32)] * 4, input_output_aliases=aliases, compiler_params=_params(("parallel",)),
    )(*args)


def _pack(arrs):
    flat = jnp.concatenate([a.reshape(-1).astype(_F32) for a in arrs])
    pad = (-flat.shape[0]) % (LANE * LANE)
    return jnp.pad(flat, (0, pad)).reshape(-1, LANE)


def _unpack(mat, shapes):
    flat = mat.reshape(-1)
    out, off = [], 0
    for s in shapes:
        n = 1
        for d in s:
            n *= d
        out.append(flat[off:off + n].reshape(s))
        off += n
    return out


def _lb_of(lb_param):
    lb_all = jnp.cumsum(jax.nn.softmax(lb_param.astype(_F32), axis=0), axis=0)
    return (lb_all - lb_all[0])[1:2]


def kernel(x, ev_w_in, ev_ln_v_g, ev_ln_v_b, ev_w_s, ev_b_s, ev_w_pool, ev_pool_scale, ev_w_out, od_w_in, od_norm_g, od_w_out, lb_param, ffn_w_up, ffn_conv_w, ffn_conv_b, ffn_w_down, ln1_g, ln1_b, ln2_g, ln2_b, loss_target, m_ev_w_in, m_ev_ln_v_g, m_ev_ln_v_b, m_ev_w_s, m_ev_b_s, m_ev_w_pool, m_ev_pool_scale, m_ev_w_out, m_od_w_in, m_od_norm_g, m_od_w_out, m_lb_param, m_ffn_w_up, m_ffn_conv_w, m_ffn_conv_b, m_ffn_w_down, m_ln1_g, m_ln1_b, m_ln2_g, m_ln2_b, v_ev_w_in, v_ev_ln_v_g, v_ev_ln_v_b, v_ev_w_s, v_ev_b_s, v_ev_w_pool, v_ev_pool_scale, v_ev_w_out, v_od_w_in, v_od_norm_g, v_od_w_out, v_lb_param, v_ffn_w_up, v_ffn_conv_w, v_ffn_conv_b, v_ffn_w_down, v_ln1_g, v_ln1_b, v_ln2_g, v_ln2_b):
    me = 4 * lax.axis_index("x") + 2 * lax.axis_index("y") + lax.axis_index("c")
    T, D = x.shape[1], x.shape[2]
    W = ev_ln_v_g.shape[1]
    H = W // A_HEAD
    Wg = W // B_GROUPS
    F2 = ffn_conv_b.shape[1]
    F = F2 // 2
    n_in0, n_out0 = ev_w_in.shape[2], ev_w_out.shape[1]
    n_in1, n_out1 = od_w_in.shape[2], od_w_out.shape[1]
    n_up, n_dn = ffn_w_up.shape[2], ffn_w_down.shape[1]
    n_pool, n_ng, n_cw = ev_w_pool.shape[2], od_norm_g.shape[1], ffn_conv_w.shape[2]

    small_shards = [od_norm_g, ffn_conv_w, ev_w_pool]
    sp_send, sp_recv, sp_own, sp_land, sp_token = _scatter_start("small_params_start", _pack(small_shards), None, None)

    me1 = me.astype(jnp.int32).reshape(1)
    weights = [
        ("w_in0", ev_w_in[0], None, 1, n_in0), ("w_out0", ev_w_out[0], None, 0, n_out0),
        ("w_up0", ffn_w_up, 0, 1, n_up), ("w_dn0", ffn_w_down, 0, 0, n_dn),
        ("w_in1", od_w_in[0], None, 1, n_in1), ("w_out1", od_w_out[0], None, 0, n_out1),
        ("w_up1", ffn_w_up, 1, 1, n_up), ("w_dn1", ffn_w_down, 1, 0, n_dn),
    ]
    started, tokens = {}, [sp_token]
    for key, w, layer, axis, n in weights:
        full = _place(w, me1, axis, "place_" + key, layer, deps=tokens)
        send, recv, full, token = _gather_start("gather_start_" + key, full, axis, n)
        started[key] = (full, axis, n, send, recv)
        tokens = [token]

    def pass_on(key, after):
        full, axis, n, send, recv = started[key]
        send2, recv2, full, token = _gather_forward("gather_forward_" + key, full, axis, n, recv, after)
        started[key] = (full, axis, n, send, recv, send2, recv2)
        return token

    def gathered(key, after):
        return _gather_wait("gather_wait_" + key, *started[key], after)

    cb_l = [ffn_conv_b[l].reshape(2, 1, F) for l in range(DEPTH)]
    ws_tril = jnp.tril(ev_w_s[0]).astype(_MM)
    bias = jnp.repeat(ev_b_s[0].T, A_HEAD, axis=1)
    lb, lb_vjp = jax.vjp(_lb_of, lb_param)

    small_names = ["ev_ln_v_g", "ev_ln_v_b", "ev_w_s", "ev_b_s", "ev_w_pool", "ev_pool_scale", "od_norm_g", "lb_param",
                   "ffn_conv_w", "ffn_conv_b", "ln1_g", "ln1_b", "ln2_g", "ln2_b"]
    given = dict(ev_ln_v_g=(ev_ln_v_g, m_ev_ln_v_g, v_ev_ln_v_g), ev_ln_v_b=(ev_ln_v_b, m_ev_ln_v_b, v_ev_ln_v_b),
                 ev_w_s=(ev_w_s, m_ev_w_s, v_ev_w_s), ev_b_s=(ev_b_s, m_ev_b_s, v_ev_b_s),
                 ev_w_pool=(ev_w_pool, m_ev_w_pool, v_ev_w_pool),
                 ev_pool_scale=(ev_pool_scale, m_ev_pool_scale, v_ev_pool_scale),
                 od_norm_g=(od_norm_g, m_od_norm_g, v_od_norm_g), lb_param=(lb_param, m_lb_param, v_lb_param),
                 ffn_conv_w=(ffn_conv_w, m_ffn_conv_w, v_ffn_conv_w), ffn_conv_b=(ffn_conv_b, m_ffn_conv_b, v_ffn_conv_b),
                 ln1_g=(ln1_g, m_ln1_g, v_ln1_g), ln1_b=(ln1_b, m_ln1_b, v_ln1_b), ln2_g=(ln2_g, m_ln2_g, v_ln2_g),
                 ln2_b=(ln2_b, m_ln2_b, v_ln2_b))
    shard_axis = dict(ev_w_pool=2, od_norm_g=1, ffn_conv_w=2)
    rep_names = [n for n in small_names if n not in shard_axis]
    shd_names = [n for n in small_names if n in shard_axis]
    small_packs = [_pack([given[n][j] for n in small_names]) for j in range(3)]

    x2 = x[0]
    xb = _cast(x2, _MM, "cast_x", deps=[pass_on("w_in0", tokens[0])])
    w_in0 = gathered("w_in0", xb)
    h0 = _mm(xb, w_in0, "nn", _F32, "ev_in", out_parts=3)

    (sp_own, sp_land), = _scatter_wait("small_params_wait", [(sp_own, sp_land, sp_send, sp_recv, None, None)], h0)
    ng_parts, cw_parts, wp_parts = [], [], []
    for j in range(N_DEV):
        slot = jnp.maximum(jnp.bitwise_xor(me, j) - 1, 0)
        pack_j = jnp.where(me == j, sp_own, lax.dynamic_index_in_dim(sp_land, slot, 0, keepdims=False))
        a, b, c = _unpack(pack_j, [s.shape for s in small_shards])
        ng_parts.append(a)
        cw_parts.append(b)
        wp_parts.append(c)
    norm_g = jnp.concatenate(ng_parts, axis=1)
    conv_w = jnp.concatenate(cw_parts, axis=2)
    wp_b = jnp.concatenate(wp_parts, axis=2)[0].astype(_MM)
    cw_l = [conv_w[l].reshape(3, 2, F).transpose(1, 0, 2) for l in range(DEPTH)]

    tie = pass_on("w_out0", h0)
    yab = _ev_mid_fwd(h0, ev_ln_v_g + tie[0, 0], ev_ln_v_b, ws_tril, bias, wp_b, ev_pool_scale, "ev_mid_fwd")
    w_out0 = gathered("w_out0", yab)
    z1 = _mm(yab, w_out0, "nn", _F32, "ev_out", add=x2, add_scale=ALPHA)
    tie = pass_on("w_up0", (z1, *small_packs))
    x1, x1b = _ln_fwd(z1, ln1_g[0:1] + tie[0, 0], ln1_b[0:1], "ln1_0")
    w_up0 = gathered("w_up0", x1b)
    hf0 = _mm(x1b, w_up0, "nn", _F32, "ffn_up", out_parts=2)
    tie = pass_on("w_dn0", hf0)
    act0, hc0 = _ffn_mid_fwd(hf0, cw_l[0], cb_l[0] + tie[0, 0], "ffn_mid_fwd")
    w_dn0 = gathered("w_dn0", act0)
    z2 = _mm(act0, w_dn0, "nn", _F32, "ffn_down", add=x1, add_scale=ALPHA)
    tie = pass_on("w_in1", z2)
    x2_, x2b = _ln_fwd(z2, ln2_g[0:1] + tie[0, 0], ln2_b[0:1], "ln2_0")
    w_in1 = gathered("w_in1", x2b)
    h1 = _mm(x2b, w_in1, "nn", _F32, "od_in", out_parts=4)
    qd, kd, ke, vb, dec = _hgrn_prep_fwd(h1, lb, "hgrn_prep_fwd")
    tie = pass_on("w_out1", qd)
    o, yo, st = _hgrn_scan_fwd(qd, kd, ke, vb, dec, h1, norm_g + tie[0, 0], "hgrn_scan_fwd")
    w_out1 = gathered("w_out1", yo)
    tie = pass_on("w_up1", yo)
    z3, x3, x3b = _mm(yo, w_out1, "nn", _F32, "od_out", add=x2_, add_scale=ALPHA, ln=(ln1_g[1:2], ln1_b[1:2]), deps=[tie])
    w_up1 = gathered("w_up1", x3b)
    hf1 = _mm(x3b, w_up1, "nn", _F32, "ffn_up", out_parts=2)
    tie = pass_on("w_dn1", hf1)
    act1, hc1 = _ffn_mid_fwd(hf1, cw_l[1], cb_l[1] + tie[0, 0], "ffn_mid_fwd")
    w_dn1 = gathered("w_dn1", act1)
    z4 = _mm(act1, w_dn1, "nn", _F32, "ffn_down", add=x3, add_scale=ALPHA)

    scat = {}

    def scatter(key, dw, axis, n):
        send, recv, dw, land, token = _scatter_start("scatter_start_" + key, dw, axis, n)
        scat[key] = (dw, land, send, recv, axis, n)
        return [token]

    loss11, dz4, dz4b, g_ln2_1, b_ln2_1 = _ln_loss_bwd(z4, ln2_g[1:2], ln2_b[1:2], loss_target[0], "ln_loss_bwd")
    tok = scatter("dn1", _mm(act1, dz4b, "tn", _XCH, "ffn_down_dw"), 0, n_dn)
    dact1 = _mm(dz4b, w_dn1, "nt", _MM, "ffn_down_dx", deps=tok)
    dhf1, dcw1, dcb1 = _ffn_mid_bwd(hf1, hc1, dact1, cw_l[1], "ffn_mid_bwd")
    tok = scatter("up1", _mm(x3b, dhf1, "tn", _XCH, "ffn_up_dw", b_parts=2, deps=tok), 1, n_up)
    dx3 = _mm(dhf1, w_up1, "nt", _F32, "ffn_up_dx", a_parts=2, add=dz4, add_scale=ALPHA, deps=tok)
    dz3, dz3b, g_ln1_1, b_ln1_1 = _ln_bwd(z3, ln1_g[1:2], dx3, "ln_bwd")
    tok = scatter("out1", _mm(yo, dz3b, "tn", _XCH, "od_out_dw", deps=tok), 0, n_out1)
    dyo = _mm(dz3b, w_out1, "nt", _F32, "od_out_dx", deps=tok)
    dqd, dkd, dke, dv, dgate, ddec, dng = _hgrn_scan_bwd(qd, kd, ke, vb, dec, st, o, h1, norm_g, dyo, "hgrn_scan_bwd")
    dh1, dlb = _hgrn_prep_bwd(h1, lb, dqd, dkd, dke, dv, dgate, ddec, "hgrn_prep_bwd")
    tok = scatter("in1", _mm(x2b, dh1, "tn", _XCH, "od_in_dw", b_parts=4, deps=tok), 1, n_in1)
    dx2 = _mm(dh1, w_in1, "nt", _F32, "od_in_dx", a_parts=4, add=dz3, add_scale=ALPHA, deps=tok)
    dz2, dz2b, g_ln2_0, b_ln2_0 = _ln_bwd(z2, ln2_g[0:1], dx2, "ln_bwd")
    tok = scatter("dn0", _mm(act0, dz2b, "tn", _XCH, "ffn_down_dw", deps=tok), 0, n_dn)
    dact0 = _mm(dz2b, w_dn0, "nt", _MM, "ffn_down_dx", deps=tok)
    dhf0, dcw0, dcb0 = _ffn_mid_bwd(hf0, hc0, dact0, cw_l[0], "ffn_mid_bwd")
    tok = scatter("up0", _mm(x1b, dhf0, "tn", _XCH, "ffn_up_dw", b_parts=2, deps=tok), 1, n_up)
    dx1 = _mm(dhf0, w_up0, "nt", _F32, "ffn_up_dx", a_parts=2, add=dz2, add_scale=ALPHA, deps=tok)
    dz1, dz1b, g_ln1_0, b_ln1_0 = _ln_bwd(z1, ln1_g[0:1], dx1, "ln_bwd")
    tok = scatter("out0", _mm(yab, dz1b, "tn", _XCH, "ev_out_dw", deps=tok), 0, n_out0)
    dyab = _mm(dz1b, w_out0, "nt", _F32, "ev_out_dx", deps=tok)
    dh0, dws, dbias, dlng, dlnb, dwp, dsc = _ev_mid_bwd(h0, dyab, ev_ln_v_g, ev_ln_v_b, ws_tril, bias, wp_b,
                                                        ev_pool_scale, "ev_mid_bwd")

    g_b_s = dbias.reshape(A_CHUNK, H, A_HEAD).sum(axis=-1).T[None]
    g_conv_w = jnp.stack([d.transpose(1, 0, 2).reshape(3, F2) for d in (dcw0, dcw1)])
    g_conv_b = jnp.stack([d.reshape(F2) for d in (dcb0, dcb1)])
    small_grads = dict(zip(small_names, [
        dlng, dlnb, dws[None], g_b_s, dwp[None], dsc, dng, lb_vjp(dlb)[0], g_conv_w, g_conv_b,
        jnp.concatenate([g_ln1_0, g_ln1_1]), jnp.concatenate([b_ln1_0, b_ln1_1]),
        jnp.concatenate([g_ln2_0, g_ln2_1]), jnp.concatenate([b_ln2_0, b_ln2_1])]))

    def by_device(g, ax):
        g = g.reshape(g.shape[:ax] + (N_DEV, g.shape[ax] // N_DEV) + g.shape[ax + 1:])
        return jnp.moveaxis(g, ax, 0).reshape(N_DEV, -1)

    shd = jnp.concatenate([by_device(small_grads[n], shard_axis[n]) for n in shd_names], axis=1)
    shd_pack = jnp.pad(shd, ((0, 0), (0, (-shd.shape[1]) % (LANE * LANE)))).reshape(-1, LANE)
    shd_rows = shd_pack.shape[0] // N_DEV
    rep_pack = _pack([small_grads[n] for n in rep_names])
    tok = scatter("in0", _mm(xb, dh0, "tn", _XCH, "ev_in_dw", deps=tok), 1, n_in0)
    tok = scatter("small_rep", rep_pack + tok[0][0, 0], None, None)
    tok = scatter("small_shd", shd_pack + tok[0][0, 0], 0, shd_rows)
    grad_x = _mm(dh0, w_in0, "nt", _F32, "ev_in_dx", add=dz1, add_scale=ALPHA, deps=tok)

    def landed(name, keys, after):
        got = _scatter_wait(name, [scat[k] for k in keys], after)
        return {k: (me1, dw, land) for k, (dw, land) in zip(keys, got)}

    early = landed("scatter_wait_early", ["dn1", "up1", "out1", "in1", "dn0", "up0", "out0"], grad_x)
    big = {}
    r_dn = _adamw_big(*early["dn1"], ffn_w_down, m_ffn_w_down, v_ffn_w_down, 0, n_dn, "adamw_w_dn1", layer=1)
    r_up = _adamw_big(*early["up1"], ffn_w_up, m_ffn_w_up, v_ffn_w_up, 1, n_up, "adamw_w_up1", layer=1)
    big["od_w_out"] = _adamw_big(*early["out1"], od_w_out[0], m_od_w_out[0], v_od_w_out[0], 0, n_out1, "adamw_w_out1")
    big["od_w_in"] = _adamw_big(*early["in1"], od_w_in[0], m_od_w_in[0], v_od_w_in[0], 1, n_in1, "adamw_w_in1")
    big["ffn_w_down"] = _adamw_big(*early["dn0"], ffn_w_down, m_ffn_w_down, v_ffn_w_down, 0, n_dn, "adamw_w_dn0", layer=0, into=r_dn)
    big["ffn_w_up"] = _adamw_big(*early["up0"], ffn_w_up, m_ffn_w_up, v_ffn_w_up, 1, n_up, "adamw_w_up0", layer=0, into=r_up)
    big["ev_w_out"] = _adamw_big(*early["out0"], ev_w_out[0], m_ev_w_out[0], v_ev_w_out[0], 0, n_out0, "adamw_w_out0")
    late = landed("scatter_wait_late", ["in0", "small_rep", "small_shd"],
                  (big["ffn_w_down"][0], big["ffn_w_up"][0], big["od_w_in"][0], big["ev_w_out"][0]))
    big["ev_w_in"] = _adamw_big(*late["in0"], ev_w_in[0], m_ev_w_in[0], v_ev_w_in[0], 1, n_in0, "adamw_w_in0")

    rep_mat = _sum_in_device_order(*late["small_rep"], "sum_small_rep")
    local_g = dict(zip(rep_names, _unpack(rep_mat, [small_grads[n].shape for n in rep_names])))
    _, shd_all, shd_land = late["small_shd"]
    shd_own = lax.dynamic_slice_in_dim(shd_all, me * shd_rows, shd_rows, axis=0)
    shd_mat = _sum_in_device_order(me1, shd_own, shd_land, "sum_small_shd")
    local_g.update(zip(shd_names, _unpack(shd_mat, [given[n][0].shape for n in shd_names])))
    local_shapes = [given[n][0].shape for n in small_names]
    res = _adamw(_pack([local_g[n] for n in small_names])[None], *small_packs, "adamw_small")
    small = {n: [] for n in small_names}
    for r in res:
        for n, a in zip(small_names, _unpack(r, local_shapes)):
            small[n].append(a)

    loss = lax.psum(loss11[0, 0], ("x", "y", "c"))
    order = ["ev_w_in", "ev_ln_v_g", "ev_ln_v_b", "ev_w_s", "ev_b_s", "ev_w_pool", "ev_pool_scale", "ev_w_out", "od_w_in",
             "od_norm_g", "od_w_out", "lb_param", "ffn_w_up", "ffn_conv_w", "ffn_conv_b", "ffn_w_down", "ln1_g", "ln1_b",
             "ln2_g", "ln2_b"]
    shapes = dict(ev_w_in=ev_w_in.shape, ev_w_out=ev_w_out.shape, od_w_in=od_w_in.shape, od_w_out=od_w_out.shape,
                  ffn_w_up=ffn_w_up.shape, ffn_w_down=ffn_w_down.shape)
    outs = [loss, grad_x[None]]
    for kind in range(4):
        for n in order:
            outs.append(big[n][kind].reshape(shapes[n]) if n in big else small[n][kind])
    return tuple(outs)
```

```python
import functools
import math

import jax
import jax.numpy as jnp
from jax import lax
from jax.experimental import pallas as pl
from jax.experimental.pallas import tpu as pltpu

_MM = jnp.bfloat16
_XCH = jnp.bfloat16

DEPTH = 2
ALPHA = (2 * DEPTH) ** 0.25
LN_EPS = 1e-5
A_CHUNK = 128
A_HEAD = 128
B_GROUPS = 4
POOL_HALO = 16
C_CHUNK = 64
C_HEAD = 128
SCAN_UNROLL = 64
CONV_HALO = 8
PACKED_ROWS = 16
FFN_ROWS, FFN_FWD_COLS = 512, 1408
FFN_BWD_ROWS, FFN_BWD_COLS = 512, 1408
FFN_CHUNK = 128
ADAM_LR, ADAM_B1, ADAM_B2, ADAM_EPS, ADAM_WD, ADAM_STEP = 0.001, 0.9, 0.999, 1e-08, 0.01, 10
N_DEV = 8
LANE = 128
VMEM_LIMIT = 56 * 1024 * 1024
MM_FULL_K = 3072
MM_FULL_K_TN = 4096
MM_DEEP_K = 2816
MM_LN_ROWS, MM_LN_K = 512, 1408

_F32 = jnp.float32
_NN = (((1,), (0,)), ((), ()))
_NT = (((1,), (1,)), ((), ()))
_TN = (((0,), (0,)), ((), ()))
_S = jax.ShapeDtypeStruct


def _dot(a, b, dims=_NN):
    return lax.dot_general(a, b, dims, preferred_element_type=_F32)


def _tile(dim, pref):
    best = None
    d = LANE
    while d <= min(dim, pref):
        if dim % d == 0:
            best = d
        d += LANE
    return best if best is not None else dim


def _params(sem):
    return pltpu.CompilerParams(dimension_semantics=sem, vmem_limit_bytes=VMEM_LIMIT)


def _sigmoid(x):
    return 0.5 * jnp.tanh(0.5 * x) + 0.5


def _sigmoid_rel(x):
    return 1.0 / (1.0 + jnp.exp(-x))


_GELU_C = 0.7978845608028654
_GELU_A = 0.044715


def _gelu_and_grad(x):
    t = jnp.tanh(_GELU_C * (x + _GELU_A * x * x * x))
    y = 0.5 * x * (1.0 + t)
    dy = 0.5 * (1.0 + t) + 0.5 * x * (1.0 - t * t) * _GELU_C * (1.0 + 3.0 * _GELU_A * x * x)
    return y, dy


def _row_index(n):
    return lax.broadcasted_iota(jnp.int32, (n, 1), 0)


def _mm_tiles(mode, M, N, K, with_add):
    if mode == "tn":
        return _tile(M, 1024), _tile(N, 1024), _tile(K, MM_FULL_K_TN)
    if K <= MM_FULL_K:
        return _tile(M, 1024 if with_add else 2048), _tile(N, 1024 if mode == "nn" else 512), K
    return _tile(M, 1024), _tile(N, 1024), _tile(K, MM_DEEP_K)


def _mm(a, b, mode, out_dtype, name, *, a_parts=1, b_parts=1, out_parts=1, add=None, add_scale=1.0, deps=(), tiles=None,
        ln=None):
    if mode == "nn":
        M, K = a.shape
        N = b.shape[1]
    elif mode == "nt":
        if a_parts > 1:
            M, K = a.shape[1], a.shape[2] * a_parts
        else:
            M, K = a.shape
        N = b.shape[0]
    else:
        K, M = a.shape
        N = b.shape[-1] * b_parts
    tm, tn, tk = tiles if tiles is not None else _mm_tiles(mode, M, N // max(b_parts, out_parts), K // a_parts, add is not None)
    if ln is not None:
        tm, tn, tk = _tile(M, MM_LN_ROWS), N, (K if K <= MM_FULL_K else _tile(K, MM_LN_K))
    nk = K // tk
    npj = (N // max(b_parts, out_parts)) // tn
    nkp = (K // a_parts) // tk
    if mode == "nn":
        a_spec = pl.BlockSpec((tm, tk), lambda i, j, k: (i, k))
        b_spec = pl.BlockSpec((tk, tn), lambda i, j, k: (k, j))
        dims = _NN
    elif mode == "nt":
        if a_parts > 1:
            a_spec = pl.BlockSpec((None, tm, tk), lambda i, j, k: (k // nkp, i, k % nkp))
        else:
            a_spec = pl.BlockSpec((tm, tk), lambda i, j, k: (i, k))
        b_spec = pl.BlockSpec((tn, tk), lambda i, j, k: (j, k))
        dims = _NT
    else:
        a_spec = pl.BlockSpec((tk, tm), lambda i, j, k: (k, i))
        if b_parts > 1:
            b_spec = pl.BlockSpec((None, tk, tn), lambda i, j, k: (j // npj, k, j % npj))
        else:
            b_spec = pl.BlockSpec((tk, tn), lambda i, j, k: (k, j))
        dims = _TN
    if out_parts > 1:
        out_spec = pl.BlockSpec((None, tm, tn), lambda i, j, k: (j // npj, i, j % npj))
        out_shape = _S((out_parts, M, N // out_parts), out_dtype)
    else:
        out_spec = pl.BlockSpec((tm, tn), lambda i, j, k: (i, j))
        out_shape = _S((M, N), out_dtype)
    in_specs = [a_spec, b_spec]
    args = [a, b]
    if add is not None:
        in_specs.append(pl.BlockSpec((tm, tn), lambda i, j, k: (i, j)))
        args.append(add)
    if ln is not None:
        vec = pl.BlockSpec((1, N), lambda i, j, k: (0, 0))
        in_specs += [vec, vec]
        args += list(ln)
        out_spec = [out_spec] * 3
        out_shape = [_S((M, N), _F32), _S((M, N), _F32), _S((M, N), _MM)]
    n_ln = 2 + (add is not None)
    in_specs += [_ANY] * len(deps)
    args += list(deps)
    n_out = 1 if ln is None else 3

    def finish(r, refs, outs):
        if add is not None:
            r = r + add_scale * refs[2][...]
        if ln is None:
            outs[0][...] = r.astype(outs[0].dtype)
            return
        mu = jnp.mean(r, axis=-1, keepdims=True)
        rc = r - mu
        y = rc * lax.rsqrt(jnp.mean(rc * rc, axis=-1, keepdims=True) + LN_EPS) * refs[n_ln][...] + refs[n_ln + 1][...]
        outs[0][...] = r
        outs[1][...] = y
        outs[2][...] = y.astype(outs[2].dtype)

    def body_one(*refs):
        finish(_dot(refs[0][...], refs[1][...], dims), refs, refs[len(refs) - n_out:])

    def body_acc(*refs):
        acc = refs[-1]
        k = pl.program_id(2)

        @pl.when(k == 0)
        def _():
            acc[...] = jnp.zeros_like(acc)

        acc[...] += _dot(refs[0][...], refs[1][...], dims)

        @pl.when(k == nk - 1)
        def _():
            finish(acc[...], refs, refs[len(refs) - 1 - n_out:len(refs) - 1])

    return pl.pallas_call(
        body_one if nk == 1 else body_acc, name=name, grid=(M // tm, N // tn, nk), in_specs=in_specs,
        out_specs=out_spec, out_shape=out_shape,
        scratch_shapes=[] if nk == 1 else [pltpu.VMEM((tm, tn), _F32)],
        compiler_params=_params(("parallel", "parallel", "arbitrary")),
    )(*args)


def _cast(x2d, dtype, name, deps=()):
    R, C = x2d.shape
    tr = _tile(R, 512) if R % LANE == 0 else R

    def body(x_ref, *rest):
        rest[-1][...] = x_ref[...].astype(rest[-1].dtype)

    return pl.pallas_call(
        body, name=name, grid=(R // tr,), in_specs=[pl.BlockSpec((tr, C), lambda i: (i, 0))] + [_ANY] * len(deps),
        out_specs=pl.BlockSpec((tr, C), lambda i: (i, 0)), out_shape=_S((R, C), dtype),
        compiler_params=_params(("parallel",)),
    )(x2d, *deps)


def _ln_fwd(z, g, b, name):
    T, D = z.shape
    tr = _tile(T, 256)

    def body(z_ref, g_ref, b_ref, y_ref, yb_ref):
        zz = z_ref[...]
        mu = jnp.mean(zz, axis=-1, keepdims=True)
        zc = zz - mu
        var = jnp.mean(zc * zc, axis=-1, keepdims=True)
        y = zc * lax.rsqrt(var + LN_EPS) * g_ref[...] + b_ref[...]
        y_ref[...] = y
        yb_ref[...] = y.astype(yb_ref.dtype)

    row = pl.BlockSpec((tr, D), lambda i: (i, 0))
    vec = pl.BlockSpec((1, D), lambda i: (0, 0))
    return pl.pallas_call(
        body, name=name, grid=(T // tr,), in_specs=[row, vec, vec], out_specs=[row, row],
        out_shape=[_S((T, D), _F32), _S((T, D), _MM)], compiler_params=_params(("parallel",)),
    )(z, g, b)


def _ln_bwd(z, g, dy, name):
    T, D = z.shape
    tr = _tile(T, 256)

    def body(z_ref, g_ref, dy_ref, dz_ref, dzb_ref, dg_ref, db_ref):
        @pl.when(pl.program_id(0) == 0)
        def _():
            dg_ref[...] = jnp.zeros_like(dg_ref)
            db_ref[...] = jnp.zeros_like(db_ref)

        zz = z_ref[...]
        mu = jnp.mean(zz, axis=-1, keepdims=True)
        zc = zz - mu
        rstd = lax.rsqrt(jnp.mean(zc * zc, axis=-1, keepdims=True) + LN_EPS)
        xh = zc * rstd
        d = dy_ref[...]
        dg_ref[...] += jnp.sum(d * xh, axis=0, keepdims=True)
        db_ref[...] += jnp.sum(d, axis=0, keepdims=True)
        dxh = d * g_ref[...]
        dz = rstd * (dxh - jnp.mean(dxh, axis=-1, keepdims=True) - xh * jnp.mean(dxh * xh, axis=-1, keepdims=True))
        dz_ref[...] = dz
        dzb_ref[...] = dz.astype(dzb_ref.dtype)

    row = pl.BlockSpec((tr, D), lambda i: (i, 0))
    vec = pl.BlockSpec((1, D), lambda i: (0, 0))
    return pl.pallas_call(
        body, name=name, grid=(T // tr,), in_specs=[row, vec, row], out_specs=[row, row, vec, vec],
        out_shape=[_S((T, D), _F32), _S((T, D), _MM), _S((1, D), _F32), _S((1, D), _F32)],
        compiler_params=_params(("arbitrary",)),
    )(z, g, dy)


def _ln_loss_bwd(z, g, b, target, name):
    T, D = z.shape
    tr = _tile(T, 256)

    def body(z_ref, g_ref, b_ref, t_ref, loss_ref, dz_ref, dzb_ref, dg_ref, db_ref, lacc):
        i = pl.program_id(0)

        @pl.when(i == 0)
        def _():
            dg_ref[...] = jnp.zeros_like(dg_ref)
            db_ref[...] = jnp.zeros_like(db_ref)
            lacc[...] = jnp.zeros_like(lacc)

        zz = z_ref[...]
        mu = jnp.mean(zz, axis=-1, keepdims=True)
        zc = zz - mu
        rstd = lax.rsqrt(jnp.mean(zc * zc, axis=-1, keepdims=True) + LN_EPS)
        xh = zc * rstd
        err = xh * g_ref[...] + b_ref[...] - t_ref[...]
        lacc[...] += jnp.sum(err * err, axis=0, keepdims=True)
        d = err * (1.0 / D)
        dg_ref[...] += jnp.sum(d * xh, axis=0, keepdims=True)
        db_ref[...] += jnp.sum(d, axis=0, keepdims=True)
        dxh = d * g_ref[...]
        dz = rstd * (dxh - jnp.mean(dxh, axis=-1, keepdims=True) - xh * jnp.mean(dxh * xh, axis=-1, keepdims=True))
        dz_ref[...] = dz
        dzb_ref[...] = dz.astype(dzb_ref.dtype)

        @pl.when(i == pl.num_programs(0) - 1)
        def _():
            loss_ref[...] = jnp.sum(lacc[...], axis=-1, keepdims=True) * (0.5 / D)

    row = pl.BlockSpec((tr, D), lambda i: (i, 0))
    vec = pl.BlockSpec((1, D), lambda i: (0, 0))
    one = pl.BlockSpec((1, 1), lambda i: (0, 0))
    return pl.pallas_call(
        body, name=name, grid=(T // tr,), in_specs=[row, vec, vec, row], out_specs=[one, row, row, vec, vec],
        out_shape=[_S((1, 1), _F32), _S((T, D), _F32), _S((T, D), _MM), _S((1, D), _F32), _S((1, D), _F32)],
        scratch_shapes=[pltpu.VMEM((1, D), _F32)], compiler_params=_params(("arbitrary",)),
    )(z, g, b, target)


def _conv3(X, cw, cb):
    return cb + cw[2:3] * X + cw[1:2] * pltpu.roll(X, 1, 0) + cw[0:1] * pltpu.roll(X, 2, 0)


def _ffn_mid_fwd(h, cw, cb, name):
    _, T, F = h.shape
    tr = _tile(T, FFN_ROWS)
    tc = _tile(F, FFN_FWD_COLS)
    nb = tr // CONV_HALO

    rc = _tile(tr, FFN_CHUNK)
    lanes = [slice(cs * LANE, (cs + 1) * LANE) for cs in range(tc // LANE)]

    def body(h_ref, p_ref, cw_ref, cb_ref, o_ref, c_ref):
        i = pl.program_id(0)

        def work(r0, cols, X):
            hc = [_conv3(X[part], cw_ref[part, :, cols], cb_ref[part, :, cols])[CONV_HALO:] for part in range(2)]
            for part in range(2):
                c_ref[part, pl.ds(r0, rc), cols] = hc[part].astype(c_ref.dtype)
            a, v = hc
            o_ref[pl.ds(r0, rc), cols] = (a * _sigmoid(a) * v).astype(o_ref.dtype)

        for cols in lanes:
            work(0, cols, [jnp.concatenate([jnp.where(i == 0, 0.0, p_ref[part, :, cols]), h_ref[part, 0:rc, cols]], axis=0)
                           for part in range(2)])

        def chunk(c, carry):
            r0 = pl.multiple_of(c * rc, rc)
            for cols in lanes:
                work(r0, cols, [h_ref[part, pl.ds(r0 - CONV_HALO, rc + CONV_HALO), cols] for part in range(2)])
            return carry

        lax.fori_loop(1, tr // rc, chunk, 0)

    return pl.pallas_call(
        body, name=name, grid=(T // tr, F // tc),
        in_specs=[pl.BlockSpec((2, tr, tc), lambda i, j: (0, i, j)),
                  pl.BlockSpec((2, CONV_HALO, tc), lambda i, j: (0, jnp.maximum(i * nb - 1, 0), j)),
                  pl.BlockSpec((2, 3, tc), lambda i, j: (0, 0, j)),
                  pl.BlockSpec((2, 1, tc), lambda i, j: (0, 0, j))],
        out_specs=[pl.BlockSpec((tr, tc), lambda i, j: (i, j)), pl.BlockSpec((2, tr, tc), lambda i, j: (0, i, j))],
        out_shape=[_S((T, F), _MM), _S((2, T, F), _MM)],
        compiler_params=_params(("parallel", "parallel")),
    )(h, h, cw, cb)


def _ffn_mid_bwd(h, hc, dact, cw, name):
    _, T, F = h.shape
    tr = _tile(T, FFN_BWD_ROWS)
    tc = _tile(F, FFN_BWD_COLS)
    nb_c = tr // PACKED_ROWS
    rc = _tile(tr, FFN_CHUNK)
    n = rc + CONV_HALO

    def body(h_ref, c_ref, cn_ref, d_ref, dn_ref, cw_ref, dh_ref, dcw_ref, dcb_ref):
        i = pl.program_id(1)
        is_last = i == pl.num_programs(1) - 1

        @pl.when(i == 0)
        def _():
            dcw_ref[...] = jnp.zeros_like(dcw_ref)
            dcb_ref[...] = jnp.zeros_like(dcb_ref)

        def work(r0, cols, a, v, D):
            sg = _sigmoid(a)
            dhc = [D * v * sg * (1.0 + a * (1.0 - sg)), D * a * sg]
            for part in range(2):
                X = h_ref[part, pl.ds(r0, rc), cols]
                cwp = cw_ref[part, :, cols]
                dh = None
                for k in range(3):
                    g = (dhc[part] if k == 0 else pltpu.roll(dhc[part], n - k, 0))[0:rc]
                    term = cwp[2 - k:3 - k] * g
                    dh = term if dh is None else dh + term
                    dcw_ref[part, 2 - k:3 - k, cols] += jnp.sum(g * X, axis=0, keepdims=True)
                    if k == 0:
                        dcb_ref[part, :, cols] += jnp.sum(g, axis=0, keepdims=True)
                dh_ref[part, pl.ds(r0, rc), cols] = dh.astype(dh_ref.dtype)

        lanes = [slice(cs * LANE, (cs + 1) * LANE) for cs in range(tc // LANE)]

        def chunk(c, carry):
            r0 = pl.multiple_of(c * rc, rc)
            for cols in lanes:
                a, v = [c_ref[part, pl.ds(r0, rc + PACKED_ROWS), cols].astype(_F32)[0:n] for part in range(2)]
                work(r0, cols, a, v, d_ref[pl.ds(r0, rc + PACKED_ROWS), cols].astype(_F32)[0:n])
            return carry

        lax.fori_loop(0, tr // rc - 1, chunk, 0)
        r0 = tr - rc
        for cols in lanes:
            a, v = [jnp.concatenate([c_ref[part, r0:tr, cols].astype(_F32), cn_ref[part, :, cols].astype(_F32)[0:CONV_HALO]],
                                    axis=0) for part in range(2)]
            D = jnp.concatenate([d_ref[r0:tr, cols].astype(_F32),
                                 jnp.where(is_last, 0.0, dn_ref[:, cols].astype(_F32)[0:CONV_HALO])], axis=0)
            work(r0, cols, a, v, D)

    return pl.pallas_call(
        body, name=name, grid=(F // tc, T // tr),
        in_specs=[pl.BlockSpec((2, tr, tc), lambda j, i: (0, i, j)),
                  pl.BlockSpec((2, tr, tc), lambda j, i: (0, i, j)),
                  pl.BlockSpec((2, PACKED_ROWS, tc), lambda j, i: (0, jnp.minimum((i + 1) * nb_c, T // PACKED_ROWS - 1), j)),
                  pl.BlockSpec((tr, tc), lambda j, i: (i, j)),
                  pl.BlockSpec((PACKED_ROWS, tc), lambda j, i: (jnp.minimum((i + 1) * nb_c, T // PACKED_ROWS - 1), j)),
                  pl.BlockSpec((2, 3, tc), lambda j, i: (0, 0, j))],
        out_specs=[pl.BlockSpec((2, tr, tc), lambda j, i: (0, i, j)),
                   pl.BlockSpec((2, 3, tc), lambda j, i: (0, 0, j)),
                   pl.BlockSpec((2, 1, tc), lambda j, i: (0, 0, j))],
        out_shape=[_S((2, T, F), _MM), _S((2, 3, F), _F32), _S((2, 1, F), _F32)],
        compiler_params=_params(("parallel", "arbitrary")),
    )(h, hc, hc, dact, dact, cw)


def _ev_common(h_ref, hp_ref, lng_ref, lnb_ref, ws_ref, bias_ref, i, tr, W):
    H = W // A_HEAD
    u, gu = _gelu_and_grad(h_ref[0])
    v, gv = _gelu_and_grad(h_ref[1])
    mu = jnp.mean(v, axis=-1, keepdims=True)
    vc = v - mu
    rstd = lax.rsqrt(jnp.mean(vc * vc, axis=-1, keepdims=True) + LN_EPS)
    vhat = vc * rstd
    vb = (vhat * lng_ref[...] + lnb_ref[...]).astype(_MM)
    s_chunks = []
    for c in range(tr // A_CHUNK):
        r0 = c * A_CHUNK
        heads = [_dot(ws_ref[hd], vb[r0:r0 + A_CHUNK, hd * A_HEAD:(hd + 1) * A_HEAD]) for hd in range(H)]
        s_chunks.append(jnp.concatenate(heads, axis=1) + bias_ref[...])
    prev = jnp.where(i == 0, 0.0, hp_ref[...])
    X = jnp.concatenate([prev, h_ref[2]], axis=0)
    return u, gu, gv, rstd, vhat, vb, s_chunks, X


def _pool_inv_count(i, tr, rows, win):
    pos = i * tr + _row_index(rows) + 1
    return 1.0 / jnp.minimum(pos, win).astype(_F32)


def _pool_fwd(X, g, Wg, i, tr):
    xg = X[:, g * Wg:(g + 1) * Wg]
    s = xg
    for k in range(g + 1):
        s = s + pltpu.roll(s, 2 ** k, 0)
    return s[POOL_HALO:] * _pool_inv_count(i, tr, tr, 2 ** (g + 1)) - xg[POOL_HALO:]


def _ev_mid_fwd(h, lng, lnb, ws, bias, wp, sc, name):
    _, T, W = h.shape
    tr = _tile(T, 256)
    H = W // A_HEAD
    Wg = W // B_GROUPS
    nb = tr // POOL_HALO

    def body(h_ref, hp_ref, lng_ref, lnb_ref, ws_ref, bias_ref, wp_ref, sc_ref, o_ref):
        i = pl.program_id(0)
        u, _, _, _, _, _, s_chunks, X = _ev_common(h_ref, hp_ref, lng_ref, lnb_ref, ws_ref, bias_ref, i, tr, W)
        for c, s in enumerate(s_chunks):
            r0 = c * A_CHUNK
            o_ref[r0:r0 + A_CHUNK, 0:W] = (u[r0:r0 + A_CHUNK] * s).astype(o_ref.dtype)
        for g in range(B_GROUPS):
            p = _pool_fwd(X, g, Wg, i, tr)
            y = _dot(p.astype(_MM), wp_ref[g]) * sc_ref[:, g * Wg:(g + 1) * Wg]
            o_ref[:, W + g * Wg:W + (g + 1) * Wg] = y.astype(o_ref.dtype)

    vec = pl.BlockSpec((1, W), lambda i: (0, 0))
    return pl.pallas_call(
        body, name=name, grid=(T // tr,),
        in_specs=[pl.BlockSpec((3, tr, W), lambda i: (0, i, 0)),
                  pl.BlockSpec((None, POOL_HALO, W), lambda i: (2, jnp.maximum(i * nb - 1, 0), 0)),
                  vec, vec,
                  pl.BlockSpec((H, A_CHUNK, A_CHUNK), lambda i: (0, 0, 0)),
                  pl.BlockSpec((A_CHUNK, W), lambda i: (0, 0)),
                  pl.BlockSpec((B_GROUPS, Wg, Wg), lambda i: (0, 0, 0)),
                  vec],
        out_specs=pl.BlockSpec((tr, 2 * W), lambda i: (i, 0)), out_shape=_S((T, 2 * W), _MM),
        compiler_params=_params(("parallel",)),
    )(h, h, lng, lnb, ws, bias, wp, sc)


def _ev_mid_bwd(h, dy, lng, lnb, ws, bias, wp, sc, name):
    _, T, W = h.shape
    tr = _tile(T, 256)
    H = W // A_HEAD
    Wg = W // B_GROUPS
    nb = tr // POOL_HALO
    last_blk = T // POOL_HALO - 1
    n = tr + POOL_HALO

    def body(h_ref, hp_ref, dy_ref, dyn_ref, lng_ref, lnb_ref, ws_ref, bias_ref, wp_ref, sc_ref,
             dh_ref, dws_ref, dbias_ref, dlng_ref, dlnb_ref, dwp_ref, dsc_ref):
        i = pl.program_id(0)

        @pl.when(i == 0)
        def _():
            for r in (dws_ref, dbias_ref, dlng_ref, dlnb_ref, dwp_ref, dsc_ref):
                r[...] = jnp.zeros_like(r)

        u, gu, gv, rstd, vhat, vb, s_chunks, X = _ev_common(h_ref, hp_ref, lng_ref, lnb_ref, ws_ref, bias_ref, i, tr, W)
        rr = lax.broadcasted_iota(jnp.int32, (A_CHUNK, A_CHUNK), 0)
        cc = lax.broadcasted_iota(jnp.int32, (A_CHUNK, A_CHUNK), 1)
        tril = rr >= cc
        du_chunks, dvln_chunks = [], []
        for c, s in enumerate(s_chunks):
            r0 = c * A_CHUNK
            dya = dy_ref[r0:r0 + A_CHUNK, 0:W]
            du_chunks.append(dya * s)
            ds = dya * u[r0:r0 + A_CHUNK]
            dbias_ref[...] += ds
            dsb = ds.astype(_MM)
            heads = []
            for hd in range(H):
                cols = slice(hd * A_HEAD, (hd + 1) * A_HEAD)
                dws_ref[hd] += jnp.where(tril, _dot(dsb[:, cols], vb[r0:r0 + A_CHUNK, cols], _NT), 0.0)
                heads.append(_dot(ws_ref[hd], dsb[:, cols], _TN))
            dvln_chunks.append(jnp.concatenate(heads, axis=1))
        du = jnp.concatenate(du_chunks, axis=0)
        dvln = jnp.concatenate(dvln_chunks, axis=0)
        dlng_ref[...] += jnp.sum(dvln * vhat, axis=0, keepdims=True)
        dlnb_ref[...] += jnp.sum(dvln, axis=0, keepdims=True)
        dxh = dvln * lng_ref[...]
        dv = rstd * (dxh - jnp.mean(dxh, axis=-1, keepdims=True) - vhat * jnp.mean(dxh * vhat, axis=-1, keepdims=True))
        dh_ref[:, 0:W] = (du * gu).astype(dh_ref.dtype)
        dh_ref[:, W:2 * W] = (dv * gv).astype(dh_ref.dtype)

        dyb = dy_ref[:, W:2 * W]
        dyb_full = jnp.concatenate([dyb, jnp.where(i == pl.num_programs(0) - 1, 0.0, dyn_ref[...])], axis=0)
        for g in range(B_GROUPS):
            cols = slice(g * Wg, (g + 1) * Wg)
            pb = _pool_fwd(X, g, Wg, i, tr).astype(_MM)
            ypre = _dot(pb, wp_ref[g])
            dsc_ref[:, cols] += jnp.sum(dyb[:, cols] * ypre, axis=0, keepdims=True)
            dyp = (dyb_full[:, cols] * sc_ref[:, cols]).astype(_MM)
            dwp_ref[g] += _dot(pb, dyp[0:tr], _TN)
            dp = _dot(dyp, wp_ref[g], _NT)
            s = dp * _pool_inv_count(i, tr, n, 2 ** (g + 1))
            for k in range(g + 1):
                s = s + pltpu.roll(s, n - 2 ** k, 0)
            dh_ref[:, 2 * W + g * Wg:2 * W + (g + 1) * Wg] = (s[0:tr] - dp[0:tr]).astype(dh_ref.dtype)

    vec = pl.BlockSpec((1, W), lambda i: (0, 0))
    ws_spec = pl.BlockSpec((H, A_CHUNK, A_CHUNK), lambda i: (0, 0, 0))
    bias_spec = pl.BlockSpec((A_CHUNK, W), lambda i: (0, 0))
    wp_spec = pl.BlockSpec((B_GROUPS, Wg, Wg), lambda i: (0, 0, 0))
    return pl.pallas_call(
        body, name=name, grid=(T // tr,),
        in_specs=[pl.BlockSpec((3, tr, W), lambda i: (0, i, 0)),
                  pl.BlockSpec((None, POOL_HALO, W), lambda i: (2, jnp.maximum(i * nb - 1, 0), 0)),
                  pl.BlockSpec((tr, 2 * W), lambda i: (i, 0)),
                  pl.BlockSpec((POOL_HALO, W), lambda i: (jnp.minimum((i + 1) * nb, last_blk), 1)),
                  vec, vec, ws_spec, bias_spec, wp_spec, vec],
        out_specs=[pl.BlockSpec((tr, 3 * W), lambda i: (i, 0)), ws_spec, bias_spec, vec, vec, wp_spec, vec],
        out_shape=[_S((T, 3 * W), _MM), _S((H, A_CHUNK, A_CHUNK), _F32), _S((A_CHUNK, W), _F32), _S((1, W), _F32),
                   _S((1, W), _F32), _S((B_GROUPS, Wg, Wg), _F32), _S((1, W), _F32)],
        compiler_params=_params(("arbitrary",)),
    )(h, h, dy, dy, lng, lnb, ws, bias, wp, sc)


def _chunk_cumsum(x, rin):
    s = 1
    while s < C_CHUNK:
        x = x + jnp.where(rin >= s, pltpu.roll(x, s, 0), 0.0)
        s *= 2
    return x


def _chunk_revcumsum(x, rin):
    n = x.shape[0]
    s = 1
    while s < C_CHUNK:
        x = x + jnp.where(rin + s < C_CHUNK, pltpu.roll(x, n - s, 0), 0.0)
        s *= 2
    return x


def _hgrn_gates(q, fl, lb, tr, tc):
    nch = tr // C_CHUNK
    sq = _sigmoid(q)
    sf = _sigmoid_rel(fl)
    f = lb + (1.0 - lb) * sf
    logf = jnp.log(f)
    rin = _row_index(tr) % C_CHUNK
    b = _chunk_cumsum(logf, rin)
    tot3 = jnp.sum(logf.reshape(nch, C_CHUNK, tc), axis=1, keepdims=True)
    eb = jnp.exp(b)
    enb = jnp.exp(-b)
    ekb = jnp.exp(tot3 - b.reshape(nch, C_CHUNK, tc)).reshape(tr, tc)
    return sq, sf, f, rin, tot3, eb, enb, ekb


def _hgrn_prep_fwd(h, lb, name):
    _, T, D = h.shape
    tr = _tile(T, 512)
    tc = _tile(D, 512)
    nch = tr // C_CHUNK

    def body(q_ref, f_ref, v_ref, lb_ref, qd_ref, kd_ref, ke_ref, vb_ref, dec_ref):
        q = q_ref[...]
        sq, _, f, _, tot3, eb, enb, ekb = _hgrn_gates(q, f_ref[...], lb_ref[...], tr, tc)
        kk = 1.0 - f
        qd_ref[...] = (q * sq * eb).astype(qd_ref.dtype)
        kd_ref[...] = (kk * enb).astype(kd_ref.dtype)
        ke_ref[...] = (kk * ekb).astype(ke_ref.dtype)
        vb_ref[...] = v_ref[...].astype(vb_ref.dtype)
        dec_ref[...] = jnp.exp(tot3).reshape(nch, tc)

    def part(p):
        return pl.BlockSpec((None, tr, tc), lambda i, j: (p, i, j))

    blk = pl.BlockSpec((tr, tc), lambda i, j: (i, j))
    return pl.pallas_call(
        body, name=name, grid=(T // tr, D // tc),
        in_specs=[part(0), part(1), part(2), pl.BlockSpec((1, tc), lambda i, j: (0, j))],
        out_specs=[blk, blk, blk, blk, pl.BlockSpec((nch, tc), lambda i, j: (i, j))],
        out_shape=[_S((T, D), _MM)] * 4 + [_S((T // C_CHUNK, D), _F32)],
        compiler_params=_params(("parallel", "parallel")),
    )(h, h, h, lb)


def _tril_mask():
    rr = lax.broadcasted_iota(jnp.int32, (C_CHUNK, C_CHUNK), 0)
    cc = lax.broadcasted_iota(jnp.int32, (C_CHUNK, C_CHUNK), 1)
    return rr >= cc


def _hgrn_scan_fwd(qd, kd, ke, vb, dec, h, ng, name):
    T, D = qd.shape
    NH = D // C_HEAD
    N = T // C_CHUNK

    def body(qd_ref, kd_ref, ke_ref, vb_ref, dec_ref, g_ref, ng_ref, o_ref, y_ref, st_ref):
        mask = _tril_mask()

        per_trip = math.gcd(N, SCAN_UNROLL)

        def trip(i, St):
            ahead = []
            for u in range(per_trip):
                n = i * per_trip + u
                r = pl.ds(pl.multiple_of(n * C_CHUNK, C_CHUNK), C_CHUNK)
                Qd, Kd, Ke, V = qd_ref[r, :], kd_ref[r, :], ke_ref[r, :], vb_ref[r, :]
                att = jnp.where(mask, _dot(Qd, Kd, _NT), 0.0).astype(_MM)
                ahead.append((n, r, _dot(att, V), _dot(V, Ke, _TN)))
            for n, r, o_intra, update in ahead:
                o_ref[r, :] = o_intra + _dot(qd_ref[r, :], St.astype(_MM), _NT)
                st_ref[n] = St
                St = St * dec_ref[pl.ds(n, 1), :] + update
            return St

        lax.fori_loop(0, N // per_trip, trip, jnp.zeros((C_HEAD, C_HEAD), _F32))
        o = o_ref[...]
        r = lax.rsqrt(jnp.mean(o * o, axis=-1, keepdims=True) + LN_EPS)
        y_ref[...] = (o * r * ng_ref[...] * _sigmoid(g_ref[...])).astype(y_ref.dtype)

    col = pl.BlockSpec((T, C_HEAD), lambda j: (0, j))
    return pl.pallas_call(
        body, name=name, grid=(NH,),
        in_specs=[col, col, col, col, pl.BlockSpec((N, C_HEAD), lambda j: (0, j)),
                  pl.BlockSpec((None, T, C_HEAD), lambda j: (3, 0, j)), pl.BlockSpec((1, C_HEAD), lambda j: (0, j))],
        out_specs=[col, col, pl.BlockSpec((None, N, C_HEAD, C_HEAD), lambda j: (j, 0, 0, 0))],
        out_shape=[_S((T, D), _F32), _S((T, D), _MM), _S((NH, N, C_HEAD, C_HEAD), _F32)],
        compiler_params=_params(("parallel",)),
    )(qd, kd, ke, vb, dec, h, ng)


def _hgrn_scan_bwd(qd, kd, ke, vb, dec, st, o, h, ng, dy, name):
    T, D = qd.shape
    NH = D // C_HEAD
    N = T // C_CHUNK

    def body(qd_ref, kd_ref, ke_ref, vb_ref, dec_ref, st_ref, o_ref, g_ref, ng_ref, dy_ref,
             dqd_ref, dkd_ref, dke_ref, dv_ref, dgate_ref, ddec_ref, dng_ref, do_s):
        o = o_ref[...]
        r = lax.rsqrt(jnp.mean(o * o, axis=-1, keepdims=True) + LN_EPS)
        oh = o * r
        gn = ng_ref[...]
        sg = _sigmoid(g_ref[...])
        d = dy_ref[...]
        dyn = d * sg
        dgate_ref[...] = (d * oh * gn * sg * (1.0 - sg)).astype(dgate_ref.dtype)
        dng_ref[...] = jnp.sum(dyn * oh, axis=0, keepdims=True)
        doh = dyn * gn
        do_s[...] = (r * (doh - oh * jnp.mean(doh * oh, axis=-1, keepdims=True))).astype(do_s.dtype)
        mask = _tril_mask()

        per_trip = math.gcd(N, SCAN_UNROLL)

        def trip(i, dSt):
            ahead = []
            for u in range(per_trip):
                n = N - 1 - (i * per_trip + u)
                rws = pl.ds(pl.multiple_of(n * C_CHUNK, C_CHUNK), C_CHUNK)
                Qd, Kd, V, dO = qd_ref[rws, :], kd_ref[rws, :], vb_ref[rws, :], do_s[rws, :]
                att = jnp.where(mask, _dot(Qd, Kd, _NT), 0.0).astype(_MM)
                dA = jnp.where(mask, _dot(dO, V, _NT), 0.0).astype(_MM)
                dqd_ref[rws, :] = _dot(dA, Kd) + _dot(dO, st_ref[n].astype(_MM))
                dkd_ref[rws, :] = _dot(dA, Qd, _TN)
                ahead.append((n, rws, _dot(att, dO, _TN), _dot(dO, Qd, _TN)))
            for n, rws, dv_intra, d_state in ahead:
                dStb = dSt.astype(_MM)
                dv_ref[rws, :] = (dv_intra + _dot(ke_ref[rws, :], dStb, _NT)).astype(dv_ref.dtype)
                dke_ref[rws, :] = _dot(vb_ref[rws, :], dStb)
                ddec_ref[pl.ds(n, 1), :] = jnp.sum(dSt * st_ref[n], axis=0, keepdims=True)
                dSt = dSt * dec_ref[pl.ds(n, 1), :] + d_state
            return dSt

        lax.fori_loop(0, N // per_trip, trip, jnp.zeros((C_HEAD, C_HEAD), _F32))

    col = pl.BlockSpec((T, C_HEAD), lambda j: (0, j))
    chk = pl.BlockSpec((N, C_HEAD), lambda j: (0, j))
    one = pl.BlockSpec((1, C_HEAD), lambda j: (0, j))
    return pl.pallas_call(
        body, name=name, grid=(NH,),
        in_specs=[col, col, col, col, chk, pl.BlockSpec((None, N, C_HEAD, C_HEAD), lambda j: (j, 0, 0, 0)), col,
                  pl.BlockSpec((None, T, C_HEAD), lambda j: (3, 0, j)), one, col],
        out_specs=[col, col, col, col, col, chk, one],
        out_shape=[_S((T, D), _F32)] * 3 + [_S((T, D), _MM)] * 2 + [_S((N, D), _F32), _S((1, D), _F32)],
        scratch_shapes=[pltpu.VMEM((T, C_HEAD), _MM)],
        compiler_params=_params(("parallel",)),
    )(qd, kd, ke, vb, dec, st, o, h, ng, dy)


def _hgrn_prep_bwd(h, lb, dqd, dkd, dke, dv, dgate, ddec, name):
    _, T, D = h.shape
    tr = _tile(T, 512)
    tc = _tile(D, 256)
    nch = tr // C_CHUNK

    def body(q_ref, f_ref, lb_ref, dqd_ref, dkd_ref, dke_ref, dv_ref, dgate_ref, ddec_ref, dh_ref, dlb_ref):
        @pl.when(pl.program_id(1) == 0)
        def _():
            dlb_ref[...] = jnp.zeros_like(dlb_ref)

        q = q_ref[...]
        lb = lb_ref[...]
        sq, sf, f, rin, tot3, eb, enb, ekb = _hgrn_gates(q, f_ref[...], lb, tr, tc)
        kk = 1.0 - f
        dQd, dKd, dKe = dqd_ref[...], dkd_ref[...], dke_ref[...]
        tq = dQd * eb
        tkd = dKd * enb
        tke = dKe * ekb
        ke_term = tke * kk
        db = tq * (q * sq) - tkd * kk - ke_term
        dtot3 = (jnp.sum(ke_term.reshape(nch, C_CHUNK, tc), axis=1, keepdims=True)
                 + (ddec_ref[...] * jnp.exp(tot3).reshape(nch, tc)).reshape(nch, 1, tc))
        dlogf = (_chunk_revcumsum(db, rin).reshape(nch, C_CHUNK, tc) + dtot3).reshape(tr, tc)
        df = dlogf / f - (tkd + tke)
        dh_ref[0] = (tq * sq * (1.0 + q * (1.0 - sq))).astype(dh_ref.dtype)
        dh_ref[1] = (df * (1.0 - lb) * sf * (1.0 - sf)).astype(dh_ref.dtype)
        dh_ref[2] = dv_ref[...]
        dh_ref[3] = dgate_ref[...]
        dlb_ref[...] += jnp.sum(df * (1.0 - sf), axis=0, keepdims=True)

    def part(p):
        return pl.BlockSpec((None, tr, tc), lambda j, i: (p, i, j))

    blk = pl.BlockSpec((tr, tc), lambda j, i: (i, j))
    vec = pl.BlockSpec((1, tc), lambda j, i: (0, j))
    return pl.pallas_call(
        body, name=name, grid=(D // tc, T // tr),
        in_specs=[part(0), part(1), vec, blk, blk, blk, blk, blk, pl.BlockSpec((nch, tc), lambda j, i: (i, j))],
        out_specs=[pl.BlockSpec((4, tr, tc), lambda j, i: (0, i, j)), vec],
        out_shape=[_S((4, T, D), _MM), _S((1, D), _F32)],
        compiler_params=_params(("parallel", "arbitrary")),
    )(h, h, lb, dqd, dkd, dke, dv, dgate, ddec)


def _sum_in_device_order(me1, own, land, name):
    R, C = own.shape
    tr = _tile(R, 256)

    def body(me_ref, own_ref, land_ref, o_ref):
        me = me_ref[0]
        g = None
        for j in range(N_DEV):
            slot = jnp.maximum(jnp.bitwise_xor(me, j) - 1, 0)
            p = jnp.where(me == j, own_ref[...], land_ref[slot])
            g = p if g is None else g + p
        o_ref[...] = g

    return pl.pallas_call(
        body, name=name,
        grid_spec=pltpu.PrefetchScalarGridSpec(
            num_scalar_prefetch=1, grid=(R // tr,),
            in_specs=[pl.BlockSpec((tr, C), lambda i, me: (i, 0)), pl.BlockSpec((N_DEV - 1, tr, C), lambda i, me: (0, i, 0))],
            out_specs=pl.BlockSpec((tr, C), lambda i, me: (i, 0))),
        out_shape=_S((R, C), _F32), compiler_params=_params(("parallel",)),
    )(me1, own, land)


def _adamw(parts, w, m, v, name):
    P, R, C = parts.shape
    tr = _tile(R, 128) if R % LANE == 0 else R

    def body(p_ref, w_ref, m_ref, v_ref, g_ref, d_ref, nm_ref, nv_ref):
        g = p_ref[0].astype(_F32)
        for s in range(1, P):
            g = g + p_ref[s].astype(_F32)
        nm = ADAM_B1 * m_ref[...] + (1.0 - ADAM_B1) * g
        nv = ADAM_B2 * v_ref[...] + (1.0 - ADAM_B2) * (g * g)
        m_hat = nm / (1.0 - ADAM_B1 ** ADAM_STEP)
        v_hat = nv / (1.0 - ADAM_B2 ** ADAM_STEP)
        g_ref[...] = g
        d_ref[...] = -ADAM_LR * (m_hat / (jnp.sqrt(v_hat) + ADAM_EPS) + ADAM_WD * w_ref[...])
        nm_ref[...] = nm
        nv_ref[...] = nv

    blk = pl.BlockSpec((tr, C), lambda i: (i, 0))
    return pl.pallas_call(
        body, name=name, grid=(R // tr,), in_specs=[pl.BlockSpec((P, tr, C), lambda i: (0, i, 0)), blk, blk, blk],
        out_specs=[blk] * 4, out_shape=[_S((R, C), _F32)] * 4, compiler_params=_params(("parallel",)),
    )(parts, w, m, v)


def _exchange(name, srcs, out_shapes, jobs, deps=()):
    ns, nj = len(srcs), len(jobs)

    nd = len(deps)

    def body(*refs):
        ins, outs = refs[:ns], refs[ns + nd:ns + nd + len(out_shapes)]
        send_sems, recv_sems, local_sems = refs[-3:]
        x, y, c = lax.axis_index("x"), lax.axis_index("y"), lax.axis_index("c")
        me = 4 * x + 2 * y + c
        local = []
        for ji, (si, src_fn, di, dst_fn) in enumerate(jobs):
            cp = pltpu.make_async_copy(src_fn(ins[si], me, me), dst_fn(outs[di], me), local_sems.at[ji])
            cp.start()
            local.append(cp)
        remote = []
        for k in range(1, N_DEV):
            px, py, pc = (x + (k >> 2)) % 2, (y + ((k >> 1) & 1)) % 2, (c + (k & 1)) % 2
            to = 4 * px + 2 * py + pc
            for ji, (si, src_fn, di, dst_fn) in enumerate(jobs):
                sem = (k - 1) * nj + ji
                cp = pltpu.make_async_remote_copy(
                    src_ref=src_fn(ins[si], me, to), dst_ref=dst_fn(outs[di], me),
                    send_sem=send_sems.at[sem], recv_sem=recv_sems.at[sem],
                    device_id=(px, py, pc), device_id_type=pl.DeviceIdType.MESH)
                cp.start()
                remote.append(cp)
        for cp in remote:
            cp.wait_recv()
        for cp in remote:
            cp.wait_send()
        for cp in local:
            cp.wait()

    hbm = pl.BlockSpec(memory_space=pltpu.HBM)
    return pl.pallas_call(
        body, name=name, in_specs=[hbm] * ns + [_ANY] * nd, out_specs=[hbm] * len(out_shapes), out_shape=list(out_shapes),
        scratch_shapes=[pltpu.SemaphoreType.DMA(((N_DEV - 1) * nj,)), pltpu.SemaphoreType.DMA(((N_DEV - 1) * nj,)),
                        pltpu.SemaphoreType.DMA((nj,))],
    )(*srcs, *deps)


def _whole(ref, me, to):
    return ref


def _slot_job(i, o):
    def dst(ref, me):
        return ref.at[me]
    return (i, _whole, o, dst)


_HBM = pl.BlockSpec(memory_space=pltpu.HBM)
_SEM = pl.BlockSpec(memory_space=pltpu.SEMAPHORE)
_ANY = pl.BlockSpec(memory_space=pl.ANY)
_N_PEER = N_DEV - 1


def _split_params():
    return pltpu.CompilerParams(has_side_effects=pltpu.SideEffectType.DATAFLOW_SIDE_EFFECTING)


def _blk(ref, axis, n, idx):
    if axis is None:
        return ref
    return ref.at[tuple([slice(None)] * axis + [pl.ds(pl.multiple_of(idx * n, n), n)])]


def _peer(k):
    x, y, c = lax.axis_index("x"), lax.axis_index("y"), lax.axis_index("c")
    px, py, pc = (x + (k >> 2)) % 2, (y + ((k >> 1) & 1)) % 2, (c + (k & 1)) % 2
    return (px, py, pc), 4 * px + 2 * py + pc, 4 * x + 2 * y + c


def _row_tile(rows, pref):
    best = None
    for d in range(16, min(rows, pref) + 1, 16):
        if rows % d == 0:
            best = d
    return best if best is not None else rows


def _place(w, me1, axis, name, layer=None, deps=()):
    R, C = w.shape[-2:]
    tr = _row_tile(R, 512)
    nb = R // tr
    lead = () if layer is None else (None,)
    pre = () if layer is None else (layer,)

    def body(me_ref, w_ref, *rest):
        rest[-1][...] = w_ref[...].astype(rest[-1].dtype)

    if axis == 1:
        out_spec = pl.BlockSpec((tr, C), lambda i, me: (i, me[0]))
        out_shape = _S((R, N_DEV * C), _MM)
    else:
        out_spec = pl.BlockSpec((tr, C), lambda i, me: (me[0] * nb + i, 0))
        out_shape = _S((N_DEV * R, C), _MM)
    return pl.pallas_call(
        body, name=name,
        grid_spec=pltpu.PrefetchScalarGridSpec(
            num_scalar_prefetch=1, grid=(nb,),
            in_specs=[pl.BlockSpec(lead + (tr, C), lambda i, me: pre + (i, 0))] + [_ANY] * len(deps), out_specs=out_spec),
        out_shape=out_shape, compiler_params=_params(("parallel",)),
    )(me1, w, *deps)


_SIBLING = 1
_CHIPS = (2, 4, 6)
_VMEM_TOKEN = pl.BlockSpec(memory_space=pltpu.VMEM)


def _remote(ref_blk, send_sem, recv_sem, dev):
    return pltpu.make_async_remote_copy(src_ref=ref_blk, dst_ref=ref_blk, send_sem=send_sem, recv_sem=recv_sem,
                                        device_id=dev, device_id_type=pl.DeviceIdType.MESH)


def _gather_start(name, full, axis, n):
    def body(f_ref, send, recv, f_out, token):
        for i, k in enumerate((_SIBLING,) + _CHIPS):
            dev, _, me = _peer(k)
            _remote(_blk(f_ref, axis, n, me), send.at[i], recv.at[i], dev).start()
        token[...] = jnp.zeros_like(token)

    return pl.pallas_call(
        body, name=name,
        out_shape=(pltpu.SemaphoreType.DMA((4,)), pltpu.SemaphoreType.DMA((4,)), pltpu.HBM(full.shape, full.dtype),
                   _S((8, LANE), _F32)),
        in_specs=(_HBM,), out_specs=(_SEM, _SEM, _HBM, _VMEM_TOKEN),
        input_output_aliases={0: 2}, compiler_params=_split_params(),
    )(pltpu.with_memory_space_constraint(full, pltpu.HBM))


def _gather_forward(name, full, axis, n, recv, after):
    after = tuple(after) if isinstance(after, (tuple, list)) else (after,)

    def body(f_ref, recv_r, *rest):
        send2, recv2, f_out, token = rest[-4:]
        sib, _, _ = _peer(_SIBLING)
        for i, k in enumerate(_CHIPS):
            dev, frm, _ = _peer(k)
            blk = _blk(f_ref, axis, n, frm)
            _remote(blk, send2.at[i], recv_r.at[1 + i], dev).wait_recv()
            _remote(blk, send2.at[i], recv2.at[i], sib).start()
        token[...] = jnp.zeros_like(token)

    return pl.pallas_call(
        body, name=name,
        out_shape=(pltpu.SemaphoreType.DMA((3,)), pltpu.SemaphoreType.DMA((3,)), pltpu.HBM(full.shape, full.dtype),
                   _S((8, LANE), _F32)),
        in_specs=(_HBM, _SEM) + (_ANY,) * len(after), out_specs=(_SEM, _SEM, _HBM, _VMEM_TOKEN),
        input_output_aliases={0: 2}, compiler_params=_split_params(),
    )(full, recv, *after)


def _gather_wait(name, full, axis, n, send, recv, send2, recv2, after):
    def body(f_ref, send_r, recv_r, send2_r, recv2_r, after_ref, f_out):
        sib, _, me = _peer(_SIBLING)
        blk = _blk(f_ref, axis, n, me)
        for i in range(4):
            _remote(blk, send_r.at[i], recv_r.at[0], sib).wait_send()
        _remote(blk, send_r.at[0], recv_r.at[0], sib).wait_recv()
        for i in range(3):
            cp = _remote(blk, send2_r.at[i], recv2_r.at[i], sib)
            cp.wait_send()
            cp.wait_recv()

    return pl.pallas_call(
        body, name=name, out_shape=pltpu.HBM(full.shape, full.dtype),
        in_specs=(_HBM, _SEM, _SEM, _SEM, _SEM, _ANY), out_specs=_HBM,
        input_output_aliases={0: 0}, compiler_params=_split_params(),
    )(full, send, recv, send2, recv2, after)


def _scatter_start(name, dw, axis, n):
    shard = tuple(n if a == axis else d for a, d in enumerate(dw.shape))
    land = lax.empty((_N_PEER,) + shard, dw.dtype)

    def body(dw_ref, land_ref, send, recv, dw_out, land_out, token):
        for k in range(1, N_DEV):
            dev, to, _ = _peer(k)
            pltpu.make_async_remote_copy(
                src_ref=_blk(dw_ref, axis, n, to), dst_ref=land_ref.at[k - 1], send_sem=send.at[k - 1],
                recv_sem=recv.at[k - 1], device_id=dev, device_id_type=pl.DeviceIdType.MESH).start()
        token[...] = jnp.zeros_like(token)

    return pl.pallas_call(
        body, name=name,
        out_shape=(pltpu.SemaphoreType.DMA((_N_PEER,)), pltpu.SemaphoreType.DMA((_N_PEER,)),
                   pltpu.HBM(dw.shape, dw.dtype), pltpu.HBM(land.shape, land.dtype), _S((8, LANE), _F32)),
        in_specs=(_HBM, _HBM), out_specs=(_SEM, _SEM, _HBM, _HBM, pl.BlockSpec(memory_space=pltpu.VMEM)),
        input_output_aliases={0: 2, 1: 3}, compiler_params=_split_params(),
    )(pltpu.with_memory_space_constraint(dw, pltpu.HBM), pltpu.with_memory_space_constraint(land, pltpu.HBM))


def _scatter_wait(name, items, after):
    ne = len(items)
    after = tuple(after) if isinstance(after, (tuple, list)) else (after,)

    def body(*refs):
        for e, (_, _, _, _, axis, n) in enumerate(items):
            dw_ref, land_ref, send_r, recv_r = refs[4 * e:4 * e + 4]
            for k in range(1, N_DEV):
                dev, to, _ = _peer(k)
                cp = pltpu.make_async_remote_copy(
                    src_ref=_blk(dw_ref, axis, n, to), dst_ref=land_ref.at[k - 1], send_sem=send_r.at[k - 1],
                    recv_sem=recv_r.at[k - 1], device_id=dev, device_id_type=pl.DeviceIdType.MESH)
                cp.wait_send()
                cp.wait_recv()

    args, out_shape = [], []
    for dw, land, send, recv, _, _ in items:
        args += [dw, land, send, recv]
        out_shape += [pltpu.HBM(dw.shape, dw.dtype), pltpu.HBM(land.shape, land.dtype)]
    res = pl.pallas_call(
        body, name=name, out_shape=tuple(out_shape),
        in_specs=(_HBM, _HBM, _SEM, _SEM) * ne + (_ANY,) * len(after), out_specs=(_HBM,) * (2 * ne),
        input_output_aliases={4 * e + j: 2 * e + j for e in range(ne) for j in range(2)},
        compiler_params=_split_params(),
    )(*args, *after)
    return [(res[2 * e], res[2 * e + 1]) for e in range(ne)]


def _adamw_big(me1, dw, land, w, m, v, axis, n, name, layer=None, into=None):
    R, C = land.shape[1:]
    tr = _row_tile(R, 128)
    nb = R // tr
    lead = () if layer is None else (None,)
    pre = () if layer is None else (layer,)

    def body(me_ref, own_ref, land_ref, w_ref, m_ref, v_ref, *rest):
        g_ref, d_ref, nm_ref, nv_ref = rest[-4:]
        g = own_ref[...].astype(_F32)
        for s in range(_N_PEER):
            g = g + land_ref[s].astype(_F32)
        nm = ADAM_B1 * m_ref[...] + (1.0 - ADAM_B1) * g
        nv = ADAM_B2 * v_ref[...] + (1.0 - ADAM_B2) * (g * g)
        m_hat = nm / (1.0 - ADAM_B1 ** ADAM_STEP)
        v_hat = nv / (1.0 - ADAM_B2 ** ADAM_STEP)
        g_ref[...] = g
        d_ref[...] = -ADAM_LR * (m_hat / (jnp.sqrt(v_hat) + ADAM_EPS) + ADAM_WD * w_ref[...])
        nm_ref[...] = nm
        nv_ref[...] = nv

    if axis == 1:
        own_spec = pl.BlockSpec((tr, C), lambda i, me: (i, me[0]))
    else:
        own_spec = pl.BlockSpec((tr, C), lambda i, me: (me[0] * nb + i, 0))
    blk = pl.BlockSpec(lead + (tr, C), lambda i, me: pre + (i, 0))
    in_specs = [own_spec, pl.BlockSpec((_N_PEER, tr, C), lambda i, me: (0, i, 0)), blk, blk, blk]
    args = [me1, dw, land, w, m, v]
    aliases = {}
    if into is not None:
        in_specs += [_ANY] * 4
        aliases = {6 + j: j for j in range(4)}
        args += list(into)
    return pl.pallas_call(
        body, name=name,
        grid_spec=pltpu.PrefetchScalarGridSpec(num_scalar_prefetch=1, grid=(nb,), in_specs=in_specs, out_specs=[blk] * 4),
        out_shape=[_S(w.shape, _F32)] * 4, input_output_aliases=aliases, compiler_params=_params(("parallel",)),
    )(*args)


def _pack(arrs):
    flat = jnp.concatenate([a.reshape(-1).astype(_F32) for a in arrs])
    pad = (-flat.shape[0]) % (LANE * LANE)
    return jnp.pad(flat, (0, pad)).reshape(-1, LANE)


def _unpack(mat, shapes):
    flat = mat.reshape(-1)
    out, off = [], 0
    for s in shapes:
        n = 1
        for d in s:
            n *= d
        out.append(flat[off:off + n].reshape(s))
        off += n
    return out


def _lb_of(lb_param):
    lb_all = jnp.cumsum(jax.nn.softmax(lb_param.astype(_F32), axis=0), axis=0)
    return (lb_all - lb_all[0])[1:2]


def kernel(x, ev_w_in, ev_ln_v_g, ev_ln_v_b, ev_w_s, ev_b_s, ev_w_pool, ev_pool_scale, ev_w_out, od_w_in, od_norm_g, od_w_out, lb_param, ffn_w_up, ffn_conv_w, ffn_conv_b, ffn_w_down, ln1_g, ln1_b, ln2_g, ln2_b, loss_target, m_ev_w_in, m_ev_ln_v_g, m_ev_ln_v_b, m_ev_w_s, m_ev_b_s, m_ev_w_pool, m_ev_pool_scale, m_ev_w_out, m_od_w_in, m_od_norm_g, m_od_w_out, m_lb_param, m_ffn_w_up, m_ffn_conv_w, m_ffn_conv_b, m_ffn_w_down, m_ln1_g, m_ln1_b, m_ln2_g, m_ln2_b, v_ev_w_in, v_ev_ln_v_g, v_ev_ln_v_b, v_ev_w_s, v_ev_b_s, v_ev_w_pool, v_ev_pool_scale, v_ev_w_out, v_od_w_in, v_od_norm_g, v_od_w_out, v_lb_param, v_ffn_w_up, v_ffn_conv_w, v_ffn_conv_b, v_ffn_w_down, v_ln1_g, v_ln1_b, v_ln2_g, v_ln2_b):
    me = 4 * lax.axis_index("x") + 2 * lax.axis_index("y") + lax.axis_index("c")
    T, D = x.shape[1], x.shape[2]
    W = ev_ln_v_g.shape[1]
    H = W // A_HEAD
    Wg = W // B_GROUPS
    F2 = ffn_conv_b.shape[1]
    F = F2 // 2
    n_in0, n_out0 = ev_w_in.shape[2], ev_w_out.shape[1]
    n_in1, n_out1 = od_w_in.shape[2], od_w_out.shape[1]
    n_up, n_dn = ffn_w_up.shape[2], ffn_w_down.shape[1]
    n_pool, n_ng, n_cw = ev_w_pool.shape[2], od_norm_g.shape[1], ffn_conv_w.shape[2]

    small_shards = [od_norm_g, ffn_conv_w, ev_w_pool]
    small_pack = _pack(small_shards)
    small_all = _exchange("gather_small_params", [small_pack], [_S((N_DEV,) + small_pack.shape, _F32)], [_slot_job(0, 0)])[0]

    me1 = me.astype(jnp.int32).reshape(1)
    weights = [
        ("w_in0", ev_w_in[0], None, 1, n_in0), ("w_out0", ev_w_out[0], None, 0, n_out0),
        ("w_up0", ffn_w_up, 0, 1, n_up), ("w_dn0", ffn_w_down, 0, 0, n_dn),
        ("w_in1", od_w_in[0], None, 1, n_in1), ("w_out1", od_w_out[0], None, 0, n_out1),
        ("w_up1", ffn_w_up, 1, 1, n_up), ("w_dn1", ffn_w_down, 1, 0, n_dn),
    ]
    started, tokens = {}, [small_all]
    for key, w, layer, axis, n in weights:
        full = _place(w, me1, axis, "place_" + key, layer, deps=tokens)
        send, recv, full, token = _gather_start("gather_start_" + key, full, axis, n)
        started[key] = (full, axis, n, send, recv)
        tokens = [token]

    def pass_on(key, after):
        full, axis, n, send, recv = started[key]
        send2, recv2, full, token = _gather_forward("gather_forward_" + key, full, axis, n, recv, after)
        started[key] = (full, axis, n, send, recv, send2, recv2)
        return token

    def gathered(key, after):
        return _gather_wait("gather_wait_" + key, *started[key], after)

    ng_parts, cw_parts, wp_parts = [], [], []
    for j in range(N_DEV):
        a, b, c = _unpack(small_all[j], [s.shape for s in small_shards])
        ng_parts.append(a)
        cw_parts.append(b)
        wp_parts.append(c)
    norm_g = jnp.concatenate(ng_parts, axis=1)
    conv_w = jnp.concatenate(cw_parts, axis=2)
    w_pool = jnp.concatenate(wp_parts, axis=2)[0]
    cw_l = [conv_w[l].reshape(3, 2, F).transpose(1, 0, 2) for l in range(DEPTH)]
    cb_l = [ffn_conv_b[l].reshape(2, 1, F) for l in range(DEPTH)]
    ws_tril = jnp.tril(ev_w_s[0]).astype(_MM)
    bias = jnp.repeat(ev_b_s[0].T, A_HEAD, axis=1)
    wp_b = w_pool.astype(_MM)
    lb, lb_vjp = jax.vjp(_lb_of, lb_param)

    small_names = ["ev_ln_v_g", "ev_ln_v_b", "ev_w_s", "ev_b_s", "ev_w_pool", "ev_pool_scale", "od_norm_g", "lb_param",
                   "ffn_conv_w", "ffn_conv_b", "ln1_g", "ln1_b", "ln2_g", "ln2_b"]
    given = dict(ev_ln_v_g=(ev_ln_v_g, m_ev_ln_v_g, v_ev_ln_v_g), ev_ln_v_b=(ev_ln_v_b, m_ev_ln_v_b, v_ev_ln_v_b),
                 ev_w_s=(ev_w_s, m_ev_w_s, v_ev_w_s), ev_b_s=(ev_b_s, m_ev_b_s, v_ev_b_s),
                 ev_w_pool=(ev_w_pool, m_ev_w_pool, v_ev_w_pool),
                 ev_pool_scale=(ev_pool_scale, m_ev_pool_scale, v_ev_pool_scale),
                 od_norm_g=(od_norm_g, m_od_norm_g, v_od_norm_g), lb_param=(lb_param, m_lb_param, v_lb_param),
                 ffn_conv_w=(ffn_conv_w, m_ffn_conv_w, v_ffn_conv_w), ffn_conv_b=(ffn_conv_b, m_ffn_conv_b, v_ffn_conv_b),
                 ln1_g=(ln1_g, m_ln1_g, v_ln1_g), ln1_b=(ln1_b, m_ln1_b, v_ln1_b), ln2_g=(ln2_g, m_ln2_g, v_ln2_g),
                 ln2_b=(ln2_b, m_ln2_b, v_ln2_b))
    shard_axis = dict(ev_w_pool=2, od_norm_g=1, ffn_conv_w=2)
    rep_names = [n for n in small_names if n not in shard_axis]
    shd_names = [n for n in small_names if n in shard_axis]
    small_packs = [_pack([given[n][j] for n in small_names]) for j in range(3)]

    x2 = x[0]
    xb = _cast(x2, _MM, "cast_x", deps=[pass_on("w_in0", tokens[0])])
    w_in0 = gathered("w_in0", xb)
    h0 = _mm(xb, w_in0, "nn", _F32, "ev_in", out_parts=3)
    tie = pass_on("w_out0", h0)
    yab = _ev_mid_fwd(h0, ev_ln_v_g + tie[0, 0], ev_ln_v_b, ws_tril, bias, wp_b, ev_pool_scale, "ev_mid_fwd")
    w_out0 = gathered("w_out0", yab)
    z1 = _mm(yab, w_out0, "nn", _F32, "ev_out", add=x2, add_scale=ALPHA)
    tie = pass_on("w_up0", (z1, *small_packs))
    x1, x1b = _ln_fwd(z1, ln1_g[0:1] + tie[0, 0], ln1_b[0:1], "ln1_0")
    w_up0 = gathered("w_up0", x1b)
    hf0 = _mm(x1b, w_up0, "nn", _F32, "ffn_up", out_parts=2)
    tie = pass_on("w_dn0", hf0)
    act0, hc0 = _ffn_mid_fwd(hf0, cw_l[0], cb_l[0] + tie[0, 0], "ffn_mid_fwd")
    w_dn0 = gathered("w_dn0", act0)
    z2 = _mm(act0, w_dn0, "nn", _F32, "ffn_down", add=x1, add_scale=ALPHA)
    tie = pass_on("w_in1", z2)
    x2_, x2b = _ln_fwd(z2, ln2_g[0:1] + tie[0, 0], ln2_b[0:1], "ln2_0")
    w_in1 = gathered("w_in1", x2b)
    h1 = _mm(x2b, w_in1, "nn", _F32, "od_in", out_parts=4)
    qd, kd, ke, vb, dec = _hgrn_prep_fwd(h1, lb, "hgrn_prep_fwd")
    tie = pass_on("w_out1", qd)
    o, yo, st = _hgrn_scan_fwd(qd, kd, ke, vb, dec, h1, norm_g + tie[0, 0], "hgrn_scan_fwd")
    w_out1 = gathered("w_out1", yo)
    tie = pass_on("w_up1", yo)
    z3, x3, x3b = _mm(yo, w_out1, "nn", _F32, "od_out", add=x2_, add_scale=ALPHA, ln=(ln1_g[1:2], ln1_b[1:2]), deps=[tie])
    w_up1 = gathered("w_up1", x3b)
    hf1 = _mm(x3b, w_up1, "nn", _F32, "ffn_up", out_parts=2)
    tie = pass_on("w_dn1", hf1)
    act1, hc1 = _ffn_mid_fwd(hf1, cw_l[1], cb_l[1] + tie[0, 0], "ffn_mid_fwd")
    w_dn1 = gathered("w_dn1", act1)
    z4 = _mm(act1, w_dn1, "nn", _F32, "ffn_down", add=x3, add_scale=ALPHA)

    scat = {}

    def scatter(key, dw, axis, n):
        send, recv, dw, land, token = _scatter_start("scatter_start_" + key, dw, axis, n)
        scat[key] = (dw, land, send, recv, axis, n)
        return [token]

    loss11, dz4, dz4b, g_ln2_1, b_ln2_1 = _ln_loss_bwd(z4, ln2_g[1:2], ln2_b[1:2], loss_target[0], "ln_loss_bwd")
    tok = scatter("dn1", _mm(act1, dz4b, "tn", _XCH, "ffn_down_dw"), 0, n_dn)
    dact1 = _mm(dz4b, w_dn1, "nt", _MM, "ffn_down_dx", deps=tok)
    dhf1, dcw1, dcb1 = _ffn_mid_bwd(hf1, hc1, dact1, cw_l[1], "ffn_mid_bwd")
    tok = scatter("up1", _mm(x3b, dhf1, "tn", _XCH, "ffn_up_dw", b_parts=2, deps=tok), 1, n_up)
    dx3 = _mm(dhf1, w_up1, "nt", _F32, "ffn_up_dx", a_parts=2, add=dz4, add_scale=ALPHA, deps=tok)
    dz3, dz3b, g_ln1_1, b_ln1_1 = _ln_bwd(z3, ln1_g[1:2], dx3, "ln_bwd")
    tok = scatter("out1", _mm(yo, dz3b, "tn", _XCH, "od_out_dw", deps=tok), 0, n_out1)
    dyo = _mm(dz3b, w_out1, "nt", _F32, "od_out_dx", deps=tok)
    dqd, dkd, dke, dv, dgate, ddec, dng = _hgrn_scan_bwd(qd, kd, ke, vb, dec, st, o, h1, norm_g, dyo, "hgrn_scan_bwd")
    dh1, dlb = _hgrn_prep_bwd(h1, lb, dqd, dkd, dke, dv, dgate, ddec, "hgrn_prep_bwd")
    tok = scatter("in1", _mm(x2b, dh1, "tn", _XCH, "od_in_dw", b_parts=4, deps=tok), 1, n_in1)
    dx2 = _mm(dh1, w_in1, "nt", _F32, "od_in_dx", a_parts=4, add=dz3, add_scale=ALPHA, deps=tok)
    dz2, dz2b, g_ln2_0, b_ln2_0 = _ln_bwd(z2, ln2_g[0:1], dx2, "ln_bwd")
    tok = scatter("dn0", _mm(act0, dz2b, "tn", _XCH, "ffn_down_dw", deps=tok), 0, n_dn)
    dact0 = _mm(dz2b, w_dn0, "nt", _MM, "ffn_down_dx", deps=tok)
    dhf0, dcw0, dcb0 = _ffn_mid_bwd(hf0, hc0, dact0, cw_l[0], "ffn_mid_bwd")
    tok = scatter("up0", _mm(x1b, dhf0, "tn", _XCH, "ffn_up_dw", b_parts=2, deps=tok), 1, n_up)
    dx1 = _mm(dhf0, w_up0, "nt", _F32, "ffn_up_dx", a_parts=2, add=dz2, add_scale=ALPHA, deps=tok)
    dz1, dz1b, g_ln1_0, b_ln1_0 = _ln_bwd(z1, ln1_g[0:1], dx1, "ln_bwd")
    tok = scatter("out0", _mm(yab, dz1b, "tn", _XCH, "ev_out_dw", deps=tok), 0, n_out0)
    dyab = _mm(dz1b, w_out0, "nt", _F32, "ev_out_dx", deps=tok)
    dh0, dws, dbias, dlng, dlnb, dwp, dsc = _ev_mid_bwd(h0, dyab, ev_ln_v_g, ev_ln_v_b, ws_tril, bias, wp_b,
                                                        ev_pool_scale, "ev_mid_bwd")

    g_b_s = dbias.reshape(A_CHUNK, H, A_HEAD).sum(axis=-1).T[None]
    g_conv_w = jnp.stack([d.transpose(1, 0, 2).reshape(3, F2) for d in (dcw0, dcw1)])
    g_conv_b = jnp.stack([d.reshape(F2) for d in (dcb0, dcb1)])
    small_grads = dict(zip(small_names, [
        dlng, dlnb, dws[None], g_b_s, dwp[None], dsc, dng, lb_vjp(dlb)[0], g_conv_w, g_conv_b,
        jnp.concatenate([g_ln1_0, g_ln1_1]), jnp.concatenate([b_ln1_0, b_ln1_1]),
        jnp.concatenate([g_ln2_0, g_ln2_1]), jnp.concatenate([b_ln2_0, b_ln2_1])]))

    def by_device(g, ax):
        g = g.reshape(g.shape[:ax] + (N_DEV, g.shape[ax] // N_DEV) + g.shape[ax + 1:])
        return jnp.moveaxis(g, ax, 0).reshape(N_DEV, -1)

    shd = jnp.concatenate([by_device(small_grads[n], shard_axis[n]) for n in shd_names], axis=1)
    shd_pack = jnp.pad(shd, ((0, 0), (0, (-shd.shape[1]) % (LANE * LANE)))).reshape(-1, LANE)
    shd_rows = shd_pack.shape[0] // N_DEV
    rep_pack = _pack([small_grads[n] for n in rep_names])
    tok = scatter("in0", _mm(xb, dh0, "tn", _XCH, "ev_in_dw", deps=tok), 1, n_in0)
    tok = scatter("small_rep", rep_pack + tok[0][0, 0], None, None)
    tok = scatter("small_shd", shd_pack + tok[0][0, 0], 0, shd_rows)
    grad_x = _mm(dh0, w_in0, "nt", _F32, "ev_in_dx", add=dz1, add_scale=ALPHA, deps=tok)

    def landed(name, keys, after):
        got = _scatter_wait(name, [scat[k] for k in keys], after)
        return {k: (me1, dw, land) for k, (dw, land) in zip(keys, got)}

    early = landed("scatter_wait_early", ["dn1", "up1", "out1", "in1", "dn0", "up0", "out0"], grad_x)
    big = {}
    r_dn = _adamw_big(*early["dn1"], ffn_w_down, m_ffn_w_down, v_ffn_w_down, 0, n_dn, "adamw_w_dn1", layer=1)
    r_up = _adamw_big(*early["up1"], ffn_w_up, m_ffn_w_up, v_ffn_w_up, 1, n_up, "adamw_w_up1", layer=1)
    big["od_w_out"] = _adamw_big(*early["out1"], od_w_out[0], m_od_w_out[0], v_od_w_out[0], 0, n_out1, "adamw_w_out1")
    big["od_w_in"] = _adamw_big(*early["in1"], od_w_in[0], m_od_w_in[0], v_od_w_in[0], 1, n_in1, "adamw_w_in1")
    big["ffn_w_down"] = _adamw_big(*early["dn0"], ffn_w_down, m_ffn_w_down, v_ffn_w_down, 0, n_dn, "adamw_w_dn0", layer=0, into=r_dn)
    big["ffn_w_up"] = _adamw_big(*early["up0"], ffn_w_up, m_ffn_w_up, v_ffn_w_up, 1, n_up, "adamw_w_up0", layer=0, into=r_up)
    big["ev_w_out"] = _adamw_big(*early["out0"], ev_w_out[0], m_ev_w_out[0], v_ev_w_out[0], 0, n_out0, "adamw_w_out0")
    late = landed("scatter_wait_late", ["in0", "small_rep", "small_shd"],
                  (big["ffn_w_down"][0], big["ffn_w_up"][0], big["od_w_in"][0], big["ev_w_out"][0]))
    big["ev_w_in"] = _adamw_big(*late["in0"], ev_w_in[0], m_ev_w_in[0], v_ev_w_in[0], 1, n_in0, "adamw_w_in0")

    rep_mat = _sum_in_device_order(*late["small_rep"], "sum_small_rep")
    local_g = dict(zip(rep_names, _unpack(rep_mat, [small_grads[n].shape for n in rep_names])))
    _, shd_all, shd_land = late["small_shd"]
    shd_own = lax.dynamic_slice_in_dim(shd_all, me * shd_rows, shd_rows, axis=0)
    shd_mat = _sum_in_device_order(me1, shd_own, shd_land, "sum_small_shd")
    local_g.update(zip(shd_names, _unpack(shd_mat, [given[n][0].shape for n in shd_names])))
    local_shapes = [given[n][0].shape for n in small_names]
    res = _adamw(_pack([local_g[n] for n in small_names])[None], *small_packs, "adamw_small")
    small = {n: [] for n in small_names}
    for r in res:
        for n, a in zip(small_names, _unpack(r, local_shapes)):
            small[n].append(a)

    loss = lax.psum(loss11[0, 0], ("x", "y", "c"))
    order = ["ev_w_in", "ev_ln_v_g", "ev_ln_v_b", "ev_w_s", "ev_b_s", "ev_w_pool", "ev_pool_scale", "ev_w_out", "od_w_in",
             "od_norm_g", "od_w_out", "lb_param", "ffn_w_up", "ffn_conv_w", "ffn_conv_b", "ffn_w_down", "ln1_g", "ln1_b",
             "ln2_g", "ln2_b"]
    shapes = dict(ev_w_in=ev_w_in.shape, ev_w_out=ev_w_out.shape, od_w_in=od_w_in.shape, od_w_out=od_w_out.shape,
                  ffn_w_up=ffn_w_up.shape, ffn_w_down=ffn_w_down.shape)
    outs = [loss, grad_x[None]]
    for kind in range(4):
        for n in order:
            outs.append(big[n][kind].reshape(shapes[n]) if n in big else small[n][kind])
    return tuple(outs)
```

```python
import functools
import math

import jax
import jax.numpy as jnp
from jax import lax
from jax.experimental import pallas as pl
from jax.experimental.pallas import tpu as pltpu

_MM = jnp.bfloat16
_XCH = jnp.bfloat16

DEPTH = 2
ALPHA = (2 * DEPTH) ** 0.25
LN_EPS = 1e-5
A_CHUNK = 128
A_HEAD = 128
B_GROUPS = 4
POOL_HALO = 16
C_CHUNK = 64
C_HEAD = 128
SCAN_UNROLL = 64
CONV_HALO = 8
PACKED_ROWS = 16
LN_ROWS = 512
FFN_ROWS, FFN_FWD_COLS = 1024, 1408
FFN_BWD_ROWS, FFN_BWD_COLS = 512, 1408
FFN_CHUNK = 128
ADAM_LR, ADAM_B1, ADAM_B2, ADAM_EPS, ADAM_WD, ADAM_STEP = 0.001, 0.9, 0.999, 1e-08, 0.01, 10
N_DEV = 8
LANE = 128
VMEM_LIMIT = 56 * 1024 * 1024
MM_FULL_K = 3072
MM_FULL_K_TN = 4096
MM_DEEP_K = 2816
MM_LN_ROWS, MM_LN_K = 512, 1408

_F32 = jnp.float32
_NN = (((1,), (0,)), ((), ()))
_NT = (((1,), (1,)), ((), ()))
_TN = (((0,), (0,)), ((), ()))
_S = jax.ShapeDtypeStruct


def _dot(a, b, dims=_NN):
    return lax.dot_general(a, b, dims, preferred_element_type=_F32)


def _tile(dim, pref):
    best = None
    d = LANE
    while d <= min(dim, pref):
        if dim % d == 0:
            best = d
        d += LANE
    return best if best is not None else dim


def _params(sem):
    return pltpu.CompilerParams(dimension_semantics=sem, vmem_limit_bytes=VMEM_LIMIT)


def _sigmoid(x):
    return 0.5 * jnp.tanh(0.5 * x) + 0.5


def _sigmoid_rel(x):
    return 1.0 / (1.0 + jnp.exp(-x))


_GELU_C = 0.7978845608028654
_GELU_A = 0.044715


def _gelu_and_grad(x):
    t = jnp.tanh(_GELU_C * (x + _GELU_A * x * x * x))
    y = 0.5 * x * (1.0 + t)
    dy = 0.5 * (1.0 + t) + 0.5 * x * (1.0 - t * t) * _GELU_C * (1.0 + 3.0 * _GELU_A * x * x)
    return y, dy


def _row_index(n):
    return lax.broadcasted_iota(jnp.int32, (n, 1), 0)


def _mm_tiles(mode, M, N, K, with_add):
    if mode == "tn":
        return _tile(M, 1024), _tile(N, 1024), _tile(K, MM_FULL_K_TN)
    if K <= MM_FULL_K:
        return _tile(M, 1024 if with_add else 2048), _tile(N, 1024 if mode == "nn" else 512), K
    return _tile(M, 1024), _tile(N, 1024), _tile(K, MM_DEEP_K)


def _mm(a, b, mode, out_dtype, name, *, a_parts=1, b_parts=1, out_parts=1, add=None, add_scale=1.0, deps=(), tiles=None,
        ln=None):
    if mode == "nn":
        M, K = a.shape
        N = b.shape[1]
    elif mode == "nt":
        if a_parts > 1:
            M, K = a.shape[1], a.shape[2] * a_parts
        else:
            M, K = a.shape
        N = b.shape[0]
    else:
        K, M = a.shape
        N = b.shape[-1] * b_parts
    tm, tn, tk = tiles if tiles is not None else _mm_tiles(mode, M, N // max(b_parts, out_parts), K // a_parts, add is not None)
    if ln is not None:
        tm, tn, tk = _tile(M, MM_LN_ROWS), N, (K if K <= MM_FULL_K else _tile(K, MM_LN_K))
    nk = K // tk
    npj = (N // max(b_parts, out_parts)) // tn
    nkp = (K // a_parts) // tk
    if mode == "nn":
        a_spec = pl.BlockSpec((tm, tk), lambda i, j, k: (i, k))
        b_spec = pl.BlockSpec((tk, tn), lambda i, j, k: (k, j))
        dims = _NN
    elif mode == "nt":
        if a_parts > 1:
            a_spec = pl.BlockSpec((None, tm, tk), lambda i, j, k: (k // nkp, i, k % nkp))
        else:
            a_spec = pl.BlockSpec((tm, tk), lambda i, j, k: (i, k))
        b_spec = pl.BlockSpec((tn, tk), lambda i, j, k: (j, k))
        dims = _NT
    else:
        a_spec = pl.BlockSpec((tk, tm), lambda i, j, k: (k, i))
        if b_parts > 1:
            b_spec = pl.BlockSpec((None, tk, tn), lambda i, j, k: (j // npj, k, j % npj))
        else:
            b_spec = pl.BlockSpec((tk, tn), lambda i, j, k: (k, j))
        dims = _TN
    if out_parts > 1:
        out_spec = pl.BlockSpec((None, tm, tn), lambda i, j, k: (j // npj, i, j % npj))
        out_shape = _S((out_parts, M, N // out_parts), out_dtype)
    else:
        out_spec = pl.BlockSpec((tm, tn), lambda i, j, k: (i, j))
        out_shape = _S((M, N), out_dtype)
    in_specs = [a_spec, b_spec]
    args = [a, b]
    if add is not None:
        in_specs.append(pl.BlockSpec((tm, tn), lambda i, j, k: (i, j)))
        args.append(add)
    if ln is not None:
        vec = pl.BlockSpec((1, N), lambda i, j, k: (0, 0))
        in_specs += [vec, vec]
        args += list(ln)
        out_spec = [out_spec] * 3
        out_shape = [_S((M, N), _F32), _S((M, N), _F32), _S((M, N), _MM)]
    n_ln = 2 + (add is not None)
    in_specs += [_ANY] * len(deps)
    args += list(deps)
    n_out = 1 if ln is None else 3

    def finish(r, refs, outs):
        if add is not None:
            r = r + add_scale * refs[2][...]
        if ln is None:
            outs[0][...] = r.astype(outs[0].dtype)
            return
        mu = jnp.mean(r, axis=-1, keepdims=True)
        rc = r - mu
        y = rc * lax.rsqrt(jnp.mean(rc * rc, axis=-1, keepdims=True) + LN_EPS) * refs[n_ln][...] + refs[n_ln + 1][...]
        outs[0][...] = r
        outs[1][...] = y
        outs[2][...] = y.astype(outs[2].dtype)

    def body_one(*refs):
        finish(_dot(refs[0][...], refs[1][...], dims), refs, refs[len(refs) - n_out:])

    def body_acc(*refs):
        acc = refs[-1]
        k = pl.program_id(2)

        @pl.when(k == 0)
        def _():
            acc[...] = jnp.zeros_like(acc)

        acc[...] += _dot(refs[0][...], refs[1][...], dims)

        @pl.when(k == nk - 1)
        def _():
            finish(acc[...], refs, refs[len(refs) - 1 - n_out:len(refs) - 1])

    return pl.pallas_call(
        body_one if nk == 1 else body_acc, name=name, grid=(M // tm, N // tn, nk), in_specs=in_specs,
        out_specs=out_spec, out_shape=out_shape,
        scratch_shapes=[] if nk == 1 else [pltpu.VMEM((tm, tn), _F32)],
        compiler_params=_params(("parallel", "parallel", "arbitrary")),
    )(*args)


def _cast(x2d, dtype, name, deps=()):
    R, C = x2d.shape
    tr = _tile(R, 512) if R % LANE == 0 else R

    def body(x_ref, *rest):
        rest[-1][...] = x_ref[...].astype(rest[-1].dtype)

    return pl.pallas_call(
        body, name=name, grid=(R // tr,), in_specs=[pl.BlockSpec((tr, C), lambda i: (i, 0))] + [_ANY] * len(deps),
        out_specs=pl.BlockSpec((tr, C), lambda i: (i, 0)), out_shape=_S((R, C), dtype),
        compiler_params=_params(("parallel",)),
    )(x2d, *deps)


def _ln_fwd(z, g, b, name):
    T, D = z.shape
    tr = _tile(T, LN_ROWS)

    def body(z_ref, g_ref, b_ref, y_ref, yb_ref):
        zz = z_ref[...]
        mu = jnp.mean(zz, axis=-1, keepdims=True)
        zc = zz - mu
        var = jnp.mean(zc * zc, axis=-1, keepdims=True)
        y = zc * lax.rsqrt(var + LN_EPS) * g_ref[...] + b_ref[...]
        y_ref[...] = y
        yb_ref[...] = y.astype(yb_ref.dtype)

    row = pl.BlockSpec((tr, D), lambda i: (i, 0))
    vec = pl.BlockSpec((1, D), lambda i: (0, 0))
    return pl.pallas_call(
        body, name=name, grid=(T // tr,), in_specs=[row, vec, vec], out_specs=[row, row],
        out_shape=[_S((T, D), _F32), _S((T, D), _MM)], compiler_params=_params(("parallel",)),
    )(z, g, b)


def _ln_bwd(z, g, dy, name):
    T, D = z.shape
    tr = _tile(T, LN_ROWS)

    def body(z_ref, g_ref, dy_ref, dz_ref, dzb_ref, dg_ref, db_ref):
        @pl.when(pl.program_id(0) == 0)
        def _():
            dg_ref[...] = jnp.zeros_like(dg_ref)
            db_ref[...] = jnp.zeros_like(db_ref)

        zz = z_ref[...]
        mu = jnp.mean(zz, axis=-1, keepdims=True)
        zc = zz - mu
        rstd = lax.rsqrt(jnp.mean(zc * zc, axis=-1, keepdims=True) + LN_EPS)
        xh = zc * rstd
        d = dy_ref[...]
        dg_ref[...] += jnp.sum(d * xh, axis=0, keepdims=True)
        db_ref[...] += jnp.sum(d, axis=0, keepdims=True)
        dxh = d * g_ref[...]
        dz = rstd * (dxh - jnp.mean(dxh, axis=-1, keepdims=True) - xh * jnp.mean(dxh * xh, axis=-1, keepdims=True))
        dz_ref[...] = dz
        dzb_ref[...] = dz.astype(dzb_ref.dtype)

    row = pl.BlockSpec((tr, D), lambda i: (i, 0))
    vec = pl.BlockSpec((1, D), lambda i: (0, 0))
    return pl.pallas_call(
        body, name=name, grid=(T // tr,), in_specs=[row, vec, row], out_specs=[row, row, vec, vec],
        out_shape=[_S((T, D), _F32), _S((T, D), _MM), _S((1, D), _F32), _S((1, D), _F32)],
        compiler_params=_params(("arbitrary",)),
    )(z, g, dy)


def _ln_loss_bwd(z, g, b, target, name):
    T, D = z.shape
    tr = _tile(T, LN_ROWS)

    def body(z_ref, g_ref, b_ref, t_ref, loss_ref, dz_ref, dzb_ref, dg_ref, db_ref, lacc):
        i = pl.program_id(0)

        @pl.when(i == 0)
        def _():
            dg_ref[...] = jnp.zeros_like(dg_ref)
            db_ref[...] = jnp.zeros_like(db_ref)
            lacc[...] = jnp.zeros_like(lacc)

        zz = z_ref[...]
        mu = jnp.mean(zz, axis=-1, keepdims=True)
        zc = zz - mu
        rstd = lax.rsqrt(jnp.mean(zc * zc, axis=-1, keepdims=True) + LN_EPS)
        xh = zc * rstd
        err = xh * g_ref[...] + b_ref[...] - t_ref[...]
        lacc[...] += jnp.sum(err * err, axis=0, keepdims=True)
        d = err * (1.0 / D)
        dg_ref[...] += jnp.sum(d * xh, axis=0, keepdims=True)
        db_ref[...] += jnp.sum(d, axis=0, keepdims=True)
        dxh = d * g_ref[...]
        dz = rstd * (dxh - jnp.mean(dxh, axis=-1, keepdims=True) - xh * jnp.mean(dxh * xh, axis=-1, keepdims=True))
        dz_ref[...] = dz
        dzb_ref[...] = dz.astype(dzb_ref.dtype)

        @pl.when(i == pl.num_programs(0) - 1)
        def _():
            loss_ref[...] = jnp.sum(lacc[...], axis=-1, keepdims=True) * (0.5 / D)

    row = pl.BlockSpec((tr, D), lambda i: (i, 0))
    vec = pl.BlockSpec((1, D), lambda i: (0, 0))
    one = pl.BlockSpec((1, 1), lambda i: (0, 0))
    return pl.pallas_call(
        body, name=name, grid=(T // tr,), in_specs=[row, vec, vec, row], out_specs=[one, row, row, vec, vec],
        out_shape=[_S((1, 1), _F32), _S((T, D), _F32), _S((T, D), _MM), _S((1, D), _F32), _S((1, D), _F32)],
        scratch_shapes=[pltpu.VMEM((1, D), _F32)], compiler_params=_params(("arbitrary",)),
    )(z, g, b, target)


def _conv3(X, cw, cb):
    return cb + cw[2:3] * X + cw[1:2] * pltpu.roll(X, 1, 0) + cw[0:1] * pltpu.roll(X, 2, 0)


def _ffn_mid_fwd(h, cw, cb, name):
    _, T, F = h.shape
    tr = _tile(T, FFN_ROWS)
    tc = _tile(F, FFN_FWD_COLS)
    nb = tr // CONV_HALO

    rc = _tile(tr, FFN_CHUNK)
    lanes = [slice(cs * LANE, (cs + 1) * LANE) for cs in range(tc // LANE)]

    def body(h_ref, p_ref, cw_ref, cb_ref, o_ref, c_ref):
        i = pl.program_id(0)

        def work(r0, cols, X):
            hc = [_conv3(X[part], cw_ref[part, :, cols], cb_ref[part, :, cols])[CONV_HALO:] for part in range(2)]
            for part in range(2):
                c_ref[part, pl.ds(r0, rc), cols] = hc[part].astype(c_ref.dtype)
            a, v = hc
            o_ref[pl.ds(r0, rc), cols] = (a * _sigmoid(a) * v).astype(o_ref.dtype)

        for cols in lanes:
            work(0, cols, [jnp.concatenate([jnp.where(i == 0, 0.0, p_ref[part, :, cols]), h_ref[part, 0:rc, cols]], axis=0)
                           for part in range(2)])

        def chunk(c, carry):
            r0 = pl.multiple_of(c * rc, rc)
            for cols in lanes:
                work(r0, cols, [h_ref[part, pl.ds(r0 - CONV_HALO, rc + CONV_HALO), cols] for part in range(2)])
            return carry

        lax.fori_loop(1, tr // rc, chunk, 0)

    return pl.pallas_call(
        body, name=name, grid=(T // tr, F // tc),
        in_specs=[pl.BlockSpec((2, tr, tc), lambda i, j: (0, i, j)),
                  pl.BlockSpec((2, CONV_HALO, tc), lambda i, j: (0, jnp.maximum(i * nb - 1, 0), j)),
                  pl.BlockSpec((2, 3, tc), lambda i, j: (0, 0, j)),
                  pl.BlockSpec((2, 1, tc), lambda i, j: (0, 0, j))],
        out_specs=[pl.BlockSpec((tr, tc), lambda i, j: (i, j)), pl.BlockSpec((2, tr, tc), lambda i, j: (0, i, j))],
        out_shape=[_S((T, F), _MM), _S((2, T, F), _MM)],
        compiler_params=_params(("parallel", "parallel")),
    )(h, h, cw, cb)


def _ffn_mid_bwd(h, hc, dact, cw, name):
    _, T, F = h.shape
    tr = _tile(T, FFN_BWD_ROWS)
    tc = _tile(F, FFN_BWD_COLS)
    nb_c = tr // PACKED_ROWS
    rc = _tile(tr, FFN_CHUNK)
    n = rc + CONV_HALO

    def body(h_ref, c_ref, cn_ref, d_ref, dn_ref, cw_ref, dh_ref, dcw_ref, dcb_ref):
        i = pl.program_id(1)
        is_last = i == pl.num_programs(1) - 1

        @pl.when(i == 0)
        def _():
            dcw_ref[...] = jnp.zeros_like(dcw_ref)
            dcb_ref[...] = jnp.zeros_like(dcb_ref)

        def work(r0, cols, a, v, D):
            sg = _sigmoid(a)
            dhc = [D * v * sg * (1.0 + a * (1.0 - sg)), D * a * sg]
            for part in range(2):
                X = h_ref[part, pl.ds(r0, rc), cols]
                cwp = cw_ref[part, :, cols]
                dh = None
                for k in range(3):
                    g = (dhc[part] if k == 0 else pltpu.roll(dhc[part], n - k, 0))[0:rc]
                    term = cwp[2 - k:3 - k] * g
                    dh = term if dh is None else dh + term
                    dcw_ref[part, 2 - k:3 - k, cols] += jnp.sum(g * X, axis=0, keepdims=True)
                    if k == 0:
                        dcb_ref[part, :, cols] += jnp.sum(g, axis=0, keepdims=True)
                dh_ref[part, pl.ds(r0, rc), cols] = dh.astype(dh_ref.dtype)

        lanes = [slice(cs * LANE, (cs + 1) * LANE) for cs in range(tc // LANE)]

        def chunk(c, carry):
            r0 = pl.multiple_of(c * rc, rc)
            for cols in lanes:
                a, v = [c_ref[part, pl.ds(r0, rc + PACKED_ROWS), cols].astype(_F32)[0:n] for part in range(2)]
                work(r0, cols, a, v, d_ref[pl.ds(r0, rc + PACKED_ROWS), cols].astype(_F32)[0:n])
            return carry

        lax.fori_loop(0, tr // rc - 1, chunk, 0)
        r0 = tr - rc
        for cols in lanes:
            a, v = [jnp.concatenate([c_ref[part, r0:tr, cols].astype(_F32), cn_ref[part, :, cols].astype(_F32)[0:CONV_HALO]],
                                    axis=0) for part in range(2)]
            D = jnp.concatenate([d_ref[r0:tr, cols].astype(_F32),
                                 jnp.where(is_last, 0.0, dn_ref[:, cols].astype(_F32)[0:CONV_HALO])], axis=0)
            work(r0, cols, a, v, D)

    return pl.pallas_call(
        body, name=name, grid=(F // tc, T // tr),
        in_specs=[pl.BlockSpec((2, tr, tc), lambda j, i: (0, i, j)),
                  pl.BlockSpec((2, tr, tc), lambda j, i: (0, i, j)),
                  pl.BlockSpec((2, PACKED_ROWS, tc), lambda j, i: (0, jnp.minimum((i + 1) * nb_c, T // PACKED_ROWS - 1), j)),
                  pl.BlockSpec((tr, tc), lambda j, i: (i, j)),
                  pl.BlockSpec((PACKED_ROWS, tc), lambda j, i: (jnp.minimum((i + 1) * nb_c, T // PACKED_ROWS - 1), j)),
                  pl.BlockSpec((2, 3, tc), lambda j, i: (0, 0, j))],
        out_specs=[pl.BlockSpec((2, tr, tc), lambda j, i: (0, i, j)),
                   pl.BlockSpec((2, 3, tc), lambda j, i: (0, 0, j)),
                   pl.BlockSpec((2, 1, tc), lambda j, i: (0, 0, j))],
        out_shape=[_S((2, T, F), _MM), _S((2, 3, F), _F32), _S((2, 1, F), _F32)],
        compiler_params=_params(("parallel", "arbitrary")),
    )(h, hc, hc, dact, dact, cw)


def _ev_common(h_ref, hp_ref, lng_ref, lnb_ref, ws_ref, bias_ref, i, tr, W):
    H = W // A_HEAD
    u, gu = _gelu_and_grad(h_ref[0])
    v, gv = _gelu_and_grad(h_ref[1])
    mu = jnp.mean(v, axis=-1, keepdims=True)
    vc = v - mu
    rstd = lax.rsqrt(jnp.mean(vc * vc, axis=-1, keepdims=True) + LN_EPS)
    vhat = vc * rstd
    vb = (vhat * lng_ref[...] + lnb_ref[...]).astype(_MM)
    s_chunks = []
    for c in range(tr // A_CHUNK):
        r0 = c * A_CHUNK
        heads = [_dot(ws_ref[hd], vb[r0:r0 + A_CHUNK, hd * A_HEAD:(hd + 1) * A_HEAD]) for hd in range(H)]
        s_chunks.append(jnp.concatenate(heads, axis=1) + bias_ref[...])
    prev = jnp.where(i == 0, 0.0, hp_ref[...])
    X = jnp.concatenate([prev, h_ref[2]], axis=0)
    return u, gu, gv, rstd, vhat, vb, s_chunks, X


def _pool_inv_count(i, tr, rows, win):
    pos = i * tr + _row_index(rows) + 1
    return 1.0 / jnp.minimum(pos, win).astype(_F32)


def _pool_fwd(X, g, Wg, i, tr):
    xg = X[:, g * Wg:(g + 1) * Wg]
    s = xg
    for k in range(g + 1):
        s = s + pltpu.roll(s, 2 ** k, 0)
    return s[POOL_HALO:] * _pool_inv_count(i, tr, tr, 2 ** (g + 1)) - xg[POOL_HALO:]


def _ev_mid_fwd(h, lng, lnb, ws, bias, wp, sc, name):
    _, T, W = h.shape
    tr = _tile(T, 256)
    H = W // A_HEAD
    Wg = W // B_GROUPS
    nb = tr // POOL_HALO

    def body(h_ref, hp_ref, lng_ref, lnb_ref, ws_ref, bias_ref, wp_ref, sc_ref, o_ref):
        i = pl.program_id(0)
        u, _, _, _, _, _, s_chunks, X = _ev_common(h_ref, hp_ref, lng_ref, lnb_ref, ws_ref, bias_ref, i, tr, W)
        for c, s in enumerate(s_chunks):
            r0 = c * A_CHUNK
            o_ref[r0:r0 + A_CHUNK, 0:W] = (u[r0:r0 + A_CHUNK] * s).astype(o_ref.dtype)
        for g in range(B_GROUPS):
            p = _pool_fwd(X, g, Wg, i, tr)
            y = _dot(p.astype(_MM), wp_ref[g]) * sc_ref[:, g * Wg:(g + 1) * Wg]
            o_ref[:, W + g * Wg:W + (g + 1) * Wg] = y.astype(o_ref.dtype)

    vec = pl.BlockSpec((1, W), lambda i: (0, 0))
    return pl.pallas_call(
        body, name=name, grid=(T // tr,),
        in_specs=[pl.BlockSpec((3, tr, W), lambda i: (0, i, 0)),
                  pl.BlockSpec((None, POOL_HALO, W), lambda i: (2, jnp.maximum(i * nb - 1, 0), 0)),
                  vec, vec,
                  pl.BlockSpec((H, A_CHUNK, A_CHUNK), lambda i: (0, 0, 0)),
                  pl.BlockSpec((A_CHUNK, W), lambda i: (0, 0)),
                  pl.BlockSpec((B_GROUPS, Wg, Wg), lambda i: (0, 0, 0)),
                  vec],
        out_specs=pl.BlockSpec((tr, 2 * W), lambda i: (i, 0)), out_shape=_S((T, 2 * W), _MM),
        compiler_params=_params(("parallel",)),
    )(h, h, lng, lnb, ws, bias, wp, sc)


def _ev_mid_bwd(h, dy, lng, lnb, ws, bias, wp, sc, name):
    _, T, W = h.shape
    tr = _tile(T, 256)
    H = W // A_HEAD
    Wg = W // B_GROUPS
    nb = tr // POOL_HALO
    last_blk = T // POOL_HALO - 1
    n = tr + POOL_HALO

    def body(h_ref, hp_ref, dy_ref, dyn_ref, lng_ref, lnb_ref, ws_ref, bias_ref, wp_ref, sc_ref,
             dh_ref, dws_ref, dbias_ref, dlng_ref, dlnb_ref, dwp_ref, dsc_ref):
        i = pl.program_id(0)

        @pl.when(i == 0)
        def _():
            for r in (dws_ref, dbias_ref, dlng_ref, dlnb_ref, dwp_ref, dsc_ref):
                r[...] = jnp.zeros_like(r)

        u, gu, gv, rstd, vhat, vb, s_chunks, X = _ev_common(h_ref, hp_ref, lng_ref, lnb_ref, ws_ref, bias_ref, i, tr, W)
        rr = lax.broadcasted_iota(jnp.int32, (A_CHUNK, A_CHUNK), 0)
        cc = lax.broadcasted_iota(jnp.int32, (A_CHUNK, A_CHUNK), 1)
        tril = rr >= cc
        du_chunks, dvln_chunks = [], []
        for c, s in enumerate(s_chunks):
            r0 = c * A_CHUNK
            dya = dy_ref[r0:r0 + A_CHUNK, 0:W]
            du_chunks.append(dya * s)
            ds = dya * u[r0:r0 + A_CHUNK]
            dbias_ref[...] += ds
            dsb = ds.astype(_MM)
            heads = []
            for hd in range(H):
                cols = slice(hd * A_HEAD, (hd + 1) * A_HEAD)
                dws_ref[hd] += jnp.where(tril, _dot(dsb[:, cols], vb[r0:r0 + A_CHUNK, cols], _NT), 0.0)
                heads.append(_dot(ws_ref[hd], dsb[:, cols], _TN))
            dvln_chunks.append(jnp.concatenate(heads, axis=1))
        du = jnp.concatenate(du_chunks, axis=0)
        dvln = jnp.concatenate(dvln_chunks, axis=0)
        dlng_ref[...] += jnp.sum(dvln * vhat, axis=0, keepdims=True)
        dlnb_ref[...] += jnp.sum(dvln, axis=0, keepdims=True)
        dxh = dvln * lng_ref[...]
        dv = rstd * (dxh - jnp.mean(dxh, axis=-1, keepdims=True) - vhat * jnp.mean(dxh * vhat, axis=-1, keepdims=True))
        dh_ref[:, 0:W] = (du * gu).astype(dh_ref.dtype)
        dh_ref[:, W:2 * W] = (dv * gv).astype(dh_ref.dtype)

        dyb = dy_ref[:, W:2 * W]
        dyb_full = jnp.concatenate([dyb, jnp.where(i == pl.num_programs(0) - 1, 0.0, dyn_ref[...])], axis=0)
        for g in range(B_GROUPS):
            cols = slice(g * Wg, (g + 1) * Wg)
            pb = _pool_fwd(X, g, Wg, i, tr).astype(_MM)
            ypre = _dot(pb, wp_ref[g])
            dsc_ref[:, cols] += jnp.sum(dyb[:, cols] * ypre, axis=0, keepdims=True)
            dyp = (dyb_full[:, cols] * sc_ref[:, cols]).astype(_MM)
            dwp_ref[g] += _dot(pb, dyp[0:tr], _TN)
            dp = _dot(dyp, wp_ref[g], _NT)
            s = dp * _pool_inv_count(i, tr, n, 2 ** (g + 1))
            for k in range(g + 1):
                s = s + pltpu.roll(s, n - 2 ** k, 0)
            dh_ref[:, 2 * W + g * Wg:2 * W + (g + 1) * Wg] = (s[0:tr] - dp[0:tr]).astype(dh_ref.dtype)

    vec = pl.BlockSpec((1, W), lambda i: (0, 0))
    ws_spec = pl.BlockSpec((H, A_CHUNK, A_CHUNK), lambda i: (0, 0, 0))
    bias_spec = pl.BlockSpec((A_CHUNK, W), lambda i: (0, 0))
    wp_spec = pl.BlockSpec((B_GROUPS, Wg, Wg), lambda i: (0, 0, 0))
    return pl.pallas_call(
        body, name=name, grid=(T // tr,),
        in_specs=[pl.BlockSpec((3, tr, W), lambda i: (0, i, 0)),
                  pl.BlockSpec((None, POOL_HALO, W), lambda i: (2, jnp.maximum(i * nb - 1, 0), 0)),
                  pl.BlockSpec((tr, 2 * W), lambda i: (i, 0)),
                  pl.BlockSpec((POOL_HALO, W), lambda i: (jnp.minimum((i + 1) * nb, last_blk), 1)),
                  vec, vec, ws_spec, bias_spec, wp_spec, vec],
        out_specs=[pl.BlockSpec((tr, 3 * W), lambda i: (i, 0)), ws_spec, bias_spec, vec, vec, wp_spec, vec],
        out_shape=[_S((T, 3 * W), _MM), _S((H, A_CHUNK, A_CHUNK), _F32), _S((A_CHUNK, W), _F32), _S((1, W), _F32),
                   _S((1, W), _F32), _S((B_GROUPS, Wg, Wg), _F32), _S((1, W), _F32)],
        compiler_params=_params(("arbitrary",)),
    )(h, h, dy, dy, lng, lnb, ws, bias, wp, sc)


def _chunk_cumsum(x, rin):
    s = 1
    while s < C_CHUNK:
        x = x + jnp.where(rin >= s, pltpu.roll(x, s, 0), 0.0)
        s *= 2
    return x


def _chunk_revcumsum(x, rin):
    n = x.shape[0]
    s = 1
    while s < C_CHUNK:
        x = x + jnp.where(rin + s < C_CHUNK, pltpu.roll(x, n - s, 0), 0.0)
        s *= 2
    return x


def _hgrn_gates(q, fl, lb, tr, tc):
    nch = tr // C_CHUNK
    sq = _sigmoid(q)
    sf = _sigmoid_rel(fl)
    f = lb + (1.0 - lb) * sf
    logf = jnp.log(f)
    rin = _row_index(tr) % C_CHUNK
    b = _chunk_cumsum(logf, rin)
    tot3 = jnp.sum(logf.reshape(nch, C_CHUNK, tc), axis=1, keepdims=True)
    eb = jnp.exp(b)
    enb = jnp.exp(-b)
    ekb = jnp.exp(tot3 - b.reshape(nch, C_CHUNK, tc)).reshape(tr, tc)
    return sq, sf, f, rin, tot3, eb, enb, ekb


def _hgrn_prep_fwd(h, lb, name):
    _, T, D = h.shape
    tr = _tile(T, 512)
    tc = _tile(D, 512)
    nch = tr // C_CHUNK

    def body(q_ref, f_ref, v_ref, lb_ref, qd_ref, kd_ref, ke_ref, vb_ref, dec_ref):
        q = q_ref[...]
        sq, _, f, _, tot3, eb, enb, ekb = _hgrn_gates(q, f_ref[...], lb_ref[...], tr, tc)
        kk = 1.0 - f
        qd_ref[...] = (q * sq * eb).astype(qd_ref.dtype)
        kd_ref[...] = (kk * enb).astype(kd_ref.dtype)
        ke_ref[...] = (kk * ekb).astype(ke_ref.dtype)
        vb_ref[...] = v_ref[...].astype(vb_ref.dtype)
        dec_ref[...] = jnp.exp(tot3).reshape(nch, tc)

    def part(p):
        return pl.BlockSpec((None, tr, tc), lambda i, j: (p, i, j))

    blk = pl.BlockSpec((tr, tc), lambda i, j: (i, j))
    return pl.pallas_call(
        body, name=name, grid=(T // tr, D // tc),
        in_specs=[part(0), part(1), part(2), pl.BlockSpec((1, tc), lambda i, j: (0, j))],
        out_specs=[blk, blk, blk, blk, pl.BlockSpec((nch, tc), lambda i, j: (i, j))],
        out_shape=[_S((T, D), _MM)] * 4 + [_S((T // C_CHUNK, D), _F32)],
        compiler_params=_params(("parallel", "parallel")),
    )(h, h, h, lb)


def _tril_mask():
    rr = lax.broadcasted_iota(jnp.int32, (C_CHUNK, C_CHUNK), 0)
    cc = lax.broadcasted_iota(jnp.int32, (C_CHUNK, C_CHUNK), 1)
    return rr >= cc


def _hgrn_scan_fwd(qd, kd, ke, vb, dec, h, ng, name):
    T, D = qd.shape
    NH = D // C_HEAD
    N = T // C_CHUNK

    def body(qd_ref, kd_ref, ke_ref, vb_ref, dec_ref, g_ref, ng_ref, o_ref, y_ref, st_ref):
        mask = _tril_mask()

        per_trip = math.gcd(N, SCAN_UNROLL)

        def trip(i, St):
            ahead = []
            for u in range(per_trip):
                n = i * per_trip + u
                r = pl.ds(pl.multiple_of(n * C_CHUNK, C_CHUNK), C_CHUNK)
                Qd, Kd, Ke, V = qd_ref[r, :], kd_ref[r, :], ke_ref[r, :], vb_ref[r, :]
                att = jnp.where(mask, _dot(Qd, Kd, _NT), 0.0).astype(_MM)
                ahead.append((n, r, _dot(att, V), _dot(V, Ke, _TN)))
            for n, r, o_intra, update in ahead:
                o_ref[r, :] = o_intra + _dot(qd_ref[r, :], St.astype(_MM), _NT)
                st_ref[n] = St
                St = St * dec_ref[pl.ds(n, 1), :] + update
            return St

        lax.fori_loop(0, N // per_trip, trip, jnp.zeros((C_HEAD, C_HEAD), _F32))
        o = o_ref[...]
        r = lax.rsqrt(jnp.mean(o * o, axis=-1, keepdims=True) + LN_EPS)
        y_ref[...] = (o * r * ng_ref[...] * _sigmoid(g_ref[...])).astype(y_ref.dtype)

    col = pl.BlockSpec((T, C_HEAD), lambda j: (0, j))
    return pl.pallas_call(
        body, name=name, grid=(NH,),
        in_specs=[col, col, col, col, pl.BlockSpec((N, C_HEAD), lambda j: (0, j)),
                  pl.BlockSpec((None, T, C_HEAD), lambda j: (3, 0, j)), pl.BlockSpec((1, C_HEAD), lambda j: (0, j))],
        out_specs=[col, col, pl.BlockSpec((None, N, C_HEAD, C_HEAD), lambda j: (j, 0, 0, 0))],
        out_shape=[_S((T, D), _F32), _S((T, D), _MM), _S((NH, N, C_HEAD, C_HEAD), _F32)],
        compiler_params=_params(("parallel",)),
    )(qd, kd, ke, vb, dec, h, ng)


def _hgrn_scan_bwd(qd, kd, ke, vb, dec, st, o, h, ng, dy, name):
    T, D = qd.shape
    NH = D // C_HEAD
    N = T // C_CHUNK

    def body(qd_ref, kd_ref, ke_ref, vb_ref, dec_ref, st_ref, o_ref, g_ref, ng_ref, dy_ref,
             dqd_ref, dkd_ref, dke_ref, dv_ref, dgate_ref, ddec_ref, dng_ref, do_s):
        o = o_ref[...]
        r = lax.rsqrt(jnp.mean(o * o, axis=-1, keepdims=True) + LN_EPS)
        oh = o * r
        gn = ng_ref[...]
        sg = _sigmoid(g_ref[...])
        d = dy_ref[...]
        dyn = d * sg
        dgate_ref[...] = (d * oh * gn * sg * (1.0 - sg)).astype(dgate_ref.dtype)
        dng_ref[...] = jnp.sum(dyn * oh, axis=0, keepdims=True)
        doh = dyn * gn
        do_s[...] = (r * (doh - oh * jnp.mean(doh * oh, axis=-1, keepdims=True))).astype(do_s.dtype)
        mask = _tril_mask()

        per_trip = math.gcd(N, SCAN_UNROLL)

        def trip(i, dSt):
            ahead = []
            for u in range(per_trip):
                n = N - 1 - (i * per_trip + u)
                rws = pl.ds(pl.multiple_of(n * C_CHUNK, C_CHUNK), C_CHUNK)
                Qd, Kd, V, dO = qd_ref[rws, :], kd_ref[rws, :], vb_ref[rws, :], do_s[rws, :]
                att = jnp.where(mask, _dot(Qd, Kd, _NT), 0.0).astype(_MM)
                dA = jnp.where(mask, _dot(dO, V, _NT), 0.0).astype(_MM)
                dqd_ref[rws, :] = _dot(dA, Kd) + _dot(dO, st_ref[n].astype(_MM))
                dkd_ref[rws, :] = _dot(dA, Qd, _TN)
                ahead.append((n, rws, _dot(att, dO, _TN), _dot(dO, Qd, _TN)))
            for n, rws, dv_intra, d_state in ahead:
                dStb = dSt.astype(_MM)
                dv_ref[rws, :] = (dv_intra + _dot(ke_ref[rws, :], dStb, _NT)).astype(dv_ref.dtype)
                dke_ref[rws, :] = _dot(vb_ref[rws, :], dStb)
                ddec_ref[pl.ds(n, 1), :] = jnp.sum(dSt * st_ref[n], axis=0, keepdims=True)
                dSt = dSt * dec_ref[pl.ds(n, 1), :] + d_state
            return dSt

        lax.fori_loop(0, N // per_trip, trip, jnp.zeros((C_HEAD, C_HEAD), _F32))

    col = pl.BlockSpec((T, C_HEAD), lambda j: (0, j))
    chk = pl.BlockSpec((N, C_HEAD), lambda j: (0, j))
    one = pl.BlockSpec((1, C_HEAD), lambda j: (0, j))
    return pl.pallas_call(
        body, name=name, grid=(NH,),
        in_specs=[col, col, col, col, chk, pl.BlockSpec((None, N, C_HEAD, C_HEAD), lambda j: (j, 0, 0, 0)), col,
                  pl.BlockSpec((None, T, C_HEAD), lambda j: (3, 0, j)), one, col],
        out_specs=[col, col, col, col, col, chk, one],
        out_shape=[_S((T, D), _F32)] * 3 + [_S((T, D), _MM)] * 2 + [_S((N, D), _F32), _S((1, D), _F32)],
        scratch_shapes=[pltpu.VMEM((T, C_HEAD), _MM)],
        compiler_params=_params(("parallel",)),
    )(qd, kd, ke, vb, dec, st, o, h, ng, dy)


def _hgrn_prep_bwd(h, lb, dqd, dkd, dke, dv, dgate, ddec, name):
    _, T, D = h.shape
    tr = _tile(T, 512)
    tc = _tile(D, 256)
    nch = tr // C_CHUNK

    def body(q_ref, f_ref, lb_ref, dqd_ref, dkd_ref, dke_ref, dv_ref, dgate_ref, ddec_ref, dh_ref, dlb_ref):
        @pl.when(pl.program_id(1) == 0)
        def _():
            dlb_ref[...] = jnp.zeros_like(dlb_ref)

        q = q_ref[...]
        lb = lb_ref[...]
        sq, sf, f, rin, tot3, eb, enb, ekb = _hgrn_gates(q, f_ref[...], lb, tr, tc)
        kk = 1.0 - f
        dQd, dKd, dKe = dqd_ref[...], dkd_ref[...], dke_ref[...]
        tq = dQd * eb
        tkd = dKd * enb
        tke = dKe * ekb
        ke_term = tke * kk
        db = tq * (q * sq) - tkd * kk - ke_term
        dtot3 = (jnp.sum(ke_term.reshape(nch, C_CHUNK, tc), axis=1, keepdims=True)
                 + (ddec_ref[...] * jnp.exp(tot3).reshape(nch, tc)).reshape(nch, 1, tc))
        dlogf = (_chunk_revcumsum(db, rin).reshape(nch, C_CHUNK, tc) + dtot3).reshape(tr, tc)
        df = dlogf / f - (tkd + tke)
        dh_ref[0] = (tq * sq * (1.0 + q * (1.0 - sq))).astype(dh_ref.dtype)
        dh_ref[1] = (df * (1.0 - lb) * sf * (1.0 - sf)).astype(dh_ref.dtype)
        dh_ref[2] = dv_ref[...]
        dh_ref[3] = dgate_ref[...]
        dlb_ref[...] += jnp.sum(df * (1.0 - sf), axis=0, keepdims=True)

    def part(p):
        return pl.BlockSpec((None, tr, tc), lambda j, i: (p, i, j))

    blk = pl.BlockSpec((tr, tc), lambda j, i: (i, j))
    vec = pl.BlockSpec((1, tc), lambda j, i: (0, j))
    return pl.pallas_call(
        body, name=name, grid=(D // tc, T // tr),
        in_specs=[part(0), part(1), vec, blk, blk, blk, blk, blk, pl.BlockSpec((nch, tc), lambda j, i: (i, j))],
        out_specs=[pl.BlockSpec((4, tr, tc), lambda j, i: (0, i, j)), vec],
        out_shape=[_S((4, T, D), _MM), _S((1, D), _F32)],
        compiler_params=_params(("parallel", "arbitrary")),
    )(h, h, lb, dqd, dkd, dke, dv, dgate, ddec)


def _sum_in_device_order(me1, own, land, name):
    R, C = own.shape
    tr = _tile(R, 256)

    def body(me_ref, own_ref, land_ref, o_ref):
        me = me_ref[0]
        g = None
        for j in range(N_DEV):
            slot = jnp.maximum(jnp.bitwise_xor(me, j) - 1, 0)
            p = jnp.where(me == j, own_ref[...], land_ref[slot])
            g = p if g is None else g + p
        o_ref[...] = g

    return pl.pallas_call(
        body, name=name,
        grid_spec=pltpu.PrefetchScalarGridSpec(
            num_scalar_prefetch=1, grid=(R // tr,),
            in_specs=[pl.BlockSpec((tr, C), lambda i, me: (i, 0)), pl.BlockSpec((N_DEV - 1, tr, C), lambda i, me: (0, i, 0))],
            out_specs=pl.BlockSpec((tr, C), lambda i, me: (i, 0))),
        out_shape=_S((R, C), _F32), compiler_params=_params(("parallel",)),
    )(me1, own, land)


def _adamw(parts, w, m, v, name):
    P, R, C = parts.shape
    tr = _tile(R, 128) if R % LANE == 0 else R

    def body(p_ref, w_ref, m_ref, v_ref, g_ref, d_ref, nm_ref, nv_ref):
        g = p_ref[0].astype(_F32)
        for s in range(1, P):
            g = g + p_ref[s].astype(_F32)
        nm = ADAM_B1 * m_ref[...] + (1.0 - ADAM_B1) * g
        nv = ADAM_B2 * v_ref[...] + (1.0 - ADAM_B2) * (g * g)
        m_hat = nm / (1.0 - ADAM_B1 ** ADAM_STEP)
        v_hat = nv / (1.0 - ADAM_B2 ** ADAM_STEP)
        g_ref[...] = g
        d_ref[...] = -ADAM_LR * (m_hat / (jnp.sqrt(v_hat) + ADAM_EPS) + ADAM_WD * w_ref[...])
        nm_ref[...] = nm
        nv_ref[...] = nv

    blk = pl.BlockSpec((tr, C), lambda i: (i, 0))
    return pl.pallas_call(
        body, name=name, grid=(R // tr,), in_specs=[pl.BlockSpec((P, tr, C), lambda i: (0, i, 0)), blk, blk, blk],
        out_specs=[blk] * 4, out_shape=[_S((R, C), _F32)] * 4, compiler_params=_params(("parallel",)),
    )(parts, w, m, v)


def _exchange(name, srcs, out_shapes, jobs, deps=()):
    ns, nj = len(srcs), len(jobs)

    nd = len(deps)

    def body(*refs):
        ins, outs = refs[:ns], refs[ns + nd:ns + nd + len(out_shapes)]
        send_sems, recv_sems, local_sems = refs[-3:]
        x, y, c = lax.axis_index("x"), lax.axis_index("y"), lax.axis_index("c")
        me = 4 * x + 2 * y + c
        local = []
        for ji, (si, src_fn, di, dst_fn) in enumerate(jobs):
            cp = pltpu.make_async_copy(src_fn(ins[si], me, me), dst_fn(outs[di], me), local_sems.at[ji])
            cp.start()
            local.append(cp)
        remote = []
        for k in range(1, N_DEV):
            px, py, pc = (x + (k >> 2)) % 2, (y + ((k >> 1) & 1)) % 2, (c + (k & 1)) % 2
            to = 4 * px + 2 * py + pc
            for ji, (si, src_fn, di, dst_fn) in enumerate(jobs):
                sem = (k - 1) * nj + ji
                cp = pltpu.make_async_remote_copy(
                    src_ref=src_fn(ins[si], me, to), dst_ref=dst_fn(outs[di], me),
                    send_sem=send_sems.at[sem], recv_sem=recv_sems.at[sem],
                    device_id=(px, py, pc), device_id_type=pl.DeviceIdType.MESH)
                cp.start()
                remote.append(cp)
        for cp in remote:
            cp.wait_recv()
        for cp in remote:
            cp.wait_send()
        for cp in local:
            cp.wait()

    hbm = pl.BlockSpec(memory_space=pltpu.HBM)
    return pl.pallas_call(
        body, name=name, in_specs=[hbm] * ns + [_ANY] * nd, out_specs=[hbm] * len(out_shapes), out_shape=list(out_shapes),
        scratch_shapes=[pltpu.SemaphoreType.DMA(((N_DEV - 1) * nj,)), pltpu.SemaphoreType.DMA(((N_DEV - 1) * nj,)),
                        pltpu.SemaphoreType.DMA((nj,))],
    )(*srcs, *deps)


def _whole(ref, me, to):
    return ref


def _slot_job(i, o):
    def dst(ref, me):
        return ref.at[me]
    return (i, _whole, o, dst)


_HBM = pl.BlockSpec(memory_space=pltpu.HBM)
_SEM = pl.BlockSpec(memory_space=pltpu.SEMAPHORE)
_ANY = pl.BlockSpec(memory_space=pl.ANY)
_N_PEER = N_DEV - 1


def _split_params():
    return pltpu.CompilerParams(has_side_effects=pltpu.SideEffectType.DATAFLOW_SIDE_EFFECTING)


def _blk(ref, axis, n, idx):
    if axis is None:
        return ref
    return ref.at[tuple([slice(None)] * axis + [pl.ds(pl.multiple_of(idx * n, n), n)])]


def _peer(k):
    x, y, c = lax.axis_index("x"), lax.axis_index("y"), lax.axis_index("c")
    px, py, pc = (x + (k >> 2)) % 2, (y + ((k >> 1) & 1)) % 2, (c + (k & 1)) % 2
    return (px, py, pc), 4 * px + 2 * py + pc, 4 * x + 2 * y + c


def _row_tile(rows, pref):
    best = None
    for d in range(16, min(rows, pref) + 1, 16):
        if rows % d == 0:
            best = d
    return best if best is not None else rows


def _place(w, me1, axis, name, layer=None, deps=()):
    R, C = w.shape[-2:]
    tr = _row_tile(R, 512)
    nb = R // tr
    lead = () if layer is None else (None,)
    pre = () if layer is None else (layer,)

    def body(me_ref, w_ref, *rest):
        rest[-1][...] = w_ref[...].astype(rest[-1].dtype)

    if axis == 1:
        out_spec = pl.BlockSpec((tr, C), lambda i, me: (i, me[0]))
        out_shape = _S((R, N_DEV * C), _MM)
    else:
        out_spec = pl.BlockSpec((tr, C), lambda i, me: (me[0] * nb + i, 0))
        out_shape = _S((N_DEV * R, C), _MM)
    return pl.pallas_call(
        body, name=name,
        grid_spec=pltpu.PrefetchScalarGridSpec(
            num_scalar_prefetch=1, grid=(nb,),
            in_specs=[pl.BlockSpec(lead + (tr, C), lambda i, me: pre + (i, 0))] + [_ANY] * len(deps), out_specs=out_spec),
        out_shape=out_shape, compiler_params=_params(("parallel",)),
    )(me1, w, *deps)


_SIBLING = 1
_CHIPS = (2, 4, 6)
_VMEM_TOKEN = pl.BlockSpec(memory_space=pltpu.VMEM)


def _remote(ref_blk, send_sem, recv_sem, dev):
    return pltpu.make_async_remote_copy(src_ref=ref_blk, dst_ref=ref_blk, send_sem=send_sem, recv_sem=recv_sem,
                                        device_id=dev, device_id_type=pl.DeviceIdType.MESH)


def _gather_start(name, full, axis, n):
    def body(f_ref, send, recv, f_out, token):
        for i, k in enumerate((_SIBLING,) + _CHIPS):
            dev, _, me = _peer(k)
            _remote(_blk(f_ref, axis, n, me), send.at[i], recv.at[i], dev).start()
        token[...] = jnp.zeros_like(token)

    return pl.pallas_call(
        body, name=name,
        out_shape=(pltpu.SemaphoreType.DMA((4,)), pltpu.SemaphoreType.DMA((4,)), pltpu.HBM(full.shape, full.dtype),
                   _S((8, LANE), _F32)),
        in_specs=(_HBM,), out_specs=(_SEM, _SEM, _HBM, _VMEM_TOKEN),
        input_output_aliases={0: 2}, compiler_params=_split_params(),
    )(pltpu.with_memory_space_constraint(full, pltpu.HBM))


def _gather_forward(name, full, axis, n, recv, after):
    after = tuple(after) if isinstance(after, (tuple, list)) else (after,)

    def body(f_ref, recv_r, *rest):
        send2, recv2, f_out, token = rest[-4:]
        sib, _, _ = _peer(_SIBLING)
        for i, k in enumerate(_CHIPS):
            dev, frm, _ = _peer(k)
            blk = _blk(f_ref, axis, n, frm)
            _remote(blk, send2.at[i], recv_r.at[1 + i], dev).wait_recv()
            _remote(blk, send2.at[i], recv2.at[i], sib).start()
        token[...] = jnp.zeros_like(token)

    return pl.pallas_call(
        body, name=name,
        out_shape=(pltpu.SemaphoreType.DMA((3,)), pltpu.SemaphoreType.DMA((3,)), pltpu.HBM(full.shape, full.dtype),
                   _S((8, LANE), _F32)),
        in_specs=(_HBM, _SEM) + (_ANY,) * len(after), out_specs=(_SEM, _SEM, _HBM, _VMEM_TOKEN),
        input_output_aliases={0: 2}, compiler_params=_split_params(),
    )(full, recv, *after)


def _gather_wait(name, full, axis, n, send, recv, send2, recv2, after):
    def body(f_ref, send_r, recv_r, send2_r, recv2_r, after_ref, f_out):
        sib, _, me = _peer(_SIBLING)
        blk = _blk(f_ref, axis, n, me)
        for i in range(4):
            _remote(blk, send_r.at[i], recv_r.at[0], sib).wait_send()
        _remote(blk, send_r.at[0], recv_r.at[0], sib).wait_recv()
        for i in range(3):
            cp = _remote(blk, send2_r.at[i], recv2_r.at[i], sib)
            cp.wait_send()
            cp.wait_recv()

    return pl.pallas_call(
        body, name=name, out_shape=pltpu.HBM(full.shape, full.dtype),
        in_specs=(_HBM, _SEM, _SEM, _SEM, _SEM, _ANY), out_specs=_HBM,
        input_output_aliases={0: 0}, compiler_params=_split_params(),
    )(full, send, recv, send2, recv2, after)


def _scatter_start(name, dw, axis, n):
    shard = tuple(n if a == axis else d for a, d in enumerate(dw.shape))
    land = lax.empty((_N_PEER,) + shard, dw.dtype)

    def body(dw_ref, land_ref, send, recv, dw_out, land_out, token):
        for k in range(1, N_DEV):
            dev, to, _ = _peer(k)
            pltpu.make_async_remote_copy(
                src_ref=_blk(dw_ref, axis, n, to), dst_ref=land_ref.at[k - 1], send_sem=send.at[k - 1],
                recv_sem=recv.at[k - 1], device_id=dev, device_id_type=pl.DeviceIdType.MESH).start()
        token[...] = jnp.zeros_like(token)

    return pl.pallas_call(
        body, name=name,
        out_shape=(pltpu.SemaphoreType.DMA((_N_PEER,)), pltpu.SemaphoreType.DMA((_N_PEER,)),
                   pltpu.HBM(dw.shape, dw.dtype), pltpu.HBM(land.shape, land.dtype), _S((8, LANE), _F32)),
        in_specs=(_HBM, _HBM), out_specs=(_SEM, _SEM, _HBM, _HBM, pl.BlockSpec(memory_space=pltpu.VMEM)),
        input_output_aliases={0: 2, 1: 3}, compiler_params=_split_params(),
    )(pltpu.with_memory_space_constraint(dw, pltpu.HBM), pltpu.with_memory_space_constraint(land, pltpu.HBM))


def _scatter_wait(name, items, after):
    ne = len(items)
    after = tuple(after) if isinstance(after, (tuple, list)) else (after,)

    def body(*refs):
        for e, (_, _, _, _, axis, n) in enumerate(items):
            dw_ref, land_ref, send_r, recv_r = refs[4 * e:4 * e + 4]
            for k in range(1, N_DEV):
                dev, to, _ = _peer(k)
                cp = pltpu.make_async_remote_copy(
                    src_ref=_blk(dw_ref, axis, n, to), dst_ref=land_ref.at[k - 1], send_sem=send_r.at[k - 1],
                    recv_sem=recv_r.at[k - 1], device_id=dev, device_id_type=pl.DeviceIdType.MESH)
                cp.wait_send()
                cp.wait_recv()

    args, out_shape = [], []
    for dw, land, send, recv, _, _ in items:
        args += [dw, land, send, recv]
        out_shape += [pltpu.HBM(dw.shape, dw.dtype), pltpu.HBM(land.shape, land.dtype)]
    res = pl.pallas_call(
        body, name=name, out_shape=tuple(out_shape),
        in_specs=(_HBM, _HBM, _SEM, _SEM) * ne + (_ANY,) * len(after), out_specs=(_HBM,) * (2 * ne),
        input_output_aliases={4 * e + j: 2 * e + j for e in range(ne) for j in range(2)},
        compiler_params=_split_params(),
    )(*args, *after)
    return [(res[2 * e], res[2 * e + 1]) for e in range(ne)]


def _adamw_big(me1, dw, land, w, m, v, axis, n, name, layer=None, into=None):
    R, C = land.shape[1:]
    tr = _row_tile(R, 128)
    nb = R // tr
    lead = () if layer is None else (None,)
    pre = () if layer is None else (layer,)

    def body(me_ref, own_ref, land_ref, w_ref, m_ref, v_ref, *rest):
        g_ref, d_ref, nm_ref, nv_ref = rest[-4:]
        g = own_ref[...].astype(_F32)
        for s in range(_N_PEER):
            g = g + land_ref[s].astype(_F32)
        nm = ADAM_B1 * m_ref[...] + (1.0 - ADAM_B1) * g
        nv = ADAM_B2 * v_ref[...] + (1.0 - ADAM_B2) * (g * g)
        m_hat = nm / (1.0 - ADAM_B1 ** ADAM_STEP)
        v_hat = nv / (1.0 - ADAM_B2 ** ADAM_STEP)
        g_ref[...] = g
        d_ref[...] = -ADAM_LR * (m_hat / (jnp.sqrt(v_hat) + ADAM_EPS) + ADAM_WD * w_ref[...])
        nm_ref[...] = nm
        nv_ref[...] = nv

    if axis == 1:
        own_spec = pl.BlockSpec((tr, C), lambda i, me: (i, me[0]))
    else:
        own_spec = pl.BlockSpec((tr, C), lambda i, me: (me[0] * nb + i, 0))
    blk = pl.BlockSpec(lead + (tr, C), lambda i, me: pre + (i, 0))
    in_specs = [own_spec, pl.BlockSpec((_N_PEER, tr, C), lambda i, me: (0, i, 0)), blk, blk, blk]
    args = [me1, dw, land, w, m, v]
    aliases = {}
    if into is not None:
        in_specs += [_ANY] * 4
        aliases = {6 + j: j for j in range(4)}
        args += list(into)
    return pl.pallas_call(
        body, name=name,
        grid_spec=pltpu.PrefetchScalarGridSpec(num_scalar_prefetch=1, grid=(nb,), in_specs=in_specs, out_specs=[blk] * 4),
        out_shape=[_S(w.shape, _F32)] * 4, input_output_aliases=aliases, compiler_params=_params(("parallel",)),
    )(*args)


def _pack(arrs):
    flat = jnp.concatenate([a.reshape(-1).astype(_F32) for a in arrs])
    pad = (-flat.shape[0]) % (LANE * LANE)
    return jnp.pad(flat, (0, pad)).reshape(-1, LANE)


def _unpack(mat, shapes):
    flat = mat.reshape(-1)
    out, off = [], 0
    for s in shapes:
        n = 1
        for d in s:
            n *= d
        out.append(flat[off:off + n].reshape(s))
        off += n
    return out


def _lb_of(lb_param):
    lb_all = jnp.cumsum(jax.nn.softmax(lb_param.astype(_F32), axis=0), axis=0)
    return (lb_all - lb_all[0])[1:2]


def kernel(x, ev_w_in, ev_ln_v_g, ev_ln_v_b, ev_w_s, ev_b_s, ev_w_pool, ev_pool_scale, ev_w_out, od_w_in, od_norm_g, od_w_out, lb_param, ffn_w_up, ffn_conv_w, ffn_conv_b, ffn_w_down, ln1_g, ln1_b, ln2_g, ln2_b, loss_target, m_ev_w_in, m_ev_ln_v_g, m_ev_ln_v_b, m_ev_w_s, m_ev_b_s, m_ev_w_pool, m_ev_pool_scale, m_ev_w_out, m_od_w_in, m_od_norm_g, m_od_w_out, m_lb_param, m_ffn_w_up, m_ffn_conv_w, m_ffn_conv_b, m_ffn_w_down, m_ln1_g, m_ln1_b, m_ln2_g, m_ln2_b, v_ev_w_in, v_ev_ln_v_g, v_ev_ln_v_b, v_ev_w_s, v_ev_b_s, v_ev_w_pool, v_ev_pool_scale, v_ev_w_out, v_od_w_in, v_od_norm_g, v_od_w_out, v_lb_param, v_ffn_w_up, v_ffn_conv_w, v_ffn_conv_b, v_ffn_w_down, v_ln1_g, v_ln1_b, v_ln2_g, v_ln2_b):
    me = 4 * lax.axis_index("x") + 2 * lax.axis_index("y") + lax.axis_index("c")
    T, D = x.shape[1], x.shape[2]
    W = ev_ln_v_g.shape[1]
    H = W // A_HEAD
    Wg = W // B_GROUPS
    F2 = ffn_conv_b.shape[1]
    F = F2 // 2
    n_in0, n_out0 = ev_w_in.shape[2], ev_w_out.shape[1]
    n_in1, n_out1 = od_w_in.shape[2], od_w_out.shape[1]
    n_up, n_dn = ffn_w_up.shape[2], ffn_w_down.shape[1]
    n_pool, n_ng, n_cw = ev_w_pool.shape[2], od_norm_g.shape[1], ffn_conv_w.shape[2]

    small_shards = [od_norm_g, ffn_conv_w, ev_w_pool]
    small_pack = _pack(small_shards)
    small_all = _exchange("gather_small_params", [small_pack], [_S((N_DEV,) + small_pack.shape, _F32)], [_slot_job(0, 0)])[0]

    me1 = me.astype(jnp.int32).reshape(1)
    weights = [
        ("w_in0", ev_w_in[0], None, 1, n_in0), ("w_out0", ev_w_out[0], None, 0, n_out0),
        ("w_up0", ffn_w_up, 0, 1, n_up), ("w_dn0", ffn_w_down, 0, 0, n_dn),
        ("w_in1", od_w_in[0], None, 1, n_in1), ("w_out1", od_w_out[0], None, 0, n_out1),
        ("w_up1", ffn_w_up, 1, 1, n_up), ("w_dn1", ffn_w_down, 1, 0, n_dn),
    ]
    started, tokens = {}, [small_all]
    for key, w, layer, axis, n in weights:
        full = _place(w, me1, axis, "place_" + key, layer, deps=tokens)
        send, recv, full, token = _gather_start("gather_start_" + key, full, axis, n)
        started[key] = (full, axis, n, send, recv)
        tokens = [token]

    def pass_on(key, after):
        full, axis, n, send, recv = started[key]
        send2, recv2, full, token = _gather_forward("gather_forward_" + key, full, axis, n, recv, after)
        started[key] = (full, axis, n, send, recv, send2, recv2)
        return token

    def gathered(key, after):
        return _gather_wait("gather_wait_" + key, *started[key], after)

    ng_parts, cw_parts, wp_parts = [], [], []
    for j in range(N_DEV):
        a, b, c = _unpack(small_all[j], [s.shape for s in small_shards])
        ng_parts.append(a)
        cw_parts.append(b)
        wp_parts.append(c)
    norm_g = jnp.concatenate(ng_parts, axis=1)
    conv_w = jnp.concatenate(cw_parts, axis=2)
    w_pool = jnp.concatenate(wp_parts, axis=2)[0]
    cw_l = [conv_w[l].reshape(3, 2, F).transpose(1, 0, 2) for l in range(DEPTH)]
    cb_l = [ffn_conv_b[l].reshape(2, 1, F) for l in range(DEPTH)]
    ws_tril = jnp.tril(ev_w_s[0]).astype(_MM)
    bias = jnp.repeat(ev_b_s[0].T, A_HEAD, axis=1)
    wp_b = w_pool.astype(_MM)
    lb, lb_vjp = jax.vjp(_lb_of, lb_param)

    small_names = ["ev_ln_v_g", "ev_ln_v_b", "ev_w_s", "ev_b_s", "ev_w_pool", "ev_pool_scale", "od_norm_g", "lb_param",
                   "ffn_conv_w", "ffn_conv_b", "ln1_g", "ln1_b", "ln2_g", "ln2_b"]
    given = dict(ev_ln_v_g=(ev_ln_v_g, m_ev_ln_v_g, v_ev_ln_v_g), ev_ln_v_b=(ev_ln_v_b, m_ev_ln_v_b, v_ev_ln_v_b),
                 ev_w_s=(ev_w_s, m_ev_w_s, v_ev_w_s), ev_b_s=(ev_b_s, m_ev_b_s, v_ev_b_s),
                 ev_w_pool=(ev_w_pool, m_ev_w_pool, v_ev_w_pool),
                 ev_pool_scale=(ev_pool_scale, m_ev_pool_scale, v_ev_pool_scale),
                 od_norm_g=(od_norm_g, m_od_norm_g, v_od_norm_g), lb_param=(lb_param, m_lb_param, v_lb_param),
                 ffn_conv_w=(ffn_conv_w, m_ffn_conv_w, v_ffn_conv_w), ffn_conv_b=(ffn_conv_b, m_ffn_conv_b, v_ffn_conv_b),
                 ln1_g=(ln1_g, m_ln1_g, v_ln1_g), ln1_b=(ln1_b, m_ln1_b, v_ln1_b), ln2_g=(ln2_g, m_ln2_g, v_ln2_g),
                 ln2_b=(ln2_b, m_ln2_b, v_ln2_b))
    shard_axis = dict(ev_w_pool=2, od_norm_g=1, ffn_conv_w=2)
    rep_names = [n for n in small_names if n not in shard_axis]
    shd_names = [n for n in small_names if n in shard_axis]
    small_packs = [_pack([given[n][j] for n in small_names]) for j in range(3)]

    x2 = x[0]
    xb = _cast(x2, _MM, "cast_x", deps=[pass_on("w_in0", tokens[0])])
    w_in0 = gathered("w_in0", xb)
    h0 = _mm(xb, w_in0, "nn", _F32, "ev_in", out_parts=3)
    tie = pass_on("w_out0", h0)
    yab = _ev_mid_fwd(h0, ev_ln_v_g + tie[0, 0], ev_ln_v_b, ws_tril, bias, wp_b, ev_pool_scale, "ev_mid_fwd")
    w_out0 = gathered("w_out0", yab)
    z1 = _mm(yab, w_out0, "nn", _F32, "ev_out", add=x2, add_scale=ALPHA)
    tie = pass_on("w_up0", (z1, *small_packs))
    x1, x1b = _ln_fwd(z1, ln1_g[0:1] + tie[0, 0], ln1_b[0:1], "ln1_0")
    w_up0 = gathered("w_up0", x1b)
    hf0 = _mm(x1b, w_up0, "nn", _F32, "ffn_up", out_parts=2)
    tie = pass_on("w_dn0", hf0)
    act0, hc0 = _ffn_mid_fwd(hf0, cw_l[0], cb_l[0] + tie[0, 0], "ffn_mid_fwd")
    w_dn0 = gathered("w_dn0", act0)
    z2 = _mm(act0, w_dn0, "nn", _F32, "ffn_down", add=x1, add_scale=ALPHA)
    tie = pass_on("w_in1", z2)
    x2_, x2b = _ln_fwd(z2, ln2_g[0:1] + tie[0, 0], ln2_b[0:1], "ln2_0")
    w_in1 = gathered("w_in1", x2b)
    h1 = _mm(x2b, w_in1, "nn", _F32, "od_in", out_parts=4)
    qd, kd, ke, vb, dec = _hgrn_prep_fwd(h1, lb, "hgrn_prep_fwd")
    tie = pass_on("w_out1", qd)
    o, yo, st = _hgrn_scan_fwd(qd, kd, ke, vb, dec, h1, norm_g + tie[0, 0], "hgrn_scan_fwd")
    w_out1 = gathered("w_out1", yo)
    tie = pass_on("w_up1", yo)
    z3, x3, x3b = _mm(yo, w_out1, "nn", _F32, "od_out", add=x2_, add_scale=ALPHA, ln=(ln1_g[1:2], ln1_b[1:2]), deps=[tie])
    w_up1 = gathered("w_up1", x3b)
    hf1 = _mm(x3b, w_up1, "nn", _F32, "ffn_up", out_parts=2)
    tie = pass_on("w_dn1", hf1)
    act1, hc1 = _ffn_mid_fwd(hf1, cw_l[1], cb_l[1] + tie[0, 0], "ffn_mid_fwd")
    w_dn1 = gathered("w_dn1", act1)
    z4 = _mm(act1, w_dn1, "nn", _F32, "ffn_down", add=x3, add_scale=ALPHA)

    scat = {}

    def scatter(key, dw, axis, n):
        send, recv, dw, land, token = _scatter_start("scatter_start_" + key, dw, axis, n)
        scat[key] = (dw, land, send, recv, axis, n)
        return [token]

    loss11, dz4, dz4b, g_ln2_1, b_ln2_1 = _ln_loss_bwd(z4, ln2_g[1:2], ln2_b[1:2], loss_target[0], "ln_loss_bwd")
    tok = scatter("dn1", _mm(act1, dz4b, "tn", _XCH, "ffn_down_dw"), 0, n_dn)
    dact1 = _mm(dz4b, w_dn1, "nt", _MM, "ffn_down_dx", deps=tok)
    dhf1, dcw1, dcb1 = _ffn_mid_bwd(hf1, hc1, dact1, cw_l[1], "ffn_mid_bwd")
    tok = scatter("up1", _mm(x3b, dhf1, "tn", _XCH, "ffn_up_dw", b_parts=2, deps=tok), 1, n_up)
    dx3 = _mm(dhf1, w_up1, "nt", _F32, "ffn_up_dx", a_parts=2, add=dz4, add_scale=ALPHA, deps=tok)
    dz3, dz3b, g_ln1_1, b_ln1_1 = _ln_bwd(z3, ln1_g[1:2], dx3, "ln_bwd")
    tok = scatter("out1", _mm(yo, dz3b, "tn", _XCH, "od_out_dw", deps=tok), 0, n_out1)
    dyo = _mm(dz3b, w_out1, "nt", _F32, "od_out_dx", deps=tok)
    dqd, dkd, dke, dv, dgate, ddec, dng = _hgrn_scan_bwd(qd, kd, ke, vb, dec, st, o, h1, norm_g, dyo, "hgrn_scan_bwd")
    dh1, dlb = _hgrn_prep_bwd(h1, lb, dqd, dkd, dke, dv, dgate, ddec, "hgrn_prep_bwd")
    tok = scatter("in1", _mm(x2b, dh1, "tn", _XCH, "od_in_dw", b_parts=4, deps=tok), 1, n_in1)
    dx2 = _mm(dh1, w_in1, "nt", _F32, "od_in_dx", a_parts=4, add=dz3, add_scale=ALPHA, deps=tok)
    dz2, dz2b, g_ln2_0, b_ln2_0 = _ln_bwd(z2, ln2_g[0:1], dx2, "ln_bwd")
    tok = scatter("dn0", _mm(act0, dz2b, "tn", _XCH, "ffn_down_dw", deps=tok), 0, n_dn)
    dact0 = _mm(dz2b, w_dn0, "nt", _MM, "ffn_down_dx", deps=tok)
    dhf0, dcw0, dcb0 = _ffn_mid_bwd(hf0, hc0, dact0, cw_l[0], "ffn_mid_bwd")
    tok = scatter("up0", _mm(x1b, dhf0, "tn", _XCH, "ffn_up_dw", b_parts=2, deps=tok), 1, n_up)
    dx1 = _mm(dhf0, w_up0, "nt", _F32, "ffn_up_dx", a_parts=2, add=dz2, add_scale=ALPHA, deps=tok)
    dz1, dz1b, g_ln1_0, b_ln1_0 = _ln_bwd(z1, ln1_g[0:1], dx1, "ln_bwd")
    tok = scatter("out0", _mm(yab, dz1b, "tn", _XCH, "ev_out_dw", deps=tok), 0, n_out0)
    dyab = _mm(dz1b, w_out0, "nt", _F32, "ev_out_dx", deps=tok)
    dh0, dws, dbias, dlng, dlnb, dwp, dsc = _ev_mid_bwd(h0, dyab, ev_ln_v_g, ev_ln_v_b, ws_tril, bias, wp_b,
                                                        ev_pool_scale, "ev_mid_bwd")

    g_b_s = dbias.reshape(A_CHUNK, H, A_HEAD).sum(axis=-1).T[None]
    g_conv_w = jnp.stack([d.transpose(1, 0, 2).reshape(3, F2) for d in (dcw0, dcw1)])
    g_conv_b = jnp.stack([d.reshape(F2) for d in (dcb0, dcb1)])
    small_grads = dict(zip(small_names, [
        dlng, dlnb, dws[None], g_b_s, dwp[None], dsc, dng, lb_vjp(dlb)[0], g_conv_w, g_conv_b,
        jnp.concatenate([g_ln1_0, g_ln1_1]), jnp.concatenate([b_ln1_0, b_ln1_1]),
        jnp.concatenate([g_ln2_0, g_ln2_1]), jnp.concatenate([b_ln2_0, b_ln2_1])]))

    def by_device(g, ax):
        g = g.reshape(g.shape[:ax] + (N_DEV, g.shape[ax] // N_DEV) + g.shape[ax + 1:])
        return jnp.moveaxis(g, ax, 0).reshape(N_DEV, -1)

    shd = jnp.concatenate([by_device(small_grads[n], shard_axis[n]) for n in shd_names], axis=1)
    shd_pack = jnp.pad(shd, ((0, 0), (0, (-shd.shape[1]) % (LANE * LANE)))).reshape(-1, LANE)
    shd_rows = shd_pack.shape[0] // N_DEV
    rep_pack = _pack([small_grads[n] for n in rep_names])
    tok = scatter("in0", _mm(xb, dh0, "tn", _XCH, "ev_in_dw", deps=tok), 1, n_in0)
    tok = scatter("small_rep", rep_pack + tok[0][0, 0], None, None)
    tok = scatter("small_shd", shd_pack + tok[0][0, 0], 0, shd_rows)
    grad_x = _mm(dh0, w_in0, "nt", _F32, "ev_in_dx", add=dz1, add_scale=ALPHA, deps=tok)

    def landed(name, keys, after):
        got = _scatter_wait(name, [scat[k] for k in keys], after)
        return {k: (me1, dw, land) for k, (dw, land) in zip(keys, got)}

    early = landed("scatter_wait_early", ["dn1", "up1", "out1", "in1", "dn0", "up0", "out0"], grad_x)
    big = {}
    r_dn = _adamw_big(*early["dn1"], ffn_w_down, m_ffn_w_down, v_ffn_w_down, 0, n_dn, "adamw_w_dn1", layer=1)
    r_up = _adamw_big(*early["up1"], ffn_w_up, m_ffn_w_up, v_ffn_w_up, 1, n_up, "adamw_w_up1", layer=1)
    big["od_w_out"] = _adamw_big(*early["out1"], od_w_out[0], m_od_w_out[0], v_od_w_out[0], 0, n_out1, "adamw_w_out1")
    big["od_w_in"] = _adamw_big(*early["in1"], od_w_in[0], m_od_w_in[0], v_od_w_in[0], 1, n_in1, "adamw_w_in1")
    big["ffn_w_down"] = _adamw_big(*early["dn0"], ffn_w_down, m_ffn_w_down, v_ffn_w_down, 0, n_dn, "adamw_w_dn0", layer=0, into=r_dn)
    big["ffn_w_up"] = _adamw_big(*early["up0"], ffn_w_up, m_ffn_w_up, v_ffn_w_up, 1, n_up, "adamw_w_up0", layer=0, into=r_up)
    big["ev_w_out"] = _adamw_big(*early["out0"], ev_w_out[0], m_ev_w_out[0], v_ev_w_out[0], 0, n_out0, "adamw_w_out0")
    late = landed("scatter_wait_late", ["in0", "small_rep", "small_shd"],
                  (big["ffn_w_down"][0], big["ffn_w_up"][0], big["od_w_in"][0], big["ev_w_out"][0]))
    big["ev_w_in"] = _adamw_big(*late["in0"], ev_w_in[0], m_ev_w_in[0], v_ev_w_in[0], 1, n_in0, "adamw_w_in0")

    rep_mat = _sum_in_device_order(*late["small_rep"], "sum_small_rep")
    local_g = dict(zip(rep_names, _unpack(rep_mat, [small_grads[n].shape for n in rep_names])))
    _, shd_all, shd_land = late["small_shd"]
    shd_own = lax.dynamic_slice_in_dim(shd_all, me * shd_rows, shd_rows, axis=0)
    shd_mat = _sum_in_device_order(me1, shd_own, shd_land, "sum_small_shd")
    local_g.update(zip(shd_names, _unpack(shd_mat, [given[n][0].shape for n in shd_names])))
    local_shapes = [given[n][0].shape for n in small_names]
    res = _adamw(_pack([local_g[n] for n in small_names])[None], *small_packs, "adamw_small")
    small = {n: [] for n in small_names}
    for r in res:
        for n, a in zip(small_names, _unpack(r, local_shapes)):
            small[n].append(a)

    loss = lax.psum(loss11[0, 0], ("x", "y", "c"))
    order = ["ev_w_in", "ev_ln_v_g", "ev_ln_v_b", "ev_w_s", "ev_b_s", "ev_w_pool", "ev_pool_scale", "ev_w_out", "od_w_in",
             "od_norm_g", "od_w_out", "lb_param", "ffn_w_up", "ffn_conv_w", "ffn_conv_b", "ffn_w_down", "ln1_g", "ln1_b",
             "ln2_g", "ln2_b"]
    shapes = dict(ev_w_in=ev_w_in.shape, ev_w_out=ev_w_out.shape, od_w_in=od_w_in.shape, od_w_out=od_w_out.shape,
                  ffn_w_up=ffn_w_up.shape, ffn_w_down=ffn_w_down.shape)
    outs = [loss, grad_x[None]]
    for kind in range(4):
        for n in order:
            outs.append(big[n][kind].reshape(shapes[n]) if n in big else small[n][kind])
    return tuple(outs)
```

```python
import functools
import math

import jax
import jax.numpy as jnp
from jax import lax
from jax.experimental import pallas as pl
from jax.experimental.pallas import tpu as pltpu

_MM = jnp.bfloat16
_XCH = jnp.bfloat16

DEPTH = 2
ALPHA = (2 * DEPTH) ** 0.25
LN_EPS = 1e-5
A_CHUNK = 128
A_HEAD = 128
B_GROUPS = 4
POOL_HALO = 16
C_CHUNK = 64
C_HEAD = 128
SCAN_UNROLL = 64
CONV_HALO = 8
PACKED_ROWS = 16
LN_ROWS = 512
FFN_ROWS, FFN_FWD_COLS = 1024, 1408
FFN_BWD_ROWS, FFN_BWD_COLS = 512, 1408
FFN_CHUNK = 128
ADAM_LR, ADAM_B1, ADAM_B2, ADAM_EPS, ADAM_WD, ADAM_STEP = 0.001, 0.9, 0.999, 1e-08, 0.01, 10
N_DEV = 8
LANE = 128
VMEM_LIMIT = 56 * 1024 * 1024
MM_FULL_K = 3072
MM_FULL_K_TN = 4096
MM_DEEP_K = 2816
MM_LN_ROWS, MM_LN_K = 512, 1408

_F32 = jnp.float32
_NN = (((1,), (0,)), ((), ()))
_NT = (((1,), (1,)), ((), ()))
_TN = (((0,), (0,)), ((), ()))
_S = jax.ShapeDtypeStruct


def _dot(a, b, dims=_NN):
    return lax.dot_general(a, b, dims, preferred_element_type=_F32)


def _tile(dim, pref):
    best = None
    d = LANE
    while d <= min(dim, pref):
        if dim % d == 0:
            best = d
        d += LANE
    return best if best is not None else dim


def _params(sem):
    return pltpu.CompilerParams(dimension_semantics=sem, vmem_limit_bytes=VMEM_LIMIT)


def _sigmoid(x):
    return 0.5 * jnp.tanh(0.5 * x) + 0.5


def _sigmoid_rel(x):
    return 1.0 / (1.0 + jnp.exp(-x))


_GELU_C = 0.7978845608028654
_GELU_A = 0.044715


def _gelu_and_grad(x):
    t = jnp.tanh(_GELU_C * (x + _GELU_A * x * x * x))
    y = 0.5 * x * (1.0 + t)
    dy = 0.5 * (1.0 + t) + 0.5 * x * (1.0 - t * t) * _GELU_C * (1.0 + 3.0 * _GELU_A * x * x)
    return y, dy


def _row_index(n):
    return lax.broadcasted_iota(jnp.int32, (n, 1), 0)


def _mm_tiles(mode, M, N, K, with_add):
    if mode == "tn":
        return _tile(M, 1024), _tile(N, 1024), _tile(K, MM_FULL_K_TN)
    if K <= MM_FULL_K:
        return _tile(M, 1024 if with_add else 2048), _tile(N, 1024 if mode == "nn" else 512), K
    return _tile(M, 1024), _tile(N, 1024), _tile(K, MM_DEEP_K)


def _mm(a, b, mode, out_dtype, name, *, a_parts=1, b_parts=1, out_parts=1, add=None, add_scale=1.0, deps=(), tiles=None,
        ln=None):
    if mode == "nn":
        M, K = a.shape
        N = b.shape[1]
    elif mode == "nt":
        if a_parts > 1:
            M, K = a.shape[1], a.shape[2] * a_parts
        else:
            M, K = a.shape
        N = b.shape[0]
    else:
        K, M = a.shape
        N = b.shape[-1] * b_parts
    tm, tn, tk = tiles if tiles is not None else _mm_tiles(mode, M, N // max(b_parts, out_parts), K // a_parts, add is not None)
    if ln is not None:
        tm, tn, tk = _tile(M, MM_LN_ROWS), N, (K if K <= MM_FULL_K else _tile(K, MM_LN_K))
    nk = K // tk
    npj = (N // max(b_parts, out_parts)) // tn
    nkp = (K // a_parts) // tk
    if mode == "nn":
        a_spec = pl.BlockSpec((tm, tk), lambda i, j, k: (i, k))
        b_spec = pl.BlockSpec((tk, tn), lambda i, j, k: (k, j))
        dims = _NN
    elif mode == "nt":
        if a_parts > 1:
            a_spec = pl.BlockSpec((None, tm, tk), lambda i, j, k: (k // nkp, i, k % nkp))
        else:
            a_spec = pl.BlockSpec((tm, tk), lambda i, j, k: (i, k))
        b_spec = pl.BlockSpec((tn, tk), lambda i, j, k: (j, k))
        dims = _NT
    else:
        a_spec = pl.BlockSpec((tk, tm), lambda i, j, k: (k, i))
        if b_parts > 1:
            b_spec = pl.BlockSpec((None, tk, tn), lambda i, j, k: (j // npj, k, j % npj))
        else:
            b_spec = pl.BlockSpec((tk, tn), lambda i, j, k: (k, j))
        dims = _TN
    if out_parts > 1:
        out_spec = pl.BlockSpec((None, tm, tn), lambda i, j, k: (j // npj, i, j % npj))
        out_shape = _S((out_parts, M, N // out_parts), out_dtype)
    else:
        out_spec = pl.BlockSpec((tm, tn), lambda i, j, k: (i, j))
        out_shape = _S((M, N), out_dtype)
    in_specs = [a_spec, b_spec]
    args = [a, b]
    if add is not None:
        in_specs.append(pl.BlockSpec((tm, tn), lambda i, j, k: (i, j)))
        args.append(add)
    if ln is not None:
        vec = pl.BlockSpec((1, N), lambda i, j, k: (0, 0))
        in_specs += [vec, vec]
        args += list(ln)
        out_spec = [out_spec] * 3
        out_shape = [_S((M, N), _F32), _S((M, N), _F32), _S((M, N), _MM)]
    n_ln = 2 + (add is not None)
    in_specs += [_ANY] * len(deps)
    args += list(deps)
    n_out = 1 if ln is None else 3

    def finish(r, refs, outs):
        if add is not None:
            r = r + add_scale * refs[2][...]
        if ln is None:
            outs[0][...] = r.astype(outs[0].dtype)
            return
        mu = jnp.mean(r, axis=-1, keepdims=True)
        rc = r - mu
        y = rc * lax.rsqrt(jnp.mean(rc * rc, axis=-1, keepdims=True) + LN_EPS) * refs[n_ln][...] + refs[n_ln + 1][...]
        outs[0][...] = r
        outs[1][...] = y
        outs[2][...] = y.astype(outs[2].dtype)

    def body_one(*refs):
        finish(_dot(refs[0][...], refs[1][...], dims), refs, refs[len(refs) - n_out:])

    def body_acc(*refs):
        acc = refs[-1]
        k = pl.program_id(2)

        @pl.when(k == 0)
        def _():
            acc[...] = jnp.zeros_like(acc)

        acc[...] += _dot(refs[0][...], refs[1][...], dims)

        @pl.when(k == nk - 1)
        def _():
            finish(acc[...], refs, refs[len(refs) - 1 - n_out:len(refs) - 1])

    return pl.pallas_call(
        body_one if nk == 1 else body_acc, name=name, grid=(M // tm, N // tn, nk), in_specs=in_specs,
        out_specs=out_spec, out_shape=out_shape,
        scratch_shapes=[] if nk == 1 else [pltpu.VMEM((tm, tn), _F32)],
        compiler_params=_params(("parallel", "parallel", "arbitrary")),
    )(*args)


def _cast(x2d, dtype, name, deps=()):
    R, C = x2d.shape
    tr = _tile(R, 512) if R % LANE == 0 else R

    def body(x_ref, *rest):
        rest[-1][...] = x_ref[...].astype(rest[-1].dtype)

    return pl.pallas_call(
        body, name=name, grid=(R // tr,), in_specs=[pl.BlockSpec((tr, C), lambda i: (i, 0))] + [_ANY] * len(deps),
        out_specs=pl.BlockSpec((tr, C), lambda i: (i, 0)), out_shape=_S((R, C), dtype),
        compiler_params=_params(("parallel",)),
    )(x2d, *deps)


def _ln_fwd(z, g, b, name):
    T, D = z.shape
    tr = _tile(T, LN_ROWS)

    def body(z_ref, g_ref, b_ref, y_ref, yb_ref):
        zz = z_ref[...]
        mu = jnp.mean(zz, axis=-1, keepdims=True)
        zc = zz - mu
        var = jnp.mean(zc * zc, axis=-1, keepdims=True)
        y = zc * lax.rsqrt(var + LN_EPS) * g_ref[...] + b_ref[...]
        y_ref[...] = y
        yb_ref[...] = y.astype(yb_ref.dtype)

    row = pl.BlockSpec((tr, D), lambda i: (i, 0))
    vec = pl.BlockSpec((1, D), lambda i: (0, 0))
    return pl.pallas_call(
        body, name=name, grid=(T // tr,), in_specs=[row, vec, vec], out_specs=[row, row],
        out_shape=[_S((T, D), _F32), _S((T, D), _MM)], compiler_params=_params(("parallel",)),
    )(z, g, b)


def _ln_bwd(z, g, dy, name):
    T, D = z.shape
    tr = _tile(T, LN_ROWS)

    def body(z_ref, g_ref, dy_ref, dz_ref, dzb_ref, dg_ref, db_ref):
        @pl.when(pl.program_id(0) == 0)
        def _():
            dg_ref[...] = jnp.zeros_like(dg_ref)
            db_ref[...] = jnp.zeros_like(db_ref)

        zz = z_ref[...]
        mu = jnp.mean(zz, axis=-1, keepdims=True)
        zc = zz - mu
        rstd = lax.rsqrt(jnp.mean(zc * zc, axis=-1, keepdims=True) + LN_EPS)
        xh = zc * rstd
        d = dy_ref[...]
        dg_ref[...] += jnp.sum(d * xh, axis=0, keepdims=True)
        db_ref[...] += jnp.sum(d, axis=0, keepdims=True)
        dxh = d * g_ref[...]
        dz = rstd * (dxh - jnp.mean(dxh, axis=-1, keepdims=True) - xh * jnp.mean(dxh * xh, axis=-1, keepdims=True))
        dz_ref[...] = dz
        dzb_ref[...] = dz.astype(dzb_ref.dtype)

    row = pl.BlockSpec((tr, D), lambda i: (i, 0))
    vec = pl.BlockSpec((1, D), lambda i: (0, 0))
    return pl.pallas_call(
        body, name=name, grid=(T // tr,), in_specs=[row, vec, row], out_specs=[row, row, vec, vec],
        out_shape=[_S((T, D), _F32), _S((T, D), _MM), _S((1, D), _F32), _S((1, D), _F32)],
        compiler_params=_params(("arbitrary",)),
    )(z, g, dy)


def _ln_loss_bwd(z, g, b, target, name):
    T, D = z.shape
    tr = _tile(T, LN_ROWS)

    def body(z_ref, g_ref, b_ref, t_ref, loss_ref, dz_ref, dzb_ref, dg_ref, db_ref, lacc):
        i = pl.program_id(0)

        @pl.when(i == 0)
        def _():
            dg_ref[...] = jnp.zeros_like(dg_ref)
            db_ref[...] = jnp.zeros_like(db_ref)
            lacc[...] = jnp.zeros_like(lacc)

        zz = z_ref[...]
        mu = jnp.mean(zz, axis=-1, keepdims=True)
        zc = zz - mu
        rstd = lax.rsqrt(jnp.mean(zc * zc, axis=-1, keepdims=True) + LN_EPS)
        xh = zc * rstd
        err = xh * g_ref[...] + b_ref[...] - t_ref[...]
        lacc[...] += jnp.sum(err * err, axis=0, keepdims=True)
        d = err * (1.0 / D)
        dg_ref[...] += jnp.sum(d * xh, axis=0, keepdims=True)
        db_ref[...] += jnp.sum(d, axis=0, keepdims=True)
        dxh = d * g_ref[...]
        dz = rstd * (dxh - jnp.mean(dxh, axis=-1, keepdims=True) - xh * jnp.mean(dxh * xh, axis=-1, keepdims=True))
        dz_ref[...] = dz
        dzb_ref[...] = dz.astype(dzb_ref.dtype)

        @pl.when(i == pl.num_programs(0) - 1)
        def _():
            loss_ref[...] = jnp.sum(lacc[...], axis=-1, keepdims=True) * (0.5 / D)

    row = pl.BlockSpec((tr, D), lambda i: (i, 0))
    vec = pl.BlockSpec((1, D), lambda i: (0, 0))
    one = pl.BlockSpec((1, 1), lambda i: (0, 0))
    return pl.pallas_call(
        body, name=name, grid=(T // tr,), in_specs=[row, vec, vec, row], out_specs=[one, row, row, vec, vec],
        out_shape=[_S((1, 1), _F32), _S((T, D), _F32), _S((T, D), _MM), _S((1, D), _F32), _S((1, D), _F32)],
        scratch_shapes=[pltpu.VMEM((1, D), _F32)], compiler_params=_params(("arbitrary",)),
    )(z, g, b, target)


def _conv3(X, cw, cb):
    return cb + cw[2:3] * X + cw[1:2] * pltpu.roll(X, 1, 0) + cw[0:1] * pltpu.roll(X, 2, 0)


def _ffn_mid_fwd(h, cw, cb, name):
    _, T, F = h.shape
    tr = _tile(T, FFN_ROWS)
    tc = _tile(F, FFN_FWD_COLS)
    nb = tr // CONV_HALO

    rc = _tile(tr, FFN_CHUNK)
    lanes = [slice(cs * LANE, (cs + 1) * LANE) for cs in range(tc // LANE)]

    def body(h_ref, p_ref, cw_ref, cb_ref, o_ref, c_ref):
        i = pl.program_id(0)

        def work(r0, cols, X):
            hc = [_conv3(X[part], cw_ref[part, :, cols], cb_ref[part, :, cols])[CONV_HALO:] for part in range(2)]
            for part in range(2):
                c_ref[part, pl.ds(r0, rc), cols] = hc[part].astype(c_ref.dtype)
            a, v = hc
            o_ref[pl.ds(r0, rc), cols] = (a * _sigmoid(a) * v).astype(o_ref.dtype)

        for cols in lanes:
            work(0, cols, [jnp.concatenate([jnp.where(i == 0, 0.0, p_ref[part, :, cols]), h_ref[part, 0:rc, cols]], axis=0)
                           for part in range(2)])

        def chunk(c, carry):
            r0 = pl.multiple_of(c * rc, rc)
            for cols in lanes:
                work(r0, cols, [h_ref[part, pl.ds(r0 - CONV_HALO, rc + CONV_HALO), cols] for part in range(2)])
            return carry

        lax.fori_loop(1, tr // rc, chunk, 0)

    return pl.pallas_call(
        body, name=name, grid=(T // tr, F // tc),
        in_specs=[pl.BlockSpec((2, tr, tc), lambda i, j: (0, i, j)),
                  pl.BlockSpec((2, CONV_HALO, tc), lambda i, j: (0, jnp.maximum(i * nb - 1, 0), j)),
                  pl.BlockSpec((2, 3, tc), lambda i, j: (0, 0, j)),
                  pl.BlockSpec((2, 1, tc), lambda i, j: (0, 0, j))],
        out_specs=[pl.BlockSpec((tr, tc), lambda i, j: (i, j)), pl.BlockSpec((2, tr, tc), lambda i, j: (0, i, j))],
        out_shape=[_S((T, F), _MM), _S((2, T, F), _MM)],
        compiler_params=_params(("parallel", "parallel")),
    )(h, h, cw, cb)


def _ffn_mid_bwd(h, hc, dact, cw, name):
    _, T, F = h.shape
    tr = _tile(T, FFN_BWD_ROWS)
    tc = _tile(F, FFN_BWD_COLS)
    nb_c = tr // PACKED_ROWS
    rc = _tile(tr, FFN_CHUNK)
    n = rc + CONV_HALO

    def body(h_ref, c_ref, cn_ref, d_ref, dn_ref, cw_ref, dh_ref, dcw_ref, dcb_ref):
        i = pl.program_id(1)
        is_last = i == pl.num_programs(1) - 1

        @pl.when(i == 0)
        def _():
            dcw_ref[...] = jnp.zeros_like(dcw_ref)
            dcb_ref[...] = jnp.zeros_like(dcb_ref)

        def work(r0, cols, a, v, D):
            sg = _sigmoid(a)
            dhc = [D * v * sg * (1.0 + a * (1.0 - sg)), D * a * sg]
            for part in range(2):
                X = h_ref[part, pl.ds(r0, rc), cols]
                cwp = cw_ref[part, :, cols]
                dh = None
                for k in range(3):
                    g = (dhc[part] if k == 0 else pltpu.roll(dhc[part], n - k, 0))[0:rc]
                    term = cwp[2 - k:3 - k] * g
                    dh = term if dh is None else dh + term
                    dcw_ref[part, 2 - k:3 - k, cols] += jnp.sum(g * X, axis=0, keepdims=True)
                    if k == 0:
                        dcb_ref[part, :, cols] += jnp.sum(g, axis=0, keepdims=True)
                dh_ref[part, pl.ds(r0, rc), cols] = dh.astype(dh_ref.dtype)

        lanes = [slice(cs * LANE, (cs + 1) * LANE) for cs in range(tc // LANE)]

        def chunk(c, carry):
            r0 = pl.multiple_of(c * rc, rc)
            for cols in lanes:
                a, v = [c_ref[part, pl.ds(r0, rc + PACKED_ROWS), cols].astype(_F32)[0:n] for part in range(2)]
                work(r0, cols, a, v, d_ref[pl.ds(r0, rc + PACKED_ROWS), cols].astype(_F32)[0:n])
            return carry

        lax.fori_loop(0, tr // rc - 1, chunk, 0)
        r0 = tr - rc
        for cols in lanes:
            a, v = [jnp.concatenate([c_ref[part, r0:tr, cols].astype(_F32), cn_ref[part, :, cols].astype(_F32)[0:CONV_HALO]],
                                    axis=0) for part in range(2)]
            D = jnp.concatenate([d_ref[r0:tr, cols].astype(_F32),
                                 jnp.where(is_last, 0.0, dn_ref[:, cols].astype(_F32)[0:CONV_HALO])], axis=0)
            work(r0, cols, a, v, D)

    return pl.pallas_call(
        body, name=name, grid=(F // tc, T // tr),
        in_specs=[pl.BlockSpec((2, tr, tc), lambda j, i: (0, i, j)),
                  pl.BlockSpec((2, tr, tc), lambda j, i: (0, i, j)),
                  pl.BlockSpec((2, PACKED_ROWS, tc), lambda j, i: (0, jnp.minimum((i + 1) * nb_c, T // PACKED_ROWS - 1), j)),
                  pl.BlockSpec((tr, tc), lambda j, i: (i, j)),
                  pl.BlockSpec((PACKED_ROWS, tc), lambda j, i: (jnp.minimum((i + 1) * nb_c, T // PACKED_ROWS - 1), j)),
                  pl.BlockSpec((2, 3, tc), lambda j, i: (0, 0, j))],
        out_specs=[pl.BlockSpec((2, tr, tc), lambda j, i: (0, i, j)),
                   pl.BlockSpec((2, 3, tc), lambda j, i: (0, 0, j)),
                   pl.BlockSpec((2, 1, tc), lambda j, i: (0, 0, j))],
        out_shape=[_S((2, T, F), _MM), _S((2, 3, F), _F32), _S((2, 1, F), _F32)],
        compiler_params=_params(("parallel", "arbitrary")),
    )(h, hc, hc, dact, dact, cw)


def _ev_common(h_ref, hp_ref, lng_ref, lnb_ref, ws_ref, bias_ref, i, tr, W):
    H = W // A_HEAD
    u, gu = _gelu_and_grad(h_ref[0])
    v, gv = _gelu_and_grad(h_ref[1])
    mu = jnp.mean(v, axis=-1, keepdims=True)
    vc = v - mu
    rstd = lax.rsqrt(jnp.mean(vc * vc, axis=-1, keepdims=True) + LN_EPS)
    vhat = vc * rstd
    vb = (vhat * lng_ref[...] + lnb_ref[...]).astype(_MM)
    s_chunks = []
    for c in range(tr // A_CHUNK):
        r0 = c * A_CHUNK
        heads = [_dot(ws_ref[hd], vb[r0:r0 + A_CHUNK, hd * A_HEAD:(hd + 1) * A_HEAD]) for hd in range(H)]
        s_chunks.append(jnp.concatenate(heads, axis=1) + bias_ref[...])
    prev = jnp.where(i == 0, 0.0, hp_ref[...])
    X = jnp.concatenate([prev, h_ref[2]], axis=0)
    return u, gu, gv, rstd, vhat, vb, s_chunks, X


def _pool_inv_count(i, tr, rows, win):
    pos = i * tr + _row_index(rows) + 1
    return 1.0 / jnp.minimum(pos, win).astype(_F32)


def _pool_fwd(X, g, Wg, i, tr):
    xg = X[:, g * Wg:(g + 1) * Wg]
    s = xg
    for k in range(g + 1):
        s = s + pltpu.roll(s, 2 ** k, 0)
    return s[POOL_HALO:] * _pool_inv_count(i, tr, tr, 2 ** (g + 1)) - xg[POOL_HALO:]


def _ev_mid_fwd(h, lng, lnb, ws, bias, wp, sc, name):
    _, T, W = h.shape
    tr = _tile(T, 256)
    H = W // A_HEAD
    Wg = W // B_GROUPS
    nb = tr // POOL_HALO

    def body(h_ref, hp_ref, lng_ref, lnb_ref, ws_ref, bias_ref, wp_ref, sc_ref, o_ref):
        i = pl.program_id(0)
        u, _, _, _, _, _, s_chunks, X = _ev_common(h_ref, hp_ref, lng_ref, lnb_ref, ws_ref, bias_ref, i, tr, W)
        for c, s in enumerate(s_chunks):
            r0 = c * A_CHUNK
            o_ref[r0:r0 + A_CHUNK, 0:W] = (u[r0:r0 + A_CHUNK] * s).astype(o_ref.dtype)
        for g in range(B_GROUPS):
            p = _pool_fwd(X, g, Wg, i, tr)
            y = _dot(p.astype(_MM), wp_ref[g]) * sc_ref[:, g * Wg:(g + 1) * Wg]
            o_ref[:, W + g * Wg:W + (g + 1) * Wg] = y.astype(o_ref.dtype)

    vec = pl.BlockSpec((1, W), lambda i: (0, 0))
    return pl.pallas_call(
        body, name=name, grid=(T // tr,),
        in_specs=[pl.BlockSpec((3, tr, W), lambda i: (0, i, 0)),
                  pl.BlockSpec((None, POOL_HALO, W), lambda i: (2, jnp.maximum(i * nb - 1, 0), 0)),
                  vec, vec,
                  pl.BlockSpec((H, A_CHUNK, A_CHUNK), lambda i: (0, 0, 0)),
                  pl.BlockSpec((A_CHUNK, W), lambda i: (0, 0)),
                  pl.BlockSpec((B_GROUPS, Wg, Wg), lambda i: (0, 0, 0)),
                  vec],
        out_specs=pl.BlockSpec((tr, 2 * W), lambda i: (i, 0)), out_shape=_S((T, 2 * W), _MM),
        compiler_params=_params(("parallel",)),
    )(h, h, lng, lnb, ws, bias, wp, sc)


def _ev_mid_bwd(h, dy, lng, lnb, ws, bias, wp, sc, name):
    _, T, W = h.shape
    tr = _tile(T, 256)
    H = W // A_HEAD
    Wg = W // B_GROUPS
    nb = tr // POOL_HALO
    last_blk = T // POOL_HALO - 1
    n = tr + POOL_HALO

    def body(h_ref, hp_ref, dy_ref, dyn_ref, lng_ref, lnb_ref, ws_ref, bias_ref, wp_ref, sc_ref,
             dh_ref, dws_ref, dbias_ref, dlng_ref, dlnb_ref, dwp_ref, dsc_ref):
        i = pl.program_id(0)

        @pl.when(i == 0)
        def _():
            for r in (dws_ref, dbias_ref, dlng_ref, dlnb_ref, dwp_ref, dsc_ref):
                r[...] = jnp.zeros_like(r)

        u, gu, gv, rstd, vhat, vb, s_chunks, X = _ev_common(h_ref, hp_ref, lng_ref, lnb_ref, ws_ref, bias_ref, i, tr, W)
        rr = lax.broadcasted_iota(jnp.int32, (A_CHUNK, A_CHUNK), 0)
        cc = lax.broadcasted_iota(jnp.int32, (A_CHUNK, A_CHUNK), 1)
        tril = rr >= cc
        du_chunks, dvln_chunks = [], []
        for c, s in enumerate(s_chunks):
            r0 = c * A_CHUNK
            dya = dy_ref[r0:r0 + A_CHUNK, 0:W]
            du_chunks.append(dya * s)
            ds = dya * u[r0:r0 + A_CHUNK]
            dbias_ref[...] += ds
            dsb = ds.astype(_MM)
            heads = []
            for hd in range(H):
                cols = slice(hd * A_HEAD, (hd + 1) * A_HEAD)
                dws_ref[hd] += jnp.where(tril, _dot(dsb[:, cols], vb[r0:r0 + A_CHUNK, cols], _NT), 0.0)
                heads.append(_dot(ws_ref[hd], dsb[:, cols], _TN))
            dvln_chunks.append(jnp.concatenate(heads, axis=1))
        du = jnp.concatenate(du_chunks, axis=0)
        dvln = jnp.concatenate(dvln_chunks, axis=0)
        dlng_ref[...] += jnp.sum(dvln * vhat, axis=0, keepdims=True)
        dlnb_ref[...] += jnp.sum(dvln, axis=0, keepdims=True)
        dxh = dvln * lng_ref[...]
        dv = rstd * (dxh - jnp.mean(dxh, axis=-1, keepdims=True) - vhat * jnp.mean(dxh * vhat, axis=-1, keepdims=True))
        dh_ref[:, 0:W] = (du * gu).astype(dh_ref.dtype)
        dh_ref[:, W:2 * W] = (dv * gv).astype(dh_ref.dtype)

        dyb = dy_ref[:, W:2 * W]
        dyb_full = jnp.concatenate([dyb, jnp.where(i == pl.num_programs(0) - 1, 0.0, dyn_ref[...])], axis=0)
        for g in range(B_GROUPS):
            cols = slice(g * Wg, (g + 1) * Wg)
            pb = _pool_fwd(X, g, Wg, i, tr).astype(_MM)
            ypre = _dot(pb, wp_ref[g])
            dsc_ref[:, cols] += jnp.sum(dyb[:, cols] * ypre, axis=0, keepdims=True)
            dyp = (dyb_full[:, cols] * sc_ref[:, cols]).astype(_MM)
            dwp_ref[g] += _dot(pb, dyp[0:tr], _TN)
            dp = _dot(dyp, wp_ref[g], _NT)
            s = dp * _pool_inv_count(i, tr, n, 2 ** (g + 1))
            for k in range(g + 1):
                s = s + pltpu.roll(s, n - 2 ** k, 0)
            dh_ref[:, 2 * W + g * Wg:2 * W + (g + 1) * Wg] = (s[0:tr] - dp[0:tr]).astype(dh_ref.dtype)

    vec = pl.BlockSpec((1, W), lambda i: (0, 0))
    ws_spec = pl.BlockSpec((H, A_CHUNK, A_CHUNK), lambda i: (0, 0, 0))
    bias_spec = pl.BlockSpec((A_CHUNK, W), lambda i: (0, 0))
    wp_spec = pl.BlockSpec((B_GROUPS, Wg, Wg), lambda i: (0, 0, 0))
    return pl.pallas_call(
        body, name=name, grid=(T // tr,),
        in_specs=[pl.BlockSpec((3, tr, W), lambda i: (0, i, 0)),
                  pl.BlockSpec((None, POOL_HALO, W), lambda i: (2, jnp.maximum(i * nb - 1, 0), 0)),
                  pl.BlockSpec((tr, 2 * W), lambda i: (i, 0)),
                  pl.BlockSpec((POOL_HALO, W), lambda i: (jnp.minimum((i + 1) * nb, last_blk), 1)),
                  vec, vec, ws_spec, bias_spec, wp_spec, vec],
        out_specs=[pl.BlockSpec((tr, 3 * W), lambda i: (i, 0)), ws_spec, bias_spec, vec, vec, wp_spec, vec],
        out_shape=[_S((T, 3 * W), _MM), _S((H, A_CHUNK, A_CHUNK), _F32), _S((A_CHUNK, W), _F32), _S((1, W), _F32),
                   _S((1, W), _F32), _S((B_GROUPS, Wg, Wg), _F32), _S((1, W), _F32)],
        compiler_params=_params(("arbitrary",)),
    )(h, h, dy, dy, lng, lnb, ws, bias, wp, sc)


def _chunk_cumsum(x, rin):
    s = 1
    while s < C_CHUNK:
        x = x + jnp.where(rin >= s, pltpu.roll(x, s, 0), 0.0)
        s *= 2
    return x


def _chunk_revcumsum(x, rin):
    n = x.shape[0]
    s = 1
    while s < C_CHUNK:
        x = x + jnp.where(rin + s < C_CHUNK, pltpu.roll(x, n - s, 0), 0.0)
        s *= 2
    return x


def _hgrn_gates(q, fl, lb, tr, tc):
    nch = tr // C_CHUNK
    sq = _sigmoid(q)
    sf = _sigmoid_rel(fl)
    f = lb + (1.0 - lb) * sf
    logf = jnp.log(f)
    rin = _row_index(tr) % C_CHUNK
    b = _chunk_cumsum(logf, rin)
    tot3 = jnp.sum(logf.reshape(nch, C_CHUNK, tc), axis=1, keepdims=True)
    eb = jnp.exp(b)
    enb = jnp.exp(-b)
    ekb = jnp.exp(tot3 - b.reshape(nch, C_CHUNK, tc)).reshape(tr, tc)
    return sq, sf, f, rin, tot3, eb, enb, ekb


def _hgrn_prep_fwd(h, lb, name):
    _, T, D = h.shape
    tr = _tile(T, 512)
    tc = _tile(D, 512)
    nch = tr // C_CHUNK

    def body(q_ref, f_ref, v_ref, lb_ref, qd_ref, kd_ref, ke_ref, vb_ref, dec_ref):
        q = q_ref[...]
        sq, _, f, _, tot3, eb, enb, ekb = _hgrn_gates(q, f_ref[...], lb_ref[...], tr, tc)
        kk = 1.0 - f
        qd_ref[...] = (q * sq * eb).astype(qd_ref.dtype)
        kd_ref[...] = (kk * enb).astype(kd_ref.dtype)
        ke_ref[...] = (kk * ekb).astype(ke_ref.dtype)
        vb_ref[...] = v_ref[...].astype(vb_ref.dtype)
        dec_ref[...] = jnp.exp(tot3).reshape(nch, tc)

    def part(p):
        return pl.BlockSpec((None, tr, tc), lambda i, j: (p, i, j))

    blk = pl.BlockSpec((tr, tc), lambda i, j: (i, j))
    return pl.pallas_call(
        body, name=name, grid=(T // tr, D // tc),
        in_specs=[part(0), part(1), part(2), pl.BlockSpec((1, tc), lambda i, j: (0, j))],
        out_specs=[blk, blk, blk, blk, pl.BlockSpec((nch, tc), lambda i, j: (i, j))],
        out_shape=[_S((T, D), _MM)] * 4 + [_S((T // C_CHUNK, D), _F32)],
        compiler_params=_params(("parallel", "parallel")),
    )(h, h, h, lb)


def _tril_mask():
    rr = lax.broadcasted_iota(jnp.int32, (C_CHUNK, C_CHUNK), 0)
    cc = lax.broadcasted_iota(jnp.int32, (C_CHUNK, C_CHUNK), 1)
    return rr >= cc


def _hgrn_scan_fwd(qd, kd, ke, vb, dec, h, ng, name):
    T, D = qd.shape
    NH = D // C_HEAD
    N = T // C_CHUNK

    def body(qd_ref, kd_ref, ke_ref, vb_ref, dec_ref, g_ref, ng_ref, o_ref, y_ref, st_ref):
        mask = _tril_mask()

        per_trip = math.gcd(N, SCAN_UNROLL)

        def trip(i, St):
            ahead = []
            for u in range(per_trip):
                n = i * per_trip + u
                r = pl.ds(pl.multiple_of(n * C_CHUNK, C_CHUNK), C_CHUNK)
                Qd, Kd, Ke, V = qd_ref[r, :], kd_ref[r, :], ke_ref[r, :], vb_ref[r, :]
                att = jnp.where(mask, _dot(Qd, Kd, _NT), 0.0).astype(_MM)
                ahead.append((n, r, _dot(att, V), _dot(V, Ke, _TN)))
            for n, r, o_intra, update in ahead:
                o_ref[r, :] = o_intra + _dot(qd_ref[r, :], St.astype(_MM), _NT)
                st_ref[n] = St
                St = St * dec_ref[pl.ds(n, 1), :] + update
            return St

        lax.fori_loop(0, N // per_trip, trip, jnp.zeros((C_HEAD, C_HEAD), _F32))
        o = o_ref[...]
        r = lax.rsqrt(jnp.mean(o * o, axis=-1, keepdims=True) + LN_EPS)
        y_ref[...] = (o * r * ng_ref[...] * _sigmoid(g_ref[...])).astype(y_ref.dtype)

    col = pl.BlockSpec((T, C_HEAD), lambda j: (0, j))
    return pl.pallas_call(
        body, name=name, grid=(NH,),
        in_specs=[col, col, col, col, pl.BlockSpec((N, C_HEAD), lambda j: (0, j)),
                  pl.BlockSpec((None, T, C_HEAD), lambda j: (3, 0, j)), pl.BlockSpec((1, C_HEAD), lambda j: (0, j))],
        out_specs=[col, col, pl.BlockSpec((None, N, C_HEAD, C_HEAD), lambda j: (j, 0, 0, 0))],
        out_shape=[_S((T, D), _F32), _S((T, D), _MM), _S((NH, N, C_HEAD, C_HEAD), _F32)],
        compiler_params=_params(("parallel",)),
    )(qd, kd, ke, vb, dec, h, ng)


def _hgrn_scan_bwd(qd, kd, ke, vb, dec, st, o, h, ng, dy, name):
    T, D = qd.shape
    NH = D // C_HEAD
    N = T // C_CHUNK

    def body(qd_ref, kd_ref, ke_ref, vb_ref, dec_ref, st_ref, o_ref, g_ref, ng_ref, dy_ref,
             dqd_ref, dkd_ref, dke_ref, dv_ref, dgate_ref, ddec_ref, dng_ref, do_s):
        o = o_ref[...]
        r = lax.rsqrt(jnp.mean(o * o, axis=-1, keepdims=True) + LN_EPS)
        oh = o * r
        gn = ng_ref[...]
        sg = _sigmoid(g_ref[...])
        d = dy_ref[...]
        dyn = d * sg
        dgate_ref[...] = (d * oh * gn * sg * (1.0 - sg)).astype(dgate_ref.dtype)
        dng_ref[...] = jnp.sum(dyn * oh, axis=0, keepdims=True)
        doh = dyn * gn
        do_s[...] = (r * (doh - oh * jnp.mean(doh * oh, axis=-1, keepdims=True))).astype(do_s.dtype)
        mask = _tril_mask()

        per_trip = math.gcd(N, SCAN_UNROLL)

        def trip(i, dSt):
            ahead = []
            for u in range(per_trip):
                n = N - 1 - (i * per_trip + u)
                rws = pl.ds(pl.multiple_of(n * C_CHUNK, C_CHUNK), C_CHUNK)
                Qd, Kd, V, dO = qd_ref[rws, :], kd_ref[rws, :], vb_ref[rws, :], do_s[rws, :]
                att = jnp.where(mask, _dot(Qd, Kd, _NT), 0.0).astype(_MM)
                dA = jnp.where(mask, _dot(dO, V, _NT), 0.0).astype(_MM)
                dqd_ref[rws, :] = _dot(dA, Kd) + _dot(dO, st_ref[n].astype(_MM))
                dkd_ref[rws, :] = _dot(dA, Qd, _TN)
                ahead.append((n, rws, _dot(att, dO, _TN), _dot(dO, Qd, _TN)))
            for n, rws, dv_intra, d_state in ahead:
                dStb = dSt.astype(_MM)
                dv_ref[rws, :] = (dv_intra + _dot(ke_ref[rws, :], dStb, _NT)).astype(dv_ref.dtype)
                dke_ref[rws, :] = _dot(vb_ref[rws, :], dStb)
                ddec_ref[pl.ds(n, 1), :] = jnp.sum(dSt * st_ref[n], axis=0, keepdims=True)
                dSt = dSt * dec_ref[pl.ds(n, 1), :] + d_state
            return dSt

        lax.fori_loop(0, N // per_trip, trip, jnp.zeros((C_HEAD, C_HEAD), _F32))

    col = pl.BlockSpec((T, C_HEAD), lambda j: (0, j))
    chk = pl.BlockSpec((N, C_HEAD), lambda j: (0, j))
    one = pl.BlockSpec((1, C_HEAD), lambda j: (0, j))
    return pl.pallas_call(
        body, name=name, grid=(NH,),
        in_specs=[col, col, col, col, chk, pl.BlockSpec((None, N, C_HEAD, C_HEAD), lambda j: (j, 0, 0, 0)), col,
                  pl.BlockSpec((None, T, C_HEAD), lambda j: (3, 0, j)), one, col],
        out_specs=[col, col, col, col, col, chk, one],
        out_shape=[_S((T, D), _F32)] * 3 + [_S((T, D), _MM)] * 2 + [_S((N, D), _F32), _S((1, D), _F32)],
        scratch_shapes=[pltpu.VMEM((T, C_HEAD), _MM)],
        compiler_params=_params(("parallel",)),
    )(qd, kd, ke, vb, dec, st, o, h, ng, dy)


def _hgrn_prep_bwd(h, lb, dqd, dkd, dke, dv, dgate, ddec, name):
    _, T, D = h.shape
    tr = _tile(T, 512)
    tc = _tile(D, 256)
    nch = tr // C_CHUNK

    def body(q_ref, f_ref, lb_ref, dqd_ref, dkd_ref, dke_ref, dv_ref, dgate_ref, ddec_ref, dh_ref, dlb_ref):
        @pl.when(pl.program_id(1) == 0)
        def _():
            dlb_ref[...] = jnp.zeros_like(dlb_ref)

        q = q_ref[...]
        lb = lb_ref[...]
        sq, sf, f, rin, tot3, eb, enb, ekb = _hgrn_gates(q, f_ref[...], lb, tr, tc)
        kk = 1.0 - f
        dQd, dKd, dKe = dqd_ref[...], dkd_ref[...], dke_ref[...]
        tq = dQd * eb
        tkd = dKd * enb
        tke = dKe * ekb
        ke_term = tke * kk
        db = tq * (q * sq) - tkd * kk - ke_term
        dtot3 = (jnp.sum(ke_term.reshape(nch, C_CHUNK, tc), axis=1, keepdims=True)
                 + (ddec_ref[...] * jnp.exp(tot3).reshape(nch, tc)).reshape(nch, 1, tc))
        dlogf = (_chunk_revcumsum(db, rin).reshape(nch, C_CHUNK, tc) + dtot3).reshape(tr, tc)
        df = dlogf / f - (tkd + tke)
        dh_ref[0] = (tq * sq * (1.0 + q * (1.0 - sq))).astype(dh_ref.dtype)
        dh_ref[1] = (df * (1.0 - lb) * sf * (1.0 - sf)).astype(dh_ref.dtype)
        dh_ref[2] = dv_ref[...]
        dh_ref[3] = dgate_ref[...]
        dlb_ref[...] += jnp.sum(df * (1.0 - sf), axis=0, keepdims=True)

    def part(p):
        return pl.BlockSpec((None, tr, tc), lambda j, i: (p, i, j))

    blk = pl.BlockSpec((tr, tc), lambda j, i: (i, j))
    vec = pl.BlockSpec((1, tc), lambda j, i: (0, j))
    return pl.pallas_call(
        body, name=name, grid=(D // tc, T // tr),
        in_specs=[part(0), part(1), vec, blk, blk, blk, blk, blk, pl.BlockSpec((nch, tc), lambda j, i: (i, j))],
        out_specs=[pl.BlockSpec((4, tr, tc), lambda j, i: (0, i, j)), vec],
        out_shape=[_S((4, T, D), _MM), _S((1, D), _F32)],
        compiler_params=_params(("parallel", "arbitrary")),
    )(h, h, lb, dqd, dkd, dke, dv, dgate, ddec)


def _sum_in_device_order(me1, own, land, name):
    R, C = own.shape
    tr = _tile(R, 256)

    def body(me_ref, own_ref, land_ref, o_ref):
        me = me_ref[0]
        g = None
        for j in range(N_DEV):
            slot = jnp.maximum(jnp.bitwise_xor(me, j) - 1, 0)
            p = jnp.where(me == j, own_ref[...], land_ref[slot])
            g = p if g is None else g + p
        o_ref[...] = g

    return pl.pallas_call(
        body, name=name,
        grid_spec=pltpu.PrefetchScalarGridSpec(
            num_scalar_prefetch=1, grid=(R // tr,),
            in_specs=[pl.BlockSpec((tr, C), lambda i, me: (i, 0)), pl.BlockSpec((N_DEV - 1, tr, C), lambda i, me: (0, i, 0))],
            out_specs=pl.BlockSpec((tr, C), lambda i, me: (i, 0))),
        out_shape=_S((R, C), _F32), compiler_params=_params(("parallel",)),
    )(me1, own, land)


def _adamw(parts, w, m, v, name):
    P, R, C = parts.shape
    tr = _tile(R, 128) if R % LANE == 0 else R

    def body(p_ref, w_ref, m_ref, v_ref, g_ref, d_ref, nm_ref, nv_ref):
        g = p_ref[0].astype(_F32)
        for s in range(1, P):
            g = g + p_ref[s].astype(_F32)
        nm = ADAM_B1 * m_ref[...] + (1.0 - ADAM_B1) * g
        nv = ADAM_B2 * v_ref[...] + (1.0 - ADAM_B2) * (g * g)
        m_hat = nm / (1.0 - ADAM_B1 ** ADAM_STEP)
        v_hat = nv / (1.0 - ADAM_B2 ** ADAM_STEP)
        g_ref[...] = g
        d_ref[...] = -ADAM_LR * (m_hat / (jnp.sqrt(v_hat) + ADAM_EPS) + ADAM_WD * w_ref[...])
        nm_ref[...] = nm
        nv_ref[...] = nv

    blk = pl.BlockSpec((tr, C), lambda i: (i, 0))
    return pl.pallas_call(
        body, name=name, grid=(R // tr,), in_specs=[pl.BlockSpec((P, tr, C), lambda i: (0, i, 0)), blk, blk, blk],
        out_specs=[blk] * 4, out_shape=[_S((R, C), _F32)] * 4, compiler_params=_params(("parallel",)),
    )(parts, w, m, v)


def _exchange(name, srcs, out_shapes, jobs, deps=()):
    ns, nj = len(srcs), len(jobs)

    nd = len(deps)

    def body(*refs):
        ins, outs = refs[:ns], refs[ns + nd:ns + nd + len(out_shapes)]
        send_sems, recv_sems, local_sems = refs[-3:]
        x, y, c = lax.axis_index("x"), lax.axis_index("y"), lax.axis_index("c")
        me = 4 * x + 2 * y + c
        local = []
        for ji, (si, src_fn, di, dst_fn) in enumerate(jobs):
            cp = pltpu.make_async_copy(src_fn(ins[si], me, me), dst_fn(outs[di], me), local_sems.at[ji])
            cp.start()
            local.append(cp)
        remote = []
        for k in range(1, N_DEV):
            px, py, pc = (x + (k >> 2)) % 2, (y + ((k >> 1) & 1)) % 2, (c + (k & 1)) % 2
            to = 4 * px + 2 * py + pc
            for ji, (si, src_fn, di, dst_fn) in enumerate(jobs):
                sem = (k - 1) * nj + ji
                cp = pltpu.make_async_remote_copy(
                    src_ref=src_fn(ins[si], me, to), dst_ref=dst_fn(outs[di], me),
                    send_sem=send_sems.at[sem], recv_sem=recv_sems.at[sem],
                    device_id=(px, py, pc), device_id_type=pl.DeviceIdType.MESH)
                cp.start()
                remote.append(cp)
        for cp in remote:
            cp.wait_recv()
        for cp in remote:
            cp.wait_send()
        for cp in local:
            cp.wait()

    hbm = pl.BlockSpec(memory_space=pltpu.HBM)
    return pl.pallas_call(
        body, name=name, in_specs=[hbm] * ns + [_ANY] * nd, out_specs=[hbm] * len(out_shapes), out_shape=list(out_shapes),
        scratch_shapes=[pltpu.SemaphoreType.DMA(((N_DEV - 1) * nj,)), pltpu.SemaphoreType.DMA(((N_DEV - 1) * nj,)),
                        pltpu.SemaphoreType.DMA((nj,))],
    )(*srcs, *deps)


def _whole(ref, me, to):
    return ref


def _slot_job(i, o):
    def dst(ref, me):
        return ref.at[me]
    return (i, _whole, o, dst)


_HBM = pl.BlockSpec(memory_space=pltpu.HBM)
_SEM = pl.BlockSpec(memory_space=pltpu.SEMAPHORE)
_ANY = pl.BlockSpec(memory_space=pl.ANY)
_N_PEER = N_DEV - 1


def _split_params():
    return pltpu.CompilerParams(has_side_effects=pltpu.SideEffectType.DATAFLOW_SIDE_EFFECTING)


def _blk(ref, axis, n, idx):
    if axis is None:
        return ref
    return ref.at[tuple([slice(None)] * axis + [pl.ds(pl.multiple_of(idx * n, n), n)])]


def _peer(k):
    x, y, c = lax.axis_index("x"), lax.axis_index("y"), lax.axis_index("c")
    px, py, pc = (x + (k >> 2)) % 2, (y + ((k >> 1) & 1)) % 2, (c + (k & 1)) % 2
    return (px, py, pc), 4 * px + 2 * py + pc, 4 * x + 2 * y + c


def _row_tile(rows, pref):
    best = None
    for d in range(16, min(rows, pref) + 1, 16):
        if rows % d == 0:
            best = d
    return best if best is not None else rows


def _place(w, me1, axis, name, layer=None, deps=()):
    R, C = w.shape[-2:]
    tr = _row_tile(R, 512)
    nb = R // tr
    lead = () if layer is None else (None,)
    pre = () if layer is None else (layer,)

    def body(me_ref, w_ref, *rest):
        rest[-1][...] = w_ref[...].astype(rest[-1].dtype)

    if axis == 1:
        out_spec = pl.BlockSpec((tr, C), lambda i, me: (i, me[0]))
        out_shape = _S((R, N_DEV * C), _MM)
    else:
        out_spec = pl.BlockSpec((tr, C), lambda i, me: (me[0] * nb + i, 0))
        out_shape = _S((N_DEV * R, C), _MM)
    return pl.pallas_call(
        body, name=name,
        grid_spec=pltpu.PrefetchScalarGridSpec(
            num_scalar_prefetch=1, grid=(nb,),
            in_specs=[pl.BlockSpec(lead + (tr, C), lambda i, me: pre + (i, 0))] + [_ANY] * len(deps), out_specs=out_spec),
        out_shape=out_shape, compiler_params=_params(("parallel",)),
    )(me1, w, *deps)


_SIBLING = 1
_CHIPS = (2, 4, 6)
_VMEM_TOKEN = pl.BlockSpec(memory_space=pltpu.VMEM)


def _remote(ref_blk, send_sem, recv_sem, dev):
    return pltpu.make_async_remote_copy(src_ref=ref_blk, dst_ref=ref_blk, send_sem=send_sem, recv_sem=recv_sem,
                                        device_id=dev, device_id_type=pl.DeviceIdType.MESH)


def _gather_start(name, full, axis, n):
    def body(f_ref, send, recv, f_out, token):
        for i, k in enumerate((_SIBLING,) + _CHIPS):
            dev, _, me = _peer(k)
            _remote(_blk(f_ref, axis, n, me), send.at[i], recv.at[i], dev).start()
        token[...] = jnp.zeros_like(token)

    return pl.pallas_call(
        body, name=name,
        out_shape=(pltpu.SemaphoreType.DMA((4,)), pltpu.SemaphoreType.DMA((4,)), pltpu.HBM(full.shape, full.dtype),
                   _S((8, LANE), _F32)),
        in_specs=(_HBM,), out_specs=(_SEM, _SEM, _HBM, _VMEM_TOKEN),
        input_output_aliases={0: 2}, compiler_params=_split_params(),
    )(pltpu.with_memory_space_constraint(full, pltpu.HBM))


def _gather_forward(name, full, axis, n, recv, after):
    after = tuple(after) if isinstance(after, (tuple, list)) else (after,)

    def body(f_ref, recv_r, *rest):
        send2, recv2, f_out, token = rest[-4:]
        sib, _, _ = _peer(_SIBLING)
        for i, k in enumerate(_CHIPS):
            dev, frm, _ = _peer(k)
            blk = _blk(f_ref, axis, n, frm)
            _remote(blk, send2.at[i], recv_r.at[1 + i], dev).wait_recv()
            _remote(blk, send2.at[i], recv2.at[i], sib).start()
        token[...] = jnp.zeros_like(token)

    return pl.pallas_call(
        body, name=name,
        out_shape=(pltpu.SemaphoreType.DMA((3,)), pltpu.SemaphoreType.DMA((3,)), pltpu.HBM(full.shape, full.dtype),
                   _S((8, LANE), _F32)),
        in_specs=(_HBM, _SEM) + (_ANY,) * len(after), out_specs=(_SEM, _SEM, _HBM, _VMEM_TOKEN),
        input_output_aliases={0: 2}, compiler_params=_split_params(),
    )(full, recv, *after)


def _gather_wait(name, full, axis, n, send, recv, send2, recv2, after):
    def body(f_ref, send_r, recv_r, send2_r, recv2_r, after_ref, f_out):
        sib, _, me = _peer(_SIBLING)
        blk = _blk(f_ref, axis, n, me)
        for i in range(4):
            _remote(blk, send_r.at[i], recv_r.at[0], sib).wait_send()
        _remote(blk, send_r.at[0], recv_r.at[0], sib).wait_recv()
        for i in range(3):
            cp = _remote(blk, send2_r.at[i], recv2_r.at[i], sib)
            cp.wait_send()
            cp.wait_recv()

    return pl.pallas_call(
        body, name=name, out_shape=pltpu.HBM(full.shape, full.dtype),
        in_specs=(_HBM, _SEM, _SEM, _SEM, _SEM, _ANY), out_specs=_HBM,
        input_output_aliases={0: 0}, compiler_params=_split_params(),
    )(full, send, recv, send2, recv2, after)


def _scatter_start(name, dw, axis, n):
    shard = tuple(n if a == axis else d for a, d in enumerate(dw.shape))
    land = lax.empty((_N_PEER,) + shard, dw.dtype)

    def body(dw_ref, land_ref, send, recv, dw_out, land_out, token):
        for k in range(1, N_DEV):
            dev, to, _ = _peer(k)
            pltpu.make_async_remote_copy(
                src_ref=_blk(dw_ref, axis, n, to), dst_ref=land_ref.at[k - 1], send_sem=send.at[k - 1],
                recv_sem=recv.at[k - 1], device_id=dev, device_id_type=pl.DeviceIdType.MESH).start()
        token[...] = jnp.zeros_like(token)

    return pl.pallas_call(
        body, name=name,
        out_shape=(pltpu.SemaphoreType.DMA((_N_PEER,)), pltpu.SemaphoreType.DMA((_N_PEER,)),
                   pltpu.HBM(dw.shape, dw.dtype), pltpu.HBM(land.shape, land.dtype), _S((8, LANE), _F32)),
        in_specs=(_HBM, _HBM), out_specs=(_SEM, _SEM, _HBM, _HBM, pl.BlockSpec(memory_space=pltpu.VMEM)),
        input_output_aliases={0: 2, 1: 3}, compiler_params=_split_params(),
    )(pltpu.with_memory_space_constraint(dw, pltpu.HBM), pltpu.with_memory_space_constraint(land, pltpu.HBM))


def _scatter_wait(name, items, after):
    ne = len(items)
    after = tuple(after) if isinstance(after, (tuple, list)) else (after,)

    def body(*refs):
        for e, (_, _, _, _, axis, n) in enumerate(items):
            dw_ref, land_ref, send_r, recv_r = refs[4 * e:4 * e + 4]
            for k in range(1, N_DEV):
                dev, to, _ = _peer(k)
                cp = pltpu.make_async_remote_copy(
                    src_ref=_blk(dw_ref, axis, n, to), dst_ref=land_ref.at[k - 1], send_sem=send_r.at[k - 1],
                    recv_sem=recv_r.at[k - 1], device_id=dev, device_id_type=pl.DeviceIdType.MESH)
                cp.wait_send()
                cp.wait_recv()

    args, out_shape = [], []
    for dw, land, send, recv, _, _ in items:
        args += [dw, land, send, recv]
        out_shape += [pltpu.HBM(dw.shape, dw.dtype), pltpu.HBM(land.shape, land.dtype)]
    res = pl.pallas_call(
        body, name=name, out_shape=tuple(out_shape),
        in_specs=(_HBM, _HBM, _SEM, _SEM) * ne + (_ANY,) * len(after), out_specs=(_HBM,) * (2 * ne),
        input_output_aliases={4 * e + j: 2 * e + j for e in range(ne) for j in range(2)},
        compiler_params=_split_params(),
    )(*args, *after)
    return [(res[2 * e], res[2 * e + 1]) for e in range(ne)]


def _adamw_big(me1, dw, land, w, m, v, axis, n, name, layer=None, into=None):
    R, C = land.shape[1:]
    tr = _row_tile(R, 256)
    nb = R // tr
    lead = () if layer is None else (None,)
    pre = () if layer is None else (layer,)

    def body(me_ref, own_ref, land_ref, w_ref, m_ref, v_ref, *rest):
        g_ref, d_ref, nm_ref, nv_ref = rest[-4:]
        g = own_ref[...].astype(_F32)
        for s in range(_N_PEER):
            g = g + land_ref[s].astype(_F32)
        nm = ADAM_B1 * m_ref[...] + (1.0 - ADAM_B1) * g
        nv = ADAM_B2 * v_ref[...] + (1.0 - ADAM_B2) * (g * g)
        m_hat = nm / (1.0 - ADAM_B1 ** ADAM_STEP)
        v_hat = nv / (1.0 - ADAM_B2 ** ADAM_STEP)
        g_ref[...] = g
        d_ref[...] = -ADAM_LR * (m_hat / (jnp.sqrt(v_hat) + ADAM_EPS) + ADAM_WD * w_ref[...])
        nm_ref[...] = nm
        nv_ref[...] = nv

    if axis == 1:
        own_spec = pl.BlockSpec((tr, C), lambda i, me: (i, me[0]))
    else:
        own_spec = pl.BlockSpec((tr, C), lambda i, me: (me[0] * nb + i, 0))
    blk = pl.BlockSpec(lead + (tr, C), lambda i, me: pre + (i, 0))
    in_specs = [own_spec, pl.BlockSpec((_N_PEER, tr, C), lambda i, me: (0, i, 0)), blk, blk, blk]
    args = [me1, dw, land, w, m, v]
    aliases = {}
    if into is not None:
        in_specs += [_ANY] * 4
        aliases = {6 + j: j for j in range(4)}
        args += list(into)
    return pl.pallas_call(
        body, name=name,
        grid_spec=pltpu.PrefetchScalarGridSpec(num_scalar_prefetch=1, grid=(nb,), in_specs=in_specs, out_specs=[blk] * 4),
        out_shape=[_S(w.shape, _F32)] * 4, input_output_aliases=aliases, compiler_params=_params(("parallel",)),
    )(*args)


def _pack(arrs):
    flat = jnp.concatenate([a.reshape(-1).astype(_F32) for a in arrs])
    pad = (-flat.shape[0]) % (LANE * LANE)
    return jnp.pad(flat, (0, pad)).reshape(-1, LANE)


def _unpack(mat, shapes):
    flat = mat.reshape(-1)
    out, off = [], 0
    for s in shapes:
        n = 1
        for d in s:
            n *= d
        out.append(flat[off:off + n].reshape(s))
        off += n
    return out


def _lb_of(lb_param):
    lb_all = jnp.cumsum(jax.nn.softmax(lb_param.astype(_F32), axis=0), axis=0)
    return (lb_all - lb_all[0])[1:2]


def kernel(x, ev_w_in, ev_ln_v_g, ev_ln_v_b, ev_w_s, ev_b_s, ev_w_pool, ev_pool_scale, ev_w_out, od_w_in, od_norm_g, od_w_out, lb_param, ffn_w_up, ffn_conv_w, ffn_conv_b, ffn_w_down, ln1_g, ln1_b, ln2_g, ln2_b, loss_target, m_ev_w_in, m_ev_ln_v_g, m_ev_ln_v_b, m_ev_w_s, m_ev_b_s, m_ev_w_pool, m_ev_pool_scale, m_ev_w_out, m_od_w_in, m_od_norm_g, m_od_w_out, m_lb_param, m_ffn_w_up, m_ffn_conv_w, m_ffn_conv_b, m_ffn_w_down, m_ln1_g, m_ln1_b, m_ln2_g, m_ln2_b, v_ev_w_in, v_ev_ln_v_g, v_ev_ln_v_b, v_ev_w_s, v_ev_b_s, v_ev_w_pool, v_ev_pool_scale, v_ev_w_out, v_od_w_in, v_od_norm_g, v_od_w_out, v_lb_param, v_ffn_w_up, v_ffn_conv_w, v_ffn_conv_b, v_ffn_w_down, v_ln1_g, v_ln1_b, v_ln2_g, v_ln2_b):
    me = 4 * lax.axis_index("x") + 2 * lax.axis_index("y") + lax.axis_index("c")
    T, D = x.shape[1], x.shape[2]
    W = ev_ln_v_g.shape[1]
    H = W // A_HEAD
    Wg = W // B_GROUPS
    F2 = ffn_conv_b.shape[1]
    F = F2 // 2
    n_in0, n_out0 = ev_w_in.shape[2], ev_w_out.shape[1]
    n_in1, n_out1 = od_w_in.shape[2], od_w_out.shape[1]
    n_up, n_dn = ffn_w_up.shape[2], ffn_w_down.shape[1]
    n_pool, n_ng, n_cw = ev_w_pool.shape[2], od_norm_g.shape[1], ffn_conv_w.shape[2]

    small_shards = [od_norm_g, ffn_conv_w, ev_w_pool]
    small_pack = _pack(small_shards)
    small_all = _exchange("gather_small_params", [small_pack], [_S((N_DEV,) + small_pack.shape, _F32)], [_slot_job(0, 0)])[0]

    me1 = me.astype(jnp.int32).reshape(1)
    weights = [
        ("w_in0", ev_w_in[0], None, 1, n_in0), ("w_out0", ev_w_out[0], None, 0, n_out0),
        ("w_up0", ffn_w_up, 0, 1, n_up), ("w_dn0", ffn_w_down, 0, 0, n_dn),
        ("w_in1", od_w_in[0], None, 1, n_in1), ("w_out1", od_w_out[0], None, 0, n_out1),
        ("w_up1", ffn_w_up, 1, 1, n_up), ("w_dn1", ffn_w_down, 1, 0, n_dn),
    ]
    started, tokens = {}, [small_all]
    for key, w, layer, axis, n in weights:
        full = _place(w, me1, axis, "place_" + key, layer, deps=tokens)
        send, recv, full, token = _gather_start("gather_start_" + key, full, axis, n)
        started[key] = (full, axis, n, send, recv)
        tokens = [token]

    def pass_on(key, after):
        full, axis, n, send, recv = started[key]
        send2, recv2, full, token = _gather_forward("gather_forward_" + key, full, axis, n, recv, after)
        started[key] = (full, axis, n, send, recv, send2, recv2)
        return token

    def gathered(key, after):
        return _gather_wait("gather_wait_" + key, *started[key], after)

    ng_parts, cw_parts, wp_parts = [], [], []
    for j in range(N_DEV):
        a, b, c = _unpack(small_all[j], [s.shape for s in small_shards])
        ng_parts.append(a)
        cw_parts.append(b)
        wp_parts.append(c)
    norm_g = jnp.concatenate(ng_parts, axis=1)
    conv_w = jnp.concatenate(cw_parts, axis=2)
    w_pool = jnp.concatenate(wp_parts, axis=2)[0]
    cw_l = [conv_w[l].reshape(3, 2, F).transpose(1, 0, 2) for l in range(DEPTH)]
    cb_l = [ffn_conv_b[l].reshape(2, 1, F) for l in range(DEPTH)]
    ws_tril = jnp.tril(ev_w_s[0]).astype(_MM)
    bias = jnp.repeat(ev_b_s[0].T, A_HEAD, axis=1)
    wp_b = w_pool.astype(_MM)
    lb, lb_vjp = jax.vjp(_lb_of, lb_param)

    small_names = ["ev_ln_v_g", "ev_ln_v_b", "ev_w_s", "ev_b_s", "ev_w_pool", "ev_pool_scale", "od_norm_g", "lb_param",
                   "ffn_conv_w", "ffn_conv_b", "ln1_g", "ln1_b", "ln2_g", "ln2_b"]
    given = dict(ev_ln_v_g=(ev_ln_v_g, m_ev_ln_v_g, v_ev_ln_v_g), ev_ln_v_b=(ev_ln_v_b, m_ev_ln_v_b, v_ev_ln_v_b),
                 ev_w_s=(ev_w_s, m_ev_w_s, v_ev_w_s), ev_b_s=(ev_b_s, m_ev_b_s, v_ev_b_s),
                 ev_w_pool=(ev_w_pool, m_ev_w_pool, v_ev_w_pool),
                 ev_pool_scale=(ev_pool_scale, m_ev_pool_scale, v_ev_pool_scale),
                 od_norm_g=(od_norm_g, m_od_norm_g, v_od_norm_g), lb_param=(lb_param, m_lb_param, v_lb_param),
                 ffn_conv_w=(ffn_conv_w, m_ffn_conv_w, v_ffn_conv_w), ffn_conv_b=(ffn_conv_b, m_ffn_conv_b, v_ffn_conv_b),
                 ln1_g=(ln1_g, m_ln1_g, v_ln1_g), ln1_b=(ln1_b, m_ln1_b, v_ln1_b), ln2_g=(ln2_g, m_ln2_g, v_ln2_g),
                 ln2_b=(ln2_b, m_ln2_b, v_ln2_b))
    shard_axis = dict(ev_w_pool=2, od_norm_g=1, ffn_conv_w=2)
    rep_names = [n for n in small_names if n not in shard_axis]
    shd_names = [n for n in small_names if n in shard_axis]
    small_packs = [_pack([given[n][j] for n in small_names]) for j in range(3)]

    x2 = x[0]
    xb = _cast(x2, _MM, "cast_x", deps=[pass_on("w_in0", tokens[0])])
    w_in0 = gathered("w_in0", xb)
    h0 = _mm(xb, w_in0, "nn", _F32, "ev_in", out_parts=3)
    tie = pass_on("w_out0", h0)
    yab = _ev_mid_fwd(h0, ev_ln_v_g + tie[0, 0], ev_ln_v_b, ws_tril, bias, wp_b, ev_pool_scale, "ev_mid_fwd")
    w_out0 = gathered("w_out0", yab)
    z1 = _mm(yab, w_out0, "nn", _F32, "ev_out", add=x2, add_scale=ALPHA)
    tie = pass_on("w_up0", (z1, *small_packs))
    x1, x1b = _ln_fwd(z1, ln1_g[0:1] + tie[0, 0], ln1_b[0:1], "ln1_0")
    w_up0 = gathered("w_up0", x1b)
    hf0 = _mm(x1b, w_up0, "nn", _F32, "ffn_up", out_parts=2)
    tie = pass_on("w_dn0", hf0)
    act0, hc0 = _ffn_mid_fwd(hf0, cw_l[0], cb_l[0] + tie[0, 0], "ffn_mid_fwd")
    w_dn0 = gathered("w_dn0", act0)
    z2 = _mm(act0, w_dn0, "nn", _F32, "ffn_down", add=x1, add_scale=ALPHA)
    tie = pass_on("w_in1", z2)
    x2_, x2b = _ln_fwd(z2, ln2_g[0:1] + tie[0, 0], ln2_b[0:1], "ln2_0")
    w_in1 = gathered("w_in1", x2b)
    h1 = _mm(x2b, w_in1, "nn", _F32, "od_in", out_parts=4)
    qd, kd, ke, vb, dec = _hgrn_prep_fwd(h1, lb, "hgrn_prep_fwd")
    tie = pass_on("w_out1", qd)
    o, yo, st = _hgrn_scan_fwd(qd, kd, ke, vb, dec, h1, norm_g + tie[0, 0], "hgrn_scan_fwd")
    w_out1 = gathered("w_out1", yo)
    tie = pass_on("w_up1", yo)
    z3, x3, x3b = _mm(yo, w_out1, "nn", _F32, "od_out", add=x2_, add_scale=ALPHA, ln=(ln1_g[1:2], ln1_b[1:2]), deps=[tie])
    w_up1 = gathered("w_up1", x3b)
    hf1 = _mm(x3b, w_up1, "nn", _F32, "ffn_up", out_parts=2)
    tie = pass_on("w_dn1", hf1)
    act1, hc1 = _ffn_mid_fwd(hf1, cw_l[1], cb_l[1] + tie[0, 0], "ffn_mid_fwd")
    w_dn1 = gathered("w_dn1", act1)
    z4 = _mm(act1, w_dn1, "nn", _F32, "ffn_down", add=x3, add_scale=ALPHA)

    scat = {}

    def scatter(key, dw, axis, n):
        send, recv, dw, land, token = _scatter_start("scatter_start_" + key, dw, axis, n)
        scat[key] = (dw, land, send, recv, axis, n)
        return [token]

    loss11, dz4, dz4b, g_ln2_1, b_ln2_1 = _ln_loss_bwd(z4, ln2_g[1:2], ln2_b[1:2], loss_target[0], "ln_loss_bwd")
    tok = scatter("dn1", _mm(act1, dz4b, "tn", _XCH, "ffn_down_dw"), 0, n_dn)
    dact1 = _mm(dz4b, w_dn1, "nt", _MM, "ffn_down_dx", deps=tok)
    dhf1, dcw1, dcb1 = _ffn_mid_bwd(hf1, hc1, dact1, cw_l[1], "ffn_mid_bwd")
    tok = scatter("up1", _mm(x3b, dhf1, "tn", _XCH, "ffn_up_dw", b_parts=2, deps=tok), 1, n_up)
    dx3 = _mm(dhf1, w_up1, "nt", _F32, "ffn_up_dx", a_parts=2, add=dz4, add_scale=ALPHA, deps=tok)
    dz3, dz3b, g_ln1_1, b_ln1_1 = _ln_bwd(z3, ln1_g[1:2], dx3, "ln_bwd")
    tok = scatter("out1", _mm(yo, dz3b, "tn", _XCH, "od_out_dw", deps=tok), 0, n_out1)
    dyo = _mm(dz3b, w_out1, "nt", _F32, "od_out_dx", deps=tok)
    dqd, dkd, dke, dv, dgate, ddec, dng = _hgrn_scan_bwd(qd, kd, ke, vb, dec, st, o, h1, norm_g, dyo, "hgrn_scan_bwd")
    dh1, dlb = _hgrn_prep_bwd(h1, lb, dqd, dkd, dke, dv, dgate, ddec, "hgrn_prep_bwd")
    tok = scatter("in1", _mm(x2b, dh1, "tn", _XCH, "od_in_dw", b_parts=4, deps=tok), 1, n_in1)
    dx2 = _mm(dh1, w_in1, "nt", _F32, "od_in_dx", a_parts=4, add=dz3, add_scale=ALPHA, deps=tok)
    dz2, dz2b, g_ln2_0, b_ln2_0 = _ln_bwd(z2, ln2_g[0:1], dx2, "ln_bwd")
    tok = scatter("dn0", _mm(act0, dz2b, "tn", _XCH, "ffn_down_dw", deps=tok), 0, n_dn)
    dact0 = _mm(dz2b, w_dn0, "nt", _MM, "ffn_down_dx", deps=tok)
    dhf0, dcw0, dcb0 = _ffn_mid_bwd(hf0, hc0, dact0, cw_l[0], "ffn_mid_bwd")
    tok = scatter("up0", _mm(x1b, dhf0, "tn", _XCH, "ffn_up_dw", b_parts=2, deps=tok), 1, n_up)
    dx1 = _mm(dhf0, w_up0, "nt", _F32, "ffn_up_dx", a_parts=2, add=dz2, add_scale=ALPHA, deps=tok)
    dz1, dz1b, g_ln1_0, b_ln1_0 = _ln_bwd(z1, ln1_g[0:1], dx1, "ln_bwd")
    tok = scatter("out0", _mm(yab, dz1b, "tn", _XCH, "ev_out_dw", deps=tok), 0, n_out0)
    dyab = _mm(dz1b, w_out0, "nt", _F32, "ev_out_dx", deps=tok)
    dh0, dws, dbias, dlng, dlnb, dwp, dsc = _ev_mid_bwd(h0, dyab, ev_ln_v_g, ev_ln_v_b, ws_tril, bias, wp_b,
                                                        ev_pool_scale, "ev_mid_bwd")

    g_b_s = dbias.reshape(A_CHUNK, H, A_HEAD).sum(axis=-1).T[None]
    g_conv_w = jnp.stack([d.transpose(1, 0, 2).reshape(3, F2) for d in (dcw0, dcw1)])
    g_conv_b = jnp.stack([d.reshape(F2) for d in (dcb0, dcb1)])
    small_grads = dict(zip(small_names, [
        dlng, dlnb, dws[None], g_b_s, dwp[None], dsc, dng, lb_vjp(dlb)[0], g_conv_w, g_conv_b,
        jnp.concatenate([g_ln1_0, g_ln1_1]), jnp.concatenate([b_ln1_0, b_ln1_1]),
        jnp.concatenate([g_ln2_0, g_ln2_1]), jnp.concatenate([b_ln2_0, b_ln2_1])]))

    def by_device(g, ax):
        g = g.reshape(g.shape[:ax] + (N_DEV, g.shape[ax] // N_DEV) + g.shape[ax + 1:])
        return jnp.moveaxis(g, ax, 0).reshape(N_DEV, -1)

    shd = jnp.concatenate([by_device(small_grads[n], shard_axis[n]) for n in shd_names], axis=1)
    shd_pack = jnp.pad(shd, ((0, 0), (0, (-shd.shape[1]) % (LANE * LANE)))).reshape(-1, LANE)
    shd_rows = shd_pack.shape[0] // N_DEV
    rep_pack = _pack([small_grads[n] for n in rep_names])
    tok = scatter("in0", _mm(xb, dh0, "tn", _XCH, "ev_in_dw", deps=tok), 1, n_in0)
    tok = scatter("small_rep", rep_pack + tok[0][0, 0], None, None)
    tok = scatter("small_shd", shd_pack + tok[0][0, 0], 0, shd_rows)
    grad_x = _mm(dh0, w_in0, "nt", _F32, "ev_in_dx", add=dz1, add_scale=ALPHA, deps=tok)

    def landed(name, keys, after):
        got = _scatter_wait(name, [scat[k] for k in keys], after)
        return {k: (me1, dw, land) for k, (dw, land) in zip(keys, got)}

    early = landed("scatter_wait_early", ["dn1", "up1", "out1", "in1", "dn0", "up0", "out0"], grad_x)
    big = {}
    r_dn = _adamw_big(*early["dn1"], ffn_w_down, m_ffn_w_down, v_ffn_w_down, 0, n_dn, "adamw_w_dn1", layer=1)
    r_up = _adamw_big(*early["up1"], ffn_w_up, m_ffn_w_up, v_ffn_w_up, 1, n_up, "adamw_w_up1", layer=1)
    big["od_w_out"] = _adamw_big(*early["out1"], od_w_out[0], m_od_w_out[0], v_od_w_out[0], 0, n_out1, "adamw_w_out1")
    big["od_w_in"] = _adamw_big(*early["in1"], od_w_in[0], m_od_w_in[0], v_od_w_in[0], 1, n_in1, "adamw_w_in1")
    big["ffn_w_down"] = _adamw_big(*early["dn0"], ffn_w_down, m_ffn_w_down, v_ffn_w_down, 0, n_dn, "adamw_w_dn0", layer=0, into=r_dn)
    big["ffn_w_up"] = _adamw_big(*early["up0"], ffn_w_up, m_ffn_w_up, v_ffn_w_up, 1, n_up, "adamw_w_up0", layer=0, into=r_up)
    big["ev_w_out"] = _adamw_big(*early["out0"], ev_w_out[0], m_ev_w_out[0], v_ev_w_out[0], 0, n_out0, "adamw_w_out0")
    late = landed("scatter_wait_late", ["in0", "small_rep", "small_shd"],
                  (big["ffn_w_down"][0], big["ffn_w_up"][0], big["od_w_in"][0], big["ev_w_out"][0]))
    big["ev_w_in"] = _adamw_big(*late["in0"], ev_w_in[0], m_ev_w_in[0], v_ev_w_in[0], 1, n_in0, "adamw_w_in0")

    rep_mat = _sum_in_device_order(*late["small_rep"], "sum_small_rep")
    local_g = dict(zip(rep_names, _unpack(rep_mat, [small_grads[n].shape for n in rep_names])))
    _, shd_all, shd_land = late["small_shd"]
    shd_own = lax.dynamic_slice_in_dim(shd_all, me * shd_rows, shd_rows, axis=0)
    shd_mat = _sum_in_device_order(me1, shd_own, shd_land, "sum_small_shd")
    local_g.update(zip(shd_names, _unpack(shd_mat, [given[n][0].shape for n in shd_names])))
    local_shapes = [given[n][0].shape for n in small_names]
    res = _adamw(_pack([local_g[n] for n in small_names])[None], *small_packs, "adamw_small")
    small = {n: [] for n in small_names}
    for r in res:
        for n, a in zip(small_names, _unpack(r, local_shapes)):
            small[n].append(a)

    loss = lax.psum(loss11[0, 0], ("x", "y", "c"))
    order = ["ev_w_in", "ev_ln_v_g", "ev_ln_v_b", "ev_w_s", "ev_b_s", "ev_w_pool", "ev_pool_scale", "ev_w_out", "od_w_in",
             "od_norm_g", "od_w_out", "lb_param", "ffn_w_up", "ffn_conv_w", "ffn_conv_b", "ffn_w_down", "ln1_g", "ln1_b",
             "ln2_g", "ln2_b"]
    shapes = dict(ev_w_in=ev_w_in.shape, ev_w_out=ev_w_out.shape, od_w_in=od_w_in.shape, od_w_out=od_w_out.shape,
                  ffn_w_up=ffn_w_up.shape, ffn_w_down=ffn_w_down.shape)
    outs = [loss, grad_x[None]]
    for kind in range(4):
        for n in order:
            outs.append(big[n][kind].reshape(shapes[n]) if n in big else small[n][kind])
    return tuple(outs)
```
